```python
import math
import jax, jax.numpy as jnp
from jax import lax
import numpy as np

D_MODEL = 2048
BATCH = 4
SEQ = 2048
DEPTH = 1
DEC_BATCH = 128
DEC_SEQ = 8
PAST_LEN = 16384
PAGE_SIZE = 128

N_META = 16
MIX = D_MODEL
W_A = MIX // 2
HEAD_A = 128
H_A = W_A // HEAD_A
W_B = MIX - W_A
HEAD_B = 64
H_B = W_B // HEAD_B
CONV_A = 4
CHUNK_A = 64
W_LORA = max(32, int(round(1.8 * W_B ** 0.5 / 32)) * 32)
A_LORA = max(32, int(round(1.8 * W_B ** 0.5 / 32)) * 32)
G_LORA = max(32, int(round(0.6 * W_B ** 0.8 / 32)) * 32)
D_FF = int(round(8 * D_MODEL / 3 / 256)) * 256
FFN_CONV = 3
RMS_EPS = 1e-6
GN_EPS = 64e-5
A_PROJ = 4 * W_A + 2 * H_A
B_PROJ = 3 * W_B + W_LORA + A_LORA + G_LORA
P_IN = A_PROJ + B_PROJ

kernel_name = 'hymba_gdn_rwkv7_convffn_step'

F32 = jnp.float32


def rmsnorm(x, g):
    xf = x.astype(F32)
    y = xf * lax.rsqrt(jnp.mean(xf * xf, axis=-1, keepdims=True) + RMS_EPS)
    return (y * g.astype(F32)).astype(x.dtype)


def l2norm(x):
    return x * lax.rsqrt(jnp.sum(x * x, axis=-1, keepdims=True) + 1e-6)


def causal_dwconv(buf, x, w):
    K = w.shape[0]
    L = x.shape[1]
    xc = jnp.concatenate([buf.astype(x.dtype), x], axis=1)
    xf = xc.astype(F32)
    wf = w.astype(F32)
    y = xf[:, 0:L] * wf[0]
    for i in range(1, K):
        y = y + xf[:, i:i + L] * wf[i]
    return y, xc[:, L:]


def gated_delta_chunked(q, k, v, beta, g, S0, C):
    Bn, L, H, dk = q.shape
    dv = v.shape[-1]
    N = L // C

    def chunks(t):
        t = t.reshape((Bn, N, C) + t.shape[2:])
        t = jnp.moveaxis(t, 1, 0)
        return jnp.moveaxis(t, 3, 2)

    qc, kc, vc, bc, gch = chunks(q), chunks(k), chunks(v), chunks(beta), chunks(g)
    gc = jnp.cumsum(gch, axis=-1)
    idx = jnp.arange(C)
    incl = idx[:, None] >= idx[None, :]
    strict = idx[:, None] > idx[None, :]
    gamma = jnp.exp(jnp.where(incl, gc[..., :, None] - gc[..., None, :], -jnp.inf))
    kb = kc * bc[..., None]
    lmat = jnp.where(strict, jnp.einsum('nbhik,nbhjk->nbhij', kb, kc) * gamma, 0.0)
    amat = lmat + jnp.eye(C, dtype=F32)
    rhs = jnp.concatenate([vc * bc[..., None], kb * jnp.exp(gc)[..., None]], axis=-1)
    sol = lax.linalg.triangular_solve(amat, rhs, left_side=True, lower=True, unit_diagonal=True)
    u = sol[..., :dv]
    w = sol[..., dv:]
    attn = jnp.einsum('nbhik,nbhjk->nbhij', qc, kc) * gamma
    qd = qc * jnp.exp(gc)[..., None]
    kd = kc * jnp.exp(gc[..., -1:] - gc)[..., None]
    glast = jnp.exp(gc[..., -1])[..., None, None]

    def step(S, inp):
        u_c, w_c, a_c, qd_c, kd_c, gl_c = inp
        v_new = u_c - jnp.einsum('bhck,bhkv->bhcv', w_c, S)
        o = jnp.einsum('bhck,bhkv->bhcv', qd_c, S) + jnp.einsum('bhij,bhjv->bhiv', a_c, v_new)
        S = S * gl_c + jnp.einsum('bhck,bhcv->bhkv', kd_c, v_new)
        return S, o

    S, o = lax.scan(step, S0, (u, w, attn, qd, kd, glast))
    o = jnp.moveaxis(jnp.moveaxis(o, 0, 1), 2, 3).reshape(Bn, L, H, dv)
    return o, S


def gated_delta_group(pa, conv_buf, S0, conv_w, a_log, dt_bias, onorm, n_lead, chunk):
    Bn, L, _ = pa.shape
    qkv, new_buf = causal_dwconv(conv_buf, pa[..., :3 * W_A], conv_w)
    qkv = jax.nn.silu(qkv)
    q = l2norm(qkv[..., :W_A].reshape(Bn, L, H_A, HEAD_A)) * (HEAD_A ** -0.5)
    k = l2norm(qkv[..., W_A:2 * W_A].reshape(Bn, L, H_A, HEAD_A))
    v = qkv[..., 2 * W_A:].reshape(Bn, L, H_A, HEAD_A)
    z = pa[..., 3 * W_A:4 * W_A].astype(F32).reshape(Bn, L, H_A, HEAD_A)
    beta = jax.nn.sigmoid(pa[..., 4 * W_A:4 * W_A + H_A].astype(F32))
    g = -jnp.exp(a_log.astype(F32)) * jax.nn.softplus(pa[..., 4 * W_A + H_A:].astype(F32) + dt_bias.astype(F32))
    S = S0.astype(F32)
    segments = ([(0, n_lead, n_lead)] if n_lead > 0 else []) + [(n_lead, L, chunk)]
    outs = []
    for (s0, s1, c) in segments:
        o_seg, S = gated_delta_chunked(q[:, s0:s1], k[:, s0:s1], v[:, s0:s1], beta[:, s0:s1], g[:, s0:s1], S, c)
        outs.append(o_seg)
    o = jnp.concatenate(outs, axis=1) if len(outs) > 1 else outs[0]
    o = o * lax.rsqrt(jnp.mean(o * o, axis=-1, keepdims=True) + RMS_EPS) * onorm.astype(F32) * jax.nn.silu(z)
    return o.reshape(Bn, L, W_A), new_buf, S


def rwkv7_scan(r, w, k, v, kk, a, S0):
    def step(S, inp):
        r_t, w_t, k_t, v_t, kk_t, a_t = inp
        sa = jnp.einsum('bhij,bhj->bhi', S, -kk_t)
        S = S * w_t[:, :, None, :] + sa[..., :, None] * (kk_t * a_t)[..., None, :] + v_t[..., :, None] * k_t[..., None, :]
        o = jnp.einsum('bhij,bhj->bhi', S, r_t)
        return S, o

    xs = tuple(jnp.moveaxis(t, 1, 0) for t in (r, w, k, v, kk, a))
    S, o = lax.scan(step, S0, xs)
    return jnp.moveaxis(o, 0, 1), S


def rwkv7_group(pb, shift_buf, S0, mu, w0, w2, a0, a2, g2, k_k, k_a, r_k, lnw, lnb):
    Bn, L, _ = pb.shape
    pbf = pb.astype(F32)
    prev = jnp.concatenate([shift_buf.astype(F32), pbf[:, :-1]], axis=1)
    xb = pbf + (prev - pbf) * mu.astype(F32)
    r = xb[..., :W_B]
    k = xb[..., W_B:2 * W_B]
    v = xb[..., 2 * W_B:3 * W_B]
    o1 = 3 * W_B
    wd = xb[..., o1:o1 + W_LORA]
    ad = xb[..., o1 + W_LORA:o1 + W_LORA + A_LORA]
    gd = xb[..., o1 + W_LORA + A_LORA:]
    wlog = -jax.nn.softplus(-(w0.astype(F32) + jnp.tanh(wd) @ w2.astype(F32))) - 0.5
    decay = jnp.exp(-jnp.exp(wlog))
    aa = jax.nn.sigmoid(a0.astype(F32) + ad @ a2.astype(F32))
    gate = jax.nn.sigmoid(gd) @ g2.astype(F32)

    def heads(t):
        return t.reshape(Bn, L, H_B, HEAD_B)

    kk = l2norm(heads(k * k_k.astype(F32)))
    k = k * (1.0 + (aa - 1.0) * k_a.astype(F32))
    rh, kh, vh = heads(r), heads(k), heads(v)
    o, S = rwkv7_scan(rh, heads(decay), kh, vh, kk, heads(aa), S0.astype(F32))
    mean = jnp.mean(o, axis=-1, keepdims=True)
    var = jnp.mean(jnp.square(o - mean), axis=-1, keepdims=True)
    o = ((o - mean) * lax.rsqrt(var + GN_EPS)).reshape(Bn, L, W_B) * lnw.astype(F32) + lnb.astype(F32)
    bonus = jnp.sum(rh * kh * r_k.astype(F32), axis=-1, keepdims=True) * vh
    o = (o + bonus.reshape(Bn, L, W_B)) * gate
    return o, pb[:, -1:], S


def layer(x, st_delta, st_conv, st_wkv, st_shift, st_ffn, lp, n_lead, chunk):
    (n1, wi, cw_a, alog, dtb, on_a, mu, w0_, w2_, a0_, a2_, g2_, kk_, ka_, rk_, lw, lb,
     wo, n2, wfi, cw_f, wfo) = lp
    h = rmsnorm(x, n1)
    proj = h @ wi
    o_a, conv_new, S_a = gated_delta_group(proj[..., :A_PROJ], st_conv, st_delta, cw_a, alog, dtb, on_a, n_lead, chunk)
    o_b, shift_new, S_b = rwkv7_group(proj[..., A_PROJ:], st_shift, st_wkv, mu, w0_, w2_, a0_, a2_, g2_, kk_, ka_, rk_, lw, lb)
    x = x + jnp.concatenate([o_a, o_b], axis=-1).astype(x.dtype) @ wo
    h2 = rmsnorm(x, n2)
    gu = h2 @ wfi
    gconv, ffn_new = causal_dwconv(st_ffn, gu[..., :D_FF], cw_f)
    x = x + (jax.nn.silu(gconv) * gu[..., D_FF:].astype(F32)).astype(x.dtype) @ wfo
    new_state = (S_a.astype(st_delta.dtype), conv_new.astype(st_conv.dtype), S_b.astype(st_wkv.dtype),
                 shift_new.astype(st_shift.dtype), ffn_new.astype(st_ffn.dtype))
    return x, new_state


def setup_inputs(seed: int = 0) -> dict:
    key = jax.random.key(seed)
    ks = iter(jax.random.split(key, 40))

    def nrm(shape, scale):
        return scale * jax.random.normal(next(ks), shape, F32)

    def uni(shape, lo, hi):
        return jax.random.uniform(next(ks), shape, F32, lo, hi)

    n = jnp.arange(W_B, dtype=F32) / (W_B - 1)
    dt = jnp.exp(uni((DEPTH, H_A), math.log(1e-3), math.log(1e-1)))
    return {
        'x_prompt': nrm((BATCH, SEQ, D_MODEL), 1.0),
        'x_sample': nrm((DEC_BATCH, DEC_SEQ, D_MODEL), 1.0),
        'state_delta': nrm((DEPTH, DEC_BATCH, H_A, HEAD_A, HEAD_A), 0.1),
        'state_conv_qkv': nrm((DEPTH, DEC_BATCH, CONV_A - 1, 3 * W_A), 1.0),
        'state_wkv': nrm((DEPTH, DEC_BATCH, H_B, HEAD_B, HEAD_B), 0.1),
        'state_shift': nrm((DEPTH, DEC_BATCH, 1, B_PROJ), 1.0),
        'state_ffn_conv': nrm((DEPTH, DEC_BATCH, FFN_CONV - 1, D_FF), 1.0),
        'meta': nrm((N_META, D_MODEL), 1.0),
        'norm1': 1.0 + nrm((DEPTH, D_MODEL), 0.02),
        'w_in': nrm((DEPTH, D_MODEL, P_IN), D_MODEL ** -0.5),
        'conv_a': nrm((DEPTH, CONV_A, 3 * W_A), CONV_A ** -0.5),
        'a_log': jnp.log(uni((DEPTH, H_A), 1.0, 16.0)),
        'dt_bias': dt + jnp.log(-jnp.expm1(-dt)),
        'onorm_a': 1.0 + nrm((DEPTH, HEAD_A), 0.02),
        'mu_b': uni((DEPTH, B_PROJ), 0.0, 1.0),
        'w0': -6.5 + 5.0 * n ** 0.85 + nrm((DEPTH, W_B), 0.05),
        'w2': nrm((DEPTH, W_LORA, W_B), 0.5 * W_LORA ** -0.5),
        'a0': nrm((DEPTH, W_B), 0.01),
        'a2': nrm((DEPTH, A_LORA, W_B), 0.5 * A_LORA ** -0.5),
        'g2': nrm((DEPTH, G_LORA, W_B), G_LORA ** -0.5),
        'k_k': 0.85 + nrm((DEPTH, W_B), 0.02),
        'k_a': 1.0 + nrm((DEPTH, W_B), 0.02),
        'r_k': -0.04 + nrm((DEPTH, H_B, HEAD_B), 0.02),
        'lnx_w': 1.0 + nrm((DEPTH, W_B), 0.02),
        'lnx_b': nrm((DEPTH, W_B), 0.01),
        'w_o': nrm((DEPTH, MIX, D_MODEL), MIX ** -0.5),
        'norm2': 1.0 + nrm((DEPTH, D_MODEL), 0.02),
        'w_ffn_in': nrm((DEPTH, D_MODEL, 2 * D_FF), D_MODEL ** -0.5),
        'conv_f': nrm((DEPTH, FFN_CONV, D_FF), FFN_CONV ** -0.5),
        'w_ffn_out': nrm((DEPTH, D_FF, D_MODEL), D_FF ** -0.5),
        'norm_f': 1.0 + nrm((D_MODEL,), 0.02),
    }


def reference(x_prompt, x_sample, state_delta, state_conv_qkv, state_wkv, state_shift, state_ffn_conv,
              meta, norm1, w_in, conv_a, a_log, dt_bias, onorm_a, mu_b, w0, w2, a0, a2, g2, k_k, k_a, r_k,
              lnx_w, lnx_b, w_o, norm2, w_ffn_in, conv_f, w_ffn_out, norm_f):
    Bp = x_prompt.shape[0]
    xp = jnp.concatenate([jnp.broadcast_to(meta.astype(x_prompt.dtype)[None], (Bp, N_META, D_MODEL)), x_prompt], axis=1)
    xs = x_sample
    sample_chunk = math.gcd(x_sample.shape[1], CHUNK_A)
    p_lists = ([], [], [], [], [])
    s_lists = ([], [], [], [], [])
    for l in range(DEPTH):
        lp = (norm1[l], w_in[l], conv_a[l], a_log[l], dt_bias[l], onorm_a[l], mu_b[l], w0[l], w2[l], a0[l],
              a2[l], g2[l], k_k[l], k_a[l], r_k[l], lnx_w[l], lnx_b[l], w_o[l], norm2[l], w_ffn_in[l],
              conv_f[l], w_ffn_out[l])
        zeros = tuple(jnp.zeros((Bp,) + s.shape[2:], s.dtype)
                      for s in (state_delta, state_conv_qkv, state_wkv, state_shift, state_ffn_conv))
        xp, p_new = layer(xp, zeros[0], zeros[1], zeros[2], zeros[3], zeros[4], lp, N_META, CHUNK_A)
        xs, s_new = layer(xs, state_delta[l], state_conv_qkv[l], state_wkv[l], state_shift[l], state_ffn_conv[l],
                          lp, 0, sample_chunk)
        for i in range(5):
            p_lists[i].append(p_new[i])
            s_lists[i].append(s_new[i])
    y_prompt = rmsnorm(xp[:, N_META:], norm_f)
    y_sample = rmsnorm(xs, norm_f)
    p_delta, p_conv_qkv, p_wkv, p_shift, p_ffn_conv = [jnp.stack(t, axis=0) for t in p_lists]
    s_delta, s_conv_qkv, s_wkv, s_shift, s_ffn_conv = [jnp.stack(t, axis=0) for t in s_lists]
    return (y_prompt, y_sample, p_delta, p_conv_qkv, p_wkv, p_shift, p_ffn_conv,
            s_delta, s_conv_qkv, s_wkv, s_shift, s_ffn_conv)
```

```python
import functools

import jax
import jax.numpy as jnp
from jax import lax
from jax.experimental import pallas as pl
from jax.experimental.pallas import tpu as pltpu

F32 = jnp.float32
BF16 = jnp.bfloat16

D_MODEL = 2048
N_META = 16
W_A = 1024
HEAD_A = 128
H_A = 8
W_B = 1024
HEAD_B = 64
H_B = 16
N_PAIR = H_B // 2
W_LORA = 64
A_LORA = 64
G_LORA = 160
D_FF = 5376
RMS_EPS = 1e-6
GN_EPS = 64e-5
A_PROJ = 4 * W_A + 2 * H_A
B_PROJ = 3 * W_B + W_LORA + A_LORA + G_LORA

QKVZ0 = 0
RKV0 = 4 * W_A
SM0 = RKV0 + 3 * W_B
SM_W = 512
SM_BA = 0
SM_WA = 128
SM_G = 256
P_CAT = SM0 + SM_W

LANE = 128
SUB = 8
VMEM_LIMIT = 48 * 1024 * 1024
NEG_BIG = -1e30

HI = lax.Precision.HIGHEST


def _bdot(a, b):
    return jnp.dot(a.astype(BF16), b.astype(BF16), preferred_element_type=F32)


def _bdot_nt(a, b):
    return lax.dot_general(a.astype(BF16), b.astype(BF16), (((1,), (1,)), ((), ())),
                           preferred_element_type=F32)


def _bdot_tn(a, b):
    return lax.dot_general(a.astype(BF16), b.astype(BF16), (((0,), (0,)), ((), ())),
                           preferred_element_type=F32)


def _hdot(a, b):
    return jnp.dot(a, b, precision=HI, preferred_element_type=F32)


def _hdot_nt(a, b):
    return lax.dot_general(a, b, (((1,), (1,)), ((), ())), precision=HI, preferred_element_type=F32)


def _sigmoid(x):
    return 1.0 / (1.0 + jnp.exp(-x))


def _silu(x):
    return x * _sigmoid(x)


def _softplus(x):
    return jnp.maximum(x, 0.0) + jnp.log(1.0 + jnp.exp(-jnp.abs(x)))


def _iota(shape, dim):
    return lax.broadcasted_iota(jnp.int32, shape, dim)


def _shifted(x, prev8, k):
    n = x.shape[0]
    xr = pltpu.roll(x, k, 0)
    pr = pltpu.roll(prev8, k, 0)
    first = jnp.where(_iota((SUB, x.shape[1]), 0) < k, pr, xr[0:SUB])
    if n == SUB:
        return first
    return jnp.concatenate([first, xr[SUB:]], axis=0)


def _tri_inverse(x):
    c = x.shape[0]
    eye = (_iota((c, c), 0) == _iota((c, c), 1)).astype(F32)
    t = eye + x
    p = x
    n = 1
    while 2 * n < c:
        p = _hdot(p, p)
        t = t + _hdot(p, t)
        n *= 2
    return t


def _norm_matmul_kernel(x_ref, g_ref, w_ref, o_ref, h_ref):
    @pl.when(pl.program_id(1) == 0)
    def _():
        x = x_ref[...]
        ms = jnp.mean(x * x, axis=-1, keepdims=True)
        h_ref[...] = (x * lax.rsqrt(ms + RMS_EPS) * g_ref[...]).astype(BF16)

    o_ref[...] = jnp.dot(h_ref[...], w_ref[...], preferred_element_type=F32)


def _norm_matmul(x, g, w, tm, tn):
    m, k = x.shape
    n = w.shape[1]
    return pl.pallas_call(
        _norm_matmul_kernel,
        out_shape=jax.ShapeDtypeStruct((m, n), F32),
        grid=(m // tm, n // tn),
        in_specs=[
            pl.BlockSpec((tm, k), lambda i, j: (i, 0)),
            pl.BlockSpec((1, k), lambda i, j: (0, 0)),
            pl.BlockSpec((k, tn), lambda i, j: (0, j)),
        ],
        out_specs=pl.BlockSpec((tm, tn), lambda i, j: (i, j)),
        scratch_shapes=[pltpu.VMEM((tm, k), BF16)],
        compiler_params=pltpu.CompilerParams(
            dimension_semantics=("parallel", "arbitrary"), vmem_limit_bytes=VMEM_LIMIT),
        name="norm_matmul",
    )(x, g, w)


def _out_proj_kernel(x_ref, oa_ref, ob_ref, wt_ref, wb_ref, o_ref):
    acc = jnp.dot(oa_ref[...], wt_ref[...], preferred_element_type=F32)
    acc = acc + jnp.dot(ob_ref[...], wb_ref[...], preferred_element_type=F32)
    o_ref[...] = x_ref[...] + acc


def _out_proj(x, oa, ob, wo, tm, tn):
    m, d = x.shape
    return pl.pallas_call(
        _out_proj_kernel,
        out_shape=jax.ShapeDtypeStruct((m, d), F32),
        grid=(m // tm, d // tn),
        in_specs=[
            pl.BlockSpec((tm, tn), lambda i, j: (i, j)),
            pl.BlockSpec((tm, W_A), lambda i, j: (i, 0)),
            pl.BlockSpec((tm, W_B), lambda i, j: (i, 0)),
            pl.BlockSpec((W_A, tn), lambda i, j: (0, j)),
            pl.BlockSpec((W_B, tn), lambda i, j: (W_A // W_B, j)),
        ],
        out_specs=pl.BlockSpec((tm, tn), lambda i, j: (i, j)),
        compiler_params=pltpu.CompilerParams(
            dimension_semantics=("parallel", "arbitrary"), vmem_limit_bytes=VMEM_LIMIT),
        name="out_proj",
    )(x, oa, ob, wo, wo)


def _ffn_act_kernel(g_ref, u_ref, prev_ref, cw_ref, o_ref, carry_ref):
    @pl.when(pl.program_id(2) == 0)
    def _():
        carry_ref[...] = prev_ref[...]

    x = g_ref[...]
    n = x.shape[0]
    prev8 = carry_ref[...]
    y = x * cw_ref[2:3, :]
    y = y + _shifted(x, prev8, 1) * cw_ref[1:2, :]
    y = y + _shifted(x, prev8, 2) * cw_ref[0:1, :]
    carry_ref[...] = x[n - SUB:n]
    o_ref[...] = (_silu(y) * u_ref[...]).astype(BF16)


def _ffn_act(gu, prev, cw, nb, seq, tr, tc):
    ncol = D_FF // tc
    nrow = seq // tr
    bcast = prev.shape[0] == 1
    return pl.pallas_call(
        _ffn_act_kernel,
        out_shape=jax.ShapeDtypeStruct((nb * seq, D_FF), BF16),
        grid=(nb, ncol, nrow),
        in_specs=[
            pl.BlockSpec((tr, tc), lambda b, j, r: (b * nrow + r, j)),
            pl.BlockSpec((tr, tc), lambda b, j, r: (b * nrow + r, ncol + j)),
            pl.BlockSpec((None, SUB, tc), (lambda b, j, r: (0, 0, j)) if bcast else (lambda b, j, r: (b, 0, j))),
            pl.BlockSpec((3, tc), lambda b, j, r: (0, j)),
        ],
        out_specs=pl.BlockSpec((tr, tc), lambda b, j, r: (b * nrow + r, j)),
        scratch_shapes=[pltpu.VMEM((SUB, tc), F32)],
        compiler_params=pltpu.CompilerParams(
            dimension_semantics=("parallel", "parallel", "arbitrary"), vmem_limit_bytes=VMEM_LIMIT),
        name="ffn_act",
    )(gu, gu, prev, cw)


def _ffn_out_kernel(a_ref, w_ref, x_ref, g_ref, o_ref, acc_ref):
    kk = pl.program_id(1)

    @pl.when(kk == 0)
    def _():
        acc_ref[...] = x_ref[...]

    acc_ref[...] += jnp.dot(a_ref[...], w_ref[...], preferred_element_type=F32)

    @pl.when(kk == pl.num_programs(1) - 1)
    def _():
        x = acc_ref[...]
        ms = jnp.mean(x * x, axis=-1, keepdims=True)
        o_ref[...] = x * lax.rsqrt(ms + RMS_EPS) * g_ref[...]


def _ffn_out(act, w, x1, g, m, tm, tk):
    kdim, d = w.shape
    return pl.pallas_call(
        _ffn_out_kernel,
        out_shape=jax.ShapeDtypeStruct((m, d), F32),
        grid=(m // tm, kdim // tk),
        in_specs=[
            pl.BlockSpec((tm, tk), lambda i, k: (i, k)),
            pl.BlockSpec((tk, d), lambda i, k: (k, 0)),
            pl.BlockSpec((tm, d), lambda i, k: (i, 0)),
            pl.BlockSpec((1, d), lambda i, k: (0, 0)),
        ],
        out_specs=pl.BlockSpec((tm, d), lambda i, k: (i, 0)),
        scratch_shapes=[pltpu.VMEM((tm, d), F32)],
        compiler_params=pltpu.CompilerParams(
            dimension_semantics=("parallel", "arbitrary"), vmem_limit_bytes=VMEM_LIMIT),
        name="ffn_out",
    )(act, w, x1, g)


def _delta_kernel(qkvz_ref, ba_ref, prev_ref, s0_ref, cw_ref, alog_ref, dtb_ref, on_ref,
                  o_ref, s_ref, carry_ref, *, C):
    ci = pl.program_id(1)

    @pl.when(ci == 0)
    def _():
        carry_ref[...] = prev_ref[...]
        s_ref[...] = s0_ref[...]

    row = _iota((C, C), 0)
    col = _iota((C, C), 1)
    incl = row >= col
    strict = row > col
    tri = incl.astype(F32)

    ba = ba_ref[...]
    beta_full = _sigmoid(ba)
    g_full = -jnp.exp(alog_ref[...]) * _softplus(ba + dtb_ref[...])
    gc_full = _hdot(tri, g_full)
    lane = _iota((C, LANE), 1)

    def conv_silu(c0):
        x = qkvz_ref[:, c0:c0 + LANE]
        prev8 = carry_ref[:, c0:c0 + LANE]
        y = x * cw_ref[3:4, c0:c0 + LANE]
        for k in (1, 2, 3):
            y = y + _shifted(x, prev8, k) * cw_ref[3 - k:4 - k, c0:c0 + LANE]
        return _silu(y)

    def l2n(x):
        return x * lax.rsqrt(jnp.sum(x * x, axis=-1, keepdims=True) + 1e-6)

    for h in range(H_A):
        q = l2n(conv_silu(h * HEAD_A)) * (HEAD_A ** -0.5)
        k = l2n(conv_silu(W_A + h * HEAD_A))
        v = conv_silu(2 * W_A + h * HEAD_A)
        z = qkvz_ref[:, 3 * W_A + h * HEAD_A:3 * W_A + (h + 1) * HEAD_A]
        bcol = beta_full[:, h:h + 1]
        gcol = gc_full[:, H_A + h:H_A + h + 1]
        onehot = (lane == H_A + h).astype(F32)
        grow = _hdot_nt(onehot, gc_full)
        gamma = jnp.exp(jnp.where(incl, gcol - grow, NEG_BIG))
        glast = gc_full[C - 1:C, H_A + h:H_A + h + 1]
        eg = jnp.exp(gcol)
        kb = k * bcol
        lmat = jnp.where(strict, _bdot_nt(kb, k) * gamma, 0.0)
        tinv = _tri_inverse(-lmat)
        u = _hdot(tinv, v * bcol)
        w = _hdot(tinv, kb * eg)
        attn = _bdot_nt(q, k) * gamma
        qd = q * eg
        kd = k * jnp.exp(glast - gcol)
        s = s_ref[h]
        ws = _bdot(w, s)
        qs = _bdot(qd, s)
        v_new = u - ws
        o = qs + _bdot(attn, v_new)
        s_ref[h] = s * jnp.exp(glast) + _bdot_tn(kd, v_new)
        o = o * lax.rsqrt(jnp.mean(o * o, axis=-1, keepdims=True) + RMS_EPS)
        o = o * on_ref[...] * _silu(z)
        o_ref[:, h * HEAD_A:(h + 1) * HEAD_A] = o.astype(BF16)

    carry_ref[...] = qkvz_ref[C - SUB:C, 0:3 * W_A]


def _delta_mixer(proj, row0, nb, seq, C, prev, s0, cw, alog_row, dtb_row, onorm):
    nch = seq // C
    blk0 = row0 // C
    bcast = s0.shape[0] == 1
    bsel = (lambda b: 0) if bcast else (lambda b: b)
    return pl.pallas_call(
        functools.partial(_delta_kernel, C=C),
        out_shape=(jax.ShapeDtypeStruct((nb * seq, W_A), BF16),
                   jax.ShapeDtypeStruct((nb, H_A, HEAD_A, HEAD_A), F32)),
        grid=(nb, nch),
        in_specs=[
            pl.BlockSpec((C, 4 * W_A), lambda b, c: (blk0 + b * nch + c, 0)),
            pl.BlockSpec((C, LANE), lambda b, c: (blk0 + b * nch + c, SM0 // LANE)),
            pl.BlockSpec((None, SUB, 3 * W_A), lambda b, c: (bsel(b), 0, 0)),
            pl.BlockSpec((None, H_A, HEAD_A, HEAD_A), lambda b, c: (bsel(b), 0, 0, 0)),
            pl.BlockSpec((4, 3 * W_A), lambda b, c: (0, 0)),
            pl.BlockSpec((1, LANE), lambda b, c: (0, 0)),
            pl.BlockSpec((1, LANE), lambda b, c: (0, 0)),
            pl.BlockSpec((1, HEAD_A), lambda b, c: (0, 0)),
        ],
        out_specs=(pl.BlockSpec((C, W_A), lambda b, c: (b * nch + c, 0)),
                   pl.BlockSpec((None, H_A, HEAD_A, HEAD_A), lambda b, c: (b, 0, 0, 0))),
        scratch_shapes=[pltpu.VMEM((SUB, 3 * W_A), F32)],
        compiler_params=pltpu.CompilerParams(
            dimension_semantics=("parallel", "arbitrary"), vmem_limit_bytes=VMEM_LIMIT),
        name="delta_mixer",
    )(proj, proj, prev, s0, cw, alog_row, dtb_row, onorm)


def _rwkv_kernel(r_ref, k_ref, v_ref, sm_ref, pr_ref, pk_ref, pv_ref, psm_ref, s0_ref,
                 mur_ref, muk_ref, muv_ref, musm_ref, w2_ref, a2_ref, g2_ref,
                 w0_ref, a0_ref, kk_ref, ka_ref, rk_ref, lnw_ref, lnb_ref,
                 o_ref, s_ref, cr_ref, ck_ref, cv_ref, csm_ref, st_ref, *, C, G):
    ci = pl.program_id(2)
    nch = pl.num_programs(2)

    r2 = _iota((LANE, LANE), 0)
    c2 = _iota((LANE, LANE), 1)
    same_head = (r2 < HEAD_B) == (c2 < HEAD_B)
    ones_bd = same_head.astype(F32)
    spread = (_iota((HEAD_B, LANE), 0) == (_iota((HEAD_B, LANE), 1) & (HEAD_B - 1))).astype(F32)
    gather = ((_iota((LANE, HEAD_B), 0) & (HEAD_B - 1)) == _iota((LANE, HEAD_B), 1)).astype(F32)

    @pl.when(ci == 0)
    def _():
        cr_ref[...] = pr_ref[...]
        ck_ref[...] = pk_ref[...]
        cv_ref[...] = pv_ref[...]
        csm_ref[...] = psm_ref[...]
        for p in range(G):
            st_ref[p] = jnp.where(same_head, _hdot(s0_ref[p], spread), 0.0)

    row = _iota((C, C), 0)
    col = _iota((C, C), 1)
    incl = row >= col
    strict = row > col
    tri = incl.astype(F32)
    head0 = _iota((C, LANE), 1) < HEAD_B
    head0_2 = _iota((2 * C, LANE), 1) < HEAD_B

    def lerp(x, prev8, mu):
        return x + (_shifted(x, prev8, 1) - x) * mu

    sm = sm_ref[...]
    xs = lerp(sm, csm_ref[...], musm_ref[...])
    slab_wa = xs[:, SM_WA:SM_WA + LANE]
    slab_g = xs[:, SM_G:SM_G + 2 * LANE]
    tanh_wa = jnp.tanh(slab_wa)
    sig_g = _sigmoid(slab_g)

    for p in range(G):
        cs = slice(p * LANE, (p + 1) * LANE)
        xr = lerp(r_ref[:, cs], cr_ref[:, cs], mur_ref[:, cs])
        xk = lerp(k_ref[:, cs], ck_ref[:, cs], muk_ref[:, cs])
        xv = lerp(v_ref[:, cs], cv_ref[:, cs], muv_ref[:, cs])
        wl = _bdot(tanh_wa, w2_ref[:, cs])
        wlog = -_softplus(-(w0_ref[:, cs] + wl)) - 0.5
        ld = -jnp.exp(wlog)
        aa = _sigmoid(a0_ref[:, cs] + _bdot(slab_wa, a2_ref[:, cs]))
        gate = _bdot(sig_g, g2_ref[:, cs])
        kkr = xk * kk_ref[:, cs]
        kkn = kkr * lax.rsqrt(_hdot(kkr * kkr, ones_bd) + 1e-6)
        k2 = xk * (1.0 + (aa - 1.0) * ka_ref[:, cs])

        lp = _hdot(tri, ld)
        e_pos = jnp.exp(lp)
        e_neg = jnp.exp(-lp)
        lp_last = lp[C - 1:C, :]
        e_rem = jnp.exp(lp_last - lp)
        at = -kkn * jnp.exp(lp - ld)
        rt = xr * e_pos
        bt = kkn * aa * e_neg
        kt = k2 * e_neg
        bhat = kkn * aa * e_rem
        khat = k2 * e_rem

        s = st_ref[p]
        ar = jnp.concatenate([at, rt], axis=0)
        ars = _bdot_nt(ar, s)
        x_state = ars[0:C]
        o_state = ars[C:2 * C]

        us, arbs, arks = [], [], []
        for hh in range(2):
            arh = jnp.where(head0_2 if hh == 0 else jnp.logical_not(head0_2), ar, 0.0)
            ab = _bdot_nt(arh, bt)
            ak = _bdot_nt(arh, kt)
            lab = jnp.where(strict, ab[0:C], 0.0)
            lak = jnp.where(strict, ak[0:C], 0.0)
            arbs.append(jnp.where(incl, ab[C:2 * C], 0.0))
            arks.append(jnp.where(incl, ak[C:2 * C], 0.0))
            us.append(_hdot(_tri_inverse(lab), x_state + _bdot(lak, xv)))
        u_pair = jnp.where(head0, us[0], us[1])
        o_pair = jnp.where(head0, _bdot(arbs[0], u_pair) + _bdot(arks[0], xv),
                           _bdot(arbs[1], u_pair) + _bdot(arks[1], xv))
        uv = jnp.concatenate([u_pair, xv], axis=0)
        o = o_state + o_pair

        bkhat = jnp.concatenate([bhat, khat], axis=0)
        s_new = s * jnp.exp(lp_last) + _bdot_tn(uv, bkhat)
        st_ref[p] = jnp.where(same_head, s_new, 0.0)

        mean = _hdot(o, ones_bd) * (1.0 / HEAD_B)
        d = o - mean
        var = _hdot(d * d, ones_bd) * (1.0 / HEAD_B)
        on = d * lax.rsqrt(var + GN_EPS) * lnw_ref[:, cs] + lnb_ref[:, cs]
        bonus = _hdot(xr * k2 * rk_ref[:, cs], ones_bd) * xv
        o_ref[:, cs] = ((on + bonus) * gate).astype(BF16)

    cr_ref[...] = r_ref[C - SUB:C, :]
    ck_ref[...] = k_ref[C - SUB:C, :]
    cv_ref[...] = v_ref[C - SUB:C, :]
    csm_ref[...] = sm_ref[C - SUB:C, :]

    @pl.when(ci == nch - 1)
    def _():
        for p in range(G):
            s_ref[p] = _hdot(st_ref[p], gather)


def _rwkv_mixer(proj, row0, nb, seq, C, G, prev_rkv, prev_sm, s0, mu_rkv, mu_sm, w2p, a2p, g2p,
                w0, a0, k_k, k_a, r_k, lnw, lnb):
    nch = seq // C
    ng = N_PAIR // G
    gw = G * LANE
    blk0 = row0 // C
    bcast = s0.shape[0] == 1
    bsel = (lambda b: 0) if bcast else (lambda b: b)

    def proj_spec(col0):
        return pl.BlockSpec((C, gw), lambda b, g, c: (blk0 + b * nch + c, col0 // gw + g))

    def prev_spec(part):
        return pl.BlockSpec((None, SUB, gw), lambda b, g, c: (bsel(b), 0, part * (W_B // gw) + g))

    def vec_spec(part=0):
        return pl.BlockSpec((1, gw), lambda b, g, c: (0, part * (W_B // gw) + g))

    in_specs = [
        proj_spec(RKV0), proj_spec(RKV0 + W_B), proj_spec(RKV0 + 2 * W_B),
        pl.BlockSpec((C, SM_W), lambda b, g, c: (blk0 + b * nch + c, SM0 // SM_W)),
        prev_spec(0), prev_spec(1), prev_spec(2),
        pl.BlockSpec((None, SUB, SM_W), lambda b, g, c: (bsel(b), 0, 0)),
        pl.BlockSpec((None, G, LANE, HEAD_B), lambda b, g, c: (bsel(b), g, 0, 0)),
        vec_spec(0), vec_spec(1), vec_spec(2),
        pl.BlockSpec((1, SM_W), lambda b, g, c: (0, 0)),
        pl.BlockSpec((LANE, gw), lambda b, g, c: (0, g)),
        pl.BlockSpec((LANE, gw), lambda b, g, c: (0, g)),
        pl.BlockSpec((2 * LANE, gw), lambda b, g, c: (0, g)),
        vec_spec(), vec_spec(), vec_spec(), vec_spec(), vec_spec(), vec_spec(), vec_spec(),
    ]
    return pl.pallas_call(
        functools.partial(_rwkv_kernel, C=C, G=G),
        out_shape=(jax.ShapeDtypeStruct((nb * seq, W_B), BF16),
                   jax.ShapeDtypeStruct((nb, N_PAIR, LANE, HEAD_B), F32)),
        grid=(nb, ng, nch),
        in_specs=in_specs,
        out_specs=(pl.BlockSpec((C, gw), lambda b, g, c: (b * nch + c, g)),
                   pl.BlockSpec((None, G, LANE, HEAD_B), lambda b, g, c: (b, g, 0, 0))),
        scratch_shapes=[pltpu.VMEM((SUB, gw), F32), pltpu.VMEM((SUB, gw), F32), pltpu.VMEM((SUB, gw), F32),
                        pltpu.VMEM((SUB, SM_W), F32), pltpu.VMEM((G, LANE, LANE), F32)],
        compiler_params=pltpu.CompilerParams(
            dimension_semantics=("parallel", "parallel", "arbitrary"), vmem_limit_bytes=VMEM_LIMIT),
        name="rwkv_mixer",
    )(proj, proj, proj, proj, prev_rkv, prev_rkv, prev_rkv, prev_sm, s0,
      mu_rkv, mu_rkv, mu_rkv, mu_sm, w2p, a2p, g2p, w0, a0, k_k, k_a, r_k, lnw, lnb)


def _small_layout(cols_ba, cols_w, cols_a, cols_g, lead):
    def z(n):
        return jnp.zeros(lead + (n,), cols_w.dtype)
    return jnp.concatenate(
        [cols_ba, z(SM_WA - cols_ba.shape[-1]), cols_w, cols_a, cols_g,
         z(SM_W - SM_G - cols_g.shape[-1])], axis=-1)


def _pad_rows8(x):
    b, n, c = x.shape
    return jnp.concatenate([jnp.zeros((b, SUB - n, c), x.dtype), x], axis=1)


def kernel(x_prompt, x_sample, state_delta, state_conv_qkv, state_wkv, state_shift, state_ffn_conv, meta, norm1, w_in, conv_a, a_log, dt_bias, onorm_a, mu_b, w0, w2, a0, a2, g2, k_k, k_a, r_k, lnx_w, lnx_b, w_o, norm2, w_ffn_in, conv_f, w_ffn_out, norm_f):
    nbp, seq_p, _ = x_prompt.shape
    nbs, seq_s, _ = x_sample.shape
    n_s = nbs * seq_s
    assert w_in.shape[0] == 1, "single-layer trunk"
    l = 0

    wi = w_in[l]
    o_b = A_PROJ
    o_l = A_PROJ + 3 * W_B
    w_cat = jnp.concatenate([
        wi[:, :4 * W_A], wi[:, o_b:o_l],
        _small_layout(wi[:, 4 * W_A:A_PROJ], wi[:, o_l:o_l + W_LORA],
                      wi[:, o_l + W_LORA:o_l + W_LORA + A_LORA], wi[:, o_l + W_LORA + A_LORA:],
                      (D_MODEL,))], axis=1).astype(BF16)
    mu = mu_b[l]
    mu_rkv = mu[None, :3 * W_B]
    mu_sm = _small_layout(jnp.zeros((1, 2 * H_A), F32), mu[None, 3 * W_B:3 * W_B + W_LORA],
                          mu[None, 3 * W_B + W_LORA:3 * W_B + W_LORA + A_LORA],
                          mu[None, 3 * W_B + W_LORA + A_LORA:], (1,))
    w2p = jnp.concatenate([w2[l], jnp.zeros((LANE - W_LORA, W_B), F32)], axis=0)
    a2p = jnp.concatenate([jnp.zeros((W_LORA, W_B), F32), a2[l]], axis=0)
    g2p = jnp.concatenate([g2[l], jnp.zeros((2 * LANE - G_LORA, W_B), F32)], axis=0)
    alog_row = jnp.concatenate([jnp.zeros((H_A,), F32), a_log[l], jnp.zeros((LANE - 2 * H_A,), F32)])[None]
    dtb_row = jnp.concatenate([jnp.zeros((H_A,), F32), dt_bias[l], jnp.zeros((LANE - 2 * H_A,), F32)])[None]
    wo_bf = w_o[l].astype(BF16)
    wfi_bf = w_ffn_in[l].astype(BF16)
    wfo_bf = w_ffn_out[l].astype(BF16)
    row = lambda v: v.reshape(1, -1)

    def mix(proj, row0, nb, seq, C, prev_qkv, prev_rkv, prev_sm, s_delta, s_wkv):
        oa, sd = _delta_mixer(proj, row0, nb, seq, C, prev_qkv, s_delta, conv_a[l], alog_row, dtb_row,
                              row(onorm_a[l]))
        ob, sw = _rwkv_mixer(proj, row0, nb, seq, C, 4, prev_rkv, prev_sm, s_wkv, mu_rkv, mu_sm,
                             w2p, a2p, g2p, row(w0[l]), row(a0[l]), row(k_k[l]), row(k_a[l]),
                             row(r_k[l]), row(lnx_w[l]), row(lnx_b[l]))
        return oa, ob, sd, sw

    xs_rows = jnp.concatenate([x_sample.reshape(n_s, D_MODEL), meta], axis=0)
    xp_rows = x_prompt.reshape(nbp * seq_p, D_MODEL)
    n_small = n_s + N_META
    proj_s = _norm_matmul(xs_rows, row(norm1[l]), w_cat, n_small, 512)
    proj_p = _norm_matmul(xp_rows, row(norm1[l]), w_cat, 1024, 512)

    zeros = lambda *s: jnp.zeros(s, F32)
    oa_m, ob_m, sd_m, sw_m = mix(proj_s, n_s, 1, N_META, N_META, zeros(1, SUB, 3 * W_A),
                                 zeros(1, SUB, 3 * W_B), zeros(1, SUB, SM_W),
                                 zeros(1, H_A, HEAD_A, HEAD_A), zeros(1, N_PAIR, LANE, HEAD_B))
    tail = proj_s[n_small - SUB:n_small]
    oa_p, ob_p, sd_p, sw_p = mix(proj_p, 0, nbp, seq_p, 64, tail[None, :, :3 * W_A],
                                 tail[None, :, RKV0:SM0], tail[None, :, SM0:], sd_m, sw_m)
    sh = state_shift[l]
    sh_sm = _small_layout(jnp.zeros((nbs, 1, 2 * H_A), F32), sh[..., 3 * W_B:3 * W_B + W_LORA],
                          sh[..., 3 * W_B + W_LORA:3 * W_B + W_LORA + A_LORA],
                          sh[..., 3 * W_B + W_LORA + A_LORA:], (nbs, 1))
    oa_s, ob_s, sd_s, sw_s = mix(proj_s, 0, nbs, seq_s, seq_s, _pad_rows8(state_conv_qkv[l]),
                                 _pad_rows8(sh[..., :3 * W_B]), _pad_rows8(sh_sm), state_delta[l],
                                 state_wkv[l].reshape(nbs, N_PAIR, LANE, HEAD_B))

    oa_small = jnp.concatenate([oa_s, oa_m], axis=0)
    ob_small = jnp.concatenate([ob_s, ob_m], axis=0)
    x1_s = _out_proj(xs_rows, oa_small, ob_small, wo_bf, n_small, 512)
    x1_p = _out_proj(xp_rows, oa_p, ob_p, wo_bf, 1024, 512)
    gu_s = _norm_matmul(x1_s, row(norm2[l]), wfi_bf, n_small, 512)
    gu_p = _norm_matmul(x1_p, row(norm2[l]), wfi_bf, 1024, 512)
    act_s = _ffn_act(gu_s, _pad_rows8(state_ffn_conv[l]), conv_f[l], nbs, seq_s, seq_s, D_FF)
    act_p = _ffn_act(gu_p, gu_s[None, n_small - SUB:n_small, :D_FF], conv_f[l], nbp, seq_p, 512, 768)
    y_s = _ffn_out(act_s, wfo_bf, x1_s, row(norm_f), n_s, 512, 768)
    y_p = _ffn_out(act_p, wfo_bf, x1_p, row(norm_f), nbp * seq_p, 512, 768)

    def states(proj, gu, nb, seq, sd, sw):
        p3 = proj[:nb * seq].reshape(nb, seq, P_CAT)
        conv_new = p3[:, seq - 3:, :3 * W_A]
        last = p3[:, seq - 1:, :]
        shift_new = jnp.concatenate([last[..., RKV0:SM0], last[..., SM0 + SM_WA:SM0 + SM_WA + W_LORA + A_LORA],
                                     last[..., SM0 + SM_G:SM0 + SM_G + G_LORA]], axis=-1)
        ffn_new = gu[:nb * seq].reshape(nb, seq, 2 * D_FF)[:, seq - 2:, :D_FF]
        return (sd[None], conv_new[None], sw.reshape(nb, H_B, HEAD_B, HEAD_B)[None], shift_new[None],
                ffn_new[None])

    return ((y_p.reshape(nbp, seq_p, D_MODEL), y_s.reshape(nbs, seq_s, D_MODEL))
            + states(proj_p, gu_p, nbp, seq_p, sd_p, sw_p)
            + states(proj_s, gu_s, nbs, seq_s, sd_s, sw_s))
```

```python
import functools

import jax
import jax.numpy as jnp
from jax import lax
from jax.experimental import pallas as pl
from jax.experimental.pallas import tpu as pltpu

F32 = jnp.float32
BF16 = jnp.bfloat16

D_MODEL = 2048
N_META = 16
W_A = 1024
HEAD_A = 128
H_A = 8
W_B = 1024
HEAD_B = 64
H_B = 16
N_PAIR = H_B // 2
W_LORA = 64
A_LORA = 64
G_LORA = 160
D_FF = 5376
RMS_EPS = 1e-6
GN_EPS = 64e-5
A_PROJ = 4 * W_A + 2 * H_A
B_PROJ = 3 * W_B + W_LORA + A_LORA + G_LORA

QKVZ0 = 0
RKV0 = 4 * W_A
SM0 = RKV0 + 3 * W_B
SM_W = 512
SM_BA = 0
SM_WA = 128
SM_G = 256
P_CAT = SM0 + SM_W

LANE = 128
SUB = 8
VMEM_LIMIT = 48 * 1024 * 1024
NEG_BIG = -1e30

NT_DIMS = (((1,), (1,)), ((), ()))


def _bdot(a, b):
    return jnp.dot(a.astype(BF16), b.astype(BF16), preferred_element_type=F32)


def _bdot_nt(a, b):
    return lax.dot_general(a.astype(BF16), b.astype(BF16), (((1,), (1,)), ((), ())),
                           preferred_element_type=F32)


def _bdot_tn(a, b):
    return lax.dot_general(a.astype(BF16), b.astype(BF16), (((0,), (0,)), ((), ())),
                           preferred_element_type=F32)


def _pieces(a, n):
    out = []
    rem = a
    for i in range(n):
        p = rem.astype(BF16)
        out.append(p)
        if i + 1 < n:
            rem = rem - p.astype(F32)
    return out


def _xdot(a, b, n, dims=(((1,), (0,)), ((), ()))):
    bb = b.astype(BF16)
    acc = None
    for p in _pieces(a, n):
        t = lax.dot_general(p, bb, dims, preferred_element_type=F32)
        acc = t if acc is None else acc + t
    return acc


def _xdot_r(a, b, n, dims=(((1,), (0,)), ((), ()))):
    ab = a.astype(BF16)
    acc = None
    for p in _pieces(b, n):
        t = lax.dot_general(ab, p, dims, preferred_element_type=F32)
        acc = t if acc is None else acc + t
    return acc


def _sigmoid(x):
    return 1.0 / (1.0 + jnp.exp(-x))


def _silu(x):
    return x * _sigmoid(x)


def _softplus(x):
    return jnp.maximum(x, 0.0) + jnp.log(1.0 + jnp.exp(-jnp.abs(x)))


def _iota(shape, dim):
    return lax.broadcasted_iota(jnp.int32, shape, dim)


def _shifted(x, prev8, k):
    n = x.shape[0]
    xr = pltpu.roll(x, k, 0)
    pr = pltpu.roll(prev8, k, 0)
    first = jnp.where(_iota((SUB, x.shape[1]), 0) < k, pr, xr[0:SUB])
    if n == SUB:
        return first
    return jnp.concatenate([first, xr[SUB:]], axis=0)


def _tri_inverse(x):
    c = x.shape[0]
    eye = (_iota((c, c), 0) == _iota((c, c), 1)).astype(F32)
    t = eye + x
    p = x
    n = 1
    while 2 * n < c:
        p = _bdot(p, p)
        t = t + _bdot(p, t)
        n *= 2
    return t


def _norm_matmul_kernel(x_ref, g_ref, w_ref, o_ref, h_ref):
    @pl.when(pl.program_id(1) == 0)
    def _():
        x = x_ref[...]
        ms = jnp.mean(x * x, axis=-1, keepdims=True)
        h_ref[...] = (x * lax.rsqrt(ms + RMS_EPS) * g_ref[...]).astype(BF16)

    o_ref[...] = jnp.dot(h_ref[...], w_ref[...], preferred_element_type=F32)


def _norm_matmul(x, g, w, tm, tn):
    m, k = x.shape
    n = w.shape[1]
    return pl.pallas_call(
        _norm_matmul_kernel,
        out_shape=jax.ShapeDtypeStruct((m, n), F32),
        grid=(m // tm, n // tn),
        in_specs=[
            pl.BlockSpec((tm, k), lambda i, j: (i, 0)),
            pl.BlockSpec((1, k), lambda i, j: (0, 0)),
            pl.BlockSpec((k, tn), lambda i, j: (0, j)),
        ],
        out_specs=pl.BlockSpec((tm, tn), lambda i, j: (i, j)),
        scratch_shapes=[pltpu.VMEM((tm, k), BF16)],
        compiler_params=pltpu.CompilerParams(
            dimension_semantics=("parallel", "arbitrary"), vmem_limit_bytes=VMEM_LIMIT),
        name="norm_matmul",
    )(x, g, w)


def _out_proj_kernel(x_ref, oa_ref, ob_ref, wt_ref, wb_ref, o_ref):
    acc = jnp.dot(oa_ref[...], wt_ref[...], preferred_element_type=F32)
    acc = acc + jnp.dot(ob_ref[...], wb_ref[...], preferred_element_type=F32)
    o_ref[...] = x_ref[...] + acc


def _out_proj(x, oa, ob, wo, tm, tn):
    m, d = x.shape
    return pl.pallas_call(
        _out_proj_kernel,
        out_shape=jax.ShapeDtypeStruct((m, d), F32),
        grid=(m // tm, d // tn),
        in_specs=[
            pl.BlockSpec((tm, tn), lambda i, j: (i, j)),
            pl.BlockSpec((tm, W_A), lambda i, j: (i, 0)),
            pl.BlockSpec((tm, W_B), lambda i, j: (i, 0)),
            pl.BlockSpec((W_A, tn), lambda i, j: (0, j)),
            pl.BlockSpec((W_B, tn), lambda i, j: (W_A // W_B, j)),
        ],
        out_specs=pl.BlockSpec((tm, tn), lambda i, j: (i, j)),
        compiler_params=pltpu.CompilerParams(
            dimension_semantics=("parallel", "arbitrary"), vmem_limit_bytes=VMEM_LIMIT),
        name="out_proj",
    )(x, oa, ob, wo, wo)


def _ffn_act_kernel(g_ref, u_ref, prev_ref, cw_ref, o_ref, carry_ref):
    @pl.when(pl.program_id(2) == 0)
    def _():
        carry_ref[...] = prev_ref[...]

    x = g_ref[...]
    n = x.shape[0]
    prev8 = carry_ref[...]
    y = x * cw_ref[2:3, :]
    y = y + _shifted(x, prev8, 1) * cw_ref[1:2, :]
    y = y + _shifted(x, prev8, 2) * cw_ref[0:1, :]
    carry_ref[...] = x[n - SUB:n]
    o_ref[...] = (_silu(y) * u_ref[...]).astype(BF16)


def _ffn_act(gu, prev, cw, nb, seq, tr, tc):
    ncol = D_FF // tc
    nrow = seq // tr
    bcast = prev.shape[0] == 1
    return pl.pallas_call(
        _ffn_act_kernel,
        out_shape=jax.ShapeDtypeStruct((nb * seq, D_FF), BF16),
        grid=(nb, ncol, nrow),
        in_specs=[
            pl.BlockSpec((tr, tc), lambda b, j, r: (b * nrow + r, j)),
            pl.BlockSpec((tr, tc), lambda b, j, r: (b * nrow + r, ncol + j)),
            pl.BlockSpec((None, SUB, tc), (lambda b, j, r: (0, 0, j)) if bcast else (lambda b, j, r: (b, 0, j))),
            pl.BlockSpec((3, tc), lambda b, j, r: (0, j)),
        ],
        out_specs=pl.BlockSpec((tr, tc), lambda b, j, r: (b * nrow + r, j)),
        scratch_shapes=[pltpu.VMEM((SUB, tc), F32)],
        compiler_params=pltpu.CompilerParams(
            dimension_semantics=("parallel", "parallel", "arbitrary"), vmem_limit_bytes=VMEM_LIMIT),
        name="ffn_act",
    )(gu, gu, prev, cw)


def _ffn_out_kernel(a_ref, w_ref, x_ref, g_ref, o_ref, acc_ref):
    kk = pl.program_id(1)

    @pl.when(kk == 0)
    def _():
        acc_ref[...] = x_ref[...]

    acc_ref[...] += jnp.dot(a_ref[...], w_ref[...], preferred_element_type=F32)

    @pl.when(kk == pl.num_programs(1) - 1)
    def _():
        x = acc_ref[...]
        ms = jnp.mean(x * x, axis=-1, keepdims=True)
        o_ref[...] = x * lax.rsqrt(ms + RMS_EPS) * g_ref[...]


def _ffn_out(act, w, x1, g, m, tm, tk):
    kdim, d = w.shape
    return pl.pallas_call(
        _ffn_out_kernel,
        out_shape=jax.ShapeDtypeStruct((m, d), F32),
        grid=(m // tm, kdim // tk),
        in_specs=[
            pl.BlockSpec((tm, tk), lambda i, k: (i, k)),
            pl.BlockSpec((tk, d), lambda i, k: (k, 0)),
            pl.BlockSpec((tm, d), lambda i, k: (i, 0)),
            pl.BlockSpec((1, d), lambda i, k: (0, 0)),
        ],
        out_specs=pl.BlockSpec((tm, d), lambda i, k: (i, 0)),
        scratch_shapes=[pltpu.VMEM((tm, d), F32)],
        compiler_params=pltpu.CompilerParams(
            dimension_semantics=("parallel", "arbitrary"), vmem_limit_bytes=VMEM_LIMIT),
        name="ffn_out",
    )(act, w, x1, g)


def _delta_kernel(qkvz_ref, ba_ref, prev_ref, s0_ref, cw_ref, alog_ref, dtb_ref, on_ref,
                  o_ref, s_ref, carry_ref, *, C):
    ci = pl.program_id(1)

    @pl.when(ci == 0)
    def _():
        carry_ref[...] = prev_ref[...]
        s_ref[...] = s0_ref[...]

    row = _iota((C, C), 0)
    col = _iota((C, C), 1)
    incl = row >= col
    strict = row > col
    tri = incl.astype(F32)

    ba = ba_ref[...]
    beta_full = _sigmoid(ba)
    g_full = -jnp.exp(alog_ref[...]) * _softplus(ba + dtb_ref[...])
    gc_full = _xdot_r(tri, g_full, 3)
    lane = _iota((C, LANE), 1)

    def conv_silu(c0):
        x = qkvz_ref[:, c0:c0 + LANE]
        prev8 = carry_ref[:, c0:c0 + LANE]
        y = x * cw_ref[3:4, c0:c0 + LANE]
        for k in (1, 2, 3):
            y = y + _shifted(x, prev8, k) * cw_ref[3 - k:4 - k, c0:c0 + LANE]
        return _silu(y)

    def l2n(x):
        return x * lax.rsqrt(jnp.sum(x * x, axis=-1, keepdims=True) + 1e-6)

    for h in range(H_A):
        q = l2n(conv_silu(h * HEAD_A)) * (HEAD_A ** -0.5)
        k = l2n(conv_silu(W_A + h * HEAD_A))
        v = conv_silu(2 * W_A + h * HEAD_A)
        z = qkvz_ref[:, 3 * W_A + h * HEAD_A:3 * W_A + (h + 1) * HEAD_A]
        bcol = beta_full[:, h:h + 1]
        gcol = gc_full[:, H_A + h:H_A + h + 1]
        onehot = (lane == H_A + h).astype(F32)
        grow = _xdot_r(onehot, gc_full, 3, NT_DIMS)
        gamma = jnp.exp(jnp.where(incl, gcol - grow, NEG_BIG))
        glast = gc_full[C - 1:C, H_A + h:H_A + h + 1]
        eg = jnp.exp(gcol)
        kb = k * bcol
        lmat = jnp.where(strict, _bdot_nt(kb, k) * gamma, 0.0)
        tinv = _tri_inverse(-lmat)
        u = _bdot(tinv, v * bcol)
        w = _bdot(tinv, kb * eg)
        attn = _bdot_nt(q, k) * gamma
        qd = q * eg
        kd = k * jnp.exp(glast - gcol)
        s = s_ref[h]
        ws = _bdot(w, s)
        qs = _bdot(qd, s)
        v_new = u - ws
        o = qs + _bdot(attn, v_new)
        s_ref[h] = s * jnp.exp(glast) + _bdot_tn(kd, v_new)
        o = o * lax.rsqrt(jnp.mean(o * o, axis=-1, keepdims=True) + RMS_EPS)
        o = o * on_ref[...] * _silu(z)
        o_ref[:, h * HEAD_A:(h + 1) * HEAD_A] = o.astype(BF16)

    carry_ref[...] = qkvz_ref[C - SUB:C, 0:3 * W_A]


def _delta_mixer(proj, row0, nb, seq, C, prev, s0, cw, alog_row, dtb_row, onorm):
    nch = seq // C
    blk0 = row0 // C
    bcast = s0.shape[0] == 1
    bsel = (lambda b: 0) if bcast else (lambda b: b)
    return pl.pallas_call(
        functools.partial(_delta_kernel, C=C),
        out_shape=(jax.ShapeDtypeStruct((nb * seq, W_A), BF16),
                   jax.ShapeDtypeStruct((nb, H_A, HEAD_A, HEAD_A), F32)),
        grid=(nb, nch),
        in_specs=[
            pl.BlockSpec((C, 4 * W_A), lambda b, c: (blk0 + b * nch + c, 0)),
            pl.BlockSpec((C, LANE), lambda b, c: (blk0 + b * nch + c, SM0 // LANE)),
            pl.BlockSpec((None, SUB, 3 * W_A), lambda b, c: (bsel(b), 0, 0)),
            pl.BlockSpec((None, H_A, HEAD_A, HEAD_A), lambda b, c: (bsel(b), 0, 0, 0)),
            pl.BlockSpec((4, 3 * W_A), lambda b, c: (0, 0)),
            pl.BlockSpec((1, LANE), lambda b, c: (0, 0)),
            pl.BlockSpec((1, LANE), lambda b, c: (0, 0)),
            pl.BlockSpec((1, HEAD_A), lambda b, c: (0, 0)),
        ],
        out_specs=(pl.BlockSpec((C, W_A), lambda b, c: (b * nch + c, 0)),
                   pl.BlockSpec((None, H_A, HEAD_A, HEAD_A), lambda b, c: (b, 0, 0, 0))),
        scratch_shapes=[pltpu.VMEM((SUB, 3 * W_A), F32)],
        compiler_params=pltpu.CompilerParams(
            dimension_semantics=("parallel", "arbitrary"), vmem_limit_bytes=VMEM_LIMIT),
        name="delta_mixer",
    )(proj, proj, prev, s0, cw, alog_row, dtb_row, onorm)


def _rwkv_kernel(r_ref, k_ref, v_ref, sm_ref, pr_ref, pk_ref, pv_ref, psm_ref, s0_ref,
                 mur_ref, muk_ref, muv_ref, musm_ref, w2_ref, a2_ref, g2_ref,
                 w0_ref, a0_ref, kk_ref, ka_ref, rk_ref, lnw_ref, lnb_ref,
                 o_ref, s_ref, cr_ref, ck_ref, cv_ref, csm_ref, st_ref, *, C, G):
    ci = pl.program_id(2)
    nch = pl.num_programs(2)

    r2 = _iota((LANE, LANE), 0)
    c2 = _iota((LANE, LANE), 1)
    same_head = (r2 < HEAD_B) == (c2 < HEAD_B)
    ones_bd = same_head.astype(F32)
    spread = (_iota((HEAD_B, LANE), 0) == (_iota((HEAD_B, LANE), 1) & (HEAD_B - 1))).astype(F32)
    gather = ((_iota((LANE, HEAD_B), 0) & (HEAD_B - 1)) == _iota((LANE, HEAD_B), 1)).astype(F32)

    @pl.when(ci == 0)
    def _():
        cr_ref[...] = pr_ref[...]
        ck_ref[...] = pk_ref[...]
        cv_ref[...] = pv_ref[...]
        csm_ref[...] = psm_ref[...]
        for p in range(G):
            st_ref[p] = jnp.where(same_head, _xdot(s0_ref[p], spread, 3), 0.0)

    row = _iota((C, C), 0)
    col = _iota((C, C), 1)
    incl = row >= col
    strict = row > col
    tri = incl.astype(F32)
    head0 = _iota((C, LANE), 1) < HEAD_B
    head0_2 = _iota((2 * C, LANE), 1) < HEAD_B

    def lerp(x, prev8, mu):
        return x + (_shifted(x, prev8, 1) - x) * mu

    sm = sm_ref[...]
    xs = lerp(sm, csm_ref[...], musm_ref[...])
    slab_wa = xs[:, SM_WA:SM_WA + LANE]
    slab_g = xs[:, SM_G:SM_G + 2 * LANE]
    tanh_wa = jnp.tanh(slab_wa)
    sig_g = _sigmoid(slab_g)

    for p in range(G):
        cs = slice(p * LANE, (p + 1) * LANE)
        xr = lerp(r_ref[:, cs], cr_ref[:, cs], mur_ref[:, cs])
        xk = lerp(k_ref[:, cs], ck_ref[:, cs], muk_ref[:, cs])
        xv = lerp(v_ref[:, cs], cv_ref[:, cs], muv_ref[:, cs])
        wl = _bdot(tanh_wa, w2_ref[:, cs])
        wlog = -_softplus(-(w0_ref[:, cs] + wl)) - 0.5
        ld = -jnp.exp(wlog)
        aa = _sigmoid(a0_ref[:, cs] + _bdot(slab_wa, a2_ref[:, cs]))
        gate = _bdot(sig_g, g2_ref[:, cs])
        kkr = xk * kk_ref[:, cs]
        kkn = kkr * lax.rsqrt(_xdot(kkr * kkr, ones_bd, 2) + 1e-6)
        k2 = xk * (1.0 + (aa - 1.0) * ka_ref[:, cs])

        lp = _xdot_r(tri, ld, 3)
        e_pos = jnp.exp(lp)
        e_neg = jnp.exp(-lp)
        lp_last = lp[C - 1:C, :]
        e_rem = jnp.exp(lp_last - lp)
        at = -kkn * jnp.exp(lp - ld)
        rt = xr * e_pos
        bt = kkn * aa * e_neg
        kt = k2 * e_neg
        bhat = kkn * aa * e_rem
        khat = k2 * e_rem

        s = st_ref[p]
        ar = jnp.concatenate([at, rt], axis=0)
        ars = _bdot_nt(ar, s)
        x_state = ars[0:C]
        o_state = ars[C:2 * C]

        us, arbs, arks = [], [], []
        for hh in range(2):
            arh = jnp.where(head0_2 if hh == 0 else jnp.logical_not(head0_2), ar, 0.0)
            ab = _bdot_nt(arh, bt)
            ak = _bdot_nt(arh, kt)
            lab = jnp.where(strict, ab[0:C], 0.0)
            lak = jnp.where(strict, ak[0:C], 0.0)
            arbs.append(jnp.where(incl, ab[C:2 * C], 0.0))
            arks.append(jnp.where(incl, ak[C:2 * C], 0.0))
            us.append(_bdot(_tri_inverse(lab), x_state + _bdot(lak, xv)))
        u_pair = jnp.where(head0, us[0], us[1])
        o_pair = jnp.where(head0, _bdot(arbs[0], u_pair) + _bdot(arks[0], xv),
                           _bdot(arbs[1], u_pair) + _bdot(arks[1], xv))
        uv = jnp.concatenate([u_pair, xv], axis=0)
        o = o_state + o_pair

        bkhat = jnp.concatenate([bhat, khat], axis=0)
        s_new = s * jnp.exp(lp_last) + _bdot_tn(uv, bkhat)
        st_ref[p] = jnp.where(same_head, s_new, 0.0)

        mean = _xdot(o, ones_bd, 2) * (1.0 / HEAD_B)
        d = o - mean
        var = _xdot(d * d, ones_bd, 2) * (1.0 / HEAD_B)
        on = d * lax.rsqrt(var + GN_EPS) * lnw_ref[:, cs] + lnb_ref[:, cs]
        bonus = _xdot(xr * k2 * rk_ref[:, cs], ones_bd, 2) * xv
        o_ref[:, cs] = ((on + bonus) * gate).astype(BF16)

    cr_ref[...] = r_ref[C - SUB:C, :]
    ck_ref[...] = k_ref[C - SUB:C, :]
    cv_ref[...] = v_ref[C - SUB:C, :]
    csm_ref[...] = sm_ref[C - SUB:C, :]

    @pl.when(ci == nch - 1)
    def _():
        for p in range(G):
            s_ref[p] = _xdot(st_ref[p], gather, 3)


def _rwkv_mixer(proj, row0, nb, seq, C, G, prev_rkv, prev_sm, s0, mu_rkv, mu_sm, w2p, a2p, g2p,
                w0, a0, k_k, k_a, r_k, lnw, lnb):
    nch = seq // C
    ng = N_PAIR // G
    gw = G * LANE
    blk0 = row0 // C
    bcast = s0.shape[0] == 1
    bsel = (lambda b: 0) if bcast else (lambda b: b)

    def proj_spec(col0):
        return pl.BlockSpec((C, gw), lambda b, g, c: (blk0 + b * nch + c, col0 // gw + g))

    def prev_spec(part):
        return pl.BlockSpec((None, SUB, gw), lambda b, g, c: (bsel(b), 0, part * (W_B // gw) + g))

    def vec_spec(part=0):
        return pl.BlockSpec((1, gw), lambda b, g, c: (0, part * (W_B // gw) + g))

    in_specs = [
        proj_spec(RKV0), proj_spec(RKV0 + W_B), proj_spec(RKV0 + 2 * W_B),
        pl.BlockSpec((C, SM_W), lambda b, g, c: (blk0 + b * nch + c, SM0 // SM_W)),
        prev_spec(0), prev_spec(1), prev_spec(2),
        pl.BlockSpec((None, SUB, SM_W), lambda b, g, c: (bsel(b), 0, 0)),
        pl.BlockSpec((None, G, LANE, HEAD_B), lambda b, g, c: (bsel(b), g, 0, 0)),
        vec_spec(0), vec_spec(1), vec_spec(2),
        pl.BlockSpec((1, SM_W), lambda b, g, c: (0, 0)),
        pl.BlockSpec((LANE, gw), lambda b, g, c: (0, g)),
        pl.BlockSpec((LANE, gw), lambda b, g, c: (0, g)),
        pl.BlockSpec((2 * LANE, gw), lambda b, g, c: (0, g)),
        vec_spec(), vec_spec(), vec_spec(), vec_spec(), vec_spec(), vec_spec(), vec_spec(),
    ]
    return pl.pallas_call(
        functools.partial(_rwkv_kernel, C=C, G=G),
        out_shape=(jax.ShapeDtypeStruct((nb * seq, W_B), BF16),
                   jax.ShapeDtypeStruct((nb, N_PAIR, LANE, HEAD_B), F32)),
        grid=(nb, ng, nch),
        in_specs=in_specs,
        out_specs=(pl.BlockSpec((C, gw), lambda b, g, c: (b * nch + c, g)),
                   pl.BlockSpec((None, G, LANE, HEAD_B), lambda b, g, c: (b, g, 0, 0))),
        scratch_shapes=[pltpu.VMEM((SUB, gw), F32), pltpu.VMEM((SUB, gw), F32), pltpu.VMEM((SUB, gw), F32),
                        pltpu.VMEM((SUB, SM_W), F32), pltpu.VMEM((G, LANE, LANE), F32)],
        compiler_params=pltpu.CompilerParams(
            dimension_semantics=("parallel", "parallel", "arbitrary"), vmem_limit_bytes=VMEM_LIMIT),
        name="rwkv_mixer",
    )(proj, proj, proj, proj, prev_rkv, prev_rkv, prev_rkv, prev_sm, s0,
      mu_rkv, mu_rkv, mu_rkv, mu_sm, w2p, a2p, g2p, w0, a0, k_k, k_a, r_k, lnw, lnb)


def _small_layout(cols_ba, cols_w, cols_a, cols_g, lead):
    def z(n):
        return jnp.zeros(lead + (n,), cols_w.dtype)
    return jnp.concatenate(
        [cols_ba, z(SM_WA - cols_ba.shape[-1]), cols_w, cols_a, cols_g,
         z(SM_W - SM_G - cols_g.shape[-1])], axis=-1)


def _pad_rows8(x):
    b, n, c = x.shape
    return jnp.concatenate([jnp.zeros((b, SUB - n, c), x.dtype), x], axis=1)


def kernel(x_prompt, x_sample, state_delta, state_conv_qkv, state_wkv, state_shift, state_ffn_conv, meta, norm1, w_in, conv_a, a_log, dt_bias, onorm_a, mu_b, w0, w2, a0, a2, g2, k_k, k_a, r_k, lnx_w, lnx_b, w_o, norm2, w_ffn_in, conv_f, w_ffn_out, norm_f):
    nbp, seq_p, _ = x_prompt.shape
    nbs, seq_s, _ = x_sample.shape
    n_s = nbs * seq_s
    assert w_in.shape[0] == 1, "single-layer trunk"
    l = 0

    wi = w_in[l]
    o_b = A_PROJ
    o_l = A_PROJ + 3 * W_B
    w_cat = jnp.concatenate([
        wi[:, :4 * W_A], wi[:, o_b:o_l],
        _small_layout(wi[:, 4 * W_A:A_PROJ], wi[:, o_l:o_l + W_LORA],
                      wi[:, o_l + W_LORA:o_l + W_LORA + A_LORA], wi[:, o_l + W_LORA + A_LORA:],
                      (D_MODEL,))], axis=1).astype(BF16)
    mu = mu_b[l]
    mu_rkv = mu[None, :3 * W_B]
    mu_sm = _small_layout(jnp.zeros((1, 2 * H_A), F32), mu[None, 3 * W_B:3 * W_B + W_LORA],
                          mu[None, 3 * W_B + W_LORA:3 * W_B + W_LORA + A_LORA],
                          mu[None, 3 * W_B + W_LORA + A_LORA:], (1,))
    w2p = jnp.concatenate([w2[l], jnp.zeros((LANE - W_LORA, W_B), F32)], axis=0)
    a2p = jnp.concatenate([jnp.zeros((W_LORA, W_B), F32), a2[l]], axis=0)
    g2p = jnp.concatenate([g2[l], jnp.zeros((2 * LANE - G_LORA, W_B), F32)], axis=0)
    alog_row = jnp.concatenate([jnp.zeros((H_A,), F32), a_log[l], jnp.zeros((LANE - 2 * H_A,), F32)])[None]
    dtb_row = jnp.concatenate([jnp.zeros((H_A,), F32), dt_bias[l], jnp.zeros((LANE - 2 * H_A,), F32)])[None]
    wo_bf = w_o[l].astype(BF16)
    wfi_bf = w_ffn_in[l].astype(BF16)
    wfo_bf = w_ffn_out[l].astype(BF16)
    row = lambda v: v.reshape(1, -1)

    def mix(proj, row0, nb, seq, C, prev_qkv, prev_rkv, prev_sm, s_delta, s_wkv):
        oa, sd = _delta_mixer(proj, row0, nb, seq, C, prev_qkv, s_delta, conv_a[l], alog_row, dtb_row,
                              row(onorm_a[l]))
        ob, sw = _rwkv_mixer(proj, row0, nb, seq, C, 4, prev_rkv, prev_sm, s_wkv, mu_rkv, mu_sm,
                             w2p, a2p, g2p, row(w0[l]), row(a0[l]), row(k_k[l]), row(k_a[l]),
                             row(r_k[l]), row(lnx_w[l]), row(lnx_b[l]))
        return oa, ob, sd, sw

    xs_rows = jnp.concatenate([x_sample.reshape(n_s, D_MODEL), meta], axis=0)
    xp_rows = x_prompt.reshape(nbp * seq_p, D_MODEL)
    n_small = n_s + N_META
    proj_s = _norm_matmul(xs_rows, row(norm1[l]), w_cat, n_small, 512)
    proj_p = _norm_matmul(xp_rows, row(norm1[l]), w_cat, 1024, 512)

    zeros = lambda *s: jnp.zeros(s, F32)
    oa_m, ob_m, sd_m, sw_m = mix(proj_s, n_s, 1, N_META, N_META, zeros(1, SUB, 3 * W_A),
                                 zeros(1, SUB, 3 * W_B), zeros(1, SUB, SM_W),
                                 zeros(1, H_A, HEAD_A, HEAD_A), zeros(1, N_PAIR, LANE, HEAD_B))
    tail = proj_s[n_small - SUB:n_small]
    oa_p, ob_p, sd_p, sw_p = mix(proj_p, 0, nbp, seq_p, 64, tail[None, :, :3 * W_A],
                                 tail[None, :, RKV0:SM0], tail[None, :, SM0:], sd_m, sw_m)
    sh = state_shift[l]
    sh_sm = _small_layout(jnp.zeros((nbs, 1, 2 * H_A), F32), sh[..., 3 * W_B:3 * W_B + W_LORA],
                          sh[..., 3 * W_B + W_LORA:3 * W_B + W_LORA + A_LORA],
                          sh[..., 3 * W_B + W_LORA + A_LORA:], (nbs, 1))
    oa_s, ob_s, sd_s, sw_s = mix(proj_s, 0, nbs, seq_s, seq_s, _pad_rows8(state_conv_qkv[l]),
                                 _pad_rows8(sh[..., :3 * W_B]), _pad_rows8(sh_sm), state_delta[l],
                                 state_wkv[l].reshape(nbs, N_PAIR, LANE, HEAD_B))

    oa_small = jnp.concatenate([oa_s, oa_m], axis=0)
    ob_small = jnp.concatenate([ob_s, ob_m], axis=0)
    x1_s = _out_proj(xs_rows, oa_small, ob_small, wo_bf, n_small, 512)
    x1_p = _out_proj(xp_rows, oa_p, ob_p, wo_bf, 1024, 512)
    gu_s = _norm_matmul(x1_s, row(norm2[l]), wfi_bf, n_small, 512)
    gu_p = _norm_matmul(x1_p, row(norm2[l]), wfi_bf, 1024, 512)
    act_s = _ffn_act(gu_s, _pad_rows8(state_ffn_conv[l]), conv_f[l], nbs, seq_s, seq_s, D_FF)
    act_p = _ffn_act(gu_p, gu_s[None, n_small - SUB:n_small, :D_FF], conv_f[l], nbp, seq_p, 512, 768)
    y_s = _ffn_out(act_s, wfo_bf, x1_s, row(norm_f), n_s, 512, 768)
    y_p = _ffn_out(act_p, wfo_bf, x1_p, row(norm_f), nbp * seq_p, 512, 768)

    def states(proj, gu, nb, seq, sd, sw):
        p3 = proj[:nb * seq].reshape(nb, seq, P_CAT)
        conv_new = p3[:, seq - 3:, :3 * W_A]
        last = p3[:, seq - 1:, :]
        shift_new = jnp.concatenate([last[..., RKV0:SM0], last[..., SM0 + SM_WA:SM0 + SM_WA + W_LORA + A_LORA],
                                     last[..., SM0 + SM_G:SM0 + SM_G + G_LORA]], axis=-1)
        ffn_new = gu[:nb * seq].reshape(nb, seq, 2 * D_FF)[:, seq - 2:, :D_FF]
        return (sd[None], conv_new[None], sw.reshape(nb, H_B, HEAD_B, HEAD_B)[None], shift_new[None],
                ffn_new[None])

    return ((y_p.reshape(nbp, seq_p, D_MODEL), y_s.reshape(nbs, seq_s, D_MODEL))
            + states(proj_p, gu_p, nbp, seq_p, sd_p, sw_p)
            + states(proj_s, gu_s, nbs, seq_s, sd_s, sw_s))
```

```python
import functools

import jax
import jax.numpy as jnp
from jax import lax
from jax.experimental import pallas as pl
from jax.experimental.pallas import tpu as pltpu

F32 = jnp.float32
BF16 = jnp.bfloat16

D_MODEL = 2048
N_META = 16
W_A = 1024
HEAD_A = 128
H_A = 8
W_B = 1024
HEAD_B = 64
H_B = 16
N_PAIR = H_B // 2
W_LORA = 64
A_LORA = 64
G_LORA = 160
D_FF = 5376
RMS_EPS = 1e-6
GN_EPS = 64e-5
A_PROJ = 4 * W_A + 2 * H_A
B_PROJ = 3 * W_B + W_LORA + A_LORA + G_LORA

QKVZ0 = 0
RKV0 = 4 * W_A
SM0 = RKV0 + 3 * W_B
SM_W = 512
SM_BA = 0
SM_WA = 128
SM_G = 256
P_CAT = SM0 + SM_W

LANE = 128
SUB = 8
VMEM_LIMIT = 48 * 1024 * 1024
NEG_BIG = -1e30

NT_DIMS = (((1,), (1,)), ((), ()))


def _bdot(a, b):
    return jnp.dot(a.astype(BF16), b.astype(BF16), preferred_element_type=F32)


def _bdot_nt(a, b):
    return lax.dot_general(a.astype(BF16), b.astype(BF16), (((1,), (1,)), ((), ())),
                           preferred_element_type=F32)


def _bdot_tn(a, b):
    return lax.dot_general(a.astype(BF16), b.astype(BF16), (((0,), (0,)), ((), ())),
                           preferred_element_type=F32)


def _pieces(a, n):
    out = []
    rem = a
    for i in range(n):
        p = rem.astype(BF16)
        out.append(p)
        if i + 1 < n:
            rem = rem - p.astype(F32)
    return out


def _xdot(a, b, n, dims=(((1,), (0,)), ((), ()))):
    bb = b.astype(BF16)
    acc = None
    for p in _pieces(a, n):
        t = lax.dot_general(p, bb, dims, preferred_element_type=F32)
        acc = t if acc is None else acc + t
    return acc


def _xdot_r(a, b, n, dims=(((1,), (0,)), ((), ()))):
    ab = a.astype(BF16)
    acc = None
    for p in _pieces(b, n):
        t = lax.dot_general(ab, p, dims, preferred_element_type=F32)
        acc = t if acc is None else acc + t
    return acc


def _sigmoid(x):
    return 1.0 / (1.0 + jnp.exp(-x))


def _silu(x):
    return x * _sigmoid(x)


def _softplus(x):
    return jnp.maximum(x, 0.0) + jnp.log(1.0 + jnp.exp(-jnp.abs(x)))


def _iota(shape, dim):
    return lax.broadcasted_iota(jnp.int32, shape, dim)


def _shifted(x, prev8, k):
    n = x.shape[0]
    xr = pltpu.roll(x, k, 0)
    pr = pltpu.roll(prev8, k, 0)
    first = jnp.where(_iota((SUB, x.shape[1]), 0) < k, pr, xr[0:SUB])
    if n == SUB:
        return first
    return jnp.concatenate([first, xr[SUB:]], axis=0)


def _tri_inverse_many(xs):
    c = xs[0].shape[0]
    eye = (_iota((c, c), 0) == _iota((c, c), 1)).astype(F32)
    ts = [eye + x for x in xs]
    ps = list(xs)
    n = 1
    while 2 * n < c:
        ps = [_bdot(p, p) for p in ps]
        ts = [t + _bdot(p, t) for p, t in zip(ps, ts)]
        n *= 2
    return ts


def _norm_matmul_kernel(x_ref, g_ref, w_ref, o_ref, h_ref):
    @pl.when(pl.program_id(1) == 0)
    def _():
        x = x_ref[...]
        ms = jnp.mean(x * x, axis=-1, keepdims=True)
        h_ref[...] = (x * lax.rsqrt(ms + RMS_EPS) * g_ref[...]).astype(BF16)

    o_ref[...] = jnp.dot(h_ref[...], w_ref[...], preferred_element_type=F32)


def _norm_matmul(x, g, w, tm, tn):
    m, k = x.shape
    n = w.shape[1]
    return pl.pallas_call(
        _norm_matmul_kernel,
        out_shape=jax.ShapeDtypeStruct((m, n), F32),
        grid=(m // tm, n // tn),
        in_specs=[
            pl.BlockSpec((tm, k), lambda i, j: (i, 0)),
            pl.BlockSpec((1, k), lambda i, j: (0, 0)),
            pl.BlockSpec((k, tn), lambda i, j: (0, j)),
        ],
        out_specs=pl.BlockSpec((tm, tn), lambda i, j: (i, j)),
        scratch_shapes=[pltpu.VMEM((tm, k), BF16)],
        compiler_params=pltpu.CompilerParams(
            dimension_semantics=("parallel", "arbitrary"), vmem_limit_bytes=VMEM_LIMIT),
        name="norm_matmul",
    )(x, g, w)


def _out_proj_kernel(x_ref, oa_ref, ob_ref, wt_ref, wb_ref, o_ref):
    acc = jnp.dot(oa_ref[...], wt_ref[...], preferred_element_type=F32)
    acc = acc + jnp.dot(ob_ref[...], wb_ref[...], preferred_element_type=F32)
    o_ref[...] = x_ref[...] + acc


def _out_proj(x, oa, ob, wo, tm, tn):
    m, d = x.shape
    return pl.pallas_call(
        _out_proj_kernel,
        out_shape=jax.ShapeDtypeStruct((m, d), F32),
        grid=(m // tm, d // tn),
        in_specs=[
            pl.BlockSpec((tm, tn), lambda i, j: (i, j)),
            pl.BlockSpec((tm, W_A), lambda i, j: (i, 0)),
            pl.BlockSpec((tm, W_B), lambda i, j: (i, 0)),
            pl.BlockSpec((W_A, tn), lambda i, j: (0, j)),
            pl.BlockSpec((W_B, tn), lambda i, j: (W_A // W_B, j)),
        ],
        out_specs=pl.BlockSpec((tm, tn), lambda i, j: (i, j)),
        compiler_params=pltpu.CompilerParams(
            dimension_semantics=("parallel", "arbitrary"), vmem_limit_bytes=VMEM_LIMIT),
        name="out_proj",
    )(x, oa, ob, wo, wo)


def _ffn_act_kernel(g_ref, u_ref, prev_ref, cw_ref, o_ref, carry_ref):
    @pl.when(pl.program_id(2) == 0)
    def _():
        carry_ref[...] = prev_ref[...]

    x = g_ref[...]
    n = x.shape[0]
    prev8 = carry_ref[...]
    y = x * cw_ref[2:3, :]
    y = y + _shifted(x, prev8, 1) * cw_ref[1:2, :]
    y = y + _shifted(x, prev8, 2) * cw_ref[0:1, :]
    carry_ref[...] = x[n - SUB:n]
    o_ref[...] = (_silu(y) * u_ref[...]).astype(BF16)


def _ffn_act(gu, prev, cw, nb, seq, tr, tc):
    ncol = D_FF // tc
    nrow = seq // tr
    bcast = prev.shape[0] == 1
    return pl.pallas_call(
        _ffn_act_kernel,
        out_shape=jax.ShapeDtypeStruct((nb * seq, D_FF), BF16),
        grid=(nb, ncol, nrow),
        in_specs=[
            pl.BlockSpec((tr, tc), lambda b, j, r: (b * nrow + r, j)),
            pl.BlockSpec((tr, tc), lambda b, j, r: (b * nrow + r, ncol + j)),
            pl.BlockSpec((None, SUB, tc), (lambda b, j, r: (0, 0, j)) if bcast else (lambda b, j, r: (b, 0, j))),
            pl.BlockSpec((3, tc), lambda b, j, r: (0, j)),
        ],
        out_specs=pl.BlockSpec((tr, tc), lambda b, j, r: (b * nrow + r, j)),
        scratch_shapes=[pltpu.VMEM((SUB, tc), F32)],
        compiler_params=pltpu.CompilerParams(
            dimension_semantics=("parallel", "parallel", "arbitrary"), vmem_limit_bytes=VMEM_LIMIT),
        name="ffn_act",
    )(gu, gu, prev, cw)


def _ffn_out_kernel(a_ref, w_ref, x_ref, g_ref, o_ref, acc_ref):
    kk = pl.program_id(1)

    @pl.when(kk == 0)
    def _():
        acc_ref[...] = x_ref[...]

    acc_ref[...] += jnp.dot(a_ref[...], w_ref[...], preferred_element_type=F32)

    @pl.when(kk == pl.num_programs(1) - 1)
    def _():
        x = acc_ref[...]
        ms = jnp.mean(x * x, axis=-1, keepdims=True)
        o_ref[...] = x * lax.rsqrt(ms + RMS_EPS) * g_ref[...]


def _ffn_out(act, w, x1, g, m, tm, tk):
    kdim, d = w.shape
    return pl.pallas_call(
        _ffn_out_kernel,
        out_shape=jax.ShapeDtypeStruct((m, d), F32),
        grid=(m // tm, kdim // tk),
        in_specs=[
            pl.BlockSpec((tm, tk), lambda i, k: (i, k)),
            pl.BlockSpec((tk, d), lambda i, k: (k, 0)),
            pl.BlockSpec((tm, d), lambda i, k: (i, 0)),
            pl.BlockSpec((1, d), lambda i, k: (0, 0)),
        ],
        out_specs=pl.BlockSpec((tm, d), lambda i, k: (i, 0)),
        scratch_shapes=[pltpu.VMEM((tm, d), F32)],
        compiler_params=pltpu.CompilerParams(
            dimension_semantics=("parallel", "arbitrary"), vmem_limit_bytes=VMEM_LIMIT),
        name="ffn_out",
    )(act, w, x1, g)


def _delta_kernel(qkvz_ref, ba_ref, prev_ref, s0_ref, cw_ref, alog_ref, dtb_ref, on_ref,
                  o_ref, s_ref, carry_ref, *, C):
    ci = pl.program_id(1)

    @pl.when(ci == 0)
    def _():
        carry_ref[...] = prev_ref[...]
        s_ref[...] = s0_ref[...]

    row = _iota((C, C), 0)
    col = _iota((C, C), 1)
    incl = row >= col
    strict = row > col
    tri = incl.astype(F32)

    ba = ba_ref[...]
    beta_full = _sigmoid(ba)
    g_full = -jnp.exp(alog_ref[...]) * _softplus(ba + dtb_ref[...])
    gc_full = _xdot_r(tri, g_full, 3)
    lane = _iota((C, LANE), 1)

    def conv_silu(c0):
        x = qkvz_ref[:, c0:c0 + LANE]
        prev8 = carry_ref[:, c0:c0 + LANE]
        y = x * cw_ref[3:4, c0:c0 + LANE]
        for k in (1, 2, 3):
            y = y + _shifted(x, prev8, k) * cw_ref[3 - k:4 - k, c0:c0 + LANE]
        return _silu(y)

    def l2n(x):
        return x * lax.rsqrt(jnp.sum(x * x, axis=-1, keepdims=True) + 1e-6)

    heads = range(H_A)
    q = [l2n(conv_silu(h * HEAD_A)) * (HEAD_A ** -0.5) for h in heads]
    k = [l2n(conv_silu(W_A + h * HEAD_A)) for h in heads]
    v = [conv_silu(2 * W_A + h * HEAD_A) for h in heads]
    bcol = [beta_full[:, h:h + 1] for h in heads]
    gcol = [gc_full[:, H_A + h:H_A + h + 1] for h in heads]
    glast = [gc_full[C - 1:C, H_A + h:H_A + h + 1] for h in heads]
    grow = [_xdot_r((lane == H_A + h).astype(F32), gc_full, 3, NT_DIMS) for h in heads]
    gamma = [jnp.exp(jnp.where(incl, gcol[h] - grow[h], NEG_BIG)) for h in heads]
    eg = [jnp.exp(gcol[h]) for h in heads]
    kb = [k[h] * bcol[h] for h in heads]
    lmat = [jnp.where(strict, _bdot_nt(kb[h], k[h]) * gamma[h], 0.0) for h in heads]
    attn = [_bdot_nt(q[h], k[h]) * gamma[h] for h in heads]
    tinv = _tri_inverse_many([-m for m in lmat])
    uw = [_bdot(tinv[h], jnp.concatenate([v[h] * bcol[h], kb[h] * eg[h]], axis=1)) for h in heads]
    s = [s_ref[h] for h in heads]
    wqs = [_bdot(jnp.concatenate([uw[h][:, HEAD_A:], q[h] * eg[h]], axis=0), s[h]) for h in heads]
    v_new = [uw[h][:, :HEAD_A] - wqs[h][0:C] for h in heads]
    o = [wqs[h][C:2 * C] + _bdot(attn[h], v_new[h]) for h in heads]
    for h in heads:
        kd = k[h] * jnp.exp(glast[h] - gcol[h])
        s_ref[h] = s[h] * jnp.exp(glast[h]) + _bdot_tn(kd, v_new[h])
    for h in heads:
        z = qkvz_ref[:, 3 * W_A + h * HEAD_A:3 * W_A + (h + 1) * HEAD_A]
        oh = o[h] * lax.rsqrt(jnp.mean(o[h] * o[h], axis=-1, keepdims=True) + RMS_EPS)
        o_ref[:, h * HEAD_A:(h + 1) * HEAD_A] = (oh * on_ref[...] * _silu(z)).astype(BF16)

    carry_ref[...] = qkvz_ref[C - SUB:C, 0:3 * W_A]


def _delta_mixer(proj, row0, nb, seq, C, prev, s0, cw, alog_row, dtb_row, onorm):
    nch = seq // C
    blk0 = row0 // C
    bcast = s0.shape[0] == 1
    bsel = (lambda b: 0) if bcast else (lambda b: b)
    return pl.pallas_call(
        functools.partial(_delta_kernel, C=C),
        out_shape=(jax.ShapeDtypeStruct((nb * seq, W_A), BF16),
                   jax.ShapeDtypeStruct((nb, H_A, HEAD_A, HEAD_A), F32)),
        grid=(nb, nch),
        in_specs=[
            pl.BlockSpec((C, 4 * W_A), lambda b, c: (blk0 + b * nch + c, 0)),
            pl.BlockSpec((C, LANE), lambda b, c: (blk0 + b * nch + c, SM0 // LANE)),
            pl.BlockSpec((None, SUB, 3 * W_A), lambda b, c: (bsel(b), 0, 0)),
            pl.BlockSpec((None, H_A, HEAD_A, HEAD_A), lambda b, c: (bsel(b), 0, 0, 0)),
            pl.BlockSpec((4, 3 * W_A), lambda b, c: (0, 0)),
            pl.BlockSpec((1, LANE), lambda b, c: (0, 0)),
            pl.BlockSpec((1, LANE), lambda b, c: (0, 0)),
            pl.BlockSpec((1, HEAD_A), lambda b, c: (0, 0)),
        ],
        out_specs=(pl.BlockSpec((C, W_A), lambda b, c: (b * nch + c, 0)),
                   pl.BlockSpec((None, H_A, HEAD_A, HEAD_A), lambda b, c: (b, 0, 0, 0))),
        scratch_shapes=[pltpu.VMEM((SUB, 3 * W_A), F32)],
        compiler_params=pltpu.CompilerParams(
            dimension_semantics=("parallel", "arbitrary"), vmem_limit_bytes=VMEM_LIMIT),
        name="delta_mixer",
    )(proj, proj, prev, s0, cw, alog_row, dtb_row, onorm)


def _rwkv_kernel(r_ref, k_ref, v_ref, sm_ref, pr_ref, pk_ref, pv_ref, psm_ref, s0_ref,
                 mur_ref, muk_ref, muv_ref, musm_ref, w2_ref, a2_ref, g2_ref,
                 w0_ref, a0_ref, kk_ref, ka_ref, rk_ref, lnw_ref, lnb_ref,
                 o_ref, s_ref, cr_ref, ck_ref, cv_ref, csm_ref, st_ref, *, C, G):
    ci = pl.program_id(2)
    nch = pl.num_programs(2)

    r2 = _iota((LANE, LANE), 0)
    c2 = _iota((LANE, LANE), 1)
    same_head = (r2 < HEAD_B) == (c2 < HEAD_B)
    ones_bd = same_head.astype(F32)
    spread = (_iota((HEAD_B, LANE), 0) == (_iota((HEAD_B, LANE), 1) & (HEAD_B - 1))).astype(F32)
    gather = ((_iota((LANE, HEAD_B), 0) & (HEAD_B - 1)) == _iota((LANE, HEAD_B), 1)).astype(F32)

    @pl.when(ci == 0)
    def _():
        cr_ref[...] = pr_ref[...]
        ck_ref[...] = pk_ref[...]
        cv_ref[...] = pv_ref[...]
        csm_ref[...] = psm_ref[...]
        for p in range(G):
            st_ref[p] = jnp.where(same_head, _xdot(s0_ref[p], spread, 3), 0.0)

    row = _iota((C, C), 0)
    col = _iota((C, C), 1)
    incl = row >= col
    strict = row > col
    tri = incl.astype(F32)
    head0 = _iota((C, LANE), 1) < HEAD_B
    head0_2 = _iota((2 * C, LANE), 1) < HEAD_B

    def lerp(x, prev8, mu):
        return x + (_shifted(x, prev8, 1) - x) * mu

    sm = sm_ref[...]
    xs = lerp(sm, csm_ref[...], musm_ref[...])
    slab_wa = xs[:, SM_WA:SM_WA + LANE]
    slab_g = xs[:, SM_G:SM_G + 2 * LANE]

    pairs = range(G)

    def col(x, p):
        return x[:, p * LANE:(p + 1) * LANE]

    def to_rows(x):
        return jnp.concatenate([col(x, p) for p in pairs], axis=0)

    def to_cols(y):
        return jnp.concatenate([y[p * C:(p + 1) * C] for p in pairs], axis=1)

    xr = lerp(r_ref[...], cr_ref[...], mur_ref[...])
    xk = lerp(k_ref[...], ck_ref[...], muk_ref[...])
    xv = lerp(v_ref[...], cv_ref[...], muv_ref[...])
    wlog = -_softplus(-(w0_ref[...] + _bdot(jnp.tanh(slab_wa), w2_ref[...]))) - 0.5
    ld = -jnp.exp(wlog)
    aa = _sigmoid(a0_ref[...] + _bdot(slab_wa, a2_ref[...]))
    gate = _bdot(_sigmoid(slab_g), g2_ref[...])
    kkr = xk * kk_ref[...]
    kkn = kkr * lax.rsqrt(to_cols(_xdot(to_rows(kkr * kkr), ones_bd, 2)) + 1e-6)
    k2 = xk * (1.0 + (aa - 1.0) * ka_ref[...])
    lp = _xdot_r(tri, ld, 3)
    e_neg = jnp.exp(-lp)
    lp_last = lp[C - 1:C, :]
    e_rem = jnp.exp(lp_last - lp)
    at = -kkn * jnp.exp(lp - ld)
    rt = xr * jnp.exp(lp)
    kb = kkn * aa
    bt = kb * e_neg
    kt = k2 * e_neg
    bhat = kb * e_rem
    khat = k2 * e_rem
    p_last = jnp.exp(lp_last)

    s = [st_ref[p] for p in pairs]
    ar = [jnp.concatenate([col(at, p), col(rt, p)], axis=0) for p in pairs]
    ars = [_bdot_nt(ar[p], s[p]) for p in pairs]
    heads = [(p, hh) for p in pairs for hh in range(2)]
    arh = [jnp.where(head0_2 if hh == 0 else jnp.logical_not(head0_2), ar[p], 0.0) for p, hh in heads]
    ab = [_bdot_nt(arh[i], col(bt, p)) for i, (p, hh) in enumerate(heads)]
    ak = [_bdot_nt(arh[i], col(kt, p)) for i, (p, hh) in enumerate(heads)]
    tinv = _tri_inverse_many([jnp.where(strict, m[0:C], 0.0) for m in ab])
    lakv = [_bdot(jnp.where(strict, ak[i][0:C], 0.0), col(xv, p)) for i, (p, hh) in enumerate(heads)]
    us = [_bdot(tinv[i], ars[p][0:C] + lakv[i]) for i, (p, hh) in enumerate(heads)]
    u = [jnp.where(head0, us[2 * p], us[2 * p + 1]) for p in pairs]
    oh = [_bdot(jnp.where(incl, ab[i][C:2 * C], 0.0), u[p])
          + _bdot(jnp.where(incl, ak[i][C:2 * C], 0.0), col(xv, p)) for i, (p, hh) in enumerate(heads)]
    o = [ars[p][C:2 * C] + jnp.where(head0, oh[2 * p], oh[2 * p + 1]) for p in pairs]
    for p in pairs:
        uv = jnp.concatenate([u[p], col(xv, p)], axis=0)
        bkhat = jnp.concatenate([col(bhat, p), col(khat, p)], axis=0)
        s_new = s[p] * col(p_last, p) + _bdot_tn(uv, bkhat)
        st_ref[p] = jnp.where(same_head, s_new, 0.0)

    o_rows = jnp.concatenate(o, axis=0)
    mean = _xdot(o_rows, ones_bd, 2) * (1.0 / HEAD_B)
    d = o_rows - mean
    var = _xdot(d * d, ones_bd, 2) * (1.0 / HEAD_B)
    on = to_cols(d * lax.rsqrt(var + GN_EPS)) * lnw_ref[...] + lnb_ref[...]
    bonus = to_cols(_xdot(to_rows(xr * k2 * rk_ref[...]), ones_bd, 2)) * xv
    o_ref[...] = ((on + bonus) * gate).astype(BF16)

    cr_ref[...] = r_ref[C - SUB:C, :]
    ck_ref[...] = k_ref[C - SUB:C, :]
    cv_ref[...] = v_ref[C - SUB:C, :]
    csm_ref[...] = sm_ref[C - SUB:C, :]

    @pl.when(ci == nch - 1)
    def _():
        for p in range(G):
            s_ref[p] = _xdot(st_ref[p], gather, 3)


def _rwkv_mixer(proj, row0, nb, seq, C, G, prev_rkv, prev_sm, s0, mu_rkv, mu_sm, w2p, a2p, g2p,
                w0, a0, k_k, k_a, r_k, lnw, lnb):
    nch = seq // C
    ng = N_PAIR // G
    gw = G * LANE
    blk0 = row0 // C
    bcast = s0.shape[0] == 1
    bsel = (lambda b: 0) if bcast else (lambda b: b)

    def proj_spec(col0):
        return pl.BlockSpec((C, gw), lambda b, g, c: (blk0 + b * nch + c, col0 // gw + g))

    def prev_spec(part):
        return pl.BlockSpec((None, SUB, gw), lambda b, g, c: (bsel(b), 0, part * (W_B // gw) + g))

    def vec_spec(part=0):
        return pl.BlockSpec((1, gw), lambda b, g, c: (0, part * (W_B // gw) + g))

    in_specs = [
        proj_spec(RKV0), proj_spec(RKV0 + W_B), proj_spec(RKV0 + 2 * W_B),
        pl.BlockSpec((C, SM_W), lambda b, g, c: (blk0 + b * nch + c, SM0 // SM_W)),
        prev_spec(0), prev_spec(1), prev_spec(2),
        pl.BlockSpec((None, SUB, SM_W), lambda b, g, c: (bsel(b), 0, 0)),
        pl.BlockSpec((None, G, LANE, HEAD_B), lambda b, g, c: (bsel(b), g, 0, 0)),
        vec_spec(0), vec_spec(1), vec_spec(2),
        pl.BlockSpec((1, SM_W), lambda b, g, c: (0, 0)),
        pl.BlockSpec((LANE, gw), lambda b, g, c: (0, g)),
        pl.BlockSpec((LANE, gw), lambda b, g, c: (0, g)),
        pl.BlockSpec((2 * LANE, gw), lambda b, g, c: (0, g)),
        vec_spec(), vec_spec(), vec_spec(), vec_spec(), vec_spec(), vec_spec(), vec_spec(),
    ]
    return pl.pallas_call(
        functools.partial(_rwkv_kernel, C=C, G=G),
        out_shape=(jax.ShapeDtypeStruct((nb * seq, W_B), BF16),
                   jax.ShapeDtypeStruct((nb, N_PAIR, LANE, HEAD_B), F32)),
        grid=(nb, ng, nch),
        in_specs=in_specs,
        out_specs=(pl.BlockSpec((C, gw), lambda b, g, c: (b * nch + c, g)),
                   pl.BlockSpec((None, G, LANE, HEAD_B), lambda b, g, c: (b, g, 0, 0))),
        scratch_shapes=[pltpu.VMEM((SUB, gw), F32), pltpu.VMEM((SUB, gw), F32), pltpu.VMEM((SUB, gw), F32),
                        pltpu.VMEM((SUB, SM_W), F32), pltpu.VMEM((G, LANE, LANE), F32)],
        compiler_params=pltpu.CompilerParams(
            dimension_semantics=("parallel", "parallel", "arbitrary"), vmem_limit_bytes=VMEM_LIMIT),
        name="rwkv_mixer",
    )(proj, proj, proj, proj, prev_rkv, prev_rkv, prev_rkv, prev_sm, s0,
      mu_rkv, mu_rkv, mu_rkv, mu_sm, w2p, a2p, g2p, w0, a0, k_k, k_a, r_k, lnw, lnb)


def _small_layout(cols_ba, cols_w, cols_a, cols_g, lead):
    def z(n):
        return jnp.zeros(lead + (n,), cols_w.dtype)
    return jnp.concatenate(
        [cols_ba, z(SM_WA - cols_ba.shape[-1]), cols_w, cols_a, cols_g,
         z(SM_W - SM_G - cols_g.shape[-1])], axis=-1)


def _pad_rows8(x):
    b, n, c = x.shape
    return jnp.concatenate([jnp.zeros((b, SUB - n, c), x.dtype), x], axis=1)


def kernel(x_prompt, x_sample, state_delta, state_conv_qkv, state_wkv, state_shift, state_ffn_conv, meta, norm1, w_in, conv_a, a_log, dt_bias, onorm_a, mu_b, w0, w2, a0, a2, g2, k_k, k_a, r_k, lnx_w, lnx_b, w_o, norm2, w_ffn_in, conv_f, w_ffn_out, norm_f):
    nbp, seq_p, _ = x_prompt.shape
    nbs, seq_s, _ = x_sample.shape
    n_s = nbs * seq_s
    assert w_in.shape[0] == 1, "single-layer trunk"
    l = 0

    wi = w_in[l]
    o_b = A_PROJ
    o_l = A_PROJ + 3 * W_B
    w_cat = jnp.concatenate([
        wi[:, :4 * W_A], wi[:, o_b:o_l],
        _small_layout(wi[:, 4 * W_A:A_PROJ], wi[:, o_l:o_l + W_LORA],
                      wi[:, o_l + W_LORA:o_l + W_LORA + A_LORA], wi[:, o_l + W_LORA + A_LORA:],
                      (D_MODEL,))], axis=1).astype(BF16)
    mu = mu_b[l]
    mu_rkv = mu[None, :3 * W_B]
    mu_sm = _small_layout(jnp.zeros((1, 2 * H_A), F32), mu[None, 3 * W_B:3 * W_B + W_LORA],
                          mu[None, 3 * W_B + W_LORA:3 * W_B + W_LORA + A_LORA],
                          mu[None, 3 * W_B + W_LORA + A_LORA:], (1,))
    w2p = jnp.concatenate([w2[l], jnp.zeros((LANE - W_LORA, W_B), F32)], axis=0)
    a2p = jnp.concatenate([jnp.zeros((W_LORA, W_B), F32), a2[l]], axis=0)
    g2p = jnp.concatenate([g2[l], jnp.zeros((2 * LANE - G_LORA, W_B), F32)], axis=0)
    alog_row = jnp.concatenate([jnp.zeros((H_A,), F32), a_log[l], jnp.zeros((LANE - 2 * H_A,), F32)])[None]
    dtb_row = jnp.concatenate([jnp.zeros((H_A,), F32), dt_bias[l], jnp.zeros((LANE - 2 * H_A,), F32)])[None]
    wo_bf = w_o[l].astype(BF16)
    wfi_bf = w_ffn_in[l].astype(BF16)
    wfo_bf = w_ffn_out[l].astype(BF16)
    row = lambda v: v.reshape(1, -1)

    def mix(proj, row0, nb, seq, C, prev_qkv, prev_rkv, prev_sm, s_delta, s_wkv):
        oa, sd = _delta_mixer(proj, row0, nb, seq, C, prev_qkv, s_delta, conv_a[l], alog_row, dtb_row,
                              row(onorm_a[l]))
        ob, sw = _rwkv_mixer(proj, row0, nb, seq, C, N_PAIR, prev_rkv, prev_sm, s_wkv, mu_rkv, mu_sm,
                             w2p, a2p, g2p, row(w0[l]), row(a0[l]), row(k_k[l]), row(k_a[l]),
                             row(r_k[l]), row(lnx_w[l]), row(lnx_b[l]))
        return oa, ob, sd, sw

    xs_rows = jnp.concatenate([x_sample.reshape(n_s, D_MODEL), meta], axis=0)
    xp_rows = x_prompt.reshape(nbp * seq_p, D_MODEL)
    n_small = n_s + N_META
    proj_s = _norm_matmul(xs_rows, row(norm1[l]), w_cat, n_small, 512)
    proj_p = _norm_matmul(xp_rows, row(norm1[l]), w_cat, 1024, 512)

    zeros = lambda *s: jnp.zeros(s, F32)
    oa_m, ob_m, sd_m, sw_m = mix(proj_s, n_s, 1, N_META, N_META, zeros(1, SUB, 3 * W_A),
                                 zeros(1, SUB, 3 * W_B), zeros(1, SUB, SM_W),
                                 zeros(1, H_A, HEAD_A, HEAD_A), zeros(1, N_PAIR, LANE, HEAD_B))
    tail = proj_s[n_small - SUB:n_small]
    oa_p, ob_p, sd_p, sw_p = mix(proj_p, 0, nbp, seq_p, 64, tail[None, :, :3 * W_A],
                                 tail[None, :, RKV0:SM0], tail[None, :, SM0:], sd_m, sw_m)
    sh = state_shift[l]
    sh_sm = _small_layout(jnp.zeros((nbs, 1, 2 * H_A), F32), sh[..., 3 * W_B:3 * W_B + W_LORA],
                          sh[..., 3 * W_B + W_LORA:3 * W_B + W_LORA + A_LORA],
                          sh[..., 3 * W_B + W_LORA + A_LORA:], (nbs, 1))
    oa_s, ob_s, sd_s, sw_s = mix(proj_s, 0, nbs, seq_s, seq_s, _pad_rows8(state_conv_qkv[l]),
                                 _pad_rows8(sh[..., :3 * W_B]), _pad_rows8(sh_sm), state_delta[l],
                                 state_wkv[l].reshape(nbs, N_PAIR, LANE, HEAD_B))

    oa_small = jnp.concatenate([oa_s, oa_m], axis=0)
    ob_small = jnp.concatenate([ob_s, ob_m], axis=0)
    x1_s = _out_proj(xs_rows, oa_small, ob_small, wo_bf, n_small, 512)
    x1_p = _out_proj(xp_rows, oa_p, ob_p, wo_bf, 1024, 512)
    gu_s = _norm_matmul(x1_s, row(norm2[l]), wfi_bf, n_small, 512)
    gu_p = _norm_matmul(x1_p, row(norm2[l]), wfi_bf, 1024, 512)
    act_s = _ffn_act(gu_s, _pad_rows8(state_ffn_conv[l]), conv_f[l], nbs, seq_s, seq_s, D_FF)
    act_p = _ffn_act(gu_p, gu_s[None, n_small - SUB:n_small, :D_FF], conv_f[l], nbp, seq_p, 512, 768)
    y_s = _ffn_out(act_s, wfo_bf, x1_s, row(norm_f), n_s, 512, 768)
    y_p = _ffn_out(act_p, wfo_bf, x1_p, row(norm_f), nbp * seq_p, 512, 768)

    def states(proj, gu, nb, seq, sd, sw):
        p3 = proj[:nb * seq].reshape(nb, seq, P_CAT)
        conv_new = p3[:, seq - 3:, :3 * W_A]
        last = p3[:, seq - 1:, :]
        shift_new = jnp.concatenate([last[..., RKV0:SM0], last[..., SM0 + SM_WA:SM0 + SM_WA + W_LORA + A_LORA],
                                     last[..., SM0 + SM_G:SM0 + SM_G + G_LORA]], axis=-1)
        ffn_new = gu[:nb * seq].reshape(nb, seq, 2 * D_FF)[:, seq - 2:, :D_FF]
        return (sd[None], conv_new[None], sw.reshape(nb, H_B, HEAD_B, HEAD_B)[None], shift_new[None],
                ffn_new[None])

    return ((y_p.reshape(nbp, seq_p, D_MODEL), y_s.reshape(nbs, seq_s, D_MODEL))
            + states(proj_p, gu_p, nbp, seq_p, sd_p, sw_p)
            + states(proj_s, gu_s, nbs, seq_s, sd_s, sw_s))
```

```python
import functools

import jax
import jax.numpy as jnp
from jax import lax
from jax.experimental import pallas as pl
from jax.experimental.pallas import tpu as pltpu

F32 = jnp.float32
BF16 = jnp.bfloat16

D_MODEL = 2048
N_META = 16
W_A = 1024
HEAD_A = 128
H_A = 8
W_B = 1024
HEAD_B = 64
H_B = 16
N_PAIR = H_B // 2
W_LORA = 64
A_LORA = 64
G_LORA = 160
D_FF = 5376
RMS_EPS = 1e-6
GN_EPS = 64e-5
A_PROJ = 4 * W_A + 2 * H_A
B_PROJ = 3 * W_B + W_LORA + A_LORA + G_LORA

QKVZ0 = 0
RKV0 = 4 * W_A
SM0 = RKV0 + 3 * W_B
SM_W = 512
SM_BA = 0
SM_WA = 128
SM_G = 256
P_CAT = SM0 + SM_W

LANE = 128
SUB = 8
VMEM_LIMIT = 48 * 1024 * 1024
NEG_BIG = -1e30

NT_DIMS = (((1,), (1,)), ((), ()))


def _bdot(a, b):
    return jnp.dot(a.astype(BF16), b.astype(BF16), preferred_element_type=F32)


def _bdot_nt(a, b):
    return lax.dot_general(a.astype(BF16), b.astype(BF16), (((1,), (1,)), ((), ())),
                           preferred_element_type=F32)


def _bdot_tn(a, b):
    return lax.dot_general(a.astype(BF16), b.astype(BF16), (((0,), (0,)), ((), ())),
                           preferred_element_type=F32)


def _pieces(a, n):
    out = []
    rem = a
    for i in range(n):
        p = rem.astype(BF16)
        out.append(p)
        if i + 1 < n:
            rem = rem - p.astype(F32)
    return out


def _xdot(a, b, n, dims=(((1,), (0,)), ((), ()))):
    bb = b.astype(BF16)
    acc = None
    for p in _pieces(a, n):
        t = lax.dot_general(p, bb, dims, preferred_element_type=F32)
        acc = t if acc is None else acc + t
    return acc


def _xdot_r(a, b, n, dims=(((1,), (0,)), ((), ()))):
    ab = a.astype(BF16)
    acc = None
    for p in _pieces(b, n):
        t = lax.dot_general(ab, p, dims, preferred_element_type=F32)
        acc = t if acc is None else acc + t
    return acc


def _sigmoid(x):
    return 1.0 / (1.0 + jnp.exp(-x))


def _silu(x):
    return x * _sigmoid(x)


def _softplus(x):
    return jnp.maximum(x, 0.0) + jnp.log(1.0 + jnp.exp(-jnp.abs(x)))


def _iota(shape, dim):
    return lax.broadcasted_iota(jnp.int32, shape, dim)


def _shifted(x, prev8, k):
    n = x.shape[0]
    xr = pltpu.roll(x, k, 0)
    pr = pltpu.roll(prev8, k, 0)
    first = jnp.where(_iota((SUB, x.shape[1]), 0) < k, pr, xr[0:SUB])
    if n == SUB:
        return first
    return jnp.concatenate([first, xr[SUB:]], axis=0)


class _Packed:
    def __init__(self, c):
        self.c = c
        row = _iota((c, 2 * c), 0)
        col = _iota((c, 2 * c), 1) & (c - 1)
        self.incl = row >= col
        self.strict = row > col
        self.eye = (row == col).astype(F32)
        self.first = _iota((c, 2 * c), 1) < c
        self.bd_mask = (_iota((2 * c, 2 * c), 0) < c) == (_iota((2 * c, 2 * c), 1) < c)

    def block_diag(self, m):
        return jnp.where(self.bd_mask, jnp.concatenate([m, m], axis=0), 0.0)

    def inverse_many(self, xs):
        ts = [self.eye + x for x in xs]
        ps = list(xs)
        n = 1
        while 2 * n < self.c:
            ps = [_bdot(p, self.block_diag(p)) for p in ps]
            ts = [t + _bdot(p, self.block_diag(t)) for p, t in zip(ps, ts)]
            n *= 2
        return ts


def _norm_matmul_kernel(x_ref, g_ref, w_ref, o_ref, h_ref):
    @pl.when(pl.program_id(1) == 0)
    def _():
        x = x_ref[...]
        ms = jnp.mean(x * x, axis=-1, keepdims=True)
        h_ref[...] = (x * lax.rsqrt(ms + RMS_EPS) * g_ref[...]).astype(BF16)

    o_ref[...] = jnp.dot(h_ref[...], w_ref[...], preferred_element_type=F32)


def _norm_matmul(x, g, w, tm, tn):
    m, k = x.shape
    n = w.shape[1]
    return pl.pallas_call(
        _norm_matmul_kernel,
        out_shape=jax.ShapeDtypeStruct((m, n), F32),
        grid=(m // tm, n // tn),
        in_specs=[
            pl.BlockSpec((tm, k), lambda i, j: (i, 0)),
            pl.BlockSpec((1, k), lambda i, j: (0, 0)),
            pl.BlockSpec((k, tn), lambda i, j: (0, j)),
        ],
        out_specs=pl.BlockSpec((tm, tn), lambda i, j: (i, j)),
        scratch_shapes=[pltpu.VMEM((tm, k), BF16)],
        compiler_params=pltpu.CompilerParams(
            dimension_semantics=("parallel", "arbitrary"), vmem_limit_bytes=VMEM_LIMIT),
        name="norm_matmul",
    )(x, g, w)


def _out_proj_kernel(x_ref, oa_ref, ob_ref, wt_ref, wb_ref, o_ref):
    acc = jnp.dot(oa_ref[...], wt_ref[...], preferred_element_type=F32)
    acc = acc + jnp.dot(ob_ref[...], wb_ref[...], preferred_element_type=F32)
    o_ref[...] = x_ref[...] + acc


def _out_proj(x, oa, ob, wo, tm, tn):
    m, d = x.shape
    return pl.pallas_call(
        _out_proj_kernel,
        out_shape=jax.ShapeDtypeStruct((m, d), F32),
        grid=(m // tm, d // tn),
        in_specs=[
            pl.BlockSpec((tm, tn), lambda i, j: (i, j)),
            pl.BlockSpec((tm, W_A), lambda i, j: (i, 0)),
            pl.BlockSpec((tm, W_B), lambda i, j: (i, 0)),
            pl.BlockSpec((W_A, tn), lambda i, j: (0, j)),
            pl.BlockSpec((W_B, tn), lambda i, j: (W_A // W_B, j)),
        ],
        out_specs=pl.BlockSpec((tm, tn), lambda i, j: (i, j)),
        compiler_params=pltpu.CompilerParams(
            dimension_semantics=("parallel", "arbitrary"), vmem_limit_bytes=VMEM_LIMIT),
        name="out_proj",
    )(x, oa, ob, wo, wo)


def _ffn_act_kernel(g_ref, u_ref, prev_ref, cw_ref, o_ref, carry_ref):
    @pl.when(pl.program_id(2) == 0)
    def _():
        carry_ref[...] = prev_ref[...]

    x = g_ref[...]
    n = x.shape[0]
    prev8 = carry_ref[...]
    y = x * cw_ref[2:3, :]
    y = y + _shifted(x, prev8, 1) * cw_ref[1:2, :]
    y = y + _shifted(x, prev8, 2) * cw_ref[0:1, :]
    carry_ref[...] = x[n - SUB:n]
    o_ref[...] = (_silu(y) * u_ref[...]).astype(BF16)


def _ffn_act(gu, prev, cw, nb, seq, tr, tc):
    ncol = D_FF // tc
    nrow = seq // tr
    bcast = prev.shape[0] == 1
    return pl.pallas_call(
        _ffn_act_kernel,
        out_shape=jax.ShapeDtypeStruct((nb * seq, D_FF), BF16),
        grid=(nb, ncol, nrow),
        in_specs=[
            pl.BlockSpec((tr, tc), lambda b, j, r: (b * nrow + r, j)),
            pl.BlockSpec((tr, tc), lambda b, j, r: (b * nrow + r, ncol + j)),
            pl.BlockSpec((None, SUB, tc), (lambda b, j, r: (0, 0, j)) if bcast else (lambda b, j, r: (b, 0, j))),
            pl.BlockSpec((3, tc), lambda b, j, r: (0, j)),
        ],
        out_specs=pl.BlockSpec((tr, tc), lambda b, j, r: (b * nrow + r, j)),
        scratch_shapes=[pltpu.VMEM((SUB, tc), F32)],
        compiler_params=pltpu.CompilerParams(
            dimension_semantics=("parallel", "parallel", "arbitrary"), vmem_limit_bytes=VMEM_LIMIT),
        name="ffn_act",
    )(gu, gu, prev, cw)


def _ffn_out_kernel(a_ref, w_ref, x_ref, g_ref, o_ref, acc_ref):
    kk = pl.program_id(1)

    @pl.when(kk == 0)
    def _():
        acc_ref[...] = x_ref[...]

    acc_ref[...] += jnp.dot(a_ref[...], w_ref[...], preferred_element_type=F32)

    @pl.when(kk == pl.num_programs(1) - 1)
    def _():
        x = acc_ref[...]
        ms = jnp.mean(x * x, axis=-1, keepdims=True)
        o_ref[...] = x * lax.rsqrt(ms + RMS_EPS) * g_ref[...]


def _ffn_out(act, w, x1, g, m, tm, tk):
    kdim, d = w.shape
    return pl.pallas_call(
        _ffn_out_kernel,
        out_shape=jax.ShapeDtypeStruct((m, d), F32),
        grid=(m // tm, kdim // tk),
        in_specs=[
            pl.BlockSpec((tm, tk), lambda i, k: (i, k)),
            pl.BlockSpec((tk, d), lambda i, k: (k, 0)),
            pl.BlockSpec((tm, d), lambda i, k: (i, 0)),
            pl.BlockSpec((1, d), lambda i, k: (0, 0)),
        ],
        out_specs=pl.BlockSpec((tm, d), lambda i, k: (i, 0)),
        scratch_shapes=[pltpu.VMEM((tm, d), F32)],
        compiler_params=pltpu.CompilerParams(
            dimension_semantics=("parallel", "arbitrary"), vmem_limit_bytes=VMEM_LIMIT),
        name="ffn_out",
    )(act, w, x1, g)


def _delta_kernel(qkvz_ref, ba_ref, prev_ref, s0_ref, cw_ref, alog_ref, dtb_ref, on_ref,
                  o_ref, s_ref, carry_ref, *, C):
    ci = pl.program_id(1)

    @pl.when(ci == 0)
    def _():
        carry_ref[...] = prev_ref[...]
        s_ref[...] = s0_ref[...]

    tri = (_iota((C, C), 0) >= _iota((C, C), 1)).astype(F32)

    ba = ba_ref[...]
    beta_full = _sigmoid(ba)
    g_full = -jnp.exp(alog_ref[...]) * _softplus(ba + dtb_ref[...])
    gc_full = _xdot_r(tri, g_full, 3)
    lane = _iota((C, LANE), 1)

    def conv_silu(c0):
        x = qkvz_ref[:, c0:c0 + LANE]
        prev8 = carry_ref[:, c0:c0 + LANE]
        y = x * cw_ref[3:4, c0:c0 + LANE]
        for k in (1, 2, 3):
            y = y + _shifted(x, prev8, k) * cw_ref[3 - k:4 - k, c0:c0 + LANE]
        return _silu(y)

    def l2n(x):
        return x * lax.rsqrt(jnp.sum(x * x, axis=-1, keepdims=True) + 1e-6)

    heads = range(H_A)
    q = [l2n(conv_silu(h * HEAD_A)) * (HEAD_A ** -0.5) for h in heads]
    k = [l2n(conv_silu(W_A + h * HEAD_A)) for h in heads]
    v = [conv_silu(2 * W_A + h * HEAD_A) for h in heads]
    bcol = [beta_full[:, h:h + 1] for h in heads]
    gcol = [gc_full[:, H_A + h:H_A + h + 1] for h in heads]
    glast = [gc_full[C - 1:C, H_A + h:H_A + h + 1] for h in heads]
    eg = [jnp.exp(gcol[h]) for h in heads]
    kb = [k[h] * bcol[h] for h in heads]

    pk = _Packed(C)
    hpairs = range(H_A // 2)
    zc = jnp.zeros((C, HEAD_A), F32)
    row2 = _iota((2 * C, LANE), 0)
    lane2 = _iota((2 * C, LANE), 1)
    ones_cl = jnp.ones((C, LANE), F32)
    gc2 = jnp.concatenate([gc_full, gc_full], axis=0)
    kq, gamma = [], []
    for hp in hpairs:
        h0, h1 = 2 * hp, 2 * hp + 1
        lhs = jnp.concatenate([jnp.concatenate([kb[h0], kb[h1]], axis=1),
                               jnp.concatenate([q[h0], q[h1]], axis=1)], axis=0)
        rk = jnp.concatenate([jnp.concatenate([k[h0], zc], axis=1),
                              jnp.concatenate([zc, k[h1]], axis=1)], axis=0)
        kq.append(_bdot_nt(lhs, rk))
        sel = lane2 == jnp.where(row2 < C, H_A + h0, H_A + h1)
        grow = _xdot_r(ones_cl, jnp.where(sel, gc2, 0.0), 3, NT_DIMS)
        gcol_p = jnp.where(pk.first, gcol[h0], gcol[h1])
        gamma.append(jnp.exp(jnp.where(pk.incl, gcol_p - grow, NEG_BIG)))
    tinv = pk.inverse_many([-jnp.where(pk.strict, kq[hp][0:C] * gamma[hp], 0.0) for hp in hpairs])
    uw = []
    for hp in hpairs:
        h0, h1 = 2 * hp, 2 * hp + 1
        z2 = jnp.zeros((C, 2 * HEAD_A), F32)
        rhs = jnp.concatenate([jnp.concatenate([v[h0] * bcol[h0], kb[h0] * eg[h0], z2], axis=1),
                               jnp.concatenate([z2, v[h1] * bcol[h1], kb[h1] * eg[h1]], axis=1)], axis=0)
        uw.append(_bdot(tinv[hp], rhs))
    u = [uw[h // 2][:, (h % 2) * 2 * HEAD_A:(h % 2) * 2 * HEAD_A + HEAD_A] for h in heads]
    w = [uw[h // 2][:, (h % 2) * 2 * HEAD_A + HEAD_A:(h % 2 + 1) * 2 * HEAD_A] for h in heads]
    s = [s_ref[h] for h in heads]
    wqs = [_bdot(jnp.concatenate([w[h], q[h] * eg[h]], axis=0), s[h]) for h in heads]
    v_new = [u[h] - wqs[h][0:C] for h in heads]
    o = []
    for hp in hpairs:
        h0, h1 = 2 * hp, 2 * hp + 1
        vn_bd = jnp.concatenate([jnp.concatenate([v_new[h0], zc], axis=1),
                                 jnp.concatenate([zc, v_new[h1]], axis=1)], axis=0)
        op = _bdot(kq[hp][C:2 * C] * gamma[hp], vn_bd)
        o.append(wqs[h0][C:2 * C] + op[:, :HEAD_A])
        o.append(wqs[h1][C:2 * C] + op[:, HEAD_A:])
    for h in heads:
        kd = k[h] * jnp.exp(glast[h] - gcol[h])
        s_ref[h] = s[h] * jnp.exp(glast[h]) + _bdot_tn(kd, v_new[h])
    for h in heads:
        z = qkvz_ref[:, 3 * W_A + h * HEAD_A:3 * W_A + (h + 1) * HEAD_A]
        oh = o[h] * lax.rsqrt(jnp.mean(o[h] * o[h], axis=-1, keepdims=True) + RMS_EPS)
        o_ref[:, h * HEAD_A:(h + 1) * HEAD_A] = (oh * on_ref[...] * _silu(z)).astype(BF16)

    carry_ref[...] = qkvz_ref[C - SUB:C, 0:3 * W_A]


def _delta_mixer(proj, row0, nb, seq, C, prev, s0, cw, alog_row, dtb_row, onorm):
    nch = seq // C
    blk0 = row0 // C
    bcast = s0.shape[0] == 1
    bsel = (lambda b: 0) if bcast else (lambda b: b)
    return pl.pallas_call(
        functools.partial(_delta_kernel, C=C),
        out_shape=(jax.ShapeDtypeStruct((nb * seq, W_A), BF16),
                   jax.ShapeDtypeStruct((nb, H_A, HEAD_A, HEAD_A), F32)),
        grid=(nb, nch),
        in_specs=[
            pl.BlockSpec((C, 4 * W_A), lambda b, c: (blk0 + b * nch + c, 0)),
            pl.BlockSpec((C, LANE), lambda b, c: (blk0 + b * nch + c, SM0 // LANE)),
            pl.BlockSpec((None, SUB, 3 * W_A), lambda b, c: (bsel(b), 0, 0)),
            pl.BlockSpec((None, H_A, HEAD_A, HEAD_A), lambda b, c: (bsel(b), 0, 0, 0)),
            pl.BlockSpec((4, 3 * W_A), lambda b, c: (0, 0)),
            pl.BlockSpec((1, LANE), lambda b, c: (0, 0)),
            pl.BlockSpec((1, LANE), lambda b, c: (0, 0)),
            pl.BlockSpec((1, HEAD_A), lambda b, c: (0, 0)),
        ],
        out_specs=(pl.BlockSpec((C, W_A), lambda b, c: (b * nch + c, 0)),
                   pl.BlockSpec((None, H_A, HEAD_A, HEAD_A), lambda b, c: (b, 0, 0, 0))),
        scratch_shapes=[pltpu.VMEM((SUB, 3 * W_A), F32)],
        compiler_params=pltpu.CompilerParams(
            dimension_semantics=("parallel", "arbitrary"), vmem_limit_bytes=VMEM_LIMIT),
        name="delta_mixer",
    )(proj, proj, prev, s0, cw, alog_row, dtb_row, onorm)


def _rwkv_kernel(r_ref, k_ref, v_ref, sm_ref, pr_ref, pk_ref, pv_ref, psm_ref, s0_ref,
                 mur_ref, muk_ref, muv_ref, musm_ref, w2_ref, a2_ref, g2_ref,
                 w0_ref, a0_ref, kk_ref, ka_ref, rk_ref, lnw_ref, lnb_ref,
                 o_ref, s_ref, cr_ref, ck_ref, cv_ref, csm_ref, st_ref, *, C, G):
    ci = pl.program_id(2)
    nch = pl.num_programs(2)

    r2 = _iota((LANE, LANE), 0)
    c2 = _iota((LANE, LANE), 1)
    same_head = (r2 < HEAD_B) == (c2 < HEAD_B)
    ones_bd = same_head.astype(F32)
    spread = (_iota((HEAD_B, LANE), 0) == (_iota((HEAD_B, LANE), 1) & (HEAD_B - 1))).astype(F32)
    gather = ((_iota((LANE, HEAD_B), 0) & (HEAD_B - 1)) == _iota((LANE, HEAD_B), 1)).astype(F32)

    @pl.when(ci == 0)
    def _():
        cr_ref[...] = pr_ref[...]
        ck_ref[...] = pk_ref[...]
        cv_ref[...] = pv_ref[...]
        csm_ref[...] = psm_ref[...]
        for p in range(G):
            st_ref[p] = jnp.where(same_head, _xdot(s0_ref[p], spread, 3), 0.0)

    tri = (_iota((C, C), 0) >= _iota((C, C), 1)).astype(F32)

    def lerp(x, prev8, mu):
        return x + (_shifted(x, prev8, 1) - x) * mu

    sm = sm_ref[...]
    xs = lerp(sm, csm_ref[...], musm_ref[...])
    slab_wa = xs[:, SM_WA:SM_WA + LANE]
    slab_g = xs[:, SM_G:SM_G + 2 * LANE]

    pairs = range(G)

    def col(x, p):
        return x[:, p * LANE:(p + 1) * LANE]

    def to_rows(x):
        return jnp.concatenate([col(x, p) for p in pairs], axis=0)

    def to_cols(y):
        return jnp.concatenate([y[p * C:(p + 1) * C] for p in pairs], axis=1)

    xr = lerp(r_ref[...], cr_ref[...], mur_ref[...])
    xk = lerp(k_ref[...], ck_ref[...], muk_ref[...])
    xv = lerp(v_ref[...], cv_ref[...], muv_ref[...])
    wlog = -_softplus(-(w0_ref[...] + _bdot(jnp.tanh(slab_wa), w2_ref[...]))) - 0.5
    ld = -jnp.exp(wlog)
    aa = _sigmoid(a0_ref[...] + _bdot(slab_wa, a2_ref[...]))
    gate = _bdot(_sigmoid(slab_g), g2_ref[...])
    kkr = xk * kk_ref[...]
    kkn = kkr * lax.rsqrt(to_cols(_bdot(to_rows(kkr * kkr), ones_bd)) + 1e-6)
    k2 = xk * (1.0 + (aa - 1.0) * ka_ref[...])
    lp = _xdot_r(tri, ld, 3)
    e_neg = jnp.exp(-lp)
    lp_last = lp[C - 1:C, :]
    e_rem = jnp.exp(lp_last - lp)
    at = -kkn * jnp.exp(lp - ld)
    rt = xr * jnp.exp(lp)
    kb = kkn * aa
    bt = kb * e_neg
    kt = k2 * e_neg
    bhat = kb * e_rem
    khat = k2 * e_rem
    p_last = jnp.exp(lp_last)

    s = [st_ref[p] for p in pairs]
    ar = [jnp.concatenate([col(at, p), col(rt, p)], axis=0) for p in pairs]
    ars = [_bdot_nt(ar[p], s[p]) for p in pairs]
    pk = _Packed(C)
    stack_mask = (_iota((2 * C, LANE), 0) < C) == (_iota((2 * C, LANE), 1) < HEAD_B)

    def by_head(x):
        return jnp.where(stack_mask, jnp.concatenate([x, x], axis=0), 0.0)

    ab = [_bdot_nt(ar[p], by_head(col(bt, p))) for p in pairs]
    ak = [_bdot_nt(ar[p], by_head(col(kt, p))) for p in pairs]
    tinv = pk.inverse_many([jnp.where(pk.strict, m[0:C], 0.0) for m in ab])
    v_bh = [by_head(col(xv, p)) for p in pairs]
    y = [ars[p][0:C] + _bdot(jnp.where(pk.strict, ak[p][0:C], 0.0), v_bh[p]) for p in pairs]
    u = [_bdot(tinv[p], by_head(y[p])) for p in pairs]
    o = [ars[p][C:2 * C] + _bdot(jnp.where(pk.incl, ab[p][C:2 * C], 0.0), by_head(u[p]))
         + _bdot(jnp.where(pk.incl, ak[p][C:2 * C], 0.0), v_bh[p]) for p in pairs]
    for p in pairs:
        uv = jnp.concatenate([u[p], col(xv, p)], axis=0)
        bkhat = jnp.concatenate([col(bhat, p), col(khat, p)], axis=0)
        s_new = s[p] * col(p_last, p) + _bdot_tn(uv, bkhat)
        st_ref[p] = jnp.where(same_head, s_new, 0.0)

    o_rows = jnp.concatenate(o, axis=0)
    mean = _bdot(o_rows, ones_bd) * (1.0 / HEAD_B)
    d = o_rows - mean
    var = _bdot(d * d, ones_bd) * (1.0 / HEAD_B)
    on = to_cols(d * lax.rsqrt(var + GN_EPS)) * lnw_ref[...] + lnb_ref[...]
    bonus = to_cols(_bdot(to_rows(xr * k2 * rk_ref[...]), ones_bd)) * xv
    o_ref[...] = ((on + bonus) * gate).astype(BF16)

    cr_ref[...] = r_ref[C - SUB:C, :]
    ck_ref[...] = k_ref[C - SUB:C, :]
    cv_ref[...] = v_ref[C - SUB:C, :]
    csm_ref[...] = sm_ref[C - SUB:C, :]

    @pl.when(ci == nch - 1)
    def _():
        for p in range(G):
            s_ref[p] = _xdot(st_ref[p], gather, 3)


def _rwkv_mixer(proj, row0, nb, seq, C, G, prev_rkv, prev_sm, s0, mu_rkv, mu_sm, w2p, a2p, g2p,
                w0, a0, k_k, k_a, r_k, lnw, lnb):
    nch = seq // C
    ng = N_PAIR // G
    gw = G * LANE
    blk0 = row0 // C
    bcast = s0.shape[0] == 1
    bsel = (lambda b: 0) if bcast else (lambda b: b)

    def proj_spec(col0):
        return pl.BlockSpec((C, gw), lambda b, g, c: (blk0 + b * nch + c, col0 // gw + g))

    def prev_spec(part):
        return pl.BlockSpec((None, SUB, gw), lambda b, g, c: (bsel(b), 0, part * (W_B // gw) + g))

    def vec_spec(part=0):
        return pl.BlockSpec((1, gw), lambda b, g, c: (0, part * (W_B // gw) + g))

    in_specs = [
        proj_spec(RKV0), proj_spec(RKV0 + W_B), proj_spec(RKV0 + 2 * W_B),
        pl.BlockSpec((C, SM_W), lambda b, g, c: (blk0 + b * nch + c, SM0 // SM_W)),
        prev_spec(0), prev_spec(1), prev_spec(2),
        pl.BlockSpec((None, SUB, SM_W), lambda b, g, c: (bsel(b), 0, 0)),
        pl.BlockSpec((None, G, LANE, HEAD_B), lambda b, g, c: (bsel(b), g, 0, 0)),
        vec_spec(0), vec_spec(1), vec_spec(2),
        pl.BlockSpec((1, SM_W), lambda b, g, c: (0, 0)),
        pl.BlockSpec((LANE, gw), lambda b, g, c: (0, g)),
        pl.BlockSpec((LANE, gw), lambda b, g, c: (0, g)),
        pl.BlockSpec((2 * LANE, gw), lambda b, g, c: (0, g)),
        vec_spec(), vec_spec(), vec_spec(), vec_spec(), vec_spec(), vec_spec(), vec_spec(),
    ]
    return pl.pallas_call(
        functools.partial(_rwkv_kernel, C=C, G=G),
        out_shape=(jax.ShapeDtypeStruct((nb * seq, W_B), BF16),
                   jax.ShapeDtypeStruct((nb, N_PAIR, LANE, HEAD_B), F32)),
        grid=(nb, ng, nch),
        in_specs=in_specs,
        out_specs=(pl.BlockSpec((C, gw), lambda b, g, c: (b * nch + c, g)),
                   pl.BlockSpec((None, G, LANE, HEAD_B), lambda b, g, c: (b, g, 0, 0))),
        scratch_shapes=[pltpu.VMEM((SUB, gw), F32), pltpu.VMEM((SUB, gw), F32), pltpu.VMEM((SUB, gw), F32),
                        pltpu.VMEM((SUB, SM_W), F32), pltpu.VMEM((G, LANE, LANE), F32)],
        compiler_params=pltpu.CompilerParams(
            dimension_semantics=("parallel", "parallel", "arbitrary"), vmem_limit_bytes=VMEM_LIMIT),
        name="rwkv_mixer",
    )(proj, proj, proj, proj, prev_rkv, prev_rkv, prev_rkv, prev_sm, s0,
      mu_rkv, mu_rkv, mu_rkv, mu_sm, w2p, a2p, g2p, w0, a0, k_k, k_a, r_k, lnw, lnb)


def _small_layout(cols_ba, cols_w, cols_a, cols_g, lead):
    def z(n):
        return jnp.zeros(lead + (n,), cols_w.dtype)
    return jnp.concatenate(
        [cols_ba, z(SM_WA - cols_ba.shape[-1]), cols_w, cols_a, cols_g,
         z(SM_W - SM_G - cols_g.shape[-1])], axis=-1)


def _pad_rows8(x):
    b, n, c = x.shape
    return jnp.concatenate([jnp.zeros((b, SUB - n, c), x.dtype), x], axis=1)


def kernel(x_prompt, x_sample, state_delta, state_conv_qkv, state_wkv, state_shift, state_ffn_conv, meta, norm1, w_in, conv_a, a_log, dt_bias, onorm_a, mu_b, w0, w2, a0, a2, g2, k_k, k_a, r_k, lnx_w, lnx_b, w_o, norm2, w_ffn_in, conv_f, w_ffn_out, norm_f):
    nbp, seq_p, _ = x_prompt.shape
    nbs, seq_s, _ = x_sample.shape
    n_s = nbs * seq_s
    assert w_in.shape[0] == 1, "single-layer trunk"
    l = 0

    wi = w_in[l]
    o_b = A_PROJ
    o_l = A_PROJ + 3 * W_B
    w_cat = jnp.concatenate([
        wi[:, :4 * W_A], wi[:, o_b:o_l],
        _small_layout(wi[:, 4 * W_A:A_PROJ], wi[:, o_l:o_l + W_LORA],
                      wi[:, o_l + W_LORA:o_l + W_LORA + A_LORA], wi[:, o_l + W_LORA + A_LORA:],
                      (D_MODEL,))], axis=1).astype(BF16)
    mu = mu_b[l]
    mu_rkv = mu[None, :3 * W_B]
    mu_sm = _small_layout(jnp.zeros((1, 2 * H_A), F32), mu[None, 3 * W_B:3 * W_B + W_LORA],
                          mu[None, 3 * W_B + W_LORA:3 * W_B + W_LORA + A_LORA],
                          mu[None, 3 * W_B + W_LORA + A_LORA:], (1,))
    w2p = jnp.concatenate([w2[l], jnp.zeros((LANE - W_LORA, W_B), F32)], axis=0)
    a2p = jnp.concatenate([jnp.zeros((W_LORA, W_B), F32), a2[l]], axis=0)
    g2p = jnp.concatenate([g2[l], jnp.zeros((2 * LANE - G_LORA, W_B), F32)], axis=0)
    alog_row = jnp.concatenate([jnp.zeros((H_A,), F32), a_log[l], jnp.zeros((LANE - 2 * H_A,), F32)])[None]
    dtb_row = jnp.concatenate([jnp.zeros((H_A,), F32), dt_bias[l], jnp.zeros((LANE - 2 * H_A,), F32)])[None]
    wo_bf = w_o[l].astype(BF16)
    wfi_bf = w_ffn_in[l].astype(BF16)
    wfo_bf = w_ffn_out[l].astype(BF16)
    row = lambda v: v.reshape(1, -1)

    def mix(proj, row0, nb, seq, C, prev_qkv, prev_rkv, prev_sm, s_delta, s_wkv):
        oa, sd = _delta_mixer(proj, row0, nb, seq, C, prev_qkv, s_delta, conv_a[l], alog_row, dtb_row,
                              row(onorm_a[l]))
        ob, sw = _rwkv_mixer(proj, row0, nb, seq, C, N_PAIR, prev_rkv, prev_sm, s_wkv, mu_rkv, mu_sm,
                             w2p, a2p, g2p, row(w0[l]), row(a0[l]), row(k_k[l]), row(k_a[l]),
                             row(r_k[l]), row(lnx_w[l]), row(lnx_b[l]))
        return oa, ob, sd, sw

    xs_rows = jnp.concatenate([x_sample.reshape(n_s, D_MODEL), meta], axis=0)
    xp_rows = x_prompt.reshape(nbp * seq_p, D_MODEL)
    n_small = n_s + N_META
    proj_s = _norm_matmul(xs_rows, row(norm1[l]), w_cat, n_small, 512)
    proj_p = _norm_matmul(xp_rows, row(norm1[l]), w_cat, 1024, 512)

    zeros = lambda *s: jnp.zeros(s, F32)
    oa_m, ob_m, sd_m, sw_m = mix(proj_s, n_s, 1, N_META, N_META, zeros(1, SUB, 3 * W_A),
                                 zeros(1, SUB, 3 * W_B), zeros(1, SUB, SM_W),
                                 zeros(1, H_A, HEAD_A, HEAD_A), zeros(1, N_PAIR, LANE, HEAD_B))
    tail = proj_s[n_small - SUB:n_small]
    oa_p, ob_p, sd_p, sw_p = mix(proj_p, 0, nbp, seq_p, 64, tail[None, :, :3 * W_A],
                                 tail[None, :, RKV0:SM0], tail[None, :, SM0:], sd_m, sw_m)
    sh = state_shift[l]
    sh_sm = _small_layout(jnp.zeros((nbs, 1, 2 * H_A), F32), sh[..., 3 * W_B:3 * W_B + W_LORA],
                          sh[..., 3 * W_B + W_LORA:3 * W_B + W_LORA + A_LORA],
                          sh[..., 3 * W_B + W_LORA + A_LORA:], (nbs, 1))
    oa_s, ob_s, sd_s, sw_s = mix(proj_s, 0, nbs, seq_s, seq_s, _pad_rows8(state_conv_qkv[l]),
                                 _pad_rows8(sh[..., :3 * W_B]), _pad_rows8(sh_sm), state_delta[l],
                                 state_wkv[l].reshape(nbs, N_PAIR, LANE, HEAD_B))

    oa_small = jnp.concatenate([oa_s, oa_m], axis=0)
    ob_small = jnp.concatenate([ob_s, ob_m], axis=0)
    x1_s = _out_proj(xs_rows, oa_small, ob_small, wo_bf, n_small, 512)
    x1_p = _out_proj(xp_rows, oa_p, ob_p, wo_bf, 1024, 512)
    gu_s = _norm_matmul(x1_s, row(norm2[l]), wfi_bf, n_small, 512)
    gu_p = _norm_matmul(x1_p, row(norm2[l]), wfi_bf, 1024, 512)
    act_s = _ffn_act(gu_s, _pad_rows8(state_ffn_conv[l]), conv_f[l], nbs, seq_s, seq_s, D_FF)
    act_p = _ffn_act(gu_p, gu_s[None, n_small - SUB:n_small, :D_FF], conv_f[l], nbp, seq_p, 512, 768)
    y_s = _ffn_out(act_s, wfo_bf, x1_s, row(norm_f), n_s, 512, 768)
    y_p = _ffn_out(act_p, wfo_bf, x1_p, row(norm_f), nbp * seq_p, 512, 768)

    def states(proj, gu, nb, seq, sd, sw):
        p3 = proj.reshape(-1, seq, P_CAT)
        conv_new = p3[:nb, seq - 3:, :3 * W_A]
        last = p3[:nb, seq - 1:, :]
        shift_new = jnp.concatenate([last[..., RKV0:SM0], last[..., SM0 + SM_WA:SM0 + SM_WA + W_LORA + A_LORA],
                                     last[..., SM0 + SM_G:SM0 + SM_G + G_LORA]], axis=-1)
        ffn_new = gu.reshape(-1, seq, 2 * D_FF)[:nb, seq - 2:, :D_FF]
        return (sd[None], conv_new[None], sw.reshape(nb, H_B, HEAD_B, HEAD_B)[None], shift_new[None],
                ffn_new[None])

    return ((y_p.reshape(nbp, seq_p, D_MODEL), y_s.reshape(nbs, seq_s, D_MODEL))
            + states(proj_p, gu_p, nbp, seq_p, sd_p, sw_p)
            + states(proj_s, gu_s, nbs, seq_s, sd_s, sw_s))
```

```python
import functools

import jax
import jax.numpy as jnp
from jax import lax
from jax.experimental import pallas as pl
from jax.experimental.pallas import tpu as pltpu

F32 = jnp.float32
BF16 = jnp.bfloat16

D_MODEL = 2048
N_META = 16
W_A = 1024
HEAD_A = 128
H_A = 8
W_B = 1024
HEAD_B = 64
H_B = 16
N_PAIR = H_B // 2
W_LORA = 64
A_LORA = 64
G_LORA = 160
D_FF = 5376
RMS_EPS = 1e-6
GN_EPS = 64e-5
A_PROJ = 4 * W_A + 2 * H_A
B_PROJ = 3 * W_B + W_LORA + A_LORA + G_LORA

QKVZ0 = 0
RKV0 = 4 * W_A
SM0 = RKV0 + 3 * W_B
SM_W = 512
SM_BA = 0
SM_WA = 128
SM_G = 256
P_CAT = SM0 + SM_W

LANE = 128
SUB = 8
VMEM_LIMIT = 48 * 1024 * 1024
NEG_BIG = -1e30

NT_DIMS = (((1,), (1,)), ((), ()))


def _bdot(a, b):
    return jnp.dot(a.astype(BF16), b.astype(BF16), preferred_element_type=F32)


def _bdot_nt(a, b):
    return lax.dot_general(a.astype(BF16), b.astype(BF16), (((1,), (1,)), ((), ())),
                           preferred_element_type=F32)


def _bdot_tn(a, b):
    return lax.dot_general(a.astype(BF16), b.astype(BF16), (((0,), (0,)), ((), ())),
                           preferred_element_type=F32)


def _pieces(a, n):
    out = []
    rem = a
    for i in range(n):
        p = rem.astype(BF16)
        out.append(p)
        if i + 1 < n:
            rem = rem - p.astype(F32)
    return out


def _xdot(a, b, n, dims=(((1,), (0,)), ((), ()))):
    bb = b.astype(BF16)
    acc = None
    for p in _pieces(a, n):
        t = lax.dot_general(p, bb, dims, preferred_element_type=F32)
        acc = t if acc is None else acc + t
    return acc


def _xdot_r(a, b, n, dims=(((1,), (0,)), ((), ()))):
    ab = a.astype(BF16)
    acc = None
    for p in _pieces(b, n):
        t = lax.dot_general(ab, p, dims, preferred_element_type=F32)
        acc = t if acc is None else acc + t
    return acc


def _sigmoid(x):
    return 1.0 / (1.0 + jnp.exp(-x))


def _silu(x):
    return x * _sigmoid(x)


def _softplus(x):
    return jnp.maximum(x, 0.0) + jnp.log(1.0 + jnp.exp(-jnp.abs(x)))


def _iota(shape, dim):
    return lax.broadcasted_iota(jnp.int32, shape, dim)


def _shifted(x, prev8, k):
    n = x.shape[0]
    xr = pltpu.roll(x, k, 0)
    pr = pltpu.roll(prev8, k, 0)
    first = jnp.where(_iota((SUB, x.shape[1]), 0) < k, pr, xr[0:SUB])
    if n == SUB:
        return first
    return jnp.concatenate([first, xr[SUB:]], axis=0)


class _Packed:
    def __init__(self, c):
        self.c = c
        row = _iota((c, 2 * c), 0)
        col = _iota((c, 2 * c), 1) & (c - 1)
        self.incl = row >= col
        self.strict = row > col
        self.eye = (row == col).astype(F32)
        self.first = _iota((c, 2 * c), 1) < c
        self.bd_mask = (_iota((2 * c, 2 * c), 0) < c) == (_iota((2 * c, 2 * c), 1) < c)

    def block_diag(self, m):
        return jnp.where(self.bd_mask, jnp.concatenate([m, m], axis=0), 0.0)

    def inverse_many(self, xs):
        ts = [self.eye + x for x in xs]
        ps = list(xs)
        n = 1
        while 2 * n < self.c:
            ps = [_bdot(p, self.block_diag(p)) for p in ps]
            ts = [t + _bdot(p, self.block_diag(t)) for p, t in zip(ps, ts)]
            n *= 2
        return ts


def _norm_matmul_kernel(x_ref, g_ref, w_ref, o_ref, h_ref):
    @pl.when(pl.program_id(1) == 0)
    def _():
        x = x_ref[...]
        ms = jnp.mean(x * x, axis=-1, keepdims=True)
        h_ref[...] = (x * lax.rsqrt(ms + RMS_EPS) * g_ref[...]).astype(BF16)

    o_ref[...] = jnp.dot(h_ref[...], w_ref[...], preferred_element_type=F32)


def _norm_matmul(x, g, w, tm, tn):
    m, k = x.shape
    n = w.shape[1]
    return pl.pallas_call(
        _norm_matmul_kernel,
        out_shape=jax.ShapeDtypeStruct((m, n), F32),
        grid=(m // tm, n // tn),
        in_specs=[
            pl.BlockSpec((tm, k), lambda i, j: (i, 0)),
            pl.BlockSpec((1, k), lambda i, j: (0, 0)),
            pl.BlockSpec((k, tn), lambda i, j: (0, j)),
        ],
        out_specs=pl.BlockSpec((tm, tn), lambda i, j: (i, j)),
        scratch_shapes=[pltpu.VMEM((tm, k), BF16)],
        compiler_params=pltpu.CompilerParams(
            dimension_semantics=("parallel", "arbitrary"), vmem_limit_bytes=VMEM_LIMIT),
        name="norm_matmul",
    )(x, g, w)


def _out_proj_kernel(x_ref, oa_ref, ob_ref, wt_ref, wb_ref, g_ref, x1_ref, h_ref):
    acc = jnp.dot(oa_ref[...], wt_ref[...], preferred_element_type=F32)
    acc = acc + jnp.dot(ob_ref[...], wb_ref[...], preferred_element_type=F32)
    x1 = x_ref[...] + acc
    x1_ref[...] = x1
    ms = jnp.mean(x1 * x1, axis=-1, keepdims=True)
    h_ref[...] = (x1 * lax.rsqrt(ms + RMS_EPS) * g_ref[...]).astype(BF16)


def _out_proj(x, oa, ob, wo, g, tm):
    m, d = x.shape
    return pl.pallas_call(
        _out_proj_kernel,
        out_shape=(jax.ShapeDtypeStruct((m, d), F32), jax.ShapeDtypeStruct((m, d), BF16)),
        grid=(m // tm,),
        in_specs=[
            pl.BlockSpec((tm, d), lambda i: (i, 0)),
            pl.BlockSpec((tm, W_A), lambda i: (i, 0)),
            pl.BlockSpec((tm, W_B), lambda i: (i, 0)),
            pl.BlockSpec((W_A, d), lambda i: (0, 0)),
            pl.BlockSpec((W_B, d), lambda i: (W_A // W_B, 0)),
            pl.BlockSpec((1, d), lambda i: (0, 0)),
        ],
        out_specs=(pl.BlockSpec((tm, d), lambda i: (i, 0)), pl.BlockSpec((tm, d), lambda i: (i, 0))),
        compiler_params=pltpu.CompilerParams(
            dimension_semantics=("parallel",), vmem_limit_bytes=VMEM_LIMIT),
        name="out_proj",
    )(x, oa, ob, wo, wo, g)


def _ffn_in_kernel(h_ref, wg_ref, wu_ref, prev_ref, cw_ref, act_ref, tail_ref, wgb_ref, wub_ref, carry_ref,
                   *, tiles_per_seq, state_rows):
    m = pl.program_id(1)

    @pl.when(m == 0)
    def _():
        wgb_ref[...] = wg_ref[...].astype(BF16)
        wub_ref[...] = wu_ref[...].astype(BF16)

    h = h_ref[...]
    gate = jnp.dot(h, wgb_ref[...], preferred_element_type=F32)
    up = jnp.dot(h, wub_ref[...], preferred_element_type=F32)
    n = gate.shape[0]
    if tiles_per_seq:
        @pl.when(m % tiles_per_seq == 0)
        def _():
            carry_ref[...] = prev_ref[...]

        prev8 = carry_ref[...]
        d1 = _shifted(gate, prev8, 1)
        d2 = _shifted(gate, prev8, 2)
        carry_ref[...] = gate[n - SUB:n]
        tail_ref[...] = gate[n - SUB:n]
    else:
        row = _iota(gate.shape, 0)
        prev = prev_ref[...]

        def delayed(k):
            use_prev = ((row & (SUB - 1)) < k) & (row < state_rows)
            return jnp.where(use_prev, pltpu.roll(prev, n + k - SUB, 0), pltpu.roll(gate, k, 0))

        d1 = delayed(1)
        d2 = delayed(2)
        tail_ref[...] = gate
    y = gate * cw_ref[2:3, :] + d1 * cw_ref[1:2, :] + d2 * cw_ref[0:1, :]
    act_ref[...] = (_silu(y) * up).astype(BF16)


def _ffn_in(h, w, prev, cw, tm, tn, tiles_per_seq, state_rows=0):
    m, d = h.shape
    nj = D_FF // tn
    nm = m // tm
    if tiles_per_seq:
        nseq = nm // tiles_per_seq
        prev_spec = pl.BlockSpec((None, SUB, tn), lambda j, i: (0, 0, j))
        tail_shape = jax.ShapeDtypeStruct((nseq, SUB, D_FF), F32)
        tail_spec = pl.BlockSpec((None, SUB, tn), lambda j, i: (i // tiles_per_seq, 0, j))
    else:
        prev_spec = pl.BlockSpec((tm, tn), lambda j, i: (i, j))
        tail_shape = jax.ShapeDtypeStruct((m, D_FF), F32)
        tail_spec = pl.BlockSpec((tm, tn), lambda j, i: (i, j))
    return pl.pallas_call(
        functools.partial(_ffn_in_kernel, tiles_per_seq=tiles_per_seq, state_rows=state_rows),
        out_shape=(jax.ShapeDtypeStruct((m, D_FF), BF16), tail_shape),
        grid=(nj, nm),
        in_specs=[
            pl.BlockSpec((tm, d), lambda j, i: (i, 0)),
            pl.BlockSpec((d, tn), lambda j, i: (0, j)),
            pl.BlockSpec((d, tn), lambda j, i: (0, nj + j)),
            prev_spec,
            pl.BlockSpec((3, tn), lambda j, i: (0, j)),
        ],
        out_specs=(pl.BlockSpec((tm, tn), lambda j, i: (i, j)), tail_spec),
        scratch_shapes=[pltpu.VMEM((d, tn), BF16), pltpu.VMEM((d, tn), BF16), pltpu.VMEM((SUB, tn), F32)],
        compiler_params=pltpu.CompilerParams(
            dimension_semantics=("parallel", "arbitrary"), vmem_limit_bytes=VMEM_LIMIT),
        name="ffn_in",
    )(h, w, w, prev, cw)


def _ffn_out_kernel(a_ref, w_ref, x_ref, g_ref, o_ref, acc_ref):
    kk = pl.program_id(1)

    @pl.when(kk == 0)
    def _():
        acc_ref[...] = x_ref[...]

    acc_ref[...] += jnp.dot(a_ref[...], w_ref[...], preferred_element_type=F32)

    @pl.when(kk == pl.num_programs(1) - 1)
    def _():
        x = acc_ref[...]
        ms = jnp.mean(x * x, axis=-1, keepdims=True)
        o_ref[...] = x * lax.rsqrt(ms + RMS_EPS) * g_ref[...]


def _ffn_out(act, w, x1, g, m, tm, tk):
    kdim, d = w.shape
    return pl.pallas_call(
        _ffn_out_kernel,
        out_shape=jax.ShapeDtypeStruct((m, d), F32),
        grid=(m // tm, kdim // tk),
        in_specs=[
            pl.BlockSpec((tm, tk), lambda i, k: (i, k)),
            pl.BlockSpec((tk, d), lambda i, k: (k, 0)),
            pl.BlockSpec((tm, d), lambda i, k: (i, 0)),
            pl.BlockSpec((1, d), lambda i, k: (0, 0)),
        ],
        out_specs=pl.BlockSpec((tm, d), lambda i, k: (i, 0)),
        scratch_shapes=[pltpu.VMEM((tm, d), F32)],
        compiler_params=pltpu.CompilerParams(
            dimension_semantics=("parallel", "arbitrary"), vmem_limit_bytes=VMEM_LIMIT),
        name="ffn_out",
    )(act, w, x1, g)


def _delta_kernel(qkvz_ref, ba_ref, prev_ref, s0_ref, cw_ref, alog_ref, dtb_ref, on_ref,
                  o_ref, s_ref, carry_ref, *, C):
    ci = pl.program_id(1)

    @pl.when(ci == 0)
    def _():
        carry_ref[...] = prev_ref[...]
        s_ref[...] = s0_ref[...]

    tri = (_iota((C, C), 0) >= _iota((C, C), 1)).astype(F32)

    ba = ba_ref[...]
    beta_full = _sigmoid(ba)
    g_full = -jnp.exp(alog_ref[...]) * _softplus(ba + dtb_ref[...])
    gc_full = _xdot_r(tri, g_full, 3)
    lane = _iota((C, LANE), 1)

    def conv_silu(c0):
        x = qkvz_ref[:, c0:c0 + LANE]
        prev8 = carry_ref[:, c0:c0 + LANE]
        y = x * cw_ref[3:4, c0:c0 + LANE]
        for k in (1, 2, 3):
            y = y + _shifted(x, prev8, k) * cw_ref[3 - k:4 - k, c0:c0 + LANE]
        return _silu(y)

    def l2n(x):
        return x * lax.rsqrt(jnp.sum(x * x, axis=-1, keepdims=True) + 1e-6)

    heads = range(H_A)
    q = [l2n(conv_silu(h * HEAD_A)) * (HEAD_A ** -0.5) for h in heads]
    k = [l2n(conv_silu(W_A + h * HEAD_A)) for h in heads]
    v = [conv_silu(2 * W_A + h * HEAD_A) for h in heads]
    bcol = [beta_full[:, h:h + 1] for h in heads]
    gcol = [gc_full[:, H_A + h:H_A + h + 1] for h in heads]
    glast = [gc_full[C - 1:C, H_A + h:H_A + h + 1] for h in heads]
    eg = [jnp.exp(gcol[h]) for h in heads]
    kb = [k[h] * bcol[h] for h in heads]

    pk = _Packed(C)
    hpairs = range(H_A // 2)
    zc = jnp.zeros((C, HEAD_A), F32)
    row2 = _iota((2 * C, LANE), 0)
    lane2 = _iota((2 * C, LANE), 1)
    ones_cl = jnp.ones((C, LANE), F32)
    gc2 = jnp.concatenate([gc_full, gc_full], axis=0)
    kq, gamma = [], []
    for hp in hpairs:
        h0, h1 = 2 * hp, 2 * hp + 1
        lhs = jnp.concatenate([jnp.concatenate([kb[h0], kb[h1]], axis=1),
                               jnp.concatenate([q[h0], q[h1]], axis=1)], axis=0)
        rk = jnp.concatenate([jnp.concatenate([k[h0], zc], axis=1),
                              jnp.concatenate([zc, k[h1]], axis=1)], axis=0)
        kq.append(_bdot_nt(lhs, rk))
        sel = lane2 == jnp.where(row2 < C, H_A + h0, H_A + h1)
        grow = _xdot_r(ones_cl, jnp.where(sel, gc2, 0.0), 3, NT_DIMS)
        gcol_p = jnp.where(pk.first, gcol[h0], gcol[h1])
        gamma.append(jnp.exp(jnp.where(pk.incl, gcol_p - grow, NEG_BIG)))
    tinv = pk.inverse_many([-jnp.where(pk.strict, kq[hp][0:C] * gamma[hp], 0.0) for hp in hpairs])
    uw = []
    for hp in hpairs:
        h0, h1 = 2 * hp, 2 * hp + 1
        z2 = jnp.zeros((C, 2 * HEAD_A), F32)
        rhs = jnp.concatenate([jnp.concatenate([v[h0] * bcol[h0], kb[h0] * eg[h0], z2], axis=1),
                               jnp.concatenate([z2, v[h1] * bcol[h1], kb[h1] * eg[h1]], axis=1)], axis=0)
        uw.append(_bdot(tinv[hp], rhs))
    u = [uw[h // 2][:, (h % 2) * 2 * HEAD_A:(h % 2) * 2 * HEAD_A + HEAD_A] for h in heads]
    w = [uw[h // 2][:, (h % 2) * 2 * HEAD_A + HEAD_A:(h % 2 + 1) * 2 * HEAD_A] for h in heads]
    s = [s_ref[h] for h in heads]
    wqs = [_bdot(jnp.concatenate([w[h], q[h] * eg[h]], axis=0), s[h]) for h in heads]
    v_new = [u[h] - wqs[h][0:C] for h in heads]
    o = []
    for hp in hpairs:
        h0, h1 = 2 * hp, 2 * hp + 1
        vn_bd = jnp.concatenate([jnp.concatenate([v_new[h0], zc], axis=1),
                                 jnp.concatenate([zc, v_new[h1]], axis=1)], axis=0)
        op = _bdot(kq[hp][C:2 * C] * gamma[hp], vn_bd)
        o.append(wqs[h0][C:2 * C] + op[:, :HEAD_A])
        o.append(wqs[h1][C:2 * C] + op[:, HEAD_A:])
    for h in heads:
        kd = k[h] * jnp.exp(glast[h] - gcol[h])
        s_ref[h] = s[h] * jnp.exp(glast[h]) + _bdot_tn(kd, v_new[h])
    for h in heads:
        z = qkvz_ref[:, 3 * W_A + h * HEAD_A:3 * W_A + (h + 1) * HEAD_A]
        oh = o[h] * lax.rsqrt(jnp.mean(o[h] * o[h], axis=-1, keepdims=True) + RMS_EPS)
        o_ref[:, h * HEAD_A:(h + 1) * HEAD_A] = (oh * on_ref[...] * _silu(z)).astype(BF16)

    carry_ref[...] = qkvz_ref[C - SUB:C, 0:3 * W_A]


def _delta_mixer(proj, row0, nb, seq, C, prev, s0, cw, alog_row, dtb_row, onorm):
    nch = seq // C
    blk0 = row0 // C
    bcast = s0.shape[0] == 1
    bsel = (lambda b: 0) if bcast else (lambda b: b)
    return pl.pallas_call(
        functools.partial(_delta_kernel, C=C),
        out_shape=(jax.ShapeDtypeStruct((nb * seq, W_A), BF16),
                   jax.ShapeDtypeStruct((nb, H_A, HEAD_A, HEAD_A), F32)),
        grid=(nb, nch),
        in_specs=[
            pl.BlockSpec((C, 4 * W_A), lambda b, c: (blk0 + b * nch + c, 0)),
            pl.BlockSpec((C, LANE), lambda b, c: (blk0 + b * nch + c, SM0 // LANE)),
            pl.BlockSpec((None, SUB, 3 * W_A), lambda b, c: (bsel(b), 0, 0)),
            pl.BlockSpec((None, H_A, HEAD_A, HEAD_A), lambda b, c: (bsel(b), 0, 0, 0)),
            pl.BlockSpec((4, 3 * W_A), lambda b, c: (0, 0)),
            pl.BlockSpec((1, LANE), lambda b, c: (0, 0)),
            pl.BlockSpec((1, LANE), lambda b, c: (0, 0)),
            pl.BlockSpec((1, HEAD_A), lambda b, c: (0, 0)),
        ],
        out_specs=(pl.BlockSpec((C, W_A), lambda b, c: (b * nch + c, 0)),
                   pl.BlockSpec((None, H_A, HEAD_A, HEAD_A), lambda b, c: (b, 0, 0, 0))),
        scratch_shapes=[pltpu.VMEM((SUB, 3 * W_A), F32)],
        compiler_params=pltpu.CompilerParams(
            dimension_semantics=("parallel", "arbitrary"), vmem_limit_bytes=VMEM_LIMIT),
        name="delta_mixer",
    )(proj, proj, prev, s0, cw, alog_row, dtb_row, onorm)


def _rwkv_kernel(r_ref, k_ref, v_ref, sm_ref, pr_ref, pk_ref, pv_ref, psm_ref, s0_ref,
                 mur_ref, muk_ref, muv_ref, musm_ref, w2_ref, a2_ref, g2_ref,
                 w0_ref, a0_ref, kk_ref, ka_ref, rk_ref, lnw_ref, lnb_ref,
                 o_ref, s_ref, cr_ref, ck_ref, cv_ref, csm_ref, st_ref, *, C, G):
    ci = pl.program_id(2)
    nch = pl.num_programs(2)

    r2 = _iota((LANE, LANE), 0)
    c2 = _iota((LANE, LANE), 1)
    same_head = (r2 < HEAD_B) == (c2 < HEAD_B)
    ones_bd = same_head.astype(F32)
    spread = (_iota((HEAD_B, LANE), 0) == (_iota((HEAD_B, LANE), 1) & (HEAD_B - 1))).astype(F32)
    gather = ((_iota((LANE, HEAD_B), 0) & (HEAD_B - 1)) == _iota((LANE, HEAD_B), 1)).astype(F32)

    @pl.when(ci == 0)
    def _():
        cr_ref[...] = pr_ref[...]
        ck_ref[...] = pk_ref[...]
        cv_ref[...] = pv_ref[...]
        csm_ref[...] = psm_ref[...]
        for p in range(G):
            st_ref[p] = jnp.where(same_head, _xdot(s0_ref[p], spread, 3), 0.0)

    tri = (_iota((C, C), 0) >= _iota((C, C), 1)).astype(F32)

    def lerp(x, prev8, mu):
        return x + (_shifted(x, prev8, 1) - x) * mu

    sm = sm_ref[...]
    xs = lerp(sm, csm_ref[...], musm_ref[...])
    slab_wa = xs[:, SM_WA:SM_WA + LANE]
    slab_g = xs[:, SM_G:SM_G + 2 * LANE]

    pairs = range(G)

    def col(x, p):
        return x[:, p * LANE:(p + 1) * LANE]

    def to_rows(x):
        return jnp.concatenate([col(x, p) for p in pairs], axis=0)

    def to_cols(y):
        return jnp.concatenate([y[p * C:(p + 1) * C] for p in pairs], axis=1)

    xr = lerp(r_ref[...], cr_ref[...], mur_ref[...])
    xk = lerp(k_ref[...], ck_ref[...], muk_ref[...])
    xv = lerp(v_ref[...], cv_ref[...], muv_ref[...])
    wlog = -_softplus(-(w0_ref[...] + _bdot(jnp.tanh(slab_wa), w2_ref[...]))) - 0.5
    ld = -jnp.exp(wlog)
    aa = _sigmoid(a0_ref[...] + _bdot(slab_wa, a2_ref[...]))
    gate = _bdot(_sigmoid(slab_g), g2_ref[...])
    kkr = xk * kk_ref[...]
    kkn = kkr * lax.rsqrt(to_cols(_bdot(to_rows(kkr * kkr), ones_bd)) + 1e-6)
    k2 = xk * (1.0 + (aa - 1.0) * ka_ref[...])
    lp = _xdot_r(tri, ld, 3)
    e_neg = jnp.exp(-lp)
    lp_last = lp[C - 1:C, :]
    e_rem = jnp.exp(lp_last - lp)
    at = -kkn * jnp.exp(lp - ld)
    rt = xr * jnp.exp(lp)
    kb = kkn * aa
    bt = kb * e_neg
    kt = k2 * e_neg
    bhat = kb * e_rem
    khat = k2 * e_rem
    p_last = jnp.exp(lp_last)

    s = [st_ref[p] for p in pairs]
    ar = [jnp.concatenate([col(at, p), col(rt, p)], axis=0) for p in pairs]
    ars = [_bdot_nt(ar[p], s[p]) for p in pairs]
    pk = _Packed(C)
    stack_mask = (_iota((2 * C, LANE), 0) < C) == (_iota((2 * C, LANE), 1) < HEAD_B)

    def by_head(x):
        return jnp.where(stack_mask, jnp.concatenate([x, x], axis=0), 0.0)

    ab = [_bdot_nt(ar[p], by_head(col(bt, p))) for p in pairs]
    ak = [_bdot_nt(ar[p], by_head(col(kt, p))) for p in pairs]
    tinv = pk.inverse_many([jnp.where(pk.strict, m[0:C], 0.0) for m in ab])
    v_bh = [by_head(col(xv, p)) for p in pairs]
    y = [ars[p][0:C] + _bdot(jnp.where(pk.strict, ak[p][0:C], 0.0), v_bh[p]) for p in pairs]
    u = [_bdot(tinv[p], by_head(y[p])) for p in pairs]
    o = [ars[p][C:2 * C] + _bdot(jnp.where(pk.incl, ab[p][C:2 * C], 0.0), by_head(u[p]))
         + _bdot(jnp.where(pk.incl, ak[p][C:2 * C], 0.0), v_bh[p]) for p in pairs]
    for p in pairs:
        uv = jnp.concatenate([u[p], col(xv, p)], axis=0)
        bkhat = jnp.concatenate([col(bhat, p), col(khat, p)], axis=0)
        s_new = s[p] * col(p_last, p) + _bdot_tn(uv, bkhat)
        st_ref[p] = jnp.where(same_head, s_new, 0.0)

    o_rows = jnp.concatenate(o, axis=0)
    mean = _bdot(o_rows, ones_bd) * (1.0 / HEAD_B)
    d = o_rows - mean
    var = _bdot(d * d, ones_bd) * (1.0 / HEAD_B)
    on = to_cols(d * lax.rsqrt(var + GN_EPS)) * lnw_ref[...] + lnb_ref[...]
    bonus = to_cols(_bdot(to_rows(xr * k2 * rk_ref[...]), ones_bd)) * xv
    o_ref[...] = ((on + bonus) * gate).astype(BF16)

    cr_ref[...] = r_ref[C - SUB:C, :]
    ck_ref[...] = k_ref[C - SUB:C, :]
    cv_ref[...] = v_ref[C - SUB:C, :]
    csm_ref[...] = sm_ref[C - SUB:C, :]

    @pl.when(ci == nch - 1)
    def _():
        for p in range(G):
            s_ref[p] = _xdot(st_ref[p], gather, 3)


def _rwkv_mixer(proj, row0, nb, seq, C, G, prev_rkv, prev_sm, s0, mu_rkv, mu_sm, w2p, a2p, g2p,
                w0, a0, k_k, k_a, r_k, lnw, lnb):
    nch = seq // C
    ng = N_PAIR // G
    gw = G * LANE
    blk0 = row0 // C
    bcast = s0.shape[0] == 1
    bsel = (lambda b: 0) if bcast else (lambda b: b)

    def proj_spec(col0):
        return pl.BlockSpec((C, gw), lambda b, g, c: (blk0 + b * nch + c, col0 // gw + g))

    def prev_spec(part):
        return pl.BlockSpec((None, SUB, gw), lambda b, g, c: (bsel(b), 0, part * (W_B // gw) + g))

    def vec_spec(part=0):
        return pl.BlockSpec((1, gw), lambda b, g, c: (0, part * (W_B // gw) + g))

    in_specs = [
        proj_spec(RKV0), proj_spec(RKV0 + W_B), proj_spec(RKV0 + 2 * W_B),
        pl.BlockSpec((C, SM_W), lambda b, g, c: (blk0 + b * nch + c, SM0 // SM_W)),
        prev_spec(0), prev_spec(1), prev_spec(2),
        pl.BlockSpec((None, SUB, SM_W), lambda b, g, c: (bsel(b), 0, 0)),
        pl.BlockSpec((None, G, LANE, HEAD_B), lambda b, g, c: (bsel(b), g, 0, 0)),
        vec_spec(0), vec_spec(1), vec_spec(2),
        pl.BlockSpec((1, SM_W), lambda b, g, c: (0, 0)),
        pl.BlockSpec((LANE, gw), lambda b, g, c: (0, g)),
        pl.BlockSpec((LANE, gw), lambda b, g, c: (0, g)),
        pl.BlockSpec((2 * LANE, gw), lambda b, g, c: (0, g)),
        vec_spec(), vec_spec(), vec_spec(), vec_spec(), vec_spec(), vec_spec(), vec_spec(),
    ]
    return pl.pallas_call(
        functools.partial(_rwkv_kernel, C=C, G=G),
        out_shape=(jax.ShapeDtypeStruct((nb * seq, W_B), BF16),
                   jax.ShapeDtypeStruct((nb, N_PAIR, LANE, HEAD_B), F32)),
        grid=(nb, ng, nch),
        in_specs=in_specs,
        out_specs=(pl.BlockSpec((C, gw), lambda b, g, c: (b * nch + c, g)),
                   pl.BlockSpec((None, G, LANE, HEAD_B), lambda b, g, c: (b, g, 0, 0))),
        scratch_shapes=[pltpu.VMEM((SUB, gw), F32), pltpu.VMEM((SUB, gw), F32), pltpu.VMEM((SUB, gw), F32),
                        pltpu.VMEM((SUB, SM_W), F32), pltpu.VMEM((G, LANE, LANE), F32)],
        compiler_params=pltpu.CompilerParams(
            dimension_semantics=("parallel", "parallel", "arbitrary"), vmem_limit_bytes=VMEM_LIMIT),
        name="rwkv_mixer",
    )(proj, proj, proj, proj, prev_rkv, prev_rkv, prev_rkv, prev_sm, s0,
      mu_rkv, mu_rkv, mu_rkv, mu_sm, w2p, a2p, g2p, w0, a0, k_k, k_a, r_k, lnw, lnb)


def _small_layout(cols_ba, cols_w, cols_a, cols_g, lead):
    def z(n):
        return jnp.zeros(lead + (n,), cols_w.dtype)
    return jnp.concatenate(
        [cols_ba, z(SM_WA - cols_ba.shape[-1]), cols_w, cols_a, cols_g,
         z(SM_W - SM_G - cols_g.shape[-1])], axis=-1)


def _pad_rows8(x):
    b, n, c = x.shape
    return jnp.concatenate([jnp.zeros((b, SUB - n, c), x.dtype), x], axis=1)


def kernel(x_prompt, x_sample, state_delta, state_conv_qkv, state_wkv, state_shift, state_ffn_conv, meta, norm1, w_in, conv_a, a_log, dt_bias, onorm_a, mu_b, w0, w2, a0, a2, g2, k_k, k_a, r_k, lnx_w, lnx_b, w_o, norm2, w_ffn_in, conv_f, w_ffn_out, norm_f):
    nbp, seq_p, _ = x_prompt.shape
    nbs, seq_s, _ = x_sample.shape
    n_s = nbs * seq_s
    assert w_in.shape[0] == 1, "single-layer trunk"
    l = 0

    wi = w_in[l]
    o_b = A_PROJ
    o_l = A_PROJ + 3 * W_B
    w_cat = jnp.concatenate([
        wi[:, :4 * W_A], wi[:, o_b:o_l],
        _small_layout(wi[:, 4 * W_A:A_PROJ], wi[:, o_l:o_l + W_LORA],
                      wi[:, o_l + W_LORA:o_l + W_LORA + A_LORA], wi[:, o_l + W_LORA + A_LORA:],
                      (D_MODEL,))], axis=1).astype(BF16)
    mu = mu_b[l]
    mu_rkv = mu[None, :3 * W_B]
    mu_sm = _small_layout(jnp.zeros((1, 2 * H_A), F32), mu[None, 3 * W_B:3 * W_B + W_LORA],
                          mu[None, 3 * W_B + W_LORA:3 * W_B + W_LORA + A_LORA],
                          mu[None, 3 * W_B + W_LORA + A_LORA:], (1,))
    w2p = jnp.concatenate([w2[l], jnp.zeros((LANE - W_LORA, W_B), F32)], axis=0)
    a2p = jnp.concatenate([jnp.zeros((W_LORA, W_B), F32), a2[l]], axis=0)
    g2p = jnp.concatenate([g2[l], jnp.zeros((2 * LANE - G_LORA, W_B), F32)], axis=0)
    alog_row = jnp.concatenate([jnp.zeros((H_A,), F32), a_log[l], jnp.zeros((LANE - 2 * H_A,), F32)])[None]
    dtb_row = jnp.concatenate([jnp.zeros((H_A,), F32), dt_bias[l], jnp.zeros((LANE - 2 * H_A,), F32)])[None]
    wo_bf = w_o[l].astype(BF16)
    wfo_bf = w_ffn_out[l].astype(BF16)
    row = lambda v: v.reshape(1, -1)

    def mix(proj, row0, nb, seq, C, prev_qkv, prev_rkv, prev_sm, s_delta, s_wkv):
        oa, sd = _delta_mixer(proj, row0, nb, seq, C, prev_qkv, s_delta, conv_a[l], alog_row, dtb_row,
                              row(onorm_a[l]))
        ob, sw = _rwkv_mixer(proj, row0, nb, seq, C, N_PAIR, prev_rkv, prev_sm, s_wkv, mu_rkv, mu_sm,
                             w2p, a2p, g2p, row(w0[l]), row(a0[l]), row(k_k[l]), row(k_a[l]),
                             row(r_k[l]), row(lnx_w[l]), row(lnx_b[l]))
        return oa, ob, sd, sw

    xs_rows = jnp.concatenate([x_sample.reshape(n_s, D_MODEL), meta], axis=0)
    xp_rows = x_prompt.reshape(nbp * seq_p, D_MODEL)
    n_small = n_s + N_META
    proj_s = _norm_matmul(xs_rows, row(norm1[l]), w_cat, n_small, 512)
    proj_p = _norm_matmul(xp_rows, row(norm1[l]), w_cat, 1024, 512)

    zeros = lambda *s: jnp.zeros(s, F32)
    oa_m, ob_m, sd_m, sw_m = mix(proj_s, n_s, 1, N_META, N_META, zeros(1, SUB, 3 * W_A),
                                 zeros(1, SUB, 3 * W_B), zeros(1, SUB, SM_W),
                                 zeros(1, H_A, HEAD_A, HEAD_A), zeros(1, N_PAIR, LANE, HEAD_B))
    tail = proj_s[n_small - SUB:n_small]
    oa_p, ob_p, sd_p, sw_p = mix(proj_p, 0, nbp, seq_p, 64, tail[None, :, :3 * W_A],
                                 tail[None, :, RKV0:SM0], tail[None, :, SM0:], sd_m, sw_m)
    sh = state_shift[l]
    sh_sm = _small_layout(jnp.zeros((nbs, 1, 2 * H_A), F32), sh[..., 3 * W_B:3 * W_B + W_LORA],
                          sh[..., 3 * W_B + W_LORA:3 * W_B + W_LORA + A_LORA],
                          sh[..., 3 * W_B + W_LORA + A_LORA:], (nbs, 1))
    oa_s, ob_s, sd_s, sw_s = mix(proj_s, 0, nbs, seq_s, seq_s, _pad_rows8(state_conv_qkv[l]),
                                 _pad_rows8(sh[..., :3 * W_B]), _pad_rows8(sh_sm), state_delta[l],
                                 state_wkv[l].reshape(nbs, N_PAIR, LANE, HEAD_B))

    oa_small = jnp.concatenate([oa_s, oa_m], axis=0)
    ob_small = jnp.concatenate([ob_s, ob_m], axis=0)
    x1_s, h2_s = _out_proj(xs_rows, oa_small, ob_small, wo_bf, row(norm2[l]), n_small // 5)
    x1_p, h2_p = _out_proj(xp_rows, oa_p, ob_p, wo_bf, row(norm2[l]), 512)
    prev_rows = jnp.concatenate([_pad_rows8(state_ffn_conv[l]).reshape(n_s, D_FF),
                                 jnp.zeros((N_META, D_FF), F32)], axis=0)
    act_s, gate_s = _ffn_in(h2_s, w_ffn_in[l], prev_rows, conv_f[l], n_small, 384, 0, n_s + SUB)
    act_p, tail_p = _ffn_in(h2_p, w_ffn_in[l], gate_s[None, n_small - SUB:n_small], conv_f[l], 1024, 384,
                            seq_p // 1024)
    y_s = _ffn_out(act_s, wfo_bf, x1_s, row(norm_f), n_s, 512, 768)
    y_p = _ffn_out(act_p, wfo_bf, x1_p, row(norm_f), nbp * seq_p, 512, 768)

    def states(proj, gate_tail, nb, seq, sd, sw):
        p3 = proj.reshape(-1, seq, P_CAT)
        conv_new = p3[:nb, seq - 3:, :3 * W_A]
        last = p3[:nb, seq - 1:, :]
        shift_new = jnp.concatenate([last[..., RKV0:SM0], last[..., SM0 + SM_WA:SM0 + SM_WA + W_LORA + A_LORA],
                                     last[..., SM0 + SM_G:SM0 + SM_G + G_LORA]], axis=-1)
        ffn_new = gate_tail[:nb, SUB - 2:, :]
        return (sd[None], conv_new[None], sw.reshape(nb, H_B, HEAD_B, HEAD_B)[None], shift_new[None],
                ffn_new[None])

    return ((y_p.reshape(nbp, seq_p, D_MODEL), y_s.reshape(nbs, seq_s, D_MODEL))
            + states(proj_p, tail_p, nbp, seq_p, sd_p, sw_p)
            + states(proj_s, gate_s.reshape(-1, SUB, D_FF), nbs, seq_s, sd_s, sw_s))
```

```python
import functools

import jax
import jax.numpy as jnp
from jax import lax
from jax.experimental import pallas as pl
from jax.experimental.pallas import tpu as pltpu

F32 = jnp.float32
BF16 = jnp.bfloat16

D_MODEL = 2048
N_META = 16
W_A = 1024
HEAD_A = 128
H_A = 8
W_B = 1024
HEAD_B = 64
H_B = 16
N_PAIR = H_B // 2
W_LORA = 64
A_LORA = 64
G_LORA = 160
D_FF = 5376
RMS_EPS = 1e-6
GN_EPS = 64e-5
A_PROJ = 4 * W_A + 2 * H_A
B_PROJ = 3 * W_B + W_LORA + A_LORA + G_LORA

QKVZ0 = 0
RKV0 = 4 * W_A
SM0 = RKV0 + 3 * W_B
SM_W = 512
SM_BA = 0
SM_WA = 128
SM_G = 256
P_CAT = SM0 + SM_W

LANE = 128
SUB = 8
VMEM_LIMIT = 48 * 1024 * 1024
VMEM_LIMIT_BIG = 58 * 1024 * 1024
NEG_BIG = -1e30

NT_DIMS = (((1,), (1,)), ((), ()))


def _bdot(a, b):
    return jnp.dot(a.astype(BF16), b.astype(BF16), preferred_element_type=F32)


def _bdot_nt(a, b):
    return lax.dot_general(a.astype(BF16), b.astype(BF16), (((1,), (1,)), ((), ())),
                           preferred_element_type=F32)


def _bdot_tn(a, b):
    return lax.dot_general(a.astype(BF16), b.astype(BF16), (((0,), (0,)), ((), ())),
                           preferred_element_type=F32)


def _pieces(a, n):
    out = []
    rem = a
    for i in range(n):
        p = rem.astype(BF16)
        out.append(p)
        if i + 1 < n:
            rem = rem - p.astype(F32)
    return out


def _xdot(a, b, n, dims=(((1,), (0,)), ((), ()))):
    bb = b.astype(BF16)
    acc = None
    for p in _pieces(a, n):
        t = lax.dot_general(p, bb, dims, preferred_element_type=F32)
        acc = t if acc is None else acc + t
    return acc


def _xdot_r(a, b, n, dims=(((1,), (0,)), ((), ()))):
    ab = a.astype(BF16)
    acc = None
    for p in _pieces(b, n):
        t = lax.dot_general(ab, p, dims, preferred_element_type=F32)
        acc = t if acc is None else acc + t
    return acc


def _sigmoid(x):
    return 1.0 / (1.0 + jnp.exp(-x))


def _silu(x):
    return x * _sigmoid(x)


def _softplus(x):
    return jnp.maximum(x, 0.0) + jnp.log(1.0 + jnp.exp(-jnp.abs(x)))


def _iota(shape, dim):
    return lax.broadcasted_iota(jnp.int32, shape, dim)


def _shifted(x, prev8, k):
    n = x.shape[0]
    xr = pltpu.roll(x, k, 0)
    pr = pltpu.roll(prev8, k, 0)
    first = jnp.where(_iota((SUB, x.shape[1]), 0) < k, pr, xr[0:SUB])
    if n == SUB:
        return first
    return jnp.concatenate([first, xr[SUB:]], axis=0)


class _Packed:
    def __init__(self, c):
        self.c = c
        row = _iota((c, 2 * c), 0)
        col = _iota((c, 2 * c), 1) & (c - 1)
        self.incl = row >= col
        self.strict = row > col
        self.eye = (row == col).astype(F32)
        self.first = _iota((c, 2 * c), 1) < c
        self.bd_mask = (_iota((2 * c, 2 * c), 0) < c) == (_iota((2 * c, 2 * c), 1) < c)

    def block_diag(self, m):
        return jnp.where(self.bd_mask, jnp.concatenate([m, m], axis=0), 0.0)

    def inverse_many(self, xs):
        ts = [self.eye + x for x in xs]
        ps = list(xs)
        n = 1
        while 2 * n < self.c:
            ps = [_bdot(p, self.block_diag(p)) for p in ps]
            ts = [t + _bdot(p, self.block_diag(t)) for p, t in zip(ps, ts)]
            n *= 2
        return ts


def _norm_matmul_kernel(x_ref, g_ref, w_ref, o_ref, h_ref):
    @pl.when(pl.program_id(1) == 0)
    def _():
        x = x_ref[...]
        ms = jnp.mean(x * x, axis=-1, keepdims=True)
        h_ref[...] = (x * lax.rsqrt(ms + RMS_EPS) * g_ref[...]).astype(BF16)

    o_ref[...] = jnp.dot(h_ref[...], w_ref[...], preferred_element_type=F32)


def _norm_matmul(x, g, w, tm, tn):
    m, k = x.shape
    n = w.shape[1]
    return pl.pallas_call(
        _norm_matmul_kernel,
        out_shape=jax.ShapeDtypeStruct((m, n), F32),
        grid=(m // tm, n // tn),
        in_specs=[
            pl.BlockSpec((tm, k), lambda i, j: (i, 0)),
            pl.BlockSpec((1, k), lambda i, j: (0, 0)),
            pl.BlockSpec((k, tn), lambda i, j: (0, j)),
        ],
        out_specs=pl.BlockSpec((tm, tn), lambda i, j: (i, j)),
        scratch_shapes=[pltpu.VMEM((tm, k), BF16)],
        compiler_params=pltpu.CompilerParams(
            dimension_semantics=("parallel", "arbitrary"), vmem_limit_bytes=VMEM_LIMIT),
        name="norm_matmul",
    )(x, g, w)


def _out_proj_kernel(x_ref, oa_ref, ob_ref, wt_ref, wb_ref, g_ref, x1_ref, h_ref):
    acc = jnp.dot(oa_ref[...], wt_ref[...], preferred_element_type=F32)
    acc = acc + jnp.dot(ob_ref[...], wb_ref[...], preferred_element_type=F32)
    x1 = x_ref[...] + acc
    x1_ref[...] = x1
    ms = jnp.mean(x1 * x1, axis=-1, keepdims=True)
    h_ref[...] = (x1 * lax.rsqrt(ms + RMS_EPS) * g_ref[...]).astype(BF16)


def _out_proj(x, oa, ob, wo, g, tm):
    m, d = x.shape
    return pl.pallas_call(
        _out_proj_kernel,
        out_shape=(jax.ShapeDtypeStruct((m, d), F32), jax.ShapeDtypeStruct((m, d), BF16)),
        grid=(m // tm,),
        in_specs=[
            pl.BlockSpec((tm, d), lambda i: (i, 0)),
            pl.BlockSpec((tm, W_A), lambda i: (i, 0)),
            pl.BlockSpec((tm, W_B), lambda i: (i, 0)),
            pl.BlockSpec((W_A, d), lambda i: (0, 0)),
            pl.BlockSpec((W_B, d), lambda i: (W_A // W_B, 0)),
            pl.BlockSpec((1, d), lambda i: (0, 0)),
        ],
        out_specs=(pl.BlockSpec((tm, d), lambda i: (i, 0)), pl.BlockSpec((tm, d), lambda i: (i, 0))),
        compiler_params=pltpu.CompilerParams(
            dimension_semantics=("parallel",), vmem_limit_bytes=VMEM_LIMIT),
        name="out_proj",
    )(x, oa, ob, wo, wo, g)


def _ffn_in_kernel(h_ref, wg_ref, wu_ref, prev_ref, cw_ref, act_ref, tail_ref, wgb_ref, wub_ref, carry_ref,
                   *, tiles_per_seq, state_rows, nsub):
    m = pl.program_id(1)

    @pl.when(m == 0)
    def _():
        wgb_ref[...] = wg_ref[...].astype(BF16)
        wub_ref[...] = wu_ref[...].astype(BF16)

    if tiles_per_seq:
        @pl.when(m % tiles_per_seq == 0)
        def _():
            carry_ref[...] = prev_ref[...]

    h = h_ref[...]
    n = h.shape[0]
    w = wgb_ref.shape[1] // nsub
    for i in range(nsub):
        cs = slice(i * w, (i + 1) * w)
        gate = jnp.dot(h, wgb_ref[:, cs], preferred_element_type=F32)
        up = jnp.dot(h, wub_ref[:, cs], preferred_element_type=F32)
        if tiles_per_seq:
            prev8 = carry_ref[:, cs]
            d1 = _shifted(gate, prev8, 1)
            d2 = _shifted(gate, prev8, 2)
            carry_ref[:, cs] = gate[n - SUB:n]
            tail_ref[:, cs] = gate[n - SUB:n]
        else:
            row = _iota(gate.shape, 0)
            prev = prev_ref[:, cs]
            tail_ref[:, cs] = gate

            def delayed(k):
                use_prev = ((row & (SUB - 1)) < k) & (row < state_rows)
                return jnp.where(use_prev, pltpu.roll(prev, n + k - SUB, 0), pltpu.roll(gate, k, 0))

            d1 = delayed(1)
            d2 = delayed(2)
        y = gate * cw_ref[2:3, cs] + d1 * cw_ref[1:2, cs] + d2 * cw_ref[0:1, cs]
        act_ref[:, cs] = (_silu(y) * up).astype(BF16)


def _ffn_in(h, w, prev, cw, tm, tn, nsub, tiles_per_seq, state_rows=0):
    m, d = h.shape
    nj = D_FF // tn
    nm = m // tm
    if tiles_per_seq:
        nseq = nm // tiles_per_seq
        prev_spec = pl.BlockSpec((None, SUB, tn), lambda j, i: (0, 0, j))
        tail_shape = jax.ShapeDtypeStruct((nseq, SUB, D_FF), F32)
        tail_spec = pl.BlockSpec((None, SUB, tn), lambda j, i: (i // tiles_per_seq, 0, j))
    else:
        prev_spec = pl.BlockSpec((tm, tn), lambda j, i: (i, j))
        tail_shape = jax.ShapeDtypeStruct((m, D_FF), F32)
        tail_spec = pl.BlockSpec((tm, tn), lambda j, i: (i, j))
    return pl.pallas_call(
        functools.partial(_ffn_in_kernel, tiles_per_seq=tiles_per_seq, state_rows=state_rows, nsub=nsub),
        out_shape=(jax.ShapeDtypeStruct((m, D_FF), BF16), tail_shape),
        grid=(nj, nm),
        in_specs=[
            pl.BlockSpec((tm, d), lambda j, i: (i, 0)),
            pl.BlockSpec((d, tn), lambda j, i: (0, j)),
            pl.BlockSpec((d, tn), lambda j, i: (0, nj + j)),
            prev_spec,
            pl.BlockSpec((3, tn), lambda j, i: (0, j)),
        ],
        out_specs=(pl.BlockSpec((tm, tn), lambda j, i: (i, j)), tail_spec),
        scratch_shapes=[pltpu.VMEM((d, tn), BF16), pltpu.VMEM((d, tn), BF16), pltpu.VMEM((SUB, tn), F32)],
        compiler_params=pltpu.CompilerParams(
            dimension_semantics=("parallel", "arbitrary"), vmem_limit_bytes=VMEM_LIMIT_BIG),
        name="ffn_in",
    )(h, w, w, prev, cw)


def _ffn_out_kernel(a_ref, w_ref, x_ref, g_ref, o_ref, acc_ref):
    kk = pl.program_id(1)

    @pl.when(kk == 0)
    def _():
        acc_ref[...] = x_ref[...]

    acc_ref[...] += jnp.dot(a_ref[...], w_ref[...], preferred_element_type=F32)

    @pl.when(kk == pl.num_programs(1) - 1)
    def _():
        x = acc_ref[...]
        ms = jnp.mean(x * x, axis=-1, keepdims=True)
        o_ref[...] = x * lax.rsqrt(ms + RMS_EPS) * g_ref[...]


def _ffn_out(act, w, x1, g, m, tm, tk):
    kdim, d = w.shape
    return pl.pallas_call(
        _ffn_out_kernel,
        out_shape=jax.ShapeDtypeStruct((m, d), F32),
        grid=(m // tm, kdim // tk),
        in_specs=[
            pl.BlockSpec((tm, tk), lambda i, k: (i, k)),
            pl.BlockSpec((tk, d), lambda i, k: (k, 0)),
            pl.BlockSpec((tm, d), lambda i, k: (i, 0)),
            pl.BlockSpec((1, d), lambda i, k: (0, 0)),
        ],
        out_specs=pl.BlockSpec((tm, d), lambda i, k: (i, 0)),
        scratch_shapes=[pltpu.VMEM((tm, d), F32)],
        compiler_params=pltpu.CompilerParams(
            dimension_semantics=("parallel", "arbitrary"), vmem_limit_bytes=VMEM_LIMIT),
        name="ffn_out",
    )(act, w, x1, g)


def _delta_kernel(qkvz_ref, ba_ref, prev_ref, s0_ref, cw_ref, alog_ref, dtb_ref, on_ref,
                  o_ref, s_ref, carry_ref, *, C):
    ci = pl.program_id(1)

    @pl.when(ci == 0)
    def _():
        carry_ref[...] = prev_ref[...]
        s_ref[...] = s0_ref[...]

    tri = (_iota((C, C), 0) >= _iota((C, C), 1)).astype(F32)

    ba = ba_ref[...]
    beta_full = _sigmoid(ba)
    g_full = -jnp.exp(alog_ref[...]) * _softplus(ba + dtb_ref[...])
    gc_full = _xdot_r(tri, g_full, 3)
    lane = _iota((C, LANE), 1)

    def conv_silu(c0):
        x = qkvz_ref[:, c0:c0 + LANE]
        prev8 = carry_ref[:, c0:c0 + LANE]
        y = x * cw_ref[3:4, c0:c0 + LANE]
        for k in (1, 2, 3):
            y = y + _shifted(x, prev8, k) * cw_ref[3 - k:4 - k, c0:c0 + LANE]
        return _silu(y)

    def l2n(x):
        return x * lax.rsqrt(jnp.sum(x * x, axis=-1, keepdims=True) + 1e-6)

    heads = range(H_A)
    q = [l2n(conv_silu(h * HEAD_A)) * (HEAD_A ** -0.5) for h in heads]
    k = [l2n(conv_silu(W_A + h * HEAD_A)) for h in heads]
    v = [conv_silu(2 * W_A + h * HEAD_A) for h in heads]
    bcol = [beta_full[:, h:h + 1] for h in heads]
    gcol = [gc_full[:, H_A + h:H_A + h + 1] for h in heads]
    glast = [gc_full[C - 1:C, H_A + h:H_A + h + 1] for h in heads]
    eg = [jnp.exp(gcol[h]) for h in heads]
    kb = [k[h] * bcol[h] for h in heads]

    pk = _Packed(C)
    hpairs = range(H_A // 2)
    zc = jnp.zeros((C, HEAD_A), F32)
    row2 = _iota((2 * C, LANE), 0)
    lane2 = _iota((2 * C, LANE), 1)
    ones_cl = jnp.ones((C, LANE), F32)
    gc2 = jnp.concatenate([gc_full, gc_full], axis=0)
    kq, gamma = [], []
    for hp in hpairs:
        h0, h1 = 2 * hp, 2 * hp + 1
        lhs = jnp.concatenate([jnp.concatenate([kb[h0], kb[h1]], axis=1),
                               jnp.concatenate([q[h0], q[h1]], axis=1)], axis=0)
        rk = jnp.concatenate([jnp.concatenate([k[h0], zc], axis=1),
                              jnp.concatenate([zc, k[h1]], axis=1)], axis=0)
        kq.append(_bdot_nt(lhs, rk))
        sel = lane2 == jnp.where(row2 < C, H_A + h0, H_A + h1)
        grow = _xdot_r(ones_cl, jnp.where(sel, gc2, 0.0), 3, NT_DIMS)
        gcol_p = jnp.where(pk.first, gcol[h0], gcol[h1])
        gamma.append(jnp.exp(jnp.where(pk.incl, gcol_p - grow, NEG_BIG)))
    tinv = pk.inverse_many([-jnp.where(pk.strict, kq[hp][0:C] * gamma[hp], 0.0) for hp in hpairs])
    uw = []
    for hp in hpairs:
        h0, h1 = 2 * hp, 2 * hp + 1
        z2 = jnp.zeros((C, 2 * HEAD_A), F32)
        rhs = jnp.concatenate([jnp.concatenate([v[h0] * bcol[h0], kb[h0] * eg[h0], z2], axis=1),
                               jnp.concatenate([z2, v[h1] * bcol[h1], kb[h1] * eg[h1]], axis=1)], axis=0)
        uw.append(_bdot(tinv[hp], rhs))
    u = [uw[h // 2][:, (h % 2) * 2 * HEAD_A:(h % 2) * 2 * HEAD_A + HEAD_A] for h in heads]
    w = [uw[h // 2][:, (h % 2) * 2 * HEAD_A + HEAD_A:(h % 2 + 1) * 2 * HEAD_A] for h in heads]
    s = [s_ref[h] for h in heads]
    wqs = [_bdot(jnp.concatenate([w[h], q[h] * eg[h]], axis=0), s[h]) for h in heads]
    v_new = [u[h] - wqs[h][0:C] for h in heads]
    o = []
    for hp in hpairs:
        h0, h1 = 2 * hp, 2 * hp + 1
        vn_bd = jnp.concatenate([jnp.concatenate([v_new[h0], zc], axis=1),
                                 jnp.concatenate([zc, v_new[h1]], axis=1)], axis=0)
        op = _bdot(kq[hp][C:2 * C] * gamma[hp], vn_bd)
        o.append(wqs[h0][C:2 * C] + op[:, :HEAD_A])
        o.append(wqs[h1][C:2 * C] + op[:, HEAD_A:])
    for h in heads:
        kd = k[h] * jnp.exp(glast[h] - gcol[h])
        s_ref[h] = s[h] * jnp.exp(glast[h]) + _bdot_tn(kd, v_new[h])
    for h in heads:
        z = qkvz_ref[:, 3 * W_A + h * HEAD_A:3 * W_A + (h + 1) * HEAD_A]
        oh = o[h] * lax.rsqrt(jnp.mean(o[h] * o[h], axis=-1, keepdims=True) + RMS_EPS)
        o_ref[:, h * HEAD_A:(h + 1) * HEAD_A] = (oh * on_ref[...] * _silu(z)).astype(BF16)

    carry_ref[...] = qkvz_ref[C - SUB:C, 0:3 * W_A]


def _delta_mixer(proj, row0, nb, seq, C, prev, s0, cw, alog_row, dtb_row, onorm):
    nch = seq // C
    blk0 = row0 // C
    bcast = s0.shape[0] == 1
    bsel = (lambda b: 0) if bcast else (lambda b: b)
    return pl.pallas_call(
        functools.partial(_delta_kernel, C=C),
        out_shape=(jax.ShapeDtypeStruct((nb * seq, W_A), BF16),
                   jax.ShapeDtypeStruct((nb, H_A, HEAD_A, HEAD_A), F32)),
        grid=(nb, nch),
        in_specs=[
            pl.BlockSpec((C, 4 * W_A), lambda b, c: (blk0 + b * nch + c, 0)),
            pl.BlockSpec((C, LANE), lambda b, c: (blk0 + b * nch + c, SM0 // LANE)),
            pl.BlockSpec((None, SUB, 3 * W_A), lambda b, c: (bsel(b), 0, 0)),
            pl.BlockSpec((None, H_A, HEAD_A, HEAD_A), lambda b, c: (bsel(b), 0, 0, 0)),
            pl.BlockSpec((4, 3 * W_A), lambda b, c: (0, 0)),
            pl.BlockSpec((1, LANE), lambda b, c: (0, 0)),
            pl.BlockSpec((1, LANE), lambda b, c: (0, 0)),
            pl.BlockSpec((1, HEAD_A), lambda b, c: (0, 0)),
        ],
        out_specs=(pl.BlockSpec((C, W_A), lambda b, c: (b * nch + c, 0)),
                   pl.BlockSpec((None, H_A, HEAD_A, HEAD_A), lambda b, c: (b, 0, 0, 0))),
        scratch_shapes=[pltpu.VMEM((SUB, 3 * W_A), F32)],
        compiler_params=pltpu.CompilerParams(
            dimension_semantics=("parallel", "arbitrary"), vmem_limit_bytes=VMEM_LIMIT),
        name="delta_mixer",
    )(proj, proj, prev, s0, cw, alog_row, dtb_row, onorm)


def _rwkv_kernel(r_ref, k_ref, v_ref, sm_ref, pr_ref, pk_ref, pv_ref, psm_ref, s0_ref,
                 mur_ref, muk_ref, muv_ref, musm_ref, w2_ref, a2_ref, g2_ref,
                 w0_ref, a0_ref, kk_ref, ka_ref, rk_ref, lnw_ref, lnb_ref,
                 o_ref, s_ref, cr_ref, ck_ref, cv_ref, csm_ref, st_ref, *, C, G):
    ci = pl.program_id(2)
    nch = pl.num_programs(2)

    r2 = _iota((LANE, LANE), 0)
    c2 = _iota((LANE, LANE), 1)
    same_head = (r2 < HEAD_B) == (c2 < HEAD_B)
    ones_bd = same_head.astype(F32)
    spread = (_iota((HEAD_B, LANE), 0) == (_iota((HEAD_B, LANE), 1) & (HEAD_B - 1))).astype(F32)
    gather = ((_iota((LANE, HEAD_B), 0) & (HEAD_B - 1)) == _iota((LANE, HEAD_B), 1)).astype(F32)

    @pl.when(ci == 0)
    def _():
        cr_ref[...] = pr_ref[...]
        ck_ref[...] = pk_ref[...]
        cv_ref[...] = pv_ref[...]
        csm_ref[...] = psm_ref[...]
        for p in range(G):
            st_ref[p] = jnp.where(same_head, _xdot(s0_ref[p], spread, 3), 0.0)

    tri = (_iota((C, C), 0) >= _iota((C, C), 1)).astype(F32)

    def lerp(x, prev8, mu):
        return x + (_shifted(x, prev8, 1) - x) * mu

    sm = sm_ref[...]
    xs = lerp(sm, csm_ref[...], musm_ref[...])
    slab_wa = xs[:, SM_WA:SM_WA + LANE]
    slab_g = xs[:, SM_G:SM_G + 2 * LANE]

    pairs = range(G)

    def col(x, p):
        return x[:, p * LANE:(p + 1) * LANE]

    def to_rows(x):
        return jnp.concatenate([col(x, p) for p in pairs], axis=0)

    def to_cols(y):
        return jnp.concatenate([y[p * C:(p + 1) * C] for p in pairs], axis=1)

    xr = lerp(r_ref[...], cr_ref[...], mur_ref[...])
    xk = lerp(k_ref[...], ck_ref[...], muk_ref[...])
    xv = lerp(v_ref[...], cv_ref[...], muv_ref[...])
    wlog = -_softplus(-(w0_ref[...] + _bdot(jnp.tanh(slab_wa), w2_ref[...]))) - 0.5
    ld = -jnp.exp(wlog)
    aa = _sigmoid(a0_ref[...] + _bdot(slab_wa, a2_ref[...]))
    gate = _bdot(_sigmoid(slab_g), g2_ref[...])
    kkr = xk * kk_ref[...]
    kkn = kkr * lax.rsqrt(to_cols(_bdot(to_rows(kkr * kkr), ones_bd)) + 1e-6)
    k2 = xk * (1.0 + (aa - 1.0) * ka_ref[...])
    lp = _xdot_r(tri, ld, 3)
    e_neg = jnp.exp(-lp)
    lp_last = lp[C - 1:C, :]
    e_rem = jnp.exp(lp_last - lp)
    at = -kkn * jnp.exp(lp - ld)
    rt = xr * jnp.exp(lp)
    kb = kkn * aa
    bt = kb * e_neg
    kt = k2 * e_neg
    bhat = kb * e_rem
    khat = k2 * e_rem
    p_last = jnp.exp(lp_last)

    s = [st_ref[p] for p in pairs]
    ar = [jnp.concatenate([col(at, p), col(rt, p)], axis=0) for p in pairs]
    ars = [_bdot_nt(ar[p], s[p]) for p in pairs]
    pk = _Packed(C)
    stack_mask = (_iota((2 * C, LANE), 0) < C) == (_iota((2 * C, LANE), 1) < HEAD_B)

    def by_head(x):
        return jnp.where(stack_mask, jnp.concatenate([x, x], axis=0), 0.0)

    ab = [_bdot_nt(ar[p], by_head(col(bt, p))) for p in pairs]
    ak = [_bdot_nt(ar[p], by_head(col(kt, p))) for p in pairs]
    tinv = pk.inverse_many([jnp.where(pk.strict, m[0:C], 0.0) for m in ab])
    v_bh = [by_head(col(xv, p)) for p in pairs]
    y = [ars[p][0:C] + _bdot(jnp.where(pk.strict, ak[p][0:C], 0.0), v_bh[p]) for p in pairs]
    u = [_bdot(tinv[p], by_head(y[p])) for p in pairs]
    o = [ars[p][C:2 * C] + _bdot(jnp.where(pk.incl, ab[p][C:2 * C], 0.0), by_head(u[p]))
         + _bdot(jnp.where(pk.incl, ak[p][C:2 * C], 0.0), v_bh[p]) for p in pairs]
    for p in pairs:
        uv = jnp.concatenate([u[p], col(xv, p)], axis=0)
        bkhat = jnp.concatenate([col(bhat, p), col(khat, p)], axis=0)
        s_new = s[p] * col(p_last, p) + _bdot_tn(uv, bkhat)
        st_ref[p] = jnp.where(same_head, s_new, 0.0)

    o_rows = jnp.concatenate(o, axis=0)
    mean = _bdot(o_rows, ones_bd) * (1.0 / HEAD_B)
    d = o_rows - mean
    var = _bdot(d * d, ones_bd) * (1.0 / HEAD_B)
    on = to_cols(d * lax.rsqrt(var + GN_EPS)) * lnw_ref[...] + lnb_ref[...]
    bonus = to_cols(_bdot(to_rows(xr * k2 * rk_ref[...]), ones_bd)) * xv
    o_ref[...] = ((on + bonus) * gate).astype(BF16)

    cr_ref[...] = r_ref[C - SUB:C, :]
    ck_ref[...] = k_ref[C - SUB:C, :]
    cv_ref[...] = v_ref[C - SUB:C, :]
    csm_ref[...] = sm_ref[C - SUB:C, :]

    @pl.when(ci == nch - 1)
    def _():
        for p in range(G):
            s_ref[p] = _xdot(st_ref[p], gather, 3)


def _rwkv_mixer(proj, row0, nb, seq, C, G, prev_rkv, prev_sm, s0, mu_rkv, mu_sm, w2p, a2p, g2p,
                w0, a0, k_k, k_a, r_k, lnw, lnb):
    nch = seq // C
    ng = N_PAIR // G
    gw = G * LANE
    blk0 = row0 // C
    bcast = s0.shape[0] == 1
    bsel = (lambda b: 0) if bcast else (lambda b: b)

    def proj_spec(col0):
        return pl.BlockSpec((C, gw), lambda b, g, c: (blk0 + b * nch + c, col0 // gw + g))

    def prev_spec(part):
        return pl.BlockSpec((None, SUB, gw), lambda b, g, c: (bsel(b), 0, part * (W_B // gw) + g))

    def vec_spec(part=0):
        return pl.BlockSpec((1, gw), lambda b, g, c: (0, part * (W_B // gw) + g))

    in_specs = [
        proj_spec(RKV0), proj_spec(RKV0 + W_B), proj_spec(RKV0 + 2 * W_B),
        pl.BlockSpec((C, SM_W), lambda b, g, c: (blk0 + b * nch + c, SM0 // SM_W)),
        prev_spec(0), prev_spec(1), prev_spec(2),
        pl.BlockSpec((None, SUB, SM_W), lambda b, g, c: (bsel(b), 0, 0)),
        pl.BlockSpec((None, G, LANE, HEAD_B), lambda b, g, c: (bsel(b), g, 0, 0)),
        vec_spec(0), vec_spec(1), vec_spec(2),
        pl.BlockSpec((1, SM_W), lambda b, g, c: (0, 0)),
        pl.BlockSpec((LANE, gw), lambda b, g, c: (0, g)),
        pl.BlockSpec((LANE, gw), lambda b, g, c: (0, g)),
        pl.BlockSpec((2 * LANE, gw), lambda b, g, c: (0, g)),
        vec_spec(), vec_spec(), vec_spec(), vec_spec(), vec_spec(), vec_spec(), vec_spec(),
    ]
    return pl.pallas_call(
        functools.partial(_rwkv_kernel, C=C, G=G),
        out_shape=(jax.ShapeDtypeStruct((nb * seq, W_B), BF16),
                   jax.ShapeDtypeStruct((nb, N_PAIR, LANE, HEAD_B), F32)),
        grid=(nb, ng, nch),
        in_specs=in_specs,
        out_specs=(pl.BlockSpec((C, gw), lambda b, g, c: (b * nch + c, g)),
                   pl.BlockSpec((None, G, LANE, HEAD_B), lambda b, g, c: (b, g, 0, 0))),
        scratch_shapes=[pltpu.VMEM((SUB, gw), F32), pltpu.VMEM((SUB, gw), F32), pltpu.VMEM((SUB, gw), F32),
                        pltpu.VMEM((SUB, SM_W), F32), pltpu.VMEM((G, LANE, LANE), F32)],
        compiler_params=pltpu.CompilerParams(
            dimension_semantics=("parallel", "parallel", "arbitrary"), vmem_limit_bytes=VMEM_LIMIT),
        name="rwkv_mixer",
    )(proj, proj, proj, proj, prev_rkv, prev_rkv, prev_rkv, prev_sm, s0,
      mu_rkv, mu_rkv, mu_rkv, mu_sm, w2p, a2p, g2p, w0, a0, k_k, k_a, r_k, lnw, lnb)


def _small_layout(cols_ba, cols_w, cols_a, cols_g, lead):
    def z(n):
        return jnp.zeros(lead + (n,), cols_w.dtype)
    return jnp.concatenate(
        [cols_ba, z(SM_WA - cols_ba.shape[-1]), cols_w, cols_a, cols_g,
         z(SM_W - SM_G - cols_g.shape[-1])], axis=-1)


def _pad_rows8(x):
    b, n, c = x.shape
    return jnp.concatenate([jnp.zeros((b, SUB - n, c), x.dtype), x], axis=1)


def kernel(x_prompt, x_sample, state_delta, state_conv_qkv, state_wkv, state_shift, state_ffn_conv, meta, norm1, w_in, conv_a, a_log, dt_bias, onorm_a, mu_b, w0, w2, a0, a2, g2, k_k, k_a, r_k, lnx_w, lnx_b, w_o, norm2, w_ffn_in, conv_f, w_ffn_out, norm_f):
    nbp, seq_p, _ = x_prompt.shape
    nbs, seq_s, _ = x_sample.shape
    n_s = nbs * seq_s
    assert w_in.shape[0] == 1, "single-layer trunk"
    l = 0

    wi = w_in[l]
    o_b = A_PROJ
    o_l = A_PROJ + 3 * W_B
    w_cat = jnp.concatenate([
        wi[:, :4 * W_A], wi[:, o_b:o_l],
        _small_layout(wi[:, 4 * W_A:A_PROJ], wi[:, o_l:o_l + W_LORA],
                      wi[:, o_l + W_LORA:o_l + W_LORA + A_LORA], wi[:, o_l + W_LORA + A_LORA:],
                      (D_MODEL,))], axis=1).astype(BF16)
    mu = mu_b[l]
    mu_rkv = mu[None, :3 * W_B]
    mu_sm = _small_layout(jnp.zeros((1, 2 * H_A), F32), mu[None, 3 * W_B:3 * W_B + W_LORA],
                          mu[None, 3 * W_B + W_LORA:3 * W_B + W_LORA + A_LORA],
                          mu[None, 3 * W_B + W_LORA + A_LORA:], (1,))
    w2p = jnp.concatenate([w2[l], jnp.zeros((LANE - W_LORA, W_B), F32)], axis=0)
    a2p = jnp.concatenate([jnp.zeros((W_LORA, W_B), F32), a2[l]], axis=0)
    g2p = jnp.concatenate([g2[l], jnp.zeros((2 * LANE - G_LORA, W_B), F32)], axis=0)
    alog_row = jnp.concatenate([jnp.zeros((H_A,), F32), a_log[l], jnp.zeros((LANE - 2 * H_A,), F32)])[None]
    dtb_row = jnp.concatenate([jnp.zeros((H_A,), F32), dt_bias[l], jnp.zeros((LANE - 2 * H_A,), F32)])[None]
    wo_bf = w_o[l].astype(BF16)
    wfo_bf = w_ffn_out[l].astype(BF16)
    row = lambda v: v.reshape(1, -1)

    def mix(proj, row0, nb, seq, C, prev_qkv, prev_rkv, prev_sm, s_delta, s_wkv):
        oa, sd = _delta_mixer(proj, row0, nb, seq, C, prev_qkv, s_delta, conv_a[l], alog_row, dtb_row,
                              row(onorm_a[l]))
        ob, sw = _rwkv_mixer(proj, row0, nb, seq, C, N_PAIR, prev_rkv, prev_sm, s_wkv, mu_rkv, mu_sm,
                             w2p, a2p, g2p, row(w0[l]), row(a0[l]), row(k_k[l]), row(k_a[l]),
                             row(r_k[l]), row(lnx_w[l]), row(lnx_b[l]))
        return oa, ob, sd, sw

    xs_rows = jnp.concatenate([x_sample.reshape(n_s, D_MODEL), meta], axis=0)
    xp_rows = x_prompt.reshape(nbp * seq_p, D_MODEL)
    n_small = n_s + N_META
    proj_s = _norm_matmul(xs_rows, row(norm1[l]), w_cat, n_small, 512)
    proj_p = _norm_matmul(xp_rows, row(norm1[l]), w_cat, 1024, 512)

    zeros = lambda *s: jnp.zeros(s, F32)
    oa_m, ob_m, sd_m, sw_m = mix(proj_s, n_s, 1, N_META, N_META, zeros(1, SUB, 3 * W_A),
                                 zeros(1, SUB, 3 * W_B), zeros(1, SUB, SM_W),
                                 zeros(1, H_A, HEAD_A, HEAD_A), zeros(1, N_PAIR, LANE, HEAD_B))
    tail = proj_s[n_small - SUB:n_small]
    oa_p, ob_p, sd_p, sw_p = mix(proj_p, 0, nbp, seq_p, 64, tail[None, :, :3 * W_A],
                                 tail[None, :, RKV0:SM0], tail[None, :, SM0:], sd_m, sw_m)
    sh = state_shift[l]
    sh_sm = _small_layout(jnp.zeros((nbs, 1, 2 * H_A), F32), sh[..., 3 * W_B:3 * W_B + W_LORA],
                          sh[..., 3 * W_B + W_LORA:3 * W_B + W_LORA + A_LORA],
                          sh[..., 3 * W_B + W_LORA + A_LORA:], (nbs, 1))
    oa_s, ob_s, sd_s, sw_s = mix(proj_s, 0, nbs, seq_s, seq_s, _pad_rows8(state_conv_qkv[l]),
                                 _pad_rows8(sh[..., :3 * W_B]), _pad_rows8(sh_sm), state_delta[l],
                                 state_wkv[l].reshape(nbs, N_PAIR, LANE, HEAD_B))

    oa_small = jnp.concatenate([oa_s, oa_m], axis=0)
    ob_small = jnp.concatenate([ob_s, ob_m], axis=0)
    x1_s, h2_s = _out_proj(xs_rows, oa_small, ob_small, wo_bf, row(norm2[l]), n_small // 5)
    x1_p, h2_p = _out_proj(xp_rows, oa_p, ob_p, wo_bf, row(norm2[l]), 512)
    prev_rows = jnp.concatenate([_pad_rows8(state_ffn_conv[l]).reshape(n_s, D_FF),
                                 jnp.zeros((N_META, D_FF), F32)], axis=0)
    act_s, gate_s = _ffn_in(h2_s, w_ffn_in[l], prev_rows, conv_f[l], n_small, 768, 3, 0, n_s + SUB)
    act_p, tail_p = _ffn_in(h2_p, w_ffn_in[l], gate_s[None, n_small - SUB:n_small], conv_f[l], 1024, 768, 3,
                            seq_p // 1024)
    y_s = _ffn_out(act_s, wfo_bf, x1_s, row(norm_f), n_s, 512, 768)
    y_p = _ffn_out(act_p, wfo_bf, x1_p, row(norm_f), nbp * seq_p, 512, 768)

    def states(proj, gate_tail, nb, seq, sd, sw):
        p3 = proj.reshape(-1, seq, P_CAT)
        conv_new = p3[:nb, seq - 3:, :3 * W_A]
        last = p3[:nb, seq - 1:, :]
        shift_new = jnp.concatenate([last[..., RKV0:SM0], last[..., SM0 + SM_WA:SM0 + SM_WA + W_LORA + A_LORA],
                                     last[..., SM0 + SM_G:SM0 + SM_G + G_LORA]], axis=-1)
        ffn_new = gate_tail[:nb, SUB - 2:, :]
        return (sd[None], conv_new[None], sw.reshape(nb, H_B, HEAD_B, HEAD_B)[None], shift_new[None],
                ffn_new[None])

    return ((y_p.reshape(nbp, seq_p, D_MODEL), y_s.reshape(nbs, seq_s, D_MODEL))
            + states(proj_p, tail_p, nbp, seq_p, sd_p, sw_p)
            + states(proj_s, gate_s.reshape(-1, SUB, D_FF), nbs, seq_s, sd_s, sw_s))
```

```python
import functools

import jax
import jax.numpy as jnp
from jax import lax
from jax.experimental import pallas as pl
from jax.experimental.pallas import tpu as pltpu

F32 = jnp.float32
BF16 = jnp.bfloat16

D_MODEL = 2048
N_META = 16
W_A = 1024
HEAD_A = 128
H_A = 8
W_B = 1024
HEAD_B = 64
H_B = 16
N_PAIR = H_B // 2
W_LORA = 64
A_LORA = 64
G_LORA = 160
D_FF = 5376
RMS_EPS = 1e-6
GN_EPS = 64e-5
A_PROJ = 4 * W_A + 2 * H_A
B_PROJ = 3 * W_B + W_LORA + A_LORA + G_LORA

QKVZ0 = 0
RKV0 = 4 * W_A
SM0 = RKV0 + 3 * W_B
SM_W = 512
SM_BA = 0
SM_WA = 128
SM_G = 256
P_CAT = SM0 + SM_W

LANE = 128
SUB = 8
VMEM_LIMIT = 48 * 1024 * 1024
VMEM_LIMIT_BIG = 58 * 1024 * 1024
NEG_BIG = -1e30

NT_DIMS = (((1,), (1,)), ((), ()))


def _bdot(a, b):
    return jnp.dot(a.astype(BF16), b.astype(BF16), preferred_element_type=F32)


def _bdot_nt(a, b):
    return lax.dot_general(a.astype(BF16), b.astype(BF16), NT_DIMS, preferred_element_type=F32)


def _bdot_tn(a, b):
    return lax.dot_general(a.astype(BF16), b.astype(BF16), (((0,), (0,)), ((), ())),
                           preferred_element_type=F32)


def _pieces(a, n):
    out = []
    rem = a
    for i in range(n):
        p = rem.astype(BF16)
        out.append(p)
        if i + 1 < n:
            rem = rem - p.astype(F32)
    return out


def _xdot(a, b, n, dims=(((1,), (0,)), ((), ()))):
    bb = b.astype(BF16)
    acc = None
    for p in _pieces(a, n):
        t = lax.dot_general(p, bb, dims, preferred_element_type=F32)
        acc = t if acc is None else acc + t
    return acc


def _xdot_r(a, b, n, dims=(((1,), (0,)), ((), ()))):
    ab = a.astype(BF16)
    acc = None
    for p in _pieces(b, n):
        t = lax.dot_general(ab, p, dims, preferred_element_type=F32)
        acc = t if acc is None else acc + t
    return acc


def _sigmoid(x):
    return 1.0 / (1.0 + jnp.exp(-x))


def _silu(x):
    return x * _sigmoid(x)


def _softplus(x):
    return jnp.maximum(x, 0.0) + jnp.log(1.0 + jnp.exp(-jnp.abs(x)))


def _iota(shape, dim):
    return lax.broadcasted_iota(jnp.int32, shape, dim)


def _shifted(x, prev8, k):
    n = x.shape[0]
    xr = pltpu.roll(x, k, 0)
    pr = pltpu.roll(prev8, k, 0)
    first = jnp.where(_iota((SUB, x.shape[1]), 0) < k, pr, xr[0:SUB])
    if n == SUB:
        return first
    return jnp.concatenate([first, xr[SUB:]], axis=0)


def _delayed(x, hist, k, nseq):
    if nseq == 1:
        return _shifted(x, hist, k)
    n = x.shape[0]
    use_hist = (_iota(x.shape, 0) & (SUB - 1)) < k
    return jnp.where(use_hist, pltpu.roll(hist, n + k - SUB, 0), pltpu.roll(x, k, 0))


def _same_seq(shape, c, rows_total):
    if c == rows_total:
        return None
    sh = c.bit_length() - 1
    return (lax.shift_right_logical(_iota(shape, 0), sh)
            == lax.shift_right_logical(_iota(shape, 1) & (rows_total - 1), sh))


class _Packed:
    def __init__(self, r, c):
        self.r = r
        self.c = c
        row = _iota((r, 2 * r), 0)
        col = _iota((r, 2 * r), 1) & (r - 1)
        same = _same_seq((r, 2 * r), c, r)
        self.incl = (row >= col) if same is None else ((row >= col) & same)
        self.strict = (row > col) if same is None else ((row > col) & same)
        self.eye = (row == col).astype(F32)
        self.first = _iota((r, 2 * r), 1) < r
        self.bd_mask = (_iota((2 * r, 2 * r), 0) < r) == (_iota((2 * r, 2 * r), 1) < r)

    def block_diag(self, m):
        return jnp.where(self.bd_mask, jnp.concatenate([m, m], axis=0), 0.0)

    def inverse_many(self, xs):
        ts = [self.eye + x for x in xs]
        ps = list(xs)
        n = 1
        while 2 * n < self.c:
            ps = [_bdot(p, self.block_diag(p)) for p in ps]
            ts = [t + _bdot(p, self.block_diag(t)) for p, t in zip(ps, ts)]
            n *= 2
        return ts


def _seq_masks(r, c):
    row = _iota((r, r), 0)
    col = _iota((r, r), 1)
    same = _same_seq((r, r), c, r)
    if same is None:
        return (row >= col).astype(F32), None
    return ((row >= col) & same).astype(F32), same.astype(F32)


def _norm_matmul_kernel(x_ref, g_ref, w_ref, o_ref, h_ref):
    @pl.when(pl.program_id(1) == 0)
    def _():
        x = x_ref[...]
        ms = jnp.mean(x * x, axis=-1, keepdims=True)
        h_ref[...] = (x * lax.rsqrt(ms + RMS_EPS) * g_ref[...]).astype(BF16)

    o_ref[...] = lax.dot_general(h_ref[...], w_ref[...], NT_DIMS, preferred_element_type=F32)


def _norm_matmul(x, g, wt, tm, tn):
    m, k = x.shape
    n = wt.shape[0]
    return pl.pallas_call(
        _norm_matmul_kernel,
        out_shape=jax.ShapeDtypeStruct((m, n), F32),
        grid=(m // tm, n // tn),
        in_specs=[
            pl.BlockSpec((tm, k), lambda i, j: (i, 0)),
            pl.BlockSpec((1, k), lambda i, j: (0, 0)),
            pl.BlockSpec((tn, k), lambda i, j: (j, 0)),
        ],
        out_specs=pl.BlockSpec((tm, tn), lambda i, j: (i, j)),
        scratch_shapes=[pltpu.VMEM((tm, k), BF16)],
        compiler_params=pltpu.CompilerParams(
            dimension_semantics=("parallel", "arbitrary"), vmem_limit_bytes=VMEM_LIMIT),
        name="norm_matmul",
    )(x, g, wt)


def _out_proj_kernel(x_ref, oa_ref, ob_ref, wt_ref, wb_ref, g_ref, x1_ref, h_ref):
    acc = jnp.dot(oa_ref[...], wt_ref[...], preferred_element_type=F32)
    acc = acc + jnp.dot(ob_ref[...], wb_ref[...], preferred_element_type=F32)
    x1 = x_ref[...] + acc
    x1_ref[...] = x1
    ms = jnp.mean(x1 * x1, axis=-1, keepdims=True)
    h_ref[...] = (x1 * lax.rsqrt(ms + RMS_EPS) * g_ref[...]).astype(BF16)


def _out_proj(x, oa, ob, wo, g, tm):
    m, d = x.shape
    return pl.pallas_call(
        _out_proj_kernel,
        out_shape=(jax.ShapeDtypeStruct((m, d), F32), jax.ShapeDtypeStruct((m, d), BF16)),
        grid=(m // tm,),
        in_specs=[
            pl.BlockSpec((tm, d), lambda i: (i, 0)),
            pl.BlockSpec((tm, W_A), lambda i: (i, 0)),
            pl.BlockSpec((tm, W_B), lambda i: (i, 0)),
            pl.BlockSpec((W_A, d), lambda i: (0, 0)),
            pl.BlockSpec((W_B, d), lambda i: (W_A // W_B, 0)),
            pl.BlockSpec((1, d), lambda i: (0, 0)),
        ],
        out_specs=(pl.BlockSpec((tm, d), lambda i: (i, 0)), pl.BlockSpec((tm, d), lambda i: (i, 0))),
        compiler_params=pltpu.CompilerParams(
            dimension_semantics=("parallel",), vmem_limit_bytes=VMEM_LIMIT),
        name="out_proj",
    )(x, oa, ob, wo, wo, g)


def _ffn_in_kernel(h_ref, wg_ref, wu_ref, prev_ref, cw_ref, act_ref, tail_ref, wgb_ref, wub_ref, carry_ref,
                   *, tiles_per_seq, state_rows, nsub):
    m = pl.program_id(1)

    @pl.when(m == 0)
    def _():
        wgb_ref[...] = wg_ref[...].astype(BF16)
        wub_ref[...] = wu_ref[...].astype(BF16)

    if tiles_per_seq:
        @pl.when(m % tiles_per_seq == 0)
        def _():
            carry_ref[...] = prev_ref[...]

    h = h_ref[...]
    n = h.shape[0]
    w = wgb_ref.shape[1] // nsub
    for i in range(nsub):
        cs = slice(i * w, (i + 1) * w)
        gate = jnp.dot(h, wgb_ref[:, cs], preferred_element_type=F32)
        up = jnp.dot(h, wub_ref[:, cs], preferred_element_type=F32)
        if tiles_per_seq:
            prev8 = carry_ref[:, cs]
            d1 = _shifted(gate, prev8, 1)
            d2 = _shifted(gate, prev8, 2)
            carry_ref[:, cs] = gate[n - SUB:n]
            tail_ref[:, cs] = gate[n - SUB:n]
        else:
            row = _iota(gate.shape, 0)
            prev = prev_ref[:, cs]
            tail_ref[:, cs] = gate

            def delayed(k):
                use_prev = ((row & (SUB - 1)) < k) & (row < state_rows)
                return jnp.where(use_prev, pltpu.roll(prev, n + k - SUB, 0), pltpu.roll(gate, k, 0))

            d1 = delayed(1)
            d2 = delayed(2)
        y = gate * cw_ref[2:3, cs] + d1 * cw_ref[1:2, cs] + d2 * cw_ref[0:1, cs]
        act_ref[:, cs] = (_silu(y) * up).astype(BF16)


def _ffn_in(h, w, prev, cw, tm, tn, nsub, tiles_per_seq, state_rows=0):
    m, d = h.shape
    nj = D_FF // tn
    nm = m // tm
    if tiles_per_seq:
        nseq = nm // tiles_per_seq
        prev_spec = pl.BlockSpec((None, SUB, tn), lambda j, i: (0, 0, j))
        tail_shape = jax.ShapeDtypeStruct((nseq, SUB, D_FF), F32)
        tail_spec = pl.BlockSpec((None, SUB, tn), lambda j, i: (i // tiles_per_seq, 0, j))
    else:
        prev_spec = pl.BlockSpec((tm, tn), lambda j, i: (i, j))
        tail_shape = jax.ShapeDtypeStruct((m, D_FF), F32)
        tail_spec = pl.BlockSpec((tm, tn), lambda j, i: (i, j))
    return pl.pallas_call(
        functools.partial(_ffn_in_kernel, tiles_per_seq=tiles_per_seq, state_rows=state_rows, nsub=nsub),
        out_shape=(jax.ShapeDtypeStruct((m, D_FF), BF16), tail_shape),
        grid=(nj, nm),
        in_specs=[
            pl.BlockSpec((tm, d), lambda j, i: (i, 0)),
            pl.BlockSpec((d, tn), lambda j, i: (0, j)),
            pl.BlockSpec((d, tn), lambda j, i: (0, nj + j)),
            prev_spec,
            pl.BlockSpec((3, tn), lambda j, i: (0, j)),
        ],
        out_specs=(pl.BlockSpec((tm, tn), lambda j, i: (i, j)), tail_spec),
        scratch_shapes=[pltpu.VMEM((d, tn), BF16), pltpu.VMEM((d, tn), BF16), pltpu.VMEM((SUB, tn), F32)],
        compiler_params=pltpu.CompilerParams(
            dimension_semantics=("parallel", "arbitrary"), vmem_limit_bytes=VMEM_LIMIT_BIG),
        name="ffn_in",
    )(h, w, w, prev, cw)


def _ffn_out_kernel(a_ref, w_ref, x_ref, g_ref, o_ref, acc_ref):
    kk = pl.program_id(1)

    @pl.when(kk == 0)
    def _():
        acc_ref[...] = x_ref[...]

    acc_ref[...] += jnp.dot(a_ref[...], w_ref[...], preferred_element_type=F32)

    @pl.when(kk == pl.num_programs(1) - 1)
    def _():
        x = acc_ref[...]
        ms = jnp.mean(x * x, axis=-1, keepdims=True)
        o_ref[...] = x * lax.rsqrt(ms + RMS_EPS) * g_ref[...]


def _ffn_out(act, w, x1, g, m, tm, tk):
    kdim, d = w.shape
    return pl.pallas_call(
        _ffn_out_kernel,
        out_shape=jax.ShapeDtypeStruct((m, d), F32),
        grid=(m // tm, kdim // tk),
        in_specs=[
            pl.BlockSpec((tm, tk), lambda i, k: (i, k)),
            pl.BlockSpec((tk, d), lambda i, k: (k, 0)),
            pl.BlockSpec((tm, d), lambda i, k: (i, 0)),
            pl.BlockSpec((1, d), lambda i, k: (0, 0)),
        ],
        out_specs=pl.BlockSpec((tm, d), lambda i, k: (i, 0)),
        scratch_shapes=[pltpu.VMEM((tm, d), F32)],
        compiler_params=pltpu.CompilerParams(
            dimension_semantics=("parallel", "arbitrary"), vmem_limit_bytes=VMEM_LIMIT),
        name="ffn_out",
    )(act, w, x1, g)


def _delta_kernel(qkvz_ref, ba_ref, prev_ref, s0_ref, cw_ref, alog_ref, dtb_ref, on_ref,
                  o_ref, s_ref, carry_ref, *, C, NB):
    R = NB * C
    ci = pl.program_id(1)

    @pl.when(ci == 0)
    def _():
        if NB == 1:
            carry_ref[...] = prev_ref[...]
        s_ref[...] = s0_ref[...]

    hist_ref = carry_ref if NB == 1 else prev_ref
    tri, ones_seq = _seq_masks(R, C)

    ba = ba_ref[...]
    beta_full = _sigmoid(ba)
    g_full = -jnp.exp(alog_ref[...]) * _softplus(ba + dtb_ref[...])
    gc_full = _xdot_r(tri, g_full, 3)
    gtot_full = gc_full[C - 1:C, :] if NB == 1 else _xdot_r(ones_seq, g_full, 3)

    def conv_silu(c0):
        x = qkvz_ref[:, c0:c0 + LANE]
        hist = hist_ref[:, c0:c0 + LANE]
        y = x * cw_ref[3:4, c0:c0 + LANE]
        for k in (1, 2, 3):
            y = y + _delayed(x, hist, k, NB) * cw_ref[3 - k:4 - k, c0:c0 + LANE]
        return _silu(y)

    def l2n(x):
        return x * lax.rsqrt(jnp.sum(x * x, axis=-1, keepdims=True) + 1e-6)

    def seq_rows(x, n):
        return x[n * C:(n + 1) * C]

    heads = range(H_A)
    seqs = range(NB)
    q = [l2n(conv_silu(h * HEAD_A)) * (HEAD_A ** -0.5) for h in heads]
    k = [l2n(conv_silu(W_A + h * HEAD_A)) for h in heads]
    v = [conv_silu(2 * W_A + h * HEAD_A) for h in heads]
    bcol = [beta_full[:, h:h + 1] for h in heads]
    gcol = [gc_full[:, H_A + h:H_A + h + 1] for h in heads]
    gtot = [gtot_full[:, H_A + h:H_A + h + 1] for h in heads]
    eg = [jnp.exp(gcol[h]) for h in heads]
    kb = [k[h] * bcol[h] for h in heads]

    pk = _Packed(R, C)
    hpairs = range(H_A // 2)
    zc = jnp.zeros((R, HEAD_A), F32)
    row2 = _iota((2 * R, LANE), 0)
    lane2 = _iota((2 * R, LANE), 1)
    ones_cl = jnp.ones((R, LANE), F32)
    gc2 = jnp.concatenate([gc_full, gc_full], axis=0)
    kq, gamma = [], []
    for hp in hpairs:
        h0, h1 = 2 * hp, 2 * hp + 1
        lhs = jnp.concatenate([jnp.concatenate([kb[h0], kb[h1]], axis=1),
                               jnp.concatenate([q[h0], q[h1]], axis=1)], axis=0)
        rk = jnp.concatenate([jnp.concatenate([k[h0], zc], axis=1),
                              jnp.concatenate([zc, k[h1]], axis=1)], axis=0)
        kq.append(_bdot_nt(lhs, rk))
        sel = lane2 == jnp.where(row2 < R, H_A + h0, H_A + h1)
        grow = _xdot_r(ones_cl, jnp.where(sel, gc2, 0.0), 3, NT_DIMS)
        gcol_p = jnp.where(pk.first, gcol[h0], gcol[h1])
        gamma.append(jnp.exp(jnp.where(pk.incl, gcol_p - grow, NEG_BIG)))
    tinv = pk.inverse_many([-jnp.where(pk.strict, kq[hp][0:R] * gamma[hp], 0.0) for hp in hpairs])
    uw = []
    for hp in hpairs:
        h0, h1 = 2 * hp, 2 * hp + 1
        z2 = jnp.zeros((R, 2 * HEAD_A), F32)
        rhs = jnp.concatenate([jnp.concatenate([v[h0] * bcol[h0], kb[h0] * eg[h0], z2], axis=1),
                               jnp.concatenate([z2, v[h1] * bcol[h1], kb[h1] * eg[h1]], axis=1)], axis=0)
        uw.append(_bdot(tinv[hp], rhs))
    u = [uw[h // 2][:, (h % 2) * 2 * HEAD_A:(h % 2) * 2 * HEAD_A + HEAD_A] for h in heads]
    w = [uw[h // 2][:, (h % 2) * 2 * HEAD_A + HEAD_A:(h % 2 + 1) * 2 * HEAD_A] for h in heads]
    qd = [q[h] * eg[h] for h in heads]
    s = [[s_ref[n, h] for h in heads] for n in seqs]
    wqs = [[_bdot(jnp.concatenate([seq_rows(w[h], n), seq_rows(qd[h], n)], axis=0), s[n][h]) for h in heads]
           for n in seqs]
    ws = [jnp.concatenate([wqs[n][h][0:C] for n in seqs], axis=0) if NB > 1 else wqs[0][h][0:C] for h in heads]
    qs = [jnp.concatenate([wqs[n][h][C:2 * C] for n in seqs], axis=0) if NB > 1 else wqs[0][h][C:2 * C]
          for h in heads]
    v_new = [u[h] - ws[h] for h in heads]
    o = []
    for hp in hpairs:
        h0, h1 = 2 * hp, 2 * hp + 1
        vn_bd = jnp.concatenate([jnp.concatenate([v_new[h0], zc], axis=1),
                                 jnp.concatenate([zc, v_new[h1]], axis=1)], axis=0)
        op = _bdot(kq[hp][R:2 * R] * gamma[hp], vn_bd)
        o.append(qs[h0] + op[:, :HEAD_A])
        o.append(qs[h1] + op[:, HEAD_A:])
    kd = [k[h] * jnp.exp(gtot[h] - gcol[h]) for h in heads]
    for n in seqs:
        for h in heads:
            glast = gtot[h] if NB == 1 else gtot[h][n * C:n * C + 1]
            s_ref[n, h] = s[n][h] * jnp.exp(glast) + _bdot_tn(seq_rows(kd[h], n), seq_rows(v_new[h], n))
    for h in heads:
        z = qkvz_ref[:, 3 * W_A + h * HEAD_A:3 * W_A + (h + 1) * HEAD_A]
        oh = o[h] * lax.rsqrt(jnp.mean(o[h] * o[h], axis=-1, keepdims=True) + RMS_EPS)
        o_ref[:, h * HEAD_A:(h + 1) * HEAD_A] = (oh * on_ref[...] * _silu(z)).astype(BF16)

    if NB == 1:
        carry_ref[...] = qkvz_ref[C - SUB:C, 0:3 * W_A]


def _delta_mixer(proj, row0, nb, seq, C, NB, prev, s0, cw, alog_row, dtb_row, onorm):
    nch = seq // C
    assert NB == 1 or (nch == 1 and C == SUB)
    R = NB * C
    blk0 = row0 // R
    bcast = s0.shape[0] == 1
    bsel = (lambda b: 0) if bcast else (lambda b: b)
    return pl.pallas_call(
        functools.partial(_delta_kernel, C=C, NB=NB),
        out_shape=(jax.ShapeDtypeStruct((nb * seq, W_A), BF16),
                   jax.ShapeDtypeStruct((nb, H_A, HEAD_A, HEAD_A), F32)),
        grid=(nb // NB, nch),
        in_specs=[
            pl.BlockSpec((R, 4 * W_A), lambda b, c: (blk0 + b * nch + c, 0)),
            pl.BlockSpec((R, LANE), lambda b, c: (blk0 + b * nch + c, SM0 // LANE)),
            pl.BlockSpec((NB * SUB, 3 * W_A), lambda b, c: (bsel(b), 0)),
            pl.BlockSpec((NB, H_A, HEAD_A, HEAD_A), lambda b, c: (bsel(b), 0, 0, 0)),
            pl.BlockSpec((4, 3 * W_A), lambda b, c: (0, 0)),
            pl.BlockSpec((1, LANE), lambda b, c: (0, 0)),
            pl.BlockSpec((1, LANE), lambda b, c: (0, 0)),
            pl.BlockSpec((1, HEAD_A), lambda b, c: (0, 0)),
        ],
        out_specs=(pl.BlockSpec((R, W_A), lambda b, c: (b * nch + c, 0)),
                   pl.BlockSpec((NB, H_A, HEAD_A, HEAD_A), lambda b, c: (b, 0, 0, 0))),
        scratch_shapes=[pltpu.VMEM((SUB, 3 * W_A), F32)],
        compiler_params=pltpu.CompilerParams(
            dimension_semantics=("parallel", "arbitrary"), vmem_limit_bytes=VMEM_LIMIT),
        name="delta_mixer",
    )(proj, proj, prev, s0, cw, alog_row, dtb_row, onorm)


def _rwkv_kernel(r_ref, k_ref, v_ref, sm_ref, pr_ref, pk_ref, pv_ref, psm_ref, s0_ref,
                 mur_ref, muk_ref, muv_ref, musm_ref, w2_ref, a2_ref, g2_ref,
                 w0_ref, a0_ref, kk_ref, ka_ref, rk_ref, lnw_ref, lnb_ref,
                 o_ref, s_ref, cr_ref, ck_ref, cv_ref, csm_ref, st_ref, *, C, NB, G):
    R = NB * C
    ci = pl.program_id(2)
    nch = pl.num_programs(2)
    seqs = range(NB)
    pairs = range(G)

    r2 = _iota((LANE, LANE), 0)
    c2 = _iota((LANE, LANE), 1)
    same_head = (r2 < HEAD_B) == (c2 < HEAD_B)
    ones_bd = same_head.astype(F32)
    spread = (_iota((HEAD_B, LANE), 0) == (_iota((HEAD_B, LANE), 1) & (HEAD_B - 1))).astype(F32)
    gather = ((_iota((LANE, HEAD_B), 0) & (HEAD_B - 1)) == _iota((LANE, HEAD_B), 1)).astype(F32)

    @pl.when(ci == 0)
    def _():
        if NB == 1:
            cr_ref[...] = pr_ref[...]
            ck_ref[...] = pk_ref[...]
            cv_ref[...] = pv_ref[...]
            csm_ref[...] = psm_ref[...]
        for n in seqs:
            for p in pairs:
                st_ref[n * G + p] = jnp.where(same_head, _xdot(s0_ref[n, p], spread, 3), 0.0)

    hr_ref, hk_ref, hv_ref, hsm_ref = ((cr_ref, ck_ref, cv_ref, csm_ref) if NB == 1
                                       else (pr_ref, pk_ref, pv_ref, psm_ref))
    tri, ones_seq = _seq_masks(R, C)

    def lerp(x, hist, mu):
        return x + (_delayed(x, hist, 1, NB) - x) * mu

    sm = sm_ref[...]
    xs = lerp(sm, hsm_ref[...], musm_ref[...])
    slab_wa = xs[:, SM_WA:SM_WA + LANE]
    slab_g = xs[:, SM_G:SM_G + 2 * LANE]

    def col(x, p):
        return x[:, p * LANE:(p + 1) * LANE]

    def to_rows(x):
        return jnp.concatenate([col(x, p) for p in pairs], axis=0)

    def to_cols(y):
        return jnp.concatenate([y[p * R:(p + 1) * R] for p in pairs], axis=1)

    def seq_rows(x, n):
        return x[n * C:(n + 1) * C]

    xr = lerp(r_ref[...], hr_ref[...], mur_ref[...])
    xk = lerp(k_ref[...], hk_ref[...], muk_ref[...])
    xv = lerp(v_ref[...], hv_ref[...], muv_ref[...])
    wlog = -_softplus(-(w0_ref[...] + _bdot(jnp.tanh(slab_wa), w2_ref[...]))) - 0.5
    ld = -jnp.exp(wlog)
    aa = _sigmoid(a0_ref[...] + _bdot(slab_wa, a2_ref[...]))
    gate = _bdot(_sigmoid(slab_g), g2_ref[...])
    kkr = xk * kk_ref[...]
    kkn = kkr * lax.rsqrt(to_cols(_bdot(to_rows(kkr * kkr), ones_bd)) + 1e-6)
    k2 = xk * (1.0 + (aa - 1.0) * ka_ref[...])
    lp = _xdot_r(tri, ld, 3)
    lp_tot = lp[C - 1:C, :] if NB == 1 else _xdot_r(ones_seq, ld, 3)
    e_neg = jnp.exp(-lp)
    e_rem = jnp.exp(lp_tot - lp)
    at = -kkn * jnp.exp(lp - ld)
    rt = xr * jnp.exp(lp)
    kb = kkn * aa
    bt = kb * e_neg
    kt = k2 * e_neg
    bhat = kb * e_rem
    khat = k2 * e_rem
    p_tot = jnp.exp(lp_tot)

    s = [[st_ref[n * G + p] for p in pairs] for n in seqs]
    atp = [col(at, p) for p in pairs]
    rtp = [col(rt, p) for p in pairs]
    ars = [[_bdot_nt(jnp.concatenate([seq_rows(atp[p], n), seq_rows(rtp[p], n)], axis=0), s[n][p])
            for p in pairs] for n in seqs]
    if NB == 1:
        x_state = [ars[0][p][0:C] for p in pairs]
        o_state = [ars[0][p][C:2 * C] for p in pairs]
    else:
        x_state = [jnp.concatenate([ars[n][p][0:C] for n in seqs], axis=0) for p in pairs]
        o_state = [jnp.concatenate([ars[n][p][C:2 * C] for n in seqs], axis=0) for p in pairs]
    pk = _Packed(R, C)
    stack_mask = (_iota((2 * R, LANE), 0) < R) == (_iota((2 * R, LANE), 1) < HEAD_B)

    def by_head(x):
        return jnp.where(stack_mask, jnp.concatenate([x, x], axis=0), 0.0)

    ar = [jnp.concatenate([atp[p], rtp[p]], axis=0) for p in pairs]
    ab = [_bdot_nt(ar[p], by_head(col(bt, p))) for p in pairs]
    ak = [_bdot_nt(ar[p], by_head(col(kt, p))) for p in pairs]
    tinv = pk.inverse_many([jnp.where(pk.strict, m[0:R], 0.0) for m in ab])
    v_bh = [by_head(col(xv, p)) for p in pairs]
    y = [x_state[p] + _bdot(jnp.where(pk.strict, ak[p][0:R], 0.0), v_bh[p]) for p in pairs]
    u = [_bdot(tinv[p], by_head(y[p])) for p in pairs]
    o = [o_state[p] + _bdot(jnp.where(pk.incl, ab[p][R:2 * R], 0.0), by_head(u[p]))
         + _bdot(jnp.where(pk.incl, ak[p][R:2 * R], 0.0), v_bh[p]) for p in pairs]
    for n in seqs:
        for p in pairs:
            uv = jnp.concatenate([seq_rows(u[p], n), seq_rows(col(xv, p), n)], axis=0)
            bkhat = jnp.concatenate([seq_rows(col(bhat, p), n), seq_rows(col(khat, p), n)], axis=0)
            decay = col(p_tot, p) if NB == 1 else col(p_tot, p)[n * C:n * C + 1]
            s_new = s[n][p] * decay + _bdot_tn(uv, bkhat)
            st_ref[n * G + p] = jnp.where(same_head, s_new, 0.0)

    o_rows = jnp.concatenate(o, axis=0)
    mean = _bdot(o_rows, ones_bd) * (1.0 / HEAD_B)
    d = o_rows - mean
    var = _bdot(d * d, ones_bd) * (1.0 / HEAD_B)
    on = to_cols(d * lax.rsqrt(var + GN_EPS)) * lnw_ref[...] + lnb_ref[...]
    bonus = to_cols(_bdot(to_rows(xr * k2 * rk_ref[...]), ones_bd)) * xv
    o_ref[...] = ((on + bonus) * gate).astype(BF16)

    if NB == 1:
        cr_ref[...] = r_ref[C - SUB:C, :]
        ck_ref[...] = k_ref[C - SUB:C, :]
        cv_ref[...] = v_ref[C - SUB:C, :]
        csm_ref[...] = sm_ref[C - SUB:C, :]

    @pl.when(ci == nch - 1)
    def _():
        for n in seqs:
            for p in pairs:
                s_ref[n, p] = _xdot(st_ref[n * G + p], gather, 3)


def _rwkv_mixer(proj, row0, nb, seq, C, NB, G, prev_rkv, prev_sm, s0, mu_rkv, mu_sm, w2p, a2p, g2p,
                w0, a0, k_k, k_a, r_k, lnw, lnb):
    nch = seq // C
    assert NB == 1 or (nch == 1 and C == SUB)
    R = NB * C
    ng = N_PAIR // G
    gw = G * LANE
    blk0 = row0 // R
    bcast = s0.shape[0] == 1
    bsel = (lambda b: 0) if bcast else (lambda b: b)

    def proj_spec(col0):
        return pl.BlockSpec((R, gw), lambda b, g, c: (blk0 + b * nch + c, col0 // gw + g))

    def prev_spec(part):
        return pl.BlockSpec((NB * SUB, gw), lambda b, g, c: (bsel(b), part * (W_B // gw) + g))

    def vec_spec(part=0):
        return pl.BlockSpec((1, gw), lambda b, g, c: (0, part * (W_B // gw) + g))

    in_specs = [
        proj_spec(RKV0), proj_spec(RKV0 + W_B), proj_spec(RKV0 + 2 * W_B),
        pl.BlockSpec((R, SM_W), lambda b, g, c: (blk0 + b * nch + c, SM0 // SM_W)),
        prev_spec(0), prev_spec(1), prev_spec(2),
        pl.BlockSpec((NB * SUB, SM_W), lambda b, g, c: (bsel(b), 0)),
        pl.BlockSpec((NB, G, LANE, HEAD_B), lambda b, g, c: (bsel(b), g, 0, 0)),
        vec_spec(0), vec_spec(1), vec_spec(2),
        pl.BlockSpec((1, SM_W), lambda b, g, c: (0, 0)),
        pl.BlockSpec((LANE, gw), lambda b, g, c: (0, g)),
        pl.BlockSpec((LANE, gw), lambda b, g, c: (0, g)),
        pl.BlockSpec((2 * LANE, gw), lambda b, g, c: (0, g)),
        vec_spec(), vec_spec(), vec_spec(), vec_spec(), vec_spec(), vec_spec(), vec_spec(),
    ]
    return pl.pallas_call(
        functools.partial(_rwkv_kernel, C=C, NB=NB, G=G),
        out_shape=(jax.ShapeDtypeStruct((nb * seq, W_B), BF16),
                   jax.ShapeDtypeStruct((nb, N_PAIR, LANE, HEAD_B), F32)),
        grid=(nb // NB, ng, nch),
        in_specs=in_specs,
        out_specs=(pl.BlockSpec((R, gw), lambda b, g, c: (b * nch + c, g)),
                   pl.BlockSpec((NB, G, LANE, HEAD_B), lambda b, g, c: (b, g, 0, 0))),
        scratch_shapes=[pltpu.VMEM((SUB, gw), F32), pltpu.VMEM((SUB, gw), F32), pltpu.VMEM((SUB, gw), F32),
                        pltpu.VMEM((SUB, SM_W), F32), pltpu.VMEM((NB * G, LANE, LANE), F32)],
        compiler_params=pltpu.CompilerParams(
            dimension_semantics=("parallel", "parallel", "arbitrary"), vmem_limit_bytes=VMEM_LIMIT),
        name="rwkv_mixer",
    )(proj, proj, proj, proj, prev_rkv, prev_rkv, prev_rkv, prev_sm, s0,
      mu_rkv, mu_rkv, mu_rkv, mu_sm, w2p, a2p, g2p, w0, a0, k_k, k_a, r_k, lnw, lnb)


def _small_layout(cols_ba, cols_w, cols_a, cols_g, axis=-1):
    def z(n):
        shape = list(cols_w.shape)
        shape[axis] = n
        return jnp.zeros(shape, cols_w.dtype)
    return jnp.concatenate(
        [cols_ba, z(SM_WA - cols_ba.shape[axis]), cols_w, cols_a, cols_g,
         z(SM_W - SM_G - cols_g.shape[axis])], axis=axis)


def _pad_rows8(x):
    b, n, c = x.shape
    return jnp.concatenate([jnp.zeros((b, SUB - n, c), x.dtype), x], axis=1).reshape(b * SUB, c)


def kernel(x_prompt, x_sample, state_delta, state_conv_qkv, state_wkv, state_shift, state_ffn_conv, meta, norm1, w_in, conv_a, a_log, dt_bias, onorm_a, mu_b, w0, w2, a0, a2, g2, k_k, k_a, r_k, lnx_w, lnx_b, w_o, norm2, w_ffn_in, conv_f, w_ffn_out, norm_f):
    nbp, seq_p, _ = x_prompt.shape
    nbs, seq_s, _ = x_sample.shape
    n_s = nbs * seq_s
    assert w_in.shape[0] == 1, "single-layer trunk"
    assert seq_s == SUB
    l = 0

    wt = w_in[l].T
    o_b = A_PROJ
    o_l = A_PROJ + 3 * W_B
    w_cat_t = jnp.concatenate([
        wt[:4 * W_A], wt[o_b:o_l],
        _small_layout(wt[4 * W_A:A_PROJ], wt[o_l:o_l + W_LORA], wt[o_l + W_LORA:o_l + W_LORA + A_LORA],
                      wt[o_l + W_LORA + A_LORA:], axis=0)], axis=0).astype(BF16)
    mu = mu_b[l]
    mu_rkv = mu[None, :3 * W_B]
    mu_sm = _small_layout(jnp.zeros((1, 2 * H_A), F32), mu[None, 3 * W_B:3 * W_B + W_LORA],
                          mu[None, 3 * W_B + W_LORA:3 * W_B + W_LORA + A_LORA],
                          mu[None, 3 * W_B + W_LORA + A_LORA:])
    w2p = jnp.concatenate([w2[l], jnp.zeros((LANE - W_LORA, W_B), F32)], axis=0)
    a2p = jnp.concatenate([jnp.zeros((W_LORA, W_B), F32), a2[l]], axis=0)
    g2p = jnp.concatenate([g2[l], jnp.zeros((2 * LANE - G_LORA, W_B), F32)], axis=0)
    alog_row = jnp.concatenate([jnp.zeros((H_A,), F32), a_log[l], jnp.zeros((LANE - 2 * H_A,), F32)])[None]
    dtb_row = jnp.concatenate([jnp.zeros((H_A,), F32), dt_bias[l], jnp.zeros((LANE - 2 * H_A,), F32)])[None]
    wo_bf = w_o[l].astype(BF16)
    wfo_bf = w_ffn_out[l].astype(BF16)
    row = lambda v: v.reshape(1, -1)

    def mix(proj, row0, nb, seq, C, NB, prev_qkv, prev_rkv, prev_sm, s_delta, s_wkv):
        oa, sd = _delta_mixer(proj, row0, nb, seq, C, NB, prev_qkv, s_delta, conv_a[l], alog_row, dtb_row,
                              row(onorm_a[l]))
        ob, sw = _rwkv_mixer(proj, row0, nb, seq, C, NB, N_PAIR, prev_rkv, prev_sm, s_wkv, mu_rkv, mu_sm,
                             w2p, a2p, g2p, row(w0[l]), row(a0[l]), row(k_k[l]), row(k_a[l]),
                             row(r_k[l]), row(lnx_w[l]), row(lnx_b[l]))
        return oa, ob, sd, sw

    xs_rows = jnp.concatenate([x_sample.reshape(n_s, D_MODEL), meta], axis=0)
    xp_rows = x_prompt.reshape(nbp * seq_p, D_MODEL)
    n_small = n_s + N_META
    proj_s = _norm_matmul(xs_rows, row(norm1[l]), w_cat_t, n_small, 512)
    proj_p = _norm_matmul(xp_rows, row(norm1[l]), w_cat_t, 1024, 512)

    zeros = lambda *s: jnp.zeros(s, F32)
    oa_m, ob_m, sd_m, sw_m = mix(proj_s, n_s, 1, N_META, N_META, 1, zeros(SUB, 3 * W_A),
                                 zeros(SUB, 3 * W_B), zeros(SUB, SM_W),
                                 zeros(1, H_A, HEAD_A, HEAD_A), zeros(1, N_PAIR, LANE, HEAD_B))
    tail = proj_s[n_small - SUB:n_small]
    oa_p, ob_p, sd_p, sw_p = mix(proj_p, 0, nbp, seq_p, 64, 1, tail[:, :3 * W_A],
                                 tail[:, RKV0:SM0], tail[:, SM0:], sd_m, sw_m)
    sh = state_shift[l]
    sh_sm = _small_layout(jnp.zeros((nbs, 1, 2 * H_A), F32), sh[..., 3 * W_B:3 * W_B + W_LORA],
                          sh[..., 3 * W_B + W_LORA:3 * W_B + W_LORA + A_LORA],
                          sh[..., 3 * W_B + W_LORA + A_LORA:])
    oa_s, ob_s, sd_s, sw_s = mix(proj_s, 0, nbs, seq_s, seq_s, 8, _pad_rows8(state_conv_qkv[l]),
                                 _pad_rows8(sh[..., :3 * W_B]), _pad_rows8(sh_sm), state_delta[l],
                                 state_wkv[l].reshape(nbs, N_PAIR, LANE, HEAD_B))

    oa_small = jnp.concatenate([oa_s, oa_m], axis=0)
    ob_small = jnp.concatenate([ob_s, ob_m], axis=0)
    x1_s, h2_s = _out_proj(xs_rows, oa_small, ob_small, wo_bf, row(norm2[l]), n_small // 5)
    x1_p, h2_p = _out_proj(xp_rows, oa_p, ob_p, wo_bf, row(norm2[l]), 512)
    prev_rows = jnp.concatenate([_pad_rows8(state_ffn_conv[l]), jnp.zeros((N_META, D_FF), F32)], axis=0)
    act_s, gate_s = _ffn_in(h2_s, w_ffn_in[l], prev_rows, conv_f[l], n_small, 768, 3, 0, n_s + SUB)
    act_p, tail_p = _ffn_in(h2_p, w_ffn_in[l], gate_s[None, n_small - SUB:n_small], conv_f[l], 1024, 768, 3,
                            seq_p // 1024)
    y_s = _ffn_out(act_s, wfo_bf, x1_s, row(norm_f), n_s, 512, 768)
    y_p = _ffn_out(act_p, wfo_bf, x1_p, row(norm_f), nbp * seq_p, 512, 768)

    def states(proj, gate_tail, nb, seq, sd, sw):
        p3 = proj.reshape(-1, seq, P_CAT)
        conv_new = p3[:nb, seq - 3:, :3 * W_A]
        last = p3[:nb, seq - 1:, :]
        shift_new = jnp.concatenate([last[..., RKV0:SM0], last[..., SM0 + SM_WA:SM0 + SM_WA + W_LORA + A_LORA],
                                     last[..., SM0 + SM_G:SM0 + SM_G + G_LORA]], axis=-1)
        ffn_new = gate_tail[:nb, SUB - 2:, :]
        return (sd[None], conv_new[None], sw.reshape(nb, H_B, HEAD_B, HEAD_B)[None], shift_new[None],
                ffn_new[None])

    return ((y_p.reshape(nbp, seq_p, D_MODEL), y_s.reshape(nbs, seq_s, D_MODEL))
            + states(proj_p, tail_p, nbp, seq_p, sd_p, sw_p)
            + states(proj_s, gate_s.reshape(-1, SUB, D_FF), nbs, seq_s, sd_s, sw_s))
```

```python
import functools

import jax
import jax.numpy as jnp
from jax import lax
from jax.experimental import pallas as pl
from jax.experimental.pallas import tpu as pltpu

F32 = jnp.float32
BF16 = jnp.bfloat16

D_MODEL = 2048
N_META = 16
W_A = 1024
HEAD_A = 128
H_A = 8
W_B = 1024
HEAD_B = 64
H_B = 16
N_PAIR = H_B // 2
W_LORA = 64
A_LORA = 64
G_LORA = 160
D_FF = 5376
RMS_EPS = 1e-6
GN_EPS = 64e-5
A_PROJ = 4 * W_A + 2 * H_A
B_PROJ = 3 * W_B + W_LORA + A_LORA + G_LORA

QKVZ0 = 0
RKV0 = 4 * W_A
SM0 = RKV0 + 3 * W_B
SM_W = 512
SM_BA = 0
SM_WA = 128
SM_G = 256
P_CAT = SM0 + SM_W

LANE = 128
SUB = 8
VMEM_LIMIT = 48 * 1024 * 1024
VMEM_LIMIT_BIG = 58 * 1024 * 1024
NEG_BIG = -1e30

NT_DIMS = (((1,), (1,)), ((), ()))


def _bdot(a, b):
    return jnp.dot(a.astype(BF16), b.astype(BF16), preferred_element_type=F32)


def _bdot_nt(a, b):
    return lax.dot_general(a.astype(BF16), b.astype(BF16), NT_DIMS, preferred_element_type=F32)


def _bdot_tn(a, b):
    return lax.dot_general(a.astype(BF16), b.astype(BF16), (((0,), (0,)), ((), ())),
                           preferred_element_type=F32)


def _pieces(a, n):
    out = []
    rem = a
    for i in range(n):
        p = rem.astype(BF16)
        out.append(p)
        if i + 1 < n:
            rem = rem - p.astype(F32)
    return out


def _xdot(a, b, n, dims=(((1,), (0,)), ((), ()))):
    bb = b.astype(BF16)
    acc = None
    for p in _pieces(a, n):
        t = lax.dot_general(p, bb, dims, preferred_element_type=F32)
        acc = t if acc is None else acc + t
    return acc


def _xdot_r(a, b, n, dims=(((1,), (0,)), ((), ()))):
    ab = a.astype(BF16)
    acc = None
    for p in _pieces(b, n):
        t = lax.dot_general(ab, p, dims, preferred_element_type=F32)
        acc = t if acc is None else acc + t
    return acc


def _sigmoid(x):
    return 1.0 / (1.0 + jnp.exp(-x))


def _silu(x):
    return x * _sigmoid(x)


def _softplus(x):
    return jnp.maximum(x, 0.0) + jnp.log(1.0 + jnp.exp(-jnp.abs(x)))


def _iota(shape, dim):
    return lax.broadcasted_iota(jnp.int32, shape, dim)


def _shifted(x, prev8, k):
    n = x.shape[0]
    xr = pltpu.roll(x, k, 0)
    pr = pltpu.roll(prev8, k, 0)
    first = jnp.where(_iota((SUB, x.shape[1]), 0) < k, pr, xr[0:SUB])
    if n == SUB:
        return first
    return jnp.concatenate([first, xr[SUB:]], axis=0)


def _delayed(x, hist, k, nseq):
    if nseq == 1:
        return _shifted(x, hist, k)
    n = x.shape[0]
    use_hist = (_iota(x.shape, 0) & (SUB - 1)) < k
    return jnp.where(use_hist, pltpu.roll(hist, n + k - SUB, 0), pltpu.roll(x, k, 0))


def _same_seq(shape, c, rows_total):
    if c == rows_total:
        return None
    sh = c.bit_length() - 1
    return (lax.shift_right_logical(_iota(shape, 0), sh)
            == lax.shift_right_logical(_iota(shape, 1) & (rows_total - 1), sh))


class _Packed:
    def __init__(self, r, c):
        self.r = r
        self.c = c
        row = _iota((r, 2 * r), 0)
        col = _iota((r, 2 * r), 1) & (r - 1)
        same = _same_seq((r, 2 * r), c, r)
        self.incl = (row >= col) if same is None else ((row >= col) & same)
        self.strict = (row > col) if same is None else ((row > col) & same)
        self.eye = (row == col).astype(F32)
        self.first = _iota((r, 2 * r), 1) < r
        self.bd_mask = (_iota((2 * r, 2 * r), 0) < r) == (_iota((2 * r, 2 * r), 1) < r)

    def block_diag(self, m):
        return jnp.where(self.bd_mask, jnp.concatenate([m, m], axis=0), 0.0)

    def inverse_many(self, xs):
        ts = [self.eye + x for x in xs]
        ps = list(xs)
        n = 1
        while 2 * n < self.c:
            ps = [_bdot(p, self.block_diag(p)) for p in ps]
            ts = [t + _bdot(p, self.block_diag(t)) for p, t in zip(ps, ts)]
            n *= 2
        return ts


def _seq_masks(r, c):
    row = _iota((r, r), 0)
    col = _iota((r, r), 1)
    same = _same_seq((r, r), c, r)
    if same is None:
        return (row >= col).astype(F32), None
    return ((row >= col) & same).astype(F32), same.astype(F32)


def _norm_matmul_kernel(x_ref, g_ref, w_ref, o_ref, h_ref):
    @pl.when(pl.program_id(1) == 0)
    def _():
        x = x_ref[...]
        ms = jnp.mean(x * x, axis=-1, keepdims=True)
        h_ref[...] = (x * lax.rsqrt(ms + RMS_EPS) * g_ref[...]).astype(BF16)

    o_ref[...] = lax.dot_general(h_ref[...], w_ref[...], NT_DIMS, preferred_element_type=F32)


def _norm_matmul(x, g, wt, tm, tn):
    m, k = x.shape
    n = wt.shape[0]
    return pl.pallas_call(
        _norm_matmul_kernel,
        out_shape=jax.ShapeDtypeStruct((m, n), F32),
        grid=(m // tm, n // tn),
        in_specs=[
            pl.BlockSpec((tm, k), lambda i, j: (i, 0)),
            pl.BlockSpec((1, k), lambda i, j: (0, 0)),
            pl.BlockSpec((tn, k), lambda i, j: (j, 0)),
        ],
        out_specs=pl.BlockSpec((tm, tn), lambda i, j: (i, j)),
        scratch_shapes=[pltpu.VMEM((tm, k), BF16)],
        compiler_params=pltpu.CompilerParams(
            dimension_semantics=("parallel", "arbitrary"), vmem_limit_bytes=VMEM_LIMIT_BIG),
        name="norm_matmul",
    )(x, g, wt)


def _out_proj_kernel(x_ref, oa_ref, ob_ref, wt_ref, wb_ref, g_ref, x1_ref, h_ref):
    acc = jnp.dot(oa_ref[...], wt_ref[...], preferred_element_type=F32)
    acc = acc + jnp.dot(ob_ref[...], wb_ref[...], preferred_element_type=F32)
    x1 = x_ref[...] + acc
    x1_ref[...] = x1
    ms = jnp.mean(x1 * x1, axis=-1, keepdims=True)
    h_ref[...] = (x1 * lax.rsqrt(ms + RMS_EPS) * g_ref[...]).astype(BF16)


def _out_proj(x, oa, ob, wo, g, tm):
    m, d = x.shape
    return pl.pallas_call(
        _out_proj_kernel,
        out_shape=(jax.ShapeDtypeStruct((m, d), F32), jax.ShapeDtypeStruct((m, d), BF16)),
        grid=(m // tm,),
        in_specs=[
            pl.BlockSpec((tm, d), lambda i: (i, 0)),
            pl.BlockSpec((tm, W_A), lambda i: (i, 0)),
            pl.BlockSpec((tm, W_B), lambda i: (i, 0)),
            pl.BlockSpec((W_A, d), lambda i: (0, 0)),
            pl.BlockSpec((W_B, d), lambda i: (W_A // W_B, 0)),
            pl.BlockSpec((1, d), lambda i: (0, 0)),
        ],
        out_specs=(pl.BlockSpec((tm, d), lambda i: (i, 0)), pl.BlockSpec((tm, d), lambda i: (i, 0))),
        compiler_params=pltpu.CompilerParams(
            dimension_semantics=("parallel",), vmem_limit_bytes=VMEM_LIMIT),
        name="out_proj",
    )(x, oa, ob, wo, wo, g)


def _ffn_in_kernel(h_ref, wg_ref, wu_ref, prev_ref, cw_ref, act_ref, tail_ref, wgb_ref, wub_ref, carry_ref,
                   *, tiles_per_seq, state_rows, nsub):
    m = pl.program_id(1)

    @pl.when(m == 0)
    def _():
        wgb_ref[...] = wg_ref[...].astype(BF16)
        wub_ref[...] = wu_ref[...].astype(BF16)

    if tiles_per_seq:
        @pl.when(m % tiles_per_seq == 0)
        def _():
            carry_ref[...] = prev_ref[...]

    h = h_ref[...]
    n = h.shape[0]
    w = wgb_ref.shape[1] // nsub
    for i in range(nsub):
        cs = slice(i * w, (i + 1) * w)
        gate = jnp.dot(h, wgb_ref[:, cs], preferred_element_type=F32)
        up = jnp.dot(h, wub_ref[:, cs], preferred_element_type=F32)
        if tiles_per_seq:
            prev8 = carry_ref[:, cs]
            d1 = _shifted(gate, prev8, 1)
            d2 = _shifted(gate, prev8, 2)
            carry_ref[:, cs] = gate[n - SUB:n]
            tail_ref[:, cs] = gate[n - SUB:n]
        else:
            row = _iota(gate.shape, 0)
            prev = prev_ref[:, cs]
            tail_ref[:, cs] = gate

            def delayed(k):
                use_prev = ((row & (SUB - 1)) < k) & (row < state_rows)
                return jnp.where(use_prev, pltpu.roll(prev, n + k - SUB, 0), pltpu.roll(gate, k, 0))

            d1 = delayed(1)
            d2 = delayed(2)
        y = gate * cw_ref[2:3, cs] + d1 * cw_ref[1:2, cs] + d2 * cw_ref[0:1, cs]
        act_ref[:, cs] = (_silu(y) * up).astype(BF16)


def _ffn_in(h, w, prev, cw, tm, tn, nsub, tiles_per_seq, state_rows=0):
    m, d = h.shape
    nj = D_FF // tn
    nm = m // tm
    if tiles_per_seq:
        nseq = nm // tiles_per_seq
        prev_spec = pl.BlockSpec((None, SUB, tn), lambda j, i: (0, 0, j))
        tail_shape = jax.ShapeDtypeStruct((nseq, SUB, D_FF), F32)
        tail_spec = pl.BlockSpec((None, SUB, tn), lambda j, i: (i // tiles_per_seq, 0, j))
    else:
        prev_spec = pl.BlockSpec((tm, tn), lambda j, i: (i, j))
        tail_shape = jax.ShapeDtypeStruct((m, D_FF), F32)
        tail_spec = pl.BlockSpec((tm, tn), lambda j, i: (i, j))
    return pl.pallas_call(
        functools.partial(_ffn_in_kernel, tiles_per_seq=tiles_per_seq, state_rows=state_rows, nsub=nsub),
        out_shape=(jax.ShapeDtypeStruct((m, D_FF), BF16), tail_shape),
        grid=(nj, nm),
        in_specs=[
            pl.BlockSpec((tm, d), lambda j, i: (i, 0)),
            pl.BlockSpec((d, tn), lambda j, i: (0, j)),
            pl.BlockSpec((d, tn), lambda j, i: (0, nj + j)),
            prev_spec,
            pl.BlockSpec((3, tn), lambda j, i: (0, j)),
        ],
        out_specs=(pl.BlockSpec((tm, tn), lambda j, i: (i, j)), tail_spec),
        scratch_shapes=[pltpu.VMEM((d, tn), BF16), pltpu.VMEM((d, tn), BF16), pltpu.VMEM((SUB, tn), F32)],
        compiler_params=pltpu.CompilerParams(
            dimension_semantics=("parallel", "arbitrary"), vmem_limit_bytes=VMEM_LIMIT_BIG),
        name="ffn_in",
    )(h, w, w, prev, cw)


def _ffn_out_kernel(a_ref, w_ref, x_ref, g_ref, o_ref):
    kk = pl.program_id(1)

    @pl.when(kk == 0)
    def _():
        o_ref[...] = x_ref[...]

    o_ref[...] += jnp.dot(a_ref[...], w_ref[...], preferred_element_type=F32)

    @pl.when(kk == pl.num_programs(1) - 1)
    def _():
        x = o_ref[...]
        ms = jnp.mean(x * x, axis=-1, keepdims=True)
        o_ref[...] = x * lax.rsqrt(ms + RMS_EPS) * g_ref[...]


def _ffn_out(act, w, x1, g, m, tm, tk):
    kdim, d = w.shape
    return pl.pallas_call(
        _ffn_out_kernel,
        out_shape=jax.ShapeDtypeStruct((m, d), F32),
        grid=(m // tm, kdim // tk),
        in_specs=[
            pl.BlockSpec((tm, tk), lambda i, k: (i, k)),
            pl.BlockSpec((tk, d), lambda i, k: (k, 0)),
            pl.BlockSpec((tm, d), lambda i, k: (i, 0)),
            pl.BlockSpec((1, d), lambda i, k: (0, 0)),
        ],
        out_specs=pl.BlockSpec((tm, d), lambda i, k: (i, 0)),
        compiler_params=pltpu.CompilerParams(
            dimension_semantics=("parallel", "arbitrary"), vmem_limit_bytes=VMEM_LIMIT),
        name="ffn_out",
    )(act, w, x1, g)


def _delta_kernel(qkvz_ref, ba_ref, prev_ref, s0_ref, cw_ref, alog_ref, dtb_ref, on_ref,
                  o_ref, s_ref, carry_ref, *, C, NB):
    R = NB * C
    ci = pl.program_id(1)

    @pl.when(ci == 0)
    def _():
        if NB == 1:
            carry_ref[...] = prev_ref[...]
        s_ref[...] = s0_ref[...]

    hist_ref = carry_ref if NB == 1 else prev_ref
    tri, ones_seq = _seq_masks(R, C)

    ba = ba_ref[...]
    beta_full = _sigmoid(ba)
    g_full = -jnp.exp(alog_ref[...]) * _softplus(ba + dtb_ref[...])
    gc_full = _xdot_r(tri, g_full, 3)
    gtot_full = gc_full[C - 1:C, :] if NB == 1 else _xdot_r(ones_seq, g_full, 3)

    def conv_silu(c0):
        x = qkvz_ref[:, c0:c0 + LANE]
        hist = hist_ref[:, c0:c0 + LANE]
        y = x * cw_ref[3:4, c0:c0 + LANE]
        for k in (1, 2, 3):
            y = y + _delayed(x, hist, k, NB) * cw_ref[3 - k:4 - k, c0:c0 + LANE]
        return _silu(y)

    def l2n(x):
        return x * lax.rsqrt(jnp.sum(x * x, axis=-1, keepdims=True) + 1e-6)

    def seq_rows(x, n):
        return x[n * C:(n + 1) * C]

    heads = range(H_A)
    seqs = range(NB)
    q = [l2n(conv_silu(h * HEAD_A)) * (HEAD_A ** -0.5) for h in heads]
    k = [l2n(conv_silu(W_A + h * HEAD_A)) for h in heads]
    v = [conv_silu(2 * W_A + h * HEAD_A) for h in heads]
    bcol = [beta_full[:, h:h + 1] for h in heads]
    gcol = [gc_full[:, H_A + h:H_A + h + 1] for h in heads]
    gtot = [gtot_full[:, H_A + h:H_A + h + 1] for h in heads]
    eg = [jnp.exp(gcol[h]) for h in heads]
    kb = [k[h] * bcol[h] for h in heads]

    pk = _Packed(R, C)
    hpairs = range(H_A // 2)
    zc = jnp.zeros((R, HEAD_A), F32)
    row2 = _iota((2 * R, LANE), 0)
    lane2 = _iota((2 * R, LANE), 1)
    ones_cl = jnp.ones((R, LANE), F32)
    gc2 = jnp.concatenate([gc_full, gc_full], axis=0)
    kq, gamma = [], []
    for hp in hpairs:
        h0, h1 = 2 * hp, 2 * hp + 1
        lhs = jnp.concatenate([jnp.concatenate([kb[h0], kb[h1]], axis=1),
                               jnp.concatenate([q[h0], q[h1]], axis=1)], axis=0)
        rk = jnp.concatenate([jnp.concatenate([k[h0], zc], axis=1),
                              jnp.concatenate([zc, k[h1]], axis=1)], axis=0)
        kq.append(_bdot_nt(lhs, rk))
        sel = lane2 == jnp.where(row2 < R, H_A + h0, H_A + h1)
        grow = _xdot_r(ones_cl, jnp.where(sel, gc2, 0.0), 3, NT_DIMS)
        gcol_p = jnp.where(pk.first, gcol[h0], gcol[h1])
        gamma.append(jnp.exp(jnp.where(pk.incl, gcol_p - grow, NEG_BIG)))
    tinv = pk.inverse_many([-jnp.where(pk.strict, kq[hp][0:R] * gamma[hp], 0.0) for hp in hpairs])
    uw = []
    for hp in hpairs:
        h0, h1 = 2 * hp, 2 * hp + 1
        z2 = jnp.zeros((R, 2 * HEAD_A), F32)
        rhs = jnp.concatenate([jnp.concatenate([v[h0] * bcol[h0], kb[h0] * eg[h0], z2], axis=1),
                               jnp.concatenate([z2, v[h1] * bcol[h1], kb[h1] * eg[h1]], axis=1)], axis=0)
        uw.append(_bdot(tinv[hp], rhs))
    u = [uw[h // 2][:, (h % 2) * 2 * HEAD_A:(h % 2) * 2 * HEAD_A + HEAD_A] for h in heads]
    w = [uw[h // 2][:, (h % 2) * 2 * HEAD_A + HEAD_A:(h % 2 + 1) * 2 * HEAD_A] for h in heads]
    qd = [q[h] * eg[h] for h in heads]
    s = [[s_ref[n, h] for h in heads] for n in seqs]
    wqs = [[_bdot(jnp.concatenate([seq_rows(w[h], n), seq_rows(qd[h], n)], axis=0), s[n][h]) for h in heads]
           for n in seqs]
    ws = [jnp.concatenate([wqs[n][h][0:C] for n in seqs], axis=0) if NB > 1 else wqs[0][h][0:C] for h in heads]
    qs = [jnp.concatenate([wqs[n][h][C:2 * C] for n in seqs], axis=0) if NB > 1 else wqs[0][h][C:2 * C]
          for h in heads]
    v_new = [u[h] - ws[h] for h in heads]
    o = []
    for hp in hpairs:
        h0, h1 = 2 * hp, 2 * hp + 1
        vn_bd = jnp.concatenate([jnp.concatenate([v_new[h0], zc], axis=1),
                                 jnp.concatenate([zc, v_new[h1]], axis=1)], axis=0)
        op = _bdot(kq[hp][R:2 * R] * gamma[hp], vn_bd)
        o.append(qs[h0] + op[:, :HEAD_A])
        o.append(qs[h1] + op[:, HEAD_A:])
    kd = [k[h] * jnp.exp(gtot[h] - gcol[h]) for h in heads]
    for n in seqs:
        for h in heads:
            glast = gtot[h] if NB == 1 else gtot[h][n * C:n * C + 1]
            s_ref[n, h] = s[n][h] * jnp.exp(glast) + _bdot_tn(seq_rows(kd[h], n), seq_rows(v_new[h], n))
    for h in heads:
        z = qkvz_ref[:, 3 * W_A + h * HEAD_A:3 * W_A + (h + 1) * HEAD_A]
        oh = o[h] * lax.rsqrt(jnp.mean(o[h] * o[h], axis=-1, keepdims=True) + RMS_EPS)
        o_ref[:, h * HEAD_A:(h + 1) * HEAD_A] = (oh * on_ref[...] * _silu(z)).astype(BF16)

    if NB == 1:
        carry_ref[...] = qkvz_ref[C - SUB:C, 0:3 * W_A]


def _delta_mixer(proj, row0, nb, seq, C, NB, prev, s0, cw, alog_row, dtb_row, onorm):
    nch = seq // C
    assert NB == 1 or (nch == 1 and C == SUB)
    R = NB * C
    blk0 = row0 // R
    bcast = s0.shape[0] == 1
    bsel = (lambda b: 0) if bcast else (lambda b: b)
    return pl.pallas_call(
        functools.partial(_delta_kernel, C=C, NB=NB),
        out_shape=(jax.ShapeDtypeStruct((nb * seq, W_A), BF16),
                   jax.ShapeDtypeStruct((nb, H_A, HEAD_A, HEAD_A), F32)),
        grid=(nb // NB, nch),
        in_specs=[
            pl.BlockSpec((R, 4 * W_A), lambda b, c: (blk0 + b * nch + c, 0)),
            pl.BlockSpec((R, LANE), lambda b, c: (blk0 + b * nch + c, SM0 // LANE)),
            pl.BlockSpec((NB * SUB, 3 * W_A), lambda b, c: (bsel(b), 0)),
            pl.BlockSpec((NB, H_A, HEAD_A, HEAD_A), lambda b, c: (bsel(b), 0, 0, 0)),
            pl.BlockSpec((4, 3 * W_A), lambda b, c: (0, 0)),
            pl.BlockSpec((1, LANE), lambda b, c: (0, 0)),
            pl.BlockSpec((1, LANE), lambda b, c: (0, 0)),
            pl.BlockSpec((1, HEAD_A), lambda b, c: (0, 0)),
        ],
        out_specs=(pl.BlockSpec((R, W_A), lambda b, c: (b * nch + c, 0)),
                   pl.BlockSpec((NB, H_A, HEAD_A, HEAD_A), lambda b, c: (b, 0, 0, 0))),
        scratch_shapes=[pltpu.VMEM((SUB, 3 * W_A), F32)],
        compiler_params=pltpu.CompilerParams(
            dimension_semantics=("parallel", "arbitrary"), vmem_limit_bytes=VMEM_LIMIT),
        name="delta_mixer",
    )(proj, proj, prev, s0, cw, alog_row, dtb_row, onorm)


def _rwkv_kernel(r_ref, k_ref, v_ref, sm_ref, pr_ref, pk_ref, pv_ref, psm_ref, s0_ref,
                 mur_ref, muk_ref, muv_ref, musm_ref, w2_ref, a2_ref, g2_ref,
                 w0_ref, a0_ref, kk_ref, ka_ref, rk_ref, lnw_ref, lnb_ref,
                 o_ref, s_ref, cr_ref, ck_ref, cv_ref, csm_ref, st_ref, *, C, NB, G):
    R = NB * C
    ci = pl.program_id(2)
    nch = pl.num_programs(2)
    seqs = range(NB)
    pairs = range(G)

    r2 = _iota((LANE, LANE), 0)
    c2 = _iota((LANE, LANE), 1)
    same_head = (r2 < HEAD_B) == (c2 < HEAD_B)
    ones_bd = same_head.astype(F32)
    spread = (_iota((HEAD_B, LANE), 0) == (_iota((HEAD_B, LANE), 1) & (HEAD_B - 1))).astype(F32)
    gather = ((_iota((LANE, HEAD_B), 0) & (HEAD_B - 1)) == _iota((LANE, HEAD_B), 1)).astype(F32)

    @pl.when(ci == 0)
    def _():
        if NB == 1:
            cr_ref[...] = pr_ref[...]
            ck_ref[...] = pk_ref[...]
            cv_ref[...] = pv_ref[...]
            csm_ref[...] = psm_ref[...]
        for n in seqs:
            for p in pairs:
                st_ref[n * G + p] = jnp.where(same_head, _xdot(s0_ref[n, p], spread, 3), 0.0)

    hr_ref, hk_ref, hv_ref, hsm_ref = ((cr_ref, ck_ref, cv_ref, csm_ref) if NB == 1
                                       else (pr_ref, pk_ref, pv_ref, psm_ref))
    tri, ones_seq = _seq_masks(R, C)

    def lerp(x, hist, mu):
        return x + (_delayed(x, hist, 1, NB) - x) * mu

    sm = sm_ref[...]
    xs = lerp(sm, hsm_ref[...], musm_ref[...])
    slab_wa = xs[:, SM_WA:SM_WA + LANE]
    slab_g = xs[:, SM_G:SM_G + 2 * LANE]

    def col(x, p):
        return x[:, p * LANE:(p + 1) * LANE]

    def to_rows(x):
        return jnp.concatenate([col(x, p) for p in pairs], axis=0)

    def to_cols(y):
        return jnp.concatenate([y[p * R:(p + 1) * R] for p in pairs], axis=1)

    def seq_rows(x, n):
        return x[n * C:(n + 1) * C]

    xr = lerp(r_ref[...], hr_ref[...], mur_ref[...])
    xk = lerp(k_ref[...], hk_ref[...], muk_ref[...])
    xv = lerp(v_ref[...], hv_ref[...], muv_ref[...])
    wlog = -_softplus(-(w0_ref[...] + _bdot(jnp.tanh(slab_wa), w2_ref[...]))) - 0.5
    ld = -jnp.exp(wlog)
    aa = _sigmoid(a0_ref[...] + _bdot(slab_wa, a2_ref[...]))
    gate = _bdot(_sigmoid(slab_g), g2_ref[...])
    kkr = xk * kk_ref[...]
    kkn = kkr * lax.rsqrt(to_cols(_bdot(to_rows(kkr * kkr), ones_bd)) + 1e-6)
    k2 = xk * (1.0 + (aa - 1.0) * ka_ref[...])
    lp = _xdot_r(tri, ld, 3)
    lp_tot = lp[C - 1:C, :] if NB == 1 else _xdot_r(ones_seq, ld, 3)
    e_neg = jnp.exp(-lp)
    e_rem = jnp.exp(lp_tot - lp)
    at = -kkn * jnp.exp(lp - ld)
    rt = xr * jnp.exp(lp)
    kb = kkn * aa
    bt = kb * e_neg
    kt = k2 * e_neg
    bhat = kb * e_rem
    khat = k2 * e_rem
    p_tot = jnp.exp(lp_tot)

    s = [[st_ref[n * G + p] for p in pairs] for n in seqs]
    atp = [col(at, p) for p in pairs]
    rtp = [col(rt, p) for p in pairs]
    ars = [[_bdot_nt(jnp.concatenate([seq_rows(atp[p], n), seq_rows(rtp[p], n)], axis=0), s[n][p])
            for p in pairs] for n in seqs]
    if NB == 1:
        x_state = [ars[0][p][0:C] for p in pairs]
        o_state = [ars[0][p][C:2 * C] for p in pairs]
    else:
        x_state = [jnp.concatenate([ars[n][p][0:C] for n in seqs], axis=0) for p in pairs]
        o_state = [jnp.concatenate([ars[n][p][C:2 * C] for n in seqs], axis=0) for p in pairs]
    pk = _Packed(R, C)
    stack_mask = (_iota((2 * R, LANE), 0) < R) == (_iota((2 * R, LANE), 1) < HEAD_B)

    def by_head(x):
        return jnp.where(stack_mask, jnp.concatenate([x, x], axis=0), 0.0)

    ar = [jnp.concatenate([atp[p], rtp[p]], axis=0) for p in pairs]
    ab = [_bdot_nt(ar[p], by_head(col(bt, p))) for p in pairs]
    ak = [_bdot_nt(ar[p], by_head(col(kt, p))) for p in pairs]
    tinv = pk.inverse_many([jnp.where(pk.strict, m[0:R], 0.0) for m in ab])
    v_bh = [by_head(col(xv, p)) for p in pairs]
    y = [x_state[p] + _bdot(jnp.where(pk.strict, ak[p][0:R], 0.0), v_bh[p]) for p in pairs]
    u = [_bdot(tinv[p], by_head(y[p])) for p in pairs]
    o = [o_state[p] + _bdot(jnp.where(pk.incl, ab[p][R:2 * R], 0.0), by_head(u[p]))
         + _bdot(jnp.where(pk.incl, ak[p][R:2 * R], 0.0), v_bh[p]) for p in pairs]
    for n in seqs:
        for p in pairs:
            uv = jnp.concatenate([seq_rows(u[p], n), seq_rows(col(xv, p), n)], axis=0)
            bkhat = jnp.concatenate([seq_rows(col(bhat, p), n), seq_rows(col(khat, p), n)], axis=0)
            decay = col(p_tot, p) if NB == 1 else col(p_tot, p)[n * C:n * C + 1]
            s_new = s[n][p] * decay + _bdot_tn(uv, bkhat)
            st_ref[n * G + p] = jnp.where(same_head, s_new, 0.0)

    o_rows = jnp.concatenate(o, axis=0)
    mean = _bdot(o_rows, ones_bd) * (1.0 / HEAD_B)
    d = o_rows - mean
    var = _bdot(d * d, ones_bd) * (1.0 / HEAD_B)
    on = to_cols(d * lax.rsqrt(var + GN_EPS)) * lnw_ref[...] + lnb_ref[...]
    bonus = to_cols(_bdot(to_rows(xr * k2 * rk_ref[...]), ones_bd)) * xv
    o_ref[...] = ((on + bonus) * gate).astype(BF16)

    if NB == 1:
        cr_ref[...] = r_ref[C - SUB:C, :]
        ck_ref[...] = k_ref[C - SUB:C, :]
        cv_ref[...] = v_ref[C - SUB:C, :]
        csm_ref[...] = sm_ref[C - SUB:C, :]

    @pl.when(ci == nch - 1)
    def _():
        for n in seqs:
            for p in pairs:
                s_ref[n, p] = _xdot(st_ref[n * G + p], gather, 3)


def _rwkv_mixer(proj, row0, nb, seq, C, NB, G, prev_rkv, prev_sm, s0, mu_rkv, mu_sm, w2p, a2p, g2p,
                w0, a0, k_k, k_a, r_k, lnw, lnb):
    nch = seq // C
    assert NB == 1 or (nch == 1 and C == SUB)
    R = NB * C
    ng = N_PAIR // G
    gw = G * LANE
    blk0 = row0 // R
    bcast = s0.shape[0] == 1
    bsel = (lambda b: 0) if bcast else (lambda b: b)

    def proj_spec(col0):
        return pl.BlockSpec((R, gw), lambda b, g, c: (blk0 + b * nch + c, col0 // gw + g))

    def prev_spec(part):
        return pl.BlockSpec((NB * SUB, gw), lambda b, g, c: (bsel(b), part * (W_B // gw) + g))

    def vec_spec(part=0):
        return pl.BlockSpec((1, gw), lambda b, g, c: (0, part * (W_B // gw) + g))

    in_specs = [
        proj_spec(RKV0), proj_spec(RKV0 + W_B), proj_spec(RKV0 + 2 * W_B),
        pl.BlockSpec((R, SM_W), lambda b, g, c: (blk0 + b * nch + c, SM0 // SM_W)),
        prev_spec(0), prev_spec(1), prev_spec(2),
        pl.BlockSpec((NB * SUB, SM_W), lambda b, g, c: (bsel(b), 0)),
        pl.BlockSpec((NB, G, LANE, HEAD_B), lambda b, g, c: (bsel(b), g, 0, 0)),
        vec_spec(0), vec_spec(1), vec_spec(2),
        pl.BlockSpec((1, SM_W), lambda b, g, c: (0, 0)),
        pl.BlockSpec((LANE, gw), lambda b, g, c: (0, g)),
        pl.BlockSpec((LANE, gw), lambda b, g, c: (0, g)),
        pl.BlockSpec((2 * LANE, gw), lambda b, g, c: (0, g)),
        vec_spec(), vec_spec(), vec_spec(), vec_spec(), vec_spec(), vec_spec(), vec_spec(),
    ]
    return pl.pallas_call(
        functools.partial(_rwkv_kernel, C=C, NB=NB, G=G),
        out_shape=(jax.ShapeDtypeStruct((nb * seq, W_B), BF16),
                   jax.ShapeDtypeStruct((nb, N_PAIR, LANE, HEAD_B), F32)),
        grid=(nb // NB, ng, nch),
        in_specs=in_specs,
        out_specs=(pl.BlockSpec((R, gw), lambda b, g, c: (b * nch + c, g)),
                   pl.BlockSpec((NB, G, LANE, HEAD_B), lambda b, g, c: (b, g, 0, 0))),
        scratch_shapes=[pltpu.VMEM((SUB, gw), F32), pltpu.VMEM((SUB, gw), F32), pltpu.VMEM((SUB, gw), F32),
                        pltpu.VMEM((SUB, SM_W), F32), pltpu.VMEM((NB * G, LANE, LANE), F32)],
        compiler_params=pltpu.CompilerParams(
            dimension_semantics=("parallel", "parallel", "arbitrary"), vmem_limit_bytes=VMEM_LIMIT),
        name="rwkv_mixer",
    )(proj, proj, proj, proj, prev_rkv, prev_rkv, prev_rkv, prev_sm, s0,
      mu_rkv, mu_rkv, mu_rkv, mu_sm, w2p, a2p, g2p, w0, a0, k_k, k_a, r_k, lnw, lnb)


def _small_layout(cols_ba, cols_w, cols_a, cols_g, axis=-1):
    def z(n):
        shape = list(cols_w.shape)
        shape[axis] = n
        return jnp.zeros(shape, cols_w.dtype)
    return jnp.concatenate(
        [cols_ba, z(SM_WA - cols_ba.shape[axis]), cols_w, cols_a, cols_g,
         z(SM_W - SM_G - cols_g.shape[axis])], axis=axis)


def _pad_rows8(x):
    b, n, c = x.shape
    return jnp.concatenate([jnp.zeros((b, SUB - n, c), x.dtype), x], axis=1).reshape(b * SUB, c)


def kernel(x_prompt, x_sample, state_delta, state_conv_qkv, state_wkv, state_shift, state_ffn_conv, meta, norm1, w_in, conv_a, a_log, dt_bias, onorm_a, mu_b, w0, w2, a0, a2, g2, k_k, k_a, r_k, lnx_w, lnx_b, w_o, norm2, w_ffn_in, conv_f, w_ffn_out, norm_f):
    nbp, seq_p, _ = x_prompt.shape
    nbs, seq_s, _ = x_sample.shape
    n_s = nbs * seq_s
    assert w_in.shape[0] == 1, "single-layer trunk"
    assert seq_s == SUB
    l = 0

    wt = w_in[l].T
    o_b = A_PROJ
    o_l = A_PROJ + 3 * W_B
    w_cat_t = jnp.concatenate([
        wt[:4 * W_A], wt[o_b:o_l],
        _small_layout(wt[4 * W_A:A_PROJ], wt[o_l:o_l + W_LORA], wt[o_l + W_LORA:o_l + W_LORA + A_LORA],
                      wt[o_l + W_LORA + A_LORA:], axis=0)], axis=0).astype(BF16)
    mu = mu_b[l]
    mu_rkv = mu[None, :3 * W_B]
    mu_sm = _small_layout(jnp.zeros((1, 2 * H_A), F32), mu[None, 3 * W_B:3 * W_B + W_LORA],
                          mu[None, 3 * W_B + W_LORA:3 * W_B + W_LORA + A_LORA],
                          mu[None, 3 * W_B + W_LORA + A_LORA:])
    w2p = jnp.concatenate([w2[l], jnp.zeros((LANE - W_LORA, W_B), F32)], axis=0)
    a2p = jnp.concatenate([jnp.zeros((W_LORA, W_B), F32), a2[l]], axis=0)
    g2p = jnp.concatenate([g2[l], jnp.zeros((2 * LANE - G_LORA, W_B), F32)], axis=0)
    alog_row = jnp.concatenate([jnp.zeros((H_A,), F32), a_log[l], jnp.zeros((LANE - 2 * H_A,), F32)])[None]
    dtb_row = jnp.concatenate([jnp.zeros((H_A,), F32), dt_bias[l], jnp.zeros((LANE - 2 * H_A,), F32)])[None]
    wo_bf = w_o[l].astype(BF16)
    wfo_bf = w_ffn_out[l].astype(BF16)
    row = lambda v: v.reshape(1, -1)

    def mix(proj, row0, nb, seq, C, NB, prev_qkv, prev_rkv, prev_sm, s_delta, s_wkv):
        oa, sd = _delta_mixer(proj, row0, nb, seq, C, NB, prev_qkv, s_delta, conv_a[l], alog_row, dtb_row,
                              row(onorm_a[l]))
        ob, sw = _rwkv_mixer(proj, row0, nb, seq, C, NB, N_PAIR, prev_rkv, prev_sm, s_wkv, mu_rkv, mu_sm,
                             w2p, a2p, g2p, row(w0[l]), row(a0[l]), row(k_k[l]), row(k_a[l]),
                             row(r_k[l]), row(lnx_w[l]), row(lnx_b[l]))
        return oa, ob, sd, sw

    xs_rows = jnp.concatenate([x_sample.reshape(n_s, D_MODEL), meta], axis=0)
    xp_rows = x_prompt.reshape(nbp * seq_p, D_MODEL)
    n_small = n_s + N_META
    proj_s = _norm_matmul(xs_rows, row(norm1[l]), w_cat_t, n_small, 1536)
    proj_p = _norm_matmul(xp_rows, row(norm1[l]), w_cat_t, 1024, 1536)

    zeros = lambda *s: jnp.zeros(s, F32)
    oa_m, ob_m, sd_m, sw_m = mix(proj_s, n_s, 1, N_META, N_META, 1, zeros(SUB, 3 * W_A),
                                 zeros(SUB, 3 * W_B), zeros(SUB, SM_W),
                                 zeros(1, H_A, HEAD_A, HEAD_A), zeros(1, N_PAIR, LANE, HEAD_B))
    tail = proj_s[n_small - SUB:n_small]
    oa_p, ob_p, sd_p, sw_p = mix(proj_p, 0, nbp, seq_p, 64, 1, tail[:, :3 * W_A],
                                 tail[:, RKV0:SM0], tail[:, SM0:], sd_m, sw_m)
    sh = state_shift[l]
    sh_sm = _small_layout(jnp.zeros((nbs, 1, 2 * H_A), F32), sh[..., 3 * W_B:3 * W_B + W_LORA],
                          sh[..., 3 * W_B + W_LORA:3 * W_B + W_LORA + A_LORA],
                          sh[..., 3 * W_B + W_LORA + A_LORA:])
    oa_s, ob_s, sd_s, sw_s = mix(proj_s, 0, nbs, seq_s, seq_s, 8, _pad_rows8(state_conv_qkv[l]),
                                 _pad_rows8(sh[..., :3 * W_B]), _pad_rows8(sh_sm), state_delta[l],
                                 state_wkv[l].reshape(nbs, N_PAIR, LANE, HEAD_B))

    oa_small = jnp.concatenate([oa_s, oa_m], axis=0)
    ob_small = jnp.concatenate([ob_s, ob_m], axis=0)
    x1_s, h2_s = _out_proj(xs_rows, oa_small, ob_small, wo_bf, row(norm2[l]), n_small // 5)
    x1_p, h2_p = _out_proj(xp_rows, oa_p, ob_p, wo_bf, row(norm2[l]), 512)
    prev_rows = jnp.concatenate([_pad_rows8(state_ffn_conv[l]), jnp.zeros((N_META, D_FF), F32)], axis=0)
    act_s, gate_s = _ffn_in(h2_s, w_ffn_in[l], prev_rows, conv_f[l], n_small, 768, 3, 0, n_s + SUB)
    act_p, tail_p = _ffn_in(h2_p, w_ffn_in[l], gate_s[None, n_small - SUB:n_small], conv_f[l], 1024, 768, 3,
                            seq_p // 1024)
    y_s = _ffn_out(act_s, wfo_bf, x1_s, row(norm_f), n_s, 1024, 768)
    y_p = _ffn_out(act_p, wfo_bf, x1_p, row(norm_f), nbp * seq_p, 1024, 768)

    def states(proj, gate_tail, nb, seq, sd, sw):
        p3 = proj.reshape(-1, seq, P_CAT)
        conv_new = p3[:nb, seq - 3:, :3 * W_A]
        last = p3[:nb, seq - 1:, :]
        shift_new = jnp.concatenate([last[..., RKV0:SM0], last[..., SM0 + SM_WA:SM0 + SM_WA + W_LORA + A_LORA],
                                     last[..., SM0 + SM_G:SM0 + SM_G + G_LORA]], axis=-1)
        ffn_new = gate_tail[:nb, SUB - 2:, :]
        return (sd[None], conv_new[None], sw.reshape(nb, H_B, HEAD_B, HEAD_B)[None], shift_new[None],
                ffn_new[None])

    return ((y_p.reshape(nbp, seq_p, D_MODEL), y_s.reshape(nbs, seq_s, D_MODEL))
            + states(proj_p, tail_p, nbp, seq_p, sd_p, sw_p)
            + states(proj_s, gate_s.reshape(-1, SUB, D_FF), nbs, seq_s, sd_s, sw_s))
```

```python
import functools

import jax
import jax.numpy as jnp
from jax import lax
from jax.experimental import pallas as pl
from jax.experimental.pallas import tpu as pltpu

F32 = jnp.float32
BF16 = jnp.bfloat16

D_MODEL = 2048
N_META = 16
W_A = 1024
HEAD_A = 128
H_A = 8
W_B = 1024
HEAD_B = 64
H_B = 16
N_PAIR = H_B // 2
W_LORA = 64
A_LORA = 64
G_LORA = 160
D_FF = 5376
RMS_EPS = 1e-6
GN_EPS = 64e-5
A_PROJ = 4 * W_A + 2 * H_A
B_PROJ = 3 * W_B + W_LORA + A_LORA + G_LORA

QKVZ0 = 0
RKV0 = 4 * W_A
SM0 = RKV0 + 3 * W_B
SM_W = 512
SM_BA = 0
SM_WA = 128
SM_G = 256
P_CAT = SM0 + SM_W

LANE = 128
SUB = 8
VMEM_LIMIT = 48 * 1024 * 1024
VMEM_LIMIT_BIG = 58 * 1024 * 1024
NEG_BIG = -1e30

NT_DIMS = (((1,), (1,)), ((), ()))


def _bdot(a, b):
    return jnp.dot(a.astype(BF16), b.astype(BF16), preferred_element_type=F32)


def _bdot_nt(a, b):
    return lax.dot_general(a.astype(BF16), b.astype(BF16), NT_DIMS, preferred_element_type=F32)


def _bdot_tn(a, b):
    return lax.dot_general(a.astype(BF16), b.astype(BF16), (((0,), (0,)), ((), ())),
                           preferred_element_type=F32)


def _pieces(a, n):
    out = []
    rem = a
    for i in range(n):
        p = rem.astype(BF16)
        out.append(p)
        if i + 1 < n:
            rem = rem - p.astype(F32)
    return out


def _xdot(a, b, n, dims=(((1,), (0,)), ((), ()))):
    bb = b.astype(BF16)
    acc = None
    for p in _pieces(a, n):
        t = lax.dot_general(p, bb, dims, preferred_element_type=F32)
        acc = t if acc is None else acc + t
    return acc


def _xdot_r(a, b, n, dims=(((1,), (0,)), ((), ()))):
    ab = a.astype(BF16)
    acc = None
    for p in _pieces(b, n):
        t = lax.dot_general(ab, p, dims, preferred_element_type=F32)
        acc = t if acc is None else acc + t
    return acc


def _sigmoid(x):
    return 1.0 / (1.0 + jnp.exp(-x))


def _silu(x):
    return x * _sigmoid(x)


def _softplus(x):
    return jnp.maximum(x, 0.0) + jnp.log(1.0 + jnp.exp(-jnp.abs(x)))


def _iota(shape, dim):
    return lax.broadcasted_iota(jnp.int32, shape, dim)


def _shifted(x, prev8, k):
    n = x.shape[0]
    xr = pltpu.roll(x, k, 0)
    pr = pltpu.roll(prev8, k, 0)
    first = jnp.where(_iota((SUB, x.shape[1]), 0) < k, pr, xr[0:SUB])
    if n == SUB:
        return first
    return jnp.concatenate([first, xr[SUB:]], axis=0)


def _delayed(x, hist, k, nseq):
    if nseq == 1:
        return _shifted(x, hist, k)
    n = x.shape[0]
    use_hist = (_iota(x.shape, 0) & (SUB - 1)) < k
    return jnp.where(use_hist, pltpu.roll(hist, n + k - SUB, 0), pltpu.roll(x, k, 0))


def _same_seq(shape, c, rows_total):
    if c == rows_total:
        return None
    sh = c.bit_length() - 1
    return (lax.shift_right_logical(_iota(shape, 0), sh)
            == lax.shift_right_logical(_iota(shape, 1) & (rows_total - 1), sh))


class _Packed:
    def __init__(self, r, c):
        self.r = r
        self.c = c
        row = _iota((r, 2 * r), 0)
        col = _iota((r, 2 * r), 1) & (r - 1)
        same = _same_seq((r, 2 * r), c, r)
        self.incl = (row >= col) if same is None else ((row >= col) & same)
        self.strict = (row > col) if same is None else ((row > col) & same)
        self.eye = (row == col).astype(F32)
        self.first = _iota((r, 2 * r), 1) < r
        self.bd_mask = (_iota((2 * r, 2 * r), 0) < r) == (_iota((2 * r, 2 * r), 1) < r)

    def block_diag(self, m):
        return jnp.where(self.bd_mask, jnp.concatenate([m, m], axis=0), 0.0)

    def inverse_many(self, xs):
        ts = [self.eye + x for x in xs]
        ps = list(xs)
        n = 1
        while 2 * n < self.c:
            ps = [_bdot(p, self.block_diag(p)) for p in ps]
            ts = [t + _bdot(p, self.block_diag(t)) for p, t in zip(ps, ts)]
            n *= 2
        return ts


def _seq_masks(r, c):
    row = _iota((r, r), 0)
    col = _iota((r, r), 1)
    same = _same_seq((r, r), c, r)
    if same is None:
        return (row >= col).astype(F32), None
    return ((row >= col) & same).astype(F32), same.astype(F32)


def _norm_matmul_kernel(x_ref, g_ref, w_ref, o_ref, h_ref):
    @pl.when(pl.program_id(1) == 0)
    def _():
        x = x_ref[...]
        ms = jnp.mean(x * x, axis=-1, keepdims=True)
        h_ref[...] = (x * lax.rsqrt(ms + RMS_EPS) * g_ref[...]).astype(BF16)

    o_ref[...] = lax.dot_general(h_ref[...], w_ref[...], NT_DIMS, preferred_element_type=F32)


def _norm_matmul(x, g, wt, tm, tn):
    m, k = x.shape
    n = wt.shape[0]
    return pl.pallas_call(
        _norm_matmul_kernel,
        out_shape=jax.ShapeDtypeStruct((m, n), F32),
        grid=(m // tm, n // tn),
        in_specs=[
            pl.BlockSpec((tm, k), lambda i, j: (i, 0)),
            pl.BlockSpec((1, k), lambda i, j: (0, 0)),
            pl.BlockSpec((tn, k), lambda i, j: (j, 0)),
        ],
        out_specs=pl.BlockSpec((tm, tn), lambda i, j: (i, j)),
        scratch_shapes=[pltpu.VMEM((tm, k), BF16)],
        compiler_params=pltpu.CompilerParams(
            dimension_semantics=("parallel", "arbitrary"), vmem_limit_bytes=VMEM_LIMIT_BIG),
        name="norm_matmul",
    )(x, g, wt)


def _out_proj_kernel(x_ref, oa_ref, ob_ref, wt_ref, wb_ref, g_ref, x1_ref, h_ref):
    acc = jnp.dot(oa_ref[...], wt_ref[...], preferred_element_type=F32)
    acc = acc + jnp.dot(ob_ref[...], wb_ref[...], preferred_element_type=F32)
    x1 = x_ref[...] + acc
    x1_ref[...] = x1
    ms = jnp.mean(x1 * x1, axis=-1, keepdims=True)
    h_ref[...] = (x1 * lax.rsqrt(ms + RMS_EPS) * g_ref[...]).astype(BF16)


def _out_proj(x, oa, ob, wo, g, tm):
    m, d = x.shape
    return pl.pallas_call(
        _out_proj_kernel,
        out_shape=(jax.ShapeDtypeStruct((m, d), F32), jax.ShapeDtypeStruct((m, d), BF16)),
        grid=(m // tm,),
        in_specs=[
            pl.BlockSpec((tm, d), lambda i: (i, 0)),
            pl.BlockSpec((tm, W_A), lambda i: (i, 0)),
            pl.BlockSpec((tm, W_B), lambda i: (i, 0)),
            pl.BlockSpec((W_A, d), lambda i: (0, 0)),
            pl.BlockSpec((W_B, d), lambda i: (W_A // W_B, 0)),
            pl.BlockSpec((1, d), lambda i: (0, 0)),
        ],
        out_specs=(pl.BlockSpec((tm, d), lambda i: (i, 0)), pl.BlockSpec((tm, d), lambda i: (i, 0))),
        compiler_params=pltpu.CompilerParams(
            dimension_semantics=("parallel",), vmem_limit_bytes=VMEM_LIMIT),
        name="out_proj",
    )(x, oa, ob, wo, wo, g)


def _ffn_in_kernel(h_ref, wg_ref, wu_ref, prev_ref, cw_ref, act_ref, tail_ref, wgb_ref, wub_ref, carry_ref,
                   *, tiles_per_seq, state_rows, nsub):
    m = pl.program_id(1)

    @pl.when(m == 0)
    def _():
        wgb_ref[...] = wg_ref[...].astype(BF16)
        wub_ref[...] = wu_ref[...].astype(BF16)

    if tiles_per_seq:
        @pl.when(m % tiles_per_seq == 0)
        def _():
            carry_ref[...] = prev_ref[...]

    h = h_ref[...]
    n = h.shape[0]
    w = wgb_ref.shape[1] // nsub
    for i in range(nsub):
        cs = slice(i * w, (i + 1) * w)
        gate = jnp.dot(h, wgb_ref[:, cs], preferred_element_type=F32)
        up = jnp.dot(h, wub_ref[:, cs], preferred_element_type=F32)
        if tiles_per_seq:
            prev8 = carry_ref[:, cs]
            d1 = _shifted(gate, prev8, 1)
            d2 = _shifted(gate, prev8, 2)
            carry_ref[:, cs] = gate[n - SUB:n]
            tail_ref[:, cs] = gate[n - SUB:n]
        else:
            row = _iota(gate.shape, 0)
            prev = prev_ref[:, cs]
            tail_ref[:, cs] = gate

            def delayed(k):
                use_prev = ((row & (SUB - 1)) < k) & (row < state_rows)
                return jnp.where(use_prev, pltpu.roll(prev, n + k - SUB, 0), pltpu.roll(gate, k, 0))

            d1 = delayed(1)
            d2 = delayed(2)
        y = gate * cw_ref[2:3, cs] + d1 * cw_ref[1:2, cs] + d2 * cw_ref[0:1, cs]
        act_ref[:, cs] = (_silu(y) * up).astype(BF16)


def _ffn_in(h, w, prev, cw, tm, tn, nsub, tiles_per_seq, state_rows=0):
    m, d = h.shape
    nj = D_FF // tn
    nm = m // tm
    if tiles_per_seq:
        nseq = nm // tiles_per_seq
        prev_spec = pl.BlockSpec((None, SUB, tn), lambda j, i: (0, 0, j))
        tail_shape = jax.ShapeDtypeStruct((nseq, SUB, D_FF), F32)
        tail_spec = pl.BlockSpec((None, SUB, tn), lambda j, i: (i // tiles_per_seq, 0, j))
    else:
        prev_spec = pl.BlockSpec((tm, tn), lambda j, i: (i, j))
        tail_shape = jax.ShapeDtypeStruct((m, D_FF), F32)
        tail_spec = pl.BlockSpec((tm, tn), lambda j, i: (i, j))
    return pl.pallas_call(
        functools.partial(_ffn_in_kernel, tiles_per_seq=tiles_per_seq, state_rows=state_rows, nsub=nsub),
        out_shape=(jax.ShapeDtypeStruct((m, D_FF), BF16), tail_shape),
        grid=(nj, nm),
        in_specs=[
            pl.BlockSpec((tm, d), lambda j, i: (i, 0)),
            pl.BlockSpec((d, tn), lambda j, i: (0, j)),
            pl.BlockSpec((d, tn), lambda j, i: (0, nj + j)),
            prev_spec,
            pl.BlockSpec((3, tn), lambda j, i: (0, j)),
        ],
        out_specs=(pl.BlockSpec((tm, tn), lambda j, i: (i, j)), tail_spec),
        scratch_shapes=[pltpu.VMEM((d, tn), BF16), pltpu.VMEM((d, tn), BF16), pltpu.VMEM((SUB, tn), F32)],
        compiler_params=pltpu.CompilerParams(
            dimension_semantics=("parallel", "arbitrary"), vmem_limit_bytes=VMEM_LIMIT_BIG),
        name="ffn_in",
    )(h, w, w, prev, cw)


def _ffn_out_kernel(a_ref, w_ref, x_ref, g_ref, o_ref):
    kk = pl.program_id(1)

    @pl.when(kk == 0)
    def _():
        o_ref[...] = x_ref[...]

    o_ref[...] += jnp.dot(a_ref[...], w_ref[...], preferred_element_type=F32)

    @pl.when(kk == pl.num_programs(1) - 1)
    def _():
        x = o_ref[...]
        ms = jnp.mean(x * x, axis=-1, keepdims=True)
        o_ref[...] = x * lax.rsqrt(ms + RMS_EPS) * g_ref[...]


def _ffn_out(act, w, x1, g, m, tm, tk):
    kdim, d = w.shape
    return pl.pallas_call(
        _ffn_out_kernel,
        out_shape=jax.ShapeDtypeStruct((m, d), F32),
        grid=(m // tm, kdim // tk),
        in_specs=[
            pl.BlockSpec((tm, tk), lambda i, k: (i, k)),
            pl.BlockSpec((tk, d), lambda i, k: (k, 0)),
            pl.BlockSpec((tm, d), lambda i, k: (i, 0)),
            pl.BlockSpec((1, d), lambda i, k: (0, 0)),
        ],
        out_specs=pl.BlockSpec((tm, d), lambda i, k: (i, 0)),
        compiler_params=pltpu.CompilerParams(
            dimension_semantics=("parallel", "arbitrary"), vmem_limit_bytes=VMEM_LIMIT),
        name="ffn_out",
    )(act, w, x1, g)


def _delta_kernel(qkvz_ref, ba_ref, prev_ref, s0_ref, cw_ref, alog_ref, dtb_ref, on_ref,
                  o_ref, s_ref, carry_ref, *, C, NB):
    R = NB * C
    ci = pl.program_id(1)

    @pl.when(ci == 0)
    def _():
        if NB == 1:
            carry_ref[...] = prev_ref[...]
        s_ref[...] = s0_ref[...]

    hist_ref = carry_ref if NB == 1 else prev_ref
    tri, ones_seq = _seq_masks(R, C)

    ba = ba_ref[...]
    beta_full = _sigmoid(ba)
    g_full = -jnp.exp(alog_ref[...]) * _softplus(ba + dtb_ref[...])
    gc_full = _xdot_r(tri, g_full, 3)
    gtot_full = gc_full[C - 1:C, :] if NB == 1 else _xdot_r(ones_seq, g_full, 3)

    def conv_silu(c0):
        x = qkvz_ref[:, c0:c0 + LANE]
        hist = hist_ref[:, c0:c0 + LANE]
        y = x * cw_ref[3:4, c0:c0 + LANE]
        for k in (1, 2, 3):
            y = y + _delayed(x, hist, k, NB) * cw_ref[3 - k:4 - k, c0:c0 + LANE]
        return _silu(y)

    def l2n(x):
        return x * lax.rsqrt(jnp.sum(x * x, axis=-1, keepdims=True) + 1e-6)

    def seq_rows(x, n):
        return x[n * C:(n + 1) * C]

    heads = range(H_A)
    seqs = range(NB)
    q = [l2n(conv_silu(h * HEAD_A)) * (HEAD_A ** -0.5) for h in heads]
    k = [l2n(conv_silu(W_A + h * HEAD_A)) for h in heads]
    v = [conv_silu(2 * W_A + h * HEAD_A) for h in heads]
    bcol = [beta_full[:, h:h + 1] for h in heads]
    gcol = [gc_full[:, H_A + h:H_A + h + 1] for h in heads]
    gtot = [gtot_full[:, H_A + h:H_A + h + 1] for h in heads]
    eg = [jnp.exp(gcol[h]) for h in heads]
    kb = [k[h] * bcol[h] for h in heads]

    pk = _Packed(R, C)
    hpairs = range(H_A // 2)
    zc = jnp.zeros((R, HEAD_A), F32)
    row2 = _iota((2 * R, LANE), 0)
    lane2 = _iota((2 * R, LANE), 1)
    ones_cl = jnp.ones((R, LANE), F32)
    gc2 = jnp.concatenate([gc_full, gc_full], axis=0)
    kq, gamma = [], []
    for hp in hpairs:
        h0, h1 = 2 * hp, 2 * hp + 1
        lhs = jnp.concatenate([jnp.concatenate([kb[h0], kb[h1]], axis=1),
                               jnp.concatenate([q[h0], q[h1]], axis=1)], axis=0)
        rk = jnp.concatenate([jnp.concatenate([k[h0], zc], axis=1),
                              jnp.concatenate([zc, k[h1]], axis=1)], axis=0)
        kq.append(_bdot_nt(lhs, rk))
        sel = lane2 == jnp.where(row2 < R, H_A + h0, H_A + h1)
        grow = _xdot_r(ones_cl, jnp.where(sel, gc2, 0.0), 3, NT_DIMS)
        gcol_p = jnp.where(pk.first, gcol[h0], gcol[h1])
        gamma.append(jnp.exp(jnp.where(pk.incl, gcol_p - grow, NEG_BIG)))
    tinv = pk.inverse_many([-jnp.where(pk.strict, kq[hp][0:R] * gamma[hp], 0.0) for hp in hpairs])
    uw = []
    for hp in hpairs:
        h0, h1 = 2 * hp, 2 * hp + 1
        z2 = jnp.zeros((R, 2 * HEAD_A), F32)
        rhs = jnp.concatenate([jnp.concatenate([v[h0] * bcol[h0], kb[h0] * eg[h0], z2], axis=1),
                               jnp.concatenate([z2, v[h1] * bcol[h1], kb[h1] * eg[h1]], axis=1)], axis=0)
        uw.append(_bdot(tinv[hp], rhs))
    u = [uw[h // 2][:, (h % 2) * 2 * HEAD_A:(h % 2) * 2 * HEAD_A + HEAD_A] for h in heads]
    w = [uw[h // 2][:, (h % 2) * 2 * HEAD_A + HEAD_A:(h % 2 + 1) * 2 * HEAD_A] for h in heads]
    qd = [q[h] * eg[h] for h in heads]
    s = [[s_ref[n, h] for h in heads] for n in seqs]
    wqs = [[_bdot(jnp.concatenate([seq_rows(w[h], n), seq_rows(qd[h], n)], axis=0), s[n][h]) for h in heads]
           for n in seqs]
    ws = [jnp.concatenate([wqs[n][h][0:C] for n in seqs], axis=0) if NB > 1 else wqs[0][h][0:C] for h in heads]
    qs = [jnp.concatenate([wqs[n][h][C:2 * C] for n in seqs], axis=0) if NB > 1 else wqs[0][h][C:2 * C]
          for h in heads]
    v_new = [u[h] - ws[h] for h in heads]
    o = []
    for hp in hpairs:
        h0, h1 = 2 * hp, 2 * hp + 1
        vn_bd = jnp.concatenate([jnp.concatenate([v_new[h0], zc], axis=1),
                                 jnp.concatenate([zc, v_new[h1]], axis=1)], axis=0)
        op = _bdot(kq[hp][R:2 * R] * gamma[hp], vn_bd)
        o.append(qs[h0] + op[:, :HEAD_A])
        o.append(qs[h1] + op[:, HEAD_A:])
    kd = [k[h] * jnp.exp(gtot[h] - gcol[h]) for h in heads]
    for n in seqs:
        for h in heads:
            glast = gtot[h] if NB == 1 else gtot[h][n * C:n * C + 1]
            s_ref[n, h] = s[n][h] * jnp.exp(glast) + _bdot_tn(seq_rows(kd[h], n), seq_rows(v_new[h], n))
    for h in heads:
        z = qkvz_ref[:, 3 * W_A + h * HEAD_A:3 * W_A + (h + 1) * HEAD_A]
        oh = o[h] * lax.rsqrt(jnp.mean(o[h] * o[h], axis=-1, keepdims=True) + RMS_EPS)
        o_ref[:, h * HEAD_A:(h + 1) * HEAD_A] = (oh * on_ref[...] * _silu(z)).astype(BF16)

    if NB == 1:
        carry_ref[...] = qkvz_ref[C - SUB:C, 0:3 * W_A]


def _delta_mixer(proj, row0, nb, seq, C, NB, prev, s0, cw, alog_row, dtb_row, onorm):
    nch = seq // C
    assert NB == 1 or (nch == 1 and C == SUB)
    R = NB * C
    blk0 = row0 // R
    bcast = s0.shape[0] == 1
    bsel = (lambda b: 0) if bcast else (lambda b: b)
    return pl.pallas_call(
        functools.partial(_delta_kernel, C=C, NB=NB),
        out_shape=(jax.ShapeDtypeStruct((nb * seq, W_A), BF16),
                   jax.ShapeDtypeStruct((nb, H_A, HEAD_A, HEAD_A), F32)),
        grid=(nb // NB, nch),
        in_specs=[
            pl.BlockSpec((R, 4 * W_A), lambda b, c: (blk0 + b * nch + c, 0)),
            pl.BlockSpec((R, LANE), lambda b, c: (blk0 + b * nch + c, SM0 // LANE)),
            pl.BlockSpec((NB * SUB, 3 * W_A), lambda b, c: (bsel(b), 0)),
            pl.BlockSpec((NB, H_A, HEAD_A, HEAD_A), lambda b, c: (bsel(b), 0, 0, 0)),
            pl.BlockSpec((4, 3 * W_A), lambda b, c: (0, 0)),
            pl.BlockSpec((1, LANE), lambda b, c: (0, 0)),
            pl.BlockSpec((1, LANE), lambda b, c: (0, 0)),
            pl.BlockSpec((1, HEAD_A), lambda b, c: (0, 0)),
        ],
        out_specs=(pl.BlockSpec((R, W_A), lambda b, c: (b * nch + c, 0)),
                   pl.BlockSpec((NB, H_A, HEAD_A, HEAD_A), lambda b, c: (b, 0, 0, 0))),
        scratch_shapes=[pltpu.VMEM((SUB, 3 * W_A), F32)],
        compiler_params=pltpu.CompilerParams(
            dimension_semantics=("parallel", "arbitrary"), vmem_limit_bytes=VMEM_LIMIT),
        name="delta_mixer",
    )(proj, proj, prev, s0, cw, alog_row, dtb_row, onorm)


def _rwkv_kernel(r_ref, k_ref, v_ref, sm_ref, pr_ref, pk_ref, pv_ref, psm_ref, s0_ref,
                 mur_ref, muk_ref, muv_ref, musm_ref, w2_ref, a2_ref, g2_ref,
                 w0_ref, a0_ref, kk_ref, ka_ref, rk_ref, lnw_ref, lnb_ref,
                 o_ref, s_ref, cr_ref, ck_ref, cv_ref, csm_ref, st_ref, *, C, NB, G):
    R = NB * C
    ci = pl.program_id(2)
    nch = pl.num_programs(2)
    seqs = range(NB)
    pairs = range(G)

    r2 = _iota((LANE, LANE), 0)
    c2 = _iota((LANE, LANE), 1)
    same_head = (r2 < HEAD_B) == (c2 < HEAD_B)
    ones_bd = same_head.astype(F32)
    spread = (_iota((HEAD_B, LANE), 0) == (_iota((HEAD_B, LANE), 1) & (HEAD_B - 1))).astype(F32)
    gather = ((_iota((LANE, HEAD_B), 0) & (HEAD_B - 1)) == _iota((LANE, HEAD_B), 1)).astype(F32)

    @pl.when(ci == 0)
    def _():
        if NB == 1:
            cr_ref[...] = pr_ref[...]
            ck_ref[...] = pk_ref[...]
            cv_ref[...] = pv_ref[...]
            csm_ref[...] = psm_ref[...]
        for n in seqs:
            for p in pairs:
                st_ref[n * G + p] = jnp.where(same_head, _xdot(s0_ref[n, p], spread, 3), 0.0)

    hr_ref, hk_ref, hv_ref, hsm_ref = ((cr_ref, ck_ref, cv_ref, csm_ref) if NB == 1
                                       else (pr_ref, pk_ref, pv_ref, psm_ref))
    tri, ones_seq = _seq_masks(R, C)

    def lerp(x, hist, mu):
        return x + (_delayed(x, hist, 1, NB) - x) * mu

    sm = sm_ref[...]
    xs = lerp(sm, hsm_ref[...], musm_ref[...])
    slab_wa = xs[:, SM_WA:SM_WA + LANE]
    slab_g = xs[:, SM_G:SM_G + 2 * LANE]

    def col(x, p):
        return x[:, p * LANE:(p + 1) * LANE]

    def to_rows(x):
        return jnp.concatenate([col(x, p) for p in pairs], axis=0)

    def to_cols(y):
        return jnp.concatenate([y[p * R:(p + 1) * R] for p in pairs], axis=1)

    def seq_rows(x, n):
        return x[n * C:(n + 1) * C]

    xr = lerp(r_ref[...], hr_ref[...], mur_ref[...])
    xk = lerp(k_ref[...], hk_ref[...], muk_ref[...])
    xv = lerp(v_ref[...], hv_ref[...], muv_ref[...])
    wlog = -_softplus(-(w0_ref[...] + _bdot(jnp.tanh(slab_wa), w2_ref[...]))) - 0.5
    ld = -jnp.exp(wlog)
    aa = _sigmoid(a0_ref[...] + _bdot(slab_wa, a2_ref[...]))
    gate = _bdot(_sigmoid(slab_g), g2_ref[...])
    kkr = xk * kk_ref[...]
    kkn = kkr * lax.rsqrt(to_cols(_bdot(to_rows(kkr * kkr), ones_bd)) + 1e-6)
    k2 = xk * (1.0 + (aa - 1.0) * ka_ref[...])
    lp = _xdot_r(tri, ld, 3)
    lp_tot = lp[C - 1:C, :] if NB == 1 else _xdot_r(ones_seq, ld, 3)
    e_neg = jnp.exp(-lp)
    e_rem = jnp.exp(lp_tot - lp)
    at = -kkn * jnp.exp(lp - ld)
    rt = xr * jnp.exp(lp)
    kb = kkn * aa
    bt = kb * e_neg
    kt = k2 * e_neg
    bhat = kb * e_rem
    khat = k2 * e_rem
    p_tot = jnp.exp(lp_tot)

    s = [[st_ref[n * G + p] for p in pairs] for n in seqs]
    atp = [col(at, p) for p in pairs]
    rtp = [col(rt, p) for p in pairs]
    ars = [[_bdot_nt(jnp.concatenate([seq_rows(atp[p], n), seq_rows(rtp[p], n)], axis=0), s[n][p])
            for p in pairs] for n in seqs]
    if NB == 1:
        x_state = [ars[0][p][0:C] for p in pairs]
        o_state = [ars[0][p][C:2 * C] for p in pairs]
    else:
        x_state = [jnp.concatenate([ars[n][p][0:C] for n in seqs], axis=0) for p in pairs]
        o_state = [jnp.concatenate([ars[n][p][C:2 * C] for n in seqs], axis=0) for p in pairs]
    pk = _Packed(R, C)
    stack_mask = (_iota((2 * R, LANE), 0) < R) == (_iota((2 * R, LANE), 1) < HEAD_B)

    def by_head(x):
        return jnp.where(stack_mask, jnp.concatenate([x, x], axis=0), 0.0)

    ar = [jnp.concatenate([atp[p], rtp[p]], axis=0) for p in pairs]
    ab = [_bdot_nt(ar[p], by_head(col(bt, p))) for p in pairs]
    ak = [_bdot_nt(ar[p], by_head(col(kt, p))) for p in pairs]
    tinv = pk.inverse_many([jnp.where(pk.strict, m[0:R], 0.0) for m in ab])
    v_bh = [by_head(col(xv, p)) for p in pairs]
    y = [x_state[p] + _bdot(jnp.where(pk.strict, ak[p][0:R], 0.0), v_bh[p]) for p in pairs]
    u = [_bdot(tinv[p], by_head(y[p])) for p in pairs]
    o = [o_state[p] + _bdot(jnp.where(pk.incl, ab[p][R:2 * R], 0.0), by_head(u[p]))
         + _bdot(jnp.where(pk.incl, ak[p][R:2 * R], 0.0), v_bh[p]) for p in pairs]
    for n in seqs:
        for p in pairs:
            uv = jnp.concatenate([seq_rows(u[p], n), seq_rows(col(xv, p), n)], axis=0)
            bkhat = jnp.concatenate([seq_rows(col(bhat, p), n), seq_rows(col(khat, p), n)], axis=0)
            decay = col(p_tot, p) if NB == 1 else col(p_tot, p)[n * C:n * C + 1]
            s_new = s[n][p] * decay + _bdot_tn(uv, bkhat)
            st_ref[n * G + p] = jnp.where(same_head, s_new, 0.0)

    o_rows = jnp.concatenate(o, axis=0)
    mean = _bdot(o_rows, ones_bd) * (1.0 / HEAD_B)
    d = o_rows - mean
    var = _bdot(d * d, ones_bd) * (1.0 / HEAD_B)
    on = to_cols(d * lax.rsqrt(var + GN_EPS)) * lnw_ref[...] + lnb_ref[...]
    bonus = to_cols(_bdot(to_rows(xr * k2 * rk_ref[...]), ones_bd)) * xv
    o_ref[...] = ((on + bonus) * gate).astype(BF16)

    if NB == 1:
        cr_ref[...] = r_ref[C - SUB:C, :]
        ck_ref[...] = k_ref[C - SUB:C, :]
        cv_ref[...] = v_ref[C - SUB:C, :]
        csm_ref[...] = sm_ref[C - SUB:C, :]

    @pl.when(ci == nch - 1)
    def _():
        for n in seqs:
            for p in pairs:
                s_ref[n, p] = _xdot(st_ref[n * G + p], gather, 3)


def _rwkv_mixer(proj, row0, nb, seq, C, NB, G, prev_rkv, prev_sm, s0, mu_rkv, mu_sm, w2p, a2p, g2p,
                w0, a0, k_k, k_a, r_k, lnw, lnb):
    nch = seq // C
    assert NB == 1 or (nch == 1 and C == SUB)
    R = NB * C
    ng = N_PAIR // G
    gw = G * LANE
    blk0 = row0 // R
    bcast = s0.shape[0] == 1
    bsel = (lambda b: 0) if bcast else (lambda b: b)

    def proj_spec(col0):
        return pl.BlockSpec((R, gw), lambda b, g, c: (blk0 + b * nch + c, col0 // gw + g))

    def prev_spec(part):
        return pl.BlockSpec((NB * SUB, gw), lambda b, g, c: (bsel(b), part * (W_B // gw) + g))

    def vec_spec(part=0):
        return pl.BlockSpec((1, gw), lambda b, g, c: (0, part * (W_B // gw) + g))

    in_specs = [
        proj_spec(RKV0), proj_spec(RKV0 + W_B), proj_spec(RKV0 + 2 * W_B),
        pl.BlockSpec((R, SM_W), lambda b, g, c: (blk0 + b * nch + c, SM0 // SM_W)),
        prev_spec(0), prev_spec(1), prev_spec(2),
        pl.BlockSpec((NB * SUB, SM_W), lambda b, g, c: (bsel(b), 0)),
        pl.BlockSpec((NB, G, LANE, HEAD_B), lambda b, g, c: (bsel(b), g, 0, 0)),
        vec_spec(0), vec_spec(1), vec_spec(2),
        pl.BlockSpec((1, SM_W), lambda b, g, c: (0, 0)),
        pl.BlockSpec((LANE, gw), lambda b, g, c: (0, g)),
        pl.BlockSpec((LANE, gw), lambda b, g, c: (0, g)),
        pl.BlockSpec((2 * LANE, gw), lambda b, g, c: (0, g)),
        vec_spec(), vec_spec(), vec_spec(), vec_spec(), vec_spec(), vec_spec(), vec_spec(),
    ]
    return pl.pallas_call(
        functools.partial(_rwkv_kernel, C=C, NB=NB, G=G),
        out_shape=(jax.ShapeDtypeStruct((nb * seq, W_B), BF16),
                   jax.ShapeDtypeStruct((nb, N_PAIR, LANE, HEAD_B), F32)),
        grid=(nb // NB, ng, nch),
        in_specs=in_specs,
        out_specs=(pl.BlockSpec((R, gw), lambda b, g, c: (b * nch + c, g)),
                   pl.BlockSpec((NB, G, LANE, HEAD_B), lambda b, g, c: (b, g, 0, 0))),
        scratch_shapes=[pltpu.VMEM((SUB, gw), F32), pltpu.VMEM((SUB, gw), F32), pltpu.VMEM((SUB, gw), F32),
                        pltpu.VMEM((SUB, SM_W), F32), pltpu.VMEM((NB * G, LANE, LANE), F32)],
        compiler_params=pltpu.CompilerParams(
            dimension_semantics=("parallel", "parallel", "arbitrary"), vmem_limit_bytes=VMEM_LIMIT),
        name="rwkv_mixer",
    )(proj, proj, proj, proj, prev_rkv, prev_rkv, prev_rkv, prev_sm, s0,
      mu_rkv, mu_rkv, mu_rkv, mu_sm, w2p, a2p, g2p, w0, a0, k_k, k_a, r_k, lnw, lnb)


_AT, _RT, _BT, _KT, _BHAT, _KHAT, _XV, _GATE, _BONUS, _N_PRE = range(10)


def _rwkv_seq_kernel(r0_ref, k0_ref, v0_ref, sm0_ref, rn_ref, kn_ref, vn_ref, smn_ref,
                     pr_ref, pk_ref, pv_ref, psm_ref, s0_ref,
                     mur_ref, muk_ref, muv_ref, musm_ref, w2_ref, a2_ref, g2_ref,
                     w0_ref, a0_ref, kk_ref, ka_ref, rk_ref, lnw_ref, lnb_ref,
                     o_ref, s_ref, cr_ref, ck_ref, cv_ref, csm_ref, st_ref, pre_ref, ptot_ref, *, C, G):
    ci = pl.program_id(2)
    nch = pl.num_programs(2)
    pairs = range(G)

    r2 = _iota((LANE, LANE), 0)
    c2 = _iota((LANE, LANE), 1)
    same_head = (r2 < HEAD_B) == (c2 < HEAD_B)
    ones_bd = same_head.astype(F32)
    spread = (_iota((HEAD_B, LANE), 0) == (_iota((HEAD_B, LANE), 1) & (HEAD_B - 1))).astype(F32)
    gather = ((_iota((LANE, HEAD_B), 0) & (HEAD_B - 1)) == _iota((LANE, HEAD_B), 1)).astype(F32)
    tri, _ = _seq_masks(C, C)

    def col(x, p):
        return x[:, p * LANE:(p + 1) * LANE]

    def to_rows(x):
        return jnp.concatenate([col(x, p) for p in pairs], axis=0)

    def to_cols(y):
        return jnp.concatenate([y[p * C:(p + 1) * C] for p in pairs], axis=1)

    def lerp(x, hist, mu):
        return x + (_shifted(x, hist, 1) - x) * mu

    def preamble(r_ref, k_ref, v_ref, sm_ref, hr_ref, hk_ref, hv_ref, hsm_ref, slot):
        xs = lerp(sm_ref[...], hsm_ref[...], musm_ref[...])
        yield
        slab_wa = xs[:, SM_WA:SM_WA + LANE]
        wl = _bdot(jnp.tanh(slab_wa), w2_ref[...])
        yield
        al = _bdot(slab_wa, a2_ref[...])
        yield
        pre_ref[slot, _GATE] = _bdot(_sigmoid(xs[:, SM_G:SM_G + 2 * LANE]), g2_ref[...])
        yield
        xr = lerp(r_ref[...], hr_ref[...], mur_ref[...])
        yield
        xk = lerp(k_ref[...], hk_ref[...], muk_ref[...])
        yield
        xv = lerp(v_ref[...], hv_ref[...], muv_ref[...])
        pre_ref[slot, _XV] = xv
        yield
        ld = -jnp.exp(-_softplus(-(w0_ref[...] + wl)) - 0.5)
        yield
        lp = _xdot_r(tri, ld, 3)
        yield
        aa = _sigmoid(a0_ref[...] + al)
        yield
        kkr = xk * kk_ref[...]
        ss = to_cols(_bdot(to_rows(kkr * kkr), ones_bd))
        yield
        kkn = kkr * lax.rsqrt(ss + 1e-6)
        k2 = xk * (1.0 + (aa - 1.0) * ka_ref[...])
        yield
        pre_ref[slot, _BONUS] = to_cols(_bdot(to_rows(xr * k2 * rk_ref[...]), ones_bd)) * xv
        yield
        pre_ref[slot, _AT] = -kkn * jnp.exp(lp - ld)
        yield
        pre_ref[slot, _RT] = xr * jnp.exp(lp)
        yield
        kb = kkn * aa
        e_neg = jnp.exp(-lp)
        pre_ref[slot, _BT] = kb * e_neg
        yield
        pre_ref[slot, _KT] = k2 * e_neg
        yield
        lp_tot = lp[C - 1:C, :]
        e_rem = jnp.exp(lp_tot - lp)
        pre_ref[slot, _BHAT] = kb * e_rem
        yield
        pre_ref[slot, _KHAT] = k2 * e_rem
        ptot_ref[slot] = jnp.broadcast_to(jnp.exp(lp_tot), (SUB, lp.shape[1]))
        yield
        cr_ref[...] = r_ref[C - SUB:C, :]
        ck_ref[...] = k_ref[C - SUB:C, :]
        cv_ref[...] = v_ref[C - SUB:C, :]
        csm_ref[...] = sm_ref[C - SUB:C, :]

    @pl.when(ci == 0)
    def _():
        for _ in preamble(r0_ref, k0_ref, v0_ref, sm0_ref, pr_ref, pk_ref, pv_ref, psm_ref, 0):
            pass
        for p in pairs:
            st_ref[p] = jnp.where(same_head, _xdot(s0_ref[0, p], spread, 3), 0.0)

    slot = ci % 2
    ahead = preamble(rn_ref, kn_ref, vn_ref, smn_ref, cr_ref, ck_ref, cv_ref, csm_ref, 1 - slot)
    calls = [0]

    def tick():
        calls[0] += 1
        if calls[0] % 6 == 0:
            next(ahead, None)

    def each(fn, items):
        out = []
        for it in items:
            out.append(fn(it))
            tick()
        return out

    def pre(idx, p):
        return pre_ref[slot, idx, :, p * LANE:(p + 1) * LANE]

    pk = _Packed(C, C)
    stack_mask = (_iota((2 * C, LANE), 0) < C) == (_iota((2 * C, LANE), 1) < HEAD_B)

    def by_head(x):
        return jnp.where(stack_mask, jnp.concatenate([x, x], axis=0), 0.0)

    s = [st_ref[p] for p in pairs]
    ar = [jnp.concatenate([pre(_AT, p), pre(_RT, p)], axis=0) for p in pairs]
    ars = each(lambda p: _bdot_nt(ar[p], s[p]), pairs)
    ab = each(lambda p: _bdot_nt(ar[p], by_head(pre(_BT, p))), pairs)
    ak = each(lambda p: _bdot_nt(ar[p], by_head(pre(_KT, p))), pairs)
    ts = [pk.eye + jnp.where(pk.strict, m[0:C], 0.0) for m in ab]
    ps = [jnp.where(pk.strict, m[0:C], 0.0) for m in ab]
    n = 1
    while 2 * n < C:
        ps = each(lambda p: _bdot(ps[p], pk.block_diag(ps[p])), pairs)
        ts = each(lambda p: ts[p] + _bdot(ps[p], pk.block_diag(ts[p])), pairs)
        n *= 2
    xv = [pre(_XV, p) for p in pairs]
    v_bh = [by_head(xv[p]) for p in pairs]
    y = each(lambda p: ars[p][0:C] + _bdot(jnp.where(pk.strict, ak[p][0:C], 0.0), v_bh[p]), pairs)
    u = each(lambda p: _bdot(ts[p], by_head(y[p])), pairs)
    o = each(lambda p: ars[p][C:2 * C] + _bdot(jnp.where(pk.incl, ab[p][C:2 * C], 0.0), by_head(u[p]))
             + _bdot(jnp.where(pk.incl, ak[p][C:2 * C], 0.0), v_bh[p]), pairs)
    for p in pairs:
        uv = jnp.concatenate([u[p], xv[p]], axis=0)
        bkhat = jnp.concatenate([pre(_BHAT, p), pre(_KHAT, p)], axis=0)
        s_new = s[p] * ptot_ref[slot, 0:1, p * LANE:(p + 1) * LANE] + _bdot_tn(uv, bkhat)
        st_ref[p] = jnp.where(same_head, s_new, 0.0)
        tick()

    o_rows = jnp.concatenate(o, axis=0)
    mean = _bdot(o_rows, ones_bd) * (1.0 / HEAD_B)
    d = o_rows - mean
    var = _bdot(d * d, ones_bd) * (1.0 / HEAD_B)
    on = to_cols(d * lax.rsqrt(var + GN_EPS)) * lnw_ref[...] + lnb_ref[...]
    o_ref[...] = ((on + pre_ref[slot, _BONUS]) * pre_ref[slot, _GATE]).astype(BF16)
    for _ in ahead:
        pass

    @pl.when(ci == nch - 1)
    def _():
        for p in pairs:
            s_ref[0, p] = _xdot(st_ref[p], gather, 3)


def _rwkv_seq_mixer(proj, nb, seq, C, prev_rkv, prev_sm, s0, mu_rkv, mu_sm, w2p, a2p, g2p,
                    w0, a0, k_k, k_a, r_k, lnw, lnb):
    G = N_PAIR
    nch = seq // C
    gw = G * LANE

    def first_spec(col0, w):
        return pl.BlockSpec((C, w), lambda b, g, c: (b * nch, col0 // w))

    def next_spec(col0, w):
        return pl.BlockSpec((C, w), lambda b, g, c: (b * nch + jnp.minimum(c + 1, nch - 1), col0 // w))

    def prev_spec(part):
        return pl.BlockSpec((SUB, gw), lambda b, g, c: (0, part))

    def vec_spec(part=0):
        return pl.BlockSpec((1, gw), lambda b, g, c: (0, part))

    in_specs = [
        first_spec(RKV0, gw), first_spec(RKV0 + W_B, gw), first_spec(RKV0 + 2 * W_B, gw), first_spec(SM0, SM_W),
        next_spec(RKV0, gw), next_spec(RKV0 + W_B, gw), next_spec(RKV0 + 2 * W_B, gw), next_spec(SM0, SM_W),
        prev_spec(0), prev_spec(1), prev_spec(2),
        pl.BlockSpec((SUB, SM_W), lambda b, g, c: (0, 0)),
        pl.BlockSpec((1, G, LANE, HEAD_B), lambda b, g, c: (0, 0, 0, 0)),
        vec_spec(0), vec_spec(1), vec_spec(2),
        pl.BlockSpec((1, SM_W), lambda b, g, c: (0, 0)),
        pl.BlockSpec((LANE, gw), lambda b, g, c: (0, 0)),
        pl.BlockSpec((LANE, gw), lambda b, g, c: (0, 0)),
        pl.BlockSpec((2 * LANE, gw), lambda b, g, c: (0, 0)),
        vec_spec(), vec_spec(), vec_spec(), vec_spec(), vec_spec(), vec_spec(), vec_spec(),
    ]
    return pl.pallas_call(
        functools.partial(_rwkv_seq_kernel, C=C, G=G),
        out_shape=(jax.ShapeDtypeStruct((nb * seq, W_B), BF16),
                   jax.ShapeDtypeStruct((nb, N_PAIR, LANE, HEAD_B), F32)),
        grid=(nb, 1, nch),
        in_specs=in_specs,
        out_specs=(pl.BlockSpec((C, gw), lambda b, g, c: (b * nch + c, 0)),
                   pl.BlockSpec((1, G, LANE, HEAD_B), lambda b, g, c: (b, 0, 0, 0))),
        scratch_shapes=[pltpu.VMEM((SUB, gw), F32), pltpu.VMEM((SUB, gw), F32), pltpu.VMEM((SUB, gw), F32),
                        pltpu.VMEM((SUB, SM_W), F32), pltpu.VMEM((G, LANE, LANE), F32),
                        pltpu.VMEM((2, _N_PRE, C, gw), F32), pltpu.VMEM((2, SUB, gw), F32)],
        compiler_params=pltpu.CompilerParams(
            dimension_semantics=("parallel", "arbitrary", "arbitrary"), vmem_limit_bytes=VMEM_LIMIT),
        name="rwkv_seq_mixer",
    )(proj, proj, proj, proj, proj, proj, proj, proj, prev_rkv, prev_rkv, prev_rkv, prev_sm, s0,
      mu_rkv, mu_rkv, mu_rkv, mu_sm, w2p, a2p, g2p, w0, a0, k_k, k_a, r_k, lnw, lnb)


def _small_layout(cols_ba, cols_w, cols_a, cols_g, axis=-1):
    def z(n):
        shape = list(cols_w.shape)
        shape[axis] = n
        return jnp.zeros(shape, cols_w.dtype)
    return jnp.concatenate(
        [cols_ba, z(SM_WA - cols_ba.shape[axis]), cols_w, cols_a, cols_g,
         z(SM_W - SM_G - cols_g.shape[axis])], axis=axis)


def _pad_rows8(x):
    b, n, c = x.shape
    return jnp.concatenate([jnp.zeros((b, SUB - n, c), x.dtype), x], axis=1).reshape(b * SUB, c)


def kernel(x_prompt, x_sample, state_delta, state_conv_qkv, state_wkv, state_shift, state_ffn_conv, meta, norm1, w_in, conv_a, a_log, dt_bias, onorm_a, mu_b, w0, w2, a0, a2, g2, k_k, k_a, r_k, lnx_w, lnx_b, w_o, norm2, w_ffn_in, conv_f, w_ffn_out, norm_f):
    nbp, seq_p, _ = x_prompt.shape
    nbs, seq_s, _ = x_sample.shape
    n_s = nbs * seq_s
    assert w_in.shape[0] == 1, "single-layer trunk"
    assert seq_s == SUB
    l = 0

    wt = w_in[l].T
    o_b = A_PROJ
    o_l = A_PROJ + 3 * W_B
    w_cat_t = jnp.concatenate([
        wt[:4 * W_A], wt[o_b:o_l],
        _small_layout(wt[4 * W_A:A_PROJ], wt[o_l:o_l + W_LORA], wt[o_l + W_LORA:o_l + W_LORA + A_LORA],
                      wt[o_l + W_LORA + A_LORA:], axis=0)], axis=0).astype(BF16)
    mu = mu_b[l]
    mu_rkv = mu[None, :3 * W_B]
    mu_sm = _small_layout(jnp.zeros((1, 2 * H_A), F32), mu[None, 3 * W_B:3 * W_B + W_LORA],
                          mu[None, 3 * W_B + W_LORA:3 * W_B + W_LORA + A_LORA],
                          mu[None, 3 * W_B + W_LORA + A_LORA:])
    w2p = jnp.concatenate([w2[l], jnp.zeros((LANE - W_LORA, W_B), F32)], axis=0)
    a2p = jnp.concatenate([jnp.zeros((W_LORA, W_B), F32), a2[l]], axis=0)
    g2p = jnp.concatenate([g2[l], jnp.zeros((2 * LANE - G_LORA, W_B), F32)], axis=0)
    alog_row = jnp.concatenate([jnp.zeros((H_A,), F32), a_log[l], jnp.zeros((LANE - 2 * H_A,), F32)])[None]
    dtb_row = jnp.concatenate([jnp.zeros((H_A,), F32), dt_bias[l], jnp.zeros((LANE - 2 * H_A,), F32)])[None]
    wo_bf = w_o[l].astype(BF16)
    wfo_bf = w_ffn_out[l].astype(BF16)
    row = lambda v: v.reshape(1, -1)

    def mix(proj, row0, nb, seq, C, NB, prev_qkv, prev_rkv, prev_sm, s_delta, s_wkv):
        oa, sd = _delta_mixer(proj, row0, nb, seq, C, NB, prev_qkv, s_delta, conv_a[l], alog_row, dtb_row,
                              row(onorm_a[l]))
        rwkv_params = (mu_rkv, mu_sm, w2p, a2p, g2p, row(w0[l]), row(a0[l]), row(k_k[l]), row(k_a[l]),
                       row(r_k[l]), row(lnx_w[l]), row(lnx_b[l]))
        if seq // C > 1:
            assert row0 == 0 and NB == 1
            ob, sw = _rwkv_seq_mixer(proj, nb, seq, C, prev_rkv, prev_sm, s_wkv, *rwkv_params)
        else:
            ob, sw = _rwkv_mixer(proj, row0, nb, seq, C, NB, N_PAIR, prev_rkv, prev_sm, s_wkv, *rwkv_params)
        return oa, ob, sd, sw

    xs_rows = jnp.concatenate([x_sample.reshape(n_s, D_MODEL), meta], axis=0)
    xp_rows = x_prompt.reshape(nbp * seq_p, D_MODEL)
    n_small = n_s + N_META
    proj_s = _norm_matmul(xs_rows, row(norm1[l]), w_cat_t, n_small, 1536)
    proj_p = _norm_matmul(xp_rows, row(norm1[l]), w_cat_t, 1024, 1536)

    zeros = lambda *s: jnp.zeros(s, F32)
    oa_m, ob_m, sd_m, sw_m = mix(proj_s, n_s, 1, N_META, N_META, 1, zeros(SUB, 3 * W_A),
                                 zeros(SUB, 3 * W_B), zeros(SUB, SM_W),
                                 zeros(1, H_A, HEAD_A, HEAD_A), zeros(1, N_PAIR, LANE, HEAD_B))
    tail = proj_s[n_small - SUB:n_small]
    oa_p, ob_p, sd_p, sw_p = mix(proj_p, 0, nbp, seq_p, 64, 1, tail[:, :3 * W_A],
                                 tail[:, RKV0:SM0], tail[:, SM0:], sd_m, sw_m)
    sh = state_shift[l]
    sh_sm = _small_layout(jnp.zeros((nbs, 1, 2 * H_A), F32), sh[..., 3 * W_B:3 * W_B + W_LORA],
                          sh[..., 3 * W_B + W_LORA:3 * W_B + W_LORA + A_LORA],
                          sh[..., 3 * W_B + W_LORA + A_LORA:])
    oa_s, ob_s, sd_s, sw_s = mix(proj_s, 0, nbs, seq_s, seq_s, 8, _pad_rows8(state_conv_qkv[l]),
                                 _pad_rows8(sh[..., :3 * W_B]), _pad_rows8(sh_sm), state_delta[l],
                                 state_wkv[l].reshape(nbs, N_PAIR, LANE, HEAD_B))

    oa_small = jnp.concatenate([oa_s, oa_m], axis=0)
    ob_small = jnp.concatenate([ob_s, ob_m], axis=0)
    x1_s, h2_s = _out_proj(xs_rows, oa_small, ob_small, wo_bf, row(norm2[l]), n_small // 5)
    x1_p, h2_p = _out_proj(xp_rows, oa_p, ob_p, wo_bf, row(norm2[l]), 512)
    prev_rows = jnp.concatenate([_pad_rows8(state_ffn_conv[l]), jnp.zeros((N_META, D_FF), F32)], axis=0)
    act_s, gate_s = _ffn_in(h2_s, w_ffn_in[l], prev_rows, conv_f[l], n_small, 768, 3, 0, n_s + SUB)
    act_p, tail_p = _ffn_in(h2_p, w_ffn_in[l], gate_s[None, n_small - SUB:n_small], conv_f[l], 1024, 768, 3,
                            seq_p // 1024)
    y_s = _ffn_out(act_s, wfo_bf, x1_s, row(norm_f), n_s, 1024, 768)
    y_p = _ffn_out(act_p, wfo_bf, x1_p, row(norm_f), nbp * seq_p, 1024, 768)

    def states(proj, gate_tail, nb, seq, sd, sw):
        p3 = proj.reshape(-1, seq, P_CAT)
        conv_new = p3[:nb, seq - 3:, :3 * W_A]
        last = p3[:nb, seq - 1:, :]
        shift_new = jnp.concatenate([last[..., RKV0:SM0], last[..., SM0 + SM_WA:SM0 + SM_WA + W_LORA + A_LORA],
                                     last[..., SM0 + SM_G:SM0 + SM_G + G_LORA]], axis=-1)
        ffn_new = gate_tail[:nb, SUB - 2:, :]
        return (sd[None], conv_new[None], sw.reshape(nb, H_B, HEAD_B, HEAD_B)[None], shift_new[None],
                ffn_new[None])

    return ((y_p.reshape(nbp, seq_p, D_MODEL), y_s.reshape(nbs, seq_s, D_MODEL))
            + states(proj_p, tail_p, nbp, seq_p, sd_p, sw_p)
            + states(proj_s, gate_s.reshape(-1, SUB, D_FF), nbs, seq_s, sd_s, sw_s))
```

```python
import functools

import jax
import jax.numpy as jnp
from jax import lax
from jax.experimental import pallas as pl
from jax.experimental.pallas import tpu as pltpu

F32 = jnp.float32
BF16 = jnp.bfloat16

D_MODEL = 2048
N_META = 16
W_A = 1024
HEAD_A = 128
H_A = 8
W_B = 1024
HEAD_B = 64
H_B = 16
N_PAIR = H_B // 2
W_LORA = 64
A_LORA = 64
G_LORA = 160
D_FF = 5376
RMS_EPS = 1e-6
GN_EPS = 64e-5
A_PROJ = 4 * W_A + 2 * H_A
B_PROJ = 3 * W_B + W_LORA + A_LORA + G_LORA

QKVZ0 = 0
RKV0 = 4 * W_A
SM0 = RKV0 + 3 * W_B
SM_W = 512
SM_BA = 0
SM_WA = 128
SM_G = 256
P_CAT = SM0 + SM_W

LANE = 128
SUB = 8
VMEM_LIMIT = 48 * 1024 * 1024
VMEM_LIMIT_BIG = 58 * 1024 * 1024
NEG_BIG = -1e30

NT_DIMS = (((1,), (1,)), ((), ()))


def _bdot(a, b):
    return jnp.dot(a.astype(BF16), b.astype(BF16), preferred_element_type=F32)


def _bdot_nt(a, b):
    return lax.dot_general(a.astype(BF16), b.astype(BF16), NT_DIMS, preferred_element_type=F32)


def _bdot_tn(a, b):
    return lax.dot_general(a.astype(BF16), b.astype(BF16), (((0,), (0,)), ((), ())),
                           preferred_element_type=F32)


def _pieces(a, n):
    out = []
    rem = a
    for i in range(n):
        p = rem.astype(BF16)
        out.append(p)
        if i + 1 < n:
            rem = rem - p.astype(F32)
    return out


def _xdot(a, b, n, dims=(((1,), (0,)), ((), ()))):
    bb = b.astype(BF16)
    acc = None
    for p in _pieces(a, n):
        t = lax.dot_general(p, bb, dims, preferred_element_type=F32)
        acc = t if acc is None else acc + t
    return acc


def _xdot_r(a, b, n, dims=(((1,), (0,)), ((), ()))):
    ab = a.astype(BF16)
    acc = None
    for p in _pieces(b, n):
        t = lax.dot_general(ab, p, dims, preferred_element_type=F32)
        acc = t if acc is None else acc + t
    return acc


def _sigmoid(x):
    return 1.0 / (1.0 + jnp.exp(-x))


def _silu(x):
    return x * _sigmoid(x)


def _softplus(x):
    return jnp.maximum(x, 0.0) + jnp.log(1.0 + jnp.exp(-jnp.abs(x)))


def _iota(shape, dim):
    return lax.broadcasted_iota(jnp.int32, shape, dim)


def _shifted(x, prev8, k):
    n = x.shape[0]
    xr = pltpu.roll(x, k, 0)
    pr = pltpu.roll(prev8, k, 0)
    first = jnp.where(_iota((SUB, x.shape[1]), 0) < k, pr, xr[0:SUB])
    if n == SUB:
        return first
    return jnp.concatenate([first, xr[SUB:]], axis=0)


def _delayed(x, hist, k, nseq):
    if nseq == 1:
        return _shifted(x, hist, k)
    n = x.shape[0]
    use_hist = (_iota(x.shape, 0) & (SUB - 1)) < k
    return jnp.where(use_hist, pltpu.roll(hist, n + k - SUB, 0), pltpu.roll(x, k, 0))


def _same_seq(shape, c, rows_total):
    if c == rows_total:
        return None
    sh = c.bit_length() - 1
    return (lax.shift_right_logical(_iota(shape, 0), sh)
            == lax.shift_right_logical(_iota(shape, 1) & (rows_total - 1), sh))


class _Packed:
    def __init__(self, r, c):
        self.r = r
        self.c = c
        row = _iota((r, 2 * r), 0)
        col = _iota((r, 2 * r), 1) & (r - 1)
        same = _same_seq((r, 2 * r), c, r)
        self.incl = (row >= col) if same is None else ((row >= col) & same)
        self.strict = (row > col) if same is None else ((row > col) & same)
        self.eye = (row == col).astype(F32)
        self.first = _iota((r, 2 * r), 1) < r
        self.bd_mask = (_iota((2 * r, 2 * r), 0) < r) == (_iota((2 * r, 2 * r), 1) < r)

    def block_diag(self, m):
        return jnp.where(self.bd_mask, jnp.concatenate([m, m], axis=0), 0.0)

    def inverse_many(self, xs):
        ts = [self.eye + x for x in xs]
        ps = list(xs)
        n = 1
        while 2 * n < self.c:
            ps = [_bdot(p, self.block_diag(p)) for p in ps]
            ts = [t + _bdot(p, self.block_diag(t)) for p, t in zip(ps, ts)]
            n *= 2
        return ts


def _seq_masks(r, c):
    row = _iota((r, r), 0)
    col = _iota((r, r), 1)
    same = _same_seq((r, r), c, r)
    if same is None:
        return (row >= col).astype(F32), None
    return ((row >= col) & same).astype(F32), same.astype(F32)


def _norm_matmul_kernel(x_ref, g_ref, w_ref, o_ref, h_ref):
    @pl.when(pl.program_id(1) == 0)
    def _():
        x = x_ref[...]
        ms = jnp.mean(x * x, axis=-1, keepdims=True)
        h_ref[...] = (x * lax.rsqrt(ms + RMS_EPS) * g_ref[...]).astype(BF16)

    o_ref[...] = lax.dot_general(h_ref[...], w_ref[...], NT_DIMS, preferred_element_type=F32)


def _norm_matmul(x, g, wt, tm, tn):
    m, k = x.shape
    n = wt.shape[0]
    return pl.pallas_call(
        _norm_matmul_kernel,
        out_shape=jax.ShapeDtypeStruct((m, n), F32),
        grid=(m // tm, n // tn),
        in_specs=[
            pl.BlockSpec((tm, k), lambda i, j: (i, 0)),
            pl.BlockSpec((1, k), lambda i, j: (0, 0)),
            pl.BlockSpec((tn, k), lambda i, j: (j, 0)),
        ],
        out_specs=pl.BlockSpec((tm, tn), lambda i, j: (i, j)),
        scratch_shapes=[pltpu.VMEM((tm, k), BF16)],
        compiler_params=pltpu.CompilerParams(
            dimension_semantics=("parallel", "arbitrary"), vmem_limit_bytes=VMEM_LIMIT_BIG),
        name="norm_matmul",
    )(x, g, wt)


def _out_proj_kernel(x_ref, oa_ref, ob_ref, wt_ref, wb_ref, g_ref, x1_ref, h_ref):
    acc = jnp.dot(oa_ref[...], wt_ref[...], preferred_element_type=F32)
    acc = acc + jnp.dot(ob_ref[...], wb_ref[...], preferred_element_type=F32)
    x1 = x_ref[...] + acc
    x1_ref[...] = x1
    ms = jnp.mean(x1 * x1, axis=-1, keepdims=True)
    h_ref[...] = (x1 * lax.rsqrt(ms + RMS_EPS) * g_ref[...]).astype(BF16)


def _out_proj(x, oa, ob, wo, g, tm):
    m, d = x.shape
    return pl.pallas_call(
        _out_proj_kernel,
        out_shape=(jax.ShapeDtypeStruct((m, d), F32), jax.ShapeDtypeStruct((m, d), BF16)),
        grid=(m // tm,),
        in_specs=[
            pl.BlockSpec((tm, d), lambda i: (i, 0)),
            pl.BlockSpec((tm, W_A), lambda i: (i, 0)),
            pl.BlockSpec((tm, W_B), lambda i: (i, 0)),
            pl.BlockSpec((W_A, d), lambda i: (0, 0)),
            pl.BlockSpec((W_B, d), lambda i: (W_A // W_B, 0)),
            pl.BlockSpec((1, d), lambda i: (0, 0)),
        ],
        out_specs=(pl.BlockSpec((tm, d), lambda i: (i, 0)), pl.BlockSpec((tm, d), lambda i: (i, 0))),
        compiler_params=pltpu.CompilerParams(
            dimension_semantics=("parallel",), vmem_limit_bytes=VMEM_LIMIT),
        name="out_proj",
    )(x, oa, ob, wo, wo, g)


def _ffn_in_kernel(h_ref, wg_ref, wu_ref, prev_ref, cw_ref, act_ref, tail_ref, wgb_ref, wub_ref, carry_ref,
                   *, tiles_per_seq, state_rows, nsub):
    m = pl.program_id(1)

    @pl.when(m == 0)
    def _():
        wgb_ref[...] = wg_ref[...].astype(BF16)
        wub_ref[...] = wu_ref[...].astype(BF16)

    if tiles_per_seq:
        @pl.when(m % tiles_per_seq == 0)
        def _():
            carry_ref[...] = prev_ref[...]

    h = h_ref[...]
    n = h.shape[0]
    w = wgb_ref.shape[1] // nsub
    for i in range(nsub):
        cs = slice(i * w, (i + 1) * w)
        gate = jnp.dot(h, wgb_ref[:, cs], preferred_element_type=F32)
        up = jnp.dot(h, wub_ref[:, cs], preferred_element_type=F32)
        if tiles_per_seq:
            prev8 = carry_ref[:, cs]
            d1 = _shifted(gate, prev8, 1)
            d2 = _shifted(gate, prev8, 2)
            carry_ref[:, cs] = gate[n - SUB:n]
            tail_ref[:, cs] = gate[n - SUB:n]
        else:
            row = _iota(gate.shape, 0)
            prev = prev_ref[:, cs]
            tail_ref[:, cs] = gate

            def delayed(k):
                use_prev = ((row & (SUB - 1)) < k) & (row < state_rows)
                return jnp.where(use_prev, pltpu.roll(prev, n + k - SUB, 0), pltpu.roll(gate, k, 0))

            d1 = delayed(1)
            d2 = delayed(2)
        y = gate * cw_ref[2:3, cs] + d1 * cw_ref[1:2, cs] + d2 * cw_ref[0:1, cs]
        act_ref[:, cs] = (_silu(y) * up).astype(BF16)


def _ffn_in(h, w, prev, cw, tm, tn, nsub, tiles_per_seq, state_rows=0):
    m, d = h.shape
    nj = D_FF // tn
    nm = m // tm
    if tiles_per_seq:
        nseq = nm // tiles_per_seq
        prev_spec = pl.BlockSpec((None, SUB, tn), lambda j, i: (0, 0, j))
        tail_shape = jax.ShapeDtypeStruct((nseq, SUB, D_FF), F32)
        tail_spec = pl.BlockSpec((None, SUB, tn), lambda j, i: (i // tiles_per_seq, 0, j))
    else:
        prev_spec = pl.BlockSpec((tm, tn), lambda j, i: (i, j))
        tail_shape = jax.ShapeDtypeStruct((m, D_FF), F32)
        tail_spec = pl.BlockSpec((tm, tn), lambda j, i: (i, j))
    return pl.pallas_call(
        functools.partial(_ffn_in_kernel, tiles_per_seq=tiles_per_seq, state_rows=state_rows, nsub=nsub),
        out_shape=(jax.ShapeDtypeStruct((m, D_FF), BF16), tail_shape),
        grid=(nj, nm),
        in_specs=[
            pl.BlockSpec((tm, d), lambda j, i: (i, 0)),
            pl.BlockSpec((d, tn), lambda j, i: (0, j)),
            pl.BlockSpec((d, tn), lambda j, i: (0, nj + j)),
            prev_spec,
            pl.BlockSpec((3, tn), lambda j, i: (0, j)),
        ],
        out_specs=(pl.BlockSpec((tm, tn), lambda j, i: (i, j)), tail_spec),
        scratch_shapes=[pltpu.VMEM((d, tn), BF16), pltpu.VMEM((d, tn), BF16), pltpu.VMEM((SUB, tn), F32)],
        compiler_params=pltpu.CompilerParams(
            dimension_semantics=("parallel", "arbitrary"), vmem_limit_bytes=VMEM_LIMIT_BIG),
        name="ffn_in",
    )(h, w, w, prev, cw)


def _ffn_out_kernel(a_ref, w_ref, x_ref, g_ref, o_ref):
    kk = pl.program_id(1)

    @pl.when(kk == 0)
    def _():
        o_ref[...] = x_ref[...]

    o_ref[...] += jnp.dot(a_ref[...], w_ref[...], preferred_element_type=F32)

    @pl.when(kk == pl.num_programs(1) - 1)
    def _():
        x = o_ref[...]
        ms = jnp.mean(x * x, axis=-1, keepdims=True)
        o_ref[...] = x * lax.rsqrt(ms + RMS_EPS) * g_ref[...]


def _ffn_out(act, w, x1, g, m, tm, tk):
    kdim, d = w.shape
    return pl.pallas_call(
        _ffn_out_kernel,
        out_shape=jax.ShapeDtypeStruct((m, d), F32),
        grid=(m // tm, kdim // tk),
        in_specs=[
            pl.BlockSpec((tm, tk), lambda i, k: (i, k)),
            pl.BlockSpec((tk, d), lambda i, k: (k, 0)),
            pl.BlockSpec((tm, d), lambda i, k: (i, 0)),
            pl.BlockSpec((1, d), lambda i, k: (0, 0)),
        ],
        out_specs=pl.BlockSpec((tm, d), lambda i, k: (i, 0)),
        compiler_params=pltpu.CompilerParams(
            dimension_semantics=("parallel", "arbitrary"), vmem_limit_bytes=VMEM_LIMIT),
        name="ffn_out",
    )(act, w, x1, g)


def _delta_kernel(qkvz_ref, ba_ref, prev_ref, s0_ref, cw_ref, alog_ref, dtb_ref, on_ref,
                  o_ref, s_ref, carry_ref, *, C, NB):
    R = NB * C
    ci = pl.program_id(1)

    @pl.when(ci == 0)
    def _():
        if NB == 1:
            carry_ref[...] = prev_ref[...]
        s_ref[...] = s0_ref[...]

    hist_ref = carry_ref if NB == 1 else prev_ref
    tri, ones_seq = _seq_masks(R, C)

    ba = ba_ref[...]
    beta_full = _sigmoid(ba)
    g_full = -jnp.exp(alog_ref[...]) * _softplus(ba + dtb_ref[...])
    gc_full = _xdot_r(tri, g_full, 3)
    gtot_full = gc_full[C - 1:C, :] if NB == 1 else _xdot_r(ones_seq, g_full, 3)

    def conv_silu(c0):
        x = qkvz_ref[:, c0:c0 + LANE]
        hist = hist_ref[:, c0:c0 + LANE]
        y = x * cw_ref[3:4, c0:c0 + LANE]
        for k in (1, 2, 3):
            y = y + _delayed(x, hist, k, NB) * cw_ref[3 - k:4 - k, c0:c0 + LANE]
        return _silu(y)

    ones_ll = jnp.ones((LANE, LANE), BF16)

    def lane_sum(x):
        return jnp.dot(x.astype(BF16), ones_ll, preferred_element_type=F32)

    def l2n_many(xs):
        sums = [lane_sum(x * x) for x in xs]
        return [x * lax.rsqrt(sq + 1e-6) for x, sq in zip(xs, sums)]

    def seq_rows(x, n):
        return x[n * C:(n + 1) * C]

    heads = range(H_A)
    seqs = range(NB)
    q = [x * (HEAD_A ** -0.5) for x in l2n_many([conv_silu(h * HEAD_A) for h in heads])]
    k = l2n_many([conv_silu(W_A + h * HEAD_A) for h in heads])
    v = [conv_silu(2 * W_A + h * HEAD_A) for h in heads]
    bcol = [beta_full[:, h:h + 1] for h in heads]
    gcol = [gc_full[:, H_A + h:H_A + h + 1] for h in heads]
    gtot = [gtot_full[:, H_A + h:H_A + h + 1] for h in heads]
    eg = [jnp.exp(gcol[h]) for h in heads]
    kb = [k[h] * bcol[h] for h in heads]

    pk = _Packed(R, C)
    hpairs = range(H_A // 2)
    zc = jnp.zeros((R, HEAD_A), F32)
    row2 = _iota((2 * R, LANE), 0)
    lane2 = _iota((2 * R, LANE), 1)
    ones_cl = jnp.ones((R, LANE), F32)
    gc2 = jnp.concatenate([gc_full, gc_full], axis=0)
    kq, gamma = [], []
    for hp in hpairs:
        h0, h1 = 2 * hp, 2 * hp + 1
        lhs = jnp.concatenate([jnp.concatenate([kb[h0], kb[h1]], axis=1),
                               jnp.concatenate([q[h0], q[h1]], axis=1)], axis=0)
        rk = jnp.concatenate([jnp.concatenate([k[h0], zc], axis=1),
                              jnp.concatenate([zc, k[h1]], axis=1)], axis=0)
        kq.append(_bdot_nt(lhs, rk))
        sel = lane2 == jnp.where(row2 < R, H_A + h0, H_A + h1)
        grow = _xdot_r(ones_cl, jnp.where(sel, gc2, 0.0), 3, NT_DIMS)
        gcol_p = jnp.where(pk.first, gcol[h0], gcol[h1])
        gamma.append(jnp.exp(jnp.where(pk.incl, gcol_p - grow, NEG_BIG)))
    tinv = pk.inverse_many([-jnp.where(pk.strict, kq[hp][0:R] * gamma[hp], 0.0) for hp in hpairs])
    uw = []
    for hp in hpairs:
        h0, h1 = 2 * hp, 2 * hp + 1
        z2 = jnp.zeros((R, 2 * HEAD_A), F32)
        rhs = jnp.concatenate([jnp.concatenate([v[h0] * bcol[h0], kb[h0] * eg[h0], z2], axis=1),
                               jnp.concatenate([z2, v[h1] * bcol[h1], kb[h1] * eg[h1]], axis=1)], axis=0)
        uw.append(_bdot(tinv[hp], rhs))
    u = [uw[h // 2][:, (h % 2) * 2 * HEAD_A:(h % 2) * 2 * HEAD_A + HEAD_A] for h in heads]
    w = [uw[h // 2][:, (h % 2) * 2 * HEAD_A + HEAD_A:(h % 2 + 1) * 2 * HEAD_A] for h in heads]
    qd = [q[h] * eg[h] for h in heads]
    s = [[s_ref[n, h] for h in heads] for n in seqs]
    wqs = [[_bdot(jnp.concatenate([seq_rows(w[h], n), seq_rows(qd[h], n)], axis=0), s[n][h]) for h in heads]
           for n in seqs]
    ws = [jnp.concatenate([wqs[n][h][0:C] for n in seqs], axis=0) if NB > 1 else wqs[0][h][0:C] for h in heads]
    qs = [jnp.concatenate([wqs[n][h][C:2 * C] for n in seqs], axis=0) if NB > 1 else wqs[0][h][C:2 * C]
          for h in heads]
    v_new = [u[h] - ws[h] for h in heads]
    o = []
    for hp in hpairs:
        h0, h1 = 2 * hp, 2 * hp + 1
        vn_bd = jnp.concatenate([jnp.concatenate([v_new[h0], zc], axis=1),
                                 jnp.concatenate([zc, v_new[h1]], axis=1)], axis=0)
        op = _bdot(kq[hp][R:2 * R] * gamma[hp], vn_bd)
        o.append(qs[h0] + op[:, :HEAD_A])
        o.append(qs[h1] + op[:, HEAD_A:])
    kd = [k[h] * jnp.exp(gtot[h] - gcol[h]) for h in heads]
    for n in seqs:
        for h in heads:
            glast = gtot[h] if NB == 1 else gtot[h][n * C:n * C + 1]
            s_ref[n, h] = s[n][h] * jnp.exp(glast) + _bdot_tn(seq_rows(kd[h], n), seq_rows(v_new[h], n))
    osq = [lane_sum(o[h] * o[h]) for h in heads]
    for h in heads:
        z = qkvz_ref[:, 3 * W_A + h * HEAD_A:3 * W_A + (h + 1) * HEAD_A]
        oh = o[h] * lax.rsqrt(osq[h] * (1.0 / HEAD_A) + RMS_EPS)
        o_ref[:, h * HEAD_A:(h + 1) * HEAD_A] = (oh * on_ref[...] * _silu(z)).astype(BF16)

    if NB == 1:
        carry_ref[...] = qkvz_ref[C - SUB:C, 0:3 * W_A]


def _delta_mixer(proj, row0, nb, seq, C, NB, prev, s0, cw, alog_row, dtb_row, onorm):
    nch = seq // C
    assert NB == 1 or (nch == 1 and C == SUB)
    R = NB * C
    blk0 = row0 // R
    bcast = s0.shape[0] == 1
    bsel = (lambda b: 0) if bcast else (lambda b: b)
    return pl.pallas_call(
        functools.partial(_delta_kernel, C=C, NB=NB),
        out_shape=(jax.ShapeDtypeStruct((nb * seq, W_A), BF16),
                   jax.ShapeDtypeStruct((nb, H_A, HEAD_A, HEAD_A), F32)),
        grid=(nb // NB, nch),
        in_specs=[
            pl.BlockSpec((R, 4 * W_A), lambda b, c: (blk0 + b * nch + c, 0)),
            pl.BlockSpec((R, LANE), lambda b, c: (blk0 + b * nch + c, SM0 // LANE)),
            pl.BlockSpec((NB * SUB, 3 * W_A), lambda b, c: (bsel(b), 0)),
            pl.BlockSpec((NB, H_A, HEAD_A, HEAD_A), lambda b, c: (bsel(b), 0, 0, 0)),
            pl.BlockSpec((4, 3 * W_A), lambda b, c: (0, 0)),
            pl.BlockSpec((1, LANE), lambda b, c: (0, 0)),
            pl.BlockSpec((1, LANE), lambda b, c: (0, 0)),
            pl.BlockSpec((1, HEAD_A), lambda b, c: (0, 0)),
        ],
        out_specs=(pl.BlockSpec((R, W_A), lambda b, c: (b * nch + c, 0)),
                   pl.BlockSpec((NB, H_A, HEAD_A, HEAD_A), lambda b, c: (b, 0, 0, 0))),
        scratch_shapes=[pltpu.VMEM((SUB, 3 * W_A), F32)],
        compiler_params=pltpu.CompilerParams(
            dimension_semantics=("parallel", "arbitrary"), vmem_limit_bytes=VMEM_LIMIT),
        name="delta_mixer",
    )(proj, proj, prev, s0, cw, alog_row, dtb_row, onorm)


def _rwkv_kernel(r_ref, k_ref, v_ref, sm_ref, pr_ref, pk_ref, pv_ref, psm_ref, s0_ref,
                 mur_ref, muk_ref, muv_ref, musm_ref, w2_ref, a2_ref, g2_ref,
                 w0_ref, a0_ref, kk_ref, ka_ref, rk_ref, lnw_ref, lnb_ref,
                 o_ref, s_ref, cr_ref, ck_ref, cv_ref, csm_ref, st_ref, *, C, NB, G):
    R = NB * C
    ci = pl.program_id(2)
    nch = pl.num_programs(2)
    seqs = range(NB)
    pairs = range(G)

    r2 = _iota((LANE, LANE), 0)
    c2 = _iota((LANE, LANE), 1)
    same_head = (r2 < HEAD_B) == (c2 < HEAD_B)
    ones_bd = same_head.astype(F32)
    spread = (_iota((HEAD_B, LANE), 0) == (_iota((HEAD_B, LANE), 1) & (HEAD_B - 1))).astype(F32)
    gather = ((_iota((LANE, HEAD_B), 0) & (HEAD_B - 1)) == _iota((LANE, HEAD_B), 1)).astype(F32)

    @pl.when(ci == 0)
    def _():
        if NB == 1:
            cr_ref[...] = pr_ref[...]
            ck_ref[...] = pk_ref[...]
            cv_ref[...] = pv_ref[...]
            csm_ref[...] = psm_ref[...]
        for n in seqs:
            for p in pairs:
                st_ref[n * G + p] = jnp.where(same_head, _xdot(s0_ref[n, p], spread, 3), 0.0)

    hr_ref, hk_ref, hv_ref, hsm_ref = ((cr_ref, ck_ref, cv_ref, csm_ref) if NB == 1
                                       else (pr_ref, pk_ref, pv_ref, psm_ref))
    tri, ones_seq = _seq_masks(R, C)

    def lerp(x, hist, mu):
        return x + (_delayed(x, hist, 1, NB) - x) * mu

    sm = sm_ref[...]
    xs = lerp(sm, hsm_ref[...], musm_ref[...])
    slab_wa = xs[:, SM_WA:SM_WA + LANE]
    slab_g = xs[:, SM_G:SM_G + 2 * LANE]

    def col(x, p):
        return x[:, p * LANE:(p + 1) * LANE]

    def to_rows(x):
        return jnp.concatenate([col(x, p) for p in pairs], axis=0)

    def to_cols(y):
        return jnp.concatenate([y[p * R:(p + 1) * R] for p in pairs], axis=1)

    def seq_rows(x, n):
        return x[n * C:(n + 1) * C]

    xr = lerp(r_ref[...], hr_ref[...], mur_ref[...])
    xk = lerp(k_ref[...], hk_ref[...], muk_ref[...])
    xv = lerp(v_ref[...], hv_ref[...], muv_ref[...])
    wlog = -_softplus(-(w0_ref[...] + _bdot(jnp.tanh(slab_wa), w2_ref[...]))) - 0.5
    ld = -jnp.exp(wlog)
    aa = _sigmoid(a0_ref[...] + _bdot(slab_wa, a2_ref[...]))
    gate = _bdot(_sigmoid(slab_g), g2_ref[...])
    kkr = xk * kk_ref[...]
    kkn = kkr * lax.rsqrt(to_cols(_bdot(to_rows(kkr * kkr), ones_bd)) + 1e-6)
    k2 = xk * (1.0 + (aa - 1.0) * ka_ref[...])
    lp = _xdot_r(tri, ld, 3)
    lp_tot = lp[C - 1:C, :] if NB == 1 else _xdot_r(ones_seq, ld, 3)
    e_neg = jnp.exp(-lp)
    e_rem = jnp.exp(lp_tot - lp)
    at = -kkn * jnp.exp(lp - ld)
    rt = xr * jnp.exp(lp)
    kb = kkn * aa
    bt = kb * e_neg
    kt = k2 * e_neg
    bhat = kb * e_rem
    khat = k2 * e_rem
    p_tot = jnp.exp(lp_tot)

    s = [[st_ref[n * G + p] for p in pairs] for n in seqs]
    atp = [col(at, p) for p in pairs]
    rtp = [col(rt, p) for p in pairs]
    ars = [[_bdot_nt(jnp.concatenate([seq_rows(atp[p], n), seq_rows(rtp[p], n)], axis=0), s[n][p])
            for p in pairs] for n in seqs]
    if NB == 1:
        x_state = [ars[0][p][0:C] for p in pairs]
        o_state = [ars[0][p][C:2 * C] for p in pairs]
    else:
        x_state = [jnp.concatenate([ars[n][p][0:C] for n in seqs], axis=0) for p in pairs]
        o_state = [jnp.concatenate([ars[n][p][C:2 * C] for n in seqs], axis=0) for p in pairs]
    pk = _Packed(R, C)
    stack_mask = (_iota((2 * R, LANE), 0) < R) == (_iota((2 * R, LANE), 1) < HEAD_B)

    def by_head(x):
        return jnp.where(stack_mask, jnp.concatenate([x, x], axis=0), 0.0)

    ar = [jnp.concatenate([atp[p], rtp[p]], axis=0) for p in pairs]
    ab = [_bdot_nt(ar[p], by_head(col(bt, p))) for p in pairs]
    ak = [_bdot_nt(ar[p], by_head(col(kt, p))) for p in pairs]
    tinv = pk.inverse_many([jnp.where(pk.strict, m[0:R], 0.0) for m in ab])
    v_bh = [by_head(col(xv, p)) for p in pairs]
    y = [x_state[p] + _bdot(jnp.where(pk.strict, ak[p][0:R], 0.0), v_bh[p]) for p in pairs]
    u = [_bdot(tinv[p], by_head(y[p])) for p in pairs]
    o = [o_state[p] + _bdot(jnp.where(pk.incl, ab[p][R:2 * R], 0.0), by_head(u[p]))
         + _bdot(jnp.where(pk.incl, ak[p][R:2 * R], 0.0), v_bh[p]) for p in pairs]
    for n in seqs:
        for p in pairs:
            uv = jnp.concatenate([seq_rows(u[p], n), seq_rows(col(xv, p), n)], axis=0)
            bkhat = jnp.concatenate([seq_rows(col(bhat, p), n), seq_rows(col(khat, p), n)], axis=0)
            decay = col(p_tot, p) if NB == 1 else col(p_tot, p)[n * C:n * C + 1]
            s_new = s[n][p] * decay + _bdot_tn(uv, bkhat)
            st_ref[n * G + p] = jnp.where(same_head, s_new, 0.0)

    o_rows = jnp.concatenate(o, axis=0)
    mean = _bdot(o_rows, ones_bd) * (1.0 / HEAD_B)
    d = o_rows - mean
    var = _bdot(d * d, ones_bd) * (1.0 / HEAD_B)
    on = to_cols(d * lax.rsqrt(var + GN_EPS)) * lnw_ref[...] + lnb_ref[...]
    bonus = to_cols(_bdot(to_rows(xr * k2 * rk_ref[...]), ones_bd)) * xv
    o_ref[...] = ((on + bonus) * gate).astype(BF16)

    if NB == 1:
        cr_ref[...] = r_ref[C - SUB:C, :]
        ck_ref[...] = k_ref[C - SUB:C, :]
        cv_ref[...] = v_ref[C - SUB:C, :]
        csm_ref[...] = sm_ref[C - SUB:C, :]

    @pl.when(ci == nch - 1)
    def _():
        for n in seqs:
            for p in pairs:
                s_ref[n, p] = _xdot(st_ref[n * G + p], gather, 3)


def _rwkv_mixer(proj, row0, nb, seq, C, NB, G, prev_rkv, prev_sm, s0, mu_rkv, mu_sm, w2p, a2p, g2p,
                w0, a0, k_k, k_a, r_k, lnw, lnb):
    nch = seq // C
    assert NB == 1 or (nch == 1 and C == SUB)
    R = NB * C
    ng = N_PAIR // G
    gw = G * LANE
    blk0 = row0 // R
    bcast = s0.shape[0] == 1
    bsel = (lambda b: 0) if bcast else (lambda b: b)

    def proj_spec(col0):
        return pl.BlockSpec((R, gw), lambda b, g, c: (blk0 + b * nch + c, col0 // gw + g))

    def prev_spec(part):
        return pl.BlockSpec((NB * SUB, gw), lambda b, g, c: (bsel(b), part * (W_B // gw) + g))

    def vec_spec(part=0):
        return pl.BlockSpec((1, gw), lambda b, g, c: (0, part * (W_B // gw) + g))

    in_specs = [
        proj_spec(RKV0), proj_spec(RKV0 + W_B), proj_spec(RKV0 + 2 * W_B),
        pl.BlockSpec((R, SM_W), lambda b, g, c: (blk0 + b * nch + c, SM0 // SM_W)),
        prev_spec(0), prev_spec(1), prev_spec(2),
        pl.BlockSpec((NB * SUB, SM_W), lambda b, g, c: (bsel(b), 0)),
        pl.BlockSpec((NB, G, LANE, HEAD_B), lambda b, g, c: (bsel(b), g, 0, 0)),
        vec_spec(0), vec_spec(1), vec_spec(2),
        pl.BlockSpec((1, SM_W), lambda b, g, c: (0, 0)),
        pl.BlockSpec((LANE, gw), lambda b, g, c: (0, g)),
        pl.BlockSpec((LANE, gw), lambda b, g, c: (0, g)),
        pl.BlockSpec((2 * LANE, gw), lambda b, g, c: (0, g)),
        vec_spec(), vec_spec(), vec_spec(), vec_spec(), vec_spec(), vec_spec(), vec_spec(),
    ]
    return pl.pallas_call(
        functools.partial(_rwkv_kernel, C=C, NB=NB, G=G),
        out_shape=(jax.ShapeDtypeStruct((nb * seq, W_B), BF16),
                   jax.ShapeDtypeStruct((nb, N_PAIR, LANE, HEAD_B), F32)),
        grid=(nb // NB, ng, nch),
        in_specs=in_specs,
        out_specs=(pl.BlockSpec((R, gw), lambda b, g, c: (b * nch + c, g)),
                   pl.BlockSpec((NB, G, LANE, HEAD_B), lambda b, g, c: (b, g, 0, 0))),
        scratch_shapes=[pltpu.VMEM((SUB, gw), F32), pltpu.VMEM((SUB, gw), F32), pltpu.VMEM((SUB, gw), F32),
                        pltpu.VMEM((SUB, SM_W), F32), pltpu.VMEM((NB * G, LANE, LANE), F32)],
        compiler_params=pltpu.CompilerParams(
            dimension_semantics=("parallel", "parallel", "arbitrary"), vmem_limit_bytes=VMEM_LIMIT),
        name="rwkv_mixer",
    )(proj, proj, proj, proj, prev_rkv, prev_rkv, prev_rkv, prev_sm, s0,
      mu_rkv, mu_rkv, mu_rkv, mu_sm, w2p, a2p, g2p, w0, a0, k_k, k_a, r_k, lnw, lnb)


_AT, _RT, _BT, _KT, _BHAT, _KHAT, _XV, _GATE, _BONUS, _N_PRE = range(10)


def _rwkv_seq_kernel(r0_ref, k0_ref, v0_ref, sm0_ref, rn_ref, kn_ref, vn_ref, smn_ref,
                     pr_ref, pk_ref, pv_ref, psm_ref, s0_ref,
                     mur_ref, muk_ref, muv_ref, musm_ref, w2_ref, a2_ref, g2_ref,
                     w0_ref, a0_ref, kk_ref, ka_ref, rk_ref, lnw_ref, lnb_ref,
                     o_ref, s_ref, cr_ref, ck_ref, cv_ref, csm_ref, st_ref, pre_ref, ptot_ref, *, C, G):
    ci = pl.program_id(2)
    nch = pl.num_programs(2)
    pairs = range(G)

    r2 = _iota((LANE, LANE), 0)
    c2 = _iota((LANE, LANE), 1)
    same_head = (r2 < HEAD_B) == (c2 < HEAD_B)
    ones_bd = same_head.astype(F32)
    spread = (_iota((HEAD_B, LANE), 0) == (_iota((HEAD_B, LANE), 1) & (HEAD_B - 1))).astype(F32)
    gather = ((_iota((LANE, HEAD_B), 0) & (HEAD_B - 1)) == _iota((LANE, HEAD_B), 1)).astype(F32)
    tri, _ = _seq_masks(C, C)

    def col(x, p):
        return x[:, p * LANE:(p + 1) * LANE]

    def to_rows(x):
        return jnp.concatenate([col(x, p) for p in pairs], axis=0)

    def to_cols(y):
        return jnp.concatenate([y[p * C:(p + 1) * C] for p in pairs], axis=1)

    def lerp(x, hist, mu):
        return x + (_shifted(x, hist, 1) - x) * mu

    def preamble(r_ref, k_ref, v_ref, sm_ref, hr_ref, hk_ref, hv_ref, hsm_ref, slot):
        xs = lerp(sm_ref[...], hsm_ref[...], musm_ref[...])
        yield
        slab_wa = xs[:, SM_WA:SM_WA + LANE]
        wl = _bdot(jnp.tanh(slab_wa), w2_ref[...])
        yield
        al = _bdot(slab_wa, a2_ref[...])
        yield
        pre_ref[slot, _GATE] = _bdot(_sigmoid(xs[:, SM_G:SM_G + 2 * LANE]), g2_ref[...])
        yield
        xr = lerp(r_ref[...], hr_ref[...], mur_ref[...])
        yield
        xk = lerp(k_ref[...], hk_ref[...], muk_ref[...])
        yield
        xv = lerp(v_ref[...], hv_ref[...], muv_ref[...])
        pre_ref[slot, _XV] = xv
        yield
        ld = -jnp.exp(-_softplus(-(w0_ref[...] + wl)) - 0.5)
        yield
        lp = _xdot_r(tri, ld, 3)
        yield
        aa = _sigmoid(a0_ref[...] + al)
        yield
        kkr = xk * kk_ref[...]
        ss = to_cols(_bdot(to_rows(kkr * kkr), ones_bd))
        yield
        kkn = kkr * lax.rsqrt(ss + 1e-6)
        k2 = xk * (1.0 + (aa - 1.0) * ka_ref[...])
        yield
        pre_ref[slot, _BONUS] = to_cols(_bdot(to_rows(xr * k2 * rk_ref[...]), ones_bd)) * xv
        yield
        pre_ref[slot, _AT] = -kkn * jnp.exp(lp - ld)
        yield
        pre_ref[slot, _RT] = xr * jnp.exp(lp)
        yield
        kb = kkn * aa
        e_neg = jnp.exp(-lp)
        pre_ref[slot, _BT] = kb * e_neg
        yield
        pre_ref[slot, _KT] = k2 * e_neg
        yield
        lp_tot = lp[C - 1:C, :]
        e_rem = jnp.exp(lp_tot - lp)
        pre_ref[slot, _BHAT] = kb * e_rem
        yield
        pre_ref[slot, _KHAT] = k2 * e_rem
        ptot_ref[slot] = jnp.broadcast_to(jnp.exp(lp_tot), (SUB, lp.shape[1]))
        yield
        cr_ref[...] = r_ref[C - SUB:C, :]
        ck_ref[...] = k_ref[C - SUB:C, :]
        cv_ref[...] = v_ref[C - SUB:C, :]
        csm_ref[...] = sm_ref[C - SUB:C, :]

    @pl.when(ci == 0)
    def _():
        for _ in preamble(r0_ref, k0_ref, v0_ref, sm0_ref, pr_ref, pk_ref, pv_ref, psm_ref, 0):
            pass
        for p in pairs:
            st_ref[p] = jnp.where(same_head, _xdot(s0_ref[0, p], spread, 3), 0.0)

    slot = ci % 2
    ahead = preamble(rn_ref, kn_ref, vn_ref, smn_ref, cr_ref, ck_ref, cv_ref, csm_ref, 1 - slot)
    calls = [0]

    def tick():
        calls[0] += 1
        if calls[0] % 6 == 0:
            next(ahead, None)

    def each(fn, items):
        out = []
        for it in items:
            out.append(fn(it))
            tick()
        return out

    def pre(idx, p):
        return pre_ref[slot, idx, :, p * LANE:(p + 1) * LANE]

    pk = _Packed(C, C)
    stack_mask = (_iota((2 * C, LANE), 0) < C) == (_iota((2 * C, LANE), 1) < HEAD_B)

    def by_head(x):
        return jnp.where(stack_mask, jnp.concatenate([x, x], axis=0), 0.0)

    s = [st_ref[p] for p in pairs]
    ar = [jnp.concatenate([pre(_AT, p), pre(_RT, p)], axis=0) for p in pairs]
    ars = each(lambda p: _bdot_nt(ar[p], s[p]), pairs)
    ab = each(lambda p: _bdot_nt(ar[p], by_head(pre(_BT, p))), pairs)
    ak = each(lambda p: _bdot_nt(ar[p], by_head(pre(_KT, p))), pairs)
    ts = [pk.eye + jnp.where(pk.strict, m[0:C], 0.0) for m in ab]
    ps = [jnp.where(pk.strict, m[0:C], 0.0) for m in ab]
    n = 1
    while 2 * n < C:
        ps = each(lambda p: _bdot(ps[p], pk.block_diag(ps[p])), pairs)
        ts = each(lambda p: ts[p] + _bdot(ps[p], pk.block_diag(ts[p])), pairs)
        n *= 2
    xv = [pre(_XV, p) for p in pairs]
    v_bh = [by_head(xv[p]) for p in pairs]
    y = each(lambda p: ars[p][0:C] + _bdot(jnp.where(pk.strict, ak[p][0:C], 0.0), v_bh[p]), pairs)
    u = each(lambda p: _bdot(ts[p], by_head(y[p])), pairs)
    o = each(lambda p: ars[p][C:2 * C] + _bdot(jnp.where(pk.incl, ab[p][C:2 * C], 0.0), by_head(u[p]))
             + _bdot(jnp.where(pk.incl, ak[p][C:2 * C], 0.0), v_bh[p]), pairs)
    for p in pairs:
        uv = jnp.concatenate([u[p], xv[p]], axis=0)
        bkhat = jnp.concatenate([pre(_BHAT, p), pre(_KHAT, p)], axis=0)
        s_new = s[p] * ptot_ref[slot, 0:1, p * LANE:(p + 1) * LANE] + _bdot_tn(uv, bkhat)
        st_ref[p] = jnp.where(same_head, s_new, 0.0)
        tick()

    o_rows = jnp.concatenate(o, axis=0)
    mean = _bdot(o_rows, ones_bd) * (1.0 / HEAD_B)
    d = o_rows - mean
    var = _bdot(d * d, ones_bd) * (1.0 / HEAD_B)
    on = to_cols(d * lax.rsqrt(var + GN_EPS)) * lnw_ref[...] + lnb_ref[...]
    o_ref[...] = ((on + pre_ref[slot, _BONUS]) * pre_ref[slot, _GATE]).astype(BF16)
    for _ in ahead:
        pass

    @pl.when(ci == nch - 1)
    def _():
        for p in pairs:
            s_ref[0, p] = _xdot(st_ref[p], gather, 3)


def _rwkv_seq_mixer(proj, nb, seq, C, prev_rkv, prev_sm, s0, mu_rkv, mu_sm, w2p, a2p, g2p,
                    w0, a0, k_k, k_a, r_k, lnw, lnb):
    G = N_PAIR
    nch = seq // C
    gw = G * LANE

    def first_spec(col0, w):
        return pl.BlockSpec((C, w), lambda b, g, c: (b * nch, col0 // w))

    def next_spec(col0, w):
        return pl.BlockSpec((C, w), lambda b, g, c: (b * nch + jnp.minimum(c + 1, nch - 1), col0 // w))

    def prev_spec(part):
        return pl.BlockSpec((SUB, gw), lambda b, g, c: (0, part))

    def vec_spec(part=0):
        return pl.BlockSpec((1, gw), lambda b, g, c: (0, part))

    in_specs = [
        first_spec(RKV0, gw), first_spec(RKV0 + W_B, gw), first_spec(RKV0 + 2 * W_B, gw), first_spec(SM0, SM_W),
        next_spec(RKV0, gw), next_spec(RKV0 + W_B, gw), next_spec(RKV0 + 2 * W_B, gw), next_spec(SM0, SM_W),
        prev_spec(0), prev_spec(1), prev_spec(2),
        pl.BlockSpec((SUB, SM_W), lambda b, g, c: (0, 0)),
        pl.BlockSpec((1, G, LANE, HEAD_B), lambda b, g, c: (0, 0, 0, 0)),
        vec_spec(0), vec_spec(1), vec_spec(2),
        pl.BlockSpec((1, SM_W), lambda b, g, c: (0, 0)),
        pl.BlockSpec((LANE, gw), lambda b, g, c: (0, 0)),
        pl.BlockSpec((LANE, gw), lambda b, g, c: (0, 0)),
        pl.BlockSpec((2 * LANE, gw), lambda b, g, c: (0, 0)),
        vec_spec(), vec_spec(), vec_spec(), vec_spec(), vec_spec(), vec_spec(), vec_spec(),
    ]
    return pl.pallas_call(
        functools.partial(_rwkv_seq_kernel, C=C, G=G),
        out_shape=(jax.ShapeDtypeStruct((nb * seq, W_B), BF16),
                   jax.ShapeDtypeStruct((nb, N_PAIR, LANE, HEAD_B), F32)),
        grid=(nb, 1, nch),
        in_specs=in_specs,
        out_specs=(pl.BlockSpec((C, gw), lambda b, g, c: (b * nch + c, 0)),
                   pl.BlockSpec((1, G, LANE, HEAD_B), lambda b, g, c: (b, 0, 0, 0))),
        scratch_shapes=[pltpu.VMEM((SUB, gw), F32), pltpu.VMEM((SUB, gw), F32), pltpu.VMEM((SUB, gw), F32),
                        pltpu.VMEM((SUB, SM_W), F32), pltpu.VMEM((G, LANE, LANE), F32),
                        pltpu.VMEM((2, _N_PRE, C, gw), F32), pltpu.VMEM((2, SUB, gw), F32)],
        compiler_params=pltpu.CompilerParams(
            dimension_semantics=("parallel", "arbitrary", "arbitrary"), vmem_limit_bytes=VMEM_LIMIT),
        name="rwkv_seq_mixer",
    )(proj, proj, proj, proj, proj, proj, proj, proj, prev_rkv, prev_rkv, prev_rkv, prev_sm, s0,
      mu_rkv, mu_rkv, mu_rkv, mu_sm, w2p, a2p, g2p, w0, a0, k_k, k_a, r_k, lnw, lnb)


def _small_layout(cols_ba, cols_w, cols_a, cols_g, axis=-1):
    def z(n):
        shape = list(cols_w.shape)
        shape[axis] = n
        return jnp.zeros(shape, cols_w.dtype)
    return jnp.concatenate(
        [cols_ba, z(SM_WA - cols_ba.shape[axis]), cols_w, cols_a, cols_g,
         z(SM_W - SM_G - cols_g.shape[axis])], axis=axis)


def _pad_rows8(x):
    b, n, c = x.shape
    return jnp.concatenate([jnp.zeros((b, SUB - n, c), x.dtype), x], axis=1).reshape(b * SUB, c)


def kernel(x_prompt, x_sample, state_delta, state_conv_qkv, state_wkv, state_shift, state_ffn_conv, meta, norm1, w_in, conv_a, a_log, dt_bias, onorm_a, mu_b, w0, w2, a0, a2, g2, k_k, k_a, r_k, lnx_w, lnx_b, w_o, norm2, w_ffn_in, conv_f, w_ffn_out, norm_f):
    nbp, seq_p, _ = x_prompt.shape
    nbs, seq_s, _ = x_sample.shape
    n_s = nbs * seq_s
    assert w_in.shape[0] == 1, "single-layer trunk"
    assert seq_s == SUB
    l = 0

    wt = w_in[l].T
    o_b = A_PROJ
    o_l = A_PROJ + 3 * W_B
    w_cat_t = jnp.concatenate([
        wt[:4 * W_A], wt[o_b:o_l],
        _small_layout(wt[4 * W_A:A_PROJ], wt[o_l:o_l + W_LORA], wt[o_l + W_LORA:o_l + W_LORA + A_LORA],
                      wt[o_l + W_LORA + A_LORA:], axis=0)], axis=0).astype(BF16)
    mu = mu_b[l]
    mu_rkv = mu[None, :3 * W_B]
    mu_sm = _small_layout(jnp.zeros((1, 2 * H_A), F32), mu[None, 3 * W_B:3 * W_B + W_LORA],
                          mu[None, 3 * W_B + W_LORA:3 * W_B + W_LORA + A_LORA],
                          mu[None, 3 * W_B + W_LORA + A_LORA:])
    w2p = jnp.concatenate([w2[l], jnp.zeros((LANE - W_LORA, W_B), F32)], axis=0)
    a2p = jnp.concatenate([jnp.zeros((W_LORA, W_B), F32), a2[l]], axis=0)
    g2p = jnp.concatenate([g2[l], jnp.zeros((2 * LANE - G_LORA, W_B), F32)], axis=0)
    alog_row = jnp.concatenate([jnp.zeros((H_A,), F32), a_log[l], jnp.zeros((LANE - 2 * H_A,), F32)])[None]
    dtb_row = jnp.concatenate([jnp.zeros((H_A,), F32), dt_bias[l], jnp.zeros((LANE - 2 * H_A,), F32)])[None]
    wo_bf = w_o[l].astype(BF16)
    wfo_bf = w_ffn_out[l].astype(BF16)
    row = lambda v: v.reshape(1, -1)

    def mix(proj, row0, nb, seq, C, NB, prev_qkv, prev_rkv, prev_sm, s_delta, s_wkv):
        oa, sd = _delta_mixer(proj, row0, nb, seq, C, NB, prev_qkv, s_delta, conv_a[l], alog_row, dtb_row,
                              row(onorm_a[l]))
        rwkv_params = (mu_rkv, mu_sm, w2p, a2p, g2p, row(w0[l]), row(a0[l]), row(k_k[l]), row(k_a[l]),
                       row(r_k[l]), row(lnx_w[l]), row(lnx_b[l]))
        if seq // C > 1:
            assert row0 == 0 and NB == 1
            ob, sw = _rwkv_seq_mixer(proj, nb, seq, C, prev_rkv, prev_sm, s_wkv, *rwkv_params)
        else:
            ob, sw = _rwkv_mixer(proj, row0, nb, seq, C, NB, N_PAIR, prev_rkv, prev_sm, s_wkv, *rwkv_params)
        return oa, ob, sd, sw

    xs_rows = jnp.concatenate([x_sample.reshape(n_s, D_MODEL), meta], axis=0)
    xp_rows = x_prompt.reshape(nbp * seq_p, D_MODEL)
    n_small = n_s + N_META
    proj_s = _norm_matmul(xs_rows, row(norm1[l]), w_cat_t, n_small, 1536)
    proj_p = _norm_matmul(xp_rows, row(norm1[l]), w_cat_t, 1024, 1536)

    zeros = lambda *s: jnp.zeros(s, F32)
    oa_m, ob_m, sd_m, sw_m = mix(proj_s, n_s, 1, N_META, N_META, 1, zeros(SUB, 3 * W_A),
                                 zeros(SUB, 3 * W_B), zeros(SUB, SM_W),
                                 zeros(1, H_A, HEAD_A, HEAD_A), zeros(1, N_PAIR, LANE, HEAD_B))
    tail = proj_s[n_small - SUB:n_small]
    oa_p, ob_p, sd_p, sw_p = mix(proj_p, 0, nbp, seq_p, 64, 1, tail[:, :3 * W_A],
                                 tail[:, RKV0:SM0], tail[:, SM0:], sd_m, sw_m)
    sh = state_shift[l]
    sh_sm = _small_layout(jnp.zeros((nbs, 1, 2 * H_A), F32), sh[..., 3 * W_B:3 * W_B + W_LORA],
                          sh[..., 3 * W_B + W_LORA:3 * W_B + W_LORA + A_LORA],
                          sh[..., 3 * W_B + W_LORA + A_LORA:])
    oa_s, ob_s, sd_s, sw_s = mix(proj_s, 0, nbs, seq_s, seq_s, 8, _pad_rows8(state_conv_qkv[l]),
                                 _pad_rows8(sh[..., :3 * W_B]), _pad_rows8(sh_sm), state_delta[l],
                                 state_wkv[l].reshape(nbs, N_PAIR, LANE, HEAD_B))

    oa_small = jnp.concatenate([oa_s, oa_m], axis=0)
    ob_small = jnp.concatenate([ob_s, ob_m], axis=0)
    x1_s, h2_s = _out_proj(xs_rows, oa_small, ob_small, wo_bf, row(norm2[l]), n_small // 5)
    x1_p, h2_p = _out_proj(xp_rows, oa_p, ob_p, wo_bf, row(norm2[l]), 512)
    prev_rows = jnp.concatenate([_pad_rows8(state_ffn_conv[l]), jnp.zeros((N_META, D_FF), F32)], axis=0)
    act_s, gate_s = _ffn_in(h2_s, w_ffn_in[l], prev_rows, conv_f[l], n_small, 768, 3, 0, n_s + SUB)
    act_p, tail_p = _ffn_in(h2_p, w_ffn_in[l], gate_s[None, n_small - SUB:n_small], conv_f[l], 1024, 768, 3,
                            seq_p // 1024)
    y_s = _ffn_out(act_s, wfo_bf, x1_s, row(norm_f), n_s, 1024, 768)
    y_p = _ffn_out(act_p, wfo_bf, x1_p, row(norm_f), nbp * seq_p, 1024, 768)

    def states(proj, gate_tail, nb, seq, sd, sw):
        p3 = proj.reshape(-1, seq, P_CAT)
        conv_new = p3[:nb, seq - 3:, :3 * W_A]
        last = p3[:nb, seq - 1:, :]
        shift_new = jnp.concatenate([last[..., RKV0:SM0], last[..., SM0 + SM_WA:SM0 + SM_WA + W_LORA + A_LORA],
                                     last[..., SM0 + SM_G:SM0 + SM_G + G_LORA]], axis=-1)
        ffn_new = gate_tail[:nb, SUB - 2:, :]
        return (sd[None], conv_new[None], sw.reshape(nb, H_B, HEAD_B, HEAD_B)[None], shift_new[None],
                ffn_new[None])

    return ((y_p.reshape(nbp, seq_p, D_MODEL), y_s.reshape(nbs, seq_s, D_MODEL))
            + states(proj_p, tail_p, nbp, seq_p, sd_p, sw_p)
            + states(proj_s, gate_s.reshape(-1, SUB, D_FF), nbs, seq_s, sd_s, sw_s))
```

```python
import functools

import jax
import jax.numpy as jnp
from jax import lax
from jax.experimental import pallas as pl
from jax.experimental.pallas import tpu as pltpu

F32 = jnp.float32
BF16 = jnp.bfloat16

D_MODEL = 2048
N_META = 16
W_A = 1024
HEAD_A = 128
H_A = 8
W_B = 1024
HEAD_B = 64
H_B = 16
N_PAIR = H_B // 2
W_LORA = 64
A_LORA = 64
G_LORA = 160
D_FF = 5376
RMS_EPS = 1e-6
GN_EPS = 64e-5
A_PROJ = 4 * W_A + 2 * H_A
B_PROJ = 3 * W_B + W_LORA + A_LORA + G_LORA

QKVZ0 = 0
RKV0 = 4 * W_A
SM0 = RKV0 + 3 * W_B
SM_W = 512
SM_BA = 0
SM_WA = 128
SM_G = 256
P_CAT = SM0 + SM_W

LANE = 128
SUB = 8
VMEM_LIMIT = 48 * 1024 * 1024
VMEM_LIMIT_BIG = 58 * 1024 * 1024
NEG_BIG = -1e30

NT_DIMS = (((1,), (1,)), ((), ()))


def _bdot(a, b):
    return jnp.dot(a.astype(BF16), b.astype(BF16), preferred_element_type=F32)


def _bdot_nt(a, b):
    return lax.dot_general(a.astype(BF16), b.astype(BF16), NT_DIMS, preferred_element_type=F32)


def _bdot_tn(a, b):
    return lax.dot_general(a.astype(BF16), b.astype(BF16), (((0,), (0,)), ((), ())),
                           preferred_element_type=F32)


def _pieces(a, n):
    out = []
    rem = a
    for i in range(n):
        p = rem.astype(BF16)
        out.append(p)
        if i + 1 < n:
            rem = rem - p.astype(F32)
    return out


def _xdot(a, b, n, dims=(((1,), (0,)), ((), ()))):
    bb = b.astype(BF16)
    acc = None
    for p in _pieces(a, n):
        t = lax.dot_general(p, bb, dims, preferred_element_type=F32)
        acc = t if acc is None else acc + t
    return acc


def _xdot_r(a, b, n, dims=(((1,), (0,)), ((), ()))):
    ab = a.astype(BF16)
    acc = None
    for p in _pieces(b, n):
        t = lax.dot_general(ab, p, dims, preferred_element_type=F32)
        acc = t if acc is None else acc + t
    return acc


def _sigmoid(x):
    return 1.0 / (1.0 + jnp.exp(-x))


def _silu(x):
    return x * _sigmoid(x)


def _softplus(x):
    return jnp.maximum(x, 0.0) + jnp.log(1.0 + jnp.exp(-jnp.abs(x)))


def _iota(shape, dim):
    return lax.broadcasted_iota(jnp.int32, shape, dim)


def _shifted(x, prev8, k):
    n = x.shape[0]
    xr = pltpu.roll(x, k, 0)
    pr = pltpu.roll(prev8, k, 0)
    first = jnp.where(_iota((SUB, x.shape[1]), 0) < k, pr, xr[0:SUB])
    if n == SUB:
        return first
    return jnp.concatenate([first, xr[SUB:]], axis=0)


def _delayed(x, hist, k, nseq):
    if nseq == 1:
        return _shifted(x, hist, k)
    n = x.shape[0]
    use_hist = (_iota(x.shape, 0) & (SUB - 1)) < k
    return jnp.where(use_hist, pltpu.roll(hist, n + k - SUB, 0), pltpu.roll(x, k, 0))


def _same_seq(shape, c, rows_total):
    if c == rows_total:
        return None
    sh = c.bit_length() - 1
    return (lax.shift_right_logical(_iota(shape, 0), sh)
            == lax.shift_right_logical(_iota(shape, 1) & (rows_total - 1), sh))


class _Packed:
    def __init__(self, r, c):
        self.r = r
        self.c = c
        row = _iota((r, 2 * r), 0)
        col = _iota((r, 2 * r), 1) & (r - 1)
        same = _same_seq((r, 2 * r), c, r)
        self.incl = (row >= col) if same is None else ((row >= col) & same)
        self.strict = (row > col) if same is None else ((row > col) & same)
        self.eye = (row == col).astype(F32)
        self.first = _iota((r, 2 * r), 1) < r
        self.bd_mask = (_iota((2 * r, 2 * r), 0) < r) == (_iota((2 * r, 2 * r), 1) < r)

    def block_diag(self, m):
        return jnp.where(self.bd_mask, jnp.concatenate([m, m], axis=0), 0.0)

    def inverse_many(self, xs):
        ts = [self.eye + x for x in xs]
        ps = list(xs)
        n = 1
        while 2 * n < self.c:
            ps = [_bdot(p, self.block_diag(p)) for p in ps]
            ts = [t + _bdot(p, self.block_diag(t)) for p, t in zip(ps, ts)]
            n *= 2
        return ts


def _seq_masks(r, c):
    row = _iota((r, r), 0)
    col = _iota((r, r), 1)
    same = _same_seq((r, r), c, r)
    if same is None:
        return (row >= col).astype(F32), None
    return ((row >= col) & same).astype(F32), same.astype(F32)


def _seq_tail_rows(x, slab_ref, dst_ref, n_tail, nseq, col0=0):
    for s in range(x.shape[1] // LANE):
        slab_ref[s] = x[:, s * LANE:(s + 1) * LANE]
        for t in range(n_tail):
            dst_ref[t, :, col0 + s * LANE:col0 + (s + 1) * LANE] = (
                slab_ref[s, pl.ds(SUB - n_tail + t, nseq, stride=SUB), :])


def _norm_matmul_kernel(x_ref, g_ref, w_ref, o_ref, *rest, n_tail, nseq):
    tail_ref, h_ref, slab_ref = rest if n_tail else (None,) + rest + (None,)

    @pl.when(pl.program_id(1) == 0)
    def _():
        x = x_ref[...]
        ms = jnp.mean(x * x, axis=-1, keepdims=True)
        h_ref[...] = (x * lax.rsqrt(ms + RMS_EPS) * g_ref[...]).astype(BF16)

    out = lax.dot_general(h_ref[...], w_ref[...], NT_DIMS, preferred_element_type=F32)
    o_ref[...] = out
    if n_tail:
        _seq_tail_rows(out, slab_ref, tail_ref, n_tail, nseq)


def _norm_matmul(x, g, wt, tm, tn, n_tail=0, nseq=0):
    m, k = x.shape
    n = wt.shape[0]
    out_shape = jax.ShapeDtypeStruct((m, n), F32)
    out_specs = pl.BlockSpec((tm, tn), lambda i, j: (i, j))
    if n_tail:
        assert m == tm
        out_shape = (out_shape, jax.ShapeDtypeStruct((n_tail, nseq, n), F32))
        out_specs = (out_specs, pl.BlockSpec((n_tail, nseq, tn), lambda i, j: (0, 0, j)))
    return pl.pallas_call(
        functools.partial(_norm_matmul_kernel, n_tail=n_tail, nseq=nseq),
        out_shape=out_shape,
        grid=(m // tm, n // tn),
        in_specs=[
            pl.BlockSpec((tm, k), lambda i, j: (i, 0)),
            pl.BlockSpec((1, k), lambda i, j: (0, 0)),
            pl.BlockSpec((tn, k), lambda i, j: (j, 0)),
        ],
        out_specs=out_specs,
        scratch_shapes=[pltpu.VMEM((tm, k), BF16)] + ([pltpu.VMEM((tn // LANE, tm, LANE), F32)] if n_tail else []),
        compiler_params=pltpu.CompilerParams(
            dimension_semantics=("parallel", "arbitrary"), vmem_limit_bytes=VMEM_LIMIT_BIG),
        name="norm_matmul",
    )(x, g, wt)


def _out_proj_kernel(x_ref, oa_ref, ob_ref, wt_ref, wb_ref, g_ref, x1_ref, h_ref):
    acc = jnp.dot(oa_ref[...], wt_ref[...], preferred_element_type=F32)
    acc = acc + jnp.dot(ob_ref[...], wb_ref[...], preferred_element_type=F32)
    x1 = x_ref[...] + acc
    x1_ref[...] = x1
    ms = jnp.mean(x1 * x1, axis=-1, keepdims=True)
    h_ref[...] = (x1 * lax.rsqrt(ms + RMS_EPS) * g_ref[...]).astype(BF16)


def _out_proj(x, oa, ob, wo, g, tm):
    m, d = x.shape
    return pl.pallas_call(
        _out_proj_kernel,
        out_shape=(jax.ShapeDtypeStruct((m, d), F32), jax.ShapeDtypeStruct((m, d), BF16)),
        grid=(m // tm,),
        in_specs=[
            pl.BlockSpec((tm, d), lambda i: (i, 0)),
            pl.BlockSpec((tm, W_A), lambda i: (i, 0)),
            pl.BlockSpec((tm, W_B), lambda i: (i, 0)),
            pl.BlockSpec((W_A, d), lambda i: (0, 0)),
            pl.BlockSpec((W_B, d), lambda i: (W_A // W_B, 0)),
            pl.BlockSpec((1, d), lambda i: (0, 0)),
        ],
        out_specs=(pl.BlockSpec((tm, d), lambda i: (i, 0)), pl.BlockSpec((tm, d), lambda i: (i, 0))),
        compiler_params=pltpu.CompilerParams(
            dimension_semantics=("parallel",), vmem_limit_bytes=VMEM_LIMIT),
        name="out_proj",
    )(x, oa, ob, wo, wo, g)


def _ffn_in_kernel(h_ref, wg_ref, wu_ref, prev_ref, cw_ref, act_ref, tail_ref, *rest,
                   tiles_per_seq, state_rows, nsub):
    if tiles_per_seq:
        wgb_ref, wub_ref, carry_ref = rest
    else:
        last_ref, wgb_ref, wub_ref, carry_ref, slab_ref = rest
    m = pl.program_id(1)

    @pl.when(m == 0)
    def _():
        wgb_ref[...] = wg_ref[...].astype(BF16)
        wub_ref[...] = wu_ref[...].astype(BF16)

    if tiles_per_seq:
        @pl.when(m % tiles_per_seq == 0)
        def _():
            carry_ref[...] = prev_ref[...]

    h = h_ref[...]
    n = h.shape[0]
    w = wgb_ref.shape[1] // nsub
    for i in range(nsub):
        cs = slice(i * w, (i + 1) * w)
        gate = jnp.dot(h, wgb_ref[:, cs], preferred_element_type=F32)
        up = jnp.dot(h, wub_ref[:, cs], preferred_element_type=F32)
        if tiles_per_seq:
            prev8 = carry_ref[:, cs]
            d1 = _shifted(gate, prev8, 1)
            d2 = _shifted(gate, prev8, 2)
            carry_ref[:, cs] = gate[n - SUB:n]
            tail_ref[:, cs] = gate[n - SUB:n]
        else:
            row = _iota(gate.shape, 0)
            prev = prev_ref[:, cs]
            _seq_tail_rows(gate, slab_ref, tail_ref, 2, (state_rows - SUB) // SUB, i * w)
            last_ref[:, cs] = gate[n - SUB:n]

            def delayed(k):
                use_prev = ((row & (SUB - 1)) < k) & (row < state_rows)
                return jnp.where(use_prev, pltpu.roll(prev, n + k - SUB, 0), pltpu.roll(gate, k, 0))

            d1 = delayed(1)
            d2 = delayed(2)
        y = gate * cw_ref[2:3, cs] + d1 * cw_ref[1:2, cs] + d2 * cw_ref[0:1, cs]
        act_ref[:, cs] = (_silu(y) * up).astype(BF16)


def _ffn_in(h, w, prev, cw, tm, tn, nsub, tiles_per_seq, state_rows=0):
    m, d = h.shape
    nj = D_FF // tn
    nm = m // tm
    if tiles_per_seq:
        nseq = nm // tiles_per_seq
        prev_spec = pl.BlockSpec((None, SUB, tn), lambda j, i: (0, 0, j))
        extra_shapes = (jax.ShapeDtypeStruct((nseq, SUB, D_FF), F32),)
        extra_specs = (pl.BlockSpec((None, SUB, tn), lambda j, i: (i // tiles_per_seq, 0, j)),)
        extra_scratch = []
    else:
        assert nm == 1
        nseq = (state_rows - SUB) // SUB
        prev_spec = pl.BlockSpec((tm, tn), lambda j, i: (i, j))
        extra_shapes = (jax.ShapeDtypeStruct((2, nseq, D_FF), F32), jax.ShapeDtypeStruct((SUB, D_FF), F32))
        extra_specs = (pl.BlockSpec((2, nseq, tn), lambda j, i: (0, 0, j)),
                       pl.BlockSpec((SUB, tn), lambda j, i: (0, j)))
        extra_scratch = [pltpu.VMEM((tn // nsub // LANE, tm, LANE), F32)]
    return pl.pallas_call(
        functools.partial(_ffn_in_kernel, tiles_per_seq=tiles_per_seq, state_rows=state_rows, nsub=nsub),
        out_shape=(jax.ShapeDtypeStruct((m, D_FF), BF16),) + extra_shapes,
        grid=(nj, nm),
        in_specs=[
            pl.BlockSpec((tm, d), lambda j, i: (i, 0)),
            pl.BlockSpec((d, tn), lambda j, i: (0, j)),
            pl.BlockSpec((d, tn), lambda j, i: (0, nj + j)),
            prev_spec,
            pl.BlockSpec((3, tn), lambda j, i: (0, j)),
        ],
        out_specs=(pl.BlockSpec((tm, tn), lambda j, i: (i, j)),) + extra_specs,
        scratch_shapes=[pltpu.VMEM((d, tn), BF16), pltpu.VMEM((d, tn), BF16), pltpu.VMEM((SUB, tn), F32)]
        + extra_scratch,
        compiler_params=pltpu.CompilerParams(
            dimension_semantics=("parallel", "arbitrary"), vmem_limit_bytes=VMEM_LIMIT_BIG),
        name="ffn_in",
    )(h, w, w, prev, cw)


def _ffn_out_kernel(a_ref, w_ref, x_ref, g_ref, o_ref):
    kk = pl.program_id(1)

    @pl.when(kk == 0)
    def _():
        o_ref[...] = x_ref[...]

    o_ref[...] += jnp.dot(a_ref[...], w_ref[...], preferred_element_type=F32)

    @pl.when(kk == pl.num_programs(1) - 1)
    def _():
        x = o_ref[...]
        ms = jnp.mean(x * x, axis=-1, keepdims=True)
        o_ref[...] = x * lax.rsqrt(ms + RMS_EPS) * g_ref[...]


def _ffn_out(act, w, x1, g, m, tm, tk):
    kdim, d = w.shape
    return pl.pallas_call(
        _ffn_out_kernel,
        out_shape=jax.ShapeDtypeStruct((m, d), F32),
        grid=(m // tm, kdim // tk),
        in_specs=[
            pl.BlockSpec((tm, tk), lambda i, k: (i, k)),
            pl.BlockSpec((tk, d), lambda i, k: (k, 0)),
            pl.BlockSpec((tm, d), lambda i, k: (i, 0)),
            pl.BlockSpec((1, d), lambda i, k: (0, 0)),
        ],
        out_specs=pl.BlockSpec((tm, d), lambda i, k: (i, 0)),
        compiler_params=pltpu.CompilerParams(
            dimension_semantics=("parallel", "arbitrary"), vmem_limit_bytes=VMEM_LIMIT),
        name="ffn_out",
    )(act, w, x1, g)


def _delta_kernel(qkvz_ref, ba_ref, prev_ref, s0_ref, cw_ref, alog_ref, dtb_ref, on_ref,
                  o_ref, s_ref, carry_ref, *, C, NB):
    R = NB * C
    ci = pl.program_id(1)

    @pl.when(ci == 0)
    def _():
        if NB == 1:
            carry_ref[...] = prev_ref[...]
        s_ref[...] = s0_ref[...]

    hist_ref = carry_ref if NB == 1 else prev_ref
    tri, ones_seq = _seq_masks(R, C)

    ba = ba_ref[...]
    beta_full = _sigmoid(ba)
    g_full = -jnp.exp(alog_ref[...]) * _softplus(ba + dtb_ref[...])
    gc_full = _xdot_r(tri, g_full, 3)
    gtot_full = gc_full[C - 1:C, :] if NB == 1 else _xdot_r(ones_seq, g_full, 3)

    def conv_silu(c0):
        x = qkvz_ref[:, c0:c0 + LANE]
        hist = hist_ref[:, c0:c0 + LANE]
        y = x * cw_ref[3:4, c0:c0 + LANE]
        for k in (1, 2, 3):
            y = y + _delayed(x, hist, k, NB) * cw_ref[3 - k:4 - k, c0:c0 + LANE]
        return _silu(y)

    ones_ll = jnp.ones((LANE, LANE), BF16)

    def lane_sum(x):
        return jnp.dot(x.astype(BF16), ones_ll, preferred_element_type=F32)

    def l2n_many(xs):
        sums = [lane_sum(x * x) for x in xs]
        return [x * lax.rsqrt(sq + 1e-6) for x, sq in zip(xs, sums)]

    def seq_rows(x, n):
        return x[n * C:(n + 1) * C]

    heads = range(H_A)
    seqs = range(NB)
    q = [x * (HEAD_A ** -0.5) for x in l2n_many([conv_silu(h * HEAD_A) for h in heads])]
    k = l2n_many([conv_silu(W_A + h * HEAD_A) for h in heads])
    v = [conv_silu(2 * W_A + h * HEAD_A) for h in heads]
    bcol = [beta_full[:, h:h + 1] for h in heads]
    gcol = [gc_full[:, H_A + h:H_A + h + 1] for h in heads]
    gtot = [gtot_full[:, H_A + h:H_A + h + 1] for h in heads]
    eg = [jnp.exp(gcol[h]) for h in heads]
    kb = [k[h] * bcol[h] for h in heads]

    pk = _Packed(R, C)
    hpairs = range(H_A // 2)
    zc = jnp.zeros((R, HEAD_A), F32)
    row2 = _iota((2 * R, LANE), 0)
    lane2 = _iota((2 * R, LANE), 1)
    ones_cl = jnp.ones((R, LANE), F32)
    gc2 = jnp.concatenate([gc_full, gc_full], axis=0)
    kq, gamma = [], []
    for hp in hpairs:
        h0, h1 = 2 * hp, 2 * hp + 1
        lhs = jnp.concatenate([jnp.concatenate([kb[h0], kb[h1]], axis=1),
                               jnp.concatenate([q[h0], q[h1]], axis=1)], axis=0)
        rk = jnp.concatenate([jnp.concatenate([k[h0], zc], axis=1),
                              jnp.concatenate([zc, k[h1]], axis=1)], axis=0)
        kq.append(_bdot_nt(lhs, rk))
        sel = lane2 == jnp.where(row2 < R, H_A + h0, H_A + h1)
        grow = _xdot_r(ones_cl, jnp.where(sel, gc2, 0.0), 3, NT_DIMS)
        gcol_p = jnp.where(pk.first, gcol[h0], gcol[h1])
        gamma.append(jnp.exp(jnp.where(pk.incl, gcol_p - grow, NEG_BIG)))
    tinv = pk.inverse_many([-jnp.where(pk.strict, kq[hp][0:R] * gamma[hp], 0.0) for hp in hpairs])
    uw = []
    for hp in hpairs:
        h0, h1 = 2 * hp, 2 * hp + 1
        z2 = jnp.zeros((R, 2 * HEAD_A), F32)
        rhs = jnp.concatenate([jnp.concatenate([v[h0] * bcol[h0], kb[h0] * eg[h0], z2], axis=1),
                               jnp.concatenate([z2, v[h1] * bcol[h1], kb[h1] * eg[h1]], axis=1)], axis=0)
        uw.append(_bdot(tinv[hp], rhs))
    u = [uw[h // 2][:, (h % 2) * 2 * HEAD_A:(h % 2) * 2 * HEAD_A + HEAD_A] for h in heads]
    w = [uw[h // 2][:, (h % 2) * 2 * HEAD_A + HEAD_A:(h % 2 + 1) * 2 * HEAD_A] for h in heads]
    qd = [q[h] * eg[h] for h in heads]
    s = [[s_ref[n, h] for h in heads] for n in seqs]
    wqs = [[_bdot(jnp.concatenate([seq_rows(w[h], n), seq_rows(qd[h], n)], axis=0), s[n][h]) for h in heads]
           for n in seqs]
    ws = [jnp.concatenate([wqs[n][h][0:C] for n in seqs], axis=0) if NB > 1 else wqs[0][h][0:C] for h in heads]
    qs = [jnp.concatenate([wqs[n][h][C:2 * C] for n in seqs], axis=0) if NB > 1 else wqs[0][h][C:2 * C]
          for h in heads]
    v_new = [u[h] - ws[h] for h in heads]
    o = []
    for hp in hpairs:
        h0, h1 = 2 * hp, 2 * hp + 1
        vn_bd = jnp.concatenate([jnp.concatenate([v_new[h0], zc], axis=1),
                                 jnp.concatenate([zc, v_new[h1]], axis=1)], axis=0)
        op = _bdot(kq[hp][R:2 * R] * gamma[hp], vn_bd)
        o.append(qs[h0] + op[:, :HEAD_A])
        o.append(qs[h1] + op[:, HEAD_A:])
    kd = [k[h] * jnp.exp(gtot[h] - gcol[h]) for h in heads]
    for n in seqs:
        for h in heads:
            glast = gtot[h] if NB == 1 else gtot[h][n * C:n * C + 1]
            s_ref[n, h] = s[n][h] * jnp.exp(glast) + _bdot_tn(seq_rows(kd[h], n), seq_rows(v_new[h], n))
    osq = [lane_sum(o[h] * o[h]) for h in heads]
    for h in heads:
        z = qkvz_ref[:, 3 * W_A + h * HEAD_A:3 * W_A + (h + 1) * HEAD_A]
        oh = o[h] * lax.rsqrt(osq[h] * (1.0 / HEAD_A) + RMS_EPS)
        o_ref[:, h * HEAD_A:(h + 1) * HEAD_A] = (oh * on_ref[...] * _silu(z)).astype(BF16)

    if NB == 1:
        carry_ref[...] = qkvz_ref[C - SUB:C, 0:3 * W_A]


def _delta_mixer(proj, row0, nb, seq, C, NB, prev, s0, cw, alog_row, dtb_row, onorm):
    nch = seq // C
    assert NB == 1 or (nch == 1 and C == SUB)
    R = NB * C
    blk0 = row0 // R
    bcast = s0.shape[0] == 1
    bsel = (lambda b: 0) if bcast else (lambda b: b)
    return pl.pallas_call(
        functools.partial(_delta_kernel, C=C, NB=NB),
        out_shape=(jax.ShapeDtypeStruct((nb * seq, W_A), BF16),
                   jax.ShapeDtypeStruct((nb, H_A, HEAD_A, HEAD_A), F32)),
        grid=(nb // NB, nch),
        in_specs=[
            pl.BlockSpec((R, 4 * W_A), lambda b, c: (blk0 + b * nch + c, 0)),
            pl.BlockSpec((R, LANE), lambda b, c: (blk0 + b * nch + c, SM0 // LANE)),
            pl.BlockSpec((NB * SUB, 3 * W_A), lambda b, c: (bsel(b), 0)),
            pl.BlockSpec((NB, H_A, HEAD_A, HEAD_A), lambda b, c: (bsel(b), 0, 0, 0)),
            pl.BlockSpec((4, 3 * W_A), lambda b, c: (0, 0)),
            pl.BlockSpec((1, LANE), lambda b, c: (0, 0)),
            pl.BlockSpec((1, LANE), lambda b, c: (0, 0)),
            pl.BlockSpec((1, HEAD_A), lambda b, c: (0, 0)),
        ],
        out_specs=(pl.BlockSpec((R, W_A), lambda b, c: (b * nch + c, 0)),
                   pl.BlockSpec((NB, H_A, HEAD_A, HEAD_A), lambda b, c: (b, 0, 0, 0))),
        scratch_shapes=[pltpu.VMEM((SUB, 3 * W_A), F32)],
        compiler_params=pltpu.CompilerParams(
            dimension_semantics=("parallel", "arbitrary"), vmem_limit_bytes=VMEM_LIMIT),
        name="delta_mixer",
    )(proj, proj, prev, s0, cw, alog_row, dtb_row, onorm)


def _rwkv_kernel(r_ref, k_ref, v_ref, sm_ref, pr_ref, pk_ref, pv_ref, psm_ref, s0_ref,
                 mur_ref, muk_ref, muv_ref, musm_ref, w2_ref, a2_ref, g2_ref,
                 w0_ref, a0_ref, kk_ref, ka_ref, rk_ref, lnw_ref, lnb_ref,
                 o_ref, s_ref, cr_ref, ck_ref, cv_ref, csm_ref, st_ref, *, C, NB, G):
    R = NB * C
    ci = pl.program_id(2)
    nch = pl.num_programs(2)
    seqs = range(NB)
    pairs = range(G)

    r2 = _iota((LANE, LANE), 0)
    c2 = _iota((LANE, LANE), 1)
    same_head = (r2 < HEAD_B) == (c2 < HEAD_B)
    ones_bd = same_head.astype(F32)
    spread = (_iota((HEAD_B, LANE), 0) == (_iota((HEAD_B, LANE), 1) & (HEAD_B - 1))).astype(F32)
    gather = ((_iota((LANE, HEAD_B), 0) & (HEAD_B - 1)) == _iota((LANE, HEAD_B), 1)).astype(F32)

    @pl.when(ci == 0)
    def _():
        if NB == 1:
            cr_ref[...] = pr_ref[...]
            ck_ref[...] = pk_ref[...]
            cv_ref[...] = pv_ref[...]
            csm_ref[...] = psm_ref[...]
        for n in seqs:
            for p in pairs:
                st_ref[n * G + p] = jnp.where(same_head, _xdot(s0_ref[n, p], spread, 3), 0.0)

    hr_ref, hk_ref, hv_ref, hsm_ref = ((cr_ref, ck_ref, cv_ref, csm_ref) if NB == 1
                                       else (pr_ref, pk_ref, pv_ref, psm_ref))
    tri, ones_seq = _seq_masks(R, C)

    def lerp(x, hist, mu):
        return x + (_delayed(x, hist, 1, NB) - x) * mu

    sm = sm_ref[...]
    xs = lerp(sm, hsm_ref[...], musm_ref[...])
    slab_wa = xs[:, SM_WA:SM_WA + LANE]
    slab_g = xs[:, SM_G:SM_G + 2 * LANE]

    def col(x, p):
        return x[:, p * LANE:(p + 1) * LANE]

    def to_rows(x):
        return jnp.concatenate([col(x, p) for p in pairs], axis=0)

    def to_cols(y):
        return jnp.concatenate([y[p * R:(p + 1) * R] for p in pairs], axis=1)

    def seq_rows(x, n):
        return x[n * C:(n + 1) * C]

    xr = lerp(r_ref[...], hr_ref[...], mur_ref[...])
    xk = lerp(k_ref[...], hk_ref[...], muk_ref[...])
    xv = lerp(v_ref[...], hv_ref[...], muv_ref[...])
    wlog = -_softplus(-(w0_ref[...] + _bdot(jnp.tanh(slab_wa), w2_ref[...]))) - 0.5
    ld = -jnp.exp(wlog)
    aa = _sigmoid(a0_ref[...] + _bdot(slab_wa, a2_ref[...]))
    gate = _bdot(_sigmoid(slab_g), g2_ref[...])
    kkr = xk * kk_ref[...]
    kkn = kkr * lax.rsqrt(to_cols(_bdot(to_rows(kkr * kkr), ones_bd)) + 1e-6)
    k2 = xk * (1.0 + (aa - 1.0) * ka_ref[...])
    lp = _xdot_r(tri, ld, 3)
    lp_tot = lp[C - 1:C, :] if NB == 1 else _xdot_r(ones_seq, ld, 3)
    e_neg = jnp.exp(-lp)
    e_rem = jnp.exp(lp_tot - lp)
    at = -kkn * jnp.exp(lp - ld)
    rt = xr * jnp.exp(lp)
    kb = kkn * aa
    bt = kb * e_neg
    kt = k2 * e_neg
    bhat = kb * e_rem
    khat = k2 * e_rem
    p_tot = jnp.exp(lp_tot)

    s = [[st_ref[n * G + p] for p in pairs] for n in seqs]
    atp = [col(at, p) for p in pairs]
    rtp = [col(rt, p) for p in pairs]
    ars = [[_bdot_nt(jnp.concatenate([seq_rows(atp[p], n), seq_rows(rtp[p], n)], axis=0), s[n][p])
            for p in pairs] for n in seqs]
    if NB == 1:
        x_state = [ars[0][p][0:C] for p in pairs]
        o_state = [ars[0][p][C:2 * C] for p in pairs]
    else:
        x_state = [jnp.concatenate([ars[n][p][0:C] for n in seqs], axis=0) for p in pairs]
        o_state = [jnp.concatenate([ars[n][p][C:2 * C] for n in seqs], axis=0) for p in pairs]
    pk = _Packed(R, C)
    stack_mask = (_iota((2 * R, LANE), 0) < R) == (_iota((2 * R, LANE), 1) < HEAD_B)

    def by_head(x):
        return jnp.where(stack_mask, jnp.concatenate([x, x], axis=0), 0.0)

    ar = [jnp.concatenate([atp[p], rtp[p]], axis=0) for p in pairs]
    ab = [_bdot_nt(ar[p], by_head(col(bt, p))) for p in pairs]
    ak = [_bdot_nt(ar[p], by_head(col(kt, p))) for p in pairs]
    tinv = pk.inverse_many([jnp.where(pk.strict, m[0:R], 0.0) for m in ab])
    v_bh = [by_head(col(xv, p)) for p in pairs]
    y = [x_state[p] + _bdot(jnp.where(pk.strict, ak[p][0:R], 0.0), v_bh[p]) for p in pairs]
    u = [_bdot(tinv[p], by_head(y[p])) for p in pairs]
    o = [o_state[p] + _bdot(jnp.where(pk.incl, ab[p][R:2 * R], 0.0), by_head(u[p]))
         + _bdot(jnp.where(pk.incl, ak[p][R:2 * R], 0.0), v_bh[p]) for p in pairs]
    for n in seqs:
        for p in pairs:
            uv = jnp.concatenate([seq_rows(u[p], n), seq_rows(col(xv, p), n)], axis=0)
            bkhat = jnp.concatenate([seq_rows(col(bhat, p), n), seq_rows(col(khat, p), n)], axis=0)
            decay = col(p_tot, p) if NB == 1 else col(p_tot, p)[n * C:n * C + 1]
            s_new = s[n][p] * decay + _bdot_tn(uv, bkhat)
            st_ref[n * G + p] = jnp.where(same_head, s_new, 0.0)

    o_rows = jnp.concatenate(o, axis=0)
    mean = _bdot(o_rows, ones_bd) * (1.0 / HEAD_B)
    d = o_rows - mean
    var = _bdot(d * d, ones_bd) * (1.0 / HEAD_B)
    on = to_cols(d * lax.rsqrt(var + GN_EPS)) * lnw_ref[...] + lnb_ref[...]
    bonus = to_cols(_bdot(to_rows(xr * k2 * rk_ref[...]), ones_bd)) * xv
    o_ref[...] = ((on + bonus) * gate).astype(BF16)

    if NB == 1:
        cr_ref[...] = r_ref[C - SUB:C, :]
        ck_ref[...] = k_ref[C - SUB:C, :]
        cv_ref[...] = v_ref[C - SUB:C, :]
        csm_ref[...] = sm_ref[C - SUB:C, :]

    @pl.when(ci == nch - 1)
    def _():
        for n in seqs:
            for p in pairs:
                s_ref[n, p] = _xdot(st_ref[n * G + p], gather, 3)


def _rwkv_mixer(proj, row0, nb, seq, C, NB, G, prev_rkv, prev_sm, s0, mu_rkv, mu_sm, w2p, a2p, g2p,
                w0, a0, k_k, k_a, r_k, lnw, lnb):
    nch = seq // C
    assert NB == 1 or (nch == 1 and C == SUB)
    R = NB * C
    ng = N_PAIR // G
    gw = G * LANE
    blk0 = row0 // R
    bcast = s0.shape[0] == 1
    bsel = (lambda b: 0) if bcast else (lambda b: b)

    def proj_spec(col0):
        return pl.BlockSpec((R, gw), lambda b, g, c: (blk0 + b * nch + c, col0 // gw + g))

    def prev_spec(part):
        return pl.BlockSpec((NB * SUB, gw), lambda b, g, c: (bsel(b), part * (W_B // gw) + g))

    def vec_spec(part=0):
        return pl.BlockSpec((1, gw), lambda b, g, c: (0, part * (W_B // gw) + g))

    in_specs = [
        proj_spec(RKV0), proj_spec(RKV0 + W_B), proj_spec(RKV0 + 2 * W_B),
        pl.BlockSpec((R, SM_W), lambda b, g, c: (blk0 + b * nch + c, SM0 // SM_W)),
        prev_spec(0), prev_spec(1), prev_spec(2),
        pl.BlockSpec((NB * SUB, SM_W), lambda b, g, c: (bsel(b), 0)),
        pl.BlockSpec((NB, G, LANE, HEAD_B), lambda b, g, c: (bsel(b), g, 0, 0)),
        vec_spec(0), vec_spec(1), vec_spec(2),
        pl.BlockSpec((1, SM_W), lambda b, g, c: (0, 0)),
        pl.BlockSpec((LANE, gw), lambda b, g, c: (0, g)),
        pl.BlockSpec((LANE, gw), lambda b, g, c: (0, g)),
        pl.BlockSpec((2 * LANE, gw), lambda b, g, c: (0, g)),
        vec_spec(), vec_spec(), vec_spec(), vec_spec(), vec_spec(), vec_spec(), vec_spec(),
    ]
    return pl.pallas_call(
        functools.partial(_rwkv_kernel, C=C, NB=NB, G=G),
        out_shape=(jax.ShapeDtypeStruct((nb * seq, W_B), BF16),
                   jax.ShapeDtypeStruct((nb, N_PAIR, LANE, HEAD_B), F32)),
        grid=(nb // NB, ng, nch),
        in_specs=in_specs,
        out_specs=(pl.BlockSpec((R, gw), lambda b, g, c: (b * nch + c, g)),
                   pl.BlockSpec((NB, G, LANE, HEAD_B), lambda b, g, c: (b, g, 0, 0))),
        scratch_shapes=[pltpu.VMEM((SUB, gw), F32), pltpu.VMEM((SUB, gw), F32), pltpu.VMEM((SUB, gw), F32),
                        pltpu.VMEM((SUB, SM_W), F32), pltpu.VMEM((NB * G, LANE, LANE), F32)],
        compiler_params=pltpu.CompilerParams(
            dimension_semantics=("parallel", "parallel", "arbitrary"), vmem_limit_bytes=VMEM_LIMIT),
        name="rwkv_mixer",
    )(proj, proj, proj, proj, prev_rkv, prev_rkv, prev_rkv, prev_sm, s0,
      mu_rkv, mu_rkv, mu_rkv, mu_sm, w2p, a2p, g2p, w0, a0, k_k, k_a, r_k, lnw, lnb)


_AT, _RT, _BT, _KT, _BHAT, _KHAT, _XV, _GATE, _BONUS, _N_PRE = range(10)


def _rwkv_seq_kernel(r0_ref, k0_ref, v0_ref, sm0_ref, rn_ref, kn_ref, vn_ref, smn_ref,
                     pr_ref, pk_ref, pv_ref, psm_ref, s0_ref,
                     mur_ref, muk_ref, muv_ref, musm_ref, w2_ref, a2_ref, g2_ref,
                     w0_ref, a0_ref, kk_ref, ka_ref, rk_ref, lnw_ref, lnb_ref,
                     o_ref, s_ref, cr_ref, ck_ref, cv_ref, csm_ref, st_ref, pre_ref, ptot_ref, *, C, G):
    ci = pl.program_id(2)
    nch = pl.num_programs(2)
    pairs = range(G)

    r2 = _iota((LANE, LANE), 0)
    c2 = _iota((LANE, LANE), 1)
    same_head = (r2 < HEAD_B) == (c2 < HEAD_B)
    ones_bd = same_head.astype(F32)
    spread = (_iota((HEAD_B, LANE), 0) == (_iota((HEAD_B, LANE), 1) & (HEAD_B - 1))).astype(F32)
    gather = ((_iota((LANE, HEAD_B), 0) & (HEAD_B - 1)) == _iota((LANE, HEAD_B), 1)).astype(F32)
    tri, _ = _seq_masks(C, C)

    def col(x, p):
        return x[:, p * LANE:(p + 1) * LANE]

    def to_rows(x):
        return jnp.concatenate([col(x, p) for p in pairs], axis=0)

    def to_cols(y):
        return jnp.concatenate([y[p * C:(p + 1) * C] for p in pairs], axis=1)

    def lerp(x, hist, mu):
        return x + (_shifted(x, hist, 1) - x) * mu

    def preamble(r_ref, k_ref, v_ref, sm_ref, hr_ref, hk_ref, hv_ref, hsm_ref, slot):
        xs = lerp(sm_ref[...], hsm_ref[...], musm_ref[...])
        yield
        slab_wa = xs[:, SM_WA:SM_WA + LANE]
        wl = _bdot(jnp.tanh(slab_wa), w2_ref[...])
        yield
        al = _bdot(slab_wa, a2_ref[...])
        yield
        pre_ref[slot, _GATE] = _bdot(_sigmoid(xs[:, SM_G:SM_G + 2 * LANE]), g2_ref[...])
        yield
        xr = lerp(r_ref[...], hr_ref[...], mur_ref[...])
        yield
        xk = lerp(k_ref[...], hk_ref[...], muk_ref[...])
        yield
        xv = lerp(v_ref[...], hv_ref[...], muv_ref[...])
        pre_ref[slot, _XV] = xv
        yield
        ld = -jnp.exp(-_softplus(-(w0_ref[...] + wl)) - 0.5)
        yield
        lp = _xdot_r(tri, ld, 3)
        yield
        aa = _sigmoid(a0_ref[...] + al)
        yield
        kkr = xk * kk_ref[...]
        ss = to_cols(_bdot(to_rows(kkr * kkr), ones_bd))
        yield
        kkn = kkr * lax.rsqrt(ss + 1e-6)
        k2 = xk * (1.0 + (aa - 1.0) * ka_ref[...])
        yield
        pre_ref[slot, _BONUS] = to_cols(_bdot(to_rows(xr * k2 * rk_ref[...]), ones_bd)) * xv
        yield
        pre_ref[slot, _AT] = -kkn * jnp.exp(lp - ld)
        yield
        pre_ref[slot, _RT] = xr * jnp.exp(lp)
        yield
        kb = kkn * aa
        e_neg = jnp.exp(-lp)
        pre_ref[slot, _BT] = kb * e_neg
        yield
        pre_ref[slot, _KT] = k2 * e_neg
        yield
        lp_tot = lp[C - 1:C, :]
        e_rem = jnp.exp(lp_tot - lp)
        pre_ref[slot, _BHAT] = kb * e_rem
        yield
        pre_ref[slot, _KHAT] = k2 * e_rem
        ptot_ref[slot] = jnp.broadcast_to(jnp.exp(lp_tot), (SUB, lp.shape[1]))
        yield
        cr_ref[...] = r_ref[C - SUB:C, :]
        ck_ref[...] = k_ref[C - SUB:C, :]
        cv_ref[...] = v_ref[C - SUB:C, :]
        csm_ref[...] = sm_ref[C - SUB:C, :]

    @pl.when(ci == 0)
    def _():
        for _ in preamble(r0_ref, k0_ref, v0_ref, sm0_ref, pr_ref, pk_ref, pv_ref, psm_ref, 0):
            pass
        for p in pairs:
            st_ref[p] = jnp.where(same_head, _xdot(s0_ref[0, p], spread, 3), 0.0)

    slot = ci % 2
    ahead = preamble(rn_ref, kn_ref, vn_ref, smn_ref, cr_ref, ck_ref, cv_ref, csm_ref, 1 - slot)
    calls = [0]

    def tick():
        calls[0] += 1
        if calls[0] % 6 == 0:
            next(ahead, None)

    def each(fn, items):
        out = []
        for it in items:
            out.append(fn(it))
            tick()
        return out

    def pre(idx, p):
        return pre_ref[slot, idx, :, p * LANE:(p + 1) * LANE]

    pk = _Packed(C, C)
    stack_mask = (_iota((2 * C, LANE), 0) < C) == (_iota((2 * C, LANE), 1) < HEAD_B)

    def by_head(x):
        return jnp.where(stack_mask, jnp.concatenate([x, x], axis=0), 0.0)

    s = [st_ref[p] for p in pairs]
    ar = [jnp.concatenate([pre(_AT, p), pre(_RT, p)], axis=0) for p in pairs]
    ars = each(lambda p: _bdot_nt(ar[p], s[p]), pairs)
    ab = each(lambda p: _bdot_nt(ar[p], by_head(pre(_BT, p))), pairs)
    ak = each(lambda p: _bdot_nt(ar[p], by_head(pre(_KT, p))), pairs)
    ts = [pk.eye + jnp.where(pk.strict, m[0:C], 0.0) for m in ab]
    ps = [jnp.where(pk.strict, m[0:C], 0.0) for m in ab]
    n = 1
    while 2 * n < C:
        ps = each(lambda p: _bdot(ps[p], pk.block_diag(ps[p])), pairs)
        ts = each(lambda p: ts[p] + _bdot(ps[p], pk.block_diag(ts[p])), pairs)
        n *= 2
    xv = [pre(_XV, p) for p in pairs]
    v_bh = [by_head(xv[p]) for p in pairs]
    y = each(lambda p: ars[p][0:C] + _bdot(jnp.where(pk.strict, ak[p][0:C], 0.0), v_bh[p]), pairs)
    u = each(lambda p: _bdot(ts[p], by_head(y[p])), pairs)
    o = each(lambda p: ars[p][C:2 * C] + _bdot(jnp.where(pk.incl, ab[p][C:2 * C], 0.0), by_head(u[p]))
             + _bdot(jnp.where(pk.incl, ak[p][C:2 * C], 0.0), v_bh[p]), pairs)
    for p in pairs:
        uv = jnp.concatenate([u[p], xv[p]], axis=0)
        bkhat = jnp.concatenate([pre(_BHAT, p), pre(_KHAT, p)], axis=0)
        s_new = s[p] * ptot_ref[slot, 0:1, p * LANE:(p + 1) * LANE] + _bdot_tn(uv, bkhat)
        st_ref[p] = jnp.where(same_head, s_new, 0.0)
        tick()

    o_rows = jnp.concatenate(o, axis=0)
    mean = _bdot(o_rows, ones_bd) * (1.0 / HEAD_B)
    d = o_rows - mean
    var = _bdot(d * d, ones_bd) * (1.0 / HEAD_B)
    on = to_cols(d * lax.rsqrt(var + GN_EPS)) * lnw_ref[...] + lnb_ref[...]
    o_ref[...] = ((on + pre_ref[slot, _BONUS]) * pre_ref[slot, _GATE]).astype(BF16)
    for _ in ahead:
        pass

    @pl.when(ci == nch - 1)
    def _():
        for p in pairs:
            s_ref[0, p] = _xdot(st_ref[p], gather, 3)


def _rwkv_seq_mixer(proj, nb, seq, C, prev_rkv, prev_sm, s0, mu_rkv, mu_sm, w2p, a2p, g2p,
                    w0, a0, k_k, k_a, r_k, lnw, lnb):
    G = N_PAIR
    nch = seq // C
    gw = G * LANE

    def first_spec(col0, w):
        return pl.BlockSpec((C, w), lambda b, g, c: (b * nch, col0 // w))

    def next_spec(col0, w):
        return pl.BlockSpec((C, w), lambda b, g, c: (b * nch + jnp.minimum(c + 1, nch - 1), col0 // w))

    def prev_spec(part):
        return pl.BlockSpec((SUB, gw), lambda b, g, c: (0, part))

    def vec_spec(part=0):
        return pl.BlockSpec((1, gw), lambda b, g, c: (0, part))

    in_specs = [
        first_spec(RKV0, gw), first_spec(RKV0 + W_B, gw), first_spec(RKV0 + 2 * W_B, gw), first_spec(SM0, SM_W),
        next_spec(RKV0, gw), next_spec(RKV0 + W_B, gw), next_spec(RKV0 + 2 * W_B, gw), next_spec(SM0, SM_W),
        prev_spec(0), prev_spec(1), prev_spec(2),
        pl.BlockSpec((SUB, SM_W), lambda b, g, c: (0, 0)),
        pl.BlockSpec((1, G, LANE, HEAD_B), lambda b, g, c: (0, 0, 0, 0)),
        vec_spec(0), vec_spec(1), vec_spec(2),
        pl.BlockSpec((1, SM_W), lambda b, g, c: (0, 0)),
        pl.BlockSpec((LANE, gw), lambda b, g, c: (0, 0)),
        pl.BlockSpec((LANE, gw), lambda b, g, c: (0, 0)),
        pl.BlockSpec((2 * LANE, gw), lambda b, g, c: (0, 0)),
        vec_spec(), vec_spec(), vec_spec(), vec_spec(), vec_spec(), vec_spec(), vec_spec(),
    ]
    return pl.pallas_call(
        functools.partial(_rwkv_seq_kernel, C=C, G=G),
        out_shape=(jax.ShapeDtypeStruct((nb * seq, W_B), BF16),
                   jax.ShapeDtypeStruct((nb, N_PAIR, LANE, HEAD_B), F32)),
        grid=(nb, 1, nch),
        in_specs=in_specs,
        out_specs=(pl.BlockSpec((C, gw), lambda b, g, c: (b * nch + c, 0)),
                   pl.BlockSpec((1, G, LANE, HEAD_B), lambda b, g, c: (b, 0, 0, 0))),
        scratch_shapes=[pltpu.VMEM((SUB, gw), F32), pltpu.VMEM((SUB, gw), F32), pltpu.VMEM((SUB, gw), F32),
                        pltpu.VMEM((SUB, SM_W), F32), pltpu.VMEM((G, LANE, LANE), F32),
                        pltpu.VMEM((2, _N_PRE, C, gw), F32), pltpu.VMEM((2, SUB, gw), F32)],
        compiler_params=pltpu.CompilerParams(
            dimension_semantics=("parallel", "arbitrary", "arbitrary"), vmem_limit_bytes=VMEM_LIMIT),
        name="rwkv_seq_mixer",
    )(proj, proj, proj, proj, proj, proj, proj, proj, prev_rkv, prev_rkv, prev_rkv, prev_sm, s0,
      mu_rkv, mu_rkv, mu_rkv, mu_sm, w2p, a2p, g2p, w0, a0, k_k, k_a, r_k, lnw, lnb)


def _small_layout(cols_ba, cols_w, cols_a, cols_g, axis=-1):
    def z(n):
        shape = list(cols_w.shape)
        shape[axis] = n
        return jnp.zeros(shape, cols_w.dtype)
    return jnp.concatenate(
        [cols_ba, z(SM_WA - cols_ba.shape[axis]), cols_w, cols_a, cols_g,
         z(SM_W - SM_G - cols_g.shape[axis])], axis=axis)


def _pad_rows8(x):
    b, n, c = x.shape
    return jnp.concatenate([jnp.zeros((b, SUB - n, c), x.dtype), x], axis=1).reshape(b * SUB, c)


def kernel(x_prompt, x_sample, state_delta, state_conv_qkv, state_wkv, state_shift, state_ffn_conv, meta, norm1, w_in, conv_a, a_log, dt_bias, onorm_a, mu_b, w0, w2, a0, a2, g2, k_k, k_a, r_k, lnx_w, lnx_b, w_o, norm2, w_ffn_in, conv_f, w_ffn_out, norm_f):
    nbp, seq_p, _ = x_prompt.shape
    nbs, seq_s, _ = x_sample.shape
    n_s = nbs * seq_s
    assert w_in.shape[0] == 1, "single-layer trunk"
    assert seq_s == SUB
    l = 0

    wt = w_in[l].T
    o_b = A_PROJ
    o_l = A_PROJ + 3 * W_B
    w_cat_t = jnp.concatenate([
        wt[:4 * W_A], wt[o_b:o_l],
        _small_layout(wt[4 * W_A:A_PROJ], wt[o_l:o_l + W_LORA], wt[o_l + W_LORA:o_l + W_LORA + A_LORA],
                      wt[o_l + W_LORA + A_LORA:], axis=0)], axis=0).astype(BF16)
    mu = mu_b[l]
    mu_rkv = mu[None, :3 * W_B]
    mu_sm = _small_layout(jnp.zeros((1, 2 * H_A), F32), mu[None, 3 * W_B:3 * W_B + W_LORA],
                          mu[None, 3 * W_B + W_LORA:3 * W_B + W_LORA + A_LORA],
                          mu[None, 3 * W_B + W_LORA + A_LORA:])
    w2p = jnp.concatenate([w2[l], jnp.zeros((LANE - W_LORA, W_B), F32)], axis=0)
    a2p = jnp.concatenate([jnp.zeros((W_LORA, W_B), F32), a2[l]], axis=0)
    g2p = jnp.concatenate([g2[l], jnp.zeros((2 * LANE - G_LORA, W_B), F32)], axis=0)
    alog_row = jnp.concatenate([jnp.zeros((H_A,), F32), a_log[l], jnp.zeros((LANE - 2 * H_A,), F32)])[None]
    dtb_row = jnp.concatenate([jnp.zeros((H_A,), F32), dt_bias[l], jnp.zeros((LANE - 2 * H_A,), F32)])[None]
    wo_bf = w_o[l].astype(BF16)
    wfo_bf = w_ffn_out[l].astype(BF16)
    row = lambda v: v.reshape(1, -1)

    def mix(proj, row0, nb, seq, C, NB, prev_qkv, prev_rkv, prev_sm, s_delta, s_wkv):
        oa, sd = _delta_mixer(proj, row0, nb, seq, C, NB, prev_qkv, s_delta, conv_a[l], alog_row, dtb_row,
                              row(onorm_a[l]))
        rwkv_params = (mu_rkv, mu_sm, w2p, a2p, g2p, row(w0[l]), row(a0[l]), row(k_k[l]), row(k_a[l]),
                       row(r_k[l]), row(lnx_w[l]), row(lnx_b[l]))
        if seq // C > 1:
            assert row0 == 0 and NB == 1
            ob, sw = _rwkv_seq_mixer(proj, nb, seq, C, prev_rkv, prev_sm, s_wkv, *rwkv_params)
        else:
            ob, sw = _rwkv_mixer(proj, row0, nb, seq, C, NB, N_PAIR, prev_rkv, prev_sm, s_wkv, *rwkv_params)
        return oa, ob, sd, sw

    xs_rows = jnp.concatenate([x_sample.reshape(n_s, D_MODEL), meta], axis=0)
    xp_rows = x_prompt.reshape(nbp * seq_p, D_MODEL)
    n_small = n_s + N_META
    proj_s, proj_tail_s = _norm_matmul(xs_rows, row(norm1[l]), w_cat_t, n_small, 1536, 3, nbs)
    proj_p = _norm_matmul(xp_rows, row(norm1[l]), w_cat_t, 1024, 1536)

    zeros = lambda *s: jnp.zeros(s, F32)
    oa_m, ob_m, sd_m, sw_m = mix(proj_s, n_s, 1, N_META, N_META, 1, zeros(SUB, 3 * W_A),
                                 zeros(SUB, 3 * W_B), zeros(SUB, SM_W),
                                 zeros(1, H_A, HEAD_A, HEAD_A), zeros(1, N_PAIR, LANE, HEAD_B))
    tail = proj_s[n_small - SUB:n_small]
    oa_p, ob_p, sd_p, sw_p = mix(proj_p, 0, nbp, seq_p, 64, 1, tail[:, :3 * W_A],
                                 tail[:, RKV0:SM0], tail[:, SM0:], sd_m, sw_m)
    sh = state_shift[l]
    sh_sm = _small_layout(jnp.zeros((nbs, 1, 2 * H_A), F32), sh[..., 3 * W_B:3 * W_B + W_LORA],
                          sh[..., 3 * W_B + W_LORA:3 * W_B + W_LORA + A_LORA],
                          sh[..., 3 * W_B + W_LORA + A_LORA:])
    oa_s, ob_s, sd_s, sw_s = mix(proj_s, 0, nbs, seq_s, seq_s, 8, _pad_rows8(state_conv_qkv[l]),
                                 _pad_rows8(sh[..., :3 * W_B]), _pad_rows8(sh_sm), state_delta[l],
                                 state_wkv[l].reshape(nbs, N_PAIR, LANE, HEAD_B))

    oa_small = jnp.concatenate([oa_s, oa_m], axis=0)
    ob_small = jnp.concatenate([ob_s, ob_m], axis=0)
    x1_s, h2_s = _out_proj(xs_rows, oa_small, ob_small, wo_bf, row(norm2[l]), n_small // 5)
    x1_p, h2_p = _out_proj(xp_rows, oa_p, ob_p, wo_bf, row(norm2[l]), 512)
    prev_rows = jnp.concatenate([_pad_rows8(state_ffn_conv[l]), jnp.zeros((N_META, D_FF), F32)], axis=0)
    act_s, gate_tail_s, gate_last_s = _ffn_in(h2_s, w_ffn_in[l], prev_rows, conv_f[l], n_small, 768, 3, 0,
                                              n_s + SUB)
    act_p, tail_p = _ffn_in(h2_p, w_ffn_in[l], gate_last_s[None], conv_f[l], 1024, 768, 3, seq_p // 1024)
    y_s = _ffn_out(act_s, wfo_bf, x1_s, row(norm_f), n_s, 1024, 768)
    y_p = _ffn_out(act_p, wfo_bf, x1_p, row(norm_f), nbp * seq_p, 1024, 768)

    def states(conv_new, last, ffn_new, nb, sd, sw):
        shift_new = jnp.concatenate([last[..., RKV0:SM0], last[..., SM0 + SM_WA:SM0 + SM_WA + W_LORA + A_LORA],
                                     last[..., SM0 + SM_G:SM0 + SM_G + G_LORA]], axis=-1)
        return (sd[None], conv_new[None], sw.reshape(nb, H_B, HEAD_B, HEAD_B)[None], shift_new[None],
                ffn_new[None])

    p3 = proj_p.reshape(nbp, seq_p, P_CAT)
    tail_s = proj_tail_s.transpose(1, 0, 2)
    return ((y_p.reshape(nbp, seq_p, D_MODEL), y_s.reshape(nbs, seq_s, D_MODEL))
            + states(p3[:, seq_p - 3:, :3 * W_A], p3[:, seq_p - 1:, :], tail_p[:, SUB - 2:, :], nbp, sd_p, sw_p)
            + states(tail_s[:, :, :3 * W_A], tail_s[:, 2:, :], gate_tail_s.transpose(1, 0, 2), nbs, sd_s, sw_s))
```

```python
import functools

import jax
import jax.numpy as jnp
from jax import lax
from jax.experimental import pallas as pl
from jax.experimental.pallas import tpu as pltpu

F32 = jnp.float32
BF16 = jnp.bfloat16

D_MODEL = 2048
N_META = 16
W_A = 1024
HEAD_A = 128
H_A = 8
W_B = 1024
HEAD_B = 64
H_B = 16
N_PAIR = H_B // 2
W_LORA = 64
A_LORA = 64
G_LORA = 160
D_FF = 5376
RMS_EPS = 1e-6
GN_EPS = 64e-5
A_PROJ = 4 * W_A + 2 * H_A
B_PROJ = 3 * W_B + W_LORA + A_LORA + G_LORA

QKVZ0 = 0
RKV0 = 4 * W_A
SM0 = RKV0 + 3 * W_B
SM_W = 512
SM_BA = 0
SM_WA = 128
SM_G = 256
P_CAT = SM0 + SM_W

LANE = 128
SUB = 8
VMEM_LIMIT = 48 * 1024 * 1024
VMEM_LIMIT_BIG = 58 * 1024 * 1024
NEG_BIG = -1e30

NT_DIMS = (((1,), (1,)), ((), ()))


def _bdot(a, b):
    return jnp.dot(a.astype(BF16), b.astype(BF16), preferred_element_type=F32)


def _bdot_nt(a, b):
    return lax.dot_general(a.astype(BF16), b.astype(BF16), NT_DIMS, preferred_element_type=F32)


def _bdot_tn(a, b):
    return lax.dot_general(a.astype(BF16), b.astype(BF16), (((0,), (0,)), ((), ())),
                           preferred_element_type=F32)


def _pieces(a, n):
    out = []
    rem = a
    for i in range(n):
        p = rem.astype(BF16)
        out.append(p)
        if i + 1 < n:
            rem = rem - p.astype(F32)
    return out


def _xdot(a, b, n, dims=(((1,), (0,)), ((), ()))):
    bb = b.astype(BF16)
    acc = None
    for p in _pieces(a, n):
        t = lax.dot_general(p, bb, dims, preferred_element_type=F32)
        acc = t if acc is None else acc + t
    return acc


def _xdot_r(a, b, n, dims=(((1,), (0,)), ((), ()))):
    ab = a.astype(BF16)
    acc = None
    for p in _pieces(b, n):
        t = lax.dot_general(ab, p, dims, preferred_element_type=F32)
        acc = t if acc is None else acc + t
    return acc


def _sigmoid(x):
    return 1.0 / (1.0 + jnp.exp(-x))


def _silu(x):
    return x * _sigmoid(x)


def _softplus(x):
    return jnp.maximum(x, 0.0) + jnp.log(1.0 + jnp.exp(-jnp.abs(x)))


def _iota(shape, dim):
    return lax.broadcasted_iota(jnp.int32, shape, dim)


def _shifted(x, prev8, k):
    n = x.shape[0]
    xr = pltpu.roll(x, k, 0)
    pr = pltpu.roll(prev8, k, 0)
    first = jnp.where(_iota((SUB, x.shape[1]), 0) < k, pr, xr[0:SUB])
    if n == SUB:
        return first
    return jnp.concatenate([first, xr[SUB:]], axis=0)


def _repeat_rows(s, reps):
    n, c = s.shape
    return jnp.broadcast_to(s[:, None, :], (n, reps, c)).reshape(n * reps, c)


def _delayed(x, hist, k, nseq):
    if nseq == 1:
        return _shifted(x, hist, k)
    t = _iota(x.shape, 0) & (SUB - 1)
    d = pltpu.roll(x, k, 0)
    for tt in range(k):
        d = jnp.where(t == tt, hist[len(hist) - k + tt], d)
    return d


def _same_seq(shape, c, rows_total):
    if c == rows_total:
        return None
    sh = c.bit_length() - 1
    return (lax.shift_right_logical(_iota(shape, 0), sh)
            == lax.shift_right_logical(_iota(shape, 1) & (rows_total - 1), sh))


class _Packed:
    def __init__(self, r, c):
        self.r = r
        self.c = c
        row = _iota((r, 2 * r), 0)
        col = _iota((r, 2 * r), 1) & (r - 1)
        same = _same_seq((r, 2 * r), c, r)
        self.incl = (row >= col) if same is None else ((row >= col) & same)
        self.strict = (row > col) if same is None else ((row > col) & same)
        self.eye = (row == col).astype(F32)
        self.first = _iota((r, 2 * r), 1) < r
        self.bd_mask = (_iota((2 * r, 2 * r), 0) < r) == (_iota((2 * r, 2 * r), 1) < r)

    def block_diag(self, m):
        return jnp.where(self.bd_mask, jnp.concatenate([m, m], axis=0), 0.0)

    def inverse_many(self, xs):
        ts = [self.eye + x for x in xs]
        ps = list(xs)
        n = 1
        while 2 * n < self.c:
            ps = [_bdot(p, self.block_diag(p)) for p in ps]
            ts = [t + _bdot(p, self.block_diag(t)) for p, t in zip(ps, ts)]
            n *= 2
        return ts


def _seq_masks(r, c):
    row = _iota((r, r), 0)
    col = _iota((r, r), 1)
    same = _same_seq((r, r), c, r)
    if same is None:
        return (row >= col).astype(F32), None
    return ((row >= col) & same).astype(F32), same.astype(F32)


def _seq_tail_rows(x, slab_ref, dst_ref, n_tail, nseq, col0=0):
    for s in range(x.shape[1] // LANE):
        slab_ref[s] = x[:, s * LANE:(s + 1) * LANE]
        for t in range(n_tail):
            dst_ref[t, :, col0 + s * LANE:col0 + (s + 1) * LANE] = (
                slab_ref[s, pl.ds(SUB - n_tail + t, nseq, stride=SUB), :])


def _norm_matmul_kernel(x_ref, g_ref, w_ref, o_ref, *rest, n_tail, nseq):
    tail_ref, h_ref, slab_ref = rest if n_tail else (None,) + rest + (None,)

    @pl.when(pl.program_id(1) == 0)
    def _():
        x = x_ref[...]
        ms = jnp.mean(x * x, axis=-1, keepdims=True)
        h_ref[...] = (x * lax.rsqrt(ms + RMS_EPS) * g_ref[...]).astype(BF16)

    out = lax.dot_general(h_ref[...], w_ref[...], NT_DIMS, preferred_element_type=F32)
    o_ref[...] = out
    if n_tail:
        _seq_tail_rows(out, slab_ref, tail_ref, n_tail, nseq)


def _norm_matmul(x, g, wt, tm, tn, n_tail=0, nseq=0):
    m, k = x.shape
    n = wt.shape[0]
    out_shape = jax.ShapeDtypeStruct((m, n), F32)
    out_specs = pl.BlockSpec((tm, tn), lambda i, j: (i, j))
    if n_tail:
        assert m == tm
        out_shape = (out_shape, jax.ShapeDtypeStruct((n_tail, nseq, n), F32))
        out_specs = (out_specs, pl.BlockSpec((n_tail, nseq, tn), lambda i, j: (0, 0, j)))
    return pl.pallas_call(
        functools.partial(_norm_matmul_kernel, n_tail=n_tail, nseq=nseq),
        out_shape=out_shape,
        grid=(m // tm, n // tn),
        in_specs=[
            pl.BlockSpec((tm, k), lambda i, j: (i, 0)),
            pl.BlockSpec((1, k), lambda i, j: (0, 0)),
            pl.BlockSpec((tn, k), lambda i, j: (j, 0)),
        ],
        out_specs=out_specs,
        scratch_shapes=[pltpu.VMEM((tm, k), BF16)] + ([pltpu.VMEM((tn // LANE, tm, LANE), F32)] if n_tail else []),
        compiler_params=pltpu.CompilerParams(
            dimension_semantics=("parallel", "arbitrary"), vmem_limit_bytes=VMEM_LIMIT_BIG),
        name="norm_matmul",
    )(x, g, wt)


def _out_proj_kernel(x_ref, oa_ref, ob_ref, wt_ref, wb_ref, g_ref, x1_ref, h_ref):
    acc = jnp.dot(oa_ref[...], wt_ref[...], preferred_element_type=F32)
    acc = acc + jnp.dot(ob_ref[...], wb_ref[...], preferred_element_type=F32)
    x1 = x_ref[...] + acc
    x1_ref[...] = x1
    ms = jnp.mean(x1 * x1, axis=-1, keepdims=True)
    h_ref[...] = (x1 * lax.rsqrt(ms + RMS_EPS) * g_ref[...]).astype(BF16)


def _out_proj(x, oa, ob, wo, g, tm):
    m, d = x.shape
    return pl.pallas_call(
        _out_proj_kernel,
        out_shape=(jax.ShapeDtypeStruct((m, d), F32), jax.ShapeDtypeStruct((m, d), BF16)),
        grid=(m // tm,),
        in_specs=[
            pl.BlockSpec((tm, d), lambda i: (i, 0)),
            pl.BlockSpec((tm, W_A), lambda i: (i, 0)),
            pl.BlockSpec((tm, W_B), lambda i: (i, 0)),
            pl.BlockSpec((W_A, d), lambda i: (0, 0)),
            pl.BlockSpec((W_B, d), lambda i: (W_A // W_B, 0)),
            pl.BlockSpec((1, d), lambda i: (0, 0)),
        ],
        out_specs=(pl.BlockSpec((tm, d), lambda i: (i, 0)), pl.BlockSpec((tm, d), lambda i: (i, 0))),
        compiler_params=pltpu.CompilerParams(
            dimension_semantics=("parallel",), vmem_limit_bytes=VMEM_LIMIT),
        name="out_proj",
    )(x, oa, ob, wo, wo, g)


def _ffn_in_kernel(h_ref, wg_ref, wu_ref, prev_ref, cw_ref, act_ref, tail_ref, *rest,
                   tiles_per_seq, state_rows, nsub):
    if tiles_per_seq:
        wgb_ref, wub_ref, carry_ref = rest
    else:
        last_ref, wgb_ref, wub_ref, carry_ref, slab_ref = rest
    m = pl.program_id(1)

    @pl.when(m == 0)
    def _():
        wgb_ref[...] = wg_ref[...].astype(BF16)
        wub_ref[...] = wu_ref[...].astype(BF16)

    if tiles_per_seq:
        @pl.when(m % tiles_per_seq == 0)
        def _():
            carry_ref[...] = prev_ref[...]

    h = h_ref[...]
    n = h.shape[0]
    w = wgb_ref.shape[1] // nsub
    for i in range(nsub):
        cs = slice(i * w, (i + 1) * w)
        gate = jnp.dot(h, wgb_ref[:, cs], preferred_element_type=F32)
        up = jnp.dot(h, wub_ref[:, cs], preferred_element_type=F32)
        if tiles_per_seq:
            prev8 = carry_ref[:, cs]
            d1 = _shifted(gate, prev8, 1)
            d2 = _shifted(gate, prev8, 2)
            carry_ref[:, cs] = gate[n - SUB:n]
            tail_ref[:, cs] = gate[n - SUB:n]
        else:
            nseq = (state_rows - SUB) // SUB
            _seq_tail_rows(gate, slab_ref, tail_ref, 2, nseq, i * w)
            last_ref[:, cs] = gate[n - SUB:n]
            pad = jnp.zeros((n - nseq * SUB, w), F32)
            hist = [jnp.concatenate([_repeat_rows(prev_ref[t, :, cs], SUB), pad], axis=0) for t in range(2)]
            t_in_seq = jnp.where(_iota(gate.shape, 0) < state_rows, _iota(gate.shape, 0) & (SUB - 1), SUB)
            d1 = jnp.where(t_in_seq == 0, hist[1], pltpu.roll(gate, 1, 0))
            d2 = jnp.where(t_in_seq == 0, hist[0], jnp.where(t_in_seq == 1, hist[1], pltpu.roll(gate, 2, 0)))
        y = gate * cw_ref[2:3, cs] + d1 * cw_ref[1:2, cs] + d2 * cw_ref[0:1, cs]
        act_ref[:, cs] = (_silu(y) * up).astype(BF16)


def _ffn_in(h, w, prev, cw, tm, tn, nsub, tiles_per_seq, state_rows=0):
    m, d = h.shape
    nj = D_FF // tn
    nm = m // tm
    if tiles_per_seq:
        nseq = nm // tiles_per_seq
        prev_spec = pl.BlockSpec((None, SUB, tn), lambda j, i: (0, 0, j))
        extra_shapes = (jax.ShapeDtypeStruct((nseq, SUB, D_FF), F32),)
        extra_specs = (pl.BlockSpec((None, SUB, tn), lambda j, i: (i // tiles_per_seq, 0, j)),)
        extra_scratch = []
    else:
        assert nm == 1
        nseq = (state_rows - SUB) // SUB
        prev_spec = pl.BlockSpec((2, nseq, tn), lambda j, i: (0, 0, j))
        extra_shapes = (jax.ShapeDtypeStruct((2, nseq, D_FF), F32), jax.ShapeDtypeStruct((SUB, D_FF), F32))
        extra_specs = (pl.BlockSpec((2, nseq, tn), lambda j, i: (0, 0, j)),
                       pl.BlockSpec((SUB, tn), lambda j, i: (0, j)))
        extra_scratch = [pltpu.VMEM((tn // nsub // LANE, tm, LANE), F32)]
    return pl.pallas_call(
        functools.partial(_ffn_in_kernel, tiles_per_seq=tiles_per_seq, state_rows=state_rows, nsub=nsub),
        out_shape=(jax.ShapeDtypeStruct((m, D_FF), BF16),) + extra_shapes,
        grid=(nj, nm),
        in_specs=[
            pl.BlockSpec((tm, d), lambda j, i: (i, 0)),
            pl.BlockSpec((d, tn), lambda j, i: (0, j)),
            pl.BlockSpec((d, tn), lambda j, i: (0, nj + j)),
            prev_spec,
            pl.BlockSpec((3, tn), lambda j, i: (0, j)),
        ],
        out_specs=(pl.BlockSpec((tm, tn), lambda j, i: (i, j)),) + extra_specs,
        scratch_shapes=[pltpu.VMEM((d, tn), BF16), pltpu.VMEM((d, tn), BF16), pltpu.VMEM((SUB, tn), F32)]
        + extra_scratch,
        compiler_params=pltpu.CompilerParams(
            dimension_semantics=("parallel", "arbitrary"), vmem_limit_bytes=VMEM_LIMIT_BIG),
        name="ffn_in",
    )(h, w, w, prev, cw)


def _ffn_out_kernel(a_ref, w_ref, x_ref, g_ref, o_ref):
    kk = pl.program_id(1)

    @pl.when(kk == 0)
    def _():
        o_ref[...] = x_ref[...]

    o_ref[...] += jnp.dot(a_ref[...], w_ref[...], preferred_element_type=F32)

    @pl.when(kk == pl.num_programs(1) - 1)
    def _():
        x = o_ref[...]
        ms = jnp.mean(x * x, axis=-1, keepdims=True)
        o_ref[...] = x * lax.rsqrt(ms + RMS_EPS) * g_ref[...]


def _ffn_out(act, w, x1, g, m, tm, tk):
    kdim, d = w.shape
    return pl.pallas_call(
        _ffn_out_kernel,
        out_shape=jax.ShapeDtypeStruct((m, d), F32),
        grid=(m // tm, kdim // tk),
        in_specs=[
            pl.BlockSpec((tm, tk), lambda i, k: (i, k)),
            pl.BlockSpec((tk, d), lambda i, k: (k, 0)),
            pl.BlockSpec((tm, d), lambda i, k: (i, 0)),
            pl.BlockSpec((1, d), lambda i, k: (0, 0)),
        ],
        out_specs=pl.BlockSpec((tm, d), lambda i, k: (i, 0)),
        compiler_params=pltpu.CompilerParams(
            dimension_semantics=("parallel", "arbitrary"), vmem_limit_bytes=VMEM_LIMIT),
        name="ffn_out",
    )(act, w, x1, g)


def _delta_kernel(qkvz_ref, ba_ref, prev_ref, s0_ref, cw_ref, alog_ref, dtb_ref, on_ref,
                  o_ref, s_ref, carry_ref, *, C, NB):
    R = NB * C
    ci = pl.program_id(1)

    @pl.when(ci == 0)
    def _():
        if NB == 1:
            carry_ref[...] = prev_ref[...]
        s_ref[...] = s0_ref[...]

    tri, ones_seq = _seq_masks(R, C)

    ba = ba_ref[...]
    beta_full = _sigmoid(ba)
    g_full = -jnp.exp(alog_ref[...]) * _softplus(ba + dtb_ref[...])
    gc_full = _xdot_r(tri, g_full, 3)
    gtot_full = gc_full[C - 1:C, :] if NB == 1 else _xdot_r(ones_seq, g_full, 3)

    def conv_silu(c0):
        x = qkvz_ref[:, c0:c0 + LANE]
        if NB == 1:
            hist = carry_ref[:, c0:c0 + LANE]
        else:
            hist = [_repeat_rows(prev_ref[j, :, c0:c0 + LANE], SUB) for j in range(3)]
        y = x * cw_ref[3:4, c0:c0 + LANE]
        for k in (1, 2, 3):
            y = y + _delayed(x, hist, k, NB) * cw_ref[3 - k:4 - k, c0:c0 + LANE]
        return _silu(y)

    ones_ll = jnp.ones((LANE, LANE), BF16)

    def lane_sum(x):
        return jnp.dot(x.astype(BF16), ones_ll, preferred_element_type=F32)

    def l2n_many(xs):
        sums = [lane_sum(x * x) for x in xs]
        return [x * lax.rsqrt(sq + 1e-6) for x, sq in zip(xs, sums)]

    def seq_rows(x, n):
        return x[n * C:(n + 1) * C]

    heads = range(H_A)
    seqs = range(NB)
    q = [x * (HEAD_A ** -0.5) for x in l2n_many([conv_silu(h * HEAD_A) for h in heads])]
    k = l2n_many([conv_silu(W_A + h * HEAD_A) for h in heads])
    v = [conv_silu(2 * W_A + h * HEAD_A) for h in heads]
    bcol = [beta_full[:, h:h + 1] for h in heads]
    gcol = [gc_full[:, H_A + h:H_A + h + 1] for h in heads]
    gtot = [gtot_full[:, H_A + h:H_A + h + 1] for h in heads]
    eg = [jnp.exp(gcol[h]) for h in heads]
    kb = [k[h] * bcol[h] for h in heads]

    pk = _Packed(R, C)
    hpairs = range(H_A // 2)
    zc = jnp.zeros((R, HEAD_A), F32)
    row2 = _iota((2 * R, LANE), 0)
    lane2 = _iota((2 * R, LANE), 1)
    ones_cl = jnp.ones((R, LANE), F32)
    gc2 = jnp.concatenate([gc_full, gc_full], axis=0)
    kq, gamma = [], []
    for hp in hpairs:
        h0, h1 = 2 * hp, 2 * hp + 1
        lhs = jnp.concatenate([jnp.concatenate([kb[h0], kb[h1]], axis=1),
                               jnp.concatenate([q[h0], q[h1]], axis=1)], axis=0)
        rk = jnp.concatenate([jnp.concatenate([k[h0], zc], axis=1),
                              jnp.concatenate([zc, k[h1]], axis=1)], axis=0)
        kq.append(_bdot_nt(lhs, rk))
        sel = lane2 == jnp.where(row2 < R, H_A + h0, H_A + h1)
        grow = _xdot_r(ones_cl, jnp.where(sel, gc2, 0.0), 3, NT_DIMS)
        gcol_p = jnp.where(pk.first, gcol[h0], gcol[h1])
        gamma.append(jnp.exp(jnp.where(pk.incl, gcol_p - grow, NEG_BIG)))
    tinv = pk.inverse_many([-jnp.where(pk.strict, kq[hp][0:R] * gamma[hp], 0.0) for hp in hpairs])
    uw = []
    for hp in hpairs:
        h0, h1 = 2 * hp, 2 * hp + 1
        z2 = jnp.zeros((R, 2 * HEAD_A), F32)
        rhs = jnp.concatenate([jnp.concatenate([v[h0] * bcol[h0], kb[h0] * eg[h0], z2], axis=1),
                               jnp.concatenate([z2, v[h1] * bcol[h1], kb[h1] * eg[h1]], axis=1)], axis=0)
        uw.append(_bdot(tinv[hp], rhs))
    u = [uw[h // 2][:, (h % 2) * 2 * HEAD_A:(h % 2) * 2 * HEAD_A + HEAD_A] for h in heads]
    w = [uw[h // 2][:, (h % 2) * 2 * HEAD_A + HEAD_A:(h % 2 + 1) * 2 * HEAD_A] for h in heads]
    qd = [q[h] * eg[h] for h in heads]
    s = [[s_ref[n, h] for h in heads] for n in seqs]
    wqs = [[_bdot(jnp.concatenate([seq_rows(w[h], n), seq_rows(qd[h], n)], axis=0), s[n][h]) for h in heads]
           for n in seqs]
    ws = [jnp.concatenate([wqs[n][h][0:C] for n in seqs], axis=0) if NB > 1 else wqs[0][h][0:C] for h in heads]
    qs = [jnp.concatenate([wqs[n][h][C:2 * C] for n in seqs], axis=0) if NB > 1 else wqs[0][h][C:2 * C]
          for h in heads]
    v_new = [u[h] - ws[h] for h in heads]
    o = []
    for hp in hpairs:
        h0, h1 = 2 * hp, 2 * hp + 1
        vn_bd = jnp.concatenate([jnp.concatenate([v_new[h0], zc], axis=1),
                                 jnp.concatenate([zc, v_new[h1]], axis=1)], axis=0)
        op = _bdot(kq[hp][R:2 * R] * gamma[hp], vn_bd)
        o.append(qs[h0] + op[:, :HEAD_A])
        o.append(qs[h1] + op[:, HEAD_A:])
    kd = [k[h] * jnp.exp(gtot[h] - gcol[h]) for h in heads]
    for n in seqs:
        for h in heads:
            glast = gtot[h] if NB == 1 else gtot[h][n * C:n * C + 1]
            s_ref[n, h] = s[n][h] * jnp.exp(glast) + _bdot_tn(seq_rows(kd[h], n), seq_rows(v_new[h], n))
    osq = [lane_sum(o[h] * o[h]) for h in heads]
    for h in heads:
        z = qkvz_ref[:, 3 * W_A + h * HEAD_A:3 * W_A + (h + 1) * HEAD_A]
        oh = o[h] * lax.rsqrt(osq[h] * (1.0 / HEAD_A) + RMS_EPS)
        o_ref[:, h * HEAD_A:(h + 1) * HEAD_A] = (oh * on_ref[...] * _silu(z)).astype(BF16)

    if NB == 1:
        carry_ref[...] = qkvz_ref[C - SUB:C, 0:3 * W_A]


def _delta_mixer(proj, row0, nb, seq, C, NB, prev, s0, cw, alog_row, dtb_row, onorm):
    nch = seq // C
    assert NB == 1 or (nch == 1 and C == SUB)
    R = NB * C
    blk0 = row0 // R
    bcast = s0.shape[0] == 1
    bsel = (lambda b: 0) if bcast else (lambda b: b)
    return pl.pallas_call(
        functools.partial(_delta_kernel, C=C, NB=NB),
        out_shape=(jax.ShapeDtypeStruct((nb * seq, W_A), BF16),
                   jax.ShapeDtypeStruct((nb, H_A, HEAD_A, HEAD_A), F32)),
        grid=(nb // NB, nch),
        in_specs=[
            pl.BlockSpec((R, 4 * W_A), lambda b, c: (blk0 + b * nch + c, 0)),
            pl.BlockSpec((R, LANE), lambda b, c: (blk0 + b * nch + c, SM0 // LANE)),
            (pl.BlockSpec((SUB, 3 * W_A), lambda b, c: (0, 0)) if NB == 1
             else pl.BlockSpec((3, NB, 3 * W_A), lambda b, c: (0, b, 0))),
            pl.BlockSpec((NB, H_A, HEAD_A, HEAD_A), lambda b, c: (bsel(b), 0, 0, 0)),
            pl.BlockSpec((4, 3 * W_A), lambda b, c: (0, 0)),
            pl.BlockSpec((1, LANE), lambda b, c: (0, 0)),
            pl.BlockSpec((1, LANE), lambda b, c: (0, 0)),
            pl.BlockSpec((1, HEAD_A), lambda b, c: (0, 0)),
        ],
        out_specs=(pl.BlockSpec((R, W_A), lambda b, c: (b * nch + c, 0)),
                   pl.BlockSpec((NB, H_A, HEAD_A, HEAD_A), lambda b, c: (b, 0, 0, 0))),
        scratch_shapes=[pltpu.VMEM((SUB, 3 * W_A), F32)],
        compiler_params=pltpu.CompilerParams(
            dimension_semantics=("parallel", "arbitrary"), vmem_limit_bytes=VMEM_LIMIT),
        name="delta_mixer",
    )(proj, proj, prev, s0, cw, alog_row, dtb_row, onorm)


def _rwkv_kernel(r_ref, k_ref, v_ref, sm_ref, pr_ref, pk_ref, pv_ref, psm_ref, s0_ref,
                 mur_ref, muk_ref, muv_ref, musm_ref, w2_ref, a2_ref, g2_ref,
                 w0_ref, a0_ref, kk_ref, ka_ref, rk_ref, lnw_ref, lnb_ref,
                 o_ref, s_ref, cr_ref, ck_ref, cv_ref, csm_ref, st_ref, *, C, NB, G):
    R = NB * C
    ci = pl.program_id(2)
    nch = pl.num_programs(2)
    seqs = range(NB)
    pairs = range(G)

    r2 = _iota((LANE, LANE), 0)
    c2 = _iota((LANE, LANE), 1)
    same_head = (r2 < HEAD_B) == (c2 < HEAD_B)
    ones_bd = same_head.astype(F32)
    spread = (_iota((HEAD_B, LANE), 0) == (_iota((HEAD_B, LANE), 1) & (HEAD_B - 1))).astype(F32)
    gather = ((_iota((LANE, HEAD_B), 0) & (HEAD_B - 1)) == _iota((LANE, HEAD_B), 1)).astype(F32)

    @pl.when(ci == 0)
    def _():
        if NB == 1:
            cr_ref[...] = pr_ref[...]
            ck_ref[...] = pk_ref[...]
            cv_ref[...] = pv_ref[...]
            csm_ref[...] = psm_ref[...]
        for n in seqs:
            for p in pairs:
                st_ref[n * G + p] = jnp.where(same_head, _xdot(s0_ref[n, p], spread, 3), 0.0)

    hr_ref, hk_ref, hv_ref, hsm_ref = ((cr_ref, ck_ref, cv_ref, csm_ref) if NB == 1
                                       else (pr_ref, pk_ref, pv_ref, psm_ref))
    tri, ones_seq = _seq_masks(R, C)

    def lerp(x, hist_ref, mu):
        hist = hist_ref[...] if NB == 1 else [_repeat_rows(hist_ref[0], SUB)]
        return x + (_delayed(x, hist, 1, NB) - x) * mu

    sm = sm_ref[...]
    xs = lerp(sm, hsm_ref, musm_ref[...])
    slab_wa = xs[:, SM_WA:SM_WA + LANE]
    slab_g = xs[:, SM_G:SM_G + 2 * LANE]

    def col(x, p):
        return x[:, p * LANE:(p + 1) * LANE]

    def to_rows(x):
        return jnp.concatenate([col(x, p) for p in pairs], axis=0)

    def to_cols(y):
        return jnp.concatenate([y[p * R:(p + 1) * R] for p in pairs], axis=1)

    def seq_rows(x, n):
        return x[n * C:(n + 1) * C]

    xr = lerp(r_ref[...], hr_ref, mur_ref[...])
    xk = lerp(k_ref[...], hk_ref, muk_ref[...])
    xv = lerp(v_ref[...], hv_ref, muv_ref[...])
    wlog = -_softplus(-(w0_ref[...] + _bdot(jnp.tanh(slab_wa), w2_ref[...]))) - 0.5
    ld = -jnp.exp(wlog)
    aa = _sigmoid(a0_ref[...] + _bdot(slab_wa, a2_ref[...]))
    gate = _bdot(_sigmoid(slab_g), g2_ref[...])
    kkr = xk * kk_ref[...]
    kkn = kkr * lax.rsqrt(to_cols(_bdot(to_rows(kkr * kkr), ones_bd)) + 1e-6)
    k2 = xk * (1.0 + (aa - 1.0) * ka_ref[...])
    lp = _xdot_r(tri, ld, 3)
    lp_tot = lp[C - 1:C, :] if NB == 1 else _xdot_r(ones_seq, ld, 3)
    e_neg = jnp.exp(-lp)
    e_rem = jnp.exp(lp_tot - lp)
    at = -kkn * jnp.exp(lp - ld)
    rt = xr * jnp.exp(lp)
    kb = kkn * aa
    bt = kb * e_neg
    kt = k2 * e_neg
    bhat = kb * e_rem
    khat = k2 * e_rem
    p_tot = jnp.exp(lp_tot)

    s = [[st_ref[n * G + p] for p in pairs] for n in seqs]
    atp = [col(at, p) for p in pairs]
    rtp = [col(rt, p) for p in pairs]
    ars = [[_bdot_nt(jnp.concatenate([seq_rows(atp[p], n), seq_rows(rtp[p], n)], axis=0), s[n][p])
            for p in pairs] for n in seqs]
    if NB == 1:
        x_state = [ars[0][p][0:C] for p in pairs]
        o_state = [ars[0][p][C:2 * C] for p in pairs]
    else:
        x_state = [jnp.concatenate([ars[n][p][0:C] for n in seqs], axis=0) for p in pairs]
        o_state = [jnp.concatenate([ars[n][p][C:2 * C] for n in seqs], axis=0) for p in pairs]
    pk = _Packed(R, C)
    stack_mask = (_iota((2 * R, LANE), 0) < R) == (_iota((2 * R, LANE), 1) < HEAD_B)

    def by_head(x):
        return jnp.where(stack_mask, jnp.concatenate([x, x], axis=0), 0.0)

    ar = [jnp.concatenate([atp[p], rtp[p]], axis=0) for p in pairs]
    ab = [_bdot_nt(ar[p], by_head(col(bt, p))) for p in pairs]
    ak = [_bdot_nt(ar[p], by_head(col(kt, p))) for p in pairs]
    tinv = pk.inverse_many([jnp.where(pk.strict, m[0:R], 0.0) for m in ab])
    v_bh = [by_head(col(xv, p)) for p in pairs]
    y = [x_state[p] + _bdot(jnp.where(pk.strict, ak[p][0:R], 0.0), v_bh[p]) for p in pairs]
    u = [_bdot(tinv[p], by_head(y[p])) for p in pairs]
    o = [o_state[p] + _bdot(jnp.where(pk.incl, ab[p][R:2 * R], 0.0), by_head(u[p]))
         + _bdot(jnp.where(pk.incl, ak[p][R:2 * R], 0.0), v_bh[p]) for p in pairs]
    for n in seqs:
        for p in pairs:
            uv = jnp.concatenate([seq_rows(u[p], n), seq_rows(col(xv, p), n)], axis=0)
            bkhat = jnp.concatenate([seq_rows(col(bhat, p), n), seq_rows(col(khat, p), n)], axis=0)
            decay = col(p_tot, p) if NB == 1 else col(p_tot, p)[n * C:n * C + 1]
            s_new = s[n][p] * decay + _bdot_tn(uv, bkhat)
            st_ref[n * G + p] = jnp.where(same_head, s_new, 0.0)

    o_rows = jnp.concatenate(o, axis=0)
    mean = _bdot(o_rows, ones_bd) * (1.0 / HEAD_B)
    d = o_rows - mean
    var = _bdot(d * d, ones_bd) * (1.0 / HEAD_B)
    on = to_cols(d * lax.rsqrt(var + GN_EPS)) * lnw_ref[...] + lnb_ref[...]
    bonus = to_cols(_bdot(to_rows(xr * k2 * rk_ref[...]), ones_bd)) * xv
    o_ref[...] = ((on + bonus) * gate).astype(BF16)

    if NB == 1:
        cr_ref[...] = r_ref[C - SUB:C, :]
        ck_ref[...] = k_ref[C - SUB:C, :]
        cv_ref[...] = v_ref[C - SUB:C, :]
        csm_ref[...] = sm_ref[C - SUB:C, :]

    @pl.when(ci == nch - 1)
    def _():
        for n in seqs:
            for p in pairs:
                s_ref[n, p] = _xdot(st_ref[n * G + p], gather, 3)


def _rwkv_mixer(proj, row0, nb, seq, C, NB, G, prev_rkv, prev_sm, s0, mu_rkv, mu_sm, w2p, a2p, g2p,
                w0, a0, k_k, k_a, r_k, lnw, lnb):
    nch = seq // C
    assert NB == 1 or (nch == 1 and C == SUB)
    R = NB * C
    ng = N_PAIR // G
    gw = G * LANE
    blk0 = row0 // R
    bcast = s0.shape[0] == 1
    bsel = (lambda b: 0) if bcast else (lambda b: b)

    def proj_spec(col0):
        return pl.BlockSpec((R, gw), lambda b, g, c: (blk0 + b * nch + c, col0 // gw + g))

    def prev_spec(part):
        if NB == 1:
            return pl.BlockSpec((SUB, gw), lambda b, g, c: (0, part * (W_B // gw) + g))
        return pl.BlockSpec((1, NB, gw), lambda b, g, c: (0, b, part * (W_B // gw) + g))

    def vec_spec(part=0):
        return pl.BlockSpec((1, gw), lambda b, g, c: (0, part * (W_B // gw) + g))

    in_specs = [
        proj_spec(RKV0), proj_spec(RKV0 + W_B), proj_spec(RKV0 + 2 * W_B),
        pl.BlockSpec((R, SM_W), lambda b, g, c: (blk0 + b * nch + c, SM0 // SM_W)),
        prev_spec(0), prev_spec(1), prev_spec(2),
        (pl.BlockSpec((SUB, SM_W), lambda b, g, c: (0, 0)) if NB == 1
         else pl.BlockSpec((1, NB, SM_W), lambda b, g, c: (0, b, 0))),
        pl.BlockSpec((NB, G, LANE, HEAD_B), lambda b, g, c: (bsel(b), g, 0, 0)),
        vec_spec(0), vec_spec(1), vec_spec(2),
        pl.BlockSpec((1, SM_W), lambda b, g, c: (0, 0)),
        pl.BlockSpec((LANE, gw), lambda b, g, c: (0, g)),
        pl.BlockSpec((LANE, gw), lambda b, g, c: (0, g)),
        pl.BlockSpec((2 * LANE, gw), lambda b, g, c: (0, g)),
        vec_spec(), vec_spec(), vec_spec(), vec_spec(), vec_spec(), vec_spec(), vec_spec(),
    ]
    return pl.pallas_call(
        functools.partial(_rwkv_kernel, C=C, NB=NB, G=G),
        out_shape=(jax.ShapeDtypeStruct((nb * seq, W_B), BF16),
                   jax.ShapeDtypeStruct((nb, N_PAIR, LANE, HEAD_B), F32)),
        grid=(nb // NB, ng, nch),
        in_specs=in_specs,
        out_specs=(pl.BlockSpec((R, gw), lambda b, g, c: (b * nch + c, g)),
                   pl.BlockSpec((NB, G, LANE, HEAD_B), lambda b, g, c: (b, g, 0, 0))),
        scratch_shapes=[pltpu.VMEM((SUB, gw), F32), pltpu.VMEM((SUB, gw), F32), pltpu.VMEM((SUB, gw), F32),
                        pltpu.VMEM((SUB, SM_W), F32), pltpu.VMEM((NB * G, LANE, LANE), F32)],
        compiler_params=pltpu.CompilerParams(
            dimension_semantics=("parallel", "parallel", "arbitrary"), vmem_limit_bytes=VMEM_LIMIT),
        name="rwkv_mixer",
    )(proj, proj, proj, proj, prev_rkv, prev_rkv, prev_rkv, prev_sm, s0,
      mu_rkv, mu_rkv, mu_rkv, mu_sm, w2p, a2p, g2p, w0, a0, k_k, k_a, r_k, lnw, lnb)


_AT, _RT, _BT, _KT, _BHAT, _KHAT, _XV, _GATE, _BONUS, _N_PRE = range(10)


def _rwkv_seq_kernel(r0_ref, k0_ref, v0_ref, sm0_ref, rn_ref, kn_ref, vn_ref, smn_ref,
                     pr_ref, pk_ref, pv_ref, psm_ref, s0_ref,
                     mur_ref, muk_ref, muv_ref, musm_ref, w2_ref, a2_ref, g2_ref,
                     w0_ref, a0_ref, kk_ref, ka_ref, rk_ref, lnw_ref, lnb_ref,
                     o_ref, s_ref, cr_ref, ck_ref, cv_ref, csm_ref, st_ref, pre_ref, ptot_ref, *, C, G):
    ci = pl.program_id(2)
    nch = pl.num_programs(2)
    pairs = range(G)

    r2 = _iota((LANE, LANE), 0)
    c2 = _iota((LANE, LANE), 1)
    same_head = (r2 < HEAD_B) == (c2 < HEAD_B)
    ones_bd = same_head.astype(F32)
    spread = (_iota((HEAD_B, LANE), 0) == (_iota((HEAD_B, LANE), 1) & (HEAD_B - 1))).astype(F32)
    gather = ((_iota((LANE, HEAD_B), 0) & (HEAD_B - 1)) == _iota((LANE, HEAD_B), 1)).astype(F32)
    tri, _ = _seq_masks(C, C)

    def col(x, p):
        return x[:, p * LANE:(p + 1) * LANE]

    def to_rows(x):
        return jnp.concatenate([col(x, p) for p in pairs], axis=0)

    def to_cols(y):
        return jnp.concatenate([y[p * C:(p + 1) * C] for p in pairs], axis=1)

    def lerp(x, hist, mu):
        return x + (_shifted(x, hist, 1) - x) * mu

    def preamble(r_ref, k_ref, v_ref, sm_ref, hr_ref, hk_ref, hv_ref, hsm_ref, slot):
        xs = lerp(sm_ref[...], hsm_ref[...], musm_ref[...])
        yield
        slab_wa = xs[:, SM_WA:SM_WA + LANE]
        wl = _bdot(jnp.tanh(slab_wa), w2_ref[...])
        yield
        al = _bdot(slab_wa, a2_ref[...])
        yield
        pre_ref[slot, _GATE] = _bdot(_sigmoid(xs[:, SM_G:SM_G + 2 * LANE]), g2_ref[...])
        yield
        xr = lerp(r_ref[...], hr_ref[...], mur_ref[...])
        yield
        xk = lerp(k_ref[...], hk_ref[...], muk_ref[...])
        yield
        xv = lerp(v_ref[...], hv_ref[...], muv_ref[...])
        pre_ref[slot, _XV] = xv
        yield
        ld = -jnp.exp(-_softplus(-(w0_ref[...] + wl)) - 0.5)
        yield
        lp = _xdot_r(tri, ld, 3)
        yield
        aa = _sigmoid(a0_ref[...] + al)
        yield
        kkr = xk * kk_ref[...]
        ss = to_cols(_bdot(to_rows(kkr * kkr), ones_bd))
        yield
        kkn = kkr * lax.rsqrt(ss + 1e-6)
        k2 = xk * (1.0 + (aa - 1.0) * ka_ref[...])
        yield
        pre_ref[slot, _BONUS] = to_cols(_bdot(to_rows(xr * k2 * rk_ref[...]), ones_bd)) * xv
        yield
        pre_ref[slot, _AT] = -kkn * jnp.exp(lp - ld)
        yield
        pre_ref[slot, _RT] = xr * jnp.exp(lp)
        yield
        kb = kkn * aa
        e_neg = jnp.exp(-lp)
        pre_ref[slot, _BT] = kb * e_neg
        yield
        pre_ref[slot, _KT] = k2 * e_neg
        yield
        lp_tot = lp[C - 1:C, :]
        e_rem = jnp.exp(lp_tot - lp)
        pre_ref[slot, _BHAT] = kb * e_rem
        yield
        pre_ref[slot, _KHAT] = k2 * e_rem
        ptot_ref[slot] = jnp.broadcast_to(jnp.exp(lp_tot), (SUB, lp.shape[1]))
        yield
        cr_ref[...] = r_ref[C - SUB:C, :]
        ck_ref[...] = k_ref[C - SUB:C, :]
        cv_ref[...] = v_ref[C - SUB:C, :]
        csm_ref[...] = sm_ref[C - SUB:C, :]

    @pl.when(ci == 0)
    def _():
        for _ in preamble(r0_ref, k0_ref, v0_ref, sm0_ref, pr_ref, pk_ref, pv_ref, psm_ref, 0):
            pass
        for p in pairs:
            st_ref[p] = jnp.where(same_head, _xdot(s0_ref[0, p], spread, 3), 0.0)

    slot = ci % 2
    ahead = preamble(rn_ref, kn_ref, vn_ref, smn_ref, cr_ref, ck_ref, cv_ref, csm_ref, 1 - slot)
    calls = [0]

    def tick():
        calls[0] += 1
        if calls[0] % 6 == 0:
            next(ahead, None)

    def each(fn, items):
        out = []
        for it in items:
            out.append(fn(it))
            tick()
        return out

    def pre(idx, p):
        return pre_ref[slot, idx, :, p * LANE:(p + 1) * LANE]

    pk = _Packed(C, C)
    stack_mask = (_iota((2 * C, LANE), 0) < C) == (_iota((2 * C, LANE), 1) < HEAD_B)

    def by_head(x):
        return jnp.where(stack_mask, jnp.concatenate([x, x], axis=0), 0.0)

    s = [st_ref[p] for p in pairs]
    ar = [jnp.concatenate([pre(_AT, p), pre(_RT, p)], axis=0) for p in pairs]
    ars = each(lambda p: _bdot_nt(ar[p], s[p]), pairs)
    ab = each(lambda p: _bdot_nt(ar[p], by_head(pre(_BT, p))), pairs)
    ak = each(lambda p: _bdot_nt(ar[p], by_head(pre(_KT, p))), pairs)
    ts = [pk.eye + jnp.where(pk.strict, m[0:C], 0.0) for m in ab]
    ps = [jnp.where(pk.strict, m[0:C], 0.0) for m in ab]
    n = 1
    while 2 * n < C:
        ps = each(lambda p: _bdot(ps[p], pk.block_diag(ps[p])), pairs)
        ts = each(lambda p: ts[p] + _bdot(ps[p], pk.block_diag(ts[p])), pairs)
        n *= 2
    xv = [pre(_XV, p) for p in pairs]
    v_bh = [by_head(xv[p]) for p in pairs]
    y = each(lambda p: ars[p][0:C] + _bdot(jnp.where(pk.strict, ak[p][0:C], 0.0), v_bh[p]), pairs)
    u = each(lambda p: _bdot(ts[p], by_head(y[p])), pairs)
    o = each(lambda p: ars[p][C:2 * C] + _bdot(jnp.where(pk.incl, ab[p][C:2 * C], 0.0), by_head(u[p]))
             + _bdot(jnp.where(pk.incl, ak[p][C:2 * C], 0.0), v_bh[p]), pairs)
    for p in pairs:
        uv = jnp.concatenate([u[p], xv[p]], axis=0)
        bkhat = jnp.concatenate([pre(_BHAT, p), pre(_KHAT, p)], axis=0)
        s_new = s[p] * ptot_ref[slot, 0:1, p * LANE:(p + 1) * LANE] + _bdot_tn(uv, bkhat)
        st_ref[p] = jnp.where(same_head, s_new, 0.0)
        tick()

    o_rows = jnp.concatenate(o, axis=0)
    mean = _bdot(o_rows, ones_bd) * (1.0 / HEAD_B)
    d = o_rows - mean
    var = _bdot(d * d, ones_bd) * (1.0 / HEAD_B)
    on = to_cols(d * lax.rsqrt(var + GN_EPS)) * lnw_ref[...] + lnb_ref[...]
    o_ref[...] = ((on + pre_ref[slot, _BONUS]) * pre_ref[slot, _GATE]).astype(BF16)
    for _ in ahead:
        pass

    @pl.when(ci == nch - 1)
    def _():
        for p in pairs:
            s_ref[0, p] = _xdot(st_ref[p], gather, 3)


def _rwkv_seq_mixer(proj, nb, seq, C, prev_rkv, prev_sm, s0, mu_rkv, mu_sm, w2p, a2p, g2p,
                    w0, a0, k_k, k_a, r_k, lnw, lnb):
    G = N_PAIR
    nch = seq // C
    gw = G * LANE

    def first_spec(col0, w):
        return pl.BlockSpec((C, w), lambda b, g, c: (b * nch, col0 // w))

    def next_spec(col0, w):
        return pl.BlockSpec((C, w), lambda b, g, c: (b * nch + jnp.minimum(c + 1, nch - 1), col0 // w))

    def prev_spec(part):
        return pl.BlockSpec((SUB, gw), lambda b, g, c: (0, part))

    def vec_spec(part=0):
        return pl.BlockSpec((1, gw), lambda b, g, c: (0, part))

    in_specs = [
        first_spec(RKV0, gw), first_spec(RKV0 + W_B, gw), first_spec(RKV0 + 2 * W_B, gw), first_spec(SM0, SM_W),
        next_spec(RKV0, gw), next_spec(RKV0 + W_B, gw), next_spec(RKV0 + 2 * W_B, gw), next_spec(SM0, SM_W),
        prev_spec(0), prev_spec(1), prev_spec(2),
        pl.BlockSpec((SUB, SM_W), lambda b, g, c: (0, 0)),
        pl.BlockSpec((1, G, LANE, HEAD_B), lambda b, g, c: (0, 0, 0, 0)),
        vec_spec(0), vec_spec(1), vec_spec(2),
        pl.BlockSpec((1, SM_W), lambda b, g, c: (0, 0)),
        pl.BlockSpec((LANE, gw), lambda b, g, c: (0, 0)),
        pl.BlockSpec((LANE, gw), lambda b, g, c: (0, 0)),
        pl.BlockSpec((2 * LANE, gw), lambda b, g, c: (0, 0)),
        vec_spec(), vec_spec(), vec_spec(), vec_spec(), vec_spec(), vec_spec(), vec_spec(),
    ]
    return pl.pallas_call(
        functools.partial(_rwkv_seq_kernel, C=C, G=G),
        out_shape=(jax.ShapeDtypeStruct((nb * seq, W_B), BF16),
                   jax.ShapeDtypeStruct((nb, N_PAIR, LANE, HEAD_B), F32)),
        grid=(nb, 1, nch),
        in_specs=in_specs,
        out_specs=(pl.BlockSpec((C, gw), lambda b, g, c: (b * nch + c, 0)),
                   pl.BlockSpec((1, G, LANE, HEAD_B), lambda b, g, c: (b, 0, 0, 0))),
        scratch_shapes=[pltpu.VMEM((SUB, gw), F32), pltpu.VMEM((SUB, gw), F32), pltpu.VMEM((SUB, gw), F32),
                        pltpu.VMEM((SUB, SM_W), F32), pltpu.VMEM((G, LANE, LANE), F32),
                        pltpu.VMEM((2, _N_PRE, C, gw), F32), pltpu.VMEM((2, SUB, gw), F32)],
        compiler_params=pltpu.CompilerParams(
            dimension_semantics=("parallel", "arbitrary", "arbitrary"), vmem_limit_bytes=VMEM_LIMIT),
        name="rwkv_seq_mixer",
    )(proj, proj, proj, proj, proj, proj, proj, proj, prev_rkv, prev_rkv, prev_rkv, prev_sm, s0,
      mu_rkv, mu_rkv, mu_rkv, mu_sm, w2p, a2p, g2p, w0, a0, k_k, k_a, r_k, lnw, lnb)


def _small_layout(cols_ba, cols_w, cols_a, cols_g, axis=-1):
    def z(n):
        shape = list(cols_w.shape)
        shape[axis] = n
        return jnp.zeros(shape, cols_w.dtype)
    return jnp.concatenate(
        [cols_ba, z(SM_WA - cols_ba.shape[axis]), cols_w, cols_a, cols_g,
         z(SM_W - SM_G - cols_g.shape[axis])], axis=axis)


def kernel(x_prompt, x_sample, state_delta, state_conv_qkv, state_wkv, state_shift, state_ffn_conv, meta, norm1, w_in, conv_a, a_log, dt_bias, onorm_a, mu_b, w0, w2, a0, a2, g2, k_k, k_a, r_k, lnx_w, lnx_b, w_o, norm2, w_ffn_in, conv_f, w_ffn_out, norm_f):
    nbp, seq_p, _ = x_prompt.shape
    nbs, seq_s, _ = x_sample.shape
    n_s = nbs * seq_s
    assert w_in.shape[0] == 1, "single-layer trunk"
    assert seq_s == SUB
    l = 0

    wt = w_in[l].T
    o_b = A_PROJ
    o_l = A_PROJ + 3 * W_B
    w_cat_t = jnp.concatenate([
        wt[:4 * W_A], wt[o_b:o_l],
        _small_layout(wt[4 * W_A:A_PROJ], wt[o_l:o_l + W_LORA], wt[o_l + W_LORA:o_l + W_LORA + A_LORA],
                      wt[o_l + W_LORA + A_LORA:], axis=0)], axis=0).astype(BF16)
    mu = mu_b[l]
    mu_rkv = mu[None, :3 * W_B]
    mu_sm = _small_layout(jnp.zeros((1, 2 * H_A), F32), mu[None, 3 * W_B:3 * W_B + W_LORA],
                          mu[None, 3 * W_B + W_LORA:3 * W_B + W_LORA + A_LORA],
                          mu[None, 3 * W_B + W_LORA + A_LORA:])
    w2p = jnp.concatenate([w2[l], jnp.zeros((LANE - W_LORA, W_B), F32)], axis=0)
    a2p = jnp.concatenate([jnp.zeros((W_LORA, W_B), F32), a2[l]], axis=0)
    g2p = jnp.concatenate([g2[l], jnp.zeros((2 * LANE - G_LORA, W_B), F32)], axis=0)
    alog_row = jnp.concatenate([jnp.zeros((H_A,), F32), a_log[l], jnp.zeros((LANE - 2 * H_A,), F32)])[None]
    dtb_row = jnp.concatenate([jnp.zeros((H_A,), F32), dt_bias[l], jnp.zeros((LANE - 2 * H_A,), F32)])[None]
    wo_bf = w_o[l].astype(BF16)
    wfo_bf = w_ffn_out[l].astype(BF16)
    row = lambda v: v.reshape(1, -1)

    def mix(proj, row0, nb, seq, C, NB, prev_qkv, prev_rkv, prev_sm, s_delta, s_wkv):
        oa, sd = _delta_mixer(proj, row0, nb, seq, C, NB, prev_qkv, s_delta, conv_a[l], alog_row, dtb_row,
                              row(onorm_a[l]))
        rwkv_params = (mu_rkv, mu_sm, w2p, a2p, g2p, row(w0[l]), row(a0[l]), row(k_k[l]), row(k_a[l]),
                       row(r_k[l]), row(lnx_w[l]), row(lnx_b[l]))
        if seq // C > 1:
            assert row0 == 0 and NB == 1
            ob, sw = _rwkv_seq_mixer(proj, nb, seq, C, prev_rkv, prev_sm, s_wkv, *rwkv_params)
        else:
            ob, sw = _rwkv_mixer(proj, row0, nb, seq, C, NB, N_PAIR, prev_rkv, prev_sm, s_wkv, *rwkv_params)
        return oa, ob, sd, sw

    xs_rows = jnp.concatenate([x_sample.reshape(n_s, D_MODEL), meta], axis=0)
    xp_rows = x_prompt.reshape(nbp * seq_p, D_MODEL)
    n_small = n_s + N_META
    proj_s, proj_tail_s = _norm_matmul(xs_rows, row(norm1[l]), w_cat_t, n_small, 1536, 3, nbs)
    proj_p = _norm_matmul(xp_rows, row(norm1[l]), w_cat_t, 1024, 1536)

    zeros = lambda *s: jnp.zeros(s, F32)
    oa_m, ob_m, sd_m, sw_m = mix(proj_s, n_s, 1, N_META, N_META, 1, zeros(SUB, 3 * W_A),
                                 zeros(SUB, 3 * W_B), zeros(SUB, SM_W),
                                 zeros(1, H_A, HEAD_A, HEAD_A), zeros(1, N_PAIR, LANE, HEAD_B))
    tail = proj_s[n_small - SUB:n_small]
    oa_p, ob_p, sd_p, sw_p = mix(proj_p, 0, nbp, seq_p, 64, 1, tail[:, :3 * W_A],
                                 tail[:, RKV0:SM0], tail[:, SM0:], sd_m, sw_m)
    sh = state_shift[l]
    sh_sm = _small_layout(jnp.zeros((nbs, 1, 2 * H_A), F32), sh[..., 3 * W_B:3 * W_B + W_LORA],
                          sh[..., 3 * W_B + W_LORA:3 * W_B + W_LORA + A_LORA],
                          sh[..., 3 * W_B + W_LORA + A_LORA:])
    tmajor = lambda s: s.transpose(1, 0, 2)
    oa_s, ob_s, sd_s, sw_s = mix(proj_s, 0, nbs, seq_s, seq_s, 8, tmajor(state_conv_qkv[l]),
                                 tmajor(sh[..., :3 * W_B]), tmajor(sh_sm), state_delta[l],
                                 state_wkv[l].reshape(nbs, N_PAIR, LANE, HEAD_B))

    oa_small = jnp.concatenate([oa_s, oa_m], axis=0)
    ob_small = jnp.concatenate([ob_s, ob_m], axis=0)
    x1_s, h2_s = _out_proj(xs_rows, oa_small, ob_small, wo_bf, row(norm2[l]), n_small // 5)
    x1_p, h2_p = _out_proj(xp_rows, oa_p, ob_p, wo_bf, row(norm2[l]), 512)
    act_s, gate_tail_s, gate_last_s = _ffn_in(h2_s, w_ffn_in[l], state_ffn_conv[l].transpose(1, 0, 2), conv_f[l],
                                              n_small, 768, 3, 0, n_s + SUB)
    act_p, tail_p = _ffn_in(h2_p, w_ffn_in[l], gate_last_s[None], conv_f[l], 1024, 768, 3, seq_p // 1024)
    y_s = _ffn_out(act_s, wfo_bf, x1_s, row(norm_f), n_s, 1024, 768)
    y_p = _ffn_out(act_p, wfo_bf, x1_p, row(norm_f), nbp * seq_p, 1024, 768)

    def states(conv_new, last, ffn_new, nb, sd, sw):
        shift_new = jnp.concatenate([last[..., RKV0:SM0], last[..., SM0 + SM_WA:SM0 + SM_WA + W_LORA + A_LORA],
                                     last[..., SM0 + SM_G:SM0 + SM_G + G_LORA]], axis=-1)
        return (sd[None], conv_new[None], sw.reshape(nb, H_B, HEAD_B, HEAD_B)[None], shift_new[None],
                ffn_new[None])

    p3 = proj_p.reshape(nbp, seq_p, P_CAT)
    tail_s = proj_tail_s.transpose(1, 0, 2)
    return ((y_p.reshape(nbp, seq_p, D_MODEL), y_s.reshape(nbs, seq_s, D_MODEL))
            + states(p3[:, seq_p - 3:, :3 * W_A], p3[:, seq_p - 1:, :], tail_p[:, SUB - 2:, :], nbp, sd_p, sw_p)
            + states(tail_s[:, :, :3 * W_A], tail_s[:, 2:, :], gate_tail_s.transpose(1, 0, 2), nbs, sd_s, sw_s))
```

```python
import functools

import jax
import jax.numpy as jnp
from jax import lax
from jax.experimental import pallas as pl
from jax.experimental.pallas import tpu as pltpu

F32 = jnp.float32
BF16 = jnp.bfloat16

D_MODEL = 2048
N_META = 16
W_A = 1024
HEAD_A = 128
H_A = 8
W_B = 1024
HEAD_B = 64
H_B = 16
N_PAIR = H_B // 2
W_LORA = 64
A_LORA = 64
G_LORA = 160
D_FF = 5376
RMS_EPS = 1e-6
GN_EPS = 64e-5
A_PROJ = 4 * W_A + 2 * H_A
B_PROJ = 3 * W_B + W_LORA + A_LORA + G_LORA

QKVZ0 = 0
RKV0 = 4 * W_A
SM0 = RKV0 + 3 * W_B
SM_W = 512
SM_BA = 0
SM_WA = 128
SM_G = 256
P_CAT = SM0 + SM_W

LANE = 128
SUB = 8
VMEM_LIMIT = 48 * 1024 * 1024
VMEM_LIMIT_BIG = 58 * 1024 * 1024
NEG_BIG = -1e30

NT_DIMS = (((1,), (1,)), ((), ()))


def _bdot(a, b):
    return jnp.dot(a.astype(BF16), b.astype(BF16), preferred_element_type=F32)


def _bdot_nt(a, b):
    return lax.dot_general(a.astype(BF16), b.astype(BF16), NT_DIMS, preferred_element_type=F32)


def _bdot_tn(a, b):
    return lax.dot_general(a.astype(BF16), b.astype(BF16), (((0,), (0,)), ((), ())),
                           preferred_element_type=F32)


def _pieces(a, n):
    out = []
    rem = a
    for i in range(n):
        p = rem.astype(BF16)
        out.append(p)
        if i + 1 < n:
            rem = rem - p.astype(F32)
    return out


def _xdot(a, b, n, dims=(((1,), (0,)), ((), ()))):
    bb = b.astype(BF16)
    acc = None
    for p in _pieces(a, n):
        t = lax.dot_general(p, bb, dims, preferred_element_type=F32)
        acc = t if acc is None else acc + t
    return acc


def _xdot_r(a, b, n, dims=(((1,), (0,)), ((), ()))):
    ab = a.astype(BF16)
    acc = None
    for p in _pieces(b, n):
        t = lax.dot_general(ab, p, dims, preferred_element_type=F32)
        acc = t if acc is None else acc + t
    return acc


def _sigmoid(x):
    return 1.0 / (1.0 + jnp.exp(-x))


def _silu(x):
    return x * _sigmoid(x)


def _softplus(x):
    return jnp.maximum(x, 0.0) + jnp.log(1.0 + jnp.exp(-jnp.abs(x)))


def _iota(shape, dim):
    return lax.broadcasted_iota(jnp.int32, shape, dim)


def _shifted(x, prev8, k):
    n = x.shape[0]
    xr = pltpu.roll(x, k, 0)
    pr = pltpu.roll(prev8, k, 0)
    first = jnp.where(_iota((SUB, x.shape[1]), 0) < k, pr, xr[0:SUB])
    if n == SUB:
        return first
    return jnp.concatenate([first, xr[SUB:]], axis=0)


def _repeat_rows(s, reps):
    n, c = s.shape
    return jnp.broadcast_to(s[:, None, :], (n, reps, c)).reshape(n * reps, c)


def _delayed(x, hist, k, nseq):
    if nseq == 1:
        return _shifted(x, hist, k)
    t = _iota(x.shape, 0) & (SUB - 1)
    d = pltpu.roll(x, k, 0)
    for tt in range(k):
        d = jnp.where(t == tt, hist[len(hist) - k + tt], d)
    return d


def _same_seq(shape, c, rows_total):
    if c == rows_total:
        return None
    sh = c.bit_length() - 1
    return (lax.shift_right_logical(_iota(shape, 0), sh)
            == lax.shift_right_logical(_iota(shape, 1) & (rows_total - 1), sh))


class _Packed:
    def __init__(self, r, c):
        self.r = r
        self.c = c
        row = _iota((r, 2 * r), 0)
        col = _iota((r, 2 * r), 1) & (r - 1)
        same = _same_seq((r, 2 * r), c, r)
        self.incl = (row >= col) if same is None else ((row >= col) & same)
        self.strict = (row > col) if same is None else ((row > col) & same)
        self.eye = (row == col).astype(F32)
        self.first = _iota((r, 2 * r), 1) < r
        self.bd_mask = (_iota((2 * r, 2 * r), 0) < r) == (_iota((2 * r, 2 * r), 1) < r)

    def block_diag(self, m):
        return jnp.where(self.bd_mask, jnp.concatenate([m, m], axis=0), 0.0)

    def inverse_many(self, xs):
        ts = [self.eye + x for x in xs]
        ps = list(xs)
        n = 1
        while 2 * n < self.c:
            ps = [_bdot(p, self.block_diag(p)) for p in ps]
            ts = [t + _bdot(p, self.block_diag(t)) for p, t in zip(ps, ts)]
            n *= 2
        return ts


def _seq_masks(r, c):
    row = _iota((r, r), 0)
    col = _iota((r, r), 1)
    same = _same_seq((r, r), c, r)
    if same is None:
        return (row >= col).astype(F32), None
    return ((row >= col) & same).astype(F32), same.astype(F32)


def _seq_tail_rows(x, slab_ref, dst_ref, n_tail, nseq, col0=0):
    for s in range(x.shape[1] // LANE):
        slab_ref[s] = x[:, s * LANE:(s + 1) * LANE]
        for t in range(n_tail):
            dst_ref[t, :, col0 + s * LANE:col0 + (s + 1) * LANE] = (
                slab_ref[s, pl.ds(SUB - n_tail + t, nseq, stride=SUB), :])


def _norm_matmul_kernel(x_ref, g_ref, w_ref, o_ref, *rest, n_tail, nseq):
    tail_ref, h_ref, slab_ref = rest if n_tail else (None,) + rest + (None,)

    @pl.when(pl.program_id(1) == 0)
    def _():
        x = x_ref[...]
        ms = jnp.mean(x * x, axis=-1, keepdims=True)
        h_ref[...] = (x * lax.rsqrt(ms + RMS_EPS) * g_ref[...]).astype(BF16)

    out = lax.dot_general(h_ref[...], w_ref[...], NT_DIMS, preferred_element_type=F32)
    o_ref[...] = out
    if n_tail:
        _seq_tail_rows(out, slab_ref, tail_ref, n_tail, nseq)


def _norm_matmul(x, g, wt, tm, tn, n_tail=0, nseq=0):
    m, k = x.shape
    n = wt.shape[0]
    out_shape = jax.ShapeDtypeStruct((m, n), F32)
    out_specs = pl.BlockSpec((tm, tn), lambda i, j: (i, j))
    if n_tail:
        assert m == tm
        out_shape = (out_shape, jax.ShapeDtypeStruct((n_tail, nseq, n), F32))
        out_specs = (out_specs, pl.BlockSpec((n_tail, nseq, tn), lambda i, j: (0, 0, j)))
    return pl.pallas_call(
        functools.partial(_norm_matmul_kernel, n_tail=n_tail, nseq=nseq),
        out_shape=out_shape,
        grid=(m // tm, n // tn),
        in_specs=[
            pl.BlockSpec((tm, k), lambda i, j: (i, 0)),
            pl.BlockSpec((1, k), lambda i, j: (0, 0)),
            pl.BlockSpec((tn, k), lambda i, j: (j, 0)),
        ],
        out_specs=out_specs,
        scratch_shapes=[pltpu.VMEM((tm, k), BF16)] + ([pltpu.VMEM((tn // LANE, tm, LANE), F32)] if n_tail else []),
        compiler_params=pltpu.CompilerParams(
            dimension_semantics=("parallel", "arbitrary"), vmem_limit_bytes=VMEM_LIMIT_BIG),
        name="norm_matmul",
    )(x, g, wt)


def _out_proj_kernel(x_ref, oa_ref, ob_ref, wt_ref, wb_ref, g_ref, x1_ref, h_ref):
    acc = jnp.dot(oa_ref[...], wt_ref[...], preferred_element_type=F32)
    acc = acc + jnp.dot(ob_ref[...], wb_ref[...], preferred_element_type=F32)
    x1 = x_ref[...] + acc
    x1_ref[...] = x1
    ms = jnp.mean(x1 * x1, axis=-1, keepdims=True)
    h_ref[...] = (x1 * lax.rsqrt(ms + RMS_EPS) * g_ref[...]).astype(BF16)


def _out_proj(x, oa, ob, wo, g, tm):
    m, d = x.shape
    return pl.pallas_call(
        _out_proj_kernel,
        out_shape=(jax.ShapeDtypeStruct((m, d), F32), jax.ShapeDtypeStruct((m, d), BF16)),
        grid=(m // tm,),
        in_specs=[
            pl.BlockSpec((tm, d), lambda i: (i, 0)),
            pl.BlockSpec((tm, W_A), lambda i: (i, 0)),
            pl.BlockSpec((tm, W_B), lambda i: (i, 0)),
            pl.BlockSpec((W_A, d), lambda i: (0, 0)),
            pl.BlockSpec((W_B, d), lambda i: (W_A // W_B, 0)),
            pl.BlockSpec((1, d), lambda i: (0, 0)),
        ],
        out_specs=(pl.BlockSpec((tm, d), lambda i: (i, 0)), pl.BlockSpec((tm, d), lambda i: (i, 0))),
        compiler_params=pltpu.CompilerParams(
            dimension_semantics=("parallel",), vmem_limit_bytes=VMEM_LIMIT),
        name="out_proj",
    )(x, oa, ob, wo, wo, g)


def _ffn_in_kernel(h_ref, wg_ref, wu_ref, prev_ref, cw_ref, act_ref, tail_ref, *rest,
                   tiles_per_seq, state_rows, nsub):
    if tiles_per_seq:
        wgb_ref, wub_ref, carry_ref = rest
    else:
        last_ref, wgb_ref, wub_ref, carry_ref, slab_ref = rest
    m = pl.program_id(1)

    @pl.when(m == 0)
    def _():
        wgb_ref[...] = wg_ref[...].astype(BF16)
        wub_ref[...] = wu_ref[...].astype(BF16)

    if tiles_per_seq:
        @pl.when(m % tiles_per_seq == 0)
        def _():
            carry_ref[...] = prev_ref[...]

    h = h_ref[...]
    n = h.shape[0]
    w = wgb_ref.shape[1] // nsub
    for i in range(nsub):
        cs = slice(i * w, (i + 1) * w)
        gate = jnp.dot(h, wgb_ref[:, cs], preferred_element_type=F32)
        up = jnp.dot(h, wub_ref[:, cs], preferred_element_type=F32)
        if tiles_per_seq:
            prev8 = carry_ref[:, cs]
            d1 = _shifted(gate, prev8, 1)
            d2 = _shifted(gate, prev8, 2)
            carry_ref[:, cs] = gate[n - SUB:n]
            tail_ref[:, cs] = gate[n - SUB:n]
        else:
            nseq = (state_rows - SUB) // SUB
            _seq_tail_rows(gate, slab_ref, tail_ref, 2, nseq, i * w)
            last_ref[:, cs] = gate[n - SUB:n]
            pad = jnp.zeros((n - nseq * SUB, w), F32)
            hist = [jnp.concatenate([_repeat_rows(prev_ref[t, :, cs], SUB), pad], axis=0) for t in range(2)]
            t_in_seq = jnp.where(_iota(gate.shape, 0) < state_rows, _iota(gate.shape, 0) & (SUB - 1), SUB)
            d1 = jnp.where(t_in_seq == 0, hist[1], pltpu.roll(gate, 1, 0))
            d2 = jnp.where(t_in_seq == 0, hist[0], jnp.where(t_in_seq == 1, hist[1], pltpu.roll(gate, 2, 0)))
        y = gate * cw_ref[2:3, cs] + d1 * cw_ref[1:2, cs] + d2 * cw_ref[0:1, cs]
        act_ref[:, cs] = (_silu(y) * up).astype(BF16)


def _ffn_in(h, w, prev, cw, tm, tn, nsub, tiles_per_seq, state_rows=0):
    m, d = h.shape
    nj = D_FF // tn
    nm = m // tm
    if tiles_per_seq:
        nseq = nm // tiles_per_seq
        prev_spec = pl.BlockSpec((None, SUB, tn), lambda j, i: (0, 0, j))
        extra_shapes = (jax.ShapeDtypeStruct((nseq, SUB, D_FF), F32),)
        extra_specs = (pl.BlockSpec((None, SUB, tn), lambda j, i: (i // tiles_per_seq, 0, j)),)
        extra_scratch = []
    else:
        assert nm == 1
        nseq = (state_rows - SUB) // SUB
        prev_spec = pl.BlockSpec((2, nseq, tn), lambda j, i: (0, 0, j))
        extra_shapes = (jax.ShapeDtypeStruct((2, nseq, D_FF), F32), jax.ShapeDtypeStruct((SUB, D_FF), F32))
        extra_specs = (pl.BlockSpec((2, nseq, tn), lambda j, i: (0, 0, j)),
                       pl.BlockSpec((SUB, tn), lambda j, i: (0, j)))
        extra_scratch = [pltpu.VMEM((tn // nsub // LANE, tm, LANE), F32)]
    return pl.pallas_call(
        functools.partial(_ffn_in_kernel, tiles_per_seq=tiles_per_seq, state_rows=state_rows, nsub=nsub),
        out_shape=(jax.ShapeDtypeStruct((m, D_FF), BF16),) + extra_shapes,
        grid=(nj, nm),
        in_specs=[
            pl.BlockSpec((tm, d), lambda j, i: (i, 0)),
            pl.BlockSpec((d, tn), lambda j, i: (0, j)),
            pl.BlockSpec((d, tn), lambda j, i: (0, nj + j)),
            prev_spec,
            pl.BlockSpec((3, tn), lambda j, i: (0, j)),
        ],
        out_specs=(pl.BlockSpec((tm, tn), lambda j, i: (i, j)),) + extra_specs,
        scratch_shapes=[pltpu.VMEM((d, tn), BF16), pltpu.VMEM((d, tn), BF16), pltpu.VMEM((SUB, tn), F32)]
        + extra_scratch,
        compiler_params=pltpu.CompilerParams(
            dimension_semantics=("parallel", "arbitrary"), vmem_limit_bytes=VMEM_LIMIT_BIG),
        name="ffn_in",
    )(h, w, w, prev, cw)


def _ffn_out_kernel(a_ref, w_ref, x_ref, g_ref, o_ref):
    kk = pl.program_id(1)

    @pl.when(kk == 0)
    def _():
        o_ref[...] = x_ref[...]

    o_ref[...] += jnp.dot(a_ref[...], w_ref[...], preferred_element_type=F32)

    @pl.when(kk == pl.num_programs(1) - 1)
    def _():
        x = o_ref[...]
        ms = jnp.mean(x * x, axis=-1, keepdims=True)
        o_ref[...] = x * lax.rsqrt(ms + RMS_EPS) * g_ref[...]


def _ffn_out(act, w, x1, g, m, tm, tk):
    kdim, d = w.shape
    return pl.pallas_call(
        _ffn_out_kernel,
        out_shape=jax.ShapeDtypeStruct((m, d), F32),
        grid=(m // tm, kdim // tk),
        in_specs=[
            pl.BlockSpec((tm, tk), lambda i, k: (i, k)),
            pl.BlockSpec((tk, d), lambda i, k: (k, 0)),
            pl.BlockSpec((tm, d), lambda i, k: (i, 0)),
            pl.BlockSpec((1, d), lambda i, k: (0, 0)),
        ],
        out_specs=pl.BlockSpec((tm, d), lambda i, k: (i, 0)),
        compiler_params=pltpu.CompilerParams(
            dimension_semantics=("parallel", "arbitrary"), vmem_limit_bytes=VMEM_LIMIT),
        name="ffn_out",
    )(act, w, x1, g)


def _delta_kernel(qkvz_ref, ba_ref, prev_ref, s0_ref, cw_ref, alog_ref, dtb_ref, on_ref,
                  o_ref, s_ref, carry_ref, *, C, NB):
    R = NB * C
    ci = pl.program_id(1)

    @pl.when(ci == 0)
    def _():
        if NB == 1:
            carry_ref[...] = prev_ref[...]
        s_ref[...] = s0_ref[...]

    tri, ones_seq = _seq_masks(R, C)

    ba = ba_ref[...]
    beta_full = _sigmoid(ba)
    g_full = -jnp.exp(alog_ref[...]) * _softplus(ba + dtb_ref[...])
    gc_full = _xdot_r(tri, g_full, 3)
    gtot_full = gc_full[C - 1:C, :] if NB == 1 else _xdot_r(ones_seq, g_full, 3)

    def conv_silu(c0):
        x = qkvz_ref[:, c0:c0 + LANE]
        if NB == 1:
            hist = carry_ref[:, c0:c0 + LANE]
        else:
            hist = [_repeat_rows(prev_ref[j, :, c0:c0 + LANE], SUB) for j in range(3)]
        y = x * cw_ref[3:4, c0:c0 + LANE]
        for k in (1, 2, 3):
            y = y + _delayed(x, hist, k, NB) * cw_ref[3 - k:4 - k, c0:c0 + LANE]
        return _silu(y)

    ones_ll = jnp.ones((LANE, LANE), BF16)

    def lane_sum(x):
        return jnp.dot(x.astype(BF16), ones_ll, preferred_element_type=F32)

    def l2n_many(xs):
        sums = [lane_sum(x * x) for x in xs]
        return [x * lax.rsqrt(sq + 1e-6) for x, sq in zip(xs, sums)]

    def seq_rows(x, n):
        return x[n * C:(n + 1) * C]

    heads = range(H_A)
    seqs = range(NB)
    q = [x * (HEAD_A ** -0.5) for x in l2n_many([conv_silu(h * HEAD_A) for h in heads])]
    k = l2n_many([conv_silu(W_A + h * HEAD_A) for h in heads])
    v = [conv_silu(2 * W_A + h * HEAD_A) for h in heads]
    bcol = [beta_full[:, h:h + 1] for h in heads]
    gcol = [gc_full[:, H_A + h:H_A + h + 1] for h in heads]
    gtot = [gtot_full[:, H_A + h:H_A + h + 1] for h in heads]
    eg = [jnp.exp(gcol[h]) for h in heads]
    kb = [k[h] * bcol[h] for h in heads]

    pk = _Packed(R, C)
    hpairs = range(H_A // 2)
    zc = jnp.zeros((R, HEAD_A), F32)
    row2 = _iota((2 * R, LANE), 0)
    lane2 = _iota((2 * R, LANE), 1)
    ones_cl = jnp.ones((R, LANE), F32)
    gc2 = jnp.concatenate([gc_full, gc_full], axis=0)
    kq, gamma = [], []
    for hp in hpairs:
        h0, h1 = 2 * hp, 2 * hp + 1
        lhs = jnp.concatenate([jnp.concatenate([kb[h0], kb[h1]], axis=1),
                               jnp.concatenate([q[h0], q[h1]], axis=1)], axis=0)
        rk = jnp.concatenate([jnp.concatenate([k[h0], zc], axis=1),
                              jnp.concatenate([zc, k[h1]], axis=1)], axis=0)
        kq.append(_bdot_nt(lhs, rk))
        sel = lane2 == jnp.where(row2 < R, H_A + h0, H_A + h1)
        grow = _xdot_r(ones_cl, jnp.where(sel, gc2, 0.0), 3, NT_DIMS)
        gcol_p = jnp.where(pk.first, gcol[h0], gcol[h1])
        gamma.append(jnp.exp(jnp.where(pk.incl, gcol_p - grow, NEG_BIG)))
    tinv = pk.inverse_many([-jnp.where(pk.strict, kq[hp][0:R] * gamma[hp], 0.0) for hp in hpairs])
    uw = []
    for hp in hpairs:
        h0, h1 = 2 * hp, 2 * hp + 1
        z2 = jnp.zeros((R, 2 * HEAD_A), F32)
        rhs = jnp.concatenate([jnp.concatenate([v[h0] * bcol[h0], kb[h0] * eg[h0], z2], axis=1),
                               jnp.concatenate([z2, v[h1] * bcol[h1], kb[h1] * eg[h1]], axis=1)], axis=0)
        uw.append(_bdot(tinv[hp], rhs))
    u = [uw[h // 2][:, (h % 2) * 2 * HEAD_A:(h % 2) * 2 * HEAD_A + HEAD_A] for h in heads]
    w = [uw[h // 2][:, (h % 2) * 2 * HEAD_A + HEAD_A:(h % 2 + 1) * 2 * HEAD_A] for h in heads]
    qd = [q[h] * eg[h] for h in heads]
    s = [[s_ref[n, h] for h in heads] for n in seqs]
    wqs = [[_bdot(jnp.concatenate([seq_rows(w[h], n), seq_rows(qd[h], n)], axis=0), s[n][h]) for h in heads]
           for n in seqs]
    ws = [jnp.concatenate([wqs[n][h][0:C] for n in seqs], axis=0) if NB > 1 else wqs[0][h][0:C] for h in heads]
    qs = [jnp.concatenate([wqs[n][h][C:2 * C] for n in seqs], axis=0) if NB > 1 else wqs[0][h][C:2 * C]
          for h in heads]
    v_new = [u[h] - ws[h] for h in heads]
    o = []
    for hp in hpairs:
        h0, h1 = 2 * hp, 2 * hp + 1
        vn_bd = jnp.concatenate([jnp.concatenate([v_new[h0], zc], axis=1),
                                 jnp.concatenate([zc, v_new[h1]], axis=1)], axis=0)
        op = _bdot(kq[hp][R:2 * R] * gamma[hp], vn_bd)
        o.append(qs[h0] + op[:, :HEAD_A])
        o.append(qs[h1] + op[:, HEAD_A:])
    kd = [k[h] * jnp.exp(gtot[h] - gcol[h]) for h in heads]
    for n in seqs:
        for h in heads:
            glast = gtot[h] if NB == 1 else gtot[h][n * C:n * C + 1]
            s_ref[n, h] = s[n][h] * jnp.exp(glast) + _bdot_tn(seq_rows(kd[h], n), seq_rows(v_new[h], n))
    osq = [lane_sum(o[h] * o[h]) for h in heads]
    for h in heads:
        z = qkvz_ref[:, 3 * W_A + h * HEAD_A:3 * W_A + (h + 1) * HEAD_A]
        oh = o[h] * lax.rsqrt(osq[h] * (1.0 / HEAD_A) + RMS_EPS)
        o_ref[:, h * HEAD_A:(h + 1) * HEAD_A] = (oh * on_ref[...] * _silu(z)).astype(BF16)

    if NB == 1:
        carry_ref[...] = qkvz_ref[C - SUB:C, 0:3 * W_A]


def _delta_mixer(proj, row0, nb, seq, C, NB, prev, s0, cw, alog_row, dtb_row, onorm):
    nch = seq // C
    assert NB == 1 or (nch == 1 and C == SUB)
    R = NB * C
    blk0 = row0 // R
    bcast = s0.shape[0] == 1
    bsel = (lambda b: 0) if bcast else (lambda b: b)
    return pl.pallas_call(
        functools.partial(_delta_kernel, C=C, NB=NB),
        out_shape=(jax.ShapeDtypeStruct((nb * seq, W_A), BF16),
                   jax.ShapeDtypeStruct((nb, H_A, HEAD_A, HEAD_A), F32)),
        grid=(nb // NB, nch),
        in_specs=[
            pl.BlockSpec((R, 4 * W_A), lambda b, c: (blk0 + b * nch + c, 0)),
            pl.BlockSpec((R, LANE), lambda b, c: (blk0 + b * nch + c, SM0 // LANE)),
            (pl.BlockSpec((SUB, 3 * W_A), lambda b, c: (0, 0)) if NB == 1
             else pl.BlockSpec((3, NB, 3 * W_A), lambda b, c: (0, b, 0))),
            pl.BlockSpec((NB, H_A, HEAD_A, HEAD_A), lambda b, c: (bsel(b), 0, 0, 0)),
            pl.BlockSpec((4, 3 * W_A), lambda b, c: (0, 0)),
            pl.BlockSpec((1, LANE), lambda b, c: (0, 0)),
            pl.BlockSpec((1, LANE), lambda b, c: (0, 0)),
            pl.BlockSpec((1, HEAD_A), lambda b, c: (0, 0)),
        ],
        out_specs=(pl.BlockSpec((R, W_A), lambda b, c: (b * nch + c, 0)),
                   pl.BlockSpec((NB, H_A, HEAD_A, HEAD_A), lambda b, c: (b, 0, 0, 0))),
        scratch_shapes=[pltpu.VMEM((SUB, 3 * W_A), F32)],
        compiler_params=pltpu.CompilerParams(
            dimension_semantics=("parallel", "arbitrary"), vmem_limit_bytes=VMEM_LIMIT),
        name="delta_mixer",
    )(proj, proj, prev, s0, cw, alog_row, dtb_row, onorm)


def _delta_seq_kernel(qkv0_ref, ba0_ref, qkvn_ref, ban_ref, z_ref, prev_ref, s0_ref, cw_ref, alog_ref, dtb_ref,
                      on_ref, o_ref, s_ref, carry_ref, pre_ref, small_ref, gtot_ref, *, C):
    ci = pl.program_id(1)
    heads = range(H_A)
    hpairs = range(H_A // 2)
    tri, _ = _seq_masks(C, C)
    ones_ll = jnp.ones((LANE, LANE), BF16)

    def lane_sum(x):
        return jnp.dot(x.astype(BF16), ones_ll, preferred_element_type=F32)

    def preamble(qkv_ref, ba_ref, hist_ref, slot):
        ba = ba_ref[...]
        g_full = -jnp.exp(alog_ref[...]) * _softplus(ba + dtb_ref[...])
        gc_full = _xdot_r(tri, g_full, 3)
        small_ref[slot, 0] = _sigmoid(ba)
        small_ref[slot, 1] = gc_full
        gtot_ref[slot] = jnp.broadcast_to(gc_full[C - 1:C, :], (SUB, LANE))
        yield
        for part in range(3):
            xs = []
            for h in heads:
                c0 = part * W_A + h * HEAD_A
                x = qkv_ref[:, c0:c0 + LANE]
                hist = hist_ref[:, c0:c0 + LANE]
                y = x * cw_ref[3:4, c0:c0 + LANE]
                for k in (1, 2, 3):
                    y = y + _shifted(x, hist, k) * cw_ref[3 - k:4 - k, c0:c0 + LANE]
                xs.append(_silu(y))
                yield
            if part < 2:
                sums = [lane_sum(x * x) for x in xs]
                yield
                scale = HEAD_A ** -0.5 if part == 0 else 1.0
                xs = [x * (lax.rsqrt(sq + 1e-6) * scale) for x, sq in zip(xs, sums)]
            for h in heads:
                pre_ref[slot, part, :, h * HEAD_A:(h + 1) * HEAD_A] = xs[h]
            yield
        carry_ref[...] = qkv_ref[C - SUB:C, :]

    @pl.when(ci == 0)
    def _():
        for _ in preamble(qkv0_ref, ba0_ref, prev_ref, 0):
            pass
        s_ref[...] = s0_ref[...]

    slot = ci % 2
    ahead = preamble(qkvn_ref, ban_ref, carry_ref, 1 - slot)
    calls = [0]

    def tick():
        calls[0] += 1
        if calls[0] % 3 == 0:
            next(ahead, None)

    def each(fn, items):
        out = {}
        for it in items:
            out[it] = fn(it)
            tick()
        return out

    beta_full = small_ref[slot, 0]
    gc_full = small_ref[slot, 1]
    q = {h: pre_ref[slot, 0, :, h * HEAD_A:(h + 1) * HEAD_A] for h in heads}
    k = {h: pre_ref[slot, 1, :, h * HEAD_A:(h + 1) * HEAD_A] for h in heads}
    v = {h: pre_ref[slot, 2, :, h * HEAD_A:(h + 1) * HEAD_A] for h in heads}
    bcol = {h: beta_full[:, h:h + 1] for h in heads}
    gcol = {h: gc_full[:, H_A + h:H_A + h + 1] for h in heads}
    gtot_row = gtot_ref[slot]
    gtot = {h: gtot_row[0:1, H_A + h:H_A + h + 1] for h in heads}
    eg = {h: jnp.exp(gcol[h]) for h in heads}
    kb = {h: k[h] * bcol[h] for h in heads}

    pk = _Packed(C, C)
    zc = jnp.zeros((C, HEAD_A), F32)
    row2 = _iota((2 * C, LANE), 0)
    lane2 = _iota((2 * C, LANE), 1)
    ones_cl = jnp.ones((C, LANE), F32)
    gc2 = jnp.concatenate([gc_full, gc_full], axis=0)

    def kq_of(hp):
        h0, h1 = 2 * hp, 2 * hp + 1
        lhs = jnp.concatenate([jnp.concatenate([kb[h0], kb[h1]], axis=1),
                               jnp.concatenate([q[h0], q[h1]], axis=1)], axis=0)
        rk = jnp.concatenate([jnp.concatenate([k[h0], zc], axis=1),
                              jnp.concatenate([zc, k[h1]], axis=1)], axis=0)
        return _bdot_nt(lhs, rk)

    def gamma_of(hp):
        h0, h1 = 2 * hp, 2 * hp + 1
        sel = lane2 == jnp.where(row2 < C, H_A + h0, H_A + h1)
        grow = _xdot_r(ones_cl, jnp.where(sel, gc2, 0.0), 3, NT_DIMS)
        gcol_p = jnp.where(pk.first, gcol[h0], gcol[h1])
        return jnp.exp(jnp.where(pk.incl, gcol_p - grow, NEG_BIG))

    kq = each(kq_of, hpairs)
    gamma = each(gamma_of, hpairs)
    ps = {hp: -jnp.where(pk.strict, kq[hp][0:C] * gamma[hp], 0.0) for hp in hpairs}
    ts = {hp: pk.eye + ps[hp] for hp in hpairs}
    n = 1
    while 2 * n < C:
        ps = each(lambda hp: _bdot(ps[hp], pk.block_diag(ps[hp])), hpairs)
        ts = each(lambda hp: ts[hp] + _bdot(ps[hp], pk.block_diag(ts[hp])), hpairs)
        n *= 2

    def uw_of(hp):
        h0, h1 = 2 * hp, 2 * hp + 1
        z2 = jnp.zeros((C, 2 * HEAD_A), F32)
        rhs = jnp.concatenate([jnp.concatenate([v[h0] * bcol[h0], kb[h0] * eg[h0], z2], axis=1),
                               jnp.concatenate([z2, v[h1] * bcol[h1], kb[h1] * eg[h1]], axis=1)], axis=0)
        return _bdot(ts[hp], rhs)

    uw = each(uw_of, hpairs)
    u = {h: uw[h // 2][:, (h % 2) * 2 * HEAD_A:(h % 2) * 2 * HEAD_A + HEAD_A] for h in heads}
    w = {h: uw[h // 2][:, (h % 2) * 2 * HEAD_A + HEAD_A:(h % 2 + 1) * 2 * HEAD_A] for h in heads}
    s = {h: s_ref[0, h] for h in heads}
    wqs = each(lambda h: _bdot(jnp.concatenate([w[h], q[h] * eg[h]], axis=0), s[h]), heads)
    v_new = {h: u[h] - wqs[h][0:C] for h in heads}

    def o_of(hp):
        h0, h1 = 2 * hp, 2 * hp + 1
        vn_bd = jnp.concatenate([jnp.concatenate([v_new[h0], zc], axis=1),
                                 jnp.concatenate([zc, v_new[h1]], axis=1)], axis=0)
        return _bdot(kq[hp][C:2 * C] * gamma[hp], vn_bd)

    op = each(o_of, hpairs)
    o = {h: wqs[h][C:2 * C] + op[h // 2][:, (h % 2) * HEAD_A:(h % 2 + 1) * HEAD_A] for h in heads}
    for h in heads:
        kd = k[h] * jnp.exp(gtot[h] - gcol[h])
        s_ref[0, h] = s[h] * jnp.exp(gtot[h]) + _bdot_tn(kd, v_new[h])
        tick()
    osq = each(lambda h: lane_sum(o[h] * o[h]), heads)
    for h in heads:
        z = z_ref[:, h * HEAD_A:(h + 1) * HEAD_A]
        oh = o[h] * lax.rsqrt(osq[h] * (1.0 / HEAD_A) + RMS_EPS)
        o_ref[:, h * HEAD_A:(h + 1) * HEAD_A] = (oh * on_ref[...] * _silu(z)).astype(BF16)
        tick()
    for _ in ahead:
        pass


def _delta_seq_mixer(proj, nb, seq, C, prev, s0, cw, alog_row, dtb_row, onorm):
    nch = seq // C
    nxt = lambda b, c: b * nch + jnp.minimum(c + 1, nch - 1)
    return pl.pallas_call(
        functools.partial(_delta_seq_kernel, C=C),
        out_shape=(jax.ShapeDtypeStruct((nb * seq, W_A), BF16),
                   jax.ShapeDtypeStruct((nb, H_A, HEAD_A, HEAD_A), F32)),
        grid=(nb, nch),
        in_specs=[
            pl.BlockSpec((C, 3 * W_A), lambda b, c: (b * nch, 0)),
            pl.BlockSpec((C, LANE), lambda b, c: (b * nch, SM0 // LANE)),
            pl.BlockSpec((C, 3 * W_A), lambda b, c: (nxt(b, c), 0)),
            pl.BlockSpec((C, LANE), lambda b, c: (nxt(b, c), SM0 // LANE)),
            pl.BlockSpec((C, W_A), lambda b, c: (b * nch + c, 3)),
            pl.BlockSpec((SUB, 3 * W_A), lambda b, c: (0, 0)),
            pl.BlockSpec((1, H_A, HEAD_A, HEAD_A), lambda b, c: (0, 0, 0, 0)),
            pl.BlockSpec((4, 3 * W_A), lambda b, c: (0, 0)),
            pl.BlockSpec((1, LANE), lambda b, c: (0, 0)),
            pl.BlockSpec((1, LANE), lambda b, c: (0, 0)),
            pl.BlockSpec((1, HEAD_A), lambda b, c: (0, 0)),
        ],
        out_specs=(pl.BlockSpec((C, W_A), lambda b, c: (b * nch + c, 0)),
                   pl.BlockSpec((1, H_A, HEAD_A, HEAD_A), lambda b, c: (b, 0, 0, 0))),
        scratch_shapes=[pltpu.VMEM((SUB, 3 * W_A), F32), pltpu.VMEM((2, 3, C, W_A), F32),
                        pltpu.VMEM((2, 2, C, LANE), F32), pltpu.VMEM((2, SUB, LANE), F32)],
        compiler_params=pltpu.CompilerParams(
            dimension_semantics=("parallel", "arbitrary"), vmem_limit_bytes=VMEM_LIMIT),
        name="delta_seq_mixer",
    )(proj, proj, proj, proj, proj, prev, s0, cw, alog_row, dtb_row, onorm)


def _rwkv_kernel(r_ref, k_ref, v_ref, sm_ref, pr_ref, pk_ref, pv_ref, psm_ref, s0_ref,
                 mur_ref, muk_ref, muv_ref, musm_ref, w2_ref, a2_ref, g2_ref,
                 w0_ref, a0_ref, kk_ref, ka_ref, rk_ref, lnw_ref, lnb_ref,
                 o_ref, s_ref, cr_ref, ck_ref, cv_ref, csm_ref, st_ref, *, C, NB, G):
    R = NB * C
    ci = pl.program_id(2)
    nch = pl.num_programs(2)
    seqs = range(NB)
    pairs = range(G)

    r2 = _iota((LANE, LANE), 0)
    c2 = _iota((LANE, LANE), 1)
    same_head = (r2 < HEAD_B) == (c2 < HEAD_B)
    ones_bd = same_head.astype(F32)
    spread = (_iota((HEAD_B, LANE), 0) == (_iota((HEAD_B, LANE), 1) & (HEAD_B - 1))).astype(F32)
    gather = ((_iota((LANE, HEAD_B), 0) & (HEAD_B - 1)) == _iota((LANE, HEAD_B), 1)).astype(F32)

    @pl.when(ci == 0)
    def _():
        if NB == 1:
            cr_ref[...] = pr_ref[...]
            ck_ref[...] = pk_ref[...]
            cv_ref[...] = pv_ref[...]
            csm_ref[...] = psm_ref[...]
        for n in seqs:
            for p in pairs:
                st_ref[n * G + p] = jnp.where(same_head, _xdot(s0_ref[n, p], spread, 3), 0.0)

    hr_ref, hk_ref, hv_ref, hsm_ref = ((cr_ref, ck_ref, cv_ref, csm_ref) if NB == 1
                                       else (pr_ref, pk_ref, pv_ref, psm_ref))
    tri, ones_seq = _seq_masks(R, C)

    def lerp(x, hist_ref, mu):
        hist = hist_ref[...] if NB == 1 else [_repeat_rows(hist_ref[0], SUB)]
        return x + (_delayed(x, hist, 1, NB) - x) * mu

    sm = sm_ref[...]
    xs = lerp(sm, hsm_ref, musm_ref[...])
    slab_wa = xs[:, SM_WA:SM_WA + LANE]
    slab_g = xs[:, SM_G:SM_G + 2 * LANE]

    def col(x, p):
        return x[:, p * LANE:(p + 1) * LANE]

    def to_rows(x):
        return jnp.concatenate([col(x, p) for p in pairs], axis=0)

    def to_cols(y):
        return jnp.concatenate([y[p * R:(p + 1) * R] for p in pairs], axis=1)

    def seq_rows(x, n):
        return x[n * C:(n + 1) * C]

    xr = lerp(r_ref[...], hr_ref, mur_ref[...])
    xk = lerp(k_ref[...], hk_ref, muk_ref[...])
    xv = lerp(v_ref[...], hv_ref, muv_ref[...])
    wlog = -_softplus(-(w0_ref[...] + _bdot(jnp.tanh(slab_wa), w2_ref[...]))) - 0.5
    ld = -jnp.exp(wlog)
    aa = _sigmoid(a0_ref[...] + _bdot(slab_wa, a2_ref[...]))
    gate = _bdot(_sigmoid(slab_g), g2_ref[...])
    kkr = xk * kk_ref[...]
    kkn = kkr * lax.rsqrt(to_cols(_bdot(to_rows(kkr * kkr), ones_bd)) + 1e-6)
    k2 = xk * (1.0 + (aa - 1.0) * ka_ref[...])
    lp = _xdot_r(tri, ld, 3)
    lp_tot = lp[C - 1:C, :] if NB == 1 else _xdot_r(ones_seq, ld, 3)
    e_neg = jnp.exp(-lp)
    e_rem = jnp.exp(lp_tot - lp)
    at = -kkn * jnp.exp(lp - ld)
    rt = xr * jnp.exp(lp)
    kb = kkn * aa
    bt = kb * e_neg
    kt = k2 * e_neg
    bhat = kb * e_rem
    khat = k2 * e_rem
    p_tot = jnp.exp(lp_tot)

    s = [[st_ref[n * G + p] for p in pairs] for n in seqs]
    atp = [col(at, p) for p in pairs]
    rtp = [col(rt, p) for p in pairs]
    ars = [[_bdot_nt(jnp.concatenate([seq_rows(atp[p], n), seq_rows(rtp[p], n)], axis=0), s[n][p])
            for p in pairs] for n in seqs]
    if NB == 1:
        x_state = [ars[0][p][0:C] for p in pairs]
        o_state = [ars[0][p][C:2 * C] for p in pairs]
    else:
        x_state = [jnp.concatenate([ars[n][p][0:C] for n in seqs], axis=0) for p in pairs]
        o_state = [jnp.concatenate([ars[n][p][C:2 * C] for n in seqs], axis=0) for p in pairs]
    pk = _Packed(R, C)
    stack_mask = (_iota((2 * R, LANE), 0) < R) == (_iota((2 * R, LANE), 1) < HEAD_B)

    def by_head(x):
        return jnp.where(stack_mask, jnp.concatenate([x, x], axis=0), 0.0)

    ar = [jnp.concatenate([atp[p], rtp[p]], axis=0) for p in pairs]
    ab = [_bdot_nt(ar[p], by_head(col(bt, p))) for p in pairs]
    ak = [_bdot_nt(ar[p], by_head(col(kt, p))) for p in pairs]
    tinv = pk.inverse_many([jnp.where(pk.strict, m[0:R], 0.0) for m in ab])
    v_bh = [by_head(col(xv, p)) for p in pairs]
    y = [x_state[p] + _bdot(jnp.where(pk.strict, ak[p][0:R], 0.0), v_bh[p]) for p in pairs]
    u = [_bdot(tinv[p], by_head(y[p])) for p in pairs]
    o = [o_state[p] + _bdot(jnp.where(pk.incl, ab[p][R:2 * R], 0.0), by_head(u[p]))
         + _bdot(jnp.where(pk.incl, ak[p][R:2 * R], 0.0), v_bh[p]) for p in pairs]
    for n in seqs:
        for p in pairs:
            uv = jnp.concatenate([seq_rows(u[p], n), seq_rows(col(xv, p), n)], axis=0)
            bkhat = jnp.concatenate([seq_rows(col(bhat, p), n), seq_rows(col(khat, p), n)], axis=0)
            decay = col(p_tot, p) if NB == 1 else col(p_tot, p)[n * C:n * C + 1]
            s_new = s[n][p] * decay + _bdot_tn(uv, bkhat)
            st_ref[n * G + p] = jnp.where(same_head, s_new, 0.0)

    o_rows = jnp.concatenate(o, axis=0)
    mean = _bdot(o_rows, ones_bd) * (1.0 / HEAD_B)
    d = o_rows - mean
    var = _bdot(d * d, ones_bd) * (1.0 / HEAD_B)
    on = to_cols(d * lax.rsqrt(var + GN_EPS)) * lnw_ref[...] + lnb_ref[...]
    bonus = to_cols(_bdot(to_rows(xr * k2 * rk_ref[...]), ones_bd)) * xv
    o_ref[...] = ((on + bonus) * gate).astype(BF16)

    if NB == 1:
        cr_ref[...] = r_ref[C - SUB:C, :]
        ck_ref[...] = k_ref[C - SUB:C, :]
        cv_ref[...] = v_ref[C - SUB:C, :]
        csm_ref[...] = sm_ref[C - SUB:C, :]

    @pl.when(ci == nch - 1)
    def _():
        for n in seqs:
            for p in pairs:
                s_ref[n, p] = _xdot(st_ref[n * G + p], gather, 3)


def _rwkv_mixer(proj, row0, nb, seq, C, NB, G, prev_rkv, prev_sm, s0, mu_rkv, mu_sm, w2p, a2p, g2p,
                w0, a0, k_k, k_a, r_k, lnw, lnb):
    nch = seq // C
    assert NB == 1 or (nch == 1 and C == SUB)
    R = NB * C
    ng = N_PAIR // G
    gw = G * LANE
    blk0 = row0 // R
    bcast = s0.shape[0] == 1
    bsel = (lambda b: 0) if bcast else (lambda b: b)

    def proj_spec(col0):
        return pl.BlockSpec((R, gw), lambda b, g, c: (blk0 + b * nch + c, col0 // gw + g))

    def prev_spec(part):
        if NB == 1:
            return pl.BlockSpec((SUB, gw), lambda b, g, c: (0, part * (W_B // gw) + g))
        return pl.BlockSpec((1, NB, gw), lambda b, g, c: (0, b, part * (W_B // gw) + g))

    def vec_spec(part=0):
        return pl.BlockSpec((1, gw), lambda b, g, c: (0, part * (W_B // gw) + g))

    in_specs = [
        proj_spec(RKV0), proj_spec(RKV0 + W_B), proj_spec(RKV0 + 2 * W_B),
        pl.BlockSpec((R, SM_W), lambda b, g, c: (blk0 + b * nch + c, SM0 // SM_W)),
        prev_spec(0), prev_spec(1), prev_spec(2),
        (pl.BlockSpec((SUB, SM_W), lambda b, g, c: (0, 0)) if NB == 1
         else pl.BlockSpec((1, NB, SM_W), lambda b, g, c: (0, b, 0))),
        pl.BlockSpec((NB, G, LANE, HEAD_B), lambda b, g, c: (bsel(b), g, 0, 0)),
        vec_spec(0), vec_spec(1), vec_spec(2),
        pl.BlockSpec((1, SM_W), lambda b, g, c: (0, 0)),
        pl.BlockSpec((LANE, gw), lambda b, g, c: (0, g)),
        pl.BlockSpec((LANE, gw), lambda b, g, c: (0, g)),
        pl.BlockSpec((2 * LANE, gw), lambda b, g, c: (0, g)),
        vec_spec(), vec_spec(), vec_spec(), vec_spec(), vec_spec(), vec_spec(), vec_spec(),
    ]
    return pl.pallas_call(
        functools.partial(_rwkv_kernel, C=C, NB=NB, G=G),
        out_shape=(jax.ShapeDtypeStruct((nb * seq, W_B), BF16),
                   jax.ShapeDtypeStruct((nb, N_PAIR, LANE, HEAD_B), F32)),
        grid=(nb // NB, ng, nch),
        in_specs=in_specs,
        out_specs=(pl.BlockSpec((R, gw), lambda b, g, c: (b * nch + c, g)),
                   pl.BlockSpec((NB, G, LANE, HEAD_B), lambda b, g, c: (b, g, 0, 0))),
        scratch_shapes=[pltpu.VMEM((SUB, gw), F32), pltpu.VMEM((SUB, gw), F32), pltpu.VMEM((SUB, gw), F32),
                        pltpu.VMEM((SUB, SM_W), F32), pltpu.VMEM((NB * G, LANE, LANE), F32)],
        compiler_params=pltpu.CompilerParams(
            dimension_semantics=("parallel", "parallel", "arbitrary"), vmem_limit_bytes=VMEM_LIMIT),
        name="rwkv_mixer",
    )(proj, proj, proj, proj, prev_rkv, prev_rkv, prev_rkv, prev_sm, s0,
      mu_rkv, mu_rkv, mu_rkv, mu_sm, w2p, a2p, g2p, w0, a0, k_k, k_a, r_k, lnw, lnb)


_AT, _RT, _BT, _KT, _BHAT, _KHAT, _XV, _GATE, _BONUS, _N_PRE = range(10)


def _rwkv_seq_kernel(r0_ref, k0_ref, v0_ref, sm0_ref, rn_ref, kn_ref, vn_ref, smn_ref,
                     pr_ref, pk_ref, pv_ref, psm_ref, s0_ref,
                     mur_ref, muk_ref, muv_ref, musm_ref, w2_ref, a2_ref, g2_ref,
                     w0_ref, a0_ref, kk_ref, ka_ref, rk_ref, lnw_ref, lnb_ref,
                     o_ref, s_ref, cr_ref, ck_ref, cv_ref, csm_ref, st_ref, pre_ref, ptot_ref, *, C, G, GRP):
    ci = pl.program_id(2)
    nch = pl.num_programs(2)
    pairs = range(G)

    r2 = _iota((LANE, LANE), 0)
    c2 = _iota((LANE, LANE), 1)
    same_head = (r2 < HEAD_B) == (c2 < HEAD_B)
    ones_bd = same_head.astype(F32)
    spread = (_iota((HEAD_B, LANE), 0) == (_iota((HEAD_B, LANE), 1) & (HEAD_B - 1))).astype(F32)
    gather = ((_iota((LANE, HEAD_B), 0) & (HEAD_B - 1)) == _iota((LANE, HEAD_B), 1)).astype(F32)
    tri, _ = _seq_masks(C, C)

    def col(x, p):
        return x[:, p * LANE:(p + 1) * LANE]

    def to_rows(x):
        return jnp.concatenate([col(x, p) for p in pairs], axis=0)

    def to_cols(y):
        return jnp.concatenate([y[p * C:(p + 1) * C] for p in pairs], axis=1)

    def lerp(x, hist, mu):
        return x + (_shifted(x, hist, 1) - x) * mu

    def preamble(r_ref, k_ref, v_ref, sm_ref, hr_ref, hk_ref, hv_ref, hsm_ref, slot):
        xs = lerp(sm_ref[...], hsm_ref[...], musm_ref[...])
        yield
        slab_wa = xs[:, SM_WA:SM_WA + LANE]
        wl = _bdot(jnp.tanh(slab_wa), w2_ref[...])
        yield
        al = _bdot(slab_wa, a2_ref[...])
        yield
        pre_ref[slot, _GATE] = _bdot(_sigmoid(xs[:, SM_G:SM_G + 2 * LANE]), g2_ref[...])
        yield
        xr = lerp(r_ref[...], hr_ref[...], mur_ref[...])
        yield
        xk = lerp(k_ref[...], hk_ref[...], muk_ref[...])
        yield
        xv = lerp(v_ref[...], hv_ref[...], muv_ref[...])
        pre_ref[slot, _XV] = xv
        yield
        ld = -jnp.exp(-_softplus(-(w0_ref[...] + wl)) - 0.5)
        yield
        lp = _xdot_r(tri, ld, 3)
        yield
        aa = _sigmoid(a0_ref[...] + al)
        yield
        kkr = xk * kk_ref[...]
        ss = to_cols(_bdot(to_rows(kkr * kkr), ones_bd))
        yield
        kkn = kkr * lax.rsqrt(ss + 1e-6)
        k2 = xk * (1.0 + (aa - 1.0) * ka_ref[...])
        yield
        pre_ref[slot, _BONUS] = to_cols(_bdot(to_rows(xr * k2 * rk_ref[...]), ones_bd)) * xv
        yield
        pre_ref[slot, _AT] = -kkn * jnp.exp(lp - ld)
        yield
        pre_ref[slot, _RT] = xr * jnp.exp(lp)
        yield
        kb = kkn * aa
        e_neg = jnp.exp(-lp)
        pre_ref[slot, _BT] = kb * e_neg
        yield
        pre_ref[slot, _KT] = k2 * e_neg
        yield
        lp_tot = lp[C - 1:C, :]
        e_rem = jnp.exp(lp_tot - lp)
        pre_ref[slot, _BHAT] = kb * e_rem
        yield
        pre_ref[slot, _KHAT] = k2 * e_rem
        ptot_ref[slot] = jnp.broadcast_to(jnp.exp(lp_tot), (SUB, lp.shape[1]))
        yield
        cr_ref[...] = r_ref[C - SUB:C, :]
        ck_ref[...] = k_ref[C - SUB:C, :]
        cv_ref[...] = v_ref[C - SUB:C, :]
        csm_ref[...] = sm_ref[C - SUB:C, :]

    @pl.when(ci == 0)
    def _():
        for _ in preamble(r0_ref, k0_ref, v0_ref, sm0_ref, pr_ref, pk_ref, pv_ref, psm_ref, 0):
            pass
        for p in pairs:
            st_ref[p] = jnp.where(same_head, _xdot(s0_ref[0, p], spread, 3), 0.0)

    slot = ci % 2
    ahead = preamble(rn_ref, kn_ref, vn_ref, smn_ref, cr_ref, ck_ref, cv_ref, csm_ref, 1 - slot)
    calls = [0]

    def tick():
        calls[0] += 1
        if calls[0] % 6 == 0:
            next(ahead, None)

    def each(fn, items):
        out = {}
        for it in items:
            out[it] = fn(it)
            tick()
        return out

    def pre(idx, p):
        return pre_ref[slot, idx, :, p * LANE:(p + 1) * LANE]

    pk = _Packed(C, C)
    stack_mask = (_iota((2 * C, LANE), 0) < C) == (_iota((2 * C, LANE), 1) < HEAD_B)

    def by_head(x):
        return jnp.where(stack_mask, jnp.concatenate([x, x], axis=0), 0.0)

    def run_group(grp):
        s = {p: st_ref[p] for p in grp}
        ar = {p: jnp.concatenate([pre(_AT, p), pre(_RT, p)], axis=0) for p in grp}
        ars = each(lambda p: _bdot_nt(ar[p], s[p]), grp)
        ab = each(lambda p: _bdot_nt(ar[p], by_head(pre(_BT, p))), grp)
        ak = each(lambda p: _bdot_nt(ar[p], by_head(pre(_KT, p))), grp)
        ps = {p: jnp.where(pk.strict, ab[p][0:C], 0.0) for p in grp}
        ts = {p: pk.eye + ps[p] for p in grp}
        n = 1
        while 2 * n < C:
            ps = each(lambda p: _bdot(ps[p], pk.block_diag(ps[p])), grp)
            ts = each(lambda p: ts[p] + _bdot(ps[p], pk.block_diag(ts[p])), grp)
            n *= 2
        xv = {p: pre(_XV, p) for p in grp}
        v_bh = {p: by_head(xv[p]) for p in grp}
        y = each(lambda p: ars[p][0:C] + _bdot(jnp.where(pk.strict, ak[p][0:C], 0.0), v_bh[p]), grp)
        u = each(lambda p: _bdot(ts[p], by_head(y[p])), grp)
        o = each(lambda p: ars[p][C:2 * C] + _bdot(jnp.where(pk.incl, ab[p][C:2 * C], 0.0), by_head(u[p]))
                 + _bdot(jnp.where(pk.incl, ak[p][C:2 * C], 0.0), v_bh[p]), grp)
        for p in grp:
            uv = jnp.concatenate([u[p], xv[p]], axis=0)
            bkhat = jnp.concatenate([pre(_BHAT, p), pre(_KHAT, p)], axis=0)
            s_new = s[p] * ptot_ref[slot, 0:1, p * LANE:(p + 1) * LANE] + _bdot_tn(uv, bkhat)
            st_ref[p] = jnp.where(same_head, s_new, 0.0)
            tick()
        return o

    o = {}
    for g0 in range(0, G, GRP):
        o.update(run_group(range(g0, g0 + GRP)))
    o_rows = jnp.concatenate([o[p] for p in pairs], axis=0)
    mean = _bdot(o_rows, ones_bd) * (1.0 / HEAD_B)
    d = o_rows - mean
    var = _bdot(d * d, ones_bd) * (1.0 / HEAD_B)
    on = to_cols(d * lax.rsqrt(var + GN_EPS)) * lnw_ref[...] + lnb_ref[...]
    o_ref[...] = ((on + pre_ref[slot, _BONUS]) * pre_ref[slot, _GATE]).astype(BF16)
    for _ in ahead:
        pass

    @pl.when(ci == nch - 1)
    def _():
        for p in pairs:
            s_ref[0, p] = _xdot(st_ref[p], gather, 3)


def _rwkv_seq_mixer(proj, nb, seq, C, prev_rkv, prev_sm, s0, mu_rkv, mu_sm, w2p, a2p, g2p,
                    w0, a0, k_k, k_a, r_k, lnw, lnb):
    G = N_PAIR
    nch = seq // C
    gw = G * LANE

    def first_spec(col0, w):
        return pl.BlockSpec((C, w), lambda b, g, c: (b * nch, col0 // w))

    def next_spec(col0, w):
        return pl.BlockSpec((C, w), lambda b, g, c: (b * nch + jnp.minimum(c + 1, nch - 1), col0 // w))

    def prev_spec(part):
        return pl.BlockSpec((SUB, gw), lambda b, g, c: (0, part))

    def vec_spec(part=0):
        return pl.BlockSpec((1, gw), lambda b, g, c: (0, part))

    in_specs = [
        first_spec(RKV0, gw), first_spec(RKV0 + W_B, gw), first_spec(RKV0 + 2 * W_B, gw), first_spec(SM0, SM_W),
        next_spec(RKV0, gw), next_spec(RKV0 + W_B, gw), next_spec(RKV0 + 2 * W_B, gw), next_spec(SM0, SM_W),
        prev_spec(0), prev_spec(1), prev_spec(2),
        pl.BlockSpec((SUB, SM_W), lambda b, g, c: (0, 0)),
        pl.BlockSpec((1, G, LANE, HEAD_B), lambda b, g, c: (0, 0, 0, 0)),
        vec_spec(0), vec_spec(1), vec_spec(2),
        pl.BlockSpec((1, SM_W), lambda b, g, c: (0, 0)),
        pl.BlockSpec((LANE, gw), lambda b, g, c: (0, 0)),
        pl.BlockSpec((LANE, gw), lambda b, g, c: (0, 0)),
        pl.BlockSpec((2 * LANE, gw), lambda b, g, c: (0, 0)),
        vec_spec(), vec_spec(), vec_spec(), vec_spec(), vec_spec(), vec_spec(), vec_spec(),
    ]
    return pl.pallas_call(
        functools.partial(_rwkv_seq_kernel, C=C, G=G, GRP=G),
        out_shape=(jax.ShapeDtypeStruct((nb * seq, W_B), BF16),
                   jax.ShapeDtypeStruct((nb, N_PAIR, LANE, HEAD_B), F32)),
        grid=(nb, 1, nch),
        in_specs=in_specs,
        out_specs=(pl.BlockSpec((C, gw), lambda b, g, c: (b * nch + c, 0)),
                   pl.BlockSpec((1, G, LANE, HEAD_B), lambda b, g, c: (b, 0, 0, 0))),
        scratch_shapes=[pltpu.VMEM((SUB, gw), F32), pltpu.VMEM((SUB, gw), F32), pltpu.VMEM((SUB, gw), F32),
                        pltpu.VMEM((SUB, SM_W), F32), pltpu.VMEM((G, LANE, LANE), F32),
                        pltpu.VMEM((2, _N_PRE, C, gw), F32), pltpu.VMEM((2, SUB, gw), F32)],
        compiler_params=pltpu.CompilerParams(
            dimension_semantics=("parallel", "arbitrary", "arbitrary"), vmem_limit_bytes=VMEM_LIMIT),
        name="rwkv_seq_mixer",
    )(proj, proj, proj, proj, proj, proj, proj, proj, prev_rkv, prev_rkv, prev_rkv, prev_sm, s0,
      mu_rkv, mu_rkv, mu_rkv, mu_sm, w2p, a2p, g2p, w0, a0, k_k, k_a, r_k, lnw, lnb)


def _small_layout(cols_ba, cols_w, cols_a, cols_g, axis=-1):
    def z(n):
        shape = list(cols_w.shape)
        shape[axis] = n
        return jnp.zeros(shape, cols_w.dtype)
    return jnp.concatenate(
        [cols_ba, z(SM_WA - cols_ba.shape[axis]), cols_w, cols_a, cols_g,
         z(SM_W - SM_G - cols_g.shape[axis])], axis=axis)


def kernel(x_prompt, x_sample, state_delta, state_conv_qkv, state_wkv, state_shift, state_ffn_conv, meta, norm1, w_in, conv_a, a_log, dt_bias, onorm_a, mu_b, w0, w2, a0, a2, g2, k_k, k_a, r_k, lnx_w, lnx_b, w_o, norm2, w_ffn_in, conv_f, w_ffn_out, norm_f):
    nbp, seq_p, _ = x_prompt.shape
    nbs, seq_s, _ = x_sample.shape
    n_s = nbs * seq_s
    assert w_in.shape[0] == 1, "single-layer trunk"
    assert seq_s == SUB
    l = 0

    wt = w_in[l].T
    o_b = A_PROJ
    o_l = A_PROJ + 3 * W_B
    w_cat_t = jnp.concatenate([
        wt[:4 * W_A], wt[o_b:o_l],
        _small_layout(wt[4 * W_A:A_PROJ], wt[o_l:o_l + W_LORA], wt[o_l + W_LORA:o_l + W_LORA + A_LORA],
                      wt[o_l + W_LORA + A_LORA:], axis=0)], axis=0).astype(BF16)
    mu = mu_b[l]
    mu_rkv = mu[None, :3 * W_B]
    mu_sm = _small_layout(jnp.zeros((1, 2 * H_A), F32), mu[None, 3 * W_B:3 * W_B + W_LORA],
                          mu[None, 3 * W_B + W_LORA:3 * W_B + W_LORA + A_LORA],
                          mu[None, 3 * W_B + W_LORA + A_LORA:])
    w2p = jnp.concatenate([w2[l], jnp.zeros((LANE - W_LORA, W_B), F32)], axis=0)
    a2p = jnp.concatenate([jnp.zeros((W_LORA, W_B), F32), a2[l]], axis=0)
    g2p = jnp.concatenate([g2[l], jnp.zeros((2 * LANE - G_LORA, W_B), F32)], axis=0)
    alog_row = jnp.concatenate([jnp.zeros((H_A,), F32), a_log[l], jnp.zeros((LANE - 2 * H_A,), F32)])[None]
    dtb_row = jnp.concatenate([jnp.zeros((H_A,), F32), dt_bias[l], jnp.zeros((LANE - 2 * H_A,), F32)])[None]
    wo_bf = w_o[l].astype(BF16)
    wfo_bf = w_ffn_out[l].astype(BF16)
    row = lambda v: v.reshape(1, -1)

    def mix(proj, row0, nb, seq, C, NB, prev_qkv, prev_rkv, prev_sm, s_delta, s_wkv):
        delta_params = (conv_a[l], alog_row, dtb_row, row(onorm_a[l]))
        rwkv_params = (mu_rkv, mu_sm, w2p, a2p, g2p, row(w0[l]), row(a0[l]), row(k_k[l]), row(k_a[l]),
                       row(r_k[l]), row(lnx_w[l]), row(lnx_b[l]))
        if seq // C > 1:
            assert row0 == 0 and NB == 1
            oa, sd = _delta_seq_mixer(proj, nb, seq, C, prev_qkv, s_delta, *delta_params)
            ob, sw = _rwkv_seq_mixer(proj, nb, seq, C, prev_rkv, prev_sm, s_wkv, *rwkv_params)
        else:
            oa, sd = _delta_mixer(proj, row0, nb, seq, C, NB, prev_qkv, s_delta, *delta_params)
            ob, sw = _rwkv_mixer(proj, row0, nb, seq, C, NB, N_PAIR, prev_rkv, prev_sm, s_wkv, *rwkv_params)
        return oa, ob, sd, sw

    xs_rows = jnp.concatenate([x_sample.reshape(n_s, D_MODEL), meta], axis=0)
    xp_rows = x_prompt.reshape(nbp * seq_p, D_MODEL)
    n_small = n_s + N_META
    proj_s, proj_tail_s = _norm_matmul(xs_rows, row(norm1[l]), w_cat_t, n_small, 1536, 3, nbs)
    proj_p = _norm_matmul(xp_rows, row(norm1[l]), w_cat_t, 1024, 1536)

    zeros = lambda *s: jnp.zeros(s, F32)
    oa_m, ob_m, sd_m, sw_m = mix(proj_s, n_s, 1, N_META, N_META, 1, zeros(SUB, 3 * W_A),
                                 zeros(SUB, 3 * W_B), zeros(SUB, SM_W),
                                 zeros(1, H_A, HEAD_A, HEAD_A), zeros(1, N_PAIR, LANE, HEAD_B))
    tail = proj_s[n_small - SUB:n_small]
    oa_p, ob_p, sd_p, sw_p = mix(proj_p, 0, nbp, seq_p, 64, 1, tail[:, :3 * W_A],
                                 tail[:, RKV0:SM0], tail[:, SM0:], sd_m, sw_m)
    sh = state_shift[l]
    sh_sm = _small_layout(jnp.zeros((nbs, 1, 2 * H_A), F32), sh[..., 3 * W_B:3 * W_B + W_LORA],
                          sh[..., 3 * W_B + W_LORA:3 * W_B + W_LORA + A_LORA],
                          sh[..., 3 * W_B + W_LORA + A_LORA:])
    tmajor = lambda s: s.transpose(1, 0, 2)
    oa_s, ob_s, sd_s, sw_s = mix(proj_s, 0, nbs, seq_s, seq_s, 8, tmajor(state_conv_qkv[l]),
                                 tmajor(sh[..., :3 * W_B]), tmajor(sh_sm), state_delta[l],
                                 state_wkv[l].reshape(nbs, N_PAIR, LANE, HEAD_B))

    oa_small = jnp.concatenate([oa_s, oa_m], axis=0)
    ob_small = jnp.concatenate([ob_s, ob_m], axis=0)
    x1_s, h2_s = _out_proj(xs_rows, oa_small, ob_small, wo_bf, row(norm2[l]), n_small // 5)
    x1_p, h2_p = _out_proj(xp_rows, oa_p, ob_p, wo_bf, row(norm2[l]), 512)
    act_s, gate_tail_s, gate_last_s = _ffn_in(h2_s, w_ffn_in[l], state_ffn_conv[l].transpose(1, 0, 2), conv_f[l],
                                              n_small, 768, 3, 0, n_s + SUB)
    act_p, tail_p = _ffn_in(h2_p, w_ffn_in[l], gate_last_s[None], conv_f[l], 1024, 768, 3, seq_p // 1024)
    y_s = _ffn_out(act_s, wfo_bf, x1_s, row(norm_f), n_s, 1024, 768)
    y_p = _ffn_out(act_p, wfo_bf, x1_p, row(norm_f), nbp * seq_p, 1024, 768)

    def states(conv_new, last, ffn_new, nb, sd, sw):
        shift_new = jnp.concatenate([last[..., RKV0:SM0], last[..., SM0 + SM_WA:SM0 + SM_WA + W_LORA + A_LORA],
                                     last[..., SM0 + SM_G:SM0 + SM_G + G_LORA]], axis=-1)
        return (sd[None], conv_new[None], sw.reshape(nb, H_B, HEAD_B, HEAD_B)[None], shift_new[None],
                ffn_new[None])

    p3 = proj_p.reshape(nbp, seq_p, P_CAT)
    tail_s = proj_tail_s.transpose(1, 0, 2)
    return ((y_p.reshape(nbp, seq_p, D_MODEL), y_s.reshape(nbs, seq_s, D_MODEL))
            + states(p3[:, seq_p - 3:, :3 * W_A], p3[:, seq_p - 1:, :], tail_p[:, SUB - 2:, :], nbp, sd_p, sw_p)
            + states(tail_s[:, :, :3 * W_A], tail_s[:, 2:, :], gate_tail_s.transpose(1, 0, 2), nbs, sd_s, sw_s))
```

```python
import functools

import jax
import jax.numpy as jnp
from jax import lax
from jax.experimental import pallas as pl
from jax.experimental.pallas import tpu as pltpu

F32 = jnp.float32
BF16 = jnp.bfloat16

D_MODEL = 2048
N_META = 16
W_A = 1024
HEAD_A = 128
H_A = 8
W_B = 1024
HEAD_B = 64
H_B = 16
N_PAIR = H_B // 2
W_LORA = 64
A_LORA = 64
G_LORA = 160
D_FF = 5376
RMS_EPS = 1e-6
GN_EPS = 64e-5
A_PROJ = 4 * W_A + 2 * H_A
B_PROJ = 3 * W_B + W_LORA + A_LORA + G_LORA

QKVZ0 = 0
RKV0 = 4 * W_A
SM0 = RKV0 + 3 * W_B
SM_W = 512
SM_BA = 0
SM_WA = 128
SM_G = 256
P_CAT = SM0 + SM_W

LANE = 128
SUB = 8
VMEM_LIMIT = 48 * 1024 * 1024
VMEM_LIMIT_BIG = 58 * 1024 * 1024
NEG_BIG = -1e30

NT_DIMS = (((1,), (1,)), ((), ()))


def _bdot(a, b):
    return jnp.dot(a.astype(BF16), b.astype(BF16), preferred_element_type=F32)


def _bdot_nt(a, b):
    return lax.dot_general(a.astype(BF16), b.astype(BF16), NT_DIMS, preferred_element_type=F32)


def _bdot_tn(a, b):
    return lax.dot_general(a.astype(BF16), b.astype(BF16), (((0,), (0,)), ((), ())),
                           preferred_element_type=F32)


def _pieces(a, n):
    out = []
    rem = a
    for i in range(n):
        p = rem.astype(BF16)
        out.append(p)
        if i + 1 < n:
            rem = rem - p.astype(F32)
    return out


def _xdot(a, b, n, dims=(((1,), (0,)), ((), ()))):
    bb = b.astype(BF16)
    acc = None
    for p in _pieces(a, n):
        t = lax.dot_general(p, bb, dims, preferred_element_type=F32)
        acc = t if acc is None else acc + t
    return acc


def _xdot_r(a, b, n, dims=(((1,), (0,)), ((), ()))):
    ab = a.astype(BF16)
    acc = None
    for p in _pieces(b, n):
        t = lax.dot_general(ab, p, dims, preferred_element_type=F32)
        acc = t if acc is None else acc + t
    return acc


def _sigmoid(x):
    return 1.0 / (1.0 + jnp.exp(-x))


def _silu(x):
    return x * _sigmoid(x)


def _softplus(x):
    return jnp.maximum(x, 0.0) + jnp.log(1.0 + jnp.exp(-jnp.abs(x)))


def _iota(shape, dim):
    return lax.broadcasted_iota(jnp.int32, shape, dim)


def _shifted(x, prev8, k):
    n = x.shape[0]
    xr = pltpu.roll(x, k, 0)
    pr = pltpu.roll(prev8, k, 0)
    first = jnp.where(_iota((SUB, x.shape[1]), 0) < k, pr, xr[0:SUB])
    if n == SUB:
        return first
    return jnp.concatenate([first, xr[SUB:]], axis=0)


def _repeat_rows(s, reps):
    n, c = s.shape
    return jnp.broadcast_to(s[:, None, :], (n, reps, c)).reshape(n * reps, c)


def _delayed(x, hist, k, nseq):
    if nseq == 1:
        return _shifted(x, hist, k)
    t = _iota(x.shape, 0) & (SUB - 1)
    d = pltpu.roll(x, k, 0)
    for tt in range(k):
        d = jnp.where(t == tt, hist[len(hist) - k + tt], d)
    return d


def _same_seq(shape, c, rows_total):
    if c == rows_total:
        return None
    sh = c.bit_length() - 1
    return (lax.shift_right_logical(_iota(shape, 0), sh)
            == lax.shift_right_logical(_iota(shape, 1) & (rows_total - 1), sh))


class _Packed:
    def __init__(self, r, c):
        self.r = r
        self.c = c
        row = _iota((r, 2 * r), 0)
        col = _iota((r, 2 * r), 1) & (r - 1)
        same = _same_seq((r, 2 * r), c, r)
        self.incl = (row >= col) if same is None else ((row >= col) & same)
        self.strict = (row > col) if same is None else ((row > col) & same)
        self.eye = (row == col).astype(F32)
        self.first = _iota((r, 2 * r), 1) < r
        self.bd_mask = (_iota((2 * r, 2 * r), 0) < r) == (_iota((2 * r, 2 * r), 1) < r)

    def block_diag(self, m):
        return jnp.where(self.bd_mask, jnp.concatenate([m, m], axis=0), 0.0)

    def inverse_many(self, xs):
        ts = [self.eye + x for x in xs]
        ps = list(xs)
        n = 1
        while 2 * n < self.c:
            ps = [_bdot(p, self.block_diag(p)) for p in ps]
            ts = [t + _bdot(p, self.block_diag(t)) for p, t in zip(ps, ts)]
            n *= 2
        return ts


def _seq_masks(r, c):
    row = _iota((r, r), 0)
    col = _iota((r, r), 1)
    same = _same_seq((r, r), c, r)
    if same is None:
        return (row >= col).astype(F32), None
    return ((row >= col) & same).astype(F32), same.astype(F32)


def _seq_tail_rows(x, slab_ref, dst_ref, n_tail, nseq, col0=0):
    for s in range(x.shape[1] // LANE):
        slab_ref[s] = x[:, s * LANE:(s + 1) * LANE]
        for t in range(n_tail):
            dst_ref[t, :, col0 + s * LANE:col0 + (s + 1) * LANE] = (
                slab_ref[s, pl.ds(SUB - n_tail + t, nseq, stride=SUB), :])


def _norm_matmul_kernel(x_ref, g_ref, w_ref, o_ref, *rest, n_tail, nseq):
    tail_ref, h_ref, slab_ref = rest if n_tail else (None,) + rest + (None,)

    @pl.when(pl.program_id(1) == 0)
    def _():
        x = x_ref[...]
        ms = jnp.mean(x * x, axis=-1, keepdims=True)
        h_ref[...] = (x * lax.rsqrt(ms + RMS_EPS) * g_ref[...]).astype(BF16)

    out = lax.dot_general(h_ref[...], w_ref[...], NT_DIMS, preferred_element_type=F32)
    o_ref[...] = out
    if n_tail:
        _seq_tail_rows(out, slab_ref, tail_ref, n_tail, nseq)


def _norm_matmul(x, g, wt, tm, tn, n_tail=0, nseq=0):
    m, k = x.shape
    n = wt.shape[0]
    out_shape = jax.ShapeDtypeStruct((m, n), F32)
    out_specs = pl.BlockSpec((tm, tn), lambda i, j: (i, j))
    if n_tail:
        assert m == tm
        out_shape = (out_shape, jax.ShapeDtypeStruct((n_tail, nseq, n), F32))
        out_specs = (out_specs, pl.BlockSpec((n_tail, nseq, tn), lambda i, j: (0, 0, j)))
    return pl.pallas_call(
        functools.partial(_norm_matmul_kernel, n_tail=n_tail, nseq=nseq),
        out_shape=out_shape,
        grid=(m // tm, n // tn),
        in_specs=[
            pl.BlockSpec((tm, k), lambda i, j: (i, 0)),
            pl.BlockSpec((1, k), lambda i, j: (0, 0)),
            pl.BlockSpec((tn, k), lambda i, j: (j, 0)),
        ],
        out_specs=out_specs,
        scratch_shapes=[pltpu.VMEM((tm, k), BF16)] + ([pltpu.VMEM((tn // LANE, tm, LANE), F32)] if n_tail else []),
        compiler_params=pltpu.CompilerParams(
            dimension_semantics=("parallel", "arbitrary"), vmem_limit_bytes=VMEM_LIMIT_BIG),
        name="norm_matmul",
    )(x, g, wt)


def _out_proj_kernel(x_ref, oa_ref, ob_ref, wt_ref, wb_ref, g_ref, x1_ref, h_ref):
    acc = jnp.dot(oa_ref[...], wt_ref[...], preferred_element_type=F32)
    acc = acc + jnp.dot(ob_ref[...], wb_ref[...], preferred_element_type=F32)
    x1 = x_ref[...] + acc
    x1_ref[...] = x1
    ms = jnp.mean(x1 * x1, axis=-1, keepdims=True)
    h_ref[...] = (x1 * lax.rsqrt(ms + RMS_EPS) * g_ref[...]).astype(BF16)


def _out_proj(x, oa, ob, wo, g, tm):
    m, d = x.shape
    return pl.pallas_call(
        _out_proj_kernel,
        out_shape=(jax.ShapeDtypeStruct((m, d), F32), jax.ShapeDtypeStruct((m, d), BF16)),
        grid=(m // tm,),
        in_specs=[
            pl.BlockSpec((tm, d), lambda i: (i, 0)),
            pl.BlockSpec((tm, W_A), lambda i: (i, 0)),
            pl.BlockSpec((tm, W_B), lambda i: (i, 0)),
            pl.BlockSpec((W_A, d), lambda i: (0, 0)),
            pl.BlockSpec((W_B, d), lambda i: (W_A // W_B, 0)),
            pl.BlockSpec((1, d), lambda i: (0, 0)),
        ],
        out_specs=(pl.BlockSpec((tm, d), lambda i: (i, 0)), pl.BlockSpec((tm, d), lambda i: (i, 0))),
        compiler_params=pltpu.CompilerParams(
            dimension_semantics=("parallel",), vmem_limit_bytes=VMEM_LIMIT),
        name="out_proj",
    )(x, oa, ob, wo, wo, g)


def _ffn_in_kernel(h_ref, wg_ref, wu_ref, prev_ref, cw_ref, act_ref, tail_ref, *rest,
                   tiles_per_seq, state_rows, nsub):
    if tiles_per_seq:
        wgb_ref, wub_ref, carry_ref = rest
    else:
        last_ref, wgb_ref, wub_ref, carry_ref, slab_ref = rest
    m = pl.program_id(1)

    @pl.when(m == 0)
    def _():
        wgb_ref[...] = wg_ref[...].astype(BF16)
        wub_ref[...] = wu_ref[...].astype(BF16)

    if tiles_per_seq:
        @pl.when(m % tiles_per_seq == 0)
        def _():
            carry_ref[...] = prev_ref[...]

    h = h_ref[...]
    n = h.shape[0]
    w = wgb_ref.shape[1] // nsub
    for i in range(nsub):
        cs = slice(i * w, (i + 1) * w)
        gate = jnp.dot(h, wgb_ref[:, cs], preferred_element_type=F32)
        up = jnp.dot(h, wub_ref[:, cs], preferred_element_type=F32)
        if tiles_per_seq:
            prev8 = carry_ref[:, cs]
            d1 = _shifted(gate, prev8, 1)
            d2 = _shifted(gate, prev8, 2)
            carry_ref[:, cs] = gate[n - SUB:n]
            tail_ref[:, cs] = gate[n - SUB:n]
        else:
            nseq = (state_rows - SUB) // SUB
            _seq_tail_rows(gate, slab_ref, tail_ref, 2, nseq, i * w)
            last_ref[:, cs] = gate[n - SUB:n]
            pad = jnp.zeros((n - nseq * SUB, w), F32)
            hist = [jnp.concatenate([_repeat_rows(prev_ref[t, :, cs], SUB), pad], axis=0) for t in range(2)]
            t_in_seq = jnp.where(_iota(gate.shape, 0) < state_rows, _iota(gate.shape, 0) & (SUB - 1), SUB)
            d1 = jnp.where(t_in_seq == 0, hist[1], pltpu.roll(gate, 1, 0))
            d2 = jnp.where(t_in_seq == 0, hist[0], jnp.where(t_in_seq == 1, hist[1], pltpu.roll(gate, 2, 0)))
        y = gate * cw_ref[2:3, cs] + d1 * cw_ref[1:2, cs] + d2 * cw_ref[0:1, cs]
        act_ref[:, cs] = (_silu(y) * up).astype(BF16)


def _ffn_in(h, w, prev, cw, tm, tn, nsub, tiles_per_seq, state_rows=0):
    m, d = h.shape
    nj = D_FF // tn
    nm = m // tm
    if tiles_per_seq:
        nseq = nm // tiles_per_seq
        prev_spec = pl.BlockSpec((None, SUB, tn), lambda j, i: (0, 0, j))
        extra_shapes = (jax.ShapeDtypeStruct((nseq, SUB, D_FF), F32),)
        extra_specs = (pl.BlockSpec((None, SUB, tn), lambda j, i: (i // tiles_per_seq, 0, j)),)
        extra_scratch = []
    else:
        assert nm == 1
        nseq = (state_rows - SUB) // SUB
        prev_spec = pl.BlockSpec((2, nseq, tn), lambda j, i: (0, 0, j))
        extra_shapes = (jax.ShapeDtypeStruct((2, nseq, D_FF), F32), jax.ShapeDtypeStruct((SUB, D_FF), F32))
        extra_specs = (pl.BlockSpec((2, nseq, tn), lambda j, i: (0, 0, j)),
                       pl.BlockSpec((SUB, tn), lambda j, i: (0, j)))
        extra_scratch = [pltpu.VMEM((tn // nsub // LANE, tm, LANE), F32)]
    return pl.pallas_call(
        functools.partial(_ffn_in_kernel, tiles_per_seq=tiles_per_seq, state_rows=state_rows, nsub=nsub),
        out_shape=(jax.ShapeDtypeStruct((m, D_FF), BF16),) + extra_shapes,
        grid=(nj, nm),
        in_specs=[
            pl.BlockSpec((tm, d), lambda j, i: (i, 0)),
            pl.BlockSpec((d, tn), lambda j, i: (0, j)),
            pl.BlockSpec((d, tn), lambda j, i: (0, nj + j)),
            prev_spec,
            pl.BlockSpec((3, tn), lambda j, i: (0, j)),
        ],
        out_specs=(pl.BlockSpec((tm, tn), lambda j, i: (i, j)),) + extra_specs,
        scratch_shapes=[pltpu.VMEM((d, tn), BF16), pltpu.VMEM((d, tn), BF16), pltpu.VMEM((SUB, tn), F32)]
        + extra_scratch,
        compiler_params=pltpu.CompilerParams(
            dimension_semantics=("parallel", "arbitrary"), vmem_limit_bytes=VMEM_LIMIT_BIG),
        name="ffn_in",
    )(h, w, w, prev, cw)


def _ffn_out_kernel(a_ref, w_ref, x_ref, g_ref, o_ref):
    kk = pl.program_id(1)

    @pl.when(kk == 0)
    def _():
        o_ref[...] = x_ref[...]

    o_ref[...] += jnp.dot(a_ref[...], w_ref[...], preferred_element_type=F32)

    @pl.when(kk == pl.num_programs(1) - 1)
    def _():
        x = o_ref[...]
        ms = jnp.mean(x * x, axis=-1, keepdims=True)
        o_ref[...] = x * lax.rsqrt(ms + RMS_EPS) * g_ref[...]


def _ffn_out(act, w, x1, g, m, tm, tk):
    kdim, d = w.shape
    return pl.pallas_call(
        _ffn_out_kernel,
        out_shape=jax.ShapeDtypeStruct((m, d), F32),
        grid=(m // tm, kdim // tk),
        in_specs=[
            pl.BlockSpec((tm, tk), lambda i, k: (i, k)),
            pl.BlockSpec((tk, d), lambda i, k: (k, 0)),
            pl.BlockSpec((tm, d), lambda i, k: (i, 0)),
            pl.BlockSpec((1, d), lambda i, k: (0, 0)),
        ],
        out_specs=pl.BlockSpec((tm, d), lambda i, k: (i, 0)),
        compiler_params=pltpu.CompilerParams(
            dimension_semantics=("parallel", "arbitrary"), vmem_limit_bytes=VMEM_LIMIT),
        name="ffn_out",
    )(act, w, x1, g)


def _delta_kernel(qkvz_ref, ba_ref, prev_ref, s0_ref, cw_ref, alog_ref, dtb_ref, on_ref,
                  o_ref, s_ref, carry_ref, *, C, NB):
    R = NB * C
    ci = pl.program_id(1)

    @pl.when(ci == 0)
    def _():
        if NB == 1:
            carry_ref[...] = prev_ref[...]
        s_ref[...] = s0_ref[...]

    tri, ones_seq = _seq_masks(R, C)

    ba = ba_ref[...]
    beta_full = _sigmoid(ba)
    g_full = -jnp.exp(alog_ref[...]) * _softplus(ba + dtb_ref[...])
    gc_full = _xdot_r(tri, g_full, 3)
    gtot_full = gc_full[C - 1:C, :] if NB == 1 else _xdot_r(ones_seq, g_full, 3)

    def conv_silu(c0):
        x = qkvz_ref[:, c0:c0 + LANE]
        if NB == 1:
            hist = carry_ref[:, c0:c0 + LANE]
        else:
            hist = [_repeat_rows(prev_ref[j, :, c0:c0 + LANE], SUB) for j in range(3)]
        y = x * cw_ref[3:4, c0:c0 + LANE]
        for k in (1, 2, 3):
            y = y + _delayed(x, hist, k, NB) * cw_ref[3 - k:4 - k, c0:c0 + LANE]
        return _silu(y)

    ones_ll = jnp.ones((LANE, LANE), BF16)

    def lane_sum(x):
        return jnp.dot(x.astype(BF16), ones_ll, preferred_element_type=F32)

    def l2n_many(xs):
        sums = [lane_sum(x * x) for x in xs]
        return [x * lax.rsqrt(sq + 1e-6) for x, sq in zip(xs, sums)]

    def seq_rows(x, n):
        return x[n * C:(n + 1) * C]

    heads = range(H_A)
    seqs = range(NB)
    q = [x * (HEAD_A ** -0.5) for x in l2n_many([conv_silu(h * HEAD_A) for h in heads])]
    k = l2n_many([conv_silu(W_A + h * HEAD_A) for h in heads])
    v = [conv_silu(2 * W_A + h * HEAD_A) for h in heads]
    bcol = [beta_full[:, h:h + 1] for h in heads]
    gcol = [gc_full[:, H_A + h:H_A + h + 1] for h in heads]
    gtot = [gtot_full[:, H_A + h:H_A + h + 1] for h in heads]
    eg = [jnp.exp(gcol[h]) for h in heads]
    kb = [k[h] * bcol[h] for h in heads]

    pk = _Packed(R, C)
    hpairs = range(H_A // 2)
    zc = jnp.zeros((R, HEAD_A), F32)
    row2 = _iota((2 * R, LANE), 0)
    lane2 = _iota((2 * R, LANE), 1)
    ones_cl = jnp.ones((R, LANE), F32)
    gc2 = jnp.concatenate([gc_full, gc_full], axis=0)
    kq, gamma = [], []
    for hp in hpairs:
        h0, h1 = 2 * hp, 2 * hp + 1
        lhs = jnp.concatenate([jnp.concatenate([kb[h0], kb[h1]], axis=1),
                               jnp.concatenate([q[h0], q[h1]], axis=1)], axis=0)
        rk = jnp.concatenate([jnp.concatenate([k[h0], zc], axis=1),
                              jnp.concatenate([zc, k[h1]], axis=1)], axis=0)
        kq.append(_bdot_nt(lhs, rk))
        sel = lane2 == jnp.where(row2 < R, H_A + h0, H_A + h1)
        grow = _xdot_r(ones_cl, jnp.where(sel, gc2, 0.0), 3, NT_DIMS)
        gcol_p = jnp.where(pk.first, gcol[h0], gcol[h1])
        gamma.append(jnp.exp(jnp.where(pk.incl, gcol_p - grow, NEG_BIG)))
    tinv = pk.inverse_many([-jnp.where(pk.strict, kq[hp][0:R] * gamma[hp], 0.0) for hp in hpairs])
    uw = []
    for hp in hpairs:
        h0, h1 = 2 * hp, 2 * hp + 1
        z2 = jnp.zeros((R, 2 * HEAD_A), F32)
        rhs = jnp.concatenate([jnp.concatenate([v[h0] * bcol[h0], kb[h0] * eg[h0], z2], axis=1),
                               jnp.concatenate([z2, v[h1] * bcol[h1], kb[h1] * eg[h1]], axis=1)], axis=0)
        uw.append(_bdot(tinv[hp], rhs))
    u = [uw[h // 2][:, (h % 2) * 2 * HEAD_A:(h % 2) * 2 * HEAD_A + HEAD_A] for h in heads]
    w = [uw[h // 2][:, (h % 2) * 2 * HEAD_A + HEAD_A:(h % 2 + 1) * 2 * HEAD_A] for h in heads]
    qd = [q[h] * eg[h] for h in heads]
    s = [[s_ref[n, h] for h in heads] for n in seqs]
    wqs = [[_bdot(jnp.concatenate([seq_rows(w[h], n), seq_rows(qd[h], n)], axis=0), s[n][h]) for h in heads]
           for n in seqs]
    ws = [jnp.concatenate([wqs[n][h][0:C] for n in seqs], axis=0) if NB > 1 else wqs[0][h][0:C] for h in heads]
    qs = [jnp.concatenate([wqs[n][h][C:2 * C] for n in seqs], axis=0) if NB > 1 else wqs[0][h][C:2 * C]
          for h in heads]
    v_new = [u[h] - ws[h] for h in heads]
    o = []
    for hp in hpairs:
        h0, h1 = 2 * hp, 2 * hp + 1
        vn_bd = jnp.concatenate([jnp.concatenate([v_new[h0], zc], axis=1),
                                 jnp.concatenate([zc, v_new[h1]], axis=1)], axis=0)
        op = _bdot(kq[hp][R:2 * R] * gamma[hp], vn_bd)
        o.append(qs[h0] + op[:, :HEAD_A])
        o.append(qs[h1] + op[:, HEAD_A:])
    kd = [k[h] * jnp.exp(gtot[h] - gcol[h]) for h in heads]
    for n in seqs:
        for h in heads:
            glast = gtot[h] if NB == 1 else gtot[h][n * C:n * C + 1]
            s_ref[n, h] = s[n][h] * jnp.exp(glast) + _bdot_tn(seq_rows(kd[h], n), seq_rows(v_new[h], n))
    osq = [lane_sum(o[h] * o[h]) for h in heads]
    for h in heads:
        z = qkvz_ref[:, 3 * W_A + h * HEAD_A:3 * W_A + (h + 1) * HEAD_A]
        oh = o[h] * lax.rsqrt(osq[h] * (1.0 / HEAD_A) + RMS_EPS)
        o_ref[:, h * HEAD_A:(h + 1) * HEAD_A] = (oh * on_ref[...] * _silu(z)).astype(BF16)

    if NB == 1:
        carry_ref[...] = qkvz_ref[C - SUB:C, 0:3 * W_A]


def _delta_mixer(proj, row0, nb, seq, C, NB, prev, s0, cw, alog_row, dtb_row, onorm):
    nch = seq // C
    assert NB == 1 or (nch == 1 and C == SUB)
    R = NB * C
    blk0 = row0 // R
    bcast = s0.shape[0] == 1
    bsel = (lambda b: 0) if bcast else (lambda b: b)
    return pl.pallas_call(
        functools.partial(_delta_kernel, C=C, NB=NB),
        out_shape=(jax.ShapeDtypeStruct((nb * seq, W_A), BF16),
                   jax.ShapeDtypeStruct((nb, H_A, HEAD_A, HEAD_A), F32)),
        grid=(nb // NB, nch),
        in_specs=[
            pl.BlockSpec((R, 4 * W_A), lambda b, c: (blk0 + b * nch + c, 0)),
            pl.BlockSpec((R, LANE), lambda b, c: (blk0 + b * nch + c, SM0 // LANE)),
            (pl.BlockSpec((SUB, 3 * W_A), lambda b, c: (0, 0)) if NB == 1
             else pl.BlockSpec((3, NB, 3 * W_A), lambda b, c: (0, b, 0))),
            pl.BlockSpec((NB, H_A, HEAD_A, HEAD_A), lambda b, c: (bsel(b), 0, 0, 0)),
            pl.BlockSpec((4, 3 * W_A), lambda b, c: (0, 0)),
            pl.BlockSpec((1, LANE), lambda b, c: (0, 0)),
            pl.BlockSpec((1, LANE), lambda b, c: (0, 0)),
            pl.BlockSpec((1, HEAD_A), lambda b, c: (0, 0)),
        ],
        out_specs=(pl.BlockSpec((R, W_A), lambda b, c: (b * nch + c, 0)),
                   pl.BlockSpec((NB, H_A, HEAD_A, HEAD_A), lambda b, c: (b, 0, 0, 0))),
        scratch_shapes=[pltpu.VMEM((SUB, 3 * W_A), F32)],
        compiler_params=pltpu.CompilerParams(
            dimension_semantics=("parallel", "arbitrary"), vmem_limit_bytes=VMEM_LIMIT),
        name="delta_mixer",
    )(proj, proj, prev, s0, cw, alog_row, dtb_row, onorm)


def _delta_seq_kernel(qkv0_ref, ba0_ref, qkvn_ref, ban_ref, z_ref, prev_ref, s0_ref, cw_ref, alog_ref, dtb_ref,
                      on_ref, o_ref, s_ref, carry_ref, pre_ref, small_ref, gtot_ref, *, C):
    ci = pl.program_id(1)
    heads = range(H_A)
    hpairs = range(H_A // 2)
    tri, _ = _seq_masks(C, C)
    ones_ll = jnp.ones((LANE, LANE), BF16)

    def lane_sum(x):
        return jnp.dot(x.astype(BF16), ones_ll, preferred_element_type=F32)

    def preamble(qkv_ref, ba_ref, hist_ref, slot):
        ba = ba_ref[...]
        g_full = -jnp.exp(alog_ref[...]) * _softplus(ba + dtb_ref[...])
        gc_full = _xdot_r(tri, g_full, 3)
        small_ref[slot, 0] = _sigmoid(ba)
        small_ref[slot, 1] = gc_full
        gtot_ref[slot] = jnp.broadcast_to(gc_full[C - 1:C, :], (SUB, LANE))
        yield
        for part in range(3):
            xs = []
            for h in heads:
                c0 = part * W_A + h * HEAD_A
                x = qkv_ref[:, c0:c0 + LANE]
                hist = hist_ref[:, c0:c0 + LANE]
                y = x * cw_ref[3:4, c0:c0 + LANE]
                for k in (1, 2, 3):
                    y = y + _shifted(x, hist, k) * cw_ref[3 - k:4 - k, c0:c0 + LANE]
                xs.append(_silu(y))
                yield
            if part < 2:
                sums = [lane_sum(x * x) for x in xs]
                yield
                scale = HEAD_A ** -0.5 if part == 0 else 1.0
                xs = [x * (lax.rsqrt(sq + 1e-6) * scale) for x, sq in zip(xs, sums)]
            for h in heads:
                pre_ref[slot, part, :, h * HEAD_A:(h + 1) * HEAD_A] = xs[h]
            yield
        carry_ref[...] = qkv_ref[C - SUB:C, :]

    @pl.when(ci == 0)
    def _():
        for _ in preamble(qkv0_ref, ba0_ref, prev_ref, 0):
            pass
        s_ref[...] = s0_ref[...]

    slot = ci % 2
    ahead = preamble(qkvn_ref, ban_ref, carry_ref, 1 - slot)
    calls = [0]

    def tick():
        calls[0] += 1
        if calls[0] % 3 == 0:
            next(ahead, None)

    def each(fn, items):
        out = {}
        for it in items:
            out[it] = fn(it)
            tick()
        return out

    beta_full = small_ref[slot, 0]
    gc_full = small_ref[slot, 1]
    q = {h: pre_ref[slot, 0, :, h * HEAD_A:(h + 1) * HEAD_A] for h in heads}
    k = {h: pre_ref[slot, 1, :, h * HEAD_A:(h + 1) * HEAD_A] for h in heads}
    v = {h: pre_ref[slot, 2, :, h * HEAD_A:(h + 1) * HEAD_A] for h in heads}
    bcol = {h: beta_full[:, h:h + 1] for h in heads}
    gcol = {h: gc_full[:, H_A + h:H_A + h + 1] for h in heads}
    gtot_row = gtot_ref[slot]
    gtot = {h: gtot_row[0:1, H_A + h:H_A + h + 1] for h in heads}
    eg = {h: jnp.exp(gcol[h]) for h in heads}
    kb = {h: k[h] * bcol[h] for h in heads}

    pk = _Packed(C, C)
    zc = jnp.zeros((C, HEAD_A), F32)
    row2 = _iota((2 * C, LANE), 0)
    lane2 = _iota((2 * C, LANE), 1)
    ones_cl = jnp.ones((C, LANE), F32)
    gc2 = jnp.concatenate([gc_full, gc_full], axis=0)

    def kq_of(hp):
        h0, h1 = 2 * hp, 2 * hp + 1
        lhs = jnp.concatenate([jnp.concatenate([kb[h0], kb[h1]], axis=1),
                               jnp.concatenate([q[h0], q[h1]], axis=1)], axis=0)
        rk = jnp.concatenate([jnp.concatenate([k[h0], zc], axis=1),
                              jnp.concatenate([zc, k[h1]], axis=1)], axis=0)
        return _bdot_nt(lhs, rk)

    def gamma_of(hp):
        h0, h1 = 2 * hp, 2 * hp + 1
        sel = lane2 == jnp.where(row2 < C, H_A + h0, H_A + h1)
        grow = _xdot_r(ones_cl, jnp.where(sel, gc2, 0.0), 3, NT_DIMS)
        gcol_p = jnp.where(pk.first, gcol[h0], gcol[h1])
        return jnp.exp(jnp.where(pk.incl, gcol_p - grow, NEG_BIG))

    kq = each(kq_of, hpairs)
    gamma = each(gamma_of, hpairs)
    ps = {hp: -jnp.where(pk.strict, kq[hp][0:C] * gamma[hp], 0.0) for hp in hpairs}
    ts = {hp: pk.eye + ps[hp] for hp in hpairs}
    n = 1
    while 2 * n < C:
        ps = each(lambda hp: _bdot(ps[hp], pk.block_diag(ps[hp])), hpairs)
        ts = each(lambda hp: ts[hp] + _bdot(ps[hp], pk.block_diag(ts[hp])), hpairs)
        n *= 2

    def uw_of(hp):
        h0, h1 = 2 * hp, 2 * hp + 1
        z2 = jnp.zeros((C, 2 * HEAD_A), F32)
        rhs = jnp.concatenate([jnp.concatenate([v[h0] * bcol[h0], kb[h0] * eg[h0], z2], axis=1),
                               jnp.concatenate([z2, v[h1] * bcol[h1], kb[h1] * eg[h1]], axis=1)], axis=0)
        return _bdot(ts[hp], rhs)

    uw = each(uw_of, hpairs)
    u = {h: uw[h // 2][:, (h % 2) * 2 * HEAD_A:(h % 2) * 2 * HEAD_A + HEAD_A] for h in heads}
    w = {h: uw[h // 2][:, (h % 2) * 2 * HEAD_A + HEAD_A:(h % 2 + 1) * 2 * HEAD_A] for h in heads}
    s = {h: s_ref[0, h] for h in heads}
    wqs = each(lambda h: _bdot(jnp.concatenate([w[h], q[h] * eg[h]], axis=0), s[h]), heads)
    v_new = {h: u[h] - wqs[h][0:C] for h in heads}

    def o_of(hp):
        h0, h1 = 2 * hp, 2 * hp + 1
        vn_bd = jnp.concatenate([jnp.concatenate([v_new[h0], zc], axis=1),
                                 jnp.concatenate([zc, v_new[h1]], axis=1)], axis=0)
        return _bdot(kq[hp][C:2 * C] * gamma[hp], vn_bd)

    op = each(o_of, hpairs)
    o = {h: wqs[h][C:2 * C] + op[h // 2][:, (h % 2) * HEAD_A:(h % 2 + 1) * HEAD_A] for h in heads}
    for h in heads:
        kd = k[h] * jnp.exp(gtot[h] - gcol[h])
        s_ref[0, h] = s[h] * jnp.exp(gtot[h]) + _bdot_tn(kd, v_new[h])
        tick()
    osq = each(lambda h: lane_sum(o[h] * o[h]), heads)
    for h in heads:
        z = z_ref[:, h * HEAD_A:(h + 1) * HEAD_A]
        oh = o[h] * lax.rsqrt(osq[h] * (1.0 / HEAD_A) + RMS_EPS)
        o_ref[:, h * HEAD_A:(h + 1) * HEAD_A] = (oh * on_ref[...] * _silu(z)).astype(BF16)
        tick()
    for _ in ahead:
        pass


def _delta_seq_mixer(proj, nb, seq, C, prev, s0, cw, alog_row, dtb_row, onorm):
    nch = seq // C
    nxt = lambda b, c: b * nch + jnp.minimum(c + 1, nch - 1)
    return pl.pallas_call(
        functools.partial(_delta_seq_kernel, C=C),
        out_shape=(jax.ShapeDtypeStruct((nb * seq, W_A), BF16),
                   jax.ShapeDtypeStruct((nb, H_A, HEAD_A, HEAD_A), F32)),
        grid=(nb, nch),
        in_specs=[
            pl.BlockSpec((C, 3 * W_A), lambda b, c: (b * nch, 0)),
            pl.BlockSpec((C, LANE), lambda b, c: (b * nch, SM0 // LANE)),
            pl.BlockSpec((C, 3 * W_A), lambda b, c: (nxt(b, c), 0)),
            pl.BlockSpec((C, LANE), lambda b, c: (nxt(b, c), SM0 // LANE)),
            pl.BlockSpec((C, W_A), lambda b, c: (b * nch + c, 3)),
            pl.BlockSpec((SUB, 3 * W_A), lambda b, c: (0, 0)),
            pl.BlockSpec((1, H_A, HEAD_A, HEAD_A), lambda b, c: (0, 0, 0, 0)),
            pl.BlockSpec((4, 3 * W_A), lambda b, c: (0, 0)),
            pl.BlockSpec((1, LANE), lambda b, c: (0, 0)),
            pl.BlockSpec((1, LANE), lambda b, c: (0, 0)),
            pl.BlockSpec((1, HEAD_A), lambda b, c: (0, 0)),
        ],
        out_specs=(pl.BlockSpec((C, W_A), lambda b, c: (b * nch + c, 0)),
                   pl.BlockSpec((1, H_A, HEAD_A, HEAD_A), lambda b, c: (b, 0, 0, 0))),
        scratch_shapes=[pltpu.VMEM((SUB, 3 * W_A), F32), pltpu.VMEM((2, 3, C, W_A), F32),
                        pltpu.VMEM((2, 2, C, LANE), F32), pltpu.VMEM((2, SUB, LANE), F32)],
        compiler_params=pltpu.CompilerParams(
            dimension_semantics=("parallel", "arbitrary"), vmem_limit_bytes=VMEM_LIMIT),
        name="delta_seq_mixer",
    )(proj, proj, proj, proj, proj, prev, s0, cw, alog_row, dtb_row, onorm)


def _delta_multi_kernel(qkv0_ref, ba0_ref, qkvn_ref, ban_ref, z_ref, prev_ref, s0_ref, cw_ref, alog_ref, dtb_ref,
                        on_ref, o_ref, s_ref, carry_ref, pre_ref, small_ref, gtot_ref, *, C, NS):
    ci = pl.program_id(1)
    heads = [(s, h) for s in range(NS) for h in range(H_A)]
    hpairs = [(s, hp) for s in range(NS) for hp in range(H_A // 2)]
    tri, _ = _seq_masks(C, C)
    ones_ll = jnp.ones((LANE, LANE), BF16)

    def lane_sum(x):
        return jnp.dot(x.astype(BF16), ones_ll, preferred_element_type=F32)

    def preamble(qkv_ref, ba_ref, hist_of, slot):
        for s in range(NS):
            ba = ba_ref[s]
            g_full = -jnp.exp(alog_ref[...]) * _softplus(ba + dtb_ref[...])
            gc_full = _xdot_r(tri, g_full, 3)
            small_ref[slot, s, 0] = _sigmoid(ba)
            small_ref[slot, s, 1] = gc_full
            gtot_ref[slot, s] = jnp.broadcast_to(gc_full[C - 1:C, :], (SUB, LANE))
            yield
        for part in range(3):
            xs = {}
            for s, h in heads:
                c0 = part * W_A + h * HEAD_A
                x = qkv_ref[s, :, c0:c0 + LANE]
                hist = hist_of(s, c0)
                y = x * cw_ref[3:4, c0:c0 + LANE]
                for k in (1, 2, 3):
                    y = y + _shifted(x, hist, k) * cw_ref[3 - k:4 - k, c0:c0 + LANE]
                xs[s, h] = _silu(y)
                yield
            if part < 2:
                sums = {key: lane_sum(x * x) for key, x in xs.items()}
                yield
                scale = HEAD_A ** -0.5 if part == 0 else 1.0
                xs = {key: x * (lax.rsqrt(sums[key] + 1e-6) * scale) for key, x in xs.items()}
            for s, h in heads:
                pre_ref[slot, s, part, :, h * HEAD_A:(h + 1) * HEAD_A] = xs[s, h]
            yield
        for s in range(NS):
            carry_ref[s] = qkv_ref[s, C - SUB:C, :]

    @pl.when(ci == 0)
    def _():
        for _ in preamble(qkv0_ref, ba0_ref, lambda s, c0: prev_ref[:, c0:c0 + LANE], 0):
            pass
        for s in range(NS):
            s_ref[s, 0] = s0_ref[...]

    slot = ci % 2
    ahead = preamble(qkvn_ref, ban_ref, lambda s, c0: carry_ref[s, :, c0:c0 + LANE], 1 - slot)
    calls = [0]

    def tick():
        calls[0] += 1
        if calls[0] % 3 == 0:
            next(ahead, None)

    def each(fn, items):
        out = {}
        for it in items:
            out[it] = fn(*it)
            tick()
        return out

    beta_full = {s: small_ref[slot, s, 0] for s in range(NS)}
    gc_full = {s: small_ref[slot, s, 1] for s in range(NS)}
    gtot_row = {s: gtot_ref[slot, s] for s in range(NS)}
    q = {(s, h): pre_ref[slot, s, 0, :, h * HEAD_A:(h + 1) * HEAD_A] for s, h in heads}
    k = {(s, h): pre_ref[slot, s, 1, :, h * HEAD_A:(h + 1) * HEAD_A] for s, h in heads}
    v = {(s, h): pre_ref[slot, s, 2, :, h * HEAD_A:(h + 1) * HEAD_A] for s, h in heads}
    bcol = {(s, h): beta_full[s][:, h:h + 1] for s, h in heads}
    gcol = {(s, h): gc_full[s][:, H_A + h:H_A + h + 1] for s, h in heads}
    gtot = {(s, h): gtot_row[s][0:1, H_A + h:H_A + h + 1] for s, h in heads}
    eg = {key: jnp.exp(gcol[key]) for key in heads}
    kb = {key: k[key] * bcol[key] for key in heads}

    pk = _Packed(C, C)
    zc = jnp.zeros((C, HEAD_A), F32)
    row2 = _iota((2 * C, LANE), 0)
    lane2 = _iota((2 * C, LANE), 1)
    ones_cl = jnp.ones((C, LANE), F32)

    def kq_of(s, hp):
        a, b = (s, 2 * hp), (s, 2 * hp + 1)
        lhs = jnp.concatenate([jnp.concatenate([kb[a], kb[b]], axis=1),
                               jnp.concatenate([q[a], q[b]], axis=1)], axis=0)
        rk = jnp.concatenate([jnp.concatenate([k[a], zc], axis=1),
                              jnp.concatenate([zc, k[b]], axis=1)], axis=0)
        return _bdot_nt(lhs, rk)

    def gamma_of(s, hp):
        a, b = (s, 2 * hp), (s, 2 * hp + 1)
        sel = lane2 == jnp.where(row2 < C, H_A + 2 * hp, H_A + 2 * hp + 1)
        gc2 = jnp.concatenate([gc_full[s], gc_full[s]], axis=0)
        grow = _xdot_r(ones_cl, jnp.where(sel, gc2, 0.0), 3, NT_DIMS)
        gcol_p = jnp.where(pk.first, gcol[a], gcol[b])
        return jnp.exp(jnp.where(pk.incl, gcol_p - grow, NEG_BIG))

    kq = each(kq_of, hpairs)
    gamma = each(gamma_of, hpairs)
    ps = {key: -jnp.where(pk.strict, kq[key][0:C] * gamma[key], 0.0) for key in hpairs}
    ts = {key: pk.eye + ps[key] for key in hpairs}
    n = 1
    while 2 * n < C:
        ps = each(lambda s, hp: _bdot(ps[s, hp], pk.block_diag(ps[s, hp])), hpairs)
        ts = each(lambda s, hp: ts[s, hp] + _bdot(ps[s, hp], pk.block_diag(ts[s, hp])), hpairs)
        n *= 2

    def uw_of(s, hp):
        a, b = (s, 2 * hp), (s, 2 * hp + 1)
        z2 = jnp.zeros((C, 2 * HEAD_A), F32)
        rhs = jnp.concatenate([jnp.concatenate([v[a] * bcol[a], kb[a] * eg[a], z2], axis=1),
                               jnp.concatenate([z2, v[b] * bcol[b], kb[b] * eg[b]], axis=1)], axis=0)
        return _bdot(ts[s, hp], rhs)

    uw = each(uw_of, hpairs)
    u = {(s, h): uw[s, h // 2][:, (h % 2) * 2 * HEAD_A:(h % 2) * 2 * HEAD_A + HEAD_A] for s, h in heads}
    w = {(s, h): uw[s, h // 2][:, (h % 2) * 2 * HEAD_A + HEAD_A:(h % 2 + 1) * 2 * HEAD_A] for s, h in heads}
    st = {(s, h): s_ref[s, 0, h] for s, h in heads}
    wqs = each(lambda s, h: _bdot(jnp.concatenate([w[s, h], q[s, h] * eg[s, h]], axis=0), st[s, h]), heads)
    v_new = {key: u[key] - wqs[key][0:C] for key in heads}

    def o_of(s, hp):
        a, b = (s, 2 * hp), (s, 2 * hp + 1)
        vn_bd = jnp.concatenate([jnp.concatenate([v_new[a], zc], axis=1),
                                 jnp.concatenate([zc, v_new[b]], axis=1)], axis=0)
        return _bdot(kq[s, hp][C:2 * C] * gamma[s, hp], vn_bd)

    op = each(o_of, hpairs)
    o = {(s, h): wqs[s, h][C:2 * C] + op[s, h // 2][:, (h % 2) * HEAD_A:(h % 2 + 1) * HEAD_A] for s, h in heads}
    for s, h in heads:
        kd = k[s, h] * jnp.exp(gtot[s, h] - gcol[s, h])
        s_ref[s, 0, h] = st[s, h] * jnp.exp(gtot[s, h]) + _bdot_tn(kd, v_new[s, h])
        tick()
    osq = each(lambda s, h: lane_sum(o[s, h] * o[s, h]), heads)
    for s, h in heads:
        z = z_ref[s, :, h * HEAD_A:(h + 1) * HEAD_A]
        oh = o[s, h] * lax.rsqrt(osq[s, h] * (1.0 / HEAD_A) + RMS_EPS)
        o_ref[s, :, h * HEAD_A:(h + 1) * HEAD_A] = (oh * on_ref[...] * _silu(z)).astype(BF16)
        tick()
    for _ in ahead:
        pass


def _delta_multi_mixer(proj, nb, seq, C, NS, prev, s0, cw, alog_row, dtb_row, onorm):
    nch = seq // C
    ng = nb // NS
    proj3 = proj.reshape(NS, ng * seq, P_CAT)
    nxt = lambda i, c: i * nch + jnp.minimum(c + 1, nch - 1)
    oa, sd = pl.pallas_call(
        functools.partial(_delta_multi_kernel, C=C, NS=NS),
        out_shape=(jax.ShapeDtypeStruct((NS, ng * seq, W_A), BF16),
                   jax.ShapeDtypeStruct((NS, ng, H_A, HEAD_A, HEAD_A), F32)),
        grid=(ng, nch),
        in_specs=[
            pl.BlockSpec((NS, C, 3 * W_A), lambda i, c: (0, i * nch, 0)),
            pl.BlockSpec((NS, C, LANE), lambda i, c: (0, i * nch, SM0 // LANE)),
            pl.BlockSpec((NS, C, 3 * W_A), lambda i, c: (0, nxt(i, c), 0)),
            pl.BlockSpec((NS, C, LANE), lambda i, c: (0, nxt(i, c), SM0 // LANE)),
            pl.BlockSpec((NS, C, W_A), lambda i, c: (0, i * nch + c, 3)),
            pl.BlockSpec((SUB, 3 * W_A), lambda i, c: (0, 0)),
            pl.BlockSpec((None, H_A, HEAD_A, HEAD_A), lambda i, c: (0, 0, 0, 0)),
            pl.BlockSpec((4, 3 * W_A), lambda i, c: (0, 0)),
            pl.BlockSpec((1, LANE), lambda i, c: (0, 0)),
            pl.BlockSpec((1, LANE), lambda i, c: (0, 0)),
            pl.BlockSpec((1, HEAD_A), lambda i, c: (0, 0)),
        ],
        out_specs=(pl.BlockSpec((NS, C, W_A), lambda i, c: (0, i * nch + c, 0)),
                   pl.BlockSpec((NS, 1, H_A, HEAD_A, HEAD_A), lambda i, c: (0, i, 0, 0, 0))),
        scratch_shapes=[pltpu.VMEM((NS, SUB, 3 * W_A), F32), pltpu.VMEM((2, NS, 3, C, W_A), F32),
                        pltpu.VMEM((2, NS, 2, C, LANE), F32), pltpu.VMEM((2, NS, SUB, LANE), F32)],
        compiler_params=pltpu.CompilerParams(
            dimension_semantics=("parallel", "arbitrary"), vmem_limit_bytes=VMEM_LIMIT),
        name="delta_multi_mixer",
    )(proj3, proj3, proj3, proj3, proj3, prev, s0, cw, alog_row, dtb_row, onorm)
    return oa.reshape(nb * seq, W_A), sd.reshape(nb, H_A, HEAD_A, HEAD_A)


def _rwkv_kernel(r_ref, k_ref, v_ref, sm_ref, pr_ref, pk_ref, pv_ref, psm_ref, s0_ref,
                 mur_ref, muk_ref, muv_ref, musm_ref, w2_ref, a2_ref, g2_ref,
                 w0_ref, a0_ref, kk_ref, ka_ref, rk_ref, lnw_ref, lnb_ref,
                 o_ref, s_ref, cr_ref, ck_ref, cv_ref, csm_ref, st_ref, *, C, NB, G):
    R = NB * C
    ci = pl.program_id(2)
    nch = pl.num_programs(2)
    seqs = range(NB)
    pairs = range(G)

    r2 = _iota((LANE, LANE), 0)
    c2 = _iota((LANE, LANE), 1)
    same_head = (r2 < HEAD_B) == (c2 < HEAD_B)
    ones_bd = same_head.astype(F32)
    spread = (_iota((HEAD_B, LANE), 0) == (_iota((HEAD_B, LANE), 1) & (HEAD_B - 1))).astype(F32)
    gather = ((_iota((LANE, HEAD_B), 0) & (HEAD_B - 1)) == _iota((LANE, HEAD_B), 1)).astype(F32)

    @pl.when(ci == 0)
    def _():
        if NB == 1:
            cr_ref[...] = pr_ref[...]
            ck_ref[...] = pk_ref[...]
            cv_ref[...] = pv_ref[...]
            csm_ref[...] = psm_ref[...]
        for n in seqs:
            for p in pairs:
                st_ref[n * G + p] = jnp.where(same_head, _xdot(s0_ref[n, p], spread, 3), 0.0)

    hr_ref, hk_ref, hv_ref, hsm_ref = ((cr_ref, ck_ref, cv_ref, csm_ref) if NB == 1
                                       else (pr_ref, pk_ref, pv_ref, psm_ref))
    tri, ones_seq = _seq_masks(R, C)

    def lerp(x, hist_ref, mu):
        hist = hist_ref[...] if NB == 1 else [_repeat_rows(hist_ref[0], SUB)]
        return x + (_delayed(x, hist, 1, NB) - x) * mu

    sm = sm_ref[...]
    xs = lerp(sm, hsm_ref, musm_ref[...])
    slab_wa = xs[:, SM_WA:SM_WA + LANE]
    slab_g = xs[:, SM_G:SM_G + 2 * LANE]

    def col(x, p):
        return x[:, p * LANE:(p + 1) * LANE]

    def to_rows(x):
        return jnp.concatenate([col(x, p) for p in pairs], axis=0)

    def to_cols(y):
        return jnp.concatenate([y[p * R:(p + 1) * R] for p in pairs], axis=1)

    def seq_rows(x, n):
        return x[n * C:(n + 1) * C]

    xr = lerp(r_ref[...], hr_ref, mur_ref[...])
    xk = lerp(k_ref[...], hk_ref, muk_ref[...])
    xv = lerp(v_ref[...], hv_ref, muv_ref[...])
    wlog = -_softplus(-(w0_ref[...] + _bdot(jnp.tanh(slab_wa), w2_ref[...]))) - 0.5
    ld = -jnp.exp(wlog)
    aa = _sigmoid(a0_ref[...] + _bdot(slab_wa, a2_ref[...]))
    gate = _bdot(_sigmoid(slab_g), g2_ref[...])
    kkr = xk * kk_ref[...]
    kkn = kkr * lax.rsqrt(to_cols(_bdot(to_rows(kkr * kkr), ones_bd)) + 1e-6)
    k2 = xk * (1.0 + (aa - 1.0) * ka_ref[...])
    lp = _xdot_r(tri, ld, 3)
    lp_tot = lp[C - 1:C, :] if NB == 1 else _xdot_r(ones_seq, ld, 3)
    e_neg = jnp.exp(-lp)
    e_rem = jnp.exp(lp_tot - lp)
    at = -kkn * jnp.exp(lp - ld)
    rt = xr * jnp.exp(lp)
    kb = kkn * aa
    bt = kb * e_neg
    kt = k2 * e_neg
    bhat = kb * e_rem
    khat = k2 * e_rem
    p_tot = jnp.exp(lp_tot)

    s = [[st_ref[n * G + p] for p in pairs] for n in seqs]
    atp = [col(at, p) for p in pairs]
    rtp = [col(rt, p) for p in pairs]
    ars = [[_bdot_nt(jnp.concatenate([seq_rows(atp[p], n), seq_rows(rtp[p], n)], axis=0), s[n][p])
            for p in pairs] for n in seqs]
    if NB == 1:
        x_state = [ars[0][p][0:C] for p in pairs]
        o_state = [ars[0][p][C:2 * C] for p in pairs]
    else:
        x_state = [jnp.concatenate([ars[n][p][0:C] for n in seqs], axis=0) for p in pairs]
        o_state = [jnp.concatenate([ars[n][p][C:2 * C] for n in seqs], axis=0) for p in pairs]
    pk = _Packed(R, C)
    stack_mask = (_iota((2 * R, LANE), 0) < R) == (_iota((2 * R, LANE), 1) < HEAD_B)

    def by_head(x):
        return jnp.where(stack_mask, jnp.concatenate([x, x], axis=0), 0.0)

    ar = [jnp.concatenate([atp[p], rtp[p]], axis=0) for p in pairs]
    ab = [_bdot_nt(ar[p], by_head(col(bt, p))) for p in pairs]
    ak = [_bdot_nt(ar[p], by_head(col(kt, p))) for p in pairs]
    tinv = pk.inverse_many([jnp.where(pk.strict, m[0:R], 0.0) for m in ab])
    v_bh = [by_head(col(xv, p)) for p in pairs]
    y = [x_state[p] + _bdot(jnp.where(pk.strict, ak[p][0:R], 0.0), v_bh[p]) for p in pairs]
    u = [_bdot(tinv[p], by_head(y[p])) for p in pairs]
    o = [o_state[p] + _bdot(jnp.where(pk.incl, ab[p][R:2 * R], 0.0), by_head(u[p]))
         + _bdot(jnp.where(pk.incl, ak[p][R:2 * R], 0.0), v_bh[p]) for p in pairs]
    for n in seqs:
        for p in pairs:
            uv = jnp.concatenate([seq_rows(u[p], n), seq_rows(col(xv, p), n)], axis=0)
            bkhat = jnp.concatenate([seq_rows(col(bhat, p), n), seq_rows(col(khat, p), n)], axis=0)
            decay = col(p_tot, p) if NB == 1 else col(p_tot, p)[n * C:n * C + 1]
            s_new = s[n][p] * decay + _bdot_tn(uv, bkhat)
            st_ref[n * G + p] = jnp.where(same_head, s_new, 0.0)

    o_rows = jnp.concatenate(o, axis=0)
    mean = _bdot(o_rows, ones_bd) * (1.0 / HEAD_B)
    d = o_rows - mean
    var = _bdot(d * d, ones_bd) * (1.0 / HEAD_B)
    on = to_cols(d * lax.rsqrt(var + GN_EPS)) * lnw_ref[...] + lnb_ref[...]
    bonus = to_cols(_bdot(to_rows(xr * k2 * rk_ref[...]), ones_bd)) * xv
    o_ref[...] = ((on + bonus) * gate).astype(BF16)

    if NB == 1:
        cr_ref[...] = r_ref[C - SUB:C, :]
        ck_ref[...] = k_ref[C - SUB:C, :]
        cv_ref[...] = v_ref[C - SUB:C, :]
        csm_ref[...] = sm_ref[C - SUB:C, :]

    @pl.when(ci == nch - 1)
    def _():
        for n in seqs:
            for p in pairs:
                s_ref[n, p] = _xdot(st_ref[n * G + p], gather, 3)


def _rwkv_mixer(proj, row0, nb, seq, C, NB, G, prev_rkv, prev_sm, s0, mu_rkv, mu_sm, w2p, a2p, g2p,
                w0, a0, k_k, k_a, r_k, lnw, lnb):
    nch = seq // C
    assert NB == 1 or (nch == 1 and C == SUB)
    R = NB * C
    ng = N_PAIR // G
    gw = G * LANE
    blk0 = row0 // R
    bcast = s0.shape[0] == 1
    bsel = (lambda b: 0) if bcast else (lambda b: b)

    def proj_spec(col0):
        return pl.BlockSpec((R, gw), lambda b, g, c: (blk0 + b * nch + c, col0 // gw + g))

    def prev_spec(part):
        if NB == 1:
            return pl.BlockSpec((SUB, gw), lambda b, g, c: (0, part * (W_B // gw) + g))
        return pl.BlockSpec((1, NB, gw), lambda b, g, c: (0, b, part * (W_B // gw) + g))

    def vec_spec(part=0):
        return pl.BlockSpec((1, gw), lambda b, g, c: (0, part * (W_B // gw) + g))

    in_specs = [
        proj_spec(RKV0), proj_spec(RKV0 + W_B), proj_spec(RKV0 + 2 * W_B),
        pl.BlockSpec((R, SM_W), lambda b, g, c: (blk0 + b * nch + c, SM0 // SM_W)),
        prev_spec(0), prev_spec(1), prev_spec(2),
        (pl.BlockSpec((SUB, SM_W), lambda b, g, c: (0, 0)) if NB == 1
         else pl.BlockSpec((1, NB, SM_W), lambda b, g, c: (0, b, 0))),
        pl.BlockSpec((NB, G, LANE, HEAD_B), lambda b, g, c: (bsel(b), g, 0, 0)),
        vec_spec(0), vec_spec(1), vec_spec(2),
        pl.BlockSpec((1, SM_W), lambda b, g, c: (0, 0)),
        pl.BlockSpec((LANE, gw), lambda b, g, c: (0, g)),
        pl.BlockSpec((LANE, gw), lambda b, g, c: (0, g)),
        pl.BlockSpec((2 * LANE, gw), lambda b, g, c: (0, g)),
        vec_spec(), vec_spec(), vec_spec(), vec_spec(), vec_spec(), vec_spec(), vec_spec(),
    ]
    return pl.pallas_call(
        functools.partial(_rwkv_kernel, C=C, NB=NB, G=G),
        out_shape=(jax.ShapeDtypeStruct((nb * seq, W_B), BF16),
                   jax.ShapeDtypeStruct((nb, N_PAIR, LANE, HEAD_B), F32)),
        grid=(nb // NB, ng, nch),
        in_specs=in_specs,
        out_specs=(pl.BlockSpec((R, gw), lambda b, g, c: (b * nch + c, g)),
                   pl.BlockSpec((NB, G, LANE, HEAD_B), lambda b, g, c: (b, g, 0, 0))),
        scratch_shapes=[pltpu.VMEM((SUB, gw), F32), pltpu.VMEM((SUB, gw), F32), pltpu.VMEM((SUB, gw), F32),
                        pltpu.VMEM((SUB, SM_W), F32), pltpu.VMEM((NB * G, LANE, LANE), F32)],
        compiler_params=pltpu.CompilerParams(
            dimension_semantics=("parallel", "parallel", "arbitrary"), vmem_limit_bytes=VMEM_LIMIT),
        name="rwkv_mixer",
    )(proj, proj, proj, proj, prev_rkv, prev_rkv, prev_rkv, prev_sm, s0,
      mu_rkv, mu_rkv, mu_rkv, mu_sm, w2p, a2p, g2p, w0, a0, k_k, k_a, r_k, lnw, lnb)


_AT, _RT, _BT, _KT, _BHAT, _KHAT, _XV, _GATE, _BONUS, _N_PRE = range(10)


def _rwkv_seq_kernel(r0_ref, k0_ref, v0_ref, sm0_ref, rn_ref, kn_ref, vn_ref, smn_ref,
                     pr_ref, pk_ref, pv_ref, psm_ref, s0_ref,
                     mur_ref, muk_ref, muv_ref, musm_ref, w2_ref, a2_ref, g2_ref,
                     w0_ref, a0_ref, kk_ref, ka_ref, rk_ref, lnw_ref, lnb_ref,
                     o_ref, s_ref, cr_ref, ck_ref, cv_ref, csm_ref, st_ref, pre_ref, ptot_ref, *, C, G, GRP):
    ci = pl.program_id(2)
    nch = pl.num_programs(2)
    pairs = range(G)

    r2 = _iota((LANE, LANE), 0)
    c2 = _iota((LANE, LANE), 1)
    same_head = (r2 < HEAD_B) == (c2 < HEAD_B)
    ones_bd = same_head.astype(F32)
    spread = (_iota((HEAD_B, LANE), 0) == (_iota((HEAD_B, LANE), 1) & (HEAD_B - 1))).astype(F32)
    gather = ((_iota((LANE, HEAD_B), 0) & (HEAD_B - 1)) == _iota((LANE, HEAD_B), 1)).astype(F32)
    tri, _ = _seq_masks(C, C)

    def col(x, p):
        return x[:, p * LANE:(p + 1) * LANE]

    def to_rows(x):
        return jnp.concatenate([col(x, p) for p in pairs], axis=0)

    def to_cols(y):
        return jnp.concatenate([y[p * C:(p + 1) * C] for p in pairs], axis=1)

    def lerp(x, hist, mu):
        return x + (_shifted(x, hist, 1) - x) * mu

    def preamble(r_ref, k_ref, v_ref, sm_ref, hr_ref, hk_ref, hv_ref, hsm_ref, slot):
        xs = lerp(sm_ref[...], hsm_ref[...], musm_ref[...])
        yield
        slab_wa = xs[:, SM_WA:SM_WA + LANE]
        wl = _bdot(jnp.tanh(slab_wa), w2_ref[...])
        yield
        al = _bdot(slab_wa, a2_ref[...])
        yield
        pre_ref[slot, _GATE] = _bdot(_sigmoid(xs[:, SM_G:SM_G + 2 * LANE]), g2_ref[...])
        yield
        xr = lerp(r_ref[...], hr_ref[...], mur_ref[...])
        yield
        xk = lerp(k_ref[...], hk_ref[...], muk_ref[...])
        yield
        xv = lerp(v_ref[...], hv_ref[...], muv_ref[...])
        pre_ref[slot, _XV] = xv
        yield
        ld = -jnp.exp(-_softplus(-(w0_ref[...] + wl)) - 0.5)
        yield
        lp = _xdot_r(tri, ld, 3)
        yield
        aa = _sigmoid(a0_ref[...] + al)
        yield
        kkr = xk * kk_ref[...]
        ss = to_cols(_bdot(to_rows(kkr * kkr), ones_bd))
        yield
        kkn = kkr * lax.rsqrt(ss + 1e-6)
        k2 = xk * (1.0 + (aa - 1.0) * ka_ref[...])
        yield
        pre_ref[slot, _BONUS] = to_cols(_bdot(to_rows(xr * k2 * rk_ref[...]), ones_bd)) * xv
        yield
        pre_ref[slot, _AT] = -kkn * jnp.exp(lp - ld)
        yield
        pre_ref[slot, _RT] = xr * jnp.exp(lp)
        yield
        kb = kkn * aa
        e_neg = jnp.exp(-lp)
        pre_ref[slot, _BT] = kb * e_neg
        yield
        pre_ref[slot, _KT] = k2 * e_neg
        yield
        lp_tot = lp[C - 1:C, :]
        e_rem = jnp.exp(lp_tot - lp)
        pre_ref[slot, _BHAT] = kb * e_rem
        yield
        pre_ref[slot, _KHAT] = k2 * e_rem
        ptot_ref[slot] = jnp.broadcast_to(jnp.exp(lp_tot), (SUB, lp.shape[1]))
        yield
        cr_ref[...] = r_ref[C - SUB:C, :]
        ck_ref[...] = k_ref[C - SUB:C, :]
        cv_ref[...] = v_ref[C - SUB:C, :]
        csm_ref[...] = sm_ref[C - SUB:C, :]

    @pl.when(ci == 0)
    def _():
        for _ in preamble(r0_ref, k0_ref, v0_ref, sm0_ref, pr_ref, pk_ref, pv_ref, psm_ref, 0):
            pass
        for p in pairs:
            st_ref[p] = jnp.where(same_head, _xdot(s0_ref[0, p], spread, 3), 0.0)

    slot = ci % 2
    ahead = preamble(rn_ref, kn_ref, vn_ref, smn_ref, cr_ref, ck_ref, cv_ref, csm_ref, 1 - slot)
    calls = [0]

    def tick():
        calls[0] += 1
        if calls[0] % 6 == 0:
            next(ahead, None)

    def each(fn, items):
        out = {}
        for it in items:
            out[it] = fn(it)
            tick()
        return out

    def pre(idx, p):
        return pre_ref[slot, idx, :, p * LANE:(p + 1) * LANE]

    pk = _Packed(C, C)
    stack_mask = (_iota((2 * C, LANE), 0) < C) == (_iota((2 * C, LANE), 1) < HEAD_B)

    def by_head(x):
        return jnp.where(stack_mask, jnp.concatenate([x, x], axis=0), 0.0)

    def run_group(grp):
        s = {p: st_ref[p] for p in grp}
        ar = {p: jnp.concatenate([pre(_AT, p), pre(_RT, p)], axis=0) for p in grp}
        ars = each(lambda p: _bdot_nt(ar[p], s[p]), grp)
        ab = each(lambda p: _bdot_nt(ar[p], by_head(pre(_BT, p))), grp)
        ak = each(lambda p: _bdot_nt(ar[p], by_head(pre(_KT, p))), grp)
        ps = {p: jnp.where(pk.strict, ab[p][0:C], 0.0) for p in grp}
        ts = {p: pk.eye + ps[p] for p in grp}
        n = 1
        while 2 * n < C:
            ps = each(lambda p: _bdot(ps[p], pk.block_diag(ps[p])), grp)
            ts = each(lambda p: ts[p] + _bdot(ps[p], pk.block_diag(ts[p])), grp)
            n *= 2
        xv = {p: pre(_XV, p) for p in grp}
        v_bh = {p: by_head(xv[p]) for p in grp}
        y = each(lambda p: ars[p][0:C] + _bdot(jnp.where(pk.strict, ak[p][0:C], 0.0), v_bh[p]), grp)
        u = each(lambda p: _bdot(ts[p], by_head(y[p])), grp)
        o = each(lambda p: ars[p][C:2 * C] + _bdot(jnp.where(pk.incl, ab[p][C:2 * C], 0.0), by_head(u[p]))
                 + _bdot(jnp.where(pk.incl, ak[p][C:2 * C], 0.0), v_bh[p]), grp)
        for p in grp:
            uv = jnp.concatenate([u[p], xv[p]], axis=0)
            bkhat = jnp.concatenate([pre(_BHAT, p), pre(_KHAT, p)], axis=0)
            s_new = s[p] * ptot_ref[slot, 0:1, p * LANE:(p + 1) * LANE] + _bdot_tn(uv, bkhat)
            st_ref[p] = jnp.where(same_head, s_new, 0.0)
            tick()
        return o

    o = {}
    for g0 in range(0, G, GRP):
        o.update(run_group(range(g0, g0 + GRP)))
    o_rows = jnp.concatenate([o[p] for p in pairs], axis=0)
    mean = _bdot(o_rows, ones_bd) * (1.0 / HEAD_B)
    d = o_rows - mean
    var = _bdot(d * d, ones_bd) * (1.0 / HEAD_B)
    on = to_cols(d * lax.rsqrt(var + GN_EPS)) * lnw_ref[...] + lnb_ref[...]
    o_ref[...] = ((on + pre_ref[slot, _BONUS]) * pre_ref[slot, _GATE]).astype(BF16)
    for _ in ahead:
        pass

    @pl.when(ci == nch - 1)
    def _():
        for p in pairs:
            s_ref[0, p] = _xdot(st_ref[p], gather, 3)


def _rwkv_seq_mixer(proj, nb, seq, C, prev_rkv, prev_sm, s0, mu_rkv, mu_sm, w2p, a2p, g2p,
                    w0, a0, k_k, k_a, r_k, lnw, lnb):
    G = N_PAIR
    nch = seq // C
    gw = G * LANE

    def first_spec(col0, w):
        return pl.BlockSpec((C, w), lambda b, g, c: (b * nch, col0 // w))

    def next_spec(col0, w):
        return pl.BlockSpec((C, w), lambda b, g, c: (b * nch + jnp.minimum(c + 1, nch - 1), col0 // w))

    def prev_spec(part):
        return pl.BlockSpec((SUB, gw), lambda b, g, c: (0, part))

    def vec_spec(part=0):
        return pl.BlockSpec((1, gw), lambda b, g, c: (0, part))

    in_specs = [
        first_spec(RKV0, gw), first_spec(RKV0 + W_B, gw), first_spec(RKV0 + 2 * W_B, gw), first_spec(SM0, SM_W),
        next_spec(RKV0, gw), next_spec(RKV0 + W_B, gw), next_spec(RKV0 + 2 * W_B, gw), next_spec(SM0, SM_W),
        prev_spec(0), prev_spec(1), prev_spec(2),
        pl.BlockSpec((SUB, SM_W), lambda b, g, c: (0, 0)),
        pl.BlockSpec((1, G, LANE, HEAD_B), lambda b, g, c: (0, 0, 0, 0)),
        vec_spec(0), vec_spec(1), vec_spec(2),
        pl.BlockSpec((1, SM_W), lambda b, g, c: (0, 0)),
        pl.BlockSpec((LANE, gw), lambda b, g, c: (0, 0)),
        pl.BlockSpec((LANE, gw), lambda b, g, c: (0, 0)),
        pl.BlockSpec((2 * LANE, gw), lambda b, g, c: (0, 0)),
        vec_spec(), vec_spec(), vec_spec(), vec_spec(), vec_spec(), vec_spec(), vec_spec(),
    ]
    return pl.pallas_call(
        functools.partial(_rwkv_seq_kernel, C=C, G=G, GRP=G),
        out_shape=(jax.ShapeDtypeStruct((nb * seq, W_B), BF16),
                   jax.ShapeDtypeStruct((nb, N_PAIR, LANE, HEAD_B), F32)),
        grid=(nb, 1, nch),
        in_specs=in_specs,
        out_specs=(pl.BlockSpec((C, gw), lambda b, g, c: (b * nch + c, 0)),
                   pl.BlockSpec((1, G, LANE, HEAD_B), lambda b, g, c: (b, 0, 0, 0))),
        scratch_shapes=[pltpu.VMEM((SUB, gw), F32), pltpu.VMEM((SUB, gw), F32), pltpu.VMEM((SUB, gw), F32),
                        pltpu.VMEM((SUB, SM_W), F32), pltpu.VMEM((G, LANE, LANE), F32),
                        pltpu.VMEM((2, _N_PRE, C, gw), F32), pltpu.VMEM((2, SUB, gw), F32)],
        compiler_params=pltpu.CompilerParams(
            dimension_semantics=("parallel", "arbitrary", "arbitrary"), vmem_limit_bytes=VMEM_LIMIT),
        name="rwkv_seq_mixer",
    )(proj, proj, proj, proj, proj, proj, proj, proj, prev_rkv, prev_rkv, prev_rkv, prev_sm, s0,
      mu_rkv, mu_rkv, mu_rkv, mu_sm, w2p, a2p, g2p, w0, a0, k_k, k_a, r_k, lnw, lnb)


def _small_layout(cols_ba, cols_w, cols_a, cols_g, axis=-1):
    def z(n):
        shape = list(cols_w.shape)
        shape[axis] = n
        return jnp.zeros(shape, cols_w.dtype)
    return jnp.concatenate(
        [cols_ba, z(SM_WA - cols_ba.shape[axis]), cols_w, cols_a, cols_g,
         z(SM_W - SM_G - cols_g.shape[axis])], axis=axis)


def kernel(x_prompt, x_sample, state_delta, state_conv_qkv, state_wkv, state_shift, state_ffn_conv, meta, norm1, w_in, conv_a, a_log, dt_bias, onorm_a, mu_b, w0, w2, a0, a2, g2, k_k, k_a, r_k, lnx_w, lnx_b, w_o, norm2, w_ffn_in, conv_f, w_ffn_out, norm_f):
    nbp, seq_p, _ = x_prompt.shape
    nbs, seq_s, _ = x_sample.shape
    n_s = nbs * seq_s
    assert w_in.shape[0] == 1, "single-layer trunk"
    assert seq_s == SUB
    l = 0

    wt = w_in[l].T
    o_b = A_PROJ
    o_l = A_PROJ + 3 * W_B
    w_cat_t = jnp.concatenate([
        wt[:4 * W_A], wt[o_b:o_l],
        _small_layout(wt[4 * W_A:A_PROJ], wt[o_l:o_l + W_LORA], wt[o_l + W_LORA:o_l + W_LORA + A_LORA],
                      wt[o_l + W_LORA + A_LORA:], axis=0)], axis=0).astype(BF16)
    mu = mu_b[l]
    mu_rkv = mu[None, :3 * W_B]
    mu_sm = _small_layout(jnp.zeros((1, 2 * H_A), F32), mu[None, 3 * W_B:3 * W_B + W_LORA],
                          mu[None, 3 * W_B + W_LORA:3 * W_B + W_LORA + A_LORA],
                          mu[None, 3 * W_B + W_LORA + A_LORA:])
    w2p = jnp.concatenate([w2[l], jnp.zeros((LANE - W_LORA, W_B), F32)], axis=0)
    a2p = jnp.concatenate([jnp.zeros((W_LORA, W_B), F32), a2[l]], axis=0)
    g2p = jnp.concatenate([g2[l], jnp.zeros((2 * LANE - G_LORA, W_B), F32)], axis=0)
    alog_row = jnp.concatenate([jnp.zeros((H_A,), F32), a_log[l], jnp.zeros((LANE - 2 * H_A,), F32)])[None]
    dtb_row = jnp.concatenate([jnp.zeros((H_A,), F32), dt_bias[l], jnp.zeros((LANE - 2 * H_A,), F32)])[None]
    wo_bf = w_o[l].astype(BF16)
    wfo_bf = w_ffn_out[l].astype(BF16)
    row = lambda v: v.reshape(1, -1)

    def mix(proj, row0, nb, seq, C, NB, prev_qkv, prev_rkv, prev_sm, s_delta, s_wkv):
        delta_params = (conv_a[l], alog_row, dtb_row, row(onorm_a[l]))
        rwkv_params = (mu_rkv, mu_sm, w2p, a2p, g2p, row(w0[l]), row(a0[l]), row(k_k[l]), row(k_a[l]),
                       row(r_k[l]), row(lnx_w[l]), row(lnx_b[l]))
        if seq // C > 1:
            assert row0 == 0 and NB == 1
            oa, sd = _delta_multi_mixer(proj, nb, seq, C, 2, prev_qkv, s_delta, *delta_params)
            ob, sw = _rwkv_seq_mixer(proj, nb, seq, C, prev_rkv, prev_sm, s_wkv, *rwkv_params)
        else:
            oa, sd = _delta_mixer(proj, row0, nb, seq, C, NB, prev_qkv, s_delta, *delta_params)
            ob, sw = _rwkv_mixer(proj, row0, nb, seq, C, NB, N_PAIR, prev_rkv, prev_sm, s_wkv, *rwkv_params)
        return oa, ob, sd, sw

    xs_rows = jnp.concatenate([x_sample.reshape(n_s, D_MODEL), meta], axis=0)
    xp_rows = x_prompt.reshape(nbp * seq_p, D_MODEL)
    n_small = n_s + N_META
    proj_s, proj_tail_s = _norm_matmul(xs_rows, row(norm1[l]), w_cat_t, n_small, 1536, 3, nbs)
    proj_p = _norm_matmul(xp_rows, row(norm1[l]), w_cat_t, 1024, 1536)

    zeros = lambda *s: jnp.zeros(s, F32)
    oa_m, ob_m, sd_m, sw_m = mix(proj_s, n_s, 1, N_META, N_META, 1, zeros(SUB, 3 * W_A),
                                 zeros(SUB, 3 * W_B), zeros(SUB, SM_W),
                                 zeros(1, H_A, HEAD_A, HEAD_A), zeros(1, N_PAIR, LANE, HEAD_B))
    tail = proj_s[n_small - SUB:n_small]
    oa_p, ob_p, sd_p, sw_p = mix(proj_p, 0, nbp, seq_p, 64, 1, tail[:, :3 * W_A],
                                 tail[:, RKV0:SM0], tail[:, SM0:], sd_m, sw_m)
    sh = state_shift[l]
    sh_sm = _small_layout(jnp.zeros((nbs, 1, 2 * H_A), F32), sh[..., 3 * W_B:3 * W_B + W_LORA],
                          sh[..., 3 * W_B + W_LORA:3 * W_B + W_LORA + A_LORA],
                          sh[..., 3 * W_B + W_LORA + A_LORA:])
    tmajor = lambda s: s.transpose(1, 0, 2)
    oa_s, ob_s, sd_s, sw_s = mix(proj_s, 0, nbs, seq_s, seq_s, 8, tmajor(state_conv_qkv[l]),
                                 tmajor(sh[..., :3 * W_B]), tmajor(sh_sm), state_delta[l],
                                 state_wkv[l].reshape(nbs, N_PAIR, LANE, HEAD_B))

    oa_small = jnp.concatenate([oa_s, oa_m], axis=0)
    ob_small = jnp.concatenate([ob_s, ob_m], axis=0)
    x1_s, h2_s = _out_proj(xs_rows, oa_small, ob_small, wo_bf, row(norm2[l]), n_small // 5)
    x1_p, h2_p = _out_proj(xp_rows, oa_p, ob_p, wo_bf, row(norm2[l]), 512)
    act_s, gate_tail_s, gate_last_s = _ffn_in(h2_s, w_ffn_in[l], state_ffn_conv[l].transpose(1, 0, 2), conv_f[l],
                                              n_small, 768, 3, 0, n_s + SUB)
    act_p, tail_p = _ffn_in(h2_p, w_ffn_in[l], gate_last_s[None], conv_f[l], 1024, 768, 3, seq_p // 1024)
    y_s = _ffn_out(act_s, wfo_bf, x1_s, row(norm_f), n_s, 1024, 768)
    y_p = _ffn_out(act_p, wfo_bf, x1_p, row(norm_f), nbp * seq_p, 1024, 768)

    def states(conv_new, last, ffn_new, nb, sd, sw):
        shift_new = jnp.concatenate([last[..., RKV0:SM0], last[..., SM0 + SM_WA:SM0 + SM_WA + W_LORA + A_LORA],
                                     last[..., SM0 + SM_G:SM0 + SM_G + G_LORA]], axis=-1)
        return (sd[None], conv_new[None], sw.reshape(nb, H_B, HEAD_B, HEAD_B)[None], shift_new[None],
                ffn_new[None])

    p3 = proj_p.reshape(nbp, seq_p, P_CAT)
    tail_s = proj_tail_s.transpose(1, 0, 2)
    return ((y_p.reshape(nbp, seq_p, D_MODEL), y_s.reshape(nbs, seq_s, D_MODEL))
            + states(p3[:, seq_p - 3:, :3 * W_A], p3[:, seq_p - 1:, :], tail_p[:, SUB - 2:, :], nbp, sd_p, sw_p)
            + states(tail_s[:, :, :3 * W_A], tail_s[:, 2:, :], gate_tail_s.transpose(1, 0, 2), nbs, sd_s, sw_s))
```

```python
import functools

import jax
import jax.numpy as jnp
from jax import lax
from jax.experimental import pallas as pl
from jax.experimental.pallas import tpu as pltpu

F32 = jnp.float32
BF16 = jnp.bfloat16

D_MODEL = 2048
N_META = 16
W_A = 1024
HEAD_A = 128
H_A = 8
W_B = 1024
HEAD_B = 64
H_B = 16
N_PAIR = H_B // 2
W_LORA = 64
A_LORA = 64
G_LORA = 160
D_FF = 5376
RMS_EPS = 1e-6
GN_EPS = 64e-5
A_PROJ = 4 * W_A + 2 * H_A
B_PROJ = 3 * W_B + W_LORA + A_LORA + G_LORA

QKVZ0 = 0
RKV0 = 4 * W_A
SM0 = RKV0 + 3 * W_B
SM_W = 512
SM_BA = 0
SM_WA = 128
SM_G = 256
P_CAT = SM0 + SM_W

LANE = 128
SUB = 8
VMEM_LIMIT = 48 * 1024 * 1024
VMEM_LIMIT_BIG = 58 * 1024 * 1024
NEG_BIG = -1e30

NT_DIMS = (((1,), (1,)), ((), ()))


def _bdot(a, b):
    return jnp.dot(a.astype(BF16), b.astype(BF16), preferred_element_type=F32)


def _bdot_nt(a, b):
    return lax.dot_general(a.astype(BF16), b.astype(BF16), NT_DIMS, preferred_element_type=F32)


def _bdot_tn(a, b):
    return lax.dot_general(a.astype(BF16), b.astype(BF16), (((0,), (0,)), ((), ())),
                           preferred_element_type=F32)


def _pieces(a, n):
    out = []
    rem = a
    for i in range(n):
        p = rem.astype(BF16)
        out.append(p)
        if i + 1 < n:
            rem = rem - p.astype(F32)
    return out


def _xdot(a, b, n, dims=(((1,), (0,)), ((), ()))):
    bb = b.astype(BF16)
    acc = None
    for p in _pieces(a, n):
        t = lax.dot_general(p, bb, dims, preferred_element_type=F32)
        acc = t if acc is None else acc + t
    return acc


def _xdot_r(a, b, n, dims=(((1,), (0,)), ((), ()))):
    ab = a.astype(BF16)
    acc = None
    for p in _pieces(b, n):
        t = lax.dot_general(ab, p, dims, preferred_element_type=F32)
        acc = t if acc is None else acc + t
    return acc


def _sigmoid(x):
    return 1.0 / (1.0 + jnp.exp(-x))


def _silu(x):
    return x * _sigmoid(x)


def _softplus(x):
    return jnp.maximum(x, 0.0) + jnp.log(1.0 + jnp.exp(-jnp.abs(x)))


def _iota(shape, dim):
    return lax.broadcasted_iota(jnp.int32, shape, dim)


def _shifted(x, prev8, k):
    n = x.shape[0]
    xr = pltpu.roll(x, k, 0)
    pr = pltpu.roll(prev8, k, 0)
    first = jnp.where(_iota((SUB, x.shape[1]), 0) < k, pr, xr[0:SUB])
    if n == SUB:
        return first
    return jnp.concatenate([first, xr[SUB:]], axis=0)


def _repeat_rows(s, reps):
    n, c = s.shape
    return jnp.broadcast_to(s[:, None, :], (n, reps, c)).reshape(n * reps, c)


def _delayed(x, hist, k, nseq):
    if nseq == 1:
        return _shifted(x, hist, k)
    t = _iota(x.shape, 0) & (SUB - 1)
    d = pltpu.roll(x, k, 0)
    for tt in range(k):
        d = jnp.where(t == tt, hist[len(hist) - k + tt], d)
    return d


def _same_seq(shape, c, rows_total):
    if c == rows_total:
        return None
    sh = c.bit_length() - 1
    return (lax.shift_right_logical(_iota(shape, 0), sh)
            == lax.shift_right_logical(_iota(shape, 1) & (rows_total - 1), sh))


class _Packed:
    def __init__(self, r, c):
        self.r = r
        self.c = c
        row = _iota((r, 2 * r), 0)
        col = _iota((r, 2 * r), 1) & (r - 1)
        same = _same_seq((r, 2 * r), c, r)
        self.incl = (row >= col) if same is None else ((row >= col) & same)
        self.strict = (row > col) if same is None else ((row > col) & same)
        self.eye = (row == col).astype(F32)
        self.first = _iota((r, 2 * r), 1) < r
        self.bd_mask = (_iota((2 * r, 2 * r), 0) < r) == (_iota((2 * r, 2 * r), 1) < r)

    def block_diag(self, m):
        return jnp.where(self.bd_mask, jnp.concatenate([m, m], axis=0), 0.0)

    def inverse_many(self, xs):
        ts = [self.eye + x for x in xs]
        ps = list(xs)
        n = 1
        while 2 * n < self.c:
            ps = [_bdot(p, self.block_diag(p)) for p in ps]
            ts = [t + _bdot(p, self.block_diag(t)) for p, t in zip(ps, ts)]
            n *= 2
        return ts


def _seq_masks(r, c):
    row = _iota((r, r), 0)
    col = _iota((r, r), 1)
    same = _same_seq((r, r), c, r)
    if same is None:
        return (row >= col).astype(F32), None
    return ((row >= col) & same).astype(F32), same.astype(F32)


def _seq_tail_rows(x, slab_ref, dst_ref, n_tail, nseq, col0=0):
    for s in range(x.shape[1] // LANE):
        slab_ref[s] = x[:, s * LANE:(s + 1) * LANE]
        for t in range(n_tail):
            dst_ref[t, :, col0 + s * LANE:col0 + (s + 1) * LANE] = (
                slab_ref[s, pl.ds(SUB - n_tail + t, nseq, stride=SUB), :])


def _norm_matmul_kernel(x_ref, g_ref, w_ref, o_ref, *rest, n_tail, nseq):
    tail_ref, h_ref, slab_ref = rest if n_tail else (None,) + rest + (None,)

    @pl.when(pl.program_id(1) == 0)
    def _():
        x = x_ref[...]
        ms = jnp.mean(x * x, axis=-1, keepdims=True)
        h_ref[...] = (x * lax.rsqrt(ms + RMS_EPS) * g_ref[...]).astype(BF16)

    out = lax.dot_general(h_ref[...], w_ref[...], NT_DIMS, preferred_element_type=F32)
    o_ref[...] = out
    if n_tail:
        _seq_tail_rows(out, slab_ref, tail_ref, n_tail, nseq)


def _norm_matmul(x, g, wt, tm, tn, n_tail=0, nseq=0):
    m, k = x.shape
    n = wt.shape[0]
    out_shape = jax.ShapeDtypeStruct((m, n), F32)
    out_specs = pl.BlockSpec((tm, tn), lambda i, j: (i, j))
    if n_tail:
        assert m == tm
        out_shape = (out_shape, jax.ShapeDtypeStruct((n_tail, nseq, n), F32))
        out_specs = (out_specs, pl.BlockSpec((n_tail, nseq, tn), lambda i, j: (0, 0, j)))
    return pl.pallas_call(
        functools.partial(_norm_matmul_kernel, n_tail=n_tail, nseq=nseq),
        out_shape=out_shape,
        grid=(m // tm, n // tn),
        in_specs=[
            pl.BlockSpec((tm, k), lambda i, j: (i, 0)),
            pl.BlockSpec((1, k), lambda i, j: (0, 0)),
            pl.BlockSpec((tn, k), lambda i, j: (j, 0)),
        ],
        out_specs=out_specs,
        scratch_shapes=[pltpu.VMEM((tm, k), BF16)] + ([pltpu.VMEM((tn // LANE, tm, LANE), F32)] if n_tail else []),
        compiler_params=pltpu.CompilerParams(
            dimension_semantics=("parallel", "arbitrary"), vmem_limit_bytes=VMEM_LIMIT_BIG),
        name="norm_matmul",
    )(x, g, wt)


def _out_proj_kernel(x_ref, oa_ref, ob_ref, wt_ref, wb_ref, g_ref, x1_ref, h_ref):
    acc = jnp.dot(oa_ref[...], wt_ref[...], preferred_element_type=F32)
    acc = acc + jnp.dot(ob_ref[...], wb_ref[...], preferred_element_type=F32)
    x1 = x_ref[...] + acc
    x1_ref[...] = x1
    ms = jnp.mean(x1 * x1, axis=-1, keepdims=True)
    h_ref[...] = (x1 * lax.rsqrt(ms + RMS_EPS) * g_ref[...]).astype(BF16)


def _out_proj(x, oa, ob, wo, g, tm):
    m, d = x.shape
    return pl.pallas_call(
        _out_proj_kernel,
        out_shape=(jax.ShapeDtypeStruct((m, d), F32), jax.ShapeDtypeStruct((m, d), BF16)),
        grid=(m // tm,),
        in_specs=[
            pl.BlockSpec((tm, d), lambda i: (i, 0)),
            pl.BlockSpec((tm, W_A), lambda i: (i, 0)),
            pl.BlockSpec((tm, W_B), lambda i: (i, 0)),
            pl.BlockSpec((W_A, d), lambda i: (0, 0)),
            pl.BlockSpec((W_B, d), lambda i: (W_A // W_B, 0)),
            pl.BlockSpec((1, d), lambda i: (0, 0)),
        ],
        out_specs=(pl.BlockSpec((tm, d), lambda i: (i, 0)), pl.BlockSpec((tm, d), lambda i: (i, 0))),
        compiler_params=pltpu.CompilerParams(
            dimension_semantics=("parallel",), vmem_limit_bytes=VMEM_LIMIT),
        name="out_proj",
    )(x, oa, ob, wo, wo, g)


def _ffn_in_kernel(h_ref, wg_ref, wu_ref, prev_ref, cw_ref, act_ref, tail_ref, *rest,
                   tiles_per_seq, state_rows, nsub):
    if tiles_per_seq:
        wgb_ref, wub_ref, carry_ref = rest
    else:
        last_ref, wgb_ref, wub_ref, carry_ref, slab_ref = rest
    m = pl.program_id(1)

    @pl.when(m == 0)
    def _():
        wgb_ref[...] = wg_ref[...].astype(BF16)
        wub_ref[...] = wu_ref[...].astype(BF16)

    if tiles_per_seq:
        @pl.when(m % tiles_per_seq == 0)
        def _():
            carry_ref[...] = prev_ref[...]

    h = h_ref[...]
    n = h.shape[0]
    w = wgb_ref.shape[1] // nsub
    for i in range(nsub):
        cs = slice(i * w, (i + 1) * w)
        gate = jnp.dot(h, wgb_ref[:, cs], preferred_element_type=F32)
        up = jnp.dot(h, wub_ref[:, cs], preferred_element_type=F32)
        if tiles_per_seq:
            prev8 = carry_ref[:, cs]
            d1 = _shifted(gate, prev8, 1)
            d2 = _shifted(gate, prev8, 2)
            carry_ref[:, cs] = gate[n - SUB:n]
            tail_ref[:, cs] = gate[n - SUB:n]
        else:
            nseq = (state_rows - SUB) // SUB
            _seq_tail_rows(gate, slab_ref, tail_ref, 2, nseq, i * w)
            last_ref[:, cs] = gate[n - SUB:n]
            pad = jnp.zeros((n - nseq * SUB, w), F32)
            hist = [jnp.concatenate([_repeat_rows(prev_ref[t, :, cs], SUB), pad], axis=0) for t in range(2)]
            t_in_seq = jnp.where(_iota(gate.shape, 0) < state_rows, _iota(gate.shape, 0) & (SUB - 1), SUB)
            d1 = jnp.where(t_in_seq == 0, hist[1], pltpu.roll(gate, 1, 0))
            d2 = jnp.where(t_in_seq == 0, hist[0], jnp.where(t_in_seq == 1, hist[1], pltpu.roll(gate, 2, 0)))
        y = gate * cw_ref[2:3, cs] + d1 * cw_ref[1:2, cs] + d2 * cw_ref[0:1, cs]
        act_ref[:, cs] = (_silu(y) * up).astype(BF16)


def _ffn_in(h, w, prev, cw, tm, tn, nsub, tiles_per_seq, state_rows=0):
    m, d = h.shape
    nj = D_FF // tn
    nm = m // tm
    if tiles_per_seq:
        nseq = nm // tiles_per_seq
        prev_spec = pl.BlockSpec((None, SUB, tn), lambda j, i: (0, 0, j))
        extra_shapes = (jax.ShapeDtypeStruct((nseq, SUB, D_FF), F32),)
        extra_specs = (pl.BlockSpec((None, SUB, tn), lambda j, i: (i // tiles_per_seq, 0, j)),)
        extra_scratch = []
    else:
        assert nm == 1
        nseq = (state_rows - SUB) // SUB
        prev_spec = pl.BlockSpec((2, nseq, tn), lambda j, i: (0, 0, j))
        extra_shapes = (jax.ShapeDtypeStruct((2, nseq, D_FF), F32), jax.ShapeDtypeStruct((SUB, D_FF), F32))
        extra_specs = (pl.BlockSpec((2, nseq, tn), lambda j, i: (0, 0, j)),
                       pl.BlockSpec((SUB, tn), lambda j, i: (0, j)))
        extra_scratch = [pltpu.VMEM((tn // nsub // LANE, tm, LANE), F32)]
    return pl.pallas_call(
        functools.partial(_ffn_in_kernel, tiles_per_seq=tiles_per_seq, state_rows=state_rows, nsub=nsub),
        out_shape=(jax.ShapeDtypeStruct((m, D_FF), BF16),) + extra_shapes,
        grid=(nj, nm),
        in_specs=[
            pl.BlockSpec((tm, d), lambda j, i: (i, 0)),
            pl.BlockSpec((d, tn), lambda j, i: (0, j)),
            pl.BlockSpec((d, tn), lambda j, i: (0, nj + j)),
            prev_spec,
            pl.BlockSpec((3, tn), lambda j, i: (0, j)),
        ],
        out_specs=(pl.BlockSpec((tm, tn), lambda j, i: (i, j)),) + extra_specs,
        scratch_shapes=[pltpu.VMEM((d, tn), BF16), pltpu.VMEM((d, tn), BF16), pltpu.VMEM((SUB, tn), F32)]
        + extra_scratch,
        compiler_params=pltpu.CompilerParams(
            dimension_semantics=("parallel", "arbitrary"), vmem_limit_bytes=VMEM_LIMIT_BIG),
        name="ffn_in",
    )(h, w, w, prev, cw)


def _ffn_out_kernel(a_ref, w_ref, x_ref, g_ref, o_ref):
    kk = pl.program_id(1)

    @pl.when(kk == 0)
    def _():
        o_ref[...] = x_ref[...]

    o_ref[...] += jnp.dot(a_ref[...], w_ref[...], preferred_element_type=F32)

    @pl.when(kk == pl.num_programs(1) - 1)
    def _():
        x = o_ref[...]
        ms = jnp.mean(x * x, axis=-1, keepdims=True)
        o_ref[...] = x * lax.rsqrt(ms + RMS_EPS) * g_ref[...]


def _ffn_out(act, w, x1, g, m, tm, tk):
    kdim, d = w.shape
    return pl.pallas_call(
        _ffn_out_kernel,
        out_shape=jax.ShapeDtypeStruct((m, d), F32),
        grid=(m // tm, kdim // tk),
        in_specs=[
            pl.BlockSpec((tm, tk), lambda i, k: (i, k)),
            pl.BlockSpec((tk, d), lambda i, k: (k, 0)),
            pl.BlockSpec((tm, d), lambda i, k: (i, 0)),
            pl.BlockSpec((1, d), lambda i, k: (0, 0)),
        ],
        out_specs=pl.BlockSpec((tm, d), lambda i, k: (i, 0)),
        compiler_params=pltpu.CompilerParams(
            dimension_semantics=("parallel", "arbitrary"), vmem_limit_bytes=VMEM_LIMIT),
        name="ffn_out",
    )(act, w, x1, g)


def _delta_kernel(qkvz_ref, ba_ref, prev_ref, s0_ref, cw_ref, alog_ref, dtb_ref, on_ref,
                  o_ref, s_ref, carry_ref, *, C, NB):
    R = NB * C
    ci = pl.program_id(1)

    @pl.when(ci == 0)
    def _():
        if NB == 1:
            carry_ref[...] = prev_ref[...]
        s_ref[...] = s0_ref[...]

    tri, ones_seq = _seq_masks(R, C)

    ba = ba_ref[...]
    beta_full = _sigmoid(ba)
    g_full = -jnp.exp(alog_ref[...]) * _softplus(ba + dtb_ref[...])
    gc_full = _xdot_r(tri, g_full, 3)
    gtot_full = gc_full[C - 1:C, :] if NB == 1 else _xdot_r(ones_seq, g_full, 3)

    def conv_silu(c0):
        x = qkvz_ref[:, c0:c0 + LANE]
        if NB == 1:
            hist = carry_ref[:, c0:c0 + LANE]
        else:
            hist = [_repeat_rows(prev_ref[j, :, c0:c0 + LANE], SUB) for j in range(3)]
        y = x * cw_ref[3:4, c0:c0 + LANE]
        for k in (1, 2, 3):
            y = y + _delayed(x, hist, k, NB) * cw_ref[3 - k:4 - k, c0:c0 + LANE]
        return _silu(y)

    ones_ll = jnp.ones((LANE, LANE), BF16)

    def lane_sum(x):
        return jnp.dot(x.astype(BF16), ones_ll, preferred_element_type=F32)

    def l2n_many(xs):
        sums = [lane_sum(x * x) for x in xs]
        return [x * lax.rsqrt(sq + 1e-6) for x, sq in zip(xs, sums)]

    def seq_rows(x, n):
        return x[n * C:(n + 1) * C]

    heads = range(H_A)
    seqs = range(NB)
    q = [x * (HEAD_A ** -0.5) for x in l2n_many([conv_silu(h * HEAD_A) for h in heads])]
    k = l2n_many([conv_silu(W_A + h * HEAD_A) for h in heads])
    v = [conv_silu(2 * W_A + h * HEAD_A) for h in heads]
    bcol = [beta_full[:, h:h + 1] for h in heads]
    gcol = [gc_full[:, H_A + h:H_A + h + 1] for h in heads]
    gtot = [gtot_full[:, H_A + h:H_A + h + 1] for h in heads]
    eg = [jnp.exp(gcol[h]) for h in heads]
    kb = [k[h] * bcol[h] for h in heads]

    pk = _Packed(R, C)
    hpairs = range(H_A // 2)
    zc = jnp.zeros((R, HEAD_A), F32)
    row2 = _iota((2 * R, LANE), 0)
    lane2 = _iota((2 * R, LANE), 1)
    ones_cl = jnp.ones((R, LANE), F32)
    gc2 = jnp.concatenate([gc_full, gc_full], axis=0)
    kq, gamma = [], []
    for hp in hpairs:
        h0, h1 = 2 * hp, 2 * hp + 1
        lhs = jnp.concatenate([jnp.concatenate([kb[h0], kb[h1]], axis=1),
                               jnp.concatenate([q[h0], q[h1]], axis=1)], axis=0)
        rk = jnp.concatenate([jnp.concatenate([k[h0], zc], axis=1),
                              jnp.concatenate([zc, k[h1]], axis=1)], axis=0)
        kq.append(_bdot_nt(lhs, rk))
        sel = lane2 == jnp.where(row2 < R, H_A + h0, H_A + h1)
        grow = _xdot_r(ones_cl, jnp.where(sel, gc2, 0.0), 3, NT_DIMS)
        gcol_p = jnp.where(pk.first, gcol[h0], gcol[h1])
        gamma.append(jnp.exp(jnp.where(pk.incl, gcol_p - grow, NEG_BIG)))
    tinv = pk.inverse_many([-jnp.where(pk.strict, kq[hp][0:R] * gamma[hp], 0.0) for hp in hpairs])
    uw = []
    for hp in hpairs:
        h0, h1 = 2 * hp, 2 * hp + 1
        z2 = jnp.zeros((R, 2 * HEAD_A), F32)
        rhs = jnp.concatenate([jnp.concatenate([v[h0] * bcol[h0], kb[h0] * eg[h0], z2], axis=1),
                               jnp.concatenate([z2, v[h1] * bcol[h1], kb[h1] * eg[h1]], axis=1)], axis=0)
        uw.append(_bdot(tinv[hp], rhs))
    u = [uw[h // 2][:, (h % 2) * 2 * HEAD_A:(h % 2) * 2 * HEAD_A + HEAD_A] for h in heads]
    w = [uw[h // 2][:, (h % 2) * 2 * HEAD_A + HEAD_A:(h % 2 + 1) * 2 * HEAD_A] for h in heads]
    qd = [q[h] * eg[h] for h in heads]
    s = [[s_ref[n, h] for h in heads] for n in seqs]
    wqs = [[_bdot(jnp.concatenate([seq_rows(w[h], n), seq_rows(qd[h], n)], axis=0), s[n][h]) for h in heads]
           for n in seqs]
    ws = [jnp.concatenate([wqs[n][h][0:C] for n in seqs], axis=0) if NB > 1 else wqs[0][h][0:C] for h in heads]
    qs = [jnp.concatenate([wqs[n][h][C:2 * C] for n in seqs], axis=0) if NB > 1 else wqs[0][h][C:2 * C]
          for h in heads]
    v_new = [u[h] - ws[h] for h in heads]
    o = []
    for hp in hpairs:
        h0, h1 = 2 * hp, 2 * hp + 1
        vn_bd = jnp.concatenate([jnp.concatenate([v_new[h0], zc], axis=1),
                                 jnp.concatenate([zc, v_new[h1]], axis=1)], axis=0)
        op = _bdot(kq[hp][R:2 * R] * gamma[hp], vn_bd)
        o.append(qs[h0] + op[:, :HEAD_A])
        o.append(qs[h1] + op[:, HEAD_A:])
    kd = [k[h] * jnp.exp(gtot[h] - gcol[h]) for h in heads]
    for n in seqs:
        for h in heads:
            glast = gtot[h] if NB == 1 else gtot[h][n * C:n * C + 1]
            s_ref[n, h] = s[n][h] * jnp.exp(glast) + _bdot_tn(seq_rows(kd[h], n), seq_rows(v_new[h], n))
    osq = [lane_sum(o[h] * o[h]) for h in heads]
    for h in heads:
        z = qkvz_ref[:, 3 * W_A + h * HEAD_A:3 * W_A + (h + 1) * HEAD_A]
        oh = o[h] * lax.rsqrt(osq[h] * (1.0 / HEAD_A) + RMS_EPS)
        o_ref[:, h * HEAD_A:(h + 1) * HEAD_A] = (oh * on_ref[...] * _silu(z)).astype(BF16)

    if NB == 1:
        carry_ref[...] = qkvz_ref[C - SUB:C, 0:3 * W_A]


def _delta_mixer(proj, row0, nb, seq, C, NB, prev, s0, cw, alog_row, dtb_row, onorm):
    nch = seq // C
    assert NB == 1 or (nch == 1 and C == SUB)
    R = NB * C
    blk0 = row0 // R
    bcast = s0.shape[0] == 1
    bsel = (lambda b: 0) if bcast else (lambda b: b)
    return pl.pallas_call(
        functools.partial(_delta_kernel, C=C, NB=NB),
        out_shape=(jax.ShapeDtypeStruct((nb * seq, W_A), BF16),
                   jax.ShapeDtypeStruct((nb, H_A, HEAD_A, HEAD_A), F32)),
        grid=(nb // NB, nch),
        in_specs=[
            pl.BlockSpec((R, 4 * W_A), lambda b, c: (blk0 + b * nch + c, 0)),
            pl.BlockSpec((R, LANE), lambda b, c: (blk0 + b * nch + c, SM0 // LANE)),
            (pl.BlockSpec((SUB, 3 * W_A), lambda b, c: (0, 0)) if NB == 1
             else pl.BlockSpec((3, NB, 3 * W_A), lambda b, c: (0, b, 0))),
            pl.BlockSpec((NB, H_A, HEAD_A, HEAD_A), lambda b, c: (bsel(b), 0, 0, 0)),
            pl.BlockSpec((4, 3 * W_A), lambda b, c: (0, 0)),
            pl.BlockSpec((1, LANE), lambda b, c: (0, 0)),
            pl.BlockSpec((1, LANE), lambda b, c: (0, 0)),
            pl.BlockSpec((1, HEAD_A), lambda b, c: (0, 0)),
        ],
        out_specs=(pl.BlockSpec((R, W_A), lambda b, c: (b * nch + c, 0)),
                   pl.BlockSpec((NB, H_A, HEAD_A, HEAD_A), lambda b, c: (b, 0, 0, 0))),
        scratch_shapes=[pltpu.VMEM((SUB, 3 * W_A), F32)],
        compiler_params=pltpu.CompilerParams(
            dimension_semantics=("parallel", "arbitrary"), vmem_limit_bytes=VMEM_LIMIT),
        name="delta_mixer",
    )(proj, proj, prev, s0, cw, alog_row, dtb_row, onorm)


def _delta_seq_kernel(qkv0_ref, ba0_ref, qkvn_ref, ban_ref, z_ref, prev_ref, s0_ref, cw_ref, alog_ref, dtb_ref,
                      on_ref, o_ref, s_ref, carry_ref, pre_ref, small_ref, gtot_ref, *, C):
    ci = pl.program_id(1)
    heads = range(H_A)
    hpairs = range(H_A // 2)
    tri, _ = _seq_masks(C, C)
    ones_ll = jnp.ones((LANE, LANE), BF16)

    def lane_sum(x):
        return jnp.dot(x.astype(BF16), ones_ll, preferred_element_type=F32)

    def preamble(qkv_ref, ba_ref, hist_ref, slot):
        ba = ba_ref[...]
        g_full = -jnp.exp(alog_ref[...]) * _softplus(ba + dtb_ref[...])
        gc_full = _xdot_r(tri, g_full, 3)
        small_ref[slot, 0] = _sigmoid(ba)
        small_ref[slot, 1] = gc_full
        gtot_ref[slot] = jnp.broadcast_to(gc_full[C - 1:C, :], (SUB, LANE))
        yield
        for part in range(3):
            xs = []
            for h in heads:
                c0 = part * W_A + h * HEAD_A
                x = qkv_ref[:, c0:c0 + LANE]
                hist = hist_ref[:, c0:c0 + LANE]
                y = x * cw_ref[3:4, c0:c0 + LANE]
                for k in (1, 2, 3):
                    y = y + _shifted(x, hist, k) * cw_ref[3 - k:4 - k, c0:c0 + LANE]
                xs.append(_silu(y))
                yield
            if part < 2:
                sums = [lane_sum(x * x) for x in xs]
                yield
                scale = HEAD_A ** -0.5 if part == 0 else 1.0
                xs = [x * (lax.rsqrt(sq + 1e-6) * scale) for x, sq in zip(xs, sums)]
            for h in heads:
                pre_ref[slot, part, :, h * HEAD_A:(h + 1) * HEAD_A] = xs[h]
            yield
        carry_ref[...] = qkv_ref[C - SUB:C, :]

    @pl.when(ci == 0)
    def _():
        for _ in preamble(qkv0_ref, ba0_ref, prev_ref, 0):
            pass
        s_ref[...] = s0_ref[...]

    slot = ci % 2
    ahead = preamble(qkvn_ref, ban_ref, carry_ref, 1 - slot)
    calls = [0]

    def tick():
        calls[0] += 1
        if calls[0] % 3 == 0:
            next(ahead, None)

    def each(fn, items):
        out = {}
        for it in items:
            out[it] = fn(it)
            tick()
        return out

    beta_full = small_ref[slot, 0]
    gc_full = small_ref[slot, 1]
    q = {h: pre_ref[slot, 0, :, h * HEAD_A:(h + 1) * HEAD_A] for h in heads}
    k = {h: pre_ref[slot, 1, :, h * HEAD_A:(h + 1) * HEAD_A] for h in heads}
    v = {h: pre_ref[slot, 2, :, h * HEAD_A:(h + 1) * HEAD_A] for h in heads}
    bcol = {h: beta_full[:, h:h + 1] for h in heads}
    gcol = {h: gc_full[:, H_A + h:H_A + h + 1] for h in heads}
    gtot_row = gtot_ref[slot]
    gtot = {h: gtot_row[0:1, H_A + h:H_A + h + 1] for h in heads}
    eg = {h: jnp.exp(gcol[h]) for h in heads}
    kb = {h: k[h] * bcol[h] for h in heads}

    pk = _Packed(C, C)
    zc = jnp.zeros((C, HEAD_A), F32)
    row2 = _iota((2 * C, LANE), 0)
    lane2 = _iota((2 * C, LANE), 1)
    ones_cl = jnp.ones((C, LANE), F32)
    gc2 = jnp.concatenate([gc_full, gc_full], axis=0)

    def kq_of(hp):
        h0, h1 = 2 * hp, 2 * hp + 1
        lhs = jnp.concatenate([jnp.concatenate([kb[h0], kb[h1]], axis=1),
                               jnp.concatenate([q[h0], q[h1]], axis=1)], axis=0)
        rk = jnp.concatenate([jnp.concatenate([k[h0], zc], axis=1),
                              jnp.concatenate([zc, k[h1]], axis=1)], axis=0)
        return _bdot_nt(lhs, rk)

    def gamma_of(hp):
        h0, h1 = 2 * hp, 2 * hp + 1
        sel = lane2 == jnp.where(row2 < C, H_A + h0, H_A + h1)
        grow = _xdot_r(ones_cl, jnp.where(sel, gc2, 0.0), 3, NT_DIMS)
        gcol_p = jnp.where(pk.first, gcol[h0], gcol[h1])
        return jnp.exp(jnp.where(pk.incl, gcol_p - grow, NEG_BIG))

    kq = each(kq_of, hpairs)
    gamma = each(gamma_of, hpairs)
    ps = {hp: -jnp.where(pk.strict, kq[hp][0:C] * gamma[hp], 0.0) for hp in hpairs}
    ts = {hp: pk.eye + ps[hp] for hp in hpairs}
    n = 1
    while 2 * n < C:
        ps = each(lambda hp: _bdot(ps[hp], pk.block_diag(ps[hp])), hpairs)
        ts = each(lambda hp: ts[hp] + _bdot(ps[hp], pk.block_diag(ts[hp])), hpairs)
        n *= 2

    def uw_of(hp):
        h0, h1 = 2 * hp, 2 * hp + 1
        z2 = jnp.zeros((C, 2 * HEAD_A), F32)
        rhs = jnp.concatenate([jnp.concatenate([v[h0] * bcol[h0], kb[h0] * eg[h0], z2], axis=1),
                               jnp.concatenate([z2, v[h1] * bcol[h1], kb[h1] * eg[h1]], axis=1)], axis=0)
        return _bdot(ts[hp], rhs)

    uw = each(uw_of, hpairs)
    u = {h: uw[h // 2][:, (h % 2) * 2 * HEAD_A:(h % 2) * 2 * HEAD_A + HEAD_A] for h in heads}
    w = {h: uw[h // 2][:, (h % 2) * 2 * HEAD_A + HEAD_A:(h % 2 + 1) * 2 * HEAD_A] for h in heads}
    s = {h: s_ref[0, h] for h in heads}
    wqs = each(lambda h: _bdot(jnp.concatenate([w[h], q[h] * eg[h]], axis=0), s[h]), heads)
    v_new = {h: u[h] - wqs[h][0:C] for h in heads}

    def o_of(hp):
        h0, h1 = 2 * hp, 2 * hp + 1
        vn_bd = jnp.concatenate([jnp.concatenate([v_new[h0], zc], axis=1),
                                 jnp.concatenate([zc, v_new[h1]], axis=1)], axis=0)
        return _bdot(kq[hp][C:2 * C] * gamma[hp], vn_bd)

    op = each(o_of, hpairs)
    o = {h: wqs[h][C:2 * C] + op[h // 2][:, (h % 2) * HEAD_A:(h % 2 + 1) * HEAD_A] for h in heads}
    for h in heads:
        kd = k[h] * jnp.exp(gtot[h] - gcol[h])
        s_ref[0, h] = s[h] * jnp.exp(gtot[h]) + _bdot_tn(kd, v_new[h])
        tick()
    osq = each(lambda h: lane_sum(o[h] * o[h]), heads)
    for h in heads:
        z = z_ref[:, h * HEAD_A:(h + 1) * HEAD_A]
        oh = o[h] * lax.rsqrt(osq[h] * (1.0 / HEAD_A) + RMS_EPS)
        o_ref[:, h * HEAD_A:(h + 1) * HEAD_A] = (oh * on_ref[...] * _silu(z)).astype(BF16)
        tick()
    for _ in ahead:
        pass


def _delta_seq_mixer(proj, nb, seq, C, prev, s0, cw, alog_row, dtb_row, onorm):
    nch = seq // C
    nxt = lambda b, c: b * nch + jnp.minimum(c + 1, nch - 1)
    return pl.pallas_call(
        functools.partial(_delta_seq_kernel, C=C),
        out_shape=(jax.ShapeDtypeStruct((nb * seq, W_A), BF16),
                   jax.ShapeDtypeStruct((nb, H_A, HEAD_A, HEAD_A), F32)),
        grid=(nb, nch),
        in_specs=[
            pl.BlockSpec((C, 3 * W_A), lambda b, c: (b * nch, 0)),
            pl.BlockSpec((C, LANE), lambda b, c: (b * nch, SM0 // LANE)),
            pl.BlockSpec((C, 3 * W_A), lambda b, c: (nxt(b, c), 0)),
            pl.BlockSpec((C, LANE), lambda b, c: (nxt(b, c), SM0 // LANE)),
            pl.BlockSpec((C, W_A), lambda b, c: (b * nch + c, 3)),
            pl.BlockSpec((SUB, 3 * W_A), lambda b, c: (0, 0)),
            pl.BlockSpec((1, H_A, HEAD_A, HEAD_A), lambda b, c: (0, 0, 0, 0)),
            pl.BlockSpec((4, 3 * W_A), lambda b, c: (0, 0)),
            pl.BlockSpec((1, LANE), lambda b, c: (0, 0)),
            pl.BlockSpec((1, LANE), lambda b, c: (0, 0)),
            pl.BlockSpec((1, HEAD_A), lambda b, c: (0, 0)),
        ],
        out_specs=(pl.BlockSpec((C, W_A), lambda b, c: (b * nch + c, 0)),
                   pl.BlockSpec((1, H_A, HEAD_A, HEAD_A), lambda b, c: (b, 0, 0, 0))),
        scratch_shapes=[pltpu.VMEM((SUB, 3 * W_A), F32), pltpu.VMEM((2, 3, C, W_A), F32),
                        pltpu.VMEM((2, 2, C, LANE), F32), pltpu.VMEM((2, SUB, LANE), F32)],
        compiler_params=pltpu.CompilerParams(
            dimension_semantics=("parallel", "arbitrary"), vmem_limit_bytes=VMEM_LIMIT),
        name="delta_seq_mixer",
    )(proj, proj, proj, proj, proj, prev, s0, cw, alog_row, dtb_row, onorm)


def _delta_multi_kernel(qkv0_ref, ba0_ref, qkvn_ref, ban_ref, z_ref, prev_ref, s0_ref, cw_ref, alog_ref, dtb_ref,
                        on_ref, o_ref, s_ref, carry_ref, pre_ref, small_ref, gtot_ref, *, C, NS):
    ci = pl.program_id(1)
    heads = [(s, h) for s in range(NS) for h in range(H_A)]
    hpairs = [(s, hp) for s in range(NS) for hp in range(H_A // 2)]
    tri, _ = _seq_masks(C, C)
    ones_ll = jnp.ones((LANE, LANE), BF16)

    def lane_sum(x):
        return jnp.dot(x.astype(BF16), ones_ll, preferred_element_type=F32)

    def preamble(qkv_ref, ba_ref, hist_of, slot):
        for s in range(NS):
            ba = ba_ref[s]
            g_full = -jnp.exp(alog_ref[...]) * _softplus(ba + dtb_ref[...])
            gc_full = _xdot_r(tri, g_full, 3)
            small_ref[slot, s, 0] = _sigmoid(ba)
            small_ref[slot, s, 1] = gc_full
            gtot_ref[slot, s] = jnp.broadcast_to(gc_full[C - 1:C, :], (SUB, LANE))
            yield
        for part in range(3):
            xs = {}
            for s, h in heads:
                c0 = part * W_A + h * HEAD_A
                x = qkv_ref[s, :, c0:c0 + LANE]
                hist = hist_of(s, c0)
                y = x * cw_ref[3:4, c0:c0 + LANE]
                for k in (1, 2, 3):
                    y = y + _shifted(x, hist, k) * cw_ref[3 - k:4 - k, c0:c0 + LANE]
                xs[s, h] = _silu(y)
                yield
            if part < 2:
                sums = {key: lane_sum(x * x) for key, x in xs.items()}
                yield
                scale = HEAD_A ** -0.5 if part == 0 else 1.0
                xs = {key: x * (lax.rsqrt(sums[key] + 1e-6) * scale) for key, x in xs.items()}
            for s, h in heads:
                pre_ref[slot, s, part, :, h * HEAD_A:(h + 1) * HEAD_A] = xs[s, h]
            yield
        for s in range(NS):
            carry_ref[s] = qkv_ref[s, C - SUB:C, :]

    @pl.when(ci == 0)
    def _():
        for _ in preamble(qkv0_ref, ba0_ref, lambda s, c0: prev_ref[:, c0:c0 + LANE], 0):
            pass
        for s in range(NS):
            s_ref[s, 0] = s0_ref[...]

    slot = ci % 2
    ahead = preamble(qkvn_ref, ban_ref, lambda s, c0: carry_ref[s, :, c0:c0 + LANE], 1 - slot)
    calls = [0]

    def tick():
        calls[0] += 1
        if calls[0] % 3 == 0:
            next(ahead, None)

    def each(fn, items):
        out = {}
        for it in items:
            out[it] = fn(*it)
            tick()
        return out

    beta_full = {s: small_ref[slot, s, 0] for s in range(NS)}
    gc_full = {s: small_ref[slot, s, 1] for s in range(NS)}
    gtot_row = {s: gtot_ref[slot, s] for s in range(NS)}
    q = {(s, h): pre_ref[slot, s, 0, :, h * HEAD_A:(h + 1) * HEAD_A] for s, h in heads}
    k = {(s, h): pre_ref[slot, s, 1, :, h * HEAD_A:(h + 1) * HEAD_A] for s, h in heads}
    v = {(s, h): pre_ref[slot, s, 2, :, h * HEAD_A:(h + 1) * HEAD_A] for s, h in heads}
    bcol = {(s, h): beta_full[s][:, h:h + 1] for s, h in heads}
    gcol = {(s, h): gc_full[s][:, H_A + h:H_A + h + 1] for s, h in heads}
    gtot = {(s, h): gtot_row[s][0:1, H_A + h:H_A + h + 1] for s, h in heads}
    eg = {key: jnp.exp(gcol[key]) for key in heads}
    kb = {key: k[key] * bcol[key] for key in heads}

    pk = _Packed(C, C)
    zc = jnp.zeros((C, HEAD_A), F32)
    row2 = _iota((2 * C, LANE), 0)
    lane2 = _iota((2 * C, LANE), 1)
    ones_cl = jnp.ones((C, LANE), F32)

    def kq_of(s, hp):
        a, b = (s, 2 * hp), (s, 2 * hp + 1)
        lhs = jnp.concatenate([jnp.concatenate([kb[a], kb[b]], axis=1),
                               jnp.concatenate([q[a], q[b]], axis=1)], axis=0)
        rk = jnp.concatenate([jnp.concatenate([k[a], zc], axis=1),
                              jnp.concatenate([zc, k[b]], axis=1)], axis=0)
        return _bdot_nt(lhs, rk)

    def gamma_of(s, hp):
        a, b = (s, 2 * hp), (s, 2 * hp + 1)
        sel = lane2 == jnp.where(row2 < C, H_A + 2 * hp, H_A + 2 * hp + 1)
        gc2 = jnp.concatenate([gc_full[s], gc_full[s]], axis=0)
        grow = _xdot_r(ones_cl, jnp.where(sel, gc2, 0.0), 3, NT_DIMS)
        gcol_p = jnp.where(pk.first, gcol[a], gcol[b])
        return jnp.exp(jnp.where(pk.incl, gcol_p - grow, NEG_BIG))

    kq = each(kq_of, hpairs)
    gamma = each(gamma_of, hpairs)
    ps = {key: -jnp.where(pk.strict, kq[key][0:C] * gamma[key], 0.0) for key in hpairs}
    ts = {key: pk.eye + ps[key] for key in hpairs}
    n = 1
    while 2 * n < C:
        ps = each(lambda s, hp: _bdot(ps[s, hp], pk.block_diag(ps[s, hp])), hpairs)
        ts = each(lambda s, hp: ts[s, hp] + _bdot(ps[s, hp], pk.block_diag(ts[s, hp])), hpairs)
        n *= 2

    def uw_of(s, hp):
        a, b = (s, 2 * hp), (s, 2 * hp + 1)
        z2 = jnp.zeros((C, 2 * HEAD_A), F32)
        rhs = jnp.concatenate([jnp.concatenate([v[a] * bcol[a], kb[a] * eg[a], z2], axis=1),
                               jnp.concatenate([z2, v[b] * bcol[b], kb[b] * eg[b]], axis=1)], axis=0)
        return _bdot(ts[s, hp], rhs)

    uw = each(uw_of, hpairs)
    u = {(s, h): uw[s, h // 2][:, (h % 2) * 2 * HEAD_A:(h % 2) * 2 * HEAD_A + HEAD_A] for s, h in heads}
    w = {(s, h): uw[s, h // 2][:, (h % 2) * 2 * HEAD_A + HEAD_A:(h % 2 + 1) * 2 * HEAD_A] for s, h in heads}
    st = {(s, h): s_ref[s, 0, h] for s, h in heads}
    wqs = each(lambda s, h: _bdot(jnp.concatenate([w[s, h], q[s, h] * eg[s, h]], axis=0), st[s, h]), heads)
    v_new = {key: u[key] - wqs[key][0:C] for key in heads}

    def o_of(s, hp):
        a, b = (s, 2 * hp), (s, 2 * hp + 1)
        vn_bd = jnp.concatenate([jnp.concatenate([v_new[a], zc], axis=1),
                                 jnp.concatenate([zc, v_new[b]], axis=1)], axis=0)
        return _bdot(kq[s, hp][C:2 * C] * gamma[s, hp], vn_bd)

    op = each(o_of, hpairs)
    o = {(s, h): wqs[s, h][C:2 * C] + op[s, h // 2][:, (h % 2) * HEAD_A:(h % 2 + 1) * HEAD_A] for s, h in heads}
    for s, h in heads:
        kd = k[s, h] * jnp.exp(gtot[s, h] - gcol[s, h])
        s_ref[s, 0, h] = st[s, h] * jnp.exp(gtot[s, h]) + _bdot_tn(kd, v_new[s, h])
        tick()
    osq = each(lambda s, h: lane_sum(o[s, h] * o[s, h]), heads)
    for s, h in heads:
        z = z_ref[s, :, h * HEAD_A:(h + 1) * HEAD_A]
        oh = o[s, h] * lax.rsqrt(osq[s, h] * (1.0 / HEAD_A) + RMS_EPS)
        o_ref[s, :, h * HEAD_A:(h + 1) * HEAD_A] = (oh * on_ref[...] * _silu(z)).astype(BF16)
        tick()
    for _ in ahead:
        pass


def _delta_multi_mixer(proj, nb, seq, C, NS, prev, s0, cw, alog_row, dtb_row, onorm):
    nch = seq // C
    ng = nb // NS
    proj3 = proj.reshape(NS, ng * seq, P_CAT)
    nxt = lambda i, c: i * nch + jnp.minimum(c + 1, nch - 1)
    oa, sd = pl.pallas_call(
        functools.partial(_delta_multi_kernel, C=C, NS=NS),
        out_shape=(jax.ShapeDtypeStruct((NS, ng * seq, W_A), BF16),
                   jax.ShapeDtypeStruct((NS, ng, H_A, HEAD_A, HEAD_A), F32)),
        grid=(ng, nch),
        in_specs=[
            pl.BlockSpec((NS, C, 3 * W_A), lambda i, c: (0, i * nch, 0)),
            pl.BlockSpec((NS, C, LANE), lambda i, c: (0, i * nch, SM0 // LANE)),
            pl.BlockSpec((NS, C, 3 * W_A), lambda i, c: (0, nxt(i, c), 0)),
            pl.BlockSpec((NS, C, LANE), lambda i, c: (0, nxt(i, c), SM0 // LANE)),
            pl.BlockSpec((NS, C, W_A), lambda i, c: (0, i * nch + c, 3)),
            pl.BlockSpec((SUB, 3 * W_A), lambda i, c: (0, 0)),
            pl.BlockSpec((None, H_A, HEAD_A, HEAD_A), lambda i, c: (0, 0, 0, 0)),
            pl.BlockSpec((4, 3 * W_A), lambda i, c: (0, 0)),
            pl.BlockSpec((1, LANE), lambda i, c: (0, 0)),
            pl.BlockSpec((1, LANE), lambda i, c: (0, 0)),
            pl.BlockSpec((1, HEAD_A), lambda i, c: (0, 0)),
        ],
        out_specs=(pl.BlockSpec((NS, C, W_A), lambda i, c: (0, i * nch + c, 0)),
                   pl.BlockSpec((NS, 1, H_A, HEAD_A, HEAD_A), lambda i, c: (0, i, 0, 0, 0))),
        scratch_shapes=[pltpu.VMEM((NS, SUB, 3 * W_A), F32), pltpu.VMEM((2, NS, 3, C, W_A), F32),
                        pltpu.VMEM((2, NS, 2, C, LANE), F32), pltpu.VMEM((2, NS, SUB, LANE), F32)],
        compiler_params=pltpu.CompilerParams(
            dimension_semantics=("parallel", "arbitrary"), vmem_limit_bytes=VMEM_LIMIT),
        name="delta_multi_mixer",
    )(proj3, proj3, proj3, proj3, proj3, prev, s0, cw, alog_row, dtb_row, onorm)
    return oa.reshape(nb * seq, W_A), sd.reshape(nb, H_A, HEAD_A, HEAD_A)


def _rwkv_kernel(r_ref, k_ref, v_ref, sm_ref, pr_ref, pk_ref, pv_ref, psm_ref, s0_ref,
                 mur_ref, muk_ref, muv_ref, musm_ref, w2_ref, a2_ref, g2_ref,
                 w0_ref, a0_ref, kk_ref, ka_ref, rk_ref, lnw_ref, lnb_ref,
                 o_ref, s_ref, cr_ref, ck_ref, cv_ref, csm_ref, st_ref, *, C, NB, G):
    R = NB * C
    ci = pl.program_id(2)
    nch = pl.num_programs(2)
    seqs = range(NB)
    pairs = range(G)

    r2 = _iota((LANE, LANE), 0)
    c2 = _iota((LANE, LANE), 1)
    same_head = (r2 < HEAD_B) == (c2 < HEAD_B)
    ones_bd = same_head.astype(F32)
    spread = (_iota((HEAD_B, LANE), 0) == (_iota((HEAD_B, LANE), 1) & (HEAD_B - 1))).astype(F32)
    gather = ((_iota((LANE, HEAD_B), 0) & (HEAD_B - 1)) == _iota((LANE, HEAD_B), 1)).astype(F32)

    @pl.when(ci == 0)
    def _():
        if NB == 1:
            cr_ref[...] = pr_ref[...]
            ck_ref[...] = pk_ref[...]
            cv_ref[...] = pv_ref[...]
            csm_ref[...] = psm_ref[...]
        for n in seqs:
            for p in pairs:
                st_ref[n * G + p] = jnp.where(same_head, _xdot(s0_ref[n, p], spread, 3), 0.0)

    hr_ref, hk_ref, hv_ref, hsm_ref = ((cr_ref, ck_ref, cv_ref, csm_ref) if NB == 1
                                       else (pr_ref, pk_ref, pv_ref, psm_ref))
    tri, ones_seq = _seq_masks(R, C)

    def lerp(x, hist_ref, mu):
        hist = hist_ref[...] if NB == 1 else [_repeat_rows(hist_ref[0], SUB)]
        return x + (_delayed(x, hist, 1, NB) - x) * mu

    sm = sm_ref[...]
    xs = lerp(sm, hsm_ref, musm_ref[...])
    slab_wa = xs[:, SM_WA:SM_WA + LANE]
    slab_g = xs[:, SM_G:SM_G + 2 * LANE]

    def col(x, p):
        return x[:, p * LANE:(p + 1) * LANE]

    def to_rows(x):
        return jnp.concatenate([col(x, p) for p in pairs], axis=0)

    def to_cols(y):
        return jnp.concatenate([y[p * R:(p + 1) * R] for p in pairs], axis=1)

    def seq_rows(x, n):
        return x[n * C:(n + 1) * C]

    xr = lerp(r_ref[...], hr_ref, mur_ref[...])
    xk = lerp(k_ref[...], hk_ref, muk_ref[...])
    xv = lerp(v_ref[...], hv_ref, muv_ref[...])
    wlog = -_softplus(-(w0_ref[...] + _bdot(jnp.tanh(slab_wa), w2_ref[...]))) - 0.5
    ld = -jnp.exp(wlog)
    aa = _sigmoid(a0_ref[...] + _bdot(slab_wa, a2_ref[...]))
    gate = _bdot(_sigmoid(slab_g), g2_ref[...])
    kkr = xk * kk_ref[...]
    kkn = kkr * lax.rsqrt(to_cols(_bdot(to_rows(kkr * kkr), ones_bd)) + 1e-6)
    k2 = xk * (1.0 + (aa - 1.0) * ka_ref[...])
    lp = _xdot_r(tri, ld, 3)
    lp_tot = lp[C - 1:C, :] if NB == 1 else _xdot_r(ones_seq, ld, 3)
    e_neg = jnp.exp(-lp)
    e_rem = jnp.exp(lp_tot - lp)
    at = -kkn * jnp.exp(lp - ld)
    rt = xr * jnp.exp(lp)
    kb = kkn * aa
    bt = kb * e_neg
    kt = k2 * e_neg
    bhat = kb * e_rem
    khat = k2 * e_rem
    p_tot = jnp.exp(lp_tot)

    s = [[st_ref[n * G + p] for p in pairs] for n in seqs]
    atp = [col(at, p) for p in pairs]
    rtp = [col(rt, p) for p in pairs]
    ars = [[_bdot_nt(jnp.concatenate([seq_rows(atp[p], n), seq_rows(rtp[p], n)], axis=0), s[n][p])
            for p in pairs] for n in seqs]
    if NB == 1:
        x_state = [ars[0][p][0:C] for p in pairs]
        o_state = [ars[0][p][C:2 * C] for p in pairs]
    else:
        x_state = [jnp.concatenate([ars[n][p][0:C] for n in seqs], axis=0) for p in pairs]
        o_state = [jnp.concatenate([ars[n][p][C:2 * C] for n in seqs], axis=0) for p in pairs]
    pk = _Packed(R, C)
    stack_mask = (_iota((2 * R, LANE), 0) < R) == (_iota((2 * R, LANE), 1) < HEAD_B)

    def by_head(x):
        return jnp.where(stack_mask, jnp.concatenate([x, x], axis=0), 0.0)

    ar = [jnp.concatenate([atp[p], rtp[p]], axis=0) for p in pairs]
    ab = [_bdot_nt(ar[p], by_head(col(bt, p))) for p in pairs]
    ak = [_bdot_nt(ar[p], by_head(col(kt, p))) for p in pairs]
    tinv = pk.inverse_many([jnp.where(pk.strict, m[0:R], 0.0) for m in ab])
    v_bh = [by_head(col(xv, p)) for p in pairs]
    y = [x_state[p] + _bdot(jnp.where(pk.strict, ak[p][0:R], 0.0), v_bh[p]) for p in pairs]
    u = [_bdot(tinv[p], by_head(y[p])) for p in pairs]
    o = [o_state[p] + _bdot(jnp.where(pk.incl, ab[p][R:2 * R], 0.0), by_head(u[p]))
         + _bdot(jnp.where(pk.incl, ak[p][R:2 * R], 0.0), v_bh[p]) for p in pairs]
    for n in seqs:
        for p in pairs:
            uv = jnp.concatenate([seq_rows(u[p], n), seq_rows(col(xv, p), n)], axis=0)
            bkhat = jnp.concatenate([seq_rows(col(bhat, p), n), seq_rows(col(khat, p), n)], axis=0)
            decay = col(p_tot, p) if NB == 1 else col(p_tot, p)[n * C:n * C + 1]
            s_new = s[n][p] * decay + _bdot_tn(uv, bkhat)
            st_ref[n * G + p] = jnp.where(same_head, s_new, 0.0)

    o_rows = jnp.concatenate(o, axis=0)
    mean = _bdot(o_rows, ones_bd) * (1.0 / HEAD_B)
    d = o_rows - mean
    var = _bdot(d * d, ones_bd) * (1.0 / HEAD_B)
    on = to_cols(d * lax.rsqrt(var + GN_EPS)) * lnw_ref[...] + lnb_ref[...]
    bonus = to_cols(_bdot(to_rows(xr * k2 * rk_ref[...]), ones_bd)) * xv
    o_ref[...] = ((on + bonus) * gate).astype(BF16)

    if NB == 1:
        cr_ref[...] = r_ref[C - SUB:C, :]
        ck_ref[...] = k_ref[C - SUB:C, :]
        cv_ref[...] = v_ref[C - SUB:C, :]
        csm_ref[...] = sm_ref[C - SUB:C, :]

    @pl.when(ci == nch - 1)
    def _():
        for n in seqs:
            for p in pairs:
                s_ref[n, p] = _xdot(st_ref[n * G + p], gather, 3)


def _rwkv_mixer(proj, row0, nb, seq, C, NB, G, prev_rkv, prev_sm, s0, mu_rkv, mu_sm, w2p, a2p, g2p,
                w0, a0, k_k, k_a, r_k, lnw, lnb):
    nch = seq // C
    assert NB == 1 or (nch == 1 and C == SUB)
    R = NB * C
    ng = N_PAIR // G
    gw = G * LANE
    blk0 = row0 // R
    bcast = s0.shape[0] == 1
    bsel = (lambda b: 0) if bcast else (lambda b: b)

    def proj_spec(col0):
        return pl.BlockSpec((R, gw), lambda b, g, c: (blk0 + b * nch + c, col0 // gw + g))

    def prev_spec(part):
        if NB == 1:
            return pl.BlockSpec((SUB, gw), lambda b, g, c: (0, part * (W_B // gw) + g))
        return pl.BlockSpec((1, NB, gw), lambda b, g, c: (0, b, part * (W_B // gw) + g))

    def vec_spec(part=0):
        return pl.BlockSpec((1, gw), lambda b, g, c: (0, part * (W_B // gw) + g))

    in_specs = [
        proj_spec(RKV0), proj_spec(RKV0 + W_B), proj_spec(RKV0 + 2 * W_B),
        pl.BlockSpec((R, SM_W), lambda b, g, c: (blk0 + b * nch + c, SM0 // SM_W)),
        prev_spec(0), prev_spec(1), prev_spec(2),
        (pl.BlockSpec((SUB, SM_W), lambda b, g, c: (0, 0)) if NB == 1
         else pl.BlockSpec((1, NB, SM_W), lambda b, g, c: (0, b, 0))),
        pl.BlockSpec((NB, G, LANE, HEAD_B), lambda b, g, c: (bsel(b), g, 0, 0)),
        vec_spec(0), vec_spec(1), vec_spec(2),
        pl.BlockSpec((1, SM_W), lambda b, g, c: (0, 0)),
        pl.BlockSpec((LANE, gw), lambda b, g, c: (0, g)),
        pl.BlockSpec((LANE, gw), lambda b, g, c: (0, g)),
        pl.BlockSpec((2 * LANE, gw), lambda b, g, c: (0, g)),
        vec_spec(), vec_spec(), vec_spec(), vec_spec(), vec_spec(), vec_spec(), vec_spec(),
    ]
    return pl.pallas_call(
        functools.partial(_rwkv_kernel, C=C, NB=NB, G=G),
        out_shape=(jax.ShapeDtypeStruct((nb * seq, W_B), BF16),
                   jax.ShapeDtypeStruct((nb, N_PAIR, LANE, HEAD_B), F32)),
        grid=(nb // NB, ng, nch),
        in_specs=in_specs,
        out_specs=(pl.BlockSpec((R, gw), lambda b, g, c: (b * nch + c, g)),
                   pl.BlockSpec((NB, G, LANE, HEAD_B), lambda b, g, c: (b, g, 0, 0))),
        scratch_shapes=[pltpu.VMEM((SUB, gw), F32), pltpu.VMEM((SUB, gw), F32), pltpu.VMEM((SUB, gw), F32),
                        pltpu.VMEM((SUB, SM_W), F32), pltpu.VMEM((NB * G, LANE, LANE), F32)],
        compiler_params=pltpu.CompilerParams(
            dimension_semantics=("parallel", "parallel", "arbitrary"), vmem_limit_bytes=VMEM_LIMIT),
        name="rwkv_mixer",
    )(proj, proj, proj, proj, prev_rkv, prev_rkv, prev_rkv, prev_sm, s0,
      mu_rkv, mu_rkv, mu_rkv, mu_sm, w2p, a2p, g2p, w0, a0, k_k, k_a, r_k, lnw, lnb)


_AT, _RT, _BT, _KT, _BHAT, _KHAT, _XV, _GATE, _BONUS, _N_PRE = range(10)


def _rwkv_seq_kernel(r0_ref, k0_ref, v0_ref, sm0_ref, rn_ref, kn_ref, vn_ref, smn_ref,
                     pr_ref, pk_ref, pv_ref, psm_ref, s0_ref,
                     mur_ref, muk_ref, muv_ref, musm_ref, w2_ref, a2_ref, g2_ref,
                     w0_ref, a0_ref, kk_ref, ka_ref, rk_ref, lnw_ref, lnb_ref,
                     o_ref, s_ref, cr_ref, ck_ref, cv_ref, csm_ref, st_ref, pre_ref, ptot_ref, *, C, G, GRP):
    ci = pl.program_id(2)
    nch = pl.num_programs(2)
    pairs = range(G)

    r2 = _iota((LANE, LANE), 0)
    c2 = _iota((LANE, LANE), 1)
    same_head = (r2 < HEAD_B) == (c2 < HEAD_B)
    ones_bd = same_head.astype(F32)
    spread = (_iota((HEAD_B, LANE), 0) == (_iota((HEAD_B, LANE), 1) & (HEAD_B - 1))).astype(F32)
    gather = ((_iota((LANE, HEAD_B), 0) & (HEAD_B - 1)) == _iota((LANE, HEAD_B), 1)).astype(F32)
    tri, _ = _seq_masks(C, C)

    def col(x, p):
        return x[:, p * LANE:(p + 1) * LANE]

    def to_rows(x):
        return jnp.concatenate([col(x, p) for p in pairs], axis=0)

    def to_cols(y):
        return jnp.concatenate([y[p * C:(p + 1) * C] for p in pairs], axis=1)

    def lerp(x, hist, mu):
        return x + (_shifted(x, hist, 1) - x) * mu

    def preamble(r_ref, k_ref, v_ref, sm_ref, hr_ref, hk_ref, hv_ref, hsm_ref, slot):
        xs = lerp(sm_ref[...], hsm_ref[...], musm_ref[...])
        yield
        slab_wa = xs[:, SM_WA:SM_WA + LANE]
        wl = _bdot(jnp.tanh(slab_wa), w2_ref[...])
        yield
        al = _bdot(slab_wa, a2_ref[...])
        yield
        pre_ref[slot, _GATE] = _bdot(_sigmoid(xs[:, SM_G:SM_G + 2 * LANE]), g2_ref[...])
        yield
        xr = lerp(r_ref[...], hr_ref[...], mur_ref[...])
        yield
        xk = lerp(k_ref[...], hk_ref[...], muk_ref[...])
        yield
        xv = lerp(v_ref[...], hv_ref[...], muv_ref[...])
        pre_ref[slot, _XV] = xv
        yield
        ld = -jnp.exp(-_softplus(-(w0_ref[...] + wl)) - 0.5)
        yield
        lp = _xdot_r(tri, ld, 3)
        yield
        aa = _sigmoid(a0_ref[...] + al)
        yield
        kkr = xk * kk_ref[...]
        ss = to_cols(_bdot(to_rows(kkr * kkr), ones_bd))
        yield
        kkn = kkr * lax.rsqrt(ss + 1e-6)
        k2 = xk * (1.0 + (aa - 1.0) * ka_ref[...])
        yield
        pre_ref[slot, _BONUS] = to_cols(_bdot(to_rows(xr * k2 * rk_ref[...]), ones_bd)) * xv
        yield
        pre_ref[slot, _AT] = -kkn * jnp.exp(lp - ld)
        yield
        pre_ref[slot, _RT] = xr * jnp.exp(lp)
        yield
        kb = kkn * aa
        e_neg = jnp.exp(-lp)
        pre_ref[slot, _BT] = kb * e_neg
        yield
        pre_ref[slot, _KT] = k2 * e_neg
        yield
        lp_tot = lp[C - 1:C, :]
        e_rem = jnp.exp(lp_tot - lp)
        pre_ref[slot, _BHAT] = kb * e_rem
        yield
        pre_ref[slot, _KHAT] = k2 * e_rem
        ptot_ref[slot] = jnp.broadcast_to(jnp.exp(lp_tot), (SUB, lp.shape[1]))
        yield
        cr_ref[...] = r_ref[C - SUB:C, :]
        ck_ref[...] = k_ref[C - SUB:C, :]
        cv_ref[...] = v_ref[C - SUB:C, :]
        csm_ref[...] = sm_ref[C - SUB:C, :]

    @pl.when(ci == 0)
    def _():
        for _ in preamble(r0_ref, k0_ref, v0_ref, sm0_ref, pr_ref, pk_ref, pv_ref, psm_ref, 0):
            pass
        for p in pairs:
            st_ref[p] = jnp.where(same_head, _xdot(s0_ref[0, p], spread, 3), 0.0)

    slot = ci % 2
    ahead = preamble(rn_ref, kn_ref, vn_ref, smn_ref, cr_ref, ck_ref, cv_ref, csm_ref, 1 - slot)
    calls = [0]

    def tick():
        calls[0] += 1
        if calls[0] % 6 == 0:
            next(ahead, None)

    def each(fn, items):
        out = {}
        for it in items:
            out[it] = fn(it)
            tick()
        return out

    def pre(idx, p):
        return pre_ref[slot, idx, :, p * LANE:(p + 1) * LANE]

    pk = _Packed(C, C)
    stack_mask = (_iota((2 * C, LANE), 0) < C) == (_iota((2 * C, LANE), 1) < HEAD_B)

    def by_head(x):
        return jnp.where(stack_mask, jnp.concatenate([x, x], axis=0), 0.0)

    def run_group(grp):
        s = {p: st_ref[p] for p in grp}
        ar = {p: jnp.concatenate([pre(_AT, p), pre(_RT, p)], axis=0) for p in grp}
        ars = each(lambda p: _bdot_nt(ar[p], s[p]), grp)
        ab = each(lambda p: _bdot_nt(ar[p], by_head(pre(_BT, p))), grp)
        ak = each(lambda p: _bdot_nt(ar[p], by_head(pre(_KT, p))), grp)
        ps = {p: jnp.where(pk.strict, ab[p][0:C], 0.0) for p in grp}
        ts = {p: pk.eye + ps[p] for p in grp}
        n = 1
        while 2 * n < C:
            ps = each(lambda p: _bdot(ps[p], pk.block_diag(ps[p])), grp)
            ts = each(lambda p: ts[p] + _bdot(ps[p], pk.block_diag(ts[p])), grp)
            n *= 2
        xv = {p: pre(_XV, p) for p in grp}
        v_bh = {p: by_head(xv[p]) for p in grp}
        y = each(lambda p: ars[p][0:C] + _bdot(jnp.where(pk.strict, ak[p][0:C], 0.0), v_bh[p]), grp)
        u = each(lambda p: _bdot(ts[p], by_head(y[p])), grp)
        o = each(lambda p: ars[p][C:2 * C] + _bdot(jnp.where(pk.incl, ab[p][C:2 * C], 0.0), by_head(u[p]))
                 + _bdot(jnp.where(pk.incl, ak[p][C:2 * C], 0.0), v_bh[p]), grp)
        for p in grp:
            uv = jnp.concatenate([u[p], xv[p]], axis=0)
            bkhat = jnp.concatenate([pre(_BHAT, p), pre(_KHAT, p)], axis=0)
            s_new = s[p] * ptot_ref[slot, 0:1, p * LANE:(p + 1) * LANE] + _bdot_tn(uv, bkhat)
            st_ref[p] = jnp.where(same_head, s_new, 0.0)
            tick()
        return o

    o = {}
    for g0 in range(0, G, GRP):
        o.update(run_group(range(g0, g0 + GRP)))
    o_rows = jnp.concatenate([o[p] for p in pairs], axis=0)
    mean = _bdot(o_rows, ones_bd) * (1.0 / HEAD_B)
    d = o_rows - mean
    var = _bdot(d * d, ones_bd) * (1.0 / HEAD_B)
    on = to_cols(d * lax.rsqrt(var + GN_EPS)) * lnw_ref[...] + lnb_ref[...]
    o_ref[...] = ((on + pre_ref[slot, _BONUS]) * pre_ref[slot, _GATE]).astype(BF16)
    for _ in ahead:
        pass

    @pl.when(ci == nch - 1)
    def _():
        for p in pairs:
            s_ref[0, p] = _xdot(st_ref[p], gather, 3)


def _rwkv_seq_mixer(proj, nb, seq, C, prev_rkv, prev_sm, s0, mu_rkv, mu_sm, w2p, a2p, g2p,
                    w0, a0, k_k, k_a, r_k, lnw, lnb):
    G = N_PAIR
    nch = seq // C
    gw = G * LANE

    def first_spec(col0, w):
        return pl.BlockSpec((C, w), lambda b, g, c: (b * nch, col0 // w))

    def next_spec(col0, w):
        return pl.BlockSpec((C, w), lambda b, g, c: (b * nch + jnp.minimum(c + 1, nch - 1), col0 // w))

    def prev_spec(part):
        return pl.BlockSpec((SUB, gw), lambda b, g, c: (0, part))

    def vec_spec(part=0):
        return pl.BlockSpec((1, gw), lambda b, g, c: (0, part))

    in_specs = [
        first_spec(RKV0, gw), first_spec(RKV0 + W_B, gw), first_spec(RKV0 + 2 * W_B, gw), first_spec(SM0, SM_W),
        next_spec(RKV0, gw), next_spec(RKV0 + W_B, gw), next_spec(RKV0 + 2 * W_B, gw), next_spec(SM0, SM_W),
        prev_spec(0), prev_spec(1), prev_spec(2),
        pl.BlockSpec((SUB, SM_W), lambda b, g, c: (0, 0)),
        pl.BlockSpec((1, G, LANE, HEAD_B), lambda b, g, c: (0, 0, 0, 0)),
        vec_spec(0), vec_spec(1), vec_spec(2),
        pl.BlockSpec((1, SM_W), lambda b, g, c: (0, 0)),
        pl.BlockSpec((LANE, gw), lambda b, g, c: (0, 0)),
        pl.BlockSpec((LANE, gw), lambda b, g, c: (0, 0)),
        pl.BlockSpec((2 * LANE, gw), lambda b, g, c: (0, 0)),
        vec_spec(), vec_spec(), vec_spec(), vec_spec(), vec_spec(), vec_spec(), vec_spec(),
    ]
    return pl.pallas_call(
        functools.partial(_rwkv_seq_kernel, C=C, G=G, GRP=G),
        out_shape=(jax.ShapeDtypeStruct((nb * seq, W_B), BF16),
                   jax.ShapeDtypeStruct((nb, N_PAIR, LANE, HEAD_B), F32)),
        grid=(nb, 1, nch),
        in_specs=in_specs,
        out_specs=(pl.BlockSpec((C, gw), lambda b, g, c: (b * nch + c, 0)),
                   pl.BlockSpec((1, G, LANE, HEAD_B), lambda b, g, c: (b, 0, 0, 0))),
        scratch_shapes=[pltpu.VMEM((SUB, gw), F32), pltpu.VMEM((SUB, gw), F32), pltpu.VMEM((SUB, gw), F32),
                        pltpu.VMEM((SUB, SM_W), F32), pltpu.VMEM((G, LANE, LANE), F32),
                        pltpu.VMEM((2, _N_PRE, C, gw), F32), pltpu.VMEM((2, SUB, gw), F32)],
        compiler_params=pltpu.CompilerParams(
            dimension_semantics=("parallel", "arbitrary", "arbitrary"), vmem_limit_bytes=VMEM_LIMIT),
        name="rwkv_seq_mixer",
    )(proj, proj, proj, proj, proj, proj, proj, proj, prev_rkv, prev_rkv, prev_rkv, prev_sm, s0,
      mu_rkv, mu_rkv, mu_rkv, mu_sm, w2p, a2p, g2p, w0, a0, k_k, k_a, r_k, lnw, lnb)


def _rwkv_multi_kernel(r0_ref, k0_ref, v0_ref, sm0_ref, rn_ref, kn_ref, vn_ref, smn_ref,
                       pr_ref, pk_ref, pv_ref, psm_ref, s0_ref,
                       mur_ref, muk_ref, muv_ref, musm_ref, w2_ref, a2_ref, g2_ref,
                       w0_ref, a0_ref, kk_ref, ka_ref, rk_ref, lnw_ref, lnb_ref,
                       o_ref, s_ref, cr_ref, ck_ref, cv_ref, csm_ref, st_ref, pre_ref, ptot_ref, *, C, G, NS):
    ci = pl.program_id(1)
    nch = pl.num_programs(1)
    pairs = range(G)
    chains = [(s, p) for s in range(NS) for p in pairs]

    r2 = _iota((LANE, LANE), 0)
    c2 = _iota((LANE, LANE), 1)
    same_head = (r2 < HEAD_B) == (c2 < HEAD_B)
    ones_bd = same_head.astype(F32)
    spread = (_iota((HEAD_B, LANE), 0) == (_iota((HEAD_B, LANE), 1) & (HEAD_B - 1))).astype(F32)
    gather = ((_iota((LANE, HEAD_B), 0) & (HEAD_B - 1)) == _iota((LANE, HEAD_B), 1)).astype(F32)
    tri, _ = _seq_masks(C, C)

    def col(x, p):
        return x[:, p * LANE:(p + 1) * LANE]

    def to_rows(x):
        return jnp.concatenate([col(x, p) for p in pairs], axis=0)

    def to_cols(y):
        return jnp.concatenate([y[p * C:(p + 1) * C] for p in pairs], axis=1)

    def lerp(x, hist, mu):
        return x + (_shifted(x, hist, 1) - x) * mu

    def preamble(r_ref, k_ref, v_ref, sm_ref, hist_of, slot):
        for s in range(NS):
            hr, hk, hv, hsm = hist_of(s)
            xs = lerp(sm_ref[s], hsm, musm_ref[...])
            yield
            slab_wa = xs[:, SM_WA:SM_WA + LANE]
            wl = _bdot(jnp.tanh(slab_wa), w2_ref[...])
            yield
            al = _bdot(slab_wa, a2_ref[...])
            yield
            pre_ref[slot, s, _GATE] = _bdot(_sigmoid(xs[:, SM_G:SM_G + 2 * LANE]), g2_ref[...])
            yield
            xr = lerp(r_ref[s], hr, mur_ref[...])
            yield
            xk = lerp(k_ref[s], hk, muk_ref[...])
            yield
            xv = lerp(v_ref[s], hv, muv_ref[...])
            pre_ref[slot, s, _XV] = xv
            yield
            ld = -jnp.exp(-_softplus(-(w0_ref[...] + wl)) - 0.5)
            yield
            lp = _xdot_r(tri, ld, 3)
            yield
            aa = _sigmoid(a0_ref[...] + al)
            yield
            kkr = xk * kk_ref[...]
            ss = to_cols(_bdot(to_rows(kkr * kkr), ones_bd))
            yield
            kkn = kkr * lax.rsqrt(ss + 1e-6)
            k2 = xk * (1.0 + (aa - 1.0) * ka_ref[...])
            yield
            pre_ref[slot, s, _BONUS] = to_cols(_bdot(to_rows(xr * k2 * rk_ref[...]), ones_bd)) * xv
            yield
            pre_ref[slot, s, _AT] = -kkn * jnp.exp(lp - ld)
            yield
            pre_ref[slot, s, _RT] = xr * jnp.exp(lp)
            yield
            kb = kkn * aa
            e_neg = jnp.exp(-lp)
            pre_ref[slot, s, _BT] = kb * e_neg
            yield
            pre_ref[slot, s, _KT] = k2 * e_neg
            yield
            lp_tot = lp[C - 1:C, :]
            e_rem = jnp.exp(lp_tot - lp)
            pre_ref[slot, s, _BHAT] = kb * e_rem
            yield
            pre_ref[slot, s, _KHAT] = k2 * e_rem
            ptot_ref[slot, s] = jnp.broadcast_to(jnp.exp(lp_tot), (SUB, lp.shape[1]))
            yield
            cr_ref[s] = r_ref[s, C - SUB:C, :]
            ck_ref[s] = k_ref[s, C - SUB:C, :]
            cv_ref[s] = v_ref[s, C - SUB:C, :]
            csm_ref[s] = sm_ref[s, C - SUB:C, :]

    @pl.when(ci == 0)
    def _():
        first_hist = lambda s: (pr_ref[...], pk_ref[...], pv_ref[...], psm_ref[...])
        for _ in preamble(r0_ref, k0_ref, v0_ref, sm0_ref, first_hist, 0):
            pass
        for s, p in chains:
            st_ref[s * G + p] = jnp.where(same_head, _xdot(s0_ref[p], spread, 3), 0.0)

    slot = ci % 2
    carried = lambda s: (cr_ref[s], ck_ref[s], cv_ref[s], csm_ref[s])
    ahead = preamble(rn_ref, kn_ref, vn_ref, smn_ref, carried, 1 - slot)
    calls = [0]

    def tick():
        calls[0] += 1
        if calls[0] % 6 == 0:
            next(ahead, None)

    def each(fn, items):
        out = {}
        for it in items:
            out[it] = fn(*it)
            tick()
        return out

    def pre(idx, s, p):
        return pre_ref[slot, s, idx, :, p * LANE:(p + 1) * LANE]

    pk = _Packed(C, C)
    stack_mask = (_iota((2 * C, LANE), 0) < C) == (_iota((2 * C, LANE), 1) < HEAD_B)

    def by_head(x):
        return jnp.where(stack_mask, jnp.concatenate([x, x], axis=0), 0.0)

    st = {(s, p): st_ref[s * G + p] for s, p in chains}
    ar = {(s, p): jnp.concatenate([pre(_AT, s, p), pre(_RT, s, p)], axis=0) for s, p in chains}
    ars = each(lambda s, p: _bdot_nt(ar[s, p], st[s, p]), chains)
    ab = each(lambda s, p: _bdot_nt(ar[s, p], by_head(pre(_BT, s, p))), chains)
    ak = each(lambda s, p: _bdot_nt(ar[s, p], by_head(pre(_KT, s, p))), chains)
    ps = {key: jnp.where(pk.strict, ab[key][0:C], 0.0) for key in chains}
    ts = {key: pk.eye + ps[key] for key in chains}
    n = 1
    while 2 * n < C:
        ps = each(lambda s, p: _bdot(ps[s, p], pk.block_diag(ps[s, p])), chains)
        ts = each(lambda s, p: ts[s, p] + _bdot(ps[s, p], pk.block_diag(ts[s, p])), chains)
        n *= 2
    xv = {(s, p): pre(_XV, s, p) for s, p in chains}
    v_bh = {key: by_head(xv[key]) for key in chains}
    y = each(lambda s, p: ars[s, p][0:C] + _bdot(jnp.where(pk.strict, ak[s, p][0:C], 0.0), v_bh[s, p]), chains)
    u = each(lambda s, p: _bdot(ts[s, p], by_head(y[s, p])), chains)
    o = each(lambda s, p: ars[s, p][C:2 * C]
             + _bdot(jnp.where(pk.incl, ab[s, p][C:2 * C], 0.0), by_head(u[s, p]))
             + _bdot(jnp.where(pk.incl, ak[s, p][C:2 * C], 0.0), v_bh[s, p]), chains)
    for s, p in chains:
        uv = jnp.concatenate([u[s, p], xv[s, p]], axis=0)
        bkhat = jnp.concatenate([pre(_BHAT, s, p), pre(_KHAT, s, p)], axis=0)
        s_new = st[s, p] * ptot_ref[slot, s, 0:1, p * LANE:(p + 1) * LANE] + _bdot_tn(uv, bkhat)
        st_ref[s * G + p] = jnp.where(same_head, s_new, 0.0)
        tick()

    o_rows = jnp.concatenate([o[key] for key in chains], axis=0)
    mean = _bdot(o_rows, ones_bd) * (1.0 / HEAD_B)
    d = o_rows - mean
    var = _bdot(d * d, ones_bd) * (1.0 / HEAD_B)
    dn = d * lax.rsqrt(var + GN_EPS)
    for s in range(NS):
        on = to_cols(dn[s * G * C:(s + 1) * G * C]) * lnw_ref[...] + lnb_ref[...]
        o_ref[s] = ((on + pre_ref[slot, s, _BONUS]) * pre_ref[slot, s, _GATE]).astype(BF16)
    for _ in ahead:
        pass

    @pl.when(ci == nch - 1)
    def _():
        for s, p in chains:
            s_ref[s, 0, p] = _xdot(st_ref[s * G + p], gather, 3)


def _rwkv_multi_mixer(proj, nb, seq, C, NS, prev_rkv, prev_sm, s0, mu_rkv, mu_sm, w2p, a2p, g2p,
                      w0, a0, k_k, k_a, r_k, lnw, lnb):
    G = N_PAIR
    nch = seq // C
    ng = nb // NS
    gw = G * LANE
    proj3 = proj.reshape(NS, ng * seq, P_CAT)

    def first_spec(col0, w):
        return pl.BlockSpec((NS, C, w), lambda i, c: (0, i * nch, col0 // w))

    def next_spec(col0, w):
        return pl.BlockSpec((NS, C, w), lambda i, c: (0, i * nch + jnp.minimum(c + 1, nch - 1), col0 // w))

    def prev_spec(part):
        return pl.BlockSpec((SUB, gw), lambda i, c: (0, part))

    def vec_spec(part=0):
        return pl.BlockSpec((1, gw), lambda i, c: (0, part))

    in_specs = [
        first_spec(RKV0, gw), first_spec(RKV0 + W_B, gw), first_spec(RKV0 + 2 * W_B, gw), first_spec(SM0, SM_W),
        next_spec(RKV0, gw), next_spec(RKV0 + W_B, gw), next_spec(RKV0 + 2 * W_B, gw), next_spec(SM0, SM_W),
        prev_spec(0), prev_spec(1), prev_spec(2),
        pl.BlockSpec((SUB, SM_W), lambda i, c: (0, 0)),
        pl.BlockSpec((None, G, LANE, HEAD_B), lambda i, c: (0, 0, 0, 0)),
        vec_spec(0), vec_spec(1), vec_spec(2),
        pl.BlockSpec((1, SM_W), lambda i, c: (0, 0)),
        pl.BlockSpec((LANE, gw), lambda i, c: (0, 0)),
        pl.BlockSpec((LANE, gw), lambda i, c: (0, 0)),
        pl.BlockSpec((2 * LANE, gw), lambda i, c: (0, 0)),
        vec_spec(), vec_spec(), vec_spec(), vec_spec(), vec_spec(), vec_spec(), vec_spec(),
    ]
    ob, sw = pl.pallas_call(
        functools.partial(_rwkv_multi_kernel, C=C, G=G, NS=NS),
        out_shape=(jax.ShapeDtypeStruct((NS, ng * seq, W_B), BF16),
                   jax.ShapeDtypeStruct((NS, ng, N_PAIR, LANE, HEAD_B), F32)),
        grid=(ng, nch),
        in_specs=in_specs,
        out_specs=(pl.BlockSpec((NS, C, gw), lambda i, c: (0, i * nch + c, 0)),
                   pl.BlockSpec((NS, 1, G, LANE, HEAD_B), lambda i, c: (0, i, 0, 0, 0))),
        scratch_shapes=[pltpu.VMEM((NS, SUB, gw), F32), pltpu.VMEM((NS, SUB, gw), F32), pltpu.VMEM((NS, SUB, gw), F32),
                        pltpu.VMEM((NS, SUB, SM_W), F32), pltpu.VMEM((NS * G, LANE, LANE), F32),
                        pltpu.VMEM((2, NS, _N_PRE, C, gw), F32), pltpu.VMEM((2, NS, SUB, gw), F32)],
        compiler_params=pltpu.CompilerParams(
            dimension_semantics=("parallel", "arbitrary"), vmem_limit_bytes=VMEM_LIMIT),
        name="rwkv_multi_mixer",
    )(proj3, proj3, proj3, proj3, proj3, proj3, proj3, proj3, prev_rkv, prev_rkv, prev_rkv, prev_sm, s0,
      mu_rkv, mu_rkv, mu_rkv, mu_sm, w2p, a2p, g2p, w0, a0, k_k, k_a, r_k, lnw, lnb)
    return ob.reshape(nb * seq, W_B), sw.reshape(nb, N_PAIR, LANE, HEAD_B)


def _small_layout(cols_ba, cols_w, cols_a, cols_g, axis=-1):
    def z(n):
        shape = list(cols_w.shape)
        shape[axis] = n
        return jnp.zeros(shape, cols_w.dtype)
    return jnp.concatenate(
        [cols_ba, z(SM_WA - cols_ba.shape[axis]), cols_w, cols_a, cols_g,
         z(SM_W - SM_G - cols_g.shape[axis])], axis=axis)


def kernel(x_prompt, x_sample, state_delta, state_conv_qkv, state_wkv, state_shift, state_ffn_conv, meta, norm1, w_in, conv_a, a_log, dt_bias, onorm_a, mu_b, w0, w2, a0, a2, g2, k_k, k_a, r_k, lnx_w, lnx_b, w_o, norm2, w_ffn_in, conv_f, w_ffn_out, norm_f):
    nbp, seq_p, _ = x_prompt.shape
    nbs, seq_s, _ = x_sample.shape
    n_s = nbs * seq_s
    assert w_in.shape[0] == 1, "single-layer trunk"
    assert seq_s == SUB
    l = 0

    wt = w_in[l].T
    o_b = A_PROJ
    o_l = A_PROJ + 3 * W_B
    w_cat_t = jnp.concatenate([
        wt[:4 * W_A], wt[o_b:o_l],
        _small_layout(wt[4 * W_A:A_PROJ], wt[o_l:o_l + W_LORA], wt[o_l + W_LORA:o_l + W_LORA + A_LORA],
                      wt[o_l + W_LORA + A_LORA:], axis=0)], axis=0).astype(BF16)
    mu = mu_b[l]
    mu_rkv = mu[None, :3 * W_B]
    mu_sm = _small_layout(jnp.zeros((1, 2 * H_A), F32), mu[None, 3 * W_B:3 * W_B + W_LORA],
                          mu[None, 3 * W_B + W_LORA:3 * W_B + W_LORA + A_LORA],
                          mu[None, 3 * W_B + W_LORA + A_LORA:])
    w2p = jnp.concatenate([w2[l], jnp.zeros((LANE - W_LORA, W_B), F32)], axis=0)
    a2p = jnp.concatenate([jnp.zeros((W_LORA, W_B), F32), a2[l]], axis=0)
    g2p = jnp.concatenate([g2[l], jnp.zeros((2 * LANE - G_LORA, W_B), F32)], axis=0)
    alog_row = jnp.concatenate([jnp.zeros((H_A,), F32), a_log[l], jnp.zeros((LANE - 2 * H_A,), F32)])[None]
    dtb_row = jnp.concatenate([jnp.zeros((H_A,), F32), dt_bias[l], jnp.zeros((LANE - 2 * H_A,), F32)])[None]
    wo_bf = w_o[l].astype(BF16)
    wfo_bf = w_ffn_out[l].astype(BF16)
    row = lambda v: v.reshape(1, -1)

    def mix(proj, row0, nb, seq, C, NB, prev_qkv, prev_rkv, prev_sm, s_delta, s_wkv):
        delta_params = (conv_a[l], alog_row, dtb_row, row(onorm_a[l]))
        rwkv_params = (mu_rkv, mu_sm, w2p, a2p, g2p, row(w0[l]), row(a0[l]), row(k_k[l]), row(k_a[l]),
                       row(r_k[l]), row(lnx_w[l]), row(lnx_b[l]))
        if seq // C > 1:
            assert row0 == 0 and NB == 1
            oa, sd = _delta_multi_mixer(proj, nb, seq, C, 4, prev_qkv, s_delta, *delta_params)
            ob, sw = _rwkv_multi_mixer(proj, nb, seq, C, 2, prev_rkv, prev_sm, s_wkv, *rwkv_params)
        else:
            oa, sd = _delta_mixer(proj, row0, nb, seq, C, NB, prev_qkv, s_delta, *delta_params)
            ob, sw = _rwkv_mixer(proj, row0, nb, seq, C, NB, N_PAIR, prev_rkv, prev_sm, s_wkv, *rwkv_params)
        return oa, ob, sd, sw

    xs_rows = jnp.concatenate([x_sample.reshape(n_s, D_MODEL), meta], axis=0)
    xp_rows = x_prompt.reshape(nbp * seq_p, D_MODEL)
    n_small = n_s + N_META
    proj_s, proj_tail_s = _norm_matmul(xs_rows, row(norm1[l]), w_cat_t, n_small, 1536, 3, nbs)
    proj_p = _norm_matmul(xp_rows, row(norm1[l]), w_cat_t, 1024, 1536)

    zeros = lambda *s: jnp.zeros(s, F32)
    oa_m, ob_m, sd_m, sw_m = mix(proj_s, n_s, 1, N_META, N_META, 1, zeros(SUB, 3 * W_A),
                                 zeros(SUB, 3 * W_B), zeros(SUB, SM_W),
                                 zeros(1, H_A, HEAD_A, HEAD_A), zeros(1, N_PAIR, LANE, HEAD_B))
    tail = proj_s[n_small - SUB:n_small]
    oa_p, ob_p, sd_p, sw_p = mix(proj_p, 0, nbp, seq_p, 64, 1, tail[:, :3 * W_A],
                                 tail[:, RKV0:SM0], tail[:, SM0:], sd_m, sw_m)
    sh = state_shift[l]
    sh_sm = _small_layout(jnp.zeros((nbs, 1, 2 * H_A), F32), sh[..., 3 * W_B:3 * W_B + W_LORA],
                          sh[..., 3 * W_B + W_LORA:3 * W_B + W_LORA + A_LORA],
                          sh[..., 3 * W_B + W_LORA + A_LORA:])
    tmajor = lambda s: s.transpose(1, 0, 2)
    oa_s, ob_s, sd_s, sw_s = mix(proj_s, 0, nbs, seq_s, seq_s, 8, tmajor(state_conv_qkv[l]),
                                 tmajor(sh[..., :3 * W_B]), tmajor(sh_sm), state_delta[l],
                                 state_wkv[l].reshape(nbs, N_PAIR, LANE, HEAD_B))

    oa_small = jnp.concatenate([oa_s, oa_m], axis=0)
    ob_small = jnp.concatenate([ob_s, ob_m], axis=0)
    x1_s, h2_s = _out_proj(xs_rows, oa_small, ob_small, wo_bf, row(norm2[l]), n_small // 5)
    x1_p, h2_p = _out_proj(xp_rows, oa_p, ob_p, wo_bf, row(norm2[l]), 512)
    act_s, gate_tail_s, gate_last_s = _ffn_in(h2_s, w_ffn_in[l], state_ffn_conv[l].transpose(1, 0, 2), conv_f[l],
                                              n_small, 768, 3, 0, n_s + SUB)
    act_p, tail_p = _ffn_in(h2_p, w_ffn_in[l], gate_last_s[None], conv_f[l], 1024, 768, 3, seq_p // 1024)
    y_s = _ffn_out(act_s, wfo_bf, x1_s, row(norm_f), n_s, 1024, 768)
    y_p = _ffn_out(act_p, wfo_bf, x1_p, row(norm_f), nbp * seq_p, 1024, 768)

    def states(conv_new, last, ffn_new, nb, sd, sw):
        shift_new = jnp.concatenate([last[..., RKV0:SM0], last[..., SM0 + SM_WA:SM0 + SM_WA + W_LORA + A_LORA],
                                     last[..., SM0 + SM_G:SM0 + SM_G + G_LORA]], axis=-1)
        return (sd[None], conv_new[None], sw.reshape(nb, H_B, HEAD_B, HEAD_B)[None], shift_new[None],
                ffn_new[None])

    p3 = proj_p.reshape(nbp, seq_p, P_CAT)
    tail_s = proj_tail_s.transpose(1, 0, 2)
    return ((y_p.reshape(nbp, seq_p, D_MODEL), y_s.reshape(nbs, seq_s, D_MODEL))
            + states(p3[:, seq_p - 3:, :3 * W_A], p3[:, seq_p - 1:, :], tail_p[:, SUB - 2:, :], nbp, sd_p, sw_p)
            + states(tail_s[:, :, :3 * W_A], tail_s[:, 2:, :], gate_tail_s.transpose(1, 0, 2), nbs, sd_s, sw_s))
```

```python
import functools

import jax
import jax.numpy as jnp
from jax import lax
from jax.experimental import pallas as pl
from jax.experimental.pallas import tpu as pltpu

F32 = jnp.float32
BF16 = jnp.bfloat16

D_MODEL = 2048
N_META = 16
W_A = 1024
HEAD_A = 128
H_A = 8
W_B = 1024
HEAD_B = 64
H_B = 16
N_PAIR = H_B // 2
W_LORA = 64
A_LORA = 64
G_LORA = 160
D_FF = 5376
RMS_EPS = 1e-6
GN_EPS = 64e-5
A_PROJ = 4 * W_A + 2 * H_A
B_PROJ = 3 * W_B + W_LORA + A_LORA + G_LORA

QKVZ0 = 0
RKV0 = 4 * W_A
SM0 = RKV0 + 3 * W_B
SM_W = 512
SM_BA = 0
SM_WA = 128
SM_G = 256
P_CAT = SM0 + SM_W

LANE = 128
SUB = 8
VMEM_LIMIT = 48 * 1024 * 1024
VMEM_LIMIT_BIG = 58 * 1024 * 1024
NEG_BIG = -1e30

NT_DIMS = (((1,), (1,)), ((), ()))


def _bdot(a, b):
    return jnp.dot(a.astype(BF16), b.astype(BF16), preferred_element_type=F32)


def _bdot_nt(a, b):
    return lax.dot_general(a.astype(BF16), b.astype(BF16), NT_DIMS, preferred_element_type=F32)


def _bdot_tn(a, b):
    return lax.dot_general(a.astype(BF16), b.astype(BF16), (((0,), (0,)), ((), ())),
                           preferred_element_type=F32)


def _pieces(a, n):
    out = []
    rem = a
    for i in range(n):
        p = rem.astype(BF16)
        out.append(p)
        if i + 1 < n:
            rem = rem - p.astype(F32)
    return out


def _xdot(a, b, n, dims=(((1,), (0,)), ((), ()))):
    bb = b.astype(BF16)
    acc = None
    for p in _pieces(a, n):
        t = lax.dot_general(p, bb, dims, preferred_element_type=F32)
        acc = t if acc is None else acc + t
    return acc


def _xdot_r(a, b, n, dims=(((1,), (0,)), ((), ()))):
    ab = a.astype(BF16)
    acc = None
    for p in _pieces(b, n):
        t = lax.dot_general(ab, p, dims, preferred_element_type=F32)
        acc = t if acc is None else acc + t
    return acc


def _sigmoid(x):
    return 1.0 / (1.0 + jnp.exp(-x))


def _silu(x):
    return x * _sigmoid(x)


def _softplus(x):
    return jnp.maximum(x, 0.0) + jnp.log(1.0 + jnp.exp(-jnp.abs(x)))


def _iota(shape, dim):
    return lax.broadcasted_iota(jnp.int32, shape, dim)


def _shifted(x, prev8, k):
    n = x.shape[0]
    xr = pltpu.roll(x, k, 0)
    pr = pltpu.roll(prev8, k, 0)
    first = jnp.where(_iota((SUB, x.shape[1]), 0) < k, pr, xr[0:SUB])
    if n == SUB:
        return first
    return jnp.concatenate([first, xr[SUB:]], axis=0)


def _repeat_rows(s, reps):
    n, c = s.shape
    return jnp.broadcast_to(s[:, None, :], (n, reps, c)).reshape(n * reps, c)


def _delayed(x, hist, k, nseq):
    if nseq == 1:
        return _shifted(x, hist, k)
    t = _iota(x.shape, 0) & (SUB - 1)
    d = pltpu.roll(x, k, 0)
    for tt in range(k):
        d = jnp.where(t == tt, hist[len(hist) - k + tt], d)
    return d


def _same_seq(shape, c, rows_total):
    if c == rows_total:
        return None
    sh = c.bit_length() - 1
    return (lax.shift_right_logical(_iota(shape, 0), sh)
            == lax.shift_right_logical(_iota(shape, 1) & (rows_total - 1), sh))


class _Packed:
    def __init__(self, r, c):
        self.r = r
        self.c = c
        row = _iota((r, 2 * r), 0)
        col = _iota((r, 2 * r), 1) & (r - 1)
        same = _same_seq((r, 2 * r), c, r)
        self.incl = (row >= col) if same is None else ((row >= col) & same)
        self.strict = (row > col) if same is None else ((row > col) & same)
        self.eye = (row == col).astype(F32)
        self.first = _iota((r, 2 * r), 1) < r
        self.bd_mask = (_iota((2 * r, 2 * r), 0) < r) == (_iota((2 * r, 2 * r), 1) < r)

    def block_diag(self, m):
        return jnp.where(self.bd_mask, jnp.concatenate([m, m], axis=0), 0.0)

    def inverse_many(self, xs):
        ts = [self.eye + x for x in xs]
        ps = list(xs)
        n = 1
        while 2 * n < self.c:
            ps = [_bdot(p, self.block_diag(p)) for p in ps]
            ts = [t + _bdot(p, self.block_diag(t)) for p, t in zip(ps, ts)]
            n *= 2
        return ts


def _seq_masks(r, c):
    row = _iota((r, r), 0)
    col = _iota((r, r), 1)
    same = _same_seq((r, r), c, r)
    if same is None:
        return (row >= col).astype(F32), None
    return ((row >= col) & same).astype(F32), same.astype(F32)


def _seq_tail_rows(x, slab_ref, dst_ref, n_tail, nseq, col0=0):
    for s in range(x.shape[1] // LANE):
        slab_ref[s] = x[:, s * LANE:(s + 1) * LANE]
        for t in range(n_tail):
            dst_ref[t, :, col0 + s * LANE:col0 + (s + 1) * LANE] = (
                slab_ref[s, pl.ds(SUB - n_tail + t, nseq, stride=SUB), :])


def _norm_matmul_kernel(x_ref, g_ref, w_ref, o_ref, *rest, n_tail, nseq):
    tail_ref, h_ref, slab_ref = rest if n_tail else (None,) + rest + (None,)

    @pl.when(pl.program_id(1) == 0)
    def _():
        x = x_ref[...]
        ms = jnp.mean(x * x, axis=-1, keepdims=True)
        h_ref[...] = (x * lax.rsqrt(ms + RMS_EPS) * g_ref[...]).astype(BF16)

    out = lax.dot_general(h_ref[...], w_ref[...], NT_DIMS, preferred_element_type=F32)
    o_ref[...] = out
    if n_tail:
        _seq_tail_rows(out, slab_ref, tail_ref, n_tail, nseq)


def _norm_matmul(x, g, wt, tm, tn, n_tail=0, nseq=0):
    m, k = x.shape
    n = wt.shape[0]
    out_shape = jax.ShapeDtypeStruct((m, n), F32)
    out_specs = pl.BlockSpec((tm, tn), lambda i, j: (i, j))
    if n_tail:
        assert m == tm
        out_shape = (out_shape, jax.ShapeDtypeStruct((n_tail, nseq, n), F32))
        out_specs = (out_specs, pl.BlockSpec((n_tail, nseq, tn), lambda i, j: (0, 0, j)))
    return pl.pallas_call(
        functools.partial(_norm_matmul_kernel, n_tail=n_tail, nseq=nseq),
        out_shape=out_shape,
        grid=(m // tm, n // tn),
        in_specs=[
            pl.BlockSpec((tm, k), lambda i, j: (i, 0)),
            pl.BlockSpec((1, k), lambda i, j: (0, 0)),
            pl.BlockSpec((tn, k), lambda i, j: (j, 0)),
        ],
        out_specs=out_specs,
        scratch_shapes=[pltpu.VMEM((tm, k), BF16)] + ([pltpu.VMEM((tn // LANE, tm, LANE), F32)] if n_tail else []),
        compiler_params=pltpu.CompilerParams(
            dimension_semantics=("parallel", "arbitrary"), vmem_limit_bytes=VMEM_LIMIT_BIG),
        name="norm_matmul",
    )(x, g, wt)


def _out_proj_kernel(x_ref, oa_ref, ob_ref, wt_ref, wb_ref, g_ref, x1_ref, h_ref):
    acc = jnp.dot(oa_ref[...], wt_ref[...], preferred_element_type=F32)
    acc = acc + jnp.dot(ob_ref[...], wb_ref[...], preferred_element_type=F32)
    x1 = x_ref[...] + acc
    x1_ref[...] = x1
    ms = jnp.mean(x1 * x1, axis=-1, keepdims=True)
    h_ref[...] = (x1 * lax.rsqrt(ms + RMS_EPS) * g_ref[...]).astype(BF16)


def _out_proj(x, oa, ob, wo, g, tm):
    m, d = x.shape
    return pl.pallas_call(
        _out_proj_kernel,
        out_shape=(jax.ShapeDtypeStruct((m, d), F32), jax.ShapeDtypeStruct((m, d), BF16)),
        grid=(m // tm,),
        in_specs=[
            pl.BlockSpec((tm, d), lambda i: (i, 0)),
            pl.BlockSpec((tm, W_A), lambda i: (i, 0)),
            pl.BlockSpec((tm, W_B), lambda i: (i, 0)),
            pl.BlockSpec((W_A, d), lambda i: (0, 0)),
            pl.BlockSpec((W_B, d), lambda i: (W_A // W_B, 0)),
            pl.BlockSpec((1, d), lambda i: (0, 0)),
        ],
        out_specs=(pl.BlockSpec((tm, d), lambda i: (i, 0)), pl.BlockSpec((tm, d), lambda i: (i, 0))),
        compiler_params=pltpu.CompilerParams(
            dimension_semantics=("parallel",), vmem_limit_bytes=VMEM_LIMIT),
        name="out_proj",
    )(x, oa, ob, wo, wo, g)


def _ffn_in_kernel(h_ref, wg_ref, wu_ref, prev_ref, cw_ref, act_ref, tail_ref, *rest,
                   tiles_per_seq, state_rows, nsub):
    if tiles_per_seq:
        wgb_ref, wub_ref, carry_ref = rest
    else:
        last_ref, wgb_ref, wub_ref, carry_ref, slab_ref = rest
    m = pl.program_id(1)

    @pl.when(m == 0)
    def _():
        wgb_ref[...] = wg_ref[...].astype(BF16)
        wub_ref[...] = wu_ref[...].astype(BF16)

    if tiles_per_seq:
        @pl.when(m % tiles_per_seq == 0)
        def _():
            carry_ref[...] = prev_ref[...]

    h = h_ref[...]
    n = h.shape[0]
    w = wgb_ref.shape[1] // nsub
    for i in range(nsub):
        cs = slice(i * w, (i + 1) * w)
        gate = jnp.dot(h, wgb_ref[:, cs], preferred_element_type=F32)
        up = jnp.dot(h, wub_ref[:, cs], preferred_element_type=F32)
        if tiles_per_seq:
            prev8 = carry_ref[:, cs]
            d1 = _shifted(gate, prev8, 1)
            d2 = _shifted(gate, prev8, 2)
            carry_ref[:, cs] = gate[n - SUB:n]
            tail_ref[:, cs] = gate[n - SUB:n]
        else:
            nseq = (state_rows - SUB) // SUB
            _seq_tail_rows(gate, slab_ref, tail_ref, 2, nseq, i * w)
            last_ref[:, cs] = gate[n - SUB:n]
            pad = jnp.zeros((n - nseq * SUB, w), F32)
            hist = [jnp.concatenate([_repeat_rows(prev_ref[t, :, cs], SUB), pad], axis=0) for t in range(2)]
            t_in_seq = jnp.where(_iota(gate.shape, 0) < state_rows, _iota(gate.shape, 0) & (SUB - 1), SUB)
            d1 = jnp.where(t_in_seq == 0, hist[1], pltpu.roll(gate, 1, 0))
            d2 = jnp.where(t_in_seq == 0, hist[0], jnp.where(t_in_seq == 1, hist[1], pltpu.roll(gate, 2, 0)))
        y = gate * cw_ref[2:3, cs] + d1 * cw_ref[1:2, cs] + d2 * cw_ref[0:1, cs]
        act_ref[:, cs] = (_silu(y) * up).astype(BF16)


def _ffn_in(h, w, prev, cw, tm, tn, nsub, tiles_per_seq, state_rows=0):
    m, d = h.shape
    nj = D_FF // tn
    nm = m // tm
    if tiles_per_seq:
        nseq = nm // tiles_per_seq
        prev_spec = pl.BlockSpec((None, SUB, tn), lambda j, i: (0, 0, j))
        extra_shapes = (jax.ShapeDtypeStruct((nseq, SUB, D_FF), F32),)
        extra_specs = (pl.BlockSpec((None, SUB, tn), lambda j, i: (i // tiles_per_seq, 0, j)),)
        extra_scratch = []
    else:
        assert nm == 1
        nseq = (state_rows - SUB) // SUB
        prev_spec = pl.BlockSpec((2, nseq, tn), lambda j, i: (0, 0, j))
        extra_shapes = (jax.ShapeDtypeStruct((2, nseq, D_FF), F32), jax.ShapeDtypeStruct((SUB, D_FF), F32))
        extra_specs = (pl.BlockSpec((2, nseq, tn), lambda j, i: (0, 0, j)),
                       pl.BlockSpec((SUB, tn), lambda j, i: (0, j)))
        extra_scratch = [pltpu.VMEM((tn // nsub // LANE, tm, LANE), F32)]
    return pl.pallas_call(
        functools.partial(_ffn_in_kernel, tiles_per_seq=tiles_per_seq, state_rows=state_rows, nsub=nsub),
        out_shape=(jax.ShapeDtypeStruct((m, D_FF), BF16),) + extra_shapes,
        grid=(nj, nm),
        in_specs=[
            pl.BlockSpec((tm, d), lambda j, i: (i, 0)),
            pl.BlockSpec((d, tn), lambda j, i: (0, j)),
            pl.BlockSpec((d, tn), lambda j, i: (0, nj + j)),
            prev_spec,
            pl.BlockSpec((3, tn), lambda j, i: (0, j)),
        ],
        out_specs=(pl.BlockSpec((tm, tn), lambda j, i: (i, j)),) + extra_specs,
        scratch_shapes=[pltpu.VMEM((d, tn), BF16), pltpu.VMEM((d, tn), BF16), pltpu.VMEM((SUB, tn), F32)]
        + extra_scratch,
        compiler_params=pltpu.CompilerParams(
            dimension_semantics=("parallel", "arbitrary"), vmem_limit_bytes=VMEM_LIMIT_BIG),
        name="ffn_in",
    )(h, w, w, prev, cw)


def _ffn_out_kernel(a_ref, w_ref, x_ref, g_ref, o_ref):
    kk = pl.program_id(1)

    @pl.when(kk == 0)
    def _():
        o_ref[...] = x_ref[...]

    o_ref[...] += jnp.dot(a_ref[...], w_ref[...], preferred_element_type=F32)

    @pl.when(kk == pl.num_programs(1) - 1)
    def _():
        x = o_ref[...]
        ms = jnp.mean(x * x, axis=-1, keepdims=True)
        o_ref[...] = x * lax.rsqrt(ms + RMS_EPS) * g_ref[...]


def _ffn_out(act, w, x1, g, m, tm, tk):
    kdim, d = w.shape
    return pl.pallas_call(
        _ffn_out_kernel,
        out_shape=jax.ShapeDtypeStruct((m, d), F32),
        grid=(m // tm, kdim // tk),
        in_specs=[
            pl.BlockSpec((tm, tk), lambda i, k: (i, k)),
            pl.BlockSpec((tk, d), lambda i, k: (k, 0)),
            pl.BlockSpec((tm, d), lambda i, k: (i, 0)),
            pl.BlockSpec((1, d), lambda i, k: (0, 0)),
        ],
        out_specs=pl.BlockSpec((tm, d), lambda i, k: (i, 0)),
        compiler_params=pltpu.CompilerParams(
            dimension_semantics=("parallel", "arbitrary"), vmem_limit_bytes=VMEM_LIMIT),
        name="ffn_out",
    )(act, w, x1, g)


def _delta_kernel(qkvz_ref, ba_ref, prev_ref, s0_ref, cw_ref, alog_ref, dtb_ref, on_ref,
                  o_ref, s_ref, carry_ref, *, C, NB):
    R = NB * C
    ci = pl.program_id(1)

    @pl.when(ci == 0)
    def _():
        if NB == 1:
            carry_ref[...] = prev_ref[...]
        s_ref[...] = s0_ref[...]

    tri, ones_seq = _seq_masks(R, C)

    ba = ba_ref[...]
    beta_full = _sigmoid(ba)
    g_full = -jnp.exp(alog_ref[...]) * _softplus(ba + dtb_ref[...])
    gc_full = _xdot_r(tri, g_full, 3)
    gtot_full = gc_full[C - 1:C, :] if NB == 1 else _xdot_r(ones_seq, g_full, 3)

    def conv_silu(c0):
        x = qkvz_ref[:, c0:c0 + LANE]
        if NB == 1:
            hist = carry_ref[:, c0:c0 + LANE]
        else:
            hist = [_repeat_rows(prev_ref[j, :, c0:c0 + LANE], SUB) for j in range(3)]
        y = x * cw_ref[3:4, c0:c0 + LANE]
        for k in (1, 2, 3):
            y = y + _delayed(x, hist, k, NB) * cw_ref[3 - k:4 - k, c0:c0 + LANE]
        return _silu(y)

    ones_ll = jnp.ones((LANE, LANE), BF16)

    def lane_sum(x):
        return jnp.dot(x.astype(BF16), ones_ll, preferred_element_type=F32)

    def l2n_many(xs):
        sums = [lane_sum(x * x) for x in xs]
        return [x * lax.rsqrt(sq + 1e-6) for x, sq in zip(xs, sums)]

    def seq_rows(x, n):
        return x[n * C:(n + 1) * C]

    heads = range(H_A)
    seqs = range(NB)
    q = [x * (HEAD_A ** -0.5) for x in l2n_many([conv_silu(h * HEAD_A) for h in heads])]
    k = l2n_many([conv_silu(W_A + h * HEAD_A) for h in heads])
    v = [conv_silu(2 * W_A + h * HEAD_A) for h in heads]
    bcol = [beta_full[:, h:h + 1] for h in heads]
    gcol = [gc_full[:, H_A + h:H_A + h + 1] for h in heads]
    gtot = [gtot_full[:, H_A + h:H_A + h + 1] for h in heads]
    eg = [jnp.exp(gcol[h]) for h in heads]
    kb = [k[h] * bcol[h] for h in heads]

    pk = _Packed(R, C)
    hpairs = range(H_A // 2)
    zc = jnp.zeros((R, HEAD_A), F32)
    row2 = _iota((2 * R, LANE), 0)
    lane2 = _iota((2 * R, LANE), 1)
    ones_cl = jnp.ones((R, LANE), F32)
    gc2 = jnp.concatenate([gc_full, gc_full], axis=0)
    kq, gamma = [], []
    for hp in hpairs:
        h0, h1 = 2 * hp, 2 * hp + 1
        lhs = jnp.concatenate([jnp.concatenate([kb[h0], kb[h1]], axis=1),
                               jnp.concatenate([q[h0], q[h1]], axis=1)], axis=0)
        rk = jnp.concatenate([jnp.concatenate([k[h0], zc], axis=1),
                              jnp.concatenate([zc, k[h1]], axis=1)], axis=0)
        kq.append(_bdot_nt(lhs, rk))
        sel = lane2 == jnp.where(row2 < R, H_A + h0, H_A + h1)
        grow = _xdot_r(ones_cl, jnp.where(sel, gc2, 0.0), 3, NT_DIMS)
        gcol_p = jnp.where(pk.first, gcol[h0], gcol[h1])
        gamma.append(jnp.exp(jnp.where(pk.incl, gcol_p - grow, NEG_BIG)))
    tinv = pk.inverse_many([-jnp.where(pk.strict, kq[hp][0:R] * gamma[hp], 0.0) for hp in hpairs])
    uw = []
    for hp in hpairs:
        h0, h1 = 2 * hp, 2 * hp + 1
        z2 = jnp.zeros((R, 2 * HEAD_A), F32)
        rhs = jnp.concatenate([jnp.concatenate([v[h0] * bcol[h0], kb[h0] * eg[h0], z2], axis=1),
                               jnp.concatenate([z2, v[h1] * bcol[h1], kb[h1] * eg[h1]], axis=1)], axis=0)
        uw.append(_bdot(tinv[hp], rhs))
    u = [uw[h // 2][:, (h % 2) * 2 * HEAD_A:(h % 2) * 2 * HEAD_A + HEAD_A] for h in heads]
    w = [uw[h // 2][:, (h % 2) * 2 * HEAD_A + HEAD_A:(h % 2 + 1) * 2 * HEAD_A] for h in heads]
    qd = [q[h] * eg[h] for h in heads]
    s = [[s_ref[n, h] for h in heads] for n in seqs]
    wqs = [[_bdot(jnp.concatenate([seq_rows(w[h], n), seq_rows(qd[h], n)], axis=0), s[n][h]) for h in heads]
           for n in seqs]
    ws = [jnp.concatenate([wqs[n][h][0:C] for n in seqs], axis=0) if NB > 1 else wqs[0][h][0:C] for h in heads]
    qs = [jnp.concatenate([wqs[n][h][C:2 * C] for n in seqs], axis=0) if NB > 1 else wqs[0][h][C:2 * C]
          for h in heads]
    v_new = [u[h] - ws[h] for h in heads]
    o = []
    for hp in hpairs:
        h0, h1 = 2 * hp, 2 * hp + 1
        vn_bd = jnp.concatenate([jnp.concatenate([v_new[h0], zc], axis=1),
                                 jnp.concatenate([zc, v_new[h1]], axis=1)], axis=0)
        op = _bdot(kq[hp][R:2 * R] * gamma[hp], vn_bd)
        o.append(qs[h0] + op[:, :HEAD_A])
        o.append(qs[h1] + op[:, HEAD_A:])
    kd = [k[h] * jnp.exp(gtot[h] - gcol[h]) for h in heads]
    for n in seqs:
        for h in heads:
            glast = gtot[h] if NB == 1 else gtot[h][n * C:n * C + 1]
            s_ref[n, h] = s[n][h] * jnp.exp(glast) + _bdot_tn(seq_rows(kd[h], n), seq_rows(v_new[h], n))
    osq = [lane_sum(o[h] * o[h]) for h in heads]
    for h in heads:
        z = qkvz_ref[:, 3 * W_A + h * HEAD_A:3 * W_A + (h + 1) * HEAD_A]
        oh = o[h] * lax.rsqrt(osq[h] * (1.0 / HEAD_A) + RMS_EPS)
        o_ref[:, h * HEAD_A:(h + 1) * HEAD_A] = (oh * on_ref[...] * _silu(z)).astype(BF16)

    if NB == 1:
        carry_ref[...] = qkvz_ref[C - SUB:C, 0:3 * W_A]


def _delta_mixer(proj, row0, nb, seq, C, NB, prev, s0, cw, alog_row, dtb_row, onorm):
    nch = seq // C
    assert NB == 1 or (nch == 1 and C == SUB)
    R = NB * C
    blk0 = row0 // R
    bcast = s0.shape[0] == 1
    bsel = (lambda b: 0) if bcast else (lambda b: b)
    return pl.pallas_call(
        functools.partial(_delta_kernel, C=C, NB=NB),
        out_shape=(jax.ShapeDtypeStruct((nb * seq, W_A), BF16),
                   jax.ShapeDtypeStruct((nb, H_A, HEAD_A, HEAD_A), F32)),
        grid=(nb // NB, nch),
        in_specs=[
            pl.BlockSpec((R, 4 * W_A), lambda b, c: (blk0 + b * nch + c, 0)),
            pl.BlockSpec((R, LANE), lambda b, c: (blk0 + b * nch + c, SM0 // LANE)),
            (pl.BlockSpec((SUB, 3 * W_A), lambda b, c: (0, 0)) if NB == 1
             else pl.BlockSpec((3, NB, 3 * W_A), lambda b, c: (0, b, 0))),
            pl.BlockSpec((NB, H_A, HEAD_A, HEAD_A), lambda b, c: (bsel(b), 0, 0, 0)),
            pl.BlockSpec((4, 3 * W_A), lambda b, c: (0, 0)),
            pl.BlockSpec((1, LANE), lambda b, c: (0, 0)),
            pl.BlockSpec((1, LANE), lambda b, c: (0, 0)),
            pl.BlockSpec((1, HEAD_A), lambda b, c: (0, 0)),
        ],
        out_specs=(pl.BlockSpec((R, W_A), lambda b, c: (b * nch + c, 0)),
                   pl.BlockSpec((NB, H_A, HEAD_A, HEAD_A), lambda b, c: (b, 0, 0, 0))),
        scratch_shapes=[pltpu.VMEM((SUB, 3 * W_A), F32)],
        compiler_params=pltpu.CompilerParams(
            dimension_semantics=("parallel", "arbitrary"), vmem_limit_bytes=VMEM_LIMIT),
        name="delta_mixer",
    )(proj, proj, prev, s0, cw, alog_row, dtb_row, onorm)


def _delta_multi_kernel(qkv0_ref, ba0_ref, qkvn_ref, ban_ref, z_ref, prev_ref, s0_ref, cw_ref, alog_ref, dtb_ref,
                        on_ref, o_ref, s_ref, carry_ref, pre_ref, small_ref, gtot_ref, *, C, NS):
    ci = pl.program_id(1)
    heads = [(s, h) for s in range(NS) for h in range(H_A)]
    hpairs = [(s, hp) for s in range(NS) for hp in range(H_A // 2)]
    tri, _ = _seq_masks(C, C)
    ones_ll = jnp.ones((LANE, LANE), BF16)

    def lane_sum(x):
        return jnp.dot(x.astype(BF16), ones_ll, preferred_element_type=F32)

    def preamble(qkv_ref, ba_ref, hist_of, slot):
        for s in range(NS):
            ba = ba_ref[s]
            g_full = -jnp.exp(alog_ref[...]) * _softplus(ba + dtb_ref[...])
            gc_full = _xdot_r(tri, g_full, 3)
            small_ref[slot, s, 0] = _sigmoid(ba)
            small_ref[slot, s, 1] = gc_full
            gtot_ref[slot, s] = jnp.broadcast_to(gc_full[C - 1:C, :], (SUB, LANE))
            yield
        for part in range(3):
            xs = {}
            for s, h in heads:
                c0 = part * W_A + h * HEAD_A
                x = qkv_ref[s, :, c0:c0 + LANE]
                hist = hist_of(s, c0)
                y = x * cw_ref[3:4, c0:c0 + LANE]
                for k in (1, 2, 3):
                    y = y + _shifted(x, hist, k) * cw_ref[3 - k:4 - k, c0:c0 + LANE]
                xs[s, h] = _silu(y)
                yield
            if part < 2:
                sums = {key: lane_sum(x * x) for key, x in xs.items()}
                yield
                scale = HEAD_A ** -0.5 if part == 0 else 1.0
                xs = {key: x * (lax.rsqrt(sums[key] + 1e-6) * scale) for key, x in xs.items()}
            for s, h in heads:
                pre_ref[slot, s, part, :, h * HEAD_A:(h + 1) * HEAD_A] = xs[s, h]
            yield
        for s in range(NS):
            carry_ref[s] = qkv_ref[s, C - SUB:C, :]

    @pl.when(ci == 0)
    def _():
        for _ in preamble(qkv0_ref, ba0_ref, lambda s, c0: prev_ref[:, c0:c0 + LANE], 0):
            pass
        for s in range(NS):
            s_ref[s, 0] = s0_ref[...]

    slot = ci % 2
    ahead = preamble(qkvn_ref, ban_ref, lambda s, c0: carry_ref[s, :, c0:c0 + LANE], 1 - slot)
    calls = [0]

    def tick():
        calls[0] += 1
        if calls[0] % 3 == 0:
            next(ahead, None)

    def each(fn, items):
        out = {}
        for it in items:
            out[it] = fn(*it)
            tick()
        return out

    beta_full = {s: small_ref[slot, s, 0] for s in range(NS)}
    gc_full = {s: small_ref[slot, s, 1] for s in range(NS)}
    gtot_row = {s: gtot_ref[slot, s] for s in range(NS)}
    q = {(s, h): pre_ref[slot, s, 0, :, h * HEAD_A:(h + 1) * HEAD_A] for s, h in heads}
    k = {(s, h): pre_ref[slot, s, 1, :, h * HEAD_A:(h + 1) * HEAD_A] for s, h in heads}
    v = {(s, h): pre_ref[slot, s, 2, :, h * HEAD_A:(h + 1) * HEAD_A] for s, h in heads}
    bcol = {(s, h): beta_full[s][:, h:h + 1] for s, h in heads}
    gcol = {(s, h): gc_full[s][:, H_A + h:H_A + h + 1] for s, h in heads}
    gtot = {(s, h): gtot_row[s][0:1, H_A + h:H_A + h + 1] for s, h in heads}
    eg = {key: jnp.exp(gcol[key]) for key in heads}
    kb = {key: k[key] * bcol[key] for key in heads}

    pk = _Packed(C, C)
    zc = jnp.zeros((C, HEAD_A), F32)
    row2 = _iota((2 * C, LANE), 0)
    lane2 = _iota((2 * C, LANE), 1)
    ones_cl = jnp.ones((C, LANE), F32)

    def kq_of(s, hp):
        a, b = (s, 2 * hp), (s, 2 * hp + 1)
        lhs = jnp.concatenate([jnp.concatenate([kb[a], kb[b]], axis=1),
                               jnp.concatenate([q[a], q[b]], axis=1)], axis=0)
        rk = jnp.concatenate([jnp.concatenate([k[a], zc], axis=1),
                              jnp.concatenate([zc, k[b]], axis=1)], axis=0)
        return _bdot_nt(lhs, rk)

    def gamma_of(s, hp):
        a, b = (s, 2 * hp), (s, 2 * hp + 1)
        sel = lane2 == jnp.where(row2 < C, H_A + 2 * hp, H_A + 2 * hp + 1)
        gc2 = jnp.concatenate([gc_full[s], gc_full[s]], axis=0)
        grow = _xdot_r(ones_cl, jnp.where(sel, gc2, 0.0), 3, NT_DIMS)
        gcol_p = jnp.where(pk.first, gcol[a], gcol[b])
        return jnp.exp(jnp.where(pk.incl, gcol_p - grow, NEG_BIG))

    kq = each(kq_of, hpairs)
    gamma = each(gamma_of, hpairs)
    ps = {key: -jnp.where(pk.strict, kq[key][0:C] * gamma[key], 0.0) for key in hpairs}
    ts = {key: pk.eye + ps[key] for key in hpairs}
    n = 1
    while 2 * n < C:
        ps = each(lambda s, hp: _bdot(ps[s, hp], pk.block_diag(ps[s, hp])), hpairs)
        ts = each(lambda s, hp: ts[s, hp] + _bdot(ps[s, hp], pk.block_diag(ts[s, hp])), hpairs)
        n *= 2

    def uw_of(s, hp):
        a, b = (s, 2 * hp), (s, 2 * hp + 1)
        z2 = jnp.zeros((C, 2 * HEAD_A), F32)
        rhs = jnp.concatenate([jnp.concatenate([v[a] * bcol[a], kb[a] * eg[a], z2], axis=1),
                               jnp.concatenate([z2, v[b] * bcol[b], kb[b] * eg[b]], axis=1)], axis=0)
        return _bdot(ts[s, hp], rhs)

    uw = each(uw_of, hpairs)
    u = {(s, h): uw[s, h // 2][:, (h % 2) * 2 * HEAD_A:(h % 2) * 2 * HEAD_A + HEAD_A] for s, h in heads}
    w = {(s, h): uw[s, h // 2][:, (h % 2) * 2 * HEAD_A + HEAD_A:(h % 2 + 1) * 2 * HEAD_A] for s, h in heads}
    st = {(s, h): s_ref[s, 0, h] for s, h in heads}
    wqs = each(lambda s, h: _bdot(jnp.concatenate([w[s, h], q[s, h] * eg[s, h]], axis=0), st[s, h]), heads)
    v_new = {key: u[key] - wqs[key][0:C] for key in heads}

    def o_of(s, hp):
        a, b = (s, 2 * hp), (s, 2 * hp + 1)
        vn_bd = jnp.concatenate([jnp.concatenate([v_new[a], zc], axis=1),
                                 jnp.concatenate([zc, v_new[b]], axis=1)], axis=0)
        return _bdot(kq[s, hp][C:2 * C] * gamma[s, hp], vn_bd)

    op = each(o_of, hpairs)
    o = {(s, h): wqs[s, h][C:2 * C] + op[s, h // 2][:, (h % 2) * HEAD_A:(h % 2 + 1) * HEAD_A] for s, h in heads}
    for s, h in heads:
        kd = k[s, h] * jnp.exp(gtot[s, h] - gcol[s, h])
        s_ref[s, 0, h] = st[s, h] * jnp.exp(gtot[s, h]) + _bdot_tn(kd, v_new[s, h])
        tick()
    osq = each(lambda s, h: lane_sum(o[s, h] * o[s, h]), heads)
    for s, h in heads:
        z = z_ref[s, :, h * HEAD_A:(h + 1) * HEAD_A]
        oh = o[s, h] * lax.rsqrt(osq[s, h] * (1.0 / HEAD_A) + RMS_EPS)
        o_ref[s, :, h * HEAD_A:(h + 1) * HEAD_A] = (oh * on_ref[...] * _silu(z)).astype(BF16)
        tick()
    for _ in ahead:
        pass


def _delta_multi_mixer(proj, nb, seq, C, NS, prev, s0, cw, alog_row, dtb_row, onorm):
    nch = seq // C
    ng = nb // NS
    proj3 = proj.reshape(NS, ng * seq, P_CAT)
    nxt = lambda i, c: i * nch + jnp.minimum(c + 1, nch - 1)
    oa, sd = pl.pallas_call(
        functools.partial(_delta_multi_kernel, C=C, NS=NS),
        out_shape=(jax.ShapeDtypeStruct((NS, ng * seq, W_A), BF16),
                   jax.ShapeDtypeStruct((NS, ng, H_A, HEAD_A, HEAD_A), F32)),
        grid=(ng, nch),
        in_specs=[
            pl.BlockSpec((NS, C, 3 * W_A), lambda i, c: (0, i * nch, 0)),
            pl.BlockSpec((NS, C, LANE), lambda i, c: (0, i * nch, SM0 // LANE)),
            pl.BlockSpec((NS, C, 3 * W_A), lambda i, c: (0, nxt(i, c), 0)),
            pl.BlockSpec((NS, C, LANE), lambda i, c: (0, nxt(i, c), SM0 // LANE)),
            pl.BlockSpec((NS, C, W_A), lambda i, c: (0, i * nch + c, 3)),
            pl.BlockSpec((SUB, 3 * W_A), lambda i, c: (0, 0)),
            pl.BlockSpec((None, H_A, HEAD_A, HEAD_A), lambda i, c: (0, 0, 0, 0)),
            pl.BlockSpec((4, 3 * W_A), lambda i, c: (0, 0)),
            pl.BlockSpec((1, LANE), lambda i, c: (0, 0)),
            pl.BlockSpec((1, LANE), lambda i, c: (0, 0)),
            pl.BlockSpec((1, HEAD_A), lambda i, c: (0, 0)),
        ],
        out_specs=(pl.BlockSpec((NS, C, W_A), lambda i, c: (0, i * nch + c, 0)),
                   pl.BlockSpec((NS, 1, H_A, HEAD_A, HEAD_A), lambda i, c: (0, i, 0, 0, 0))),
        scratch_shapes=[pltpu.VMEM((NS, SUB, 3 * W_A), F32), pltpu.VMEM((2, NS, 3, C, W_A), F32),
                        pltpu.VMEM((2, NS, 2, C, LANE), F32), pltpu.VMEM((2, NS, SUB, LANE), F32)],
        compiler_params=pltpu.CompilerParams(
            dimension_semantics=("parallel", "arbitrary"), vmem_limit_bytes=VMEM_LIMIT),
        name="delta_multi_mixer",
    )(proj3, proj3, proj3, proj3, proj3, prev, s0, cw, alog_row, dtb_row, onorm)
    return oa.reshape(nb * seq, W_A), sd.reshape(nb, H_A, HEAD_A, HEAD_A)


def _rwkv_kernel(r_ref, k_ref, v_ref, sm_ref, pr_ref, pk_ref, pv_ref, psm_ref, s0_ref,
                 mur_ref, muk_ref, muv_ref, musm_ref, w2_ref, a2_ref, g2_ref,
                 w0_ref, a0_ref, kk_ref, ka_ref, rk_ref, lnw_ref, lnb_ref,
                 o_ref, s_ref, *, C, NB, G):
    R = NB * C
    seqs = range(NB)
    pairs = range(G)

    r2 = _iota((LANE, LANE), 0)
    c2 = _iota((LANE, LANE), 1)
    ones_bd = ((r2 < HEAD_B) == (c2 < HEAD_B)).astype(F32)
    hr_ref, hk_ref, hv_ref, hsm_ref = pr_ref, pk_ref, pv_ref, psm_ref
    tri, ones_seq = _seq_masks(R, C)

    def lerp(x, hist_ref, mu):
        hist = hist_ref[...] if NB == 1 else [_repeat_rows(hist_ref[0], SUB)]
        return x + (_delayed(x, hist, 1, NB) - x) * mu

    sm = sm_ref[...]
    xs = lerp(sm, hsm_ref, musm_ref[...])
    slab_wa = xs[:, SM_WA:SM_WA + LANE]
    slab_g = xs[:, SM_G:SM_G + 2 * LANE]

    def col(x, p):
        return x[:, p * LANE:(p + 1) * LANE]

    def to_rows(x):
        return jnp.concatenate([col(x, p) for p in pairs], axis=0)

    def to_cols(y):
        return jnp.concatenate([y[p * R:(p + 1) * R] for p in pairs], axis=1)

    def seq_rows(x, n):
        return x[n * C:(n + 1) * C]

    xr = lerp(r_ref[...], hr_ref, mur_ref[...])
    xk = lerp(k_ref[...], hk_ref, muk_ref[...])
    xv = lerp(v_ref[...], hv_ref, muv_ref[...])
    wlog = -_softplus(-(w0_ref[...] + _bdot(jnp.tanh(slab_wa), w2_ref[...]))) - 0.5
    ld = -jnp.exp(wlog)
    aa = _sigmoid(a0_ref[...] + _bdot(slab_wa, a2_ref[...]))
    gate = _bdot(_sigmoid(slab_g), g2_ref[...])
    kkr = xk * kk_ref[...]
    kkn = kkr * lax.rsqrt(to_cols(_bdot(to_rows(kkr * kkr), ones_bd)) + 1e-6)
    k2 = xk * (1.0 + (aa - 1.0) * ka_ref[...])
    lp = _xdot_r(tri, ld, 3)
    lp_tot = lp[C - 1:C, :] if NB == 1 else _xdot_r(ones_seq, ld, 3)
    e_neg = jnp.exp(-lp)
    e_rem = jnp.exp(lp_tot - lp)
    at = -kkn * jnp.exp(lp - ld)
    rt = xr * jnp.exp(lp)
    kb = kkn * aa
    bt = kb * e_neg
    kt = k2 * e_neg
    bhat = kb * e_rem
    khat = k2 * e_rem
    p_tot = jnp.exp(lp_tot)

    s = [[s0_ref[n, p] for p in pairs] for n in seqs]
    atp = [col(at, p) for p in pairs]
    rtp = [col(rt, p) for p in pairs]
    first_head = _iota((2 * C, LANE), 1) < HEAD_B
    top_rows = _iota((LANE, LANE), 0) < HEAD_B

    def other_half(x):
        return pltpu.roll(x, HEAD_B, 1)

    def times_state_t(a, sp):
        both = _bdot_nt(jnp.concatenate([a[:, :HEAD_B], other_half(a)[:, :HEAD_B]], axis=0), sp)
        m = a.shape[0]
        return jnp.where(first_head, both[0:m], both[m:2 * m])

    ars = [[times_state_t(jnp.concatenate([seq_rows(atp[p], n), seq_rows(rtp[p], n)], axis=0), s[n][p])
            for p in pairs] for n in seqs]
    if NB == 1:
        x_state = [ars[0][p][0:C] for p in pairs]
        o_state = [ars[0][p][C:2 * C] for p in pairs]
    else:
        x_state = [jnp.concatenate([ars[n][p][0:C] for n in seqs], axis=0) for p in pairs]
        o_state = [jnp.concatenate([ars[n][p][C:2 * C] for n in seqs], axis=0) for p in pairs]
    pk = _Packed(R, C)
    stack_mask = (_iota((2 * R, LANE), 0) < R) == (_iota((2 * R, LANE), 1) < HEAD_B)

    def by_head(x):
        return jnp.where(stack_mask, jnp.concatenate([x, x], axis=0), 0.0)

    ar = [jnp.concatenate([atp[p], rtp[p]], axis=0) for p in pairs]
    ab = [_bdot_nt(ar[p], by_head(col(bt, p))) for p in pairs]
    ak = [_bdot_nt(ar[p], by_head(col(kt, p))) for p in pairs]
    tinv = pk.inverse_many([jnp.where(pk.strict, m[0:R], 0.0) for m in ab])
    v_bh = [by_head(col(xv, p)) for p in pairs]
    y = [x_state[p] + _bdot(jnp.where(pk.strict, ak[p][0:R], 0.0), v_bh[p]) for p in pairs]
    u = [_bdot(tinv[p], by_head(y[p])) for p in pairs]
    o = [o_state[p] + _bdot(jnp.where(pk.incl, ab[p][R:2 * R], 0.0), by_head(u[p]))
         + _bdot(jnp.where(pk.incl, ak[p][R:2 * R], 0.0), v_bh[p]) for p in pairs]
    for n in seqs:
        for p in pairs:
            uv = jnp.concatenate([seq_rows(u[p], n), seq_rows(col(xv, p), n)], axis=0)
            bkhat = jnp.concatenate([seq_rows(col(bhat, p), n), seq_rows(col(khat, p), n)], axis=0)
            decay = col(p_tot, p) if NB == 1 else col(p_tot, p)[n * C:n * C + 1]
            full = _bdot_tn(uv, bkhat)
            upd = jnp.where(top_rows, full, other_half(full))[:, :HEAD_B]
            dec = jnp.where(top_rows[:, :HEAD_B], decay[:, :HEAD_B], other_half(decay)[:, :HEAD_B])
            s_ref[n, p] = s[n][p] * dec + upd

    o_rows = jnp.concatenate(o, axis=0)
    mean = _bdot(o_rows, ones_bd) * (1.0 / HEAD_B)
    d = o_rows - mean
    var = _bdot(d * d, ones_bd) * (1.0 / HEAD_B)
    on = to_cols(d * lax.rsqrt(var + GN_EPS)) * lnw_ref[...] + lnb_ref[...]
    bonus = to_cols(_bdot(to_rows(xr * k2 * rk_ref[...]), ones_bd)) * xv
    o_ref[...] = ((on + bonus) * gate).astype(BF16)


def _rwkv_mixer(proj, row0, nb, seq, C, NB, G, prev_rkv, prev_sm, s0, mu_rkv, mu_sm, w2p, a2p, g2p,
                w0, a0, k_k, k_a, r_k, lnw, lnb):
    nch = seq // C
    assert nch == 1 and (NB == 1 or C == SUB)
    R = NB * C
    ng = N_PAIR // G
    gw = G * LANE
    blk0 = row0 // R
    bcast = s0.shape[0] == 1
    bsel = (lambda b: 0) if bcast else (lambda b: b)

    def proj_spec(col0):
        return pl.BlockSpec((R, gw), lambda b, g, c: (blk0 + b * nch + c, col0 // gw + g))

    def prev_spec(part):
        if NB == 1:
            return pl.BlockSpec((SUB, gw), lambda b, g, c: (0, part * (W_B // gw) + g))
        return pl.BlockSpec((1, NB, gw), lambda b, g, c: (0, b, part * (W_B // gw) + g))

    def vec_spec(part=0):
        return pl.BlockSpec((1, gw), lambda b, g, c: (0, part * (W_B // gw) + g))

    in_specs = [
        proj_spec(RKV0), proj_spec(RKV0 + W_B), proj_spec(RKV0 + 2 * W_B),
        pl.BlockSpec((R, SM_W), lambda b, g, c: (blk0 + b * nch + c, SM0 // SM_W)),
        prev_spec(0), prev_spec(1), prev_spec(2),
        (pl.BlockSpec((SUB, SM_W), lambda b, g, c: (0, 0)) if NB == 1
         else pl.BlockSpec((1, NB, SM_W), lambda b, g, c: (0, b, 0))),
        pl.BlockSpec((NB, G, LANE, HEAD_B), lambda b, g, c: (bsel(b), g, 0, 0)),
        vec_spec(0), vec_spec(1), vec_spec(2),
        pl.BlockSpec((1, SM_W), lambda b, g, c: (0, 0)),
        pl.BlockSpec((LANE, gw), lambda b, g, c: (0, g)),
        pl.BlockSpec((LANE, gw), lambda b, g, c: (0, g)),
        pl.BlockSpec((2 * LANE, gw), lambda b, g, c: (0, g)),
        vec_spec(), vec_spec(), vec_spec(), vec_spec(), vec_spec(), vec_spec(), vec_spec(),
    ]
    return pl.pallas_call(
        functools.partial(_rwkv_kernel, C=C, NB=NB, G=G),
        out_shape=(jax.ShapeDtypeStruct((nb * seq, W_B), BF16),
                   jax.ShapeDtypeStruct((nb, N_PAIR, LANE, HEAD_B), F32)),
        grid=(nb // NB, ng, nch),
        in_specs=in_specs,
        out_specs=(pl.BlockSpec((R, gw), lambda b, g, c: (b * nch + c, g)),
                   pl.BlockSpec((NB, G, LANE, HEAD_B), lambda b, g, c: (b, g, 0, 0))),
        compiler_params=pltpu.CompilerParams(
            dimension_semantics=("parallel", "parallel", "arbitrary"), vmem_limit_bytes=VMEM_LIMIT),
        name="rwkv_mixer",
    )(proj, proj, proj, proj, prev_rkv, prev_rkv, prev_rkv, prev_sm, s0,
      mu_rkv, mu_rkv, mu_rkv, mu_sm, w2p, a2p, g2p, w0, a0, k_k, k_a, r_k, lnw, lnb)


_AT, _RT, _BT, _KT, _BHAT, _KHAT, _XV, _GATE, _BONUS, _N_PRE = range(10)


def _rwkv_multi_kernel(r0_ref, k0_ref, v0_ref, sm0_ref, rn_ref, kn_ref, vn_ref, smn_ref,
                       pr_ref, pk_ref, pv_ref, psm_ref, s0_ref,
                       mur_ref, muk_ref, muv_ref, musm_ref, w2_ref, a2_ref, g2_ref,
                       w0_ref, a0_ref, kk_ref, ka_ref, rk_ref, lnw_ref, lnb_ref,
                       o_ref, s_ref, cr_ref, ck_ref, cv_ref, csm_ref, st_ref, pre_ref, ptot_ref, *, C, G, NS):
    ci = pl.program_id(1)
    nch = pl.num_programs(1)
    pairs = range(G)
    chains = [(s, p) for s in range(NS) for p in pairs]

    r2 = _iota((LANE, LANE), 0)
    c2 = _iota((LANE, LANE), 1)
    same_head = (r2 < HEAD_B) == (c2 < HEAD_B)
    ones_bd = same_head.astype(F32)
    spread = (_iota((HEAD_B, LANE), 0) == (_iota((HEAD_B, LANE), 1) & (HEAD_B - 1))).astype(F32)
    gather = ((_iota((LANE, HEAD_B), 0) & (HEAD_B - 1)) == _iota((LANE, HEAD_B), 1)).astype(F32)
    tri, _ = _seq_masks(C, C)

    def col(x, p):
        return x[:, p * LANE:(p + 1) * LANE]

    def to_rows(x):
        return jnp.concatenate([col(x, p) for p in pairs], axis=0)

    def to_cols(y):
        return jnp.concatenate([y[p * C:(p + 1) * C] for p in pairs], axis=1)

    def lerp(x, hist, mu):
        return x + (_shifted(x, hist, 1) - x) * mu

    def preamble(r_ref, k_ref, v_ref, sm_ref, hist_of, slot):
        for s in range(NS):
            hr, hk, hv, hsm = hist_of(s)
            xs = lerp(sm_ref[s], hsm, musm_ref[...])
            yield
            slab_wa = xs[:, SM_WA:SM_WA + LANE]
            wl = _bdot(jnp.tanh(slab_wa), w2_ref[...])
            yield
            al = _bdot(slab_wa, a2_ref[...])
            yield
            pre_ref[slot, s, _GATE] = _bdot(_sigmoid(xs[:, SM_G:SM_G + 2 * LANE]), g2_ref[...])
            yield
            xr = lerp(r_ref[s], hr, mur_ref[...])
            yield
            xk = lerp(k_ref[s], hk, muk_ref[...])
            yield
            xv = lerp(v_ref[s], hv, muv_ref[...])
            pre_ref[slot, s, _XV] = xv
            yield
            ld = -jnp.exp(-_softplus(-(w0_ref[...] + wl)) - 0.5)
            yield
            lp = _xdot_r(tri, ld, 3)
            yield
            aa = _sigmoid(a0_ref[...] + al)
            yield
            kkr = xk * kk_ref[...]
            ss = to_cols(_bdot(to_rows(kkr * kkr), ones_bd))
            yield
            kkn = kkr * lax.rsqrt(ss + 1e-6)
            k2 = xk * (1.0 + (aa - 1.0) * ka_ref[...])
            yield
            pre_ref[slot, s, _BONUS] = to_cols(_bdot(to_rows(xr * k2 * rk_ref[...]), ones_bd)) * xv
            yield
            pre_ref[slot, s, _AT] = -kkn * jnp.exp(lp - ld)
            yield
            pre_ref[slot, s, _RT] = xr * jnp.exp(lp)
            yield
            kb = kkn * aa
            e_neg = jnp.exp(-lp)
            pre_ref[slot, s, _BT] = kb * e_neg
            yield
            pre_ref[slot, s, _KT] = k2 * e_neg
            yield
            lp_tot = lp[C - 1:C, :]
            e_rem = jnp.exp(lp_tot - lp)
            pre_ref[slot, s, _BHAT] = kb * e_rem
            yield
            pre_ref[slot, s, _KHAT] = k2 * e_rem
            ptot_ref[slot, s] = jnp.broadcast_to(jnp.exp(lp_tot), (SUB, lp.shape[1]))
            yield
            cr_ref[s] = r_ref[s, C - SUB:C, :]
            ck_ref[s] = k_ref[s, C - SUB:C, :]
            cv_ref[s] = v_ref[s, C - SUB:C, :]
            csm_ref[s] = sm_ref[s, C - SUB:C, :]

    @pl.when(ci == 0)
    def _():
        first_hist = lambda s: (pr_ref[...], pk_ref[...], pv_ref[...], psm_ref[...])
        for _ in preamble(r0_ref, k0_ref, v0_ref, sm0_ref, first_hist, 0):
            pass
        for s, p in chains:
            st_ref[s * G + p] = jnp.where(same_head, _xdot(s0_ref[p], spread, 3), 0.0)

    slot = ci % 2
    carried = lambda s: (cr_ref[s], ck_ref[s], cv_ref[s], csm_ref[s])
    ahead = preamble(rn_ref, kn_ref, vn_ref, smn_ref, carried, 1 - slot)
    calls = [0]

    def tick():
        calls[0] += 1
        if calls[0] % 6 == 0:
            next(ahead, None)

    def each(fn, items):
        out = {}
        for it in items:
            out[it] = fn(*it)
            tick()
        return out

    def pre(idx, s, p):
        return pre_ref[slot, s, idx, :, p * LANE:(p + 1) * LANE]

    pk = _Packed(C, C)
    stack_mask = (_iota((2 * C, LANE), 0) < C) == (_iota((2 * C, LANE), 1) < HEAD_B)

    def by_head(x):
        return jnp.where(stack_mask, jnp.concatenate([x, x], axis=0), 0.0)

    st = {(s, p): st_ref[s * G + p] for s, p in chains}
    ar = {(s, p): jnp.concatenate([pre(_AT, s, p), pre(_RT, s, p)], axis=0) for s, p in chains}
    ars = each(lambda s, p: _bdot_nt(ar[s, p], st[s, p]), chains)
    ab = each(lambda s, p: _bdot_nt(ar[s, p], by_head(pre(_BT, s, p))), chains)
    ak = each(lambda s, p: _bdot_nt(ar[s, p], by_head(pre(_KT, s, p))), chains)
    ps = {key: jnp.where(pk.strict, ab[key][0:C], 0.0) for key in chains}
    ts = {key: pk.eye + ps[key] for key in chains}
    n = 1
    while 2 * n < C:
        ps = each(lambda s, p: _bdot(ps[s, p], pk.block_diag(ps[s, p])), chains)
        ts = each(lambda s, p: ts[s, p] + _bdot(ps[s, p], pk.block_diag(ts[s, p])), chains)
        n *= 2
    xv = {(s, p): pre(_XV, s, p) for s, p in chains}
    v_bh = {key: by_head(xv[key]) for key in chains}
    y = each(lambda s, p: ars[s, p][0:C] + _bdot(jnp.where(pk.strict, ak[s, p][0:C], 0.0), v_bh[s, p]), chains)
    u = each(lambda s, p: _bdot(ts[s, p], by_head(y[s, p])), chains)
    o = each(lambda s, p: ars[s, p][C:2 * C]
             + _bdot(jnp.where(pk.incl, ab[s, p][C:2 * C], 0.0), by_head(u[s, p]))
             + _bdot(jnp.where(pk.incl, ak[s, p][C:2 * C], 0.0), v_bh[s, p]), chains)
    for s, p in chains:
        uv = jnp.concatenate([u[s, p], xv[s, p]], axis=0)
        bkhat = jnp.concatenate([pre(_BHAT, s, p), pre(_KHAT, s, p)], axis=0)
        s_new = st[s, p] * ptot_ref[slot, s, 0:1, p * LANE:(p + 1) * LANE] + _bdot_tn(uv, bkhat)
        st_ref[s * G + p] = jnp.where(same_head, s_new, 0.0)
        tick()

    o_rows = jnp.concatenate([o[key] for key in chains], axis=0)
    mean = _bdot(o_rows, ones_bd) * (1.0 / HEAD_B)
    d = o_rows - mean
    var = _bdot(d * d, ones_bd) * (1.0 / HEAD_B)
    dn = d * lax.rsqrt(var + GN_EPS)
    for s in range(NS):
        on = to_cols(dn[s * G * C:(s + 1) * G * C]) * lnw_ref[...] + lnb_ref[...]
        o_ref[s] = ((on + pre_ref[slot, s, _BONUS]) * pre_ref[slot, s, _GATE]).astype(BF16)
    for _ in ahead:
        pass

    @pl.when(ci == nch - 1)
    def _():
        for s, p in chains:
            s_ref[s, 0, p] = _xdot(st_ref[s * G + p], gather, 3)


def _rwkv_multi_mixer(proj, nb, seq, C, NS, prev_rkv, prev_sm, s0, mu_rkv, mu_sm, w2p, a2p, g2p,
                      w0, a0, k_k, k_a, r_k, lnw, lnb):
    G = N_PAIR
    nch = seq // C
    ng = nb // NS
    gw = G * LANE
    proj3 = proj.reshape(NS, ng * seq, P_CAT)

    def first_spec(col0, w):
        return pl.BlockSpec((NS, C, w), lambda i, c: (0, i * nch, col0 // w))

    def next_spec(col0, w):
        return pl.BlockSpec((NS, C, w), lambda i, c: (0, i * nch + jnp.minimum(c + 1, nch - 1), col0 // w))

    def prev_spec(part):
        return pl.BlockSpec((SUB, gw), lambda i, c: (0, part))

    def vec_spec(part=0):
        return pl.BlockSpec((1, gw), lambda i, c: (0, part))

    in_specs = [
        first_spec(RKV0, gw), first_spec(RKV0 + W_B, gw), first_spec(RKV0 + 2 * W_B, gw), first_spec(SM0, SM_W),
        next_spec(RKV0, gw), next_spec(RKV0 + W_B, gw), next_spec(RKV0 + 2 * W_B, gw), next_spec(SM0, SM_W),
        prev_spec(0), prev_spec(1), prev_spec(2),
        pl.BlockSpec((SUB, SM_W), lambda i, c: (0, 0)),
        pl.BlockSpec((None, G, LANE, HEAD_B), lambda i, c: (0, 0, 0, 0)),
        vec_spec(0), vec_spec(1), vec_spec(2),
        pl.BlockSpec((1, SM_W), lambda i, c: (0, 0)),
        pl.BlockSpec((LANE, gw), lambda i, c: (0, 0)),
        pl.BlockSpec((LANE, gw), lambda i, c: (0, 0)),
        pl.BlockSpec((2 * LANE, gw), lambda i, c: (0, 0)),
        vec_spec(), vec_spec(), vec_spec(), vec_spec(), vec_spec(), vec_spec(), vec_spec(),
    ]
    ob, sw = pl.pallas_call(
        functools.partial(_rwkv_multi_kernel, C=C, G=G, NS=NS),
        out_shape=(jax.ShapeDtypeStruct((NS, ng * seq, W_B), BF16),
                   jax.ShapeDtypeStruct((NS, ng, N_PAIR, LANE, HEAD_B), F32)),
        grid=(ng, nch),
        in_specs=in_specs,
        out_specs=(pl.BlockSpec((NS, C, gw), lambda i, c: (0, i * nch + c, 0)),
                   pl.BlockSpec((NS, 1, G, LANE, HEAD_B), lambda i, c: (0, i, 0, 0, 0))),
        scratch_shapes=[pltpu.VMEM((NS, SUB, gw), F32), pltpu.VMEM((NS, SUB, gw), F32), pltpu.VMEM((NS, SUB, gw), F32),
                        pltpu.VMEM((NS, SUB, SM_W), F32), pltpu.VMEM((NS * G, LANE, LANE), F32),
                        pltpu.VMEM((2, NS, _N_PRE, C, gw), F32), pltpu.VMEM((2, NS, SUB, gw), F32)],
        compiler_params=pltpu.CompilerParams(
            dimension_semantics=("parallel", "arbitrary"), vmem_limit_bytes=VMEM_LIMIT),
        name="rwkv_multi_mixer",
    )(proj3, proj3, proj3, proj3, proj3, proj3, proj3, proj3, prev_rkv, prev_rkv, prev_rkv, prev_sm, s0,
      mu_rkv, mu_rkv, mu_rkv, mu_sm, w2p, a2p, g2p, w0, a0, k_k, k_a, r_k, lnw, lnb)
    return ob.reshape(nb * seq, W_B), sw.reshape(nb, N_PAIR, LANE, HEAD_B)


def _small_layout(cols_ba, cols_w, cols_a, cols_g, axis=-1):
    def z(n):
        shape = list(cols_w.shape)
        shape[axis] = n
        return jnp.zeros(shape, cols_w.dtype)
    return jnp.concatenate(
        [cols_ba, z(SM_WA - cols_ba.shape[axis]), cols_w, cols_a, cols_g,
         z(SM_W - SM_G - cols_g.shape[axis])], axis=axis)


def kernel(x_prompt, x_sample, state_delta, state_conv_qkv, state_wkv, state_shift, state_ffn_conv, meta, norm1, w_in, conv_a, a_log, dt_bias, onorm_a, mu_b, w0, w2, a0, a2, g2, k_k, k_a, r_k, lnx_w, lnx_b, w_o, norm2, w_ffn_in, conv_f, w_ffn_out, norm_f):
    nbp, seq_p, _ = x_prompt.shape
    nbs, seq_s, _ = x_sample.shape
    n_s = nbs * seq_s
    assert w_in.shape[0] == 1, "single-layer trunk"
    assert seq_s == SUB
    l = 0

    wt = w_in[l].T
    o_b = A_PROJ
    o_l = A_PROJ + 3 * W_B
    w_cat_t = jnp.concatenate([
        wt[:4 * W_A], wt[o_b:o_l],
        _small_layout(wt[4 * W_A:A_PROJ], wt[o_l:o_l + W_LORA], wt[o_l + W_LORA:o_l + W_LORA + A_LORA],
                      wt[o_l + W_LORA + A_LORA:], axis=0)], axis=0).astype(BF16)
    mu = mu_b[l]
    mu_rkv = mu[None, :3 * W_B]
    mu_sm = _small_layout(jnp.zeros((1, 2 * H_A), F32), mu[None, 3 * W_B:3 * W_B + W_LORA],
                          mu[None, 3 * W_B + W_LORA:3 * W_B + W_LORA + A_LORA],
                          mu[None, 3 * W_B + W_LORA + A_LORA:])
    w2p = jnp.concatenate([w2[l], jnp.zeros((LANE - W_LORA, W_B), F32)], axis=0)
    a2p = jnp.concatenate([jnp.zeros((W_LORA, W_B), F32), a2[l]], axis=0)
    g2p = jnp.concatenate([g2[l], jnp.zeros((2 * LANE - G_LORA, W_B), F32)], axis=0)
    alog_row = jnp.concatenate([jnp.zeros((H_A,), F32), a_log[l], jnp.zeros((LANE - 2 * H_A,), F32)])[None]
    dtb_row = jnp.concatenate([jnp.zeros((H_A,), F32), dt_bias[l], jnp.zeros((LANE - 2 * H_A,), F32)])[None]
    wo_bf = w_o[l].astype(BF16)
    wfo_bf = w_ffn_out[l].astype(BF16)
    row = lambda v: v.reshape(1, -1)

    def mix(proj, row0, nb, seq, C, NB, prev_qkv, prev_rkv, prev_sm, s_delta, s_wkv):
        delta_params = (conv_a[l], alog_row, dtb_row, row(onorm_a[l]))
        rwkv_params = (mu_rkv, mu_sm, w2p, a2p, g2p, row(w0[l]), row(a0[l]), row(k_k[l]), row(k_a[l]),
                       row(r_k[l]), row(lnx_w[l]), row(lnx_b[l]))
        if seq // C > 1:
            assert row0 == 0 and NB == 1
            oa, sd = _delta_multi_mixer(proj, nb, seq, C, 4, prev_qkv, s_delta, *delta_params)
            ob, sw = _rwkv_multi_mixer(proj, nb, seq, C, 2, prev_rkv, prev_sm, s_wkv, *rwkv_params)
        else:
            oa, sd = _delta_mixer(proj, row0, nb, seq, C, NB, prev_qkv, s_delta, *delta_params)
            ob, sw = _rwkv_mixer(proj, row0, nb, seq, C, NB, N_PAIR, prev_rkv, prev_sm, s_wkv, *rwkv_params)
        return oa, ob, sd, sw

    xs_rows = jnp.concatenate([x_sample.reshape(n_s, D_MODEL), meta], axis=0)
    xp_rows = x_prompt.reshape(nbp * seq_p, D_MODEL)
    n_small = n_s + N_META
    proj_s, proj_tail_s = _norm_matmul(xs_rows, row(norm1[l]), w_cat_t, n_small, 1536, 3, nbs)
    proj_p = _norm_matmul(xp_rows, row(norm1[l]), w_cat_t, 1024, 1536)

    zeros = lambda *s: jnp.zeros(s, F32)
    oa_m, ob_m, sd_m, sw_m = mix(proj_s, n_s, 1, N_META, N_META, 1, zeros(SUB, 3 * W_A),
                                 zeros(SUB, 3 * W_B), zeros(SUB, SM_W),
                                 zeros(1, H_A, HEAD_A, HEAD_A), zeros(1, N_PAIR, LANE, HEAD_B))
    tail = proj_s[n_small - SUB:n_small]
    oa_p, ob_p, sd_p, sw_p = mix(proj_p, 0, nbp, seq_p, 64, 1, tail[:, :3 * W_A],
                                 tail[:, RKV0:SM0], tail[:, SM0:], sd_m, sw_m)
    sh = state_shift[l]
    sh_sm = _small_layout(jnp.zeros((nbs, 1, 2 * H_A), F32), sh[..., 3 * W_B:3 * W_B + W_LORA],
                          sh[..., 3 * W_B + W_LORA:3 * W_B + W_LORA + A_LORA],
                          sh[..., 3 * W_B + W_LORA + A_LORA:])
    tmajor = lambda s: s.transpose(1, 0, 2)
    oa_s, ob_s, sd_s, sw_s = mix(proj_s, 0, nbs, seq_s, seq_s, 8, tmajor(state_conv_qkv[l]),
                                 tmajor(sh[..., :3 * W_B]), tmajor(sh_sm), state_delta[l],
                                 state_wkv[l].reshape(nbs, N_PAIR, LANE, HEAD_B))

    oa_small = jnp.concatenate([oa_s, oa_m], axis=0)
    ob_small = jnp.concatenate([ob_s, ob_m], axis=0)
    x1_s, h2_s = _out_proj(xs_rows, oa_small, ob_small, wo_bf, row(norm2[l]), n_small // 5)
    x1_p, h2_p = _out_proj(xp_rows, oa_p, ob_p, wo_bf, row(norm2[l]), 512)
    act_s, gate_tail_s, gate_last_s = _ffn_in(h2_s, w_ffn_in[l], state_ffn_conv[l].transpose(1, 0, 2), conv_f[l],
                                              n_small, 768, 3, 0, n_s + SUB)
    act_p, tail_p = _ffn_in(h2_p, w_ffn_in[l], gate_last_s[None], conv_f[l], 1024, 768, 3, seq_p // 1024)
    y_s = _ffn_out(act_s, wfo_bf, x1_s, row(norm_f), n_s, 1024, 768)
    y_p = _ffn_out(act_p, wfo_bf, x1_p, row(norm_f), nbp * seq_p, 1024, 768)

    def states(conv_new, last, ffn_new, nb, sd, sw):
        shift_new = jnp.concatenate([last[..., RKV0:SM0], last[..., SM0 + SM_WA:SM0 + SM_WA + W_LORA + A_LORA],
                                     last[..., SM0 + SM_G:SM0 + SM_G + G_LORA]], axis=-1)
        return (sd[None], conv_new[None], sw.reshape(nb, H_B, HEAD_B, HEAD_B)[None], shift_new[None],
                ffn_new[None])

    p3 = proj_p.reshape(nbp, seq_p, P_CAT)
    tail_s = proj_tail_s.transpose(1, 0, 2)
    return ((y_p.reshape(nbp, seq_p, D_MODEL), y_s.reshape(nbs, seq_s, D_MODEL))
            + states(p3[:, seq_p - 3:, :3 * W_A], p3[:, seq_p - 1:, :], tail_p[:, SUB - 2:, :], nbp, sd_p, sw_p)
            + states(tail_s[:, :, :3 * W_A], tail_s[:, 2:, :], gate_tail_s.transpose(1, 0, 2), nbs, sd_s, sw_s))
```

```python
import functools

import jax
import jax.numpy as jnp
from jax import lax
from jax.experimental import pallas as pl
from jax.experimental.pallas import tpu as pltpu

F32 = jnp.float32
BF16 = jnp.bfloat16

D_MODEL = 2048
N_META = 16
W_A = 1024
HEAD_A = 128
H_A = 8
W_B = 1024
HEAD_B = 64
H_B = 16
N_PAIR = H_B // 2
W_LORA = 64
A_LORA = 64
G_LORA = 160
D_FF = 5376
RMS_EPS = 1e-6
GN_EPS = 64e-5
A_PROJ = 4 * W_A + 2 * H_A
B_PROJ = 3 * W_B + W_LORA + A_LORA + G_LORA

QKVZ0 = 0
RKV0 = 4 * W_A
SM0 = RKV0 + 3 * W_B
SM_W = 512
SM_BA = 0
SM_WA = 128
SM_G = 256
P_CAT = SM0 + SM_W

LANE = 128
SUB = 8
VMEM_LIMIT = 48 * 1024 * 1024
VMEM_LIMIT_BIG = 58 * 1024 * 1024
NEG_BIG = -1e30

NT_DIMS = (((1,), (1,)), ((), ()))


def _bdot(a, b):
    return jnp.dot(a.astype(BF16), b.astype(BF16), preferred_element_type=F32)


def _bdot_nt(a, b):
    return lax.dot_general(a.astype(BF16), b.astype(BF16), NT_DIMS, preferred_element_type=F32)


def _bdot_tn(a, b):
    return lax.dot_general(a.astype(BF16), b.astype(BF16), (((0,), (0,)), ((), ())),
                           preferred_element_type=F32)


def _pieces(a, n):
    out = []
    rem = a
    for i in range(n):
        p = rem.astype(BF16)
        out.append(p)
        if i + 1 < n:
            rem = rem - p.astype(F32)
    return out


def _xdot(a, b, n, dims=(((1,), (0,)), ((), ()))):
    bb = b.astype(BF16)
    acc = None
    for p in _pieces(a, n):
        t = lax.dot_general(p, bb, dims, preferred_element_type=F32)
        acc = t if acc is None else acc + t
    return acc


def _xdot_r(a, b, n, dims=(((1,), (0,)), ((), ()))):
    ab = a.astype(BF16)
    acc = None
    for p in _pieces(b, n):
        t = lax.dot_general(ab, p, dims, preferred_element_type=F32)
        acc = t if acc is None else acc + t
    return acc


def _sigmoid(x):
    return 1.0 / (1.0 + jnp.exp(-x))


def _silu(x):
    return x * _sigmoid(x)


def _softplus(x):
    return jnp.maximum(x, 0.0) + jnp.log(1.0 + jnp.exp(-jnp.abs(x)))


def _iota(shape, dim):
    return lax.broadcasted_iota(jnp.int32, shape, dim)


def _shifted(x, prev8, k):
    n = x.shape[0]
    xr = pltpu.roll(x, k, 0)
    pr = pltpu.roll(prev8, k, 0)
    first = jnp.where(_iota((SUB, x.shape[1]), 0) < k, pr, xr[0:SUB])
    if n == SUB:
        return first
    return jnp.concatenate([first, xr[SUB:]], axis=0)


def _repeat_rows(s, reps):
    n, c = s.shape
    return jnp.broadcast_to(s[:, None, :], (n, reps, c)).reshape(n * reps, c)


def _delayed(x, hist, k, nseq):
    if nseq == 1:
        return _shifted(x, hist, k)
    t = _iota(x.shape, 0) & (SUB - 1)
    d = pltpu.roll(x, k, 0)
    for tt in range(k):
        d = jnp.where(t == tt, hist[len(hist) - k + tt], d)
    return d


def _same_seq(shape, c, rows_total):
    if c == rows_total:
        return None
    sh = c.bit_length() - 1
    return (lax.shift_right_logical(_iota(shape, 0) & (rows_total - 1), sh)
            == lax.shift_right_logical(_iota(shape, 1) & (rows_total - 1), sh))


class _Packed:
    def __init__(self, r, c):
        self.r = r
        self.c = c
        row = _iota((r, 2 * r), 0)
        col = _iota((r, 2 * r), 1) & (r - 1)
        same = _same_seq((r, 2 * r), c, r)
        self.incl = (row >= col) if same is None else ((row >= col) & same)
        self.strict = (row > col) if same is None else ((row > col) & same)
        self.eye = (row == col).astype(F32)
        row2 = _iota((2 * r, 2 * r), 0)
        lower2 = ((row2 & (r - 1)) - (_iota((2 * r, 2 * r), 1) & (r - 1)) + jnp.where(row2 < r, 0, 1)) > 0
        same2 = _same_seq((2 * r, 2 * r), c, r)
        self.strict_over_incl = lower2 if same2 is None else (lower2 & same2)
        self.first = _iota((r, 2 * r), 1) < r
        self.bd_mask = (_iota((2 * r, 2 * r), 0) < r) == (_iota((2 * r, 2 * r), 1) < r)

    def block_diag(self, m):
        return jnp.where(self.bd_mask, jnp.concatenate([m, m], axis=0), 0.0)

    def inverse_many(self, xs):
        ts = [self.eye + x for x in xs]
        ps = list(xs)
        n = 1
        while 2 * n < self.c:
            ps = [_bdot(p, self.block_diag(p)) for p in ps]
            ts = [t + _bdot(p, self.block_diag(t)) for p, t in zip(ps, ts)]
            n *= 2
        return ts


def _seq_masks(r, c):
    row = _iota((r, r), 0)
    col = _iota((r, r), 1)
    same = _same_seq((r, r), c, r)
    if same is None:
        return (row >= col).astype(F32), None
    return ((row >= col) & same).astype(F32), same.astype(F32)


def _seq_tail_rows(x, slab_ref, dst_ref, n_tail, nseq, col0=0):
    for s in range(x.shape[1] // LANE):
        slab_ref[s] = x[:, s * LANE:(s + 1) * LANE]
        for t in range(n_tail):
            dst_ref[t, :, col0 + s * LANE:col0 + (s + 1) * LANE] = (
                slab_ref[s, pl.ds(SUB - n_tail + t, nseq, stride=SUB), :])


def _norm_matmul_kernel(x_ref, g_ref, w_ref, o_ref, *rest, n_tail, nseq):
    tail_ref, h_ref, slab_ref = rest if n_tail else (None,) + rest + (None,)

    @pl.when(pl.program_id(1) == 0)
    def _():
        x = x_ref[...]
        ms = jnp.mean(x * x, axis=-1, keepdims=True)
        h_ref[...] = (x * lax.rsqrt(ms + RMS_EPS) * g_ref[...]).astype(BF16)

    out = lax.dot_general(h_ref[...], w_ref[...], NT_DIMS, preferred_element_type=F32)
    o_ref[...] = out
    if n_tail:
        _seq_tail_rows(out, slab_ref, tail_ref, n_tail, nseq)


def _norm_matmul(x, g, wt, tm, tn, n_tail=0, nseq=0):
    m, k = x.shape
    n = wt.shape[0]
    out_shape = jax.ShapeDtypeStruct((m, n), F32)
    out_specs = pl.BlockSpec((tm, tn), lambda i, j: (i, j))
    if n_tail:
        assert m == tm
        out_shape = (out_shape, jax.ShapeDtypeStruct((n_tail, nseq, n), F32))
        out_specs = (out_specs, pl.BlockSpec((n_tail, nseq, tn), lambda i, j: (0, 0, j)))
    return pl.pallas_call(
        functools.partial(_norm_matmul_kernel, n_tail=n_tail, nseq=nseq),
        out_shape=out_shape,
        grid=(m // tm, n // tn),
        in_specs=[
            pl.BlockSpec((tm, k), lambda i, j: (i, 0)),
            pl.BlockSpec((1, k), lambda i, j: (0, 0)),
            pl.BlockSpec((tn, k), lambda i, j: (j, 0)),
        ],
        out_specs=out_specs,
        scratch_shapes=[pltpu.VMEM((tm, k), BF16)] + ([pltpu.VMEM((tn // LANE, tm, LANE), F32)] if n_tail else []),
        compiler_params=pltpu.CompilerParams(
            dimension_semantics=("parallel", "arbitrary"), vmem_limit_bytes=VMEM_LIMIT_BIG),
        name="norm_matmul",
    )(x, g, wt)


def _out_proj_kernel(x_ref, oa_ref, ob_ref, wt_ref, wb_ref, g_ref, x1_ref, h_ref):
    acc = jnp.dot(oa_ref[...], wt_ref[...], preferred_element_type=F32)
    acc = acc + jnp.dot(ob_ref[...], wb_ref[...], preferred_element_type=F32)
    x1 = x_ref[...] + acc
    x1_ref[...] = x1
    ms = jnp.mean(x1 * x1, axis=-1, keepdims=True)
    h_ref[...] = (x1 * lax.rsqrt(ms + RMS_EPS) * g_ref[...]).astype(BF16)


def _out_proj(x, oa, ob, wo, g, tm):
    m, d = x.shape
    return pl.pallas_call(
        _out_proj_kernel,
        out_shape=(jax.ShapeDtypeStruct((m, d), F32), jax.ShapeDtypeStruct((m, d), BF16)),
        grid=(m // tm,),
        in_specs=[
            pl.BlockSpec((tm, d), lambda i: (i, 0)),
            pl.BlockSpec((tm, W_A), lambda i: (i, 0)),
            pl.BlockSpec((tm, W_B), lambda i: (i, 0)),
            pl.BlockSpec((W_A, d), lambda i: (0, 0)),
            pl.BlockSpec((W_B, d), lambda i: (W_A // W_B, 0)),
            pl.BlockSpec((1, d), lambda i: (0, 0)),
        ],
        out_specs=(pl.BlockSpec((tm, d), lambda i: (i, 0)), pl.BlockSpec((tm, d), lambda i: (i, 0))),
        compiler_params=pltpu.CompilerParams(
            dimension_semantics=("parallel",), vmem_limit_bytes=VMEM_LIMIT),
        name="out_proj",
    )(x, oa, ob, wo, wo, g)


def _ffn_in_kernel(h_ref, wg_ref, wu_ref, prev_ref, cw_ref, act_ref, tail_ref, *rest,
                   tiles_per_seq, state_rows, nsub):
    if tiles_per_seq:
        wgb_ref, wub_ref, carry_ref = rest
    else:
        last_ref, wgb_ref, wub_ref, carry_ref, slab_ref = rest
    m = pl.program_id(1)

    @pl.when(m == 0)
    def _():
        wgb_ref[...] = wg_ref[...].astype(BF16)
        wub_ref[...] = wu_ref[...].astype(BF16)

    if tiles_per_seq:
        @pl.when(m % tiles_per_seq == 0)
        def _():
            carry_ref[...] = prev_ref[...]

    h = h_ref[...]
    n = h.shape[0]
    w = wgb_ref.shape[1] // nsub
    for i in range(nsub):
        cs = slice(i * w, (i + 1) * w)
        gate = jnp.dot(h, wgb_ref[:, cs], preferred_element_type=F32)
        up = jnp.dot(h, wub_ref[:, cs], preferred_element_type=F32)
        if tiles_per_seq:
            prev8 = carry_ref[:, cs]
            d1 = _shifted(gate, prev8, 1)
            d2 = _shifted(gate, prev8, 2)
            carry_ref[:, cs] = gate[n - SUB:n]
            tail_ref[:, cs] = gate[n - SUB:n]
        else:
            nseq = (state_rows - SUB) // SUB
            _seq_tail_rows(gate, slab_ref, tail_ref, 2, nseq, i * w)
            last_ref[:, cs] = gate[n - SUB:n]
            pad = jnp.zeros((n - nseq * SUB, w), F32)
            hist = [jnp.concatenate([_repeat_rows(prev_ref[t, :, cs], SUB), pad], axis=0) for t in range(2)]
            t_in_seq = jnp.where(_iota(gate.shape, 0) < state_rows, _iota(gate.shape, 0) & (SUB - 1), SUB)
            d1 = jnp.where(t_in_seq == 0, hist[1], pltpu.roll(gate, 1, 0))
            d2 = jnp.where(t_in_seq == 0, hist[0], jnp.where(t_in_seq == 1, hist[1], pltpu.roll(gate, 2, 0)))
        y = gate * cw_ref[2:3, cs] + d1 * cw_ref[1:2, cs] + d2 * cw_ref[0:1, cs]
        act_ref[:, cs] = (_silu(y) * up).astype(BF16)


def _ffn_in(h, w, prev, cw, tm, tn, nsub, tiles_per_seq, state_rows=0):
    m, d = h.shape
    nj = D_FF // tn
    nm = m // tm
    if tiles_per_seq:
        nseq = nm // tiles_per_seq
        prev_spec = pl.BlockSpec((None, SUB, tn), lambda j, i: (0, 0, j))
        extra_shapes = (jax.ShapeDtypeStruct((nseq, SUB, D_FF), F32),)
        extra_specs = (pl.BlockSpec((None, SUB, tn), lambda j, i: (i // tiles_per_seq, 0, j)),)
        extra_scratch = []
    else:
        assert nm == 1
        nseq = (state_rows - SUB) // SUB
        prev_spec = pl.BlockSpec((2, nseq, tn), lambda j, i: (0, 0, j))
        extra_shapes = (jax.ShapeDtypeStruct((2, nseq, D_FF), F32), jax.ShapeDtypeStruct((SUB, D_FF), F32))
        extra_specs = (pl.BlockSpec((2, nseq, tn), lambda j, i: (0, 0, j)),
                       pl.BlockSpec((SUB, tn), lambda j, i: (0, j)))
        extra_scratch = [pltpu.VMEM((tn // nsub // LANE, tm, LANE), F32)]
    return pl.pallas_call(
        functools.partial(_ffn_in_kernel, tiles_per_seq=tiles_per_seq, state_rows=state_rows, nsub=nsub),
        out_shape=(jax.ShapeDtypeStruct((m, D_FF), BF16),) + extra_shapes,
        grid=(nj, nm),
        in_specs=[
            pl.BlockSpec((tm, d), lambda j, i: (i, 0)),
            pl.BlockSpec((d, tn), lambda j, i: (0, j)),
            pl.BlockSpec((d, tn), lambda j, i: (0, nj + j)),
            prev_spec,
            pl.BlockSpec((3, tn), lambda j, i: (0, j)),
        ],
        out_specs=(pl.BlockSpec((tm, tn), lambda j, i: (i, j)),) + extra_specs,
        scratch_shapes=[pltpu.VMEM((d, tn), BF16), pltpu.VMEM((d, tn), BF16), pltpu.VMEM((SUB, tn), F32)]
        + extra_scratch,
        compiler_params=pltpu.CompilerParams(
            dimension_semantics=("parallel", "arbitrary"), vmem_limit_bytes=VMEM_LIMIT_BIG),
        name="ffn_in",
    )(h, w, w, prev, cw)


def _ffn_out_kernel(a_ref, w_ref, x_ref, g_ref, o_ref):
    kk = pl.program_id(1)

    @pl.when(kk == 0)
    def _():
        o_ref[...] = x_ref[...]

    o_ref[...] += jnp.dot(a_ref[...], w_ref[...], preferred_element_type=F32)

    @pl.when(kk == pl.num_programs(1) - 1)
    def _():
        x = o_ref[...]
        ms = jnp.mean(x * x, axis=-1, keepdims=True)
        o_ref[...] = x * lax.rsqrt(ms + RMS_EPS) * g_ref[...]


def _ffn_out(act, w, x1, g, m, tm, tk):
    kdim, d = w.shape
    return pl.pallas_call(
        _ffn_out_kernel,
        out_shape=jax.ShapeDtypeStruct((m, d), F32),
        grid=(m // tm, kdim // tk),
        in_specs=[
            pl.BlockSpec((tm, tk), lambda i, k: (i, k)),
            pl.BlockSpec((tk, d), lambda i, k: (k, 0)),
            pl.BlockSpec((tm, d), lambda i, k: (i, 0)),
            pl.BlockSpec((1, d), lambda i, k: (0, 0)),
        ],
        out_specs=pl.BlockSpec((tm, d), lambda i, k: (i, 0)),
        compiler_params=pltpu.CompilerParams(
            dimension_semantics=("parallel", "arbitrary"), vmem_limit_bytes=VMEM_LIMIT),
        name="ffn_out",
    )(act, w, x1, g)


def _delta_kernel(qkvz_ref, ba_ref, prev_ref, s0_ref, cw_ref, alog_ref, dtb_ref, on_ref,
                  o_ref, s_ref, carry_ref, *, C, NB):
    R = NB * C
    ci = pl.program_id(1)

    @pl.when(ci == 0)
    def _():
        if NB == 1:
            carry_ref[...] = prev_ref[...]
        s_ref[...] = s0_ref[...]

    tri, ones_seq = _seq_masks(R, C)

    ba = ba_ref[...]
    beta_full = _sigmoid(ba)
    g_full = -jnp.exp(alog_ref[...]) * _softplus(ba + dtb_ref[...])
    gc_full = _xdot_r(tri, g_full, 3)
    gtot_full = gc_full[C - 1:C, :] if NB == 1 else _xdot_r(ones_seq, g_full, 3)

    def conv_silu(c0):
        x = qkvz_ref[:, c0:c0 + LANE]
        if NB == 1:
            hist = carry_ref[:, c0:c0 + LANE]
        else:
            hist = [_repeat_rows(prev_ref[j, :, c0:c0 + LANE], SUB) for j in range(3)]
        y = x * cw_ref[3:4, c0:c0 + LANE]
        for k in (1, 2, 3):
            y = y + _delayed(x, hist, k, NB) * cw_ref[3 - k:4 - k, c0:c0 + LANE]
        return _silu(y)

    ones_ll = jnp.ones((LANE, LANE), BF16)

    def lane_sum(x):
        return jnp.dot(x.astype(BF16), ones_ll, preferred_element_type=F32)

    def l2n_many(xs):
        sums = [lane_sum(x * x) for x in xs]
        return [x * lax.rsqrt(sq + 1e-6) for x, sq in zip(xs, sums)]

    def seq_rows(x, n):
        return x[n * C:(n + 1) * C]

    heads = range(H_A)
    seqs = range(NB)
    q = [x * (HEAD_A ** -0.5) for x in l2n_many([conv_silu(h * HEAD_A) for h in heads])]
    k = l2n_many([conv_silu(W_A + h * HEAD_A) for h in heads])
    v = [conv_silu(2 * W_A + h * HEAD_A) for h in heads]
    bcol = [beta_full[:, h:h + 1] for h in heads]
    gcol = [gc_full[:, H_A + h:H_A + h + 1] for h in heads]
    gtot = [gtot_full[:, H_A + h:H_A + h + 1] for h in heads]
    eg = [jnp.exp(gcol[h]) for h in heads]
    kb = [k[h] * bcol[h] for h in heads]

    pk = _Packed(R, C)
    hpairs = range(H_A // 2)
    zc = jnp.zeros((R, HEAD_A), F32)
    row2 = _iota((2 * R, LANE), 0)
    lane2 = _iota((2 * R, LANE), 1)
    ones_cl = jnp.ones((R, LANE), F32)
    gc2 = jnp.concatenate([gc_full, gc_full], axis=0)
    kq, gamma = [], []
    for hp in hpairs:
        h0, h1 = 2 * hp, 2 * hp + 1
        lhs = jnp.concatenate([jnp.concatenate([kb[h0], kb[h1]], axis=1),
                               jnp.concatenate([q[h0], q[h1]], axis=1)], axis=0)
        rk = jnp.concatenate([jnp.concatenate([k[h0], zc], axis=1),
                              jnp.concatenate([zc, k[h1]], axis=1)], axis=0)
        kq.append(_bdot_nt(lhs, rk))
        sel = lane2 == jnp.where(row2 < R, H_A + h0, H_A + h1)
        grow = _xdot_r(ones_cl, jnp.where(sel, gc2, 0.0), 3, NT_DIMS)
        gcol_p = jnp.where(pk.first, gcol[h0], gcol[h1])
        gamma.append(jnp.exp(jnp.where(pk.incl, gcol_p - grow, NEG_BIG)))
    tinv = pk.inverse_many([-jnp.where(pk.strict, kq[hp][0:R] * gamma[hp], 0.0) for hp in hpairs])
    uw = []
    for hp in hpairs:
        h0, h1 = 2 * hp, 2 * hp + 1
        z2 = jnp.zeros((R, 2 * HEAD_A), F32)
        rhs = jnp.concatenate([jnp.concatenate([v[h0] * bcol[h0], kb[h0] * eg[h0], z2], axis=1),
                               jnp.concatenate([z2, v[h1] * bcol[h1], kb[h1] * eg[h1]], axis=1)], axis=0)
        uw.append(_bdot(tinv[hp], rhs))
    u = [uw[h // 2][:, (h % 2) * 2 * HEAD_A:(h % 2) * 2 * HEAD_A + HEAD_A] for h in heads]
    w = [uw[h // 2][:, (h % 2) * 2 * HEAD_A + HEAD_A:(h % 2 + 1) * 2 * HEAD_A] for h in heads]
    qd = [q[h] * eg[h] for h in heads]
    s = [[s_ref[n, h] for h in heads] for n in seqs]
    wqs = [[_bdot(jnp.concatenate([seq_rows(w[h], n), seq_rows(qd[h], n)], axis=0), s[n][h]) for h in heads]
           for n in seqs]
    ws = [jnp.concatenate([wqs[n][h][0:C] for n in seqs], axis=0) if NB > 1 else wqs[0][h][0:C] for h in heads]
    qs = [jnp.concatenate([wqs[n][h][C:2 * C] for n in seqs], axis=0) if NB > 1 else wqs[0][h][C:2 * C]
          for h in heads]
    v_new = [u[h] - ws[h] for h in heads]
    o = []
    for hp in hpairs:
        h0, h1 = 2 * hp, 2 * hp + 1
        vn_bd = jnp.concatenate([jnp.concatenate([v_new[h0], zc], axis=1),
                                 jnp.concatenate([zc, v_new[h1]], axis=1)], axis=0)
        op = _bdot(kq[hp][R:2 * R] * gamma[hp], vn_bd)
        o.append(qs[h0] + op[:, :HEAD_A])
        o.append(qs[h1] + op[:, HEAD_A:])
    kd = [k[h] * jnp.exp(gtot[h] - gcol[h]) for h in heads]
    for n in seqs:
        for h in heads:
            glast = gtot[h] if NB == 1 else gtot[h][n * C:n * C + 1]
            s_ref[n, h] = s[n][h] * jnp.exp(glast) + _bdot_tn(seq_rows(kd[h], n), seq_rows(v_new[h], n))
    osq = [lane_sum(o[h] * o[h]) for h in heads]
    for h in heads:
        z = qkvz_ref[:, 3 * W_A + h * HEAD_A:3 * W_A + (h + 1) * HEAD_A]
        oh = o[h] * lax.rsqrt(osq[h] * (1.0 / HEAD_A) + RMS_EPS)
        o_ref[:, h * HEAD_A:(h + 1) * HEAD_A] = (oh * on_ref[...] * _silu(z)).astype(BF16)

    if NB == 1:
        carry_ref[...] = qkvz_ref[C - SUB:C, 0:3 * W_A]


def _delta_mixer(proj, row0, nb, seq, C, NB, prev, s0, cw, alog_row, dtb_row, onorm):
    nch = seq // C
    assert NB == 1 or (nch == 1 and C == SUB)
    R = NB * C
    blk0 = row0 // R
    bcast = s0.shape[0] == 1
    bsel = (lambda b: 0) if bcast else (lambda b: b)
    return pl.pallas_call(
        functools.partial(_delta_kernel, C=C, NB=NB),
        out_shape=(jax.ShapeDtypeStruct((nb * seq, W_A), BF16),
                   jax.ShapeDtypeStruct((nb, H_A, HEAD_A, HEAD_A), F32)),
        grid=(nb // NB, nch),
        in_specs=[
            pl.BlockSpec((R, 4 * W_A), lambda b, c: (blk0 + b * nch + c, 0)),
            pl.BlockSpec((R, LANE), lambda b, c: (blk0 + b * nch + c, SM0 // LANE)),
            (pl.BlockSpec((SUB, 3 * W_A), lambda b, c: (0, 0)) if NB == 1
             else pl.BlockSpec((3, NB, 3 * W_A), lambda b, c: (0, b, 0))),
            pl.BlockSpec((NB, H_A, HEAD_A, HEAD_A), lambda b, c: (bsel(b), 0, 0, 0)),
            pl.BlockSpec((4, 3 * W_A), lambda b, c: (0, 0)),
            pl.BlockSpec((1, LANE), lambda b, c: (0, 0)),
            pl.BlockSpec((1, LANE), lambda b, c: (0, 0)),
            pl.BlockSpec((1, HEAD_A), lambda b, c: (0, 0)),
        ],
        out_specs=(pl.BlockSpec((R, W_A), lambda b, c: (b * nch + c, 0)),
                   pl.BlockSpec((NB, H_A, HEAD_A, HEAD_A), lambda b, c: (b, 0, 0, 0))),
        scratch_shapes=[pltpu.VMEM((SUB, 3 * W_A), F32)],
        compiler_params=pltpu.CompilerParams(
            dimension_semantics=("parallel", "arbitrary"), vmem_limit_bytes=VMEM_LIMIT),
        name="delta_mixer",
    )(proj, proj, prev, s0, cw, alog_row, dtb_row, onorm)


def _delta_multi_kernel(qkv0_ref, ba0_ref, qkvn_ref, ban_ref, z_ref, prev_ref, s0_ref, cw_ref, alog_ref, dtb_ref,
                        on_ref, o_ref, s_ref, carry_ref, pre_ref, small_ref, gtot_ref, *, C, NS):
    ci = pl.program_id(1)
    heads = [(s, h) for s in range(NS) for h in range(H_A)]
    hpairs = [(s, hp) for s in range(NS) for hp in range(H_A // 2)]
    tri, _ = _seq_masks(C, C)
    ones_ll = jnp.ones((LANE, LANE), BF16)

    def lane_sum(x):
        return jnp.dot(x.astype(BF16), ones_ll, preferred_element_type=F32)

    def preamble(qkv_ref, ba_ref, hist_of, slot):
        for s in range(NS):
            ba = ba_ref[s]
            g_full = -jnp.exp(alog_ref[...]) * _softplus(ba + dtb_ref[...])
            gc_full = _xdot_r(tri, g_full, 3)
            small_ref[slot, s, 0] = _sigmoid(ba)
            small_ref[slot, s, 1] = gc_full
            gtot_ref[slot, s] = jnp.broadcast_to(gc_full[C - 1:C, :], (SUB, LANE))
            yield
        for part in range(3):
            xs = {}
            for s, h in heads:
                c0 = part * W_A + h * HEAD_A
                x = qkv_ref[s, :, c0:c0 + LANE]
                hist = hist_of(s, c0)
                y = x * cw_ref[3:4, c0:c0 + LANE]
                for k in (1, 2, 3):
                    y = y + _shifted(x, hist, k) * cw_ref[3 - k:4 - k, c0:c0 + LANE]
                xs[s, h] = _silu(y)
                yield
            if part < 2:
                sums = {key: lane_sum(x * x) for key, x in xs.items()}
                yield
                scale = HEAD_A ** -0.5 if part == 0 else 1.0
                xs = {key: x * (lax.rsqrt(sums[key] + 1e-6) * scale) for key, x in xs.items()}
            for s, h in heads:
                pre_ref[slot, s, part, :, h * HEAD_A:(h + 1) * HEAD_A] = xs[s, h]
            yield
        for s in range(NS):
            carry_ref[s] = qkv_ref[s, C - SUB:C, :]

    @pl.when(ci == 0)
    def _():
        for _ in preamble(qkv0_ref, ba0_ref, lambda s, c0: prev_ref[:, c0:c0 + LANE], 0):
            pass
        for s in range(NS):
            s_ref[s, 0] = s0_ref[...]

    slot = ci % 2
    ahead = preamble(qkvn_ref, ban_ref, lambda s, c0: carry_ref[s, :, c0:c0 + LANE], 1 - slot)
    calls = [0]

    def tick():
        calls[0] += 1
        if calls[0] % 3 == 0:
            next(ahead, None)

    def each(fn, items):
        out = {}
        for it in items:
            out[it] = fn(*it)
            tick()
        return out

    beta_full = {s: small_ref[slot, s, 0] for s in range(NS)}
    gc_full = {s: small_ref[slot, s, 1] for s in range(NS)}
    gtot_row = {s: gtot_ref[slot, s] for s in range(NS)}
    q = {(s, h): pre_ref[slot, s, 0, :, h * HEAD_A:(h + 1) * HEAD_A] for s, h in heads}
    k = {(s, h): pre_ref[slot, s, 1, :, h * HEAD_A:(h + 1) * HEAD_A] for s, h in heads}
    v = {(s, h): pre_ref[slot, s, 2, :, h * HEAD_A:(h + 1) * HEAD_A] for s, h in heads}
    bcol = {(s, h): beta_full[s][:, h:h + 1] for s, h in heads}
    gcol = {(s, h): gc_full[s][:, H_A + h:H_A + h + 1] for s, h in heads}
    gtot = {(s, h): gtot_row[s][0:1, H_A + h:H_A + h + 1] for s, h in heads}
    eg = {key: jnp.exp(gcol[key]) for key in heads}
    kb = {key: k[key] * bcol[key] for key in heads}

    pk = _Packed(C, C)
    zc = jnp.zeros((C, HEAD_A), F32)
    row2 = _iota((2 * C, LANE), 0)
    lane2 = _iota((2 * C, LANE), 1)
    ones_cl = jnp.ones((C, LANE), F32)

    def kq_of(s, hp):
        a, b = (s, 2 * hp), (s, 2 * hp + 1)
        lhs = jnp.concatenate([jnp.concatenate([kb[a], kb[b]], axis=1),
                               jnp.concatenate([q[a], q[b]], axis=1)], axis=0)
        rk = jnp.concatenate([jnp.concatenate([k[a], zc], axis=1),
                              jnp.concatenate([zc, k[b]], axis=1)], axis=0)
        return _bdot_nt(lhs, rk)

    def gamma_of(s, hp):
        a, b = (s, 2 * hp), (s, 2 * hp + 1)
        sel = lane2 == jnp.where(row2 < C, H_A + 2 * hp, H_A + 2 * hp + 1)
        gc2 = jnp.concatenate([gc_full[s], gc_full[s]], axis=0)
        grow = _xdot_r(ones_cl, jnp.where(sel, gc2, 0.0), 3, NT_DIMS)
        gcol_p = jnp.where(pk.first, gcol[a], gcol[b])
        return jnp.exp(jnp.where(pk.incl, gcol_p - grow, NEG_BIG))

    kq = each(kq_of, hpairs)
    gamma = each(gamma_of, hpairs)
    ps = {key: -jnp.where(pk.strict, kq[key][0:C] * gamma[key], 0.0) for key in hpairs}
    ts = {key: pk.eye + ps[key] for key in hpairs}
    n = 1
    while 2 * n < C:
        ps = each(lambda s, hp: _bdot(ps[s, hp], pk.block_diag(ps[s, hp])), hpairs)
        ts = each(lambda s, hp: ts[s, hp] + _bdot(ps[s, hp], pk.block_diag(ts[s, hp])), hpairs)
        n *= 2

    def uw_of(s, hp):
        a, b = (s, 2 * hp), (s, 2 * hp + 1)
        z2 = jnp.zeros((C, 2 * HEAD_A), F32)
        rhs = jnp.concatenate([jnp.concatenate([v[a] * bcol[a], kb[a] * eg[a], z2], axis=1),
                               jnp.concatenate([z2, v[b] * bcol[b], kb[b] * eg[b]], axis=1)], axis=0)
        return _bdot(ts[s, hp], rhs)

    uw = each(uw_of, hpairs)
    u = {(s, h): uw[s, h // 2][:, (h % 2) * 2 * HEAD_A:(h % 2) * 2 * HEAD_A + HEAD_A] for s, h in heads}
    w = {(s, h): uw[s, h // 2][:, (h % 2) * 2 * HEAD_A + HEAD_A:(h % 2 + 1) * 2 * HEAD_A] for s, h in heads}
    st = {(s, h): s_ref[s, 0, h] for s, h in heads}
    wqs = each(lambda s, h: _bdot(jnp.concatenate([w[s, h], q[s, h] * eg[s, h]], axis=0), st[s, h]), heads)
    v_new = {key: u[key] - wqs[key][0:C] for key in heads}

    def o_of(s, hp):
        a, b = (s, 2 * hp), (s, 2 * hp + 1)
        vn_bd = jnp.concatenate([jnp.concatenate([v_new[a], zc], axis=1),
                                 jnp.concatenate([zc, v_new[b]], axis=1)], axis=0)
        return _bdot(kq[s, hp][C:2 * C] * gamma[s, hp], vn_bd)

    op = each(o_of, hpairs)
    o = {(s, h): wqs[s, h][C:2 * C] + op[s, h // 2][:, (h % 2) * HEAD_A:(h % 2 + 1) * HEAD_A] for s, h in heads}
    for s, h in heads:
        kd = k[s, h] * jnp.exp(gtot[s, h] - gcol[s, h])
        s_ref[s, 0, h] = st[s, h] * jnp.exp(gtot[s, h]) + _bdot_tn(kd, v_new[s, h])
        tick()
    osq = each(lambda s, h: lane_sum(o[s, h] * o[s, h]), heads)
    for s, h in heads:
        z = z_ref[s, :, h * HEAD_A:(h + 1) * HEAD_A]
        oh = o[s, h] * lax.rsqrt(osq[s, h] * (1.0 / HEAD_A) + RMS_EPS)
        o_ref[s, :, h * HEAD_A:(h + 1) * HEAD_A] = (oh * on_ref[...] * _silu(z)).astype(BF16)
        tick()
    for _ in ahead:
        pass


def _delta_multi_mixer(proj, nb, seq, C, NS, prev, s0, cw, alog_row, dtb_row, onorm):
    nch = seq // C
    ng = nb // NS
    proj3 = proj.reshape(NS, ng * seq, P_CAT)
    nxt = lambda i, c: i * nch + jnp.minimum(c + 1, nch - 1)
    oa, sd = pl.pallas_call(
        functools.partial(_delta_multi_kernel, C=C, NS=NS),
        out_shape=(jax.ShapeDtypeStruct((NS, ng * seq, W_A), BF16),
                   jax.ShapeDtypeStruct((NS, ng, H_A, HEAD_A, HEAD_A), F32)),
        grid=(ng, nch),
        in_specs=[
            pl.BlockSpec((NS, C, 3 * W_A), lambda i, c: (0, i * nch, 0)),
            pl.BlockSpec((NS, C, LANE), lambda i, c: (0, i * nch, SM0 // LANE)),
            pl.BlockSpec((NS, C, 3 * W_A), lambda i, c: (0, nxt(i, c), 0)),
            pl.BlockSpec((NS, C, LANE), lambda i, c: (0, nxt(i, c), SM0 // LANE)),
            pl.BlockSpec((NS, C, W_A), lambda i, c: (0, i * nch + c, 3)),
            pl.BlockSpec((SUB, 3 * W_A), lambda i, c: (0, 0)),
            pl.BlockSpec((None, H_A, HEAD_A, HEAD_A), lambda i, c: (0, 0, 0, 0)),
            pl.BlockSpec((4, 3 * W_A), lambda i, c: (0, 0)),
            pl.BlockSpec((1, LANE), lambda i, c: (0, 0)),
            pl.BlockSpec((1, LANE), lambda i, c: (0, 0)),
            pl.BlockSpec((1, HEAD_A), lambda i, c: (0, 0)),
        ],
        out_specs=(pl.BlockSpec((NS, C, W_A), lambda i, c: (0, i * nch + c, 0)),
                   pl.BlockSpec((NS, 1, H_A, HEAD_A, HEAD_A), lambda i, c: (0, i, 0, 0, 0))),
        scratch_shapes=[pltpu.VMEM((NS, SUB, 3 * W_A), F32), pltpu.VMEM((2, NS, 3, C, W_A), F32),
                        pltpu.VMEM((2, NS, 2, C, LANE), F32), pltpu.VMEM((2, NS, SUB, LANE), F32)],
        compiler_params=pltpu.CompilerParams(
            dimension_semantics=("parallel", "arbitrary"), vmem_limit_bytes=VMEM_LIMIT),
        name="delta_multi_mixer",
    )(proj3, proj3, proj3, proj3, proj3, prev, s0, cw, alog_row, dtb_row, onorm)
    return oa.reshape(nb * seq, W_A), sd.reshape(nb, H_A, HEAD_A, HEAD_A)


def _rwkv_kernel(r_ref, k_ref, v_ref, sm_ref, pr_ref, pk_ref, pv_ref, psm_ref, s0_ref,
                 mur_ref, muk_ref, muv_ref, musm_ref, w2_ref, a2_ref, g2_ref,
                 w0_ref, a0_ref, kk_ref, ka_ref, rk_ref, lnw_ref, lnb_ref,
                 o_ref, s_ref, *, C, NB, G):
    R = NB * C
    seqs = range(NB)
    pairs = range(G)

    r2 = _iota((LANE, LANE), 0)
    c2 = _iota((LANE, LANE), 1)
    ones_bd = ((r2 < HEAD_B) == (c2 < HEAD_B)).astype(F32)
    hr_ref, hk_ref, hv_ref, hsm_ref = pr_ref, pk_ref, pv_ref, psm_ref
    tri, ones_seq = _seq_masks(R, C)

    def lerp(x, hist_ref, mu):
        hist = hist_ref[...] if NB == 1 else [_repeat_rows(hist_ref[0], SUB)]
        return x + (_delayed(x, hist, 1, NB) - x) * mu

    sm = sm_ref[...]
    xs = lerp(sm, hsm_ref, musm_ref[...])
    slab_wa = xs[:, SM_WA:SM_WA + LANE]
    slab_g = xs[:, SM_G:SM_G + 2 * LANE]

    def col(x, p):
        return x[:, p * LANE:(p + 1) * LANE]

    def to_rows(x):
        return jnp.concatenate([col(x, p) for p in pairs], axis=0)

    def to_cols(y):
        return jnp.concatenate([y[p * R:(p + 1) * R] for p in pairs], axis=1)

    def seq_rows(x, n):
        return x[n * C:(n + 1) * C]

    xr = lerp(r_ref[...], hr_ref, mur_ref[...])
    xk = lerp(k_ref[...], hk_ref, muk_ref[...])
    xv = lerp(v_ref[...], hv_ref, muv_ref[...])
    wlog = -_softplus(-(w0_ref[...] + _bdot(jnp.tanh(slab_wa), w2_ref[...]))) - 0.5
    ld = -jnp.exp(wlog)
    aa = _sigmoid(a0_ref[...] + _bdot(slab_wa, a2_ref[...]))
    gate = _bdot(_sigmoid(slab_g), g2_ref[...])
    kkr = xk * kk_ref[...]
    kkn = kkr * lax.rsqrt(to_cols(_bdot(to_rows(kkr * kkr), ones_bd)) + 1e-6)
    k2 = xk * (1.0 + (aa - 1.0) * ka_ref[...])
    lp = _xdot_r(tri, ld, 3)
    lp_tot = lp[C - 1:C, :] if NB == 1 else _xdot_r(ones_seq, ld, 3)
    e_neg = jnp.exp(-lp)
    e_rem = jnp.exp(lp_tot - lp)
    at = -kkn * jnp.exp(lp - ld)
    rt = xr * jnp.exp(lp)
    kb = kkn * aa
    bt = kb * e_neg
    kt = k2 * e_neg
    bhat = kb * e_rem
    khat = k2 * e_rem
    p_tot = jnp.exp(lp_tot)

    s = [[s0_ref[n, p] for p in pairs] for n in seqs]
    atp = [col(at, p) for p in pairs]
    rtp = [col(rt, p) for p in pairs]
    first_head = _iota((2 * C, LANE), 1) < HEAD_B
    top_rows = _iota((LANE, LANE), 0) < HEAD_B

    def other_half(x):
        return pltpu.roll(x, HEAD_B, 1)

    def times_state_t(a, sp):
        both = _bdot_nt(jnp.concatenate([a[:, :HEAD_B], other_half(a)[:, :HEAD_B]], axis=0), sp)
        m = a.shape[0]
        return jnp.where(first_head, both[0:m], both[m:2 * m])

    ars = [[times_state_t(jnp.concatenate([seq_rows(atp[p], n), seq_rows(rtp[p], n)], axis=0), s[n][p])
            for p in pairs] for n in seqs]
    if NB == 1:
        x_state = [ars[0][p][0:C] for p in pairs]
        o_state = [ars[0][p][C:2 * C] for p in pairs]
    else:
        x_state = [jnp.concatenate([ars[n][p][0:C] for n in seqs], axis=0) for p in pairs]
        o_state = [jnp.concatenate([ars[n][p][C:2 * C] for n in seqs], axis=0) for p in pairs]
    pk = _Packed(R, C)
    stack_mask = (_iota((2 * R, LANE), 0) < R) == (_iota((2 * R, LANE), 1) < HEAD_B)

    def by_head(x):
        return jnp.where(stack_mask, jnp.concatenate([x, x], axis=0), 0.0)

    ar = [jnp.concatenate([atp[p], rtp[p]], axis=0) for p in pairs]
    if 2 * R == LANE:
        abk = [_bdot_nt(ar[p], jnp.concatenate([by_head(col(bt, p)), by_head(col(kt, p))], axis=0)) for p in pairs]
        ab = [m[:, 0:2 * R] for m in abk]
        ak = [m[:, 2 * R:4 * R] for m in abk]
    else:
        ab = [_bdot_nt(ar[p], by_head(col(bt, p))) for p in pairs]
        ak = [_bdot_nt(ar[p], by_head(col(kt, p))) for p in pairs]
    tinv = pk.inverse_many([jnp.where(pk.strict, m[0:R], 0.0) for m in ab])
    v_bh = [by_head(col(xv, p)) for p in pairs]
    akv = [_bdot(jnp.where(pk.strict_over_incl, ak[p], 0.0), v_bh[p]) for p in pairs]
    y = [x_state[p] + akv[p][0:R] for p in pairs]
    u = [_bdot(tinv[p], by_head(y[p])) for p in pairs]
    o = [o_state[p] + akv[p][R:2 * R] + _bdot(jnp.where(pk.incl, ab[p][R:2 * R], 0.0), by_head(u[p]))
         for p in pairs]
    for n in seqs:
        for p in pairs:
            uv = jnp.concatenate([seq_rows(u[p], n), seq_rows(col(xv, p), n)], axis=0)
            bkhat = jnp.concatenate([seq_rows(col(bhat, p), n), seq_rows(col(khat, p), n)], axis=0)
            decay = col(p_tot, p) if NB == 1 else col(p_tot, p)[n * C:n * C + 1]
            full = _bdot_tn(uv, bkhat)
            upd = jnp.where(top_rows, full, other_half(full))[:, :HEAD_B]
            dec = jnp.where(top_rows[:, :HEAD_B], decay[:, :HEAD_B], other_half(decay)[:, :HEAD_B])
            s_ref[n, p] = s[n][p] * dec + upd

    o_rows = jnp.concatenate(o, axis=0)
    mean = _bdot(o_rows, ones_bd) * (1.0 / HEAD_B)
    d = o_rows - mean
    var = _bdot(d * d, ones_bd) * (1.0 / HEAD_B)
    on = to_cols(d * lax.rsqrt(var + GN_EPS)) * lnw_ref[...] + lnb_ref[...]
    bonus = to_cols(_bdot(to_rows(xr * k2 * rk_ref[...]), ones_bd)) * xv
    o_ref[...] = ((on + bonus) * gate).astype(BF16)


def _rwkv_mixer(proj, row0, nb, seq, C, NB, G, prev_rkv, prev_sm, s0, mu_rkv, mu_sm, w2p, a2p, g2p,
                w0, a0, k_k, k_a, r_k, lnw, lnb):
    nch = seq // C
    assert nch == 1 and (NB == 1 or C == SUB)
    R = NB * C
    ng = N_PAIR // G
    gw = G * LANE
    blk0 = row0 // R
    bcast = s0.shape[0] == 1
    bsel = (lambda b: 0) if bcast else (lambda b: b)

    def proj_spec(col0):
        return pl.BlockSpec((R, gw), lambda b, g, c: (blk0 + b * nch + c, col0 // gw + g))

    def prev_spec(part):
        if NB == 1:
            return pl.BlockSpec((SUB, gw), lambda b, g, c: (0, part * (W_B // gw) + g))
        return pl.BlockSpec((1, NB, gw), lambda b, g, c: (0, b, part * (W_B // gw) + g))

    def vec_spec(part=0):
        return pl.BlockSpec((1, gw), lambda b, g, c: (0, part * (W_B // gw) + g))

    in_specs = [
        proj_spec(RKV0), proj_spec(RKV0 + W_B), proj_spec(RKV0 + 2 * W_B),
        pl.BlockSpec((R, SM_W), lambda b, g, c: (blk0 + b * nch + c, SM0 // SM_W)),
        prev_spec(0), prev_spec(1), prev_spec(2),
        (pl.BlockSpec((SUB, SM_W), lambda b, g, c: (0, 0)) if NB == 1
         else pl.BlockSpec((1, NB, SM_W), lambda b, g, c: (0, b, 0))),
        pl.BlockSpec((NB, G, LANE, HEAD_B), lambda b, g, c: (bsel(b), g, 0, 0)),
        vec_spec(0), vec_spec(1), vec_spec(2),
        pl.BlockSpec((1, SM_W), lambda b, g, c: (0, 0)),
        pl.BlockSpec((LANE, gw), lambda b, g, c: (0, g)),
        pl.BlockSpec((LANE, gw), lambda b, g, c: (0, g)),
        pl.BlockSpec((2 * LANE, gw), lambda b, g, c: (0, g)),
        vec_spec(), vec_spec(), vec_spec(), vec_spec(), vec_spec(), vec_spec(), vec_spec(),
    ]
    return pl.pallas_call(
        functools.partial(_rwkv_kernel, C=C, NB=NB, G=G),
        out_shape=(jax.ShapeDtypeStruct((nb * seq, W_B), BF16),
                   jax.ShapeDtypeStruct((nb, N_PAIR, LANE, HEAD_B), F32)),
        grid=(nb // NB, ng, nch),
        in_specs=in_specs,
        out_specs=(pl.BlockSpec((R, gw), lambda b, g, c: (b * nch + c, g)),
                   pl.BlockSpec((NB, G, LANE, HEAD_B), lambda b, g, c: (b, g, 0, 0))),
        compiler_params=pltpu.CompilerParams(
            dimension_semantics=("parallel", "parallel", "arbitrary"), vmem_limit_bytes=VMEM_LIMIT),
        name="rwkv_mixer",
    )(proj, proj, proj, proj, prev_rkv, prev_rkv, prev_rkv, prev_sm, s0,
      mu_rkv, mu_rkv, mu_rkv, mu_sm, w2p, a2p, g2p, w0, a0, k_k, k_a, r_k, lnw, lnb)


_AT, _RT, _BT, _KT, _BHAT, _KHAT, _XV, _GATE, _BONUS, _N_PRE = range(10)


def _rwkv_multi_kernel(r0_ref, k0_ref, v0_ref, sm0_ref, rn_ref, kn_ref, vn_ref, smn_ref,
                       pr_ref, pk_ref, pv_ref, psm_ref, s0_ref,
                       mur_ref, muk_ref, muv_ref, musm_ref, w2_ref, a2_ref, g2_ref,
                       w0_ref, a0_ref, kk_ref, ka_ref, rk_ref, lnw_ref, lnb_ref,
                       o_ref, s_ref, cr_ref, ck_ref, cv_ref, csm_ref, st_ref, pre_ref, ptot_ref, *, C, G, NS):
    ci = pl.program_id(1)
    nch = pl.num_programs(1)
    pairs = range(G)
    chains = [(s, p) for s in range(NS) for p in pairs]

    r2 = _iota((LANE, LANE), 0)
    c2 = _iota((LANE, LANE), 1)
    same_head = (r2 < HEAD_B) == (c2 < HEAD_B)
    ones_bd = same_head.astype(F32)
    spread = (_iota((HEAD_B, LANE), 0) == (_iota((HEAD_B, LANE), 1) & (HEAD_B - 1))).astype(F32)
    gather = ((_iota((LANE, HEAD_B), 0) & (HEAD_B - 1)) == _iota((LANE, HEAD_B), 1)).astype(F32)
    tri, _ = _seq_masks(C, C)

    def col(x, p):
        return x[:, p * LANE:(p + 1) * LANE]

    def to_rows(x):
        return jnp.concatenate([col(x, p) for p in pairs], axis=0)

    def to_cols(y):
        return jnp.concatenate([y[p * C:(p + 1) * C] for p in pairs], axis=1)

    def lerp(x, hist, mu):
        return x + (_shifted(x, hist, 1) - x) * mu

    def preamble(r_ref, k_ref, v_ref, sm_ref, hist_of, slot):
        for s in range(NS):
            hr, hk, hv, hsm = hist_of(s)
            xs = lerp(sm_ref[s], hsm, musm_ref[...])
            yield
            slab_wa = xs[:, SM_WA:SM_WA + LANE]
            wl = _bdot(jnp.tanh(slab_wa), w2_ref[...])
            yield
            al = _bdot(slab_wa, a2_ref[...])
            yield
            pre_ref[slot, s, _GATE] = _bdot(_sigmoid(xs[:, SM_G:SM_G + 2 * LANE]), g2_ref[...])
            yield
            xr = lerp(r_ref[s], hr, mur_ref[...])
            yield
            xk = lerp(k_ref[s], hk, muk_ref[...])
            yield
            xv = lerp(v_ref[s], hv, muv_ref[...])
            pre_ref[slot, s, _XV] = xv
            yield
            ld = -jnp.exp(-_softplus(-(w0_ref[...] + wl)) - 0.5)
            yield
            lp = _xdot_r(tri, ld, 3)
            yield
            aa = _sigmoid(a0_ref[...] + al)
            yield
            kkr = xk * kk_ref[...]
            ss = to_cols(_bdot(to_rows(kkr * kkr), ones_bd))
            yield
            kkn = kkr * lax.rsqrt(ss + 1e-6)
            k2 = xk * (1.0 + (aa - 1.0) * ka_ref[...])
            yield
            pre_ref[slot, s, _BONUS] = to_cols(_bdot(to_rows(xr * k2 * rk_ref[...]), ones_bd)) * xv
            yield
            pre_ref[slot, s, _AT] = -kkn * jnp.exp(lp - ld)
            yield
            pre_ref[slot, s, _RT] = xr * jnp.exp(lp)
            yield
            kb = kkn * aa
            e_neg = jnp.exp(-lp)
            pre_ref[slot, s, _BT] = kb * e_neg
            yield
            pre_ref[slot, s, _KT] = k2 * e_neg
            yield
            lp_tot = lp[C - 1:C, :]
            e_rem = jnp.exp(lp_tot - lp)
            pre_ref[slot, s, _BHAT] = kb * e_rem
            yield
            pre_ref[slot, s, _KHAT] = k2 * e_rem
            ptot_ref[slot, s] = jnp.broadcast_to(jnp.exp(lp_tot), (SUB, lp.shape[1]))
            yield
            cr_ref[s] = r_ref[s, C - SUB:C, :]
            ck_ref[s] = k_ref[s, C - SUB:C, :]
            cv_ref[s] = v_ref[s, C - SUB:C, :]
            csm_ref[s] = sm_ref[s, C - SUB:C, :]

    @pl.when(ci == 0)
    def _():
        first_hist = lambda s: (pr_ref[...], pk_ref[...], pv_ref[...], psm_ref[...])
        for _ in preamble(r0_ref, k0_ref, v0_ref, sm0_ref, first_hist, 0):
            pass
        for s, p in chains:
            st_ref[s * G + p] = jnp.where(same_head, _xdot(s0_ref[p], spread, 3), 0.0)

    slot = ci % 2
    carried = lambda s: (cr_ref[s], ck_ref[s], cv_ref[s], csm_ref[s])
    ahead = preamble(rn_ref, kn_ref, vn_ref, smn_ref, carried, 1 - slot)
    calls = [0]

    def tick():
        calls[0] += 1
        if calls[0] % 6 == 0:
            next(ahead, None)

    def each(fn, items):
        out = {}
        for it in items:
            out[it] = fn(*it)
            tick()
        return out

    def pre(idx, s, p):
        return pre_ref[slot, s, idx, :, p * LANE:(p + 1) * LANE]

    pk = _Packed(C, C)
    stack_mask = (_iota((2 * C, LANE), 0) < C) == (_iota((2 * C, LANE), 1) < HEAD_B)

    def by_head(x):
        return jnp.where(stack_mask, jnp.concatenate([x, x], axis=0), 0.0)

    st = {(s, p): st_ref[s * G + p] for s, p in chains}
    ar = {(s, p): jnp.concatenate([pre(_AT, s, p), pre(_RT, s, p)], axis=0) for s, p in chains}
    ars = each(lambda s, p: _bdot_nt(ar[s, p], st[s, p]), chains)
    abk = each(lambda s, p: _bdot_nt(ar[s, p], jnp.concatenate([by_head(pre(_BT, s, p)), by_head(pre(_KT, s, p))],
                                                               axis=0)), chains)
    ab = {key: abk[key][:, 0:2 * C] for key in chains}
    ak = {key: abk[key][:, 2 * C:4 * C] for key in chains}
    ps = {key: jnp.where(pk.strict, ab[key][0:C], 0.0) for key in chains}
    ts = {key: pk.eye + ps[key] for key in chains}
    n = 1
    while 2 * n < C:
        ps = each(lambda s, p: _bdot(ps[s, p], pk.block_diag(ps[s, p])), chains)
        ts = each(lambda s, p: ts[s, p] + _bdot(ps[s, p], pk.block_diag(ts[s, p])), chains)
        n *= 2
    xv = {(s, p): pre(_XV, s, p) for s, p in chains}
    v_bh = {key: by_head(xv[key]) for key in chains}
    akv = each(lambda s, p: _bdot(jnp.where(pk.strict_over_incl, ak[s, p], 0.0), v_bh[s, p]), chains)
    y = {key: ars[key][0:C] + akv[key][0:C] for key in chains}
    u = each(lambda s, p: _bdot(ts[s, p], by_head(y[s, p])), chains)
    o = each(lambda s, p: ars[s, p][C:2 * C] + akv[s, p][C:2 * C]
             + _bdot(jnp.where(pk.incl, ab[s, p][C:2 * C], 0.0), by_head(u[s, p])), chains)
    for s, p in chains:
        uv = jnp.concatenate([u[s, p], xv[s, p]], axis=0)
        bkhat = jnp.concatenate([pre(_BHAT, s, p), pre(_KHAT, s, p)], axis=0)
        s_new = st[s, p] * ptot_ref[slot, s, 0:1, p * LANE:(p + 1) * LANE] + _bdot_tn(uv, bkhat)
        st_ref[s * G + p] = jnp.where(same_head, s_new, 0.0)
        tick()

    o_rows = jnp.concatenate([o[key] for key in chains], axis=0)
    mean = _bdot(o_rows, ones_bd) * (1.0 / HEAD_B)
    d = o_rows - mean
    var = _bdot(d * d, ones_bd) * (1.0 / HEAD_B)
    dn = d * lax.rsqrt(var + GN_EPS)
    for s in range(NS):
        on = to_cols(dn[s * G * C:(s + 1) * G * C]) * lnw_ref[...] + lnb_ref[...]
        o_ref[s] = ((on + pre_ref[slot, s, _BONUS]) * pre_ref[slot, s, _GATE]).astype(BF16)
    for _ in ahead:
        pass

    @pl.when(ci == nch - 1)
    def _():
        for s, p in chains:
            s_ref[s, 0, p] = _xdot(st_ref[s * G + p], gather, 3)


def _rwkv_multi_mixer(proj, nb, seq, C, NS, prev_rkv, prev_sm, s0, mu_rkv, mu_sm, w2p, a2p, g2p,
                      w0, a0, k_k, k_a, r_k, lnw, lnb):
    G = N_PAIR
    nch = seq // C
    ng = nb // NS
    gw = G * LANE
    proj3 = proj.reshape(NS, ng * seq, P_CAT)

    def first_spec(col0, w):
        return pl.BlockSpec((NS, C, w), lambda i, c: (0, i * nch, col0 // w))

    def next_spec(col0, w):
        return pl.BlockSpec((NS, C, w), lambda i, c: (0, i * nch + jnp.minimum(c + 1, nch - 1), col0 // w))

    def prev_spec(part):
        return pl.BlockSpec((SUB, gw), lambda i, c: (0, part))

    def vec_spec(part=0):
        return pl.BlockSpec((1, gw), lambda i, c: (0, part))

    in_specs = [
        first_spec(RKV0, gw), first_spec(RKV0 + W_B, gw), first_spec(RKV0 + 2 * W_B, gw), first_spec(SM0, SM_W),
        next_spec(RKV0, gw), next_spec(RKV0 + W_B, gw), next_spec(RKV0 + 2 * W_B, gw), next_spec(SM0, SM_W),
        prev_spec(0), prev_spec(1), prev_spec(2),
        pl.BlockSpec((SUB, SM_W), lambda i, c: (0, 0)),
        pl.BlockSpec((None, G, LANE, HEAD_B), lambda i, c: (0, 0, 0, 0)),
        vec_spec(0), vec_spec(1), vec_spec(2),
        pl.BlockSpec((1, SM_W), lambda i, c: (0, 0)),
        pl.BlockSpec((LANE, gw), lambda i, c: (0, 0)),
        pl.BlockSpec((LANE, gw), lambda i, c: (0, 0)),
        pl.BlockSpec((2 * LANE, gw), lambda i, c: (0, 0)),
        vec_spec(), vec_spec(), vec_spec(), vec_spec(), vec_spec(), vec_spec(), vec_spec(),
    ]
    ob, sw = pl.pallas_call(
        functools.partial(_rwkv_multi_kernel, C=C, G=G, NS=NS),
        out_shape=(jax.ShapeDtypeStruct((NS, ng * seq, W_B), BF16),
                   jax.ShapeDtypeStruct((NS, ng, N_PAIR, LANE, HEAD_B), F32)),
        grid=(ng, nch),
        in_specs=in_specs,
        out_specs=(pl.BlockSpec((NS, C, gw), lambda i, c: (0, i * nch + c, 0)),
                   pl.BlockSpec((NS, 1, G, LANE, HEAD_B), lambda i, c: (0, i, 0, 0, 0))),
        scratch_shapes=[pltpu.VMEM((NS, SUB, gw), F32), pltpu.VMEM((NS, SUB, gw), F32), pltpu.VMEM((NS, SUB, gw), F32),
                        pltpu.VMEM((NS, SUB, SM_W), F32), pltpu.VMEM((NS * G, LANE, LANE), F32),
                        pltpu.VMEM((2, NS, _N_PRE, C, gw), F32), pltpu.VMEM((2, NS, SUB, gw), F32)],
        compiler_params=pltpu.CompilerParams(
            dimension_semantics=("parallel", "arbitrary"), vmem_limit_bytes=VMEM_LIMIT),
        name="rwkv_multi_mixer",
    )(proj3, proj3, proj3, proj3, proj3, proj3, proj3, proj3, prev_rkv, prev_rkv, prev_rkv, prev_sm, s0,
      mu_rkv, mu_rkv, mu_rkv, mu_sm, w2p, a2p, g2p, w0, a0, k_k, k_a, r_k, lnw, lnb)
    return ob.reshape(nb * seq, W_B), sw.reshape(nb, N_PAIR, LANE, HEAD_B)


def _small_layout(cols_ba, cols_w, cols_a, cols_g, axis=-1):
    def z(n):
        shape = list(cols_w.shape)
        shape[axis] = n
        return jnp.zeros(shape, cols_w.dtype)
    return jnp.concatenate(
        [cols_ba, z(SM_WA - cols_ba.shape[axis]), cols_w, cols_a, cols_g,
         z(SM_W - SM_G - cols_g.shape[axis])], axis=axis)


def kernel(x_prompt, x_sample, state_delta, state_conv_qkv, state_wkv, state_shift, state_ffn_conv, meta, norm1, w_in, conv_a, a_log, dt_bias, onorm_a, mu_b, w0, w2, a0, a2, g2, k_k, k_a, r_k, lnx_w, lnx_b, w_o, norm2, w_ffn_in, conv_f, w_ffn_out, norm_f):
    nbp, seq_p, _ = x_prompt.shape
    nbs, seq_s, _ = x_sample.shape
    n_s = nbs * seq_s
    assert w_in.shape[0] == 1, "single-layer trunk"
    assert seq_s == SUB
    l = 0

    wt = w_in[l].T
    o_b = A_PROJ
    o_l = A_PROJ + 3 * W_B
    w_cat_t = jnp.concatenate([
        wt[:4 * W_A], wt[o_b:o_l],
        _small_layout(wt[4 * W_A:A_PROJ], wt[o_l:o_l + W_LORA], wt[o_l + W_LORA:o_l + W_LORA + A_LORA],
                      wt[o_l + W_LORA + A_LORA:], axis=0)], axis=0).astype(BF16)
    mu = mu_b[l]
    mu_rkv = mu[None, :3 * W_B]
    mu_sm = _small_layout(jnp.zeros((1, 2 * H_A), F32), mu[None, 3 * W_B:3 * W_B + W_LORA],
                          mu[None, 3 * W_B + W_LORA:3 * W_B + W_LORA + A_LORA],
                          mu[None, 3 * W_B + W_LORA + A_LORA:])
    w2p = jnp.concatenate([w2[l], jnp.zeros((LANE - W_LORA, W_B), F32)], axis=0)
    a2p = jnp.concatenate([jnp.zeros((W_LORA, W_B), F32), a2[l]], axis=0)
    g2p = jnp.concatenate([g2[l], jnp.zeros((2 * LANE - G_LORA, W_B), F32)], axis=0)
    alog_row = jnp.concatenate([jnp.zeros((H_A,), F32), a_log[l], jnp.zeros((LANE - 2 * H_A,), F32)])[None]
    dtb_row = jnp.concatenate([jnp.zeros((H_A,), F32), dt_bias[l], jnp.zeros((LANE - 2 * H_A,), F32)])[None]
    wo_bf = w_o[l].astype(BF16)
    wfo_bf = w_ffn_out[l].astype(BF16)
    row = lambda v: v.reshape(1, -1)

    def mix(proj, row0, nb, seq, C, NB, prev_qkv, prev_rkv, prev_sm, s_delta, s_wkv):
        delta_params = (conv_a[l], alog_row, dtb_row, row(onorm_a[l]))
        rwkv_params = (mu_rkv, mu_sm, w2p, a2p, g2p, row(w0[l]), row(a0[l]), row(k_k[l]), row(k_a[l]),
                       row(r_k[l]), row(lnx_w[l]), row(lnx_b[l]))
        if seq // C > 1:
            assert row0 == 0 and NB == 1
            oa, sd = _delta_multi_mixer(proj, nb, seq, C, 4, prev_qkv, s_delta, *delta_params)
            ob, sw = _rwkv_multi_mixer(proj, nb, seq, C, 2, prev_rkv, prev_sm, s_wkv, *rwkv_params)
        else:
            oa, sd = _delta_mixer(proj, row0, nb, seq, C, NB, prev_qkv, s_delta, *delta_params)
            ob, sw = _rwkv_mixer(proj, row0, nb, seq, C, NB, N_PAIR, prev_rkv, prev_sm, s_wkv, *rwkv_params)
        return oa, ob, sd, sw

    xs_rows = jnp.concatenate([x_sample.reshape(n_s, D_MODEL), meta], axis=0)
    xp_rows = x_prompt.reshape(nbp * seq_p, D_MODEL)
    n_small = n_s + N_META
    proj_s, proj_tail_s = _norm_matmul(xs_rows, row(norm1[l]), w_cat_t, n_small, 1536, 3, nbs)
    proj_p = _norm_matmul(xp_rows, row(norm1[l]), w_cat_t, 1024, 1536)

    zeros = lambda *s: jnp.zeros(s, F32)
    oa_m, ob_m, sd_m, sw_m = mix(proj_s, n_s, 1, N_META, N_META, 1, zeros(SUB, 3 * W_A),
                                 zeros(SUB, 3 * W_B), zeros(SUB, SM_W),
                                 zeros(1, H_A, HEAD_A, HEAD_A), zeros(1, N_PAIR, LANE, HEAD_B))
    tail = proj_s[n_small - SUB:n_small]
    oa_p, ob_p, sd_p, sw_p = mix(proj_p, 0, nbp, seq_p, 64, 1, tail[:, :3 * W_A],
                                 tail[:, RKV0:SM0], tail[:, SM0:], sd_m, sw_m)
    sh = state_shift[l]
    sh_sm = _small_layout(jnp.zeros((nbs, 1, 2 * H_A), F32), sh[..., 3 * W_B:3 * W_B + W_LORA],
                          sh[..., 3 * W_B + W_LORA:3 * W_B + W_LORA + A_LORA],
                          sh[..., 3 * W_B + W_LORA + A_LORA:])
    tmajor = lambda s: s.transpose(1, 0, 2)
    oa_s, ob_s, sd_s, sw_s = mix(proj_s, 0, nbs, seq_s, seq_s, 8, tmajor(state_conv_qkv[l]),
                                 tmajor(sh[..., :3 * W_B]), tmajor(sh_sm), state_delta[l],
                                 state_wkv[l].reshape(nbs, N_PAIR, LANE, HEAD_B))

    oa_small = jnp.concatenate([oa_s, oa_m], axis=0)
    ob_small = jnp.concatenate([ob_s, ob_m], axis=0)
    x1_s, h2_s = _out_proj(xs_rows, oa_small, ob_small, wo_bf, row(norm2[l]), n_small // 5)
    x1_p, h2_p = _out_proj(xp_rows, oa_p, ob_p, wo_bf, row(norm2[l]), 512)
    act_s, gate_tail_s, gate_last_s = _ffn_in(h2_s, w_ffn_in[l], state_ffn_conv[l].transpose(1, 0, 2), conv_f[l],
                                              n_small, 768, 3, 0, n_s + SUB)
    act_p, tail_p = _ffn_in(h2_p, w_ffn_in[l], gate_last_s[None], conv_f[l], 1024, 768, 3, seq_p // 1024)
    y_s = _ffn_out(act_s, wfo_bf, x1_s, row(norm_f), n_s, 1024, 768)
    y_p = _ffn_out(act_p, wfo_bf, x1_p, row(norm_f), nbp * seq_p, 1024, 768)

    def states(conv_new, last, ffn_new, nb, sd, sw):
        shift_new = jnp.concatenate([last[..., RKV0:SM0], last[..., SM0 + SM_WA:SM0 + SM_WA + W_LORA + A_LORA],
                                     last[..., SM0 + SM_G:SM0 + SM_G + G_LORA]], axis=-1)
        return (sd[None], conv_new[None], sw.reshape(nb, H_B, HEAD_B, HEAD_B)[None], shift_new[None],
                ffn_new[None])

    p3 = proj_p.reshape(nbp, seq_p, P_CAT)
    tail_s = proj_tail_s.transpose(1, 0, 2)
    return ((y_p.reshape(nbp, seq_p, D_MODEL), y_s.reshape(nbs, seq_s, D_MODEL))
            + states(p3[:, seq_p - 3:, :3 * W_A], p3[:, seq_p - 1:, :], tail_p[:, SUB - 2:, :], nbp, sd_p, sw_p)
            + states(tail_s[:, :, :3 * W_A], tail_s[:, 2:, :], gate_tail_s.transpose(1, 0, 2), nbs, sd_s, sw_s))
```

```python
import functools

import jax
import jax.numpy as jnp
from jax import lax
from jax.experimental import pallas as pl
from jax.experimental.pallas import tpu as pltpu

F32 = jnp.float32
BF16 = jnp.bfloat16

D_MODEL = 2048
N_META = 16
W_A = 1024
HEAD_A = 128
H_A = 8
W_B = 1024
HEAD_B = 64
H_B = 16
N_PAIR = H_B // 2
W_LORA = 64
A_LORA = 64
G_LORA = 160
D_FF = 5376
RMS_EPS = 1e-6
GN_EPS = 64e-5
A_PROJ = 4 * W_A + 2 * H_A
B_PROJ = 3 * W_B + W_LORA + A_LORA + G_LORA

QKVZ0 = 0
RKV0 = 4 * W_A
SM0 = RKV0 + 3 * W_B
SM_W = 512
SM_BA = 0
SM_WA = 128
SM_G = 256
P_CAT = SM0 + SM_W

LANE = 128
SUB = 8
VMEM_LIMIT = 48 * 1024 * 1024
VMEM_LIMIT_BIG = 58 * 1024 * 1024
NEG_BIG = -1e30

NT_DIMS = (((1,), (1,)), ((), ()))


def _bdot(a, b):
    return jnp.dot(a.astype(BF16), b.astype(BF16), preferred_element_type=F32)


def _bdot_nt(a, b):
    return lax.dot_general(a.astype(BF16), b.astype(BF16), NT_DIMS, preferred_element_type=F32)


def _bdot_tn(a, b):
    return lax.dot_general(a.astype(BF16), b.astype(BF16), (((0,), (0,)), ((), ())),
                           preferred_element_type=F32)


def _pieces(a, n):
    out = []
    rem = a
    for i in range(n):
        p = rem.astype(BF16)
        out.append(p)
        if i + 1 < n:
            rem = rem - p.astype(F32)
    return out


def _xdot(a, b, n, dims=(((1,), (0,)), ((), ()))):
    bb = b.astype(BF16)
    acc = None
    for p in _pieces(a, n):
        t = lax.dot_general(p, bb, dims, preferred_element_type=F32)
        acc = t if acc is None else acc + t
    return acc


def _xdot_r(a, b, n, dims=(((1,), (0,)), ((), ()))):
    ab = a.astype(BF16)
    acc = None
    for p in _pieces(b, n):
        t = lax.dot_general(ab, p, dims, preferred_element_type=F32)
        acc = t if acc is None else acc + t
    return acc


def _sigmoid(x):
    return 1.0 / (1.0 + jnp.exp(-x))


def _silu(x):
    return x * _sigmoid(x)


def _softplus(x):
    return jnp.maximum(x, 0.0) + jnp.log(1.0 + jnp.exp(-jnp.abs(x)))


def _iota(shape, dim):
    return lax.broadcasted_iota(jnp.int32, shape, dim)


def _shifted(x, prev8, k):
    n = x.shape[0]
    xr = pltpu.roll(x, k, 0)
    pr = pltpu.roll(prev8, k, 0)
    first = jnp.where(_iota((SUB, x.shape[1]), 0) < k, pr, xr[0:SUB])
    if n == SUB:
        return first
    return jnp.concatenate([first, xr[SUB:]], axis=0)


def _repeat_rows(s, reps):
    n, c = s.shape
    return jnp.broadcast_to(s[:, None, :], (n, reps, c)).reshape(n * reps, c)


def _delayed(x, hist, k, nseq):
    if nseq == 1:
        return _shifted(x, hist, k)
    t = _iota(x.shape, 0) & (SUB - 1)
    d = pltpu.roll(x, k, 0)
    for tt in range(k):
        d = jnp.where(t == tt, hist[len(hist) - k + tt], d)
    return d


def _same_seq(shape, c, rows_total):
    if c == rows_total:
        return None
    sh = c.bit_length() - 1
    return (lax.shift_right_logical(_iota(shape, 0) & (rows_total - 1), sh)
            == lax.shift_right_logical(_iota(shape, 1) & (rows_total - 1), sh))


class _Packed:
    def __init__(self, r, c):
        self.r = r
        self.c = c
        row = _iota((r, 2 * r), 0)
        col = _iota((r, 2 * r), 1) & (r - 1)
        same = _same_seq((r, 2 * r), c, r)
        self.incl = (row >= col) if same is None else ((row >= col) & same)
        self.strict = (row > col) if same is None else ((row > col) & same)
        self.eye = (row == col).astype(F32)
        row2 = _iota((2 * r, 2 * r), 0)
        lower2 = ((row2 & (r - 1)) - (_iota((2 * r, 2 * r), 1) & (r - 1)) + jnp.where(row2 < r, 0, 1)) > 0
        same2 = _same_seq((2 * r, 2 * r), c, r)
        self.strict_over_incl = lower2 if same2 is None else (lower2 & same2)
        self.first = _iota((r, 2 * r), 1) < r
        self.bd_mask = (_iota((2 * r, 2 * r), 0) < r) == (_iota((2 * r, 2 * r), 1) < r)

    def block_diag(self, m):
        return jnp.where(self.bd_mask, jnp.concatenate([m, m], axis=0), 0.0)

    def inverse_many(self, xs, each=None):
        if each is None:
            each = lambda fn, keys: {k: fn(k) for k in keys}
        keys = list(xs)
        r = self.r
        a = dict(xs)
        s = {k: self.eye + xs[k] for k in keys}
        levels = self.c.bit_length() - 1
        if levels >= 2:
            a = each(lambda k: _bdot(a[k], self.block_diag(a[k])), keys)
        for j in range(1, levels):
            if j < levels - 1:
                both = each(lambda k: _bdot(jnp.concatenate([s[k], a[k]], axis=0), self.block_diag(a[k])), keys)
                s = {k: s[k] + both[k][0:r] for k in keys}
                a = {k: both[k][r:2 * r] for k in keys}
            else:
                sa = each(lambda k: _bdot(s[k], self.block_diag(a[k])), keys)
                s = {k: s[k] + sa[k] for k in keys}
        return s


def _seq_masks(r, c):
    row = _iota((r, r), 0)
    col = _iota((r, r), 1)
    same = _same_seq((r, r), c, r)
    if same is None:
        return (row >= col).astype(F32), None
    return ((row >= col) & same).astype(F32), same.astype(F32)


def _seq_tail_rows(x, slab_ref, dst_ref, n_tail, nseq, col0=0):
    for s in range(x.shape[1] // LANE):
        slab_ref[s] = x[:, s * LANE:(s + 1) * LANE]
        for t in range(n_tail):
            dst_ref[t, :, col0 + s * LANE:col0 + (s + 1) * LANE] = (
                slab_ref[s, pl.ds(SUB - n_tail + t, nseq, stride=SUB), :])


def _norm_matmul_kernel(x_ref, g_ref, w_ref, o_ref, *rest, n_tail, nseq):
    tail_ref, h_ref, slab_ref = rest if n_tail else (None,) + rest + (None,)

    @pl.when(pl.program_id(1) == 0)
    def _():
        x = x_ref[...]
        ms = jnp.mean(x * x, axis=-1, keepdims=True)
        h_ref[...] = (x * lax.rsqrt(ms + RMS_EPS) * g_ref[...]).astype(BF16)

    out = lax.dot_general(h_ref[...], w_ref[...], NT_DIMS, preferred_element_type=F32)
    o_ref[...] = out
    if n_tail:
        _seq_tail_rows(out, slab_ref, tail_ref, n_tail, nseq)


def _norm_matmul(x, g, wt, tm, tn, n_tail=0, nseq=0):
    m, k = x.shape
    n = wt.shape[0]
    out_shape = jax.ShapeDtypeStruct((m, n), F32)
    out_specs = pl.BlockSpec((tm, tn), lambda i, j: (i, j))
    if n_tail:
        assert m == tm
        out_shape = (out_shape, jax.ShapeDtypeStruct((n_tail, nseq, n), F32))
        out_specs = (out_specs, pl.BlockSpec((n_tail, nseq, tn), lambda i, j: (0, 0, j)))
    return pl.pallas_call(
        functools.partial(_norm_matmul_kernel, n_tail=n_tail, nseq=nseq),
        out_shape=out_shape,
        grid=(m // tm, n // tn),
        in_specs=[
            pl.BlockSpec((tm, k), lambda i, j: (i, 0)),
            pl.BlockSpec((1, k), lambda i, j: (0, 0)),
            pl.BlockSpec((tn, k), lambda i, j: (j, 0)),
        ],
        out_specs=out_specs,
        scratch_shapes=[pltpu.VMEM((tm, k), BF16)] + ([pltpu.VMEM((tn // LANE, tm, LANE), F32)] if n_tail else []),
        compiler_params=pltpu.CompilerParams(
            dimension_semantics=("parallel", "arbitrary"), vmem_limit_bytes=VMEM_LIMIT_BIG),
        name="norm_matmul",
    )(x, g, wt)


def _out_proj_kernel(x_ref, oa_ref, ob_ref, wt_ref, wb_ref, g_ref, x1_ref, h_ref):
    acc = jnp.dot(oa_ref[...], wt_ref[...], preferred_element_type=F32)
    acc = acc + jnp.dot(ob_ref[...], wb_ref[...], preferred_element_type=F32)
    x1 = x_ref[...] + acc
    x1_ref[...] = x1
    ms = jnp.mean(x1 * x1, axis=-1, keepdims=True)
    h_ref[...] = (x1 * lax.rsqrt(ms + RMS_EPS) * g_ref[...]).astype(BF16)


def _out_proj(x, oa, ob, wo, g, tm):
    m, d = x.shape
    return pl.pallas_call(
        _out_proj_kernel,
        out_shape=(jax.ShapeDtypeStruct((m, d), F32), jax.ShapeDtypeStruct((m, d), BF16)),
        grid=(m // tm,),
        in_specs=[
            pl.BlockSpec((tm, d), lambda i: (i, 0)),
            pl.BlockSpec((tm, W_A), lambda i: (i, 0)),
            pl.BlockSpec((tm, W_B), lambda i: (i, 0)),
            pl.BlockSpec((W_A, d), lambda i: (0, 0)),
            pl.BlockSpec((W_B, d), lambda i: (W_A // W_B, 0)),
            pl.BlockSpec((1, d), lambda i: (0, 0)),
        ],
        out_specs=(pl.BlockSpec((tm, d), lambda i: (i, 0)), pl.BlockSpec((tm, d), lambda i: (i, 0))),
        compiler_params=pltpu.CompilerParams(
            dimension_semantics=("parallel",), vmem_limit_bytes=VMEM_LIMIT),
        name="out_proj",
    )(x, oa, ob, wo, wo, g)


def _ffn_in_kernel(h_ref, wg_ref, wu_ref, prev_ref, cw_ref, act_ref, tail_ref, *rest,
                   tiles_per_seq, state_rows, nsub):
    if tiles_per_seq:
        wgb_ref, wub_ref, carry_ref = rest
    else:
        last_ref, wgb_ref, wub_ref, carry_ref, slab_ref = rest
    m = pl.program_id(1)

    @pl.when(m == 0)
    def _():
        wgb_ref[...] = wg_ref[...].astype(BF16)
        wub_ref[...] = wu_ref[...].astype(BF16)

    if tiles_per_seq:
        @pl.when(m % tiles_per_seq == 0)
        def _():
            carry_ref[...] = prev_ref[...]

    h = h_ref[...]
    n = h.shape[0]
    w = wgb_ref.shape[1] // nsub
    for i in range(nsub):
        cs = slice(i * w, (i + 1) * w)
        gate = jnp.dot(h, wgb_ref[:, cs], preferred_element_type=F32)
        up = jnp.dot(h, wub_ref[:, cs], preferred_element_type=F32)
        if tiles_per_seq:
            prev8 = carry_ref[:, cs]
            d1 = _shifted(gate, prev8, 1)
            d2 = _shifted(gate, prev8, 2)
            carry_ref[:, cs] = gate[n - SUB:n]
            tail_ref[:, cs] = gate[n - SUB:n]
        else:
            nseq = (state_rows - SUB) // SUB
            _seq_tail_rows(gate, slab_ref, tail_ref, 2, nseq, i * w)
            last_ref[:, cs] = gate[n - SUB:n]
            pad = jnp.zeros((n - nseq * SUB, w), F32)
            hist = [jnp.concatenate([_repeat_rows(prev_ref[t, :, cs], SUB), pad], axis=0) for t in range(2)]
            t_in_seq = jnp.where(_iota(gate.shape, 0) < state_rows, _iota(gate.shape, 0) & (SUB - 1), SUB)
            d1 = jnp.where(t_in_seq == 0, hist[1], pltpu.roll(gate, 1, 0))
            d2 = jnp.where(t_in_seq == 0, hist[0], jnp.where(t_in_seq == 1, hist[1], pltpu.roll(gate, 2, 0)))
        y = gate * cw_ref[2:3, cs] + d1 * cw_ref[1:2, cs] + d2 * cw_ref[0:1, cs]
        act_ref[:, cs] = (_silu(y) * up).astype(BF16)


def _ffn_in(h, w, prev, cw, tm, tn, nsub, tiles_per_seq, state_rows=0):
    m, d = h.shape
    nj = D_FF // tn
    nm = m // tm
    if tiles_per_seq:
        nseq = nm // tiles_per_seq
        prev_spec = pl.BlockSpec((None, SUB, tn), lambda j, i: (0, 0, j))
        extra_shapes = (jax.ShapeDtypeStruct((nseq, SUB, D_FF), F32),)
        extra_specs = (pl.BlockSpec((None, SUB, tn), lambda j, i: (i // tiles_per_seq, 0, j)),)
        extra_scratch = []
    else:
        assert nm == 1
        nseq = (state_rows - SUB) // SUB
        prev_spec = pl.BlockSpec((2, nseq, tn), lambda j, i: (0, 0, j))
        extra_shapes = (jax.ShapeDtypeStruct((2, nseq, D_FF), F32), jax.ShapeDtypeStruct((SUB, D_FF), F32))
        extra_specs = (pl.BlockSpec((2, nseq, tn), lambda j, i: (0, 0, j)),
                       pl.BlockSpec((SUB, tn), lambda j, i: (0, j)))
        extra_scratch = [pltpu.VMEM((tn // nsub // LANE, tm, LANE), F32)]
    return pl.pallas_call(
        functools.partial(_ffn_in_kernel, tiles_per_seq=tiles_per_seq, state_rows=state_rows, nsub=nsub),
        out_shape=(jax.ShapeDtypeStruct((m, D_FF), BF16),) + extra_shapes,
        grid=(nj, nm),
        in_specs=[
            pl.BlockSpec((tm, d), lambda j, i: (i, 0)),
            pl.BlockSpec((d, tn), lambda j, i: (0, j)),
            pl.BlockSpec((d, tn), lambda j, i: (0, nj + j)),
            prev_spec,
            pl.BlockSpec((3, tn), lambda j, i: (0, j)),
        ],
        out_specs=(pl.BlockSpec((tm, tn), lambda j, i: (i, j)),) + extra_specs,
        scratch_shapes=[pltpu.VMEM((d, tn), BF16), pltpu.VMEM((d, tn), BF16), pltpu.VMEM((SUB, tn), F32)]
        + extra_scratch,
        compiler_params=pltpu.CompilerParams(
            dimension_semantics=("parallel", "arbitrary"), vmem_limit_bytes=VMEM_LIMIT_BIG),
        name="ffn_in",
    )(h, w, w, prev, cw)


def _ffn_out_kernel(a_ref, w_ref, x_ref, g_ref, o_ref):
    kk = pl.program_id(1)

    @pl.when(kk == 0)
    def _():
        o_ref[...] = x_ref[...]

    o_ref[...] += jnp.dot(a_ref[...], w_ref[...], preferred_element_type=F32)

    @pl.when(kk == pl.num_programs(1) - 1)
    def _():
        x = o_ref[...]
        ms = jnp.mean(x * x, axis=-1, keepdims=True)
        o_ref[...] = x * lax.rsqrt(ms + RMS_EPS) * g_ref[...]


def _ffn_out(act, w, x1, g, m, tm, tk):
    kdim, d = w.shape
    return pl.pallas_call(
        _ffn_out_kernel,
        out_shape=jax.ShapeDtypeStruct((m, d), F32),
        grid=(m // tm, kdim // tk),
        in_specs=[
            pl.BlockSpec((tm, tk), lambda i, k: (i, k)),
            pl.BlockSpec((tk, d), lambda i, k: (k, 0)),
            pl.BlockSpec((tm, d), lambda i, k: (i, 0)),
            pl.BlockSpec((1, d), lambda i, k: (0, 0)),
        ],
        out_specs=pl.BlockSpec((tm, d), lambda i, k: (i, 0)),
        compiler_params=pltpu.CompilerParams(
            dimension_semantics=("parallel", "arbitrary"), vmem_limit_bytes=VMEM_LIMIT),
        name="ffn_out",
    )(act, w, x1, g)


def _delta_kernel(qkvz_ref, ba_ref, prev_ref, s0_ref, cw_ref, alog_ref, dtb_ref, on_ref,
                  o_ref, s_ref, carry_ref, *, C, NB):
    R = NB * C
    ci = pl.program_id(1)

    @pl.when(ci == 0)
    def _():
        if NB == 1:
            carry_ref[...] = prev_ref[...]
        s_ref[...] = s0_ref[...]

    tri, ones_seq = _seq_masks(R, C)

    ba = ba_ref[...]
    beta_full = _sigmoid(ba)
    g_full = -jnp.exp(alog_ref[...]) * _softplus(ba + dtb_ref[...])
    gc_full = _xdot_r(tri, g_full, 3)
    gtot_full = gc_full[C - 1:C, :] if NB == 1 else _xdot_r(ones_seq, g_full, 3)

    def conv_silu(c0):
        x = qkvz_ref[:, c0:c0 + LANE]
        if NB == 1:
            hist = carry_ref[:, c0:c0 + LANE]
        else:
            hist = [_repeat_rows(prev_ref[j, :, c0:c0 + LANE], SUB) for j in range(3)]
        y = x * cw_ref[3:4, c0:c0 + LANE]
        for k in (1, 2, 3):
            y = y + _delayed(x, hist, k, NB) * cw_ref[3 - k:4 - k, c0:c0 + LANE]
        return _silu(y)

    ones_ll = jnp.ones((LANE, LANE), BF16)

    def lane_sum(x):
        return jnp.dot(x.astype(BF16), ones_ll, preferred_element_type=F32)

    def l2n_many(xs):
        sums = [lane_sum(x * x) for x in xs]
        return [x * lax.rsqrt(sq + 1e-6) for x, sq in zip(xs, sums)]

    def seq_rows(x, n):
        return x[n * C:(n + 1) * C]

    heads = range(H_A)
    seqs = range(NB)
    q = [x * (HEAD_A ** -0.5) for x in l2n_many([conv_silu(h * HEAD_A) for h in heads])]
    k = l2n_many([conv_silu(W_A + h * HEAD_A) for h in heads])
    v = [conv_silu(2 * W_A + h * HEAD_A) for h in heads]
    bcol = [beta_full[:, h:h + 1] for h in heads]
    gcol = [gc_full[:, H_A + h:H_A + h + 1] for h in heads]
    gtot = [gtot_full[:, H_A + h:H_A + h + 1] for h in heads]
    eg = [jnp.exp(gcol[h]) for h in heads]
    kb = [k[h] * bcol[h] for h in heads]

    pk = _Packed(R, C)
    hpairs = range(H_A // 2)
    zc = jnp.zeros((R, HEAD_A), F32)
    row2 = _iota((2 * R, LANE), 0)
    lane2 = _iota((2 * R, LANE), 1)
    ones_cl = jnp.ones((R, LANE), F32)
    gc2 = jnp.concatenate([gc_full, gc_full], axis=0)
    kq, gamma = [], []
    for hp in hpairs:
        h0, h1 = 2 * hp, 2 * hp + 1
        lhs = jnp.concatenate([jnp.concatenate([kb[h0], kb[h1]], axis=1),
                               jnp.concatenate([q[h0], q[h1]], axis=1)], axis=0)
        rk = jnp.concatenate([jnp.concatenate([k[h0], zc], axis=1),
                              jnp.concatenate([zc, k[h1]], axis=1)], axis=0)
        kq.append(_bdot_nt(lhs, rk))
        sel = lane2 == jnp.where(row2 < R, H_A + h0, H_A + h1)
        grow = _xdot_r(ones_cl, jnp.where(sel, gc2, 0.0), 3, NT_DIMS)
        gcol_p = jnp.where(pk.first, gcol[h0], gcol[h1])
        gamma.append(jnp.exp(jnp.where(pk.incl, gcol_p - grow, NEG_BIG)))
    tinv = pk.inverse_many({hp: -jnp.where(pk.strict, kq[hp][0:R] * gamma[hp], 0.0) for hp in hpairs})
    uw = []
    for hp in hpairs:
        h0, h1 = 2 * hp, 2 * hp + 1
        z2 = jnp.zeros((R, 2 * HEAD_A), F32)
        rhs = jnp.concatenate([jnp.concatenate([v[h0] * bcol[h0], kb[h0] * eg[h0], z2], axis=1),
                               jnp.concatenate([z2, v[h1] * bcol[h1], kb[h1] * eg[h1]], axis=1)], axis=0)
        uw.append(_bdot(tinv[hp], rhs))
    u = [uw[h // 2][:, (h % 2) * 2 * HEAD_A:(h % 2) * 2 * HEAD_A + HEAD_A] for h in heads]
    w = [uw[h // 2][:, (h % 2) * 2 * HEAD_A + HEAD_A:(h % 2 + 1) * 2 * HEAD_A] for h in heads]
    qd = [q[h] * eg[h] for h in heads]
    s = [[s_ref[n, h] for h in heads] for n in seqs]
    wqs = [[_bdot(jnp.concatenate([seq_rows(w[h], n), seq_rows(qd[h], n)], axis=0), s[n][h]) for h in heads]
           for n in seqs]
    ws = [jnp.concatenate([wqs[n][h][0:C] for n in seqs], axis=0) if NB > 1 else wqs[0][h][0:C] for h in heads]
    qs = [jnp.concatenate([wqs[n][h][C:2 * C] for n in seqs], axis=0) if NB > 1 else wqs[0][h][C:2 * C]
          for h in heads]
    v_new = [u[h] - ws[h] for h in heads]
    o = []
    for hp in hpairs:
        h0, h1 = 2 * hp, 2 * hp + 1
        vn_bd = jnp.concatenate([jnp.concatenate([v_new[h0], zc], axis=1),
                                 jnp.concatenate([zc, v_new[h1]], axis=1)], axis=0)
        op = _bdot(kq[hp][R:2 * R] * gamma[hp], vn_bd)
        o.append(qs[h0] + op[:, :HEAD_A])
        o.append(qs[h1] + op[:, HEAD_A:])
    kd = [k[h] * jnp.exp(gtot[h] - gcol[h]) for h in heads]
    for n in seqs:
        for h in heads:
            glast = gtot[h] if NB == 1 else gtot[h][n * C:n * C + 1]
            s_ref[n, h] = s[n][h] * jnp.exp(glast) + _bdot_tn(seq_rows(kd[h], n), seq_rows(v_new[h], n))
    osq = [lane_sum(o[h] * o[h]) for h in heads]
    for h in heads:
        z = qkvz_ref[:, 3 * W_A + h * HEAD_A:3 * W_A + (h + 1) * HEAD_A]
        oh = o[h] * lax.rsqrt(osq[h] * (1.0 / HEAD_A) + RMS_EPS)
        o_ref[:, h * HEAD_A:(h + 1) * HEAD_A] = (oh * on_ref[...] * _silu(z)).astype(BF16)

    if NB == 1:
        carry_ref[...] = qkvz_ref[C - SUB:C, 0:3 * W_A]


def _delta_mixer(proj, row0, nb, seq, C, NB, prev, s0, cw, alog_row, dtb_row, onorm):
    nch = seq // C
    assert NB == 1 or (nch == 1 and C == SUB)
    R = NB * C
    blk0 = row0 // R
    bcast = s0.shape[0] == 1
    bsel = (lambda b: 0) if bcast else (lambda b: b)
    return pl.pallas_call(
        functools.partial(_delta_kernel, C=C, NB=NB),
        out_shape=(jax.ShapeDtypeStruct((nb * seq, W_A), BF16),
                   jax.ShapeDtypeStruct((nb, H_A, HEAD_A, HEAD_A), F32)),
        grid=(nb // NB, nch),
        in_specs=[
            pl.BlockSpec((R, 4 * W_A), lambda b, c: (blk0 + b * nch + c, 0)),
            pl.BlockSpec((R, LANE), lambda b, c: (blk0 + b * nch + c, SM0 // LANE)),
            (pl.BlockSpec((SUB, 3 * W_A), lambda b, c: (0, 0)) if NB == 1
             else pl.BlockSpec((3, NB, 3 * W_A), lambda b, c: (0, b, 0))),
            pl.BlockSpec((NB, H_A, HEAD_A, HEAD_A), lambda b, c: (bsel(b), 0, 0, 0)),
            pl.BlockSpec((4, 3 * W_A), lambda b, c: (0, 0)),
            pl.BlockSpec((1, LANE), lambda b, c: (0, 0)),
            pl.BlockSpec((1, LANE), lambda b, c: (0, 0)),
            pl.BlockSpec((1, HEAD_A), lambda b, c: (0, 0)),
        ],
        out_specs=(pl.BlockSpec((R, W_A), lambda b, c: (b * nch + c, 0)),
                   pl.BlockSpec((NB, H_A, HEAD_A, HEAD_A), lambda b, c: (b, 0, 0, 0))),
        scratch_shapes=[pltpu.VMEM((SUB, 3 * W_A), F32)],
        compiler_params=pltpu.CompilerParams(
            dimension_semantics=("parallel", "arbitrary"), vmem_limit_bytes=VMEM_LIMIT),
        name="delta_mixer",
    )(proj, proj, prev, s0, cw, alog_row, dtb_row, onorm)


def _delta_multi_kernel(qkv0_ref, ba0_ref, qkvn_ref, ban_ref, z_ref, prev_ref, s0_ref, cw_ref, alog_ref, dtb_ref,
                        on_ref, o_ref, s_ref, carry_ref, pre_ref, small_ref, gtot_ref, *, C, NS):
    ci = pl.program_id(1)
    heads = [(s, h) for s in range(NS) for h in range(H_A)]
    hpairs = [(s, hp) for s in range(NS) for hp in range(H_A // 2)]
    tri, _ = _seq_masks(C, C)
    ones_ll = jnp.ones((LANE, LANE), BF16)

    def lane_sum(x):
        return jnp.dot(x.astype(BF16), ones_ll, preferred_element_type=F32)

    def preamble(qkv_ref, ba_ref, hist_of, slot):
        for s in range(NS):
            ba = ba_ref[s]
            g_full = -jnp.exp(alog_ref[...]) * _softplus(ba + dtb_ref[...])
            gc_full = _xdot_r(tri, g_full, 3)
            small_ref[slot, s, 0] = _sigmoid(ba)
            small_ref[slot, s, 1] = gc_full
            gtot_ref[slot, s] = jnp.broadcast_to(gc_full[C - 1:C, :], (SUB, LANE))
            yield
        for part in range(3):
            xs = {}
            for s, h in heads:
                c0 = part * W_A + h * HEAD_A
                x = qkv_ref[s, :, c0:c0 + LANE]
                hist = hist_of(s, c0)
                y = x * cw_ref[3:4, c0:c0 + LANE]
                for k in (1, 2, 3):
                    y = y + _shifted(x, hist, k) * cw_ref[3 - k:4 - k, c0:c0 + LANE]
                xs[s, h] = _silu(y)
                yield
            if part < 2:
                sums = {key: lane_sum(x * x) for key, x in xs.items()}
                yield
                scale = HEAD_A ** -0.5 if part == 0 else 1.0
                xs = {key: x * (lax.rsqrt(sums[key] + 1e-6) * scale) for key, x in xs.items()}
            for s, h in heads:
                pre_ref[slot, s, part, :, h * HEAD_A:(h + 1) * HEAD_A] = xs[s, h]
            yield
        for s in range(NS):
            carry_ref[s] = qkv_ref[s, C - SUB:C, :]

    @pl.when(ci == 0)
    def _():
        for _ in preamble(qkv0_ref, ba0_ref, lambda s, c0: prev_ref[:, c0:c0 + LANE], 0):
            pass
        for s in range(NS):
            s_ref[s, 0] = s0_ref[...]

    slot = ci % 2
    ahead = preamble(qkvn_ref, ban_ref, lambda s, c0: carry_ref[s, :, c0:c0 + LANE], 1 - slot)
    calls = [0]

    def tick():
        calls[0] += 1
        if calls[0] % 3 == 0:
            next(ahead, None)

    def each(fn, items):
        out = {}
        for it in items:
            out[it] = fn(*it)
            tick()
        return out

    beta_full = {s: small_ref[slot, s, 0] for s in range(NS)}
    gc_full = {s: small_ref[slot, s, 1] for s in range(NS)}
    gtot_row = {s: gtot_ref[slot, s] for s in range(NS)}
    q = {(s, h): pre_ref[slot, s, 0, :, h * HEAD_A:(h + 1) * HEAD_A] for s, h in heads}
    k = {(s, h): pre_ref[slot, s, 1, :, h * HEAD_A:(h + 1) * HEAD_A] for s, h in heads}
    v = {(s, h): pre_ref[slot, s, 2, :, h * HEAD_A:(h + 1) * HEAD_A] for s, h in heads}
    bcol = {(s, h): beta_full[s][:, h:h + 1] for s, h in heads}
    gcol = {(s, h): gc_full[s][:, H_A + h:H_A + h + 1] for s, h in heads}
    gtot = {(s, h): gtot_row[s][0:1, H_A + h:H_A + h + 1] for s, h in heads}
    eg = {key: jnp.exp(gcol[key]) for key in heads}
    kb = {key: k[key] * bcol[key] for key in heads}

    pk = _Packed(C, C)
    zc = jnp.zeros((C, HEAD_A), F32)
    row2 = _iota((2 * C, LANE), 0)
    lane2 = _iota((2 * C, LANE), 1)
    ones_cl = jnp.ones((C, LANE), F32)

    def kq_of(s, hp):
        a, b = (s, 2 * hp), (s, 2 * hp + 1)
        lhs = jnp.concatenate([jnp.concatenate([kb[a], kb[b]], axis=1),
                               jnp.concatenate([q[a], q[b]], axis=1)], axis=0)
        rk = jnp.concatenate([jnp.concatenate([k[a], zc], axis=1),
                              jnp.concatenate([zc, k[b]], axis=1)], axis=0)
        return _bdot_nt(lhs, rk)

    def gamma_of(s, hp):
        a, b = (s, 2 * hp), (s, 2 * hp + 1)
        sel = lane2 == jnp.where(row2 < C, H_A + 2 * hp, H_A + 2 * hp + 1)
        gc2 = jnp.concatenate([gc_full[s], gc_full[s]], axis=0)
        grow = _xdot_r(ones_cl, jnp.where(sel, gc2, 0.0), 3, NT_DIMS)
        gcol_p = jnp.where(pk.first, gcol[a], gcol[b])
        return jnp.exp(jnp.where(pk.incl, gcol_p - grow, NEG_BIG))

    kq = each(kq_of, hpairs)
    gamma = each(gamma_of, hpairs)
    ts = pk.inverse_many({key: -jnp.where(pk.strict, kq[key][0:C] * gamma[key], 0.0) for key in hpairs},
                         lambda fn, keys: each(lambda *key: fn(key), keys))

    def uw_of(s, hp):
        a, b = (s, 2 * hp), (s, 2 * hp + 1)
        z2 = jnp.zeros((C, 2 * HEAD_A), F32)
        rhs = jnp.concatenate([jnp.concatenate([v[a] * bcol[a], kb[a] * eg[a], z2], axis=1),
                               jnp.concatenate([z2, v[b] * bcol[b], kb[b] * eg[b]], axis=1)], axis=0)
        return _bdot(ts[s, hp], rhs)

    uw = each(uw_of, hpairs)
    u = {(s, h): uw[s, h // 2][:, (h % 2) * 2 * HEAD_A:(h % 2) * 2 * HEAD_A + HEAD_A] for s, h in heads}
    w = {(s, h): uw[s, h // 2][:, (h % 2) * 2 * HEAD_A + HEAD_A:(h % 2 + 1) * 2 * HEAD_A] for s, h in heads}
    st = {(s, h): s_ref[s, 0, h] for s, h in heads}
    wqs = each(lambda s, h: _bdot(jnp.concatenate([w[s, h], q[s, h] * eg[s, h]], axis=0), st[s, h]), heads)
    v_new = {key: u[key] - wqs[key][0:C] for key in heads}

    def o_of(s, hp):
        a, b = (s, 2 * hp), (s, 2 * hp + 1)
        vn_bd = jnp.concatenate([jnp.concatenate([v_new[a], zc], axis=1),
                                 jnp.concatenate([zc, v_new[b]], axis=1)], axis=0)
        return _bdot(kq[s, hp][C:2 * C] * gamma[s, hp], vn_bd)

    op = each(o_of, hpairs)
    o = {(s, h): wqs[s, h][C:2 * C] + op[s, h // 2][:, (h % 2) * HEAD_A:(h % 2 + 1) * HEAD_A] for s, h in heads}
    for s, h in heads:
        kd = k[s, h] * jnp.exp(gtot[s, h] - gcol[s, h])
        s_ref[s, 0, h] = st[s, h] * jnp.exp(gtot[s, h]) + _bdot_tn(kd, v_new[s, h])
        tick()
    osq = each(lambda s, h: lane_sum(o[s, h] * o[s, h]), heads)
    for s, h in heads:
        z = z_ref[s, :, h * HEAD_A:(h + 1) * HEAD_A]
        oh = o[s, h] * lax.rsqrt(osq[s, h] * (1.0 / HEAD_A) + RMS_EPS)
        o_ref[s, :, h * HEAD_A:(h + 1) * HEAD_A] = (oh * on_ref[...] * _silu(z)).astype(BF16)
        tick()
    for _ in ahead:
        pass


def _delta_multi_mixer(proj, nb, seq, C, NS, prev, s0, cw, alog_row, dtb_row, onorm):
    nch = seq // C
    ng = nb // NS
    proj3 = proj.reshape(NS, ng * seq, P_CAT)
    nxt = lambda i, c: i * nch + jnp.minimum(c + 1, nch - 1)
    oa, sd = pl.pallas_call(
        functools.partial(_delta_multi_kernel, C=C, NS=NS),
        out_shape=(jax.ShapeDtypeStruct((NS, ng * seq, W_A), BF16),
                   jax.ShapeDtypeStruct((NS, ng, H_A, HEAD_A, HEAD_A), F32)),
        grid=(ng, nch),
        in_specs=[
            pl.BlockSpec((NS, C, 3 * W_A), lambda i, c: (0, i * nch, 0)),
            pl.BlockSpec((NS, C, LANE), lambda i, c: (0, i * nch, SM0 // LANE)),
            pl.BlockSpec((NS, C, 3 * W_A), lambda i, c: (0, nxt(i, c), 0)),
            pl.BlockSpec((NS, C, LANE), lambda i, c: (0, nxt(i, c), SM0 // LANE)),
            pl.BlockSpec((NS, C, W_A), lambda i, c: (0, i * nch + c, 3)),
            pl.BlockSpec((SUB, 3 * W_A), lambda i, c: (0, 0)),
            pl.BlockSpec((None, H_A, HEAD_A, HEAD_A), lambda i, c: (0, 0, 0, 0)),
            pl.BlockSpec((4, 3 * W_A), lambda i, c: (0, 0)),
            pl.BlockSpec((1, LANE), lambda i, c: (0, 0)),
            pl.BlockSpec((1, LANE), lambda i, c: (0, 0)),
            pl.BlockSpec((1, HEAD_A), lambda i, c: (0, 0)),
        ],
        out_specs=(pl.BlockSpec((NS, C, W_A), lambda i, c: (0, i * nch + c, 0)),
                   pl.BlockSpec((NS, 1, H_A, HEAD_A, HEAD_A), lambda i, c: (0, i, 0, 0, 0))),
        scratch_shapes=[pltpu.VMEM((NS, SUB, 3 * W_A), F32), pltpu.VMEM((2, NS, 3, C, W_A), F32),
                        pltpu.VMEM((2, NS, 2, C, LANE), F32), pltpu.VMEM((2, NS, SUB, LANE), F32)],
        compiler_params=pltpu.CompilerParams(
            dimension_semantics=("parallel", "arbitrary"), vmem_limit_bytes=VMEM_LIMIT),
        name="delta_multi_mixer",
    )(proj3, proj3, proj3, proj3, proj3, prev, s0, cw, alog_row, dtb_row, onorm)
    return oa.reshape(nb * seq, W_A), sd.reshape(nb, H_A, HEAD_A, HEAD_A)


def _rwkv_kernel(r_ref, k_ref, v_ref, sm_ref, pr_ref, pk_ref, pv_ref, psm_ref, s0_ref,
                 mur_ref, muk_ref, muv_ref, musm_ref, w2_ref, a2_ref, g2_ref,
                 w0_ref, a0_ref, kk_ref, ka_ref, rk_ref, lnw_ref, lnb_ref,
                 o_ref, s_ref, *, C, NB, G):
    R = NB * C
    seqs = range(NB)
    pairs = range(G)

    r2 = _iota((LANE, LANE), 0)
    c2 = _iota((LANE, LANE), 1)
    ones_bd = ((r2 < HEAD_B) == (c2 < HEAD_B)).astype(F32)
    hr_ref, hk_ref, hv_ref, hsm_ref = pr_ref, pk_ref, pv_ref, psm_ref
    tri, ones_seq = _seq_masks(R, C)

    def lerp(x, hist_ref, mu):
        hist = hist_ref[...] if NB == 1 else [_repeat_rows(hist_ref[0], SUB)]
        return x + (_delayed(x, hist, 1, NB) - x) * mu

    sm = sm_ref[...]
    xs = lerp(sm, hsm_ref, musm_ref[...])
    slab_wa = xs[:, SM_WA:SM_WA + LANE]
    slab_g = xs[:, SM_G:SM_G + 2 * LANE]

    def col(x, p):
        return x[:, p * LANE:(p + 1) * LANE]

    def to_rows(x):
        return jnp.concatenate([col(x, p) for p in pairs], axis=0)

    def to_cols(y):
        return jnp.concatenate([y[p * R:(p + 1) * R] for p in pairs], axis=1)

    def seq_rows(x, n):
        return x[n * C:(n + 1) * C]

    xr = lerp(r_ref[...], hr_ref, mur_ref[...])
    xk = lerp(k_ref[...], hk_ref, muk_ref[...])
    xv = lerp(v_ref[...], hv_ref, muv_ref[...])
    wlog = -_softplus(-(w0_ref[...] + _bdot(jnp.tanh(slab_wa), w2_ref[...]))) - 0.5
    ld = -jnp.exp(wlog)
    aa = _sigmoid(a0_ref[...] + _bdot(slab_wa, a2_ref[...]))
    gate = _bdot(_sigmoid(slab_g), g2_ref[...])
    kkr = xk * kk_ref[...]
    kkn = kkr * lax.rsqrt(to_cols(_bdot(to_rows(kkr * kkr), ones_bd)) + 1e-6)
    k2 = xk * (1.0 + (aa - 1.0) * ka_ref[...])
    lp = _xdot_r(tri, ld, 3)
    lp_tot = lp[C - 1:C, :] if NB == 1 else _xdot_r(ones_seq, ld, 3)
    e_neg = jnp.exp(-lp)
    e_rem = jnp.exp(lp_tot - lp)
    at = -kkn * jnp.exp(lp - ld)
    rt = xr * jnp.exp(lp)
    kb = kkn * aa
    bt = kb * e_neg
    kt = k2 * e_neg
    bhat = kb * e_rem
    khat = k2 * e_rem
    p_tot = jnp.exp(lp_tot)

    s = [[s0_ref[n, p] for p in pairs] for n in seqs]
    atp = [col(at, p) for p in pairs]
    rtp = [col(rt, p) for p in pairs]
    first_head = _iota((2 * C, LANE), 1) < HEAD_B
    top_rows = _iota((LANE, LANE), 0) < HEAD_B

    def other_half(x):
        return pltpu.roll(x, HEAD_B, 1)

    def times_state_t(a, sp):
        both = _bdot_nt(jnp.concatenate([a[:, :HEAD_B], other_half(a)[:, :HEAD_B]], axis=0), sp)
        m = a.shape[0]
        return jnp.where(first_head, both[0:m], both[m:2 * m])

    ars = [[times_state_t(jnp.concatenate([seq_rows(atp[p], n), seq_rows(rtp[p], n)], axis=0), s[n][p])
            for p in pairs] for n in seqs]
    if NB == 1:
        x_state = [ars[0][p][0:C] for p in pairs]
        o_state = [ars[0][p][C:2 * C] for p in pairs]
    else:
        x_state = [jnp.concatenate([ars[n][p][0:C] for n in seqs], axis=0) for p in pairs]
        o_state = [jnp.concatenate([ars[n][p][C:2 * C] for n in seqs], axis=0) for p in pairs]
    pk = _Packed(R, C)
    stack_mask = (_iota((2 * R, LANE), 0) < R) == (_iota((2 * R, LANE), 1) < HEAD_B)

    def by_head(x):
        return jnp.where(stack_mask, jnp.concatenate([x, x], axis=0), 0.0)

    ar = [jnp.concatenate([atp[p], rtp[p]], axis=0) for p in pairs]
    if 2 * R == LANE:
        abk = [_bdot_nt(ar[p], jnp.concatenate([by_head(col(bt, p)), by_head(col(kt, p))], axis=0)) for p in pairs]
        ab = [m[:, 0:2 * R] for m in abk]
        ak = [m[:, 2 * R:4 * R] for m in abk]
    else:
        ab = [_bdot_nt(ar[p], by_head(col(bt, p))) for p in pairs]
        ak = [_bdot_nt(ar[p], by_head(col(kt, p))) for p in pairs]
    tinv = pk.inverse_many({p: jnp.where(pk.strict, ab[p][0:R], 0.0) for p in pairs})
    v_bh = [by_head(col(xv, p)) for p in pairs]
    akv = [_bdot(jnp.where(pk.strict_over_incl, ak[p], 0.0), v_bh[p]) for p in pairs]
    y = [x_state[p] + akv[p][0:R] for p in pairs]
    u = [_bdot(tinv[p], by_head(y[p])) for p in pairs]
    o = [o_state[p] + akv[p][R:2 * R] + _bdot(jnp.where(pk.incl, ab[p][R:2 * R], 0.0), by_head(u[p]))
         for p in pairs]
    for n in seqs:
        for p in pairs:
            uv = jnp.concatenate([seq_rows(u[p], n), seq_rows(col(xv, p), n)], axis=0)
            bkhat = jnp.concatenate([seq_rows(col(bhat, p), n), seq_rows(col(khat, p), n)], axis=0)
            decay = col(p_tot, p) if NB == 1 else col(p_tot, p)[n * C:n * C + 1]
            full = _bdot_tn(uv, bkhat)
            upd = jnp.where(top_rows, full, other_half(full))[:, :HEAD_B]
            dec = jnp.where(top_rows[:, :HEAD_B], decay[:, :HEAD_B], other_half(decay)[:, :HEAD_B])
            s_ref[n, p] = s[n][p] * dec + upd

    o_rows = jnp.concatenate(o, axis=0)
    mean = _bdot(o_rows, ones_bd) * (1.0 / HEAD_B)
    d = o_rows - mean
    var = _bdot(d * d, ones_bd) * (1.0 / HEAD_B)
    on = to_cols(d * lax.rsqrt(var + GN_EPS)) * lnw_ref[...] + lnb_ref[...]
    bonus = to_cols(_bdot(to_rows(xr * k2 * rk_ref[...]), ones_bd)) * xv
    o_ref[...] = ((on + bonus) * gate).astype(BF16)


def _rwkv_mixer(proj, row0, nb, seq, C, NB, G, prev_rkv, prev_sm, s0, mu_rkv, mu_sm, w2p, a2p, g2p,
                w0, a0, k_k, k_a, r_k, lnw, lnb):
    nch = seq // C
    assert nch == 1 and (NB == 1 or C == SUB)
    R = NB * C
    ng = N_PAIR // G
    gw = G * LANE
    blk0 = row0 // R
    bcast = s0.shape[0] == 1
    bsel = (lambda b: 0) if bcast else (lambda b: b)

    def proj_spec(col0):
        return pl.BlockSpec((R, gw), lambda b, g, c: (blk0 + b * nch + c, col0 // gw + g))

    def prev_spec(part):
        if NB == 1:
            return pl.BlockSpec((SUB, gw), lambda b, g, c: (0, part * (W_B // gw) + g))
        return pl.BlockSpec((1, NB, gw), lambda b, g, c: (0, b, part * (W_B // gw) + g))

    def vec_spec(part=0):
        return pl.BlockSpec((1, gw), lambda b, g, c: (0, part * (W_B // gw) + g))

    in_specs = [
        proj_spec(RKV0), proj_spec(RKV0 + W_B), proj_spec(RKV0 + 2 * W_B),
        pl.BlockSpec((R, SM_W), lambda b, g, c: (blk0 + b * nch + c, SM0 // SM_W)),
        prev_spec(0), prev_spec(1), prev_spec(2),
        (pl.BlockSpec((SUB, SM_W), lambda b, g, c: (0, 0)) if NB == 1
         else pl.BlockSpec((1, NB, SM_W), lambda b, g, c: (0, b, 0))),
        pl.BlockSpec((NB, G, LANE, HEAD_B), lambda b, g, c: (bsel(b), g, 0, 0)),
        vec_spec(0), vec_spec(1), vec_spec(2),
        pl.BlockSpec((1, SM_W), lambda b, g, c: (0, 0)),
        pl.BlockSpec((LANE, gw), lambda b, g, c: (0, g)),
        pl.BlockSpec((LANE, gw), lambda b, g, c: (0, g)),
        pl.BlockSpec((2 * LANE, gw), lambda b, g, c: (0, g)),
        vec_spec(), vec_spec(), vec_spec(), vec_spec(), vec_spec(), vec_spec(), vec_spec(),
    ]
    return pl.pallas_call(
        functools.partial(_rwkv_kernel, C=C, NB=NB, G=G),
        out_shape=(jax.ShapeDtypeStruct((nb * seq, W_B), BF16),
                   jax.ShapeDtypeStruct((nb, N_PAIR, LANE, HEAD_B), F32)),
        grid=(nb // NB, ng, nch),
        in_specs=in_specs,
        out_specs=(pl.BlockSpec((R, gw), lambda b, g, c: (b * nch + c, g)),
                   pl.BlockSpec((NB, G, LANE, HEAD_B), lambda b, g, c: (b, g, 0, 0))),
        compiler_params=pltpu.CompilerParams(
            dimension_semantics=("parallel", "parallel", "arbitrary"), vmem_limit_bytes=VMEM_LIMIT),
        name="rwkv_mixer",
    )(proj, proj, proj, proj, prev_rkv, prev_rkv, prev_rkv, prev_sm, s0,
      mu_rkv, mu_rkv, mu_rkv, mu_sm, w2p, a2p, g2p, w0, a0, k_k, k_a, r_k, lnw, lnb)


_AT, _RT, _BT, _KT, _BHAT, _KHAT, _XV, _GATE, _BONUS, _N_PRE = range(10)


def _rwkv_multi_kernel(r0_ref, k0_ref, v0_ref, sm0_ref, rn_ref, kn_ref, vn_ref, smn_ref,
                       pr_ref, pk_ref, pv_ref, psm_ref, s0_ref,
                       mur_ref, muk_ref, muv_ref, musm_ref, w2_ref, a2_ref, g2_ref,
                       w0_ref, a0_ref, kk_ref, ka_ref, rk_ref, lnw_ref, lnb_ref,
                       o_ref, s_ref, cr_ref, ck_ref, cv_ref, csm_ref, st_ref, pre_ref, ptot_ref, *, C, G, NS):
    ci = pl.program_id(1)
    nch = pl.num_programs(1)
    pairs = range(G)
    chains = [(s, p) for s in range(NS) for p in pairs]

    r2 = _iota((LANE, LANE), 0)
    c2 = _iota((LANE, LANE), 1)
    same_head = (r2 < HEAD_B) == (c2 < HEAD_B)
    ones_bd = same_head.astype(F32)
    spread = (_iota((HEAD_B, LANE), 0) == (_iota((HEAD_B, LANE), 1) & (HEAD_B - 1))).astype(F32)
    gather = ((_iota((LANE, HEAD_B), 0) & (HEAD_B - 1)) == _iota((LANE, HEAD_B), 1)).astype(F32)
    tri, _ = _seq_masks(C, C)

    def col(x, p):
        return x[:, p * LANE:(p + 1) * LANE]

    def to_rows(x):
        return jnp.concatenate([col(x, p) for p in pairs], axis=0)

    def to_cols(y):
        return jnp.concatenate([y[p * C:(p + 1) * C] for p in pairs], axis=1)

    def lerp(x, hist, mu):
        return x + (_shifted(x, hist, 1) - x) * mu

    def preamble(r_ref, k_ref, v_ref, sm_ref, hist_of, slot):
        for s in range(NS):
            hr, hk, hv, hsm = hist_of(s)
            xs = lerp(sm_ref[s], hsm, musm_ref[...])
            yield
            slab_wa = xs[:, SM_WA:SM_WA + LANE]
            wl = _bdot(jnp.tanh(slab_wa), w2_ref[...])
            yield
            al = _bdot(slab_wa, a2_ref[...])
            yield
            pre_ref[slot, s, _GATE] = _bdot(_sigmoid(xs[:, SM_G:SM_G + 2 * LANE]), g2_ref[...])
            yield
            xr = lerp(r_ref[s], hr, mur_ref[...])
            yield
            xk = lerp(k_ref[s], hk, muk_ref[...])
            yield
            xv = lerp(v_ref[s], hv, muv_ref[...])
            pre_ref[slot, s, _XV] = xv
            yield
            ld = -jnp.exp(-_softplus(-(w0_ref[...] + wl)) - 0.5)
            yield
            lp = _xdot_r(tri, ld, 3)
            yield
            aa = _sigmoid(a0_ref[...] + al)
            yield
            kkr = xk * kk_ref[...]
            ss = to_cols(_bdot(to_rows(kkr * kkr), ones_bd))
            yield
            kkn = kkr * lax.rsqrt(ss + 1e-6)
            k2 = xk * (1.0 + (aa - 1.0) * ka_ref[...])
            yield
            pre_ref[slot, s, _BONUS] = to_cols(_bdot(to_rows(xr * k2 * rk_ref[...]), ones_bd)) * xv
            yield
            pre_ref[slot, s, _AT] = -kkn * jnp.exp(lp - ld)
            yield
            pre_ref[slot, s, _RT] = xr * jnp.exp(lp)
            yield
            kb = kkn * aa
            e_neg = jnp.exp(-lp)
            pre_ref[slot, s, _BT] = kb * e_neg
            yield
            pre_ref[slot, s, _KT] = k2 * e_neg
            yield
            lp_tot = lp[C - 1:C, :]
            e_rem = jnp.exp(lp_tot - lp)
            pre_ref[slot, s, _BHAT] = kb * e_rem
            yield
            pre_ref[slot, s, _KHAT] = k2 * e_rem
            ptot_ref[slot, s] = jnp.broadcast_to(jnp.exp(lp_tot), (SUB, lp.shape[1]))
            yield
            cr_ref[s] = r_ref[s, C - SUB:C, :]
            ck_ref[s] = k_ref[s, C - SUB:C, :]
            cv_ref[s] = v_ref[s, C - SUB:C, :]
            csm_ref[s] = sm_ref[s, C - SUB:C, :]

    @pl.when(ci == 0)
    def _():
        first_hist = lambda s: (pr_ref[...], pk_ref[...], pv_ref[...], psm_ref[...])
        for _ in preamble(r0_ref, k0_ref, v0_ref, sm0_ref, first_hist, 0):
            pass
        for s, p in chains:
            st_ref[s * G + p] = jnp.where(same_head, _xdot(s0_ref[p], spread, 3), 0.0)

    slot = ci % 2
    carried = lambda s: (cr_ref[s], ck_ref[s], cv_ref[s], csm_ref[s])
    ahead = preamble(rn_ref, kn_ref, vn_ref, smn_ref, carried, 1 - slot)
    calls = [0]

    def tick():
        calls[0] += 1
        if calls[0] % 6 == 0:
            next(ahead, None)

    def each(fn, items):
        out = {}
        for it in items:
            out[it] = fn(*it)
            tick()
        return out

    def pre(idx, s, p):
        return pre_ref[slot, s, idx, :, p * LANE:(p + 1) * LANE]

    pk = _Packed(C, C)
    stack_mask = (_iota((2 * C, LANE), 0) < C) == (_iota((2 * C, LANE), 1) < HEAD_B)

    def by_head(x):
        return jnp.where(stack_mask, jnp.concatenate([x, x], axis=0), 0.0)

    st = {(s, p): st_ref[s * G + p] for s, p in chains}
    ar = {(s, p): jnp.concatenate([pre(_AT, s, p), pre(_RT, s, p)], axis=0) for s, p in chains}
    ars = each(lambda s, p: _bdot_nt(ar[s, p], st[s, p]), chains)
    abk = each(lambda s, p: _bdot_nt(ar[s, p], jnp.concatenate([by_head(pre(_BT, s, p)), by_head(pre(_KT, s, p))],
                                                               axis=0)), chains)
    ab = {key: abk[key][:, 0:2 * C] for key in chains}
    ak = {key: abk[key][:, 2 * C:4 * C] for key in chains}
    ts = pk.inverse_many({key: jnp.where(pk.strict, ab[key][0:C], 0.0) for key in chains},
                         lambda fn, keys: each(lambda *key: fn(key), keys))
    xv = {(s, p): pre(_XV, s, p) for s, p in chains}
    v_bh = {key: by_head(xv[key]) for key in chains}
    akv = each(lambda s, p: _bdot(jnp.where(pk.strict_over_incl, ak[s, p], 0.0), v_bh[s, p]), chains)
    y = {key: ars[key][0:C] + akv[key][0:C] for key in chains}
    u = each(lambda s, p: _bdot(ts[s, p], by_head(y[s, p])), chains)
    o = each(lambda s, p: ars[s, p][C:2 * C] + akv[s, p][C:2 * C]
             + _bdot(jnp.where(pk.incl, ab[s, p][C:2 * C], 0.0), by_head(u[s, p])), chains)
    for s, p in chains:
        uv = jnp.concatenate([u[s, p], xv[s, p]], axis=0)
        bkhat = jnp.concatenate([pre(_BHAT, s, p), pre(_KHAT, s, p)], axis=0)
        s_new = st[s, p] * ptot_ref[slot, s, 0:1, p * LANE:(p + 1) * LANE] + _bdot_tn(uv, bkhat)
        st_ref[s * G + p] = jnp.where(same_head, s_new, 0.0)
        tick()

    o_rows = jnp.concatenate([o[key] for key in chains], axis=0)
    mean = _bdot(o_rows, ones_bd) * (1.0 / HEAD_B)
    d = o_rows - mean
    var = _bdot(d * d, ones_bd) * (1.0 / HEAD_B)
    dn = d * lax.rsqrt(var + GN_EPS)
    for s in range(NS):
        on = to_cols(dn[s * G * C:(s + 1) * G * C]) * lnw_ref[...] + lnb_ref[...]
        o_ref[s] = ((on + pre_ref[slot, s, _BONUS]) * pre_ref[slot, s, _GATE]).astype(BF16)
    for _ in ahead:
        pass

    @pl.when(ci == nch - 1)
    def _():
        for s, p in chains:
            s_ref[s, 0, p] = _xdot(st_ref[s * G + p], gather, 3)


def _rwkv_multi_mixer(proj, nb, seq, C, NS, prev_rkv, prev_sm, s0, mu_rkv, mu_sm, w2p, a2p, g2p,
                      w0, a0, k_k, k_a, r_k, lnw, lnb):
    G = N_PAIR
    nch = seq // C
    ng = nb // NS
    gw = G * LANE
    proj3 = proj.reshape(NS, ng * seq, P_CAT)

    def first_spec(col0, w):
        return pl.BlockSpec((NS, C, w), lambda i, c: (0, i * nch, col0 // w))

    def next_spec(col0, w):
        return pl.BlockSpec((NS, C, w), lambda i, c: (0, i * nch + jnp.minimum(c + 1, nch - 1), col0 // w))

    def prev_spec(part):
        return pl.BlockSpec((SUB, gw), lambda i, c: (0, part))

    def vec_spec(part=0):
        return pl.BlockSpec((1, gw), lambda i, c: (0, part))

    in_specs = [
        first_spec(RKV0, gw), first_spec(RKV0 + W_B, gw), first_spec(RKV0 + 2 * W_B, gw), first_spec(SM0, SM_W),
        next_spec(RKV0, gw), next_spec(RKV0 + W_B, gw), next_spec(RKV0 + 2 * W_B, gw), next_spec(SM0, SM_W),
        prev_spec(0), prev_spec(1), prev_spec(2),
        pl.BlockSpec((SUB, SM_W), lambda i, c: (0, 0)),
        pl.BlockSpec((None, G, LANE, HEAD_B), lambda i, c: (0, 0, 0, 0)),
        vec_spec(0), vec_spec(1), vec_spec(2),
        pl.BlockSpec((1, SM_W), lambda i, c: (0, 0)),
        pl.BlockSpec((LANE, gw), lambda i, c: (0, 0)),
        pl.BlockSpec((LANE, gw), lambda i, c: (0, 0)),
        pl.BlockSpec((2 * LANE, gw), lambda i, c: (0, 0)),
        vec_spec(), vec_spec(), vec_spec(), vec_spec(), vec_spec(), vec_spec(), vec_spec(),
    ]
    ob, sw = pl.pallas_call(
        functools.partial(_rwkv_multi_kernel, C=C, G=G, NS=NS),
        out_shape=(jax.ShapeDtypeStruct((NS, ng * seq, W_B), BF16),
                   jax.ShapeDtypeStruct((NS, ng, N_PAIR, LANE, HEAD_B), F32)),
        grid=(ng, nch),
        in_specs=in_specs,
        out_specs=(pl.BlockSpec((NS, C, gw), lambda i, c: (0, i * nch + c, 0)),
                   pl.BlockSpec((NS, 1, G, LANE, HEAD_B), lambda i, c: (0, i, 0, 0, 0))),
        scratch_shapes=[pltpu.VMEM((NS, SUB, gw), F32), pltpu.VMEM((NS, SUB, gw), F32), pltpu.VMEM((NS, SUB, gw), F32),
                        pltpu.VMEM((NS, SUB, SM_W), F32), pltpu.VMEM((NS * G, LANE, LANE), F32),
                        pltpu.VMEM((2, NS, _N_PRE, C, gw), F32), pltpu.VMEM((2, NS, SUB, gw), F32)],
        compiler_params=pltpu.CompilerParams(
            dimension_semantics=("parallel", "arbitrary"), vmem_limit_bytes=VMEM_LIMIT),
        name="rwkv_multi_mixer",
    )(proj3, proj3, proj3, proj3, proj3, proj3, proj3, proj3, prev_rkv, prev_rkv, prev_rkv, prev_sm, s0,
      mu_rkv, mu_rkv, mu_rkv, mu_sm, w2p, a2p, g2p, w0, a0, k_k, k_a, r_k, lnw, lnb)
    return ob.reshape(nb * seq, W_B), sw.reshape(nb, N_PAIR, LANE, HEAD_B)


def _small_layout(cols_ba, cols_w, cols_a, cols_g, axis=-1):
    def z(n):
        shape = list(cols_w.shape)
        shape[axis] = n
        return jnp.zeros(shape, cols_w.dtype)
    return jnp.concatenate(
        [cols_ba, z(SM_WA - cols_ba.shape[axis]), cols_w, cols_a, cols_g,
         z(SM_W - SM_G - cols_g.shape[axis])], axis=axis)


def kernel(x_prompt, x_sample, state_delta, state_conv_qkv, state_wkv, state_shift, state_ffn_conv, meta, norm1, w_in, conv_a, a_log, dt_bias, onorm_a, mu_b, w0, w2, a0, a2, g2, k_k, k_a, r_k, lnx_w, lnx_b, w_o, norm2, w_ffn_in, conv_f, w_ffn_out, norm_f):
    nbp, seq_p, _ = x_prompt.shape
    nbs, seq_s, _ = x_sample.shape
    n_s = nbs * seq_s
    assert w_in.shape[0] == 1, "single-layer trunk"
    assert seq_s == SUB
    l = 0

    wt = w_in[l].T
    o_b = A_PROJ
    o_l = A_PROJ + 3 * W_B
    w_cat_t = jnp.concatenate([
        wt[:4 * W_A], wt[o_b:o_l],
        _small_layout(wt[4 * W_A:A_PROJ], wt[o_l:o_l + W_LORA], wt[o_l + W_LORA:o_l + W_LORA + A_LORA],
                      wt[o_l + W_LORA + A_LORA:], axis=0)], axis=0).astype(BF16)
    mu = mu_b[l]
    mu_rkv = mu[None, :3 * W_B]
    mu_sm = _small_layout(jnp.zeros((1, 2 * H_A), F32), mu[None, 3 * W_B:3 * W_B + W_LORA],
                          mu[None, 3 * W_B + W_LORA:3 * W_B + W_LORA + A_LORA],
                          mu[None, 3 * W_B + W_LORA + A_LORA:])
    w2p = jnp.concatenate([w2[l], jnp.zeros((LANE - W_LORA, W_B), F32)], axis=0)
    a2p = jnp.concatenate([jnp.zeros((W_LORA, W_B), F32), a2[l]], axis=0)
    g2p = jnp.concatenate([g2[l], jnp.zeros((2 * LANE - G_LORA, W_B), F32)], axis=0)
    alog_row = jnp.concatenate([jnp.zeros((H_A,), F32), a_log[l], jnp.zeros((LANE - 2 * H_A,), F32)])[None]
    dtb_row = jnp.concatenate([jnp.zeros((H_A,), F32), dt_bias[l], jnp.zeros((LANE - 2 * H_A,), F32)])[None]
    wo_bf = w_o[l].astype(BF16)
    wfo_bf = w_ffn_out[l].astype(BF16)
    row = lambda v: v.reshape(1, -1)

    def mix(proj, row0, nb, seq, C, NB, prev_qkv, prev_rkv, prev_sm, s_delta, s_wkv):
        delta_params = (conv_a[l], alog_row, dtb_row, row(onorm_a[l]))
        rwkv_params = (mu_rkv, mu_sm, w2p, a2p, g2p, row(w0[l]), row(a0[l]), row(k_k[l]), row(k_a[l]),
                       row(r_k[l]), row(lnx_w[l]), row(lnx_b[l]))
        if seq // C > 1:
            assert row0 == 0 and NB == 1
            oa, sd = _delta_multi_mixer(proj, nb, seq, C, 4, prev_qkv, s_delta, *delta_params)
            ob, sw = _rwkv_multi_mixer(proj, nb, seq, C, 2, prev_rkv, prev_sm, s_wkv, *rwkv_params)
        else:
            oa, sd = _delta_mixer(proj, row0, nb, seq, C, NB, prev_qkv, s_delta, *delta_params)
            ob, sw = _rwkv_mixer(proj, row0, nb, seq, C, NB, N_PAIR, prev_rkv, prev_sm, s_wkv, *rwkv_params)
        return oa, ob, sd, sw

    xs_rows = jnp.concatenate([x_sample.reshape(n_s, D_MODEL), meta], axis=0)
    xp_rows = x_prompt.reshape(nbp * seq_p, D_MODEL)
    n_small = n_s + N_META
    proj_s, proj_tail_s = _norm_matmul(xs_rows, row(norm1[l]), w_cat_t, n_small, 1536, 3, nbs)
    proj_p = _norm_matmul(xp_rows, row(norm1[l]), w_cat_t, 1024, 1536)

    zeros = lambda *s: jnp.zeros(s, F32)
    oa_m, ob_m, sd_m, sw_m = mix(proj_s, n_s, 1, N_META, N_META, 1, zeros(SUB, 3 * W_A),
                                 zeros(SUB, 3 * W_B), zeros(SUB, SM_W),
                                 zeros(1, H_A, HEAD_A, HEAD_A), zeros(1, N_PAIR, LANE, HEAD_B))
    tail = proj_s[n_small - SUB:n_small]
    oa_p, ob_p, sd_p, sw_p = mix(proj_p, 0, nbp, seq_p, 64, 1, tail[:, :3 * W_A],
                                 tail[:, RKV0:SM0], tail[:, SM0:], sd_m, sw_m)
    sh = state_shift[l]
    sh_sm = _small_layout(jnp.zeros((nbs, 1, 2 * H_A), F32), sh[..., 3 * W_B:3 * W_B + W_LORA],
                          sh[..., 3 * W_B + W_LORA:3 * W_B + W_LORA + A_LORA],
                          sh[..., 3 * W_B + W_LORA + A_LORA:])
    tmajor = lambda s: s.transpose(1, 0, 2)
    oa_s, ob_s, sd_s, sw_s = mix(proj_s, 0, nbs, seq_s, seq_s, 8, tmajor(state_conv_qkv[l]),
                                 tmajor(sh[..., :3 * W_B]), tmajor(sh_sm), state_delta[l],
                                 state_wkv[l].reshape(nbs, N_PAIR, LANE, HEAD_B))

    oa_small = jnp.concatenate([oa_s, oa_m], axis=0)
    ob_small = jnp.concatenate([ob_s, ob_m], axis=0)
    x1_s, h2_s = _out_proj(xs_rows, oa_small, ob_small, wo_bf, row(norm2[l]), n_small // 5)
    x1_p, h2_p = _out_proj(xp_rows, oa_p, ob_p, wo_bf, row(norm2[l]), 512)
    act_s, gate_tail_s, gate_last_s = _ffn_in(h2_s, w_ffn_in[l], state_ffn_conv[l].transpose(1, 0, 2), conv_f[l],
                                              n_small, 768, 3, 0, n_s + SUB)
    act_p, tail_p = _ffn_in(h2_p, w_ffn_in[l], gate_last_s[None], conv_f[l], 1024, 768, 3, seq_p // 1024)
    y_s = _ffn_out(act_s, wfo_bf, x1_s, row(norm_f), n_s, 1024, 768)
    y_p = _ffn_out(act_p, wfo_bf, x1_p, row(norm_f), nbp * seq_p, 1024, 768)

    def states(conv_new, last, ffn_new, nb, sd, sw):
        shift_new = jnp.concatenate([last[..., RKV0:SM0], last[..., SM0 + SM_WA:SM0 + SM_WA + W_LORA + A_LORA],
                                     last[..., SM0 + SM_G:SM0 + SM_G + G_LORA]], axis=-1)
        return (sd[None], conv_new[None], sw.reshape(nb, H_B, HEAD_B, HEAD_B)[None], shift_new[None],
                ffn_new[None])

    p3 = proj_p.reshape(nbp, seq_p, P_CAT)
    tail_s = proj_tail_s.transpose(1, 0, 2)
    return ((y_p.reshape(nbp, seq_p, D_MODEL), y_s.reshape(nbs, seq_s, D_MODEL))
            + states(p3[:, seq_p - 3:, :3 * W_A], p3[:, seq_p - 1:, :], tail_p[:, SUB - 2:, :], nbp, sd_p, sw_p)
            + states(tail_s[:, :, :3 * W_A], tail_s[:, 2:, :], gate_tail_s.transpose(1, 0, 2), nbs, sd_s, sw_s))
```

```python
import functools

import jax
import jax.numpy as jnp
from jax import lax
from jax.experimental import pallas as pl
from jax.experimental.pallas import tpu as pltpu

F32 = jnp.float32
BF16 = jnp.bfloat16

D_MODEL = 2048
N_META = 16
W_A = 1024
HEAD_A = 128
H_A = 8
W_B = 1024
HEAD_B = 64
H_B = 16
N_PAIR = H_B // 2
W_LORA = 64
A_LORA = 64
G_LORA = 160
D_FF = 5376
RMS_EPS = 1e-6
GN_EPS = 64e-5
A_PROJ = 4 * W_A + 2 * H_A
B_PROJ = 3 * W_B + W_LORA + A_LORA + G_LORA

QKVZ0 = 0
RKV0 = 4 * W_A
SM0 = RKV0 + 3 * W_B
SM_W = 512
SM_BA = 0
SM_WA = 128
SM_G = 256
P_CAT = SM0 + SM_W

LANE = 128
SUB = 8
VMEM_LIMIT = 48 * 1024 * 1024
VMEM_LIMIT_BIG = 58 * 1024 * 1024
NEG_BIG = -1e30

NT_DIMS = (((1,), (1,)), ((), ()))


def _bdot(a, b):
    return jnp.dot(a.astype(BF16), b.astype(BF16), preferred_element_type=F32)


def _bdot_nt(a, b):
    return lax.dot_general(a.astype(BF16), b.astype(BF16), NT_DIMS, preferred_element_type=F32)


def _bdot_tn(a, b):
    return lax.dot_general(a.astype(BF16), b.astype(BF16), (((0,), (0,)), ((), ())),
                           preferred_element_type=F32)


def _pieces(a, n):
    out = []
    rem = a
    for i in range(n):
        p = rem.astype(BF16)
        out.append(p)
        if i + 1 < n:
            rem = rem - p.astype(F32)
    return out


def _xdot(a, b, n, dims=(((1,), (0,)), ((), ()))):
    bb = b.astype(BF16)
    acc = None
    for p in _pieces(a, n):
        t = lax.dot_general(p, bb, dims, preferred_element_type=F32)
        acc = t if acc is None else acc + t
    return acc


def _xdot_r(a, b, n, dims=(((1,), (0,)), ((), ()))):
    ab = a.astype(BF16)
    acc = None
    for p in _pieces(b, n):
        t = lax.dot_general(ab, p, dims, preferred_element_type=F32)
        acc = t if acc is None else acc + t
    return acc


def _sigmoid(x):
    return 1.0 / (1.0 + jnp.exp(-x))


def _silu(x):
    return x * _sigmoid(x)


def _softplus(x):
    return jnp.maximum(x, 0.0) + jnp.log(1.0 + jnp.exp(-jnp.abs(x)))


def _iota(shape, dim):
    return lax.broadcasted_iota(jnp.int32, shape, dim)


def _shifted(x, prev8, k):
    n = x.shape[0]
    xr = pltpu.roll(x, k, 0)
    pr = pltpu.roll(prev8, k, 0)
    first = jnp.where(_iota((SUB, x.shape[1]), 0) < k, pr, xr[0:SUB])
    if n == SUB:
        return first
    return jnp.concatenate([first, xr[SUB:]], axis=0)


def _repeat_rows(s, reps):
    n, c = s.shape
    return jnp.broadcast_to(s[:, None, :], (n, reps, c)).reshape(n * reps, c)


def _delayed(x, hist, k, nseq):
    if nseq == 1:
        return _shifted(x, hist, k)
    t = _iota(x.shape, 0) & (SUB - 1)
    d = pltpu.roll(x, k, 0)
    for tt in range(k):
        d = jnp.where(t == tt, hist[len(hist) - k + tt], d)
    return d


def _same_seq(shape, c, rows_total):
    if c == rows_total:
        return None
    sh = c.bit_length() - 1
    return (lax.shift_right_logical(_iota(shape, 0) & (rows_total - 1), sh)
            == lax.shift_right_logical(_iota(shape, 1) & (rows_total - 1), sh))


class _Packed:
    def __init__(self, r, c):
        self.r = r
        self.c = c
        row = _iota((r, 2 * r), 0)
        col = _iota((r, 2 * r), 1) & (r - 1)
        same = _same_seq((r, 2 * r), c, r)
        self.incl = (row >= col) if same is None else ((row >= col) & same)
        self.strict = (row > col) if same is None else ((row > col) & same)
        self.eye = (row == col).astype(F32)
        row2 = _iota((2 * r, 2 * r), 0)
        lower2 = ((row2 & (r - 1)) - (_iota((2 * r, 2 * r), 1) & (r - 1)) + jnp.where(row2 < r, 0, 1)) > 0
        same2 = _same_seq((2 * r, 2 * r), c, r)
        self.strict_over_incl = lower2 if same2 is None else (lower2 & same2)
        self.first = _iota((r, 2 * r), 1) < r
        self.bd_mask = (_iota((2 * r, 2 * r), 0) < r) == (_iota((2 * r, 2 * r), 1) < r)

    def block_diag(self, m):
        return jnp.where(self.bd_mask, jnp.concatenate([m, m], axis=0), 0.0)

    def inverse_many(self, xs, each=None):
        if each is None:
            each = lambda fn, keys: {k: fn(k) for k in keys}
        keys = list(xs)
        r = self.r
        a = dict(xs)
        s = {k: self.eye + xs[k] for k in keys}
        levels = self.c.bit_length() - 1
        if levels >= 2:
            a = each(lambda k: _bdot(a[k], self.block_diag(a[k])), keys)
        for j in range(1, levels):
            if j < levels - 1:
                both = each(lambda k: _bdot(jnp.concatenate([s[k], a[k]], axis=0), self.block_diag(a[k])), keys)
                s = {k: s[k] + both[k][0:r] for k in keys}
                a = {k: both[k][r:2 * r] for k in keys}
            else:
                sa = each(lambda k: _bdot(s[k], self.block_diag(a[k])), keys)
                s = {k: s[k] + sa[k] for k in keys}
        return s


def _seq_masks(r, c):
    row = _iota((r, r), 0)
    col = _iota((r, r), 1)
    same = _same_seq((r, r), c, r)
    if same is None:
        return (row >= col).astype(F32), None
    return ((row >= col) & same).astype(F32), same.astype(F32)


def _seq_tail_rows(x, slab_ref, dst_ref, n_tail, nseq, col0=0):
    for s in range(x.shape[1] // LANE):
        slab_ref[s] = x[:, s * LANE:(s + 1) * LANE]
        for t in range(n_tail):
            dst_ref[t, :, col0 + s * LANE:col0 + (s + 1) * LANE] = (
                slab_ref[s, pl.ds(SUB - n_tail + t, nseq, stride=SUB), :])


def _norm_matmul_kernel(x_ref, g_ref, w_ref, o_ref, *rest, n_tail, nseq):
    tail_ref, h_ref, slab_ref = rest if n_tail else (None,) + rest + (None,)

    @pl.when(pl.program_id(1) == 0)
    def _():
        x = x_ref[...]
        ms = jnp.mean(x * x, axis=-1, keepdims=True)
        h_ref[...] = (x * lax.rsqrt(ms + RMS_EPS) * g_ref[...]).astype(BF16)

    out = lax.dot_general(h_ref[...], w_ref[...], NT_DIMS, preferred_element_type=F32)
    o_ref[...] = out
    if n_tail:
        _seq_tail_rows(out, slab_ref, tail_ref, n_tail, nseq)


def _norm_matmul(x, g, wt, tm, tn, n_tail=0, nseq=0):
    m, k = x.shape
    n = wt.shape[0]
    out_shape = jax.ShapeDtypeStruct((m, n), F32)
    out_specs = pl.BlockSpec((tm, tn), lambda i, j: (i, j))
    if n_tail:
        assert m == tm
        out_shape = (out_shape, jax.ShapeDtypeStruct((n_tail, nseq, n), F32))
        out_specs = (out_specs, pl.BlockSpec((n_tail, nseq, tn), lambda i, j: (0, 0, j)))
    return pl.pallas_call(
        functools.partial(_norm_matmul_kernel, n_tail=n_tail, nseq=nseq),
        out_shape=out_shape,
        grid=(m // tm, n // tn),
        in_specs=[
            pl.BlockSpec((tm, k), lambda i, j: (i, 0)),
            pl.BlockSpec((1, k), lambda i, j: (0, 0)),
            pl.BlockSpec((tn, k), lambda i, j: (j, 0)),
        ],
        out_specs=out_specs,
        scratch_shapes=[pltpu.VMEM((tm, k), BF16)] + ([pltpu.VMEM((tn // LANE, tm, LANE), F32)] if n_tail else []),
        compiler_params=pltpu.CompilerParams(
            dimension_semantics=("parallel", "arbitrary"), vmem_limit_bytes=VMEM_LIMIT_BIG),
        name="norm_matmul",
    )(x, g, wt)


def _out_proj_kernel(x_ref, oa_ref, ob_ref, wt_ref, wb_ref, g_ref, x1_ref, h_ref):
    acc = jnp.dot(oa_ref[...], wt_ref[...], preferred_element_type=F32)
    acc = acc + jnp.dot(ob_ref[...], wb_ref[...], preferred_element_type=F32)
    x1 = x_ref[...] + acc
    x1_ref[...] = x1
    ms = jnp.mean(x1 * x1, axis=-1, keepdims=True)
    h_ref[...] = (x1 * lax.rsqrt(ms + RMS_EPS) * g_ref[...]).astype(BF16)


def _out_proj(x, oa, ob, wo, g, tm):
    m, d = x.shape
    return pl.pallas_call(
        _out_proj_kernel,
        out_shape=(jax.ShapeDtypeStruct((m, d), F32), jax.ShapeDtypeStruct((m, d), BF16)),
        grid=(m // tm,),
        in_specs=[
            pl.BlockSpec((tm, d), lambda i: (i, 0)),
            pl.BlockSpec((tm, W_A), lambda i: (i, 0)),
            pl.BlockSpec((tm, W_B), lambda i: (i, 0)),
            pl.BlockSpec((W_A, d), lambda i: (0, 0)),
            pl.BlockSpec((W_B, d), lambda i: (W_A // W_B, 0)),
            pl.BlockSpec((1, d), lambda i: (0, 0)),
        ],
        out_specs=(pl.BlockSpec((tm, d), lambda i: (i, 0)), pl.BlockSpec((tm, d), lambda i: (i, 0))),
        compiler_params=pltpu.CompilerParams(
            dimension_semantics=("parallel",), vmem_limit_bytes=VMEM_LIMIT),
        name="out_proj",
    )(x, oa, ob, wo, wo, g)


def _ffn_in_kernel(h_ref, wg_ref, wu_ref, prev_ref, cw_ref, act_ref, tail_ref, *rest,
                   tiles_per_seq, state_rows, nsub):
    if tiles_per_seq:
        wgb_ref, wub_ref, carry_ref = rest
    else:
        last_ref, wgb_ref, wub_ref, carry_ref, slab_ref = rest
    m = pl.program_id(1)

    @pl.when(m == 0)
    def _():
        wgb_ref[...] = wg_ref[...].astype(BF16)
        wub_ref[...] = wu_ref[...].astype(BF16)

    if tiles_per_seq:
        @pl.when(m % tiles_per_seq == 0)
        def _():
            carry_ref[...] = prev_ref[...]

    h = h_ref[...]
    n = h.shape[0]
    w = wgb_ref.shape[1] // nsub
    for i in range(nsub):
        cs = slice(i * w, (i + 1) * w)
        gate = jnp.dot(h, wgb_ref[:, cs], preferred_element_type=F32)
        up = jnp.dot(h, wub_ref[:, cs], preferred_element_type=F32)
        if tiles_per_seq:
            prev8 = carry_ref[:, cs]
            d1 = _shifted(gate, prev8, 1)
            d2 = _shifted(gate, prev8, 2)
            carry_ref[:, cs] = gate[n - SUB:n]
            tail_ref[:, cs] = gate[n - SUB:n]
        else:
            nseq = (state_rows - SUB) // SUB
            _seq_tail_rows(gate, slab_ref, tail_ref, 2, nseq, i * w)
            last_ref[:, cs] = gate[n - SUB:n]
            pad = jnp.zeros((n - nseq * SUB, w), F32)
            hist = [jnp.concatenate([_repeat_rows(prev_ref[t, :, cs], SUB), pad], axis=0) for t in range(2)]
            t_in_seq = jnp.where(_iota(gate.shape, 0) < state_rows, _iota(gate.shape, 0) & (SUB - 1), SUB)
            d1 = jnp.where(t_in_seq == 0, hist[1], pltpu.roll(gate, 1, 0))
            d2 = jnp.where(t_in_seq == 0, hist[0], jnp.where(t_in_seq == 1, hist[1], pltpu.roll(gate, 2, 0)))
        y = gate * cw_ref[2:3, cs] + d1 * cw_ref[1:2, cs] + d2 * cw_ref[0:1, cs]
        act_ref[:, cs] = (_silu(y) * up).astype(BF16)


def _ffn_in(h, w, prev, cw, tm, tn, nsub, tiles_per_seq, state_rows=0):
    m, d = h.shape
    nj = D_FF // tn
    nm = m // tm
    if tiles_per_seq:
        nseq = nm // tiles_per_seq
        prev_spec = pl.BlockSpec((None, SUB, tn), lambda j, i: (0, 0, j))
        extra_shapes = (jax.ShapeDtypeStruct((nseq, SUB, D_FF), F32),)
        extra_specs = (pl.BlockSpec((None, SUB, tn), lambda j, i: (i // tiles_per_seq, 0, j)),)
        extra_scratch = []
    else:
        assert nm == 1
        nseq = (state_rows - SUB) // SUB
        prev_spec = pl.BlockSpec((2, nseq, tn), lambda j, i: (0, 0, j))
        extra_shapes = (jax.ShapeDtypeStruct((2, nseq, D_FF), F32), jax.ShapeDtypeStruct((SUB, D_FF), F32))
        extra_specs = (pl.BlockSpec((2, nseq, tn), lambda j, i: (0, 0, j)),
                       pl.BlockSpec((SUB, tn), lambda j, i: (0, j)))
        extra_scratch = [pltpu.VMEM((tn // nsub // LANE, tm, LANE), F32)]
    return pl.pallas_call(
        functools.partial(_ffn_in_kernel, tiles_per_seq=tiles_per_seq, state_rows=state_rows, nsub=nsub),
        out_shape=(jax.ShapeDtypeStruct((m, D_FF), BF16),) + extra_shapes,
        grid=(nj, nm),
        in_specs=[
            pl.BlockSpec((tm, d), lambda j, i: (i, 0)),
            pl.BlockSpec((d, tn), lambda j, i: (0, j)),
            pl.BlockSpec((d, tn), lambda j, i: (0, nj + j)),
            prev_spec,
            pl.BlockSpec((3, tn), lambda j, i: (0, j)),
        ],
        out_specs=(pl.BlockSpec((tm, tn), lambda j, i: (i, j)),) + extra_specs,
        scratch_shapes=[pltpu.VMEM((d, tn), BF16), pltpu.VMEM((d, tn), BF16), pltpu.VMEM((SUB, tn), F32)]
        + extra_scratch,
        compiler_params=pltpu.CompilerParams(
            dimension_semantics=("parallel", "arbitrary"), vmem_limit_bytes=VMEM_LIMIT_BIG),
        name="ffn_in",
    )(h, w, w, prev, cw)


def _ffn_out_kernel(a_ref, w_ref, x_ref, g_ref, o_ref):
    kk = pl.program_id(1)

    @pl.when(kk == 0)
    def _():
        o_ref[...] = x_ref[...]

    o_ref[...] += jnp.dot(a_ref[...], w_ref[...], preferred_element_type=F32)

    @pl.when(kk == pl.num_programs(1) - 1)
    def _():
        x = o_ref[...]
        ms = jnp.mean(x * x, axis=-1, keepdims=True)
        o_ref[...] = x * lax.rsqrt(ms + RMS_EPS) * g_ref[...]


def _ffn_out(act, w, x1, g, m, tm, tk):
    kdim, d = w.shape
    return pl.pallas_call(
        _ffn_out_kernel,
        out_shape=jax.ShapeDtypeStruct((m, d), F32),
        grid=(m // tm, kdim // tk),
        in_specs=[
            pl.BlockSpec((tm, tk), lambda i, k: (i, k)),
            pl.BlockSpec((tk, d), lambda i, k: (k, 0)),
            pl.BlockSpec((tm, d), lambda i, k: (i, 0)),
            pl.BlockSpec((1, d), lambda i, k: (0, 0)),
        ],
        out_specs=pl.BlockSpec((tm, d), lambda i, k: (i, 0)),
        compiler_params=pltpu.CompilerParams(
            dimension_semantics=("parallel", "arbitrary"), vmem_limit_bytes=VMEM_LIMIT_BIG),
        name="ffn_out",
    )(act, w, x1, g)


def _delta_kernel(qkvz_ref, ba_ref, prev_ref, s0_ref, cw_ref, alog_ref, dtb_ref, on_ref,
                  o_ref, s_ref, carry_ref, *, C, NB):
    R = NB * C
    ci = pl.program_id(1)

    @pl.when(ci == 0)
    def _():
        if NB == 1:
            carry_ref[...] = prev_ref[...]
        s_ref[...] = s0_ref[...]

    tri, ones_seq = _seq_masks(R, C)

    ba = ba_ref[...]
    beta_full = _sigmoid(ba)
    g_full = -jnp.exp(alog_ref[...]) * _softplus(ba + dtb_ref[...])
    gc_full = _xdot_r(tri, g_full, 3)
    gtot_full = gc_full[C - 1:C, :] if NB == 1 else _xdot_r(ones_seq, g_full, 3)

    def conv_silu(c0):
        x = qkvz_ref[:, c0:c0 + LANE]
        if NB == 1:
            hist = carry_ref[:, c0:c0 + LANE]
        else:
            hist = [_repeat_rows(prev_ref[j, :, c0:c0 + LANE], SUB) for j in range(3)]
        y = x * cw_ref[3:4, c0:c0 + LANE]
        for k in (1, 2, 3):
            y = y + _delayed(x, hist, k, NB) * cw_ref[3 - k:4 - k, c0:c0 + LANE]
        return _silu(y)

    ones_ll = jnp.ones((LANE, LANE), BF16)

    def lane_sum(x):
        return jnp.dot(x.astype(BF16), ones_ll, preferred_element_type=F32)

    def l2n_many(xs):
        sums = [lane_sum(x * x) for x in xs]
        return [x * lax.rsqrt(sq + 1e-6) for x, sq in zip(xs, sums)]

    def seq_rows(x, n):
        return x[n * C:(n + 1) * C]

    heads = range(H_A)
    seqs = range(NB)
    q = [x * (HEAD_A ** -0.5) for x in l2n_many([conv_silu(h * HEAD_A) for h in heads])]
    k = l2n_many([conv_silu(W_A + h * HEAD_A) for h in heads])
    v = [conv_silu(2 * W_A + h * HEAD_A) for h in heads]
    bcol = [beta_full[:, h:h + 1] for h in heads]
    gcol = [gc_full[:, H_A + h:H_A + h + 1] for h in heads]
    gtot = [gtot_full[:, H_A + h:H_A + h + 1] for h in heads]
    eg = [jnp.exp(gcol[h]) for h in heads]
    kb = [k[h] * bcol[h] for h in heads]

    pk = _Packed(R, C)
    hpairs = range(H_A // 2)
    zc = jnp.zeros((R, HEAD_A), F32)
    row2 = _iota((2 * R, LANE), 0)
    lane2 = _iota((2 * R, LANE), 1)
    ones_cl = jnp.ones((R, LANE), F32)
    gc2 = jnp.concatenate([gc_full, gc_full], axis=0)
    kq, gamma = [], []
    for hp in hpairs:
        h0, h1 = 2 * hp, 2 * hp + 1
        lhs = jnp.concatenate([jnp.concatenate([kb[h0], kb[h1]], axis=1),
                               jnp.concatenate([q[h0], q[h1]], axis=1)], axis=0)
        rk = jnp.concatenate([jnp.concatenate([k[h0], zc], axis=1),
                              jnp.concatenate([zc, k[h1]], axis=1)], axis=0)
        kq.append(_bdot_nt(lhs, rk))
        sel = lane2 == jnp.where(row2 < R, H_A + h0, H_A + h1)
        grow = _xdot_r(ones_cl, jnp.where(sel, gc2, 0.0), 3, NT_DIMS)
        gcol_p = jnp.where(pk.first, gcol[h0], gcol[h1])
        gamma.append(jnp.exp(jnp.where(pk.incl, gcol_p - grow, NEG_BIG)))
    tinv = pk.inverse_many({hp: -jnp.where(pk.strict, kq[hp][0:R] * gamma[hp], 0.0) for hp in hpairs})
    uw = []
    for hp in hpairs:
        h0, h1 = 2 * hp, 2 * hp + 1
        z2 = jnp.zeros((R, 2 * HEAD_A), F32)
        rhs = jnp.concatenate([jnp.concatenate([v[h0] * bcol[h0], kb[h0] * eg[h0], z2], axis=1),
                               jnp.concatenate([z2, v[h1] * bcol[h1], kb[h1] * eg[h1]], axis=1)], axis=0)
        uw.append(_bdot(tinv[hp], rhs))
    u = [uw[h // 2][:, (h % 2) * 2 * HEAD_A:(h % 2) * 2 * HEAD_A + HEAD_A] for h in heads]
    w = [uw[h // 2][:, (h % 2) * 2 * HEAD_A + HEAD_A:(h % 2 + 1) * 2 * HEAD_A] for h in heads]
    qd = [q[h] * eg[h] for h in heads]
    s = [[s_ref[n, h] for h in heads] for n in seqs]
    wqs = [[_bdot(jnp.concatenate([seq_rows(w[h], n), seq_rows(qd[h], n)], axis=0), s[n][h]) for h in heads]
           for n in seqs]
    ws = [jnp.concatenate([wqs[n][h][0:C] for n in seqs], axis=0) if NB > 1 else wqs[0][h][0:C] for h in heads]
    qs = [jnp.concatenate([wqs[n][h][C:2 * C] for n in seqs], axis=0) if NB > 1 else wqs[0][h][C:2 * C]
          for h in heads]
    v_new = [u[h] - ws[h] for h in heads]
    o = []
    for hp in hpairs:
        h0, h1 = 2 * hp, 2 * hp + 1
        vn_bd = jnp.concatenate([jnp.concatenate([v_new[h0], zc], axis=1),
                                 jnp.concatenate([zc, v_new[h1]], axis=1)], axis=0)
        op = _bdot(kq[hp][R:2 * R] * gamma[hp], vn_bd)
        o.append(qs[h0] + op[:, :HEAD_A])
        o.append(qs[h1] + op[:, HEAD_A:])
    kd = [k[h] * jnp.exp(gtot[h] - gcol[h]) for h in heads]
    for n in seqs:
        for h in heads:
            glast = gtot[h] if NB == 1 else gtot[h][n * C:n * C + 1]
            s_ref[n, h] = s[n][h] * jnp.exp(glast) + _bdot_tn(seq_rows(kd[h], n), seq_rows(v_new[h], n))
    osq = [lane_sum(o[h] * o[h]) for h in heads]
    for h in heads:
        z = qkvz_ref[:, 3 * W_A + h * HEAD_A:3 * W_A + (h + 1) * HEAD_A]
        oh = o[h] * lax.rsqrt(osq[h] * (1.0 / HEAD_A) + RMS_EPS)
        o_ref[:, h * HEAD_A:(h + 1) * HEAD_A] = (oh * on_ref[...] * _silu(z)).astype(BF16)

    if NB == 1:
        carry_ref[...] = qkvz_ref[C - SUB:C, 0:3 * W_A]


def _delta_mixer(proj, row0, nb, seq, C, NB, prev, s0, cw, alog_row, dtb_row, onorm):
    nch = seq // C
    assert NB == 1 or (nch == 1 and C == SUB)
    R = NB * C
    blk0 = row0 // R
    bcast = s0.shape[0] == 1
    bsel = (lambda b: 0) if bcast else (lambda b: b)
    return pl.pallas_call(
        functools.partial(_delta_kernel, C=C, NB=NB),
        out_shape=(jax.ShapeDtypeStruct((nb * seq, W_A), BF16),
                   jax.ShapeDtypeStruct((nb, H_A, HEAD_A, HEAD_A), F32)),
        grid=(nb // NB, nch),
        in_specs=[
            pl.BlockSpec((R, 4 * W_A), lambda b, c: (blk0 + b * nch + c, 0)),
            pl.BlockSpec((R, LANE), lambda b, c: (blk0 + b * nch + c, SM0 // LANE)),
            (pl.BlockSpec((SUB, 3 * W_A), lambda b, c: (0, 0)) if NB == 1
             else pl.BlockSpec((3, NB, 3 * W_A), lambda b, c: (0, b, 0))),
            pl.BlockSpec((NB, H_A, HEAD_A, HEAD_A), lambda b, c: (bsel(b), 0, 0, 0)),
            pl.BlockSpec((4, 3 * W_A), lambda b, c: (0, 0)),
            pl.BlockSpec((1, LANE), lambda b, c: (0, 0)),
            pl.BlockSpec((1, LANE), lambda b, c: (0, 0)),
            pl.BlockSpec((1, HEAD_A), lambda b, c: (0, 0)),
        ],
        out_specs=(pl.BlockSpec((R, W_A), lambda b, c: (b * nch + c, 0)),
                   pl.BlockSpec((NB, H_A, HEAD_A, HEAD_A), lambda b, c: (b, 0, 0, 0))),
        scratch_shapes=[pltpu.VMEM((SUB, 3 * W_A), F32)],
        compiler_params=pltpu.CompilerParams(
            dimension_semantics=("parallel", "arbitrary"), vmem_limit_bytes=VMEM_LIMIT),
        name="delta_mixer",
    )(proj, proj, prev, s0, cw, alog_row, dtb_row, onorm)


def _delta_multi_kernel(qkv0_ref, ba0_ref, qkvn_ref, ban_ref, z_ref, prev_ref, s0_ref, cw_ref, alog_ref, dtb_ref,
                        on_ref, o_ref, s_ref, carry_ref, pre_ref, small_ref, gtot_ref, *, C, NS):
    ci = pl.program_id(1)
    heads = [(s, h) for s in range(NS) for h in range(H_A)]
    hpairs = [(s, hp) for s in range(NS) for hp in range(H_A // 2)]
    tri, _ = _seq_masks(C, C)
    ones_ll = jnp.ones((LANE, LANE), BF16)

    def lane_sum(x):
        return jnp.dot(x.astype(BF16), ones_ll, preferred_element_type=F32)

    def preamble(qkv_ref, ba_ref, hist_of, slot):
        for s in range(NS):
            ba = ba_ref[s]
            g_full = -jnp.exp(alog_ref[...]) * _softplus(ba + dtb_ref[...])
            gc_full = _xdot_r(tri, g_full, 3)
            small_ref[slot, s, 0] = _sigmoid(ba)
            small_ref[slot, s, 1] = gc_full
            gtot_ref[slot, s] = jnp.broadcast_to(gc_full[C - 1:C, :], (SUB, LANE))
            yield
        for part in range(3):
            xs = {}
            for s, h in heads:
                c0 = part * W_A + h * HEAD_A
                x = qkv_ref[s, :, c0:c0 + LANE]
                hist = hist_of(s, c0)
                y = x * cw_ref[3:4, c0:c0 + LANE]
                for k in (1, 2, 3):
                    y = y + _shifted(x, hist, k) * cw_ref[3 - k:4 - k, c0:c0 + LANE]
                xs[s, h] = _silu(y)
                yield
            if part < 2:
                sums = {key: lane_sum(x * x) for key, x in xs.items()}
                yield
                scale = HEAD_A ** -0.5 if part == 0 else 1.0
                xs = {key: x * (lax.rsqrt(sums[key] + 1e-6) * scale) for key, x in xs.items()}
            for s, h in heads:
                pre_ref[slot, s, part, :, h * HEAD_A:(h + 1) * HEAD_A] = xs[s, h]
            yield
        for s in range(NS):
            carry_ref[s] = qkv_ref[s, C - SUB:C, :]

    @pl.when(ci == 0)
    def _():
        for _ in preamble(qkv0_ref, ba0_ref, lambda s, c0: prev_ref[:, c0:c0 + LANE], 0):
            pass
        for s in range(NS):
            s_ref[s, 0] = s0_ref[...]

    slot = ci % 2
    ahead = preamble(qkvn_ref, ban_ref, lambda s, c0: carry_ref[s, :, c0:c0 + LANE], 1 - slot)
    calls = [0]

    def tick():
        calls[0] += 1
        if calls[0] % 3 == 0:
            next(ahead, None)

    def each(fn, items):
        out = {}
        for it in items:
            out[it] = fn(*it)
            tick()
        return out

    beta_full = {s: small_ref[slot, s, 0] for s in range(NS)}
    gc_full = {s: small_ref[slot, s, 1] for s in range(NS)}
    gtot_row = {s: gtot_ref[slot, s] for s in range(NS)}
    q = {(s, h): pre_ref[slot, s, 0, :, h * HEAD_A:(h + 1) * HEAD_A] for s, h in heads}
    k = {(s, h): pre_ref[slot, s, 1, :, h * HEAD_A:(h + 1) * HEAD_A] for s, h in heads}
    v = {(s, h): pre_ref[slot, s, 2, :, h * HEAD_A:(h + 1) * HEAD_A] for s, h in heads}
    bcol = {(s, h): beta_full[s][:, h:h + 1] for s, h in heads}
    gcol = {(s, h): gc_full[s][:, H_A + h:H_A + h + 1] for s, h in heads}
    gtot = {(s, h): gtot_row[s][0:1, H_A + h:H_A + h + 1] for s, h in heads}
    eg = {key: jnp.exp(gcol[key]) for key in heads}
    kb = {key: k[key] * bcol[key] for key in heads}

    pk = _Packed(C, C)
    zc = jnp.zeros((C, HEAD_A), F32)
    row2 = _iota((2 * C, LANE), 0)
    lane2 = _iota((2 * C, LANE), 1)
    ones_cl = jnp.ones((C, LANE), F32)

    def kq_of(s, hp):
        a, b = (s, 2 * hp), (s, 2 * hp + 1)
        lhs = jnp.concatenate([jnp.concatenate([kb[a], kb[b]], axis=1),
                               jnp.concatenate([q[a], q[b]], axis=1)], axis=0)
        rk = jnp.concatenate([jnp.concatenate([k[a], zc], axis=1),
                              jnp.concatenate([zc, k[b]], axis=1)], axis=0)
        return _bdot_nt(lhs, rk)

    def gamma_of(s, hp):
        a, b = (s, 2 * hp), (s, 2 * hp + 1)
        sel = lane2 == jnp.where(row2 < C, H_A + 2 * hp, H_A + 2 * hp + 1)
        gc2 = jnp.concatenate([gc_full[s], gc_full[s]], axis=0)
        grow = _xdot_r(ones_cl, jnp.where(sel, gc2, 0.0), 3, NT_DIMS)
        gcol_p = jnp.where(pk.first, gcol[a], gcol[b])
        return jnp.exp(jnp.where(pk.incl, gcol_p - grow, NEG_BIG))

    kq = each(kq_of, hpairs)
    gamma = each(gamma_of, hpairs)
    ts = pk.inverse_many({key: -jnp.where(pk.strict, kq[key][0:C] * gamma[key], 0.0) for key in hpairs},
                         lambda fn, keys: each(lambda *key: fn(key), keys))

    def uw_of(s, hp):
        a, b = (s, 2 * hp), (s, 2 * hp + 1)
        z2 = jnp.zeros((C, 2 * HEAD_A), F32)
        rhs = jnp.concatenate([jnp.concatenate([v[a] * bcol[a], kb[a] * eg[a], z2], axis=1),
                               jnp.concatenate([z2, v[b] * bcol[b], kb[b] * eg[b]], axis=1)], axis=0)
        return _bdot(ts[s, hp], rhs)

    uw = each(uw_of, hpairs)
    u = {(s, h): uw[s, h // 2][:, (h % 2) * 2 * HEAD_A:(h % 2) * 2 * HEAD_A + HEAD_A] for s, h in heads}
    w = {(s, h): uw[s, h // 2][:, (h % 2) * 2 * HEAD_A + HEAD_A:(h % 2 + 1) * 2 * HEAD_A] for s, h in heads}
    st = {(s, h): s_ref[s, 0, h] for s, h in heads}
    wqs = each(lambda s, h: _bdot(jnp.concatenate([w[s, h], q[s, h] * eg[s, h]], axis=0), st[s, h]), heads)
    v_new = {key: u[key] - wqs[key][0:C] for key in heads}

    def o_of(s, hp):
        a, b = (s, 2 * hp), (s, 2 * hp + 1)
        vn_bd = jnp.concatenate([jnp.concatenate([v_new[a], zc], axis=1),
                                 jnp.concatenate([zc, v_new[b]], axis=1)], axis=0)
        return _bdot(kq[s, hp][C:2 * C] * gamma[s, hp], vn_bd)

    op = each(o_of, hpairs)
    o = {(s, h): wqs[s, h][C:2 * C] + op[s, h // 2][:, (h % 2) * HEAD_A:(h % 2 + 1) * HEAD_A] for s, h in heads}
    for s, h in heads:
        kd = k[s, h] * jnp.exp(gtot[s, h] - gcol[s, h])
        s_ref[s, 0, h] = st[s, h] * jnp.exp(gtot[s, h]) + _bdot_tn(kd, v_new[s, h])
        tick()
    osq = each(lambda s, h: lane_sum(o[s, h] * o[s, h]), heads)
    for s, h in heads:
        z = z_ref[s, :, h * HEAD_A:(h + 1) * HEAD_A]
        oh = o[s, h] * lax.rsqrt(osq[s, h] * (1.0 / HEAD_A) + RMS_EPS)
        o_ref[s, :, h * HEAD_A:(h + 1) * HEAD_A] = (oh * on_ref[...] * _silu(z)).astype(BF16)
        tick()
    for _ in ahead:
        pass


def _delta_multi_mixer(proj, nb, seq, C, NS, prev, s0, cw, alog_row, dtb_row, onorm):
    nch = seq // C
    ng = nb // NS
    proj3 = proj.reshape(NS, ng * seq, P_CAT)
    nxt = lambda i, c: i * nch + jnp.minimum(c + 1, nch - 1)
    oa, sd = pl.pallas_call(
        functools.partial(_delta_multi_kernel, C=C, NS=NS),
        out_shape=(jax.ShapeDtypeStruct((NS, ng * seq, W_A), BF16),
                   jax.ShapeDtypeStruct((NS, ng, H_A, HEAD_A, HEAD_A), F32)),
        grid=(ng, nch),
        in_specs=[
            pl.BlockSpec((NS, C, 3 * W_A), lambda i, c: (0, i * nch, 0)),
            pl.BlockSpec((NS, C, LANE), lambda i, c: (0, i * nch, SM0 // LANE)),
            pl.BlockSpec((NS, C, 3 * W_A), lambda i, c: (0, nxt(i, c), 0)),
            pl.BlockSpec((NS, C, LANE), lambda i, c: (0, nxt(i, c), SM0 // LANE)),
            pl.BlockSpec((NS, C, W_A), lambda i, c: (0, i * nch + c, 3)),
            pl.BlockSpec((SUB, 3 * W_A), lambda i, c: (0, 0)),
            pl.BlockSpec((None, H_A, HEAD_A, HEAD_A), lambda i, c: (0, 0, 0, 0)),
            pl.BlockSpec((4, 3 * W_A), lambda i, c: (0, 0)),
            pl.BlockSpec((1, LANE), lambda i, c: (0, 0)),
            pl.BlockSpec((1, LANE), lambda i, c: (0, 0)),
            pl.BlockSpec((1, HEAD_A), lambda i, c: (0, 0)),
        ],
        out_specs=(pl.BlockSpec((NS, C, W_A), lambda i, c: (0, i * nch + c, 0)),
                   pl.BlockSpec((NS, 1, H_A, HEAD_A, HEAD_A), lambda i, c: (0, i, 0, 0, 0))),
        scratch_shapes=[pltpu.VMEM((NS, SUB, 3 * W_A), F32), pltpu.VMEM((2, NS, 3, C, W_A), F32),
                        pltpu.VMEM((2, NS, 2, C, LANE), F32), pltpu.VMEM((2, NS, SUB, LANE), F32)],
        compiler_params=pltpu.CompilerParams(
            dimension_semantics=("parallel", "arbitrary"), vmem_limit_bytes=VMEM_LIMIT),
        name="delta_multi_mixer",
    )(proj3, proj3, proj3, proj3, proj3, prev, s0, cw, alog_row, dtb_row, onorm)
    return oa.reshape(nb * seq, W_A), sd.reshape(nb, H_A, HEAD_A, HEAD_A)


def _rwkv_kernel(r_ref, k_ref, v_ref, sm_ref, pr_ref, pk_ref, pv_ref, psm_ref, s0_ref,
                 mur_ref, muk_ref, muv_ref, musm_ref, w2_ref, a2_ref, g2_ref,
                 w0_ref, a0_ref, kk_ref, ka_ref, rk_ref, lnw_ref, lnb_ref,
                 o_ref, s_ref, *, C, NB, G):
    R = NB * C
    seqs = range(NB)
    pairs = range(G)

    r2 = _iota((LANE, LANE), 0)
    c2 = _iota((LANE, LANE), 1)
    ones_bd = ((r2 < HEAD_B) == (c2 < HEAD_B)).astype(F32)
    hr_ref, hk_ref, hv_ref, hsm_ref = pr_ref, pk_ref, pv_ref, psm_ref
    tri, ones_seq = _seq_masks(R, C)

    def lerp(x, hist_ref, mu):
        hist = hist_ref[...] if NB == 1 else [_repeat_rows(hist_ref[0], SUB)]
        return x + (_delayed(x, hist, 1, NB) - x) * mu

    sm = sm_ref[...]
    xs = lerp(sm, hsm_ref, musm_ref[...])
    slab_wa = xs[:, SM_WA:SM_WA + LANE]
    slab_g = xs[:, SM_G:SM_G + 2 * LANE]

    def col(x, p):
        return x[:, p * LANE:(p + 1) * LANE]

    def to_rows(x):
        return jnp.concatenate([col(x, p) for p in pairs], axis=0)

    def to_cols(y):
        return jnp.concatenate([y[p * R:(p + 1) * R] for p in pairs], axis=1)

    def seq_rows(x, n):
        return x[n * C:(n + 1) * C]

    xr = lerp(r_ref[...], hr_ref, mur_ref[...])
    xk = lerp(k_ref[...], hk_ref, muk_ref[...])
    xv = lerp(v_ref[...], hv_ref, muv_ref[...])
    wlog = -_softplus(-(w0_ref[...] + _bdot(jnp.tanh(slab_wa), w2_ref[...]))) - 0.5
    ld = -jnp.exp(wlog)
    aa = _sigmoid(a0_ref[...] + _bdot(slab_wa, a2_ref[...]))
    gate = _bdot(_sigmoid(slab_g), g2_ref[...])
    kkr = xk * kk_ref[...]
    kkn = kkr * lax.rsqrt(to_cols(_bdot(to_rows(kkr * kkr), ones_bd)) + 1e-6)
    k2 = xk * (1.0 + (aa - 1.0) * ka_ref[...])
    lp = _xdot_r(tri, ld, 3)
    lp_tot = lp[C - 1:C, :] if NB == 1 else _xdot_r(ones_seq, ld, 3)
    e_neg = jnp.exp(-lp)
    e_rem = jnp.exp(lp_tot - lp)
    at = -kkn * jnp.exp(lp - ld)
    rt = xr * jnp.exp(lp)
    kb = kkn * aa
    bt = kb * e_neg
    kt = k2 * e_neg
    bhat = kb * e_rem
    khat = k2 * e_rem
    p_tot = jnp.exp(lp_tot)

    s = [[s0_ref[n, p] for p in pairs] for n in seqs]
    atp = [col(at, p) for p in pairs]
    rtp = [col(rt, p) for p in pairs]
    first_head = _iota((2 * C, LANE), 1) < HEAD_B
    top_rows = _iota((LANE, LANE), 0) < HEAD_B

    def other_half(x):
        return pltpu.roll(x, HEAD_B, 1)

    def times_state_t(a, sp):
        both = _bdot_nt(jnp.concatenate([a[:, :HEAD_B], other_half(a)[:, :HEAD_B]], axis=0), sp)
        m = a.shape[0]
        return jnp.where(first_head, both[0:m], both[m:2 * m])

    ars = [[times_state_t(jnp.concatenate([seq_rows(atp[p], n), seq_rows(rtp[p], n)], axis=0), s[n][p])
            for p in pairs] for n in seqs]
    if NB == 1:
        x_state = [ars[0][p][0:C] for p in pairs]
        o_state = [ars[0][p][C:2 * C] for p in pairs]
    else:
        x_state = [jnp.concatenate([ars[n][p][0:C] for n in seqs], axis=0) for p in pairs]
        o_state = [jnp.concatenate([ars[n][p][C:2 * C] for n in seqs], axis=0) for p in pairs]
    pk = _Packed(R, C)
    stack_mask = (_iota((2 * R, LANE), 0) < R) == (_iota((2 * R, LANE), 1) < HEAD_B)

    def by_head(x):
        return jnp.where(stack_mask, jnp.concatenate([x, x], axis=0), 0.0)

    ar = [jnp.concatenate([atp[p], rtp[p]], axis=0) for p in pairs]
    if 2 * R == LANE:
        abk = [_bdot_nt(ar[p], jnp.concatenate([by_head(col(bt, p)), by_head(col(kt, p))], axis=0)) for p in pairs]
        ab = [m[:, 0:2 * R] for m in abk]
        ak = [m[:, 2 * R:4 * R] for m in abk]
    else:
        ab = [_bdot_nt(ar[p], by_head(col(bt, p))) for p in pairs]
        ak = [_bdot_nt(ar[p], by_head(col(kt, p))) for p in pairs]
    tinv = pk.inverse_many({p: jnp.where(pk.strict, ab[p][0:R], 0.0) for p in pairs})
    v_bh = [by_head(col(xv, p)) for p in pairs]
    akv = [_bdot(jnp.where(pk.strict_over_incl, ak[p], 0.0), v_bh[p]) for p in pairs]
    y = [x_state[p] + akv[p][0:R] for p in pairs]
    u = [_bdot(tinv[p], by_head(y[p])) for p in pairs]
    o = [o_state[p] + akv[p][R:2 * R] + _bdot(jnp.where(pk.incl, ab[p][R:2 * R], 0.0), by_head(u[p]))
         for p in pairs]
    for n in seqs:
        for p in pairs:
            uv = jnp.concatenate([seq_rows(u[p], n), seq_rows(col(xv, p), n)], axis=0)
            bkhat = jnp.concatenate([seq_rows(col(bhat, p), n), seq_rows(col(khat, p), n)], axis=0)
            decay = col(p_tot, p) if NB == 1 else col(p_tot, p)[n * C:n * C + 1]
            full = _bdot_tn(uv, bkhat)
            upd = jnp.where(top_rows, full, other_half(full))[:, :HEAD_B]
            dec = jnp.where(top_rows[:, :HEAD_B], decay[:, :HEAD_B], other_half(decay)[:, :HEAD_B])
            s_ref[n, p] = s[n][p] * dec + upd

    o_rows = jnp.concatenate(o, axis=0)
    mean = _bdot(o_rows, ones_bd) * (1.0 / HEAD_B)
    d = o_rows - mean
    var = _bdot(d * d, ones_bd) * (1.0 / HEAD_B)
    on = to_cols(d * lax.rsqrt(var + GN_EPS)) * lnw_ref[...] + lnb_ref[...]
    bonus = to_cols(_bdot(to_rows(xr * k2 * rk_ref[...]), ones_bd)) * xv
    o_ref[...] = ((on + bonus) * gate).astype(BF16)


def _rwkv_mixer(proj, row0, nb, seq, C, NB, G, prev_rkv, prev_sm, s0, mu_rkv, mu_sm, w2p, a2p, g2p,
                w0, a0, k_k, k_a, r_k, lnw, lnb):
    nch = seq // C
    assert nch == 1 and (NB == 1 or C == SUB)
    R = NB * C
    ng = N_PAIR // G
    gw = G * LANE
    blk0 = row0 // R
    bcast = s0.shape[0] == 1
    bsel = (lambda b: 0) if bcast else (lambda b: b)

    def proj_spec(col0):
        return pl.BlockSpec((R, gw), lambda b, g, c: (blk0 + b * nch + c, col0 // gw + g))

    def prev_spec(part):
        if NB == 1:
            return pl.BlockSpec((SUB, gw), lambda b, g, c: (0, part * (W_B // gw) + g))
        return pl.BlockSpec((1, NB, gw), lambda b, g, c: (0, b, part * (W_B // gw) + g))

    def vec_spec(part=0):
        return pl.BlockSpec((1, gw), lambda b, g, c: (0, part * (W_B // gw) + g))

    in_specs = [
        proj_spec(RKV0), proj_spec(RKV0 + W_B), proj_spec(RKV0 + 2 * W_B),
        pl.BlockSpec((R, SM_W), lambda b, g, c: (blk0 + b * nch + c, SM0 // SM_W)),
        prev_spec(0), prev_spec(1), prev_spec(2),
        (pl.BlockSpec((SUB, SM_W), lambda b, g, c: (0, 0)) if NB == 1
         else pl.BlockSpec((1, NB, SM_W), lambda b, g, c: (0, b, 0))),
        pl.BlockSpec((NB, G, LANE, HEAD_B), lambda b, g, c: (bsel(b), g, 0, 0)),
        vec_spec(0), vec_spec(1), vec_spec(2),
        pl.BlockSpec((1, SM_W), lambda b, g, c: (0, 0)),
        pl.BlockSpec((LANE, gw), lambda b, g, c: (0, g)),
        pl.BlockSpec((LANE, gw), lambda b, g, c: (0, g)),
        pl.BlockSpec((2 * LANE, gw), lambda b, g, c: (0, g)),
        vec_spec(), vec_spec(), vec_spec(), vec_spec(), vec_spec(), vec_spec(), vec_spec(),
    ]
    return pl.pallas_call(
        functools.partial(_rwkv_kernel, C=C, NB=NB, G=G),
        out_shape=(jax.ShapeDtypeStruct((nb * seq, W_B), BF16),
                   jax.ShapeDtypeStruct((nb, N_PAIR, LANE, HEAD_B), F32)),
        grid=(nb // NB, ng, nch),
        in_specs=in_specs,
        out_specs=(pl.BlockSpec((R, gw), lambda b, g, c: (b * nch + c, g)),
                   pl.BlockSpec((NB, G, LANE, HEAD_B), lambda b, g, c: (b, g, 0, 0))),
        compiler_params=pltpu.CompilerParams(
            dimension_semantics=("parallel", "parallel", "arbitrary"), vmem_limit_bytes=VMEM_LIMIT),
        name="rwkv_mixer",
    )(proj, proj, proj, proj, prev_rkv, prev_rkv, prev_rkv, prev_sm, s0,
      mu_rkv, mu_rkv, mu_rkv, mu_sm, w2p, a2p, g2p, w0, a0, k_k, k_a, r_k, lnw, lnb)


_AT, _RT, _BT, _KT, _BHAT, _KHAT, _XV, _GATE, _BONUS, _N_PRE = range(10)


def _rwkv_multi_kernel(r0_ref, k0_ref, v0_ref, sm0_ref, rn_ref, kn_ref, vn_ref, smn_ref,
                       pr_ref, pk_ref, pv_ref, psm_ref, s0_ref,
                       mur_ref, muk_ref, muv_ref, musm_ref, w2_ref, a2_ref, g2_ref,
                       w0_ref, a0_ref, kk_ref, ka_ref, rk_ref, lnw_ref, lnb_ref,
                       o_ref, s_ref, cr_ref, ck_ref, cv_ref, csm_ref, st_ref, pre_ref, ptot_ref, *, C, G, NS):
    ci = pl.program_id(1)
    nch = pl.num_programs(1)
    pairs = range(G)
    chains = [(s, p) for s in range(NS) for p in pairs]

    r2 = _iota((LANE, LANE), 0)
    c2 = _iota((LANE, LANE), 1)
    same_head = (r2 < HEAD_B) == (c2 < HEAD_B)
    ones_bd = same_head.astype(F32)
    spread = (_iota((HEAD_B, LANE), 0) == (_iota((HEAD_B, LANE), 1) & (HEAD_B - 1))).astype(F32)
    gather = ((_iota((LANE, HEAD_B), 0) & (HEAD_B - 1)) == _iota((LANE, HEAD_B), 1)).astype(F32)
    tri, _ = _seq_masks(C, C)

    def col(x, p):
        return x[:, p * LANE:(p + 1) * LANE]

    def to_rows(x):
        return jnp.concatenate([col(x, p) for p in pairs], axis=0)

    def to_cols(y):
        return jnp.concatenate([y[p * C:(p + 1) * C] for p in pairs], axis=1)

    def lerp(x, hist, mu):
        return x + (_shifted(x, hist, 1) - x) * mu

    def preamble(r_ref, k_ref, v_ref, sm_ref, hist_of, slot):
        for s in range(NS):
            hr, hk, hv, hsm = hist_of(s)
            xs = lerp(sm_ref[s], hsm, musm_ref[...])
            yield
            slab_wa = xs[:, SM_WA:SM_WA + LANE]
            wl = _bdot(jnp.tanh(slab_wa), w2_ref[...])
            yield
            al = _bdot(slab_wa, a2_ref[...])
            yield
            pre_ref[slot, s, _GATE] = _bdot(_sigmoid(xs[:, SM_G:SM_G + 2 * LANE]), g2_ref[...])
            yield
            xr = lerp(r_ref[s], hr, mur_ref[...])
            yield
            xk = lerp(k_ref[s], hk, muk_ref[...])
            yield
            xv = lerp(v_ref[s], hv, muv_ref[...])
            pre_ref[slot, s, _XV] = xv
            yield
            ld = -jnp.exp(-_softplus(-(w0_ref[...] + wl)) - 0.5)
            yield
            lp = _xdot_r(tri, ld, 3)
            yield
            aa = _sigmoid(a0_ref[...] + al)
            yield
            kkr = xk * kk_ref[...]
            ss = to_cols(_bdot(to_rows(kkr * kkr), ones_bd))
            yield
            kkn = kkr * lax.rsqrt(ss + 1e-6)
            k2 = xk * (1.0 + (aa - 1.0) * ka_ref[...])
            yield
            pre_ref[slot, s, _BONUS] = to_cols(_bdot(to_rows(xr * k2 * rk_ref[...]), ones_bd)) * xv
            yield
            pre_ref[slot, s, _AT] = -kkn * jnp.exp(lp - ld)
            yield
            pre_ref[slot, s, _RT] = xr * jnp.exp(lp)
            yield
            kb = kkn * aa
            e_neg = jnp.exp(-lp)
            pre_ref[slot, s, _BT] = kb * e_neg
            yield
            pre_ref[slot, s, _KT] = k2 * e_neg
            yield
            lp_tot = lp[C - 1:C, :]
            e_rem = jnp.exp(lp_tot - lp)
            pre_ref[slot, s, _BHAT] = kb * e_rem
            yield
            pre_ref[slot, s, _KHAT] = k2 * e_rem
            ptot_ref[slot, s] = jnp.broadcast_to(jnp.exp(lp_tot), (SUB, lp.shape[1]))
            yield
            cr_ref[s] = r_ref[s, C - SUB:C, :]
            ck_ref[s] = k_ref[s, C - SUB:C, :]
            cv_ref[s] = v_ref[s, C - SUB:C, :]
            csm_ref[s] = sm_ref[s, C - SUB:C, :]

    @pl.when(ci == 0)
    def _():
        first_hist = lambda s: (pr_ref[...], pk_ref[...], pv_ref[...], psm_ref[...])
        for _ in preamble(r0_ref, k0_ref, v0_ref, sm0_ref, first_hist, 0):
            pass
        for s, p in chains:
            st_ref[s * G + p] = jnp.where(same_head, _xdot(s0_ref[p], spread, 3), 0.0)

    slot = ci % 2
    carried = lambda s: (cr_ref[s], ck_ref[s], cv_ref[s], csm_ref[s])
    ahead = preamble(rn_ref, kn_ref, vn_ref, smn_ref, carried, 1 - slot)
    calls = [0]

    def tick():
        calls[0] += 1
        if calls[0] % 6 == 0:
            next(ahead, None)

    def each(fn, items):
        out = {}
        for it in items:
            out[it] = fn(*it)
            tick()
        return out

    def pre(idx, s, p):
        return pre_ref[slot, s, idx, :, p * LANE:(p + 1) * LANE]

    pk = _Packed(C, C)
    stack_mask = (_iota((2 * C, LANE), 0) < C) == (_iota((2 * C, LANE), 1) < HEAD_B)

    def by_head(x):
        return jnp.where(stack_mask, jnp.concatenate([x, x], axis=0), 0.0)

    st = {(s, p): st_ref[s * G + p] for s, p in chains}
    ar = {(s, p): jnp.concatenate([pre(_AT, s, p), pre(_RT, s, p)], axis=0) for s, p in chains}
    ars = each(lambda s, p: _bdot_nt(ar[s, p], st[s, p]), chains)
    abk = each(lambda s, p: _bdot_nt(ar[s, p], jnp.concatenate([by_head(pre(_BT, s, p)), by_head(pre(_KT, s, p))],
                                                               axis=0)), chains)
    ab = {key: abk[key][:, 0:2 * C] for key in chains}
    ak = {key: abk[key][:, 2 * C:4 * C] for key in chains}
    ts = pk.inverse_many({key: jnp.where(pk.strict, ab[key][0:C], 0.0) for key in chains},
                         lambda fn, keys: each(lambda *key: fn(key), keys))
    xv = {(s, p): pre(_XV, s, p) for s, p in chains}
    v_bh = {key: by_head(xv[key]) for key in chains}
    akv = each(lambda s, p: _bdot(jnp.where(pk.strict_over_incl, ak[s, p], 0.0), v_bh[s, p]), chains)
    y = {key: ars[key][0:C] + akv[key][0:C] for key in chains}
    u = each(lambda s, p: _bdot(ts[s, p], by_head(y[s, p])), chains)
    o = each(lambda s, p: ars[s, p][C:2 * C] + akv[s, p][C:2 * C]
             + _bdot(jnp.where(pk.incl, ab[s, p][C:2 * C], 0.0), by_head(u[s, p])), chains)
    for s, p in chains:
        uv = jnp.concatenate([u[s, p], xv[s, p]], axis=0)
        bkhat = jnp.concatenate([pre(_BHAT, s, p), pre(_KHAT, s, p)], axis=0)
        s_new = st[s, p] * ptot_ref[slot, s, 0:1, p * LANE:(p + 1) * LANE] + _bdot_tn(uv, bkhat)
        st_ref[s * G + p] = jnp.where(same_head, s_new, 0.0)
        tick()

    o_rows = jnp.concatenate([o[key] for key in chains], axis=0)
    mean = _bdot(o_rows, ones_bd) * (1.0 / HEAD_B)
    d = o_rows - mean
    var = _bdot(d * d, ones_bd) * (1.0 / HEAD_B)
    dn = d * lax.rsqrt(var + GN_EPS)
    for s in range(NS):
        on = to_cols(dn[s * G * C:(s + 1) * G * C]) * lnw_ref[...] + lnb_ref[...]
        o_ref[s] = ((on + pre_ref[slot, s, _BONUS]) * pre_ref[slot, s, _GATE]).astype(BF16)
    for _ in ahead:
        pass

    @pl.when(ci == nch - 1)
    def _():
        for s, p in chains:
            s_ref[s, 0, p] = _xdot(st_ref[s * G + p], gather, 3)


def _rwkv_multi_mixer(proj, nb, seq, C, NS, prev_rkv, prev_sm, s0, mu_rkv, mu_sm, w2p, a2p, g2p,
                      w0, a0, k_k, k_a, r_k, lnw, lnb):
    G = N_PAIR
    nch = seq // C
    ng = nb // NS
    gw = G * LANE
    proj3 = proj.reshape(NS, ng * seq, P_CAT)

    def first_spec(col0, w):
        return pl.BlockSpec((NS, C, w), lambda i, c: (0, i * nch, col0 // w))

    def next_spec(col0, w):
        return pl.BlockSpec((NS, C, w), lambda i, c: (0, i * nch + jnp.minimum(c + 1, nch - 1), col0 // w))

    def prev_spec(part):
        return pl.BlockSpec((SUB, gw), lambda i, c: (0, part))

    def vec_spec(part=0):
        return pl.BlockSpec((1, gw), lambda i, c: (0, part))

    in_specs = [
        first_spec(RKV0, gw), first_spec(RKV0 + W_B, gw), first_spec(RKV0 + 2 * W_B, gw), first_spec(SM0, SM_W),
        next_spec(RKV0, gw), next_spec(RKV0 + W_B, gw), next_spec(RKV0 + 2 * W_B, gw), next_spec(SM0, SM_W),
        prev_spec(0), prev_spec(1), prev_spec(2),
        pl.BlockSpec((SUB, SM_W), lambda i, c: (0, 0)),
        pl.BlockSpec((None, G, LANE, HEAD_B), lambda i, c: (0, 0, 0, 0)),
        vec_spec(0), vec_spec(1), vec_spec(2),
        pl.BlockSpec((1, SM_W), lambda i, c: (0, 0)),
        pl.BlockSpec((LANE, gw), lambda i, c: (0, 0)),
        pl.BlockSpec((LANE, gw), lambda i, c: (0, 0)),
        pl.BlockSpec((2 * LANE, gw), lambda i, c: (0, 0)),
        vec_spec(), vec_spec(), vec_spec(), vec_spec(), vec_spec(), vec_spec(), vec_spec(),
    ]
    ob, sw = pl.pallas_call(
        functools.partial(_rwkv_multi_kernel, C=C, G=G, NS=NS),
        out_shape=(jax.ShapeDtypeStruct((NS, ng * seq, W_B), BF16),
                   jax.ShapeDtypeStruct((NS, ng, N_PAIR, LANE, HEAD_B), F32)),
        grid=(ng, nch),
        in_specs=in_specs,
        out_specs=(pl.BlockSpec((NS, C, gw), lambda i, c: (0, i * nch + c, 0)),
                   pl.BlockSpec((NS, 1, G, LANE, HEAD_B), lambda i, c: (0, i, 0, 0, 0))),
        scratch_shapes=[pltpu.VMEM((NS, SUB, gw), F32), pltpu.VMEM((NS, SUB, gw), F32), pltpu.VMEM((NS, SUB, gw), F32),
                        pltpu.VMEM((NS, SUB, SM_W), F32), pltpu.VMEM((NS * G, LANE, LANE), F32),
                        pltpu.VMEM((2, NS, _N_PRE, C, gw), F32), pltpu.VMEM((2, NS, SUB, gw), F32)],
        compiler_params=pltpu.CompilerParams(
            dimension_semantics=("parallel", "arbitrary"), vmem_limit_bytes=VMEM_LIMIT),
        name="rwkv_multi_mixer",
    )(proj3, proj3, proj3, proj3, proj3, proj3, proj3, proj3, prev_rkv, prev_rkv, prev_rkv, prev_sm, s0,
      mu_rkv, mu_rkv, mu_rkv, mu_sm, w2p, a2p, g2p, w0, a0, k_k, k_a, r_k, lnw, lnb)
    return ob.reshape(nb * seq, W_B), sw.reshape(nb, N_PAIR, LANE, HEAD_B)


def _small_layout(cols_ba, cols_w, cols_a, cols_g, axis=-1):
    def z(n):
        shape = list(cols_w.shape)
        shape[axis] = n
        return jnp.zeros(shape, cols_w.dtype)
    return jnp.concatenate(
        [cols_ba, z(SM_WA - cols_ba.shape[axis]), cols_w, cols_a, cols_g,
         z(SM_W - SM_G - cols_g.shape[axis])], axis=axis)


def kernel(x_prompt, x_sample, state_delta, state_conv_qkv, state_wkv, state_shift, state_ffn_conv, meta, norm1, w_in, conv_a, a_log, dt_bias, onorm_a, mu_b, w0, w2, a0, a2, g2, k_k, k_a, r_k, lnx_w, lnx_b, w_o, norm2, w_ffn_in, conv_f, w_ffn_out, norm_f):
    nbp, seq_p, _ = x_prompt.shape
    nbs, seq_s, _ = x_sample.shape
    n_s = nbs * seq_s
    assert w_in.shape[0] == 1, "single-layer trunk"
    assert seq_s == SUB
    l = 0

    wt = w_in[l].T
    o_b = A_PROJ
    o_l = A_PROJ + 3 * W_B
    w_cat_t = jnp.concatenate([
        wt[:4 * W_A], wt[o_b:o_l],
        _small_layout(wt[4 * W_A:A_PROJ], wt[o_l:o_l + W_LORA], wt[o_l + W_LORA:o_l + W_LORA + A_LORA],
                      wt[o_l + W_LORA + A_LORA:], axis=0)], axis=0).astype(BF16)
    mu = mu_b[l]
    mu_rkv = mu[None, :3 * W_B]
    mu_sm = _small_layout(jnp.zeros((1, 2 * H_A), F32), mu[None, 3 * W_B:3 * W_B + W_LORA],
                          mu[None, 3 * W_B + W_LORA:3 * W_B + W_LORA + A_LORA],
                          mu[None, 3 * W_B + W_LORA + A_LORA:])
    w2p = jnp.concatenate([w2[l], jnp.zeros((LANE - W_LORA, W_B), F32)], axis=0)
    a2p = jnp.concatenate([jnp.zeros((W_LORA, W_B), F32), a2[l]], axis=0)
    g2p = jnp.concatenate([g2[l], jnp.zeros((2 * LANE - G_LORA, W_B), F32)], axis=0)
    alog_row = jnp.concatenate([jnp.zeros((H_A,), F32), a_log[l], jnp.zeros((LANE - 2 * H_A,), F32)])[None]
    dtb_row = jnp.concatenate([jnp.zeros((H_A,), F32), dt_bias[l], jnp.zeros((LANE - 2 * H_A,), F32)])[None]
    wo_bf = w_o[l].astype(BF16)
    wfo_bf = w_ffn_out[l].astype(BF16)
    row = lambda v: v.reshape(1, -1)

    def mix(proj, row0, nb, seq, C, NB, prev_qkv, prev_rkv, prev_sm, s_delta, s_wkv):
        delta_params = (conv_a[l], alog_row, dtb_row, row(onorm_a[l]))
        rwkv_params = (mu_rkv, mu_sm, w2p, a2p, g2p, row(w0[l]), row(a0[l]), row(k_k[l]), row(k_a[l]),
                       row(r_k[l]), row(lnx_w[l]), row(lnx_b[l]))
        if seq // C > 1:
            assert row0 == 0 and NB == 1
            oa, sd = _delta_multi_mixer(proj, nb, seq, C, 4, prev_qkv, s_delta, *delta_params)
            ob, sw = _rwkv_multi_mixer(proj, nb, seq, C, 2, prev_rkv, prev_sm, s_wkv, *rwkv_params)
        else:
            oa, sd = _delta_mixer(proj, row0, nb, seq, C, NB, prev_qkv, s_delta, *delta_params)
            ob, sw = _rwkv_mixer(proj, row0, nb, seq, C, NB, N_PAIR, prev_rkv, prev_sm, s_wkv, *rwkv_params)
        return oa, ob, sd, sw

    xs_rows = jnp.concatenate([x_sample.reshape(n_s, D_MODEL), meta], axis=0)
    xp_rows = x_prompt.reshape(nbp * seq_p, D_MODEL)
    n_small = n_s + N_META
    proj_s, proj_tail_s = _norm_matmul(xs_rows, row(norm1[l]), w_cat_t, n_small, 1536, 3, nbs)
    proj_p = _norm_matmul(xp_rows, row(norm1[l]), w_cat_t, 1024, 1536)

    zeros = lambda *s: jnp.zeros(s, F32)
    oa_m, ob_m, sd_m, sw_m = mix(proj_s, n_s, 1, N_META, N_META, 1, zeros(SUB, 3 * W_A),
                                 zeros(SUB, 3 * W_B), zeros(SUB, SM_W),
                                 zeros(1, H_A, HEAD_A, HEAD_A), zeros(1, N_PAIR, LANE, HEAD_B))
    tail = proj_s[n_small - SUB:n_small]
    oa_p, ob_p, sd_p, sw_p = mix(proj_p, 0, nbp, seq_p, 64, 1, tail[:, :3 * W_A],
                                 tail[:, RKV0:SM0], tail[:, SM0:], sd_m, sw_m)
    sh = state_shift[l]
    sh_sm = _small_layout(jnp.zeros((nbs, 1, 2 * H_A), F32), sh[..., 3 * W_B:3 * W_B + W_LORA],
                          sh[..., 3 * W_B + W_LORA:3 * W_B + W_LORA + A_LORA],
                          sh[..., 3 * W_B + W_LORA + A_LORA:])
    tmajor = lambda s: s.transpose(1, 0, 2)
    oa_s, ob_s, sd_s, sw_s = mix(proj_s, 0, nbs, seq_s, seq_s, 8, tmajor(state_conv_qkv[l]),
                                 tmajor(sh[..., :3 * W_B]), tmajor(sh_sm), state_delta[l],
                                 state_wkv[l].reshape(nbs, N_PAIR, LANE, HEAD_B))

    oa_small = jnp.concatenate([oa_s, oa_m], axis=0)
    ob_small = jnp.concatenate([ob_s, ob_m], axis=0)
    x1_s, h2_s = _out_proj(xs_rows, oa_small, ob_small, wo_bf, row(norm2[l]), n_small // 5)
    x1_p, h2_p = _out_proj(xp_rows, oa_p, ob_p, wo_bf, row(norm2[l]), 512)
    act_s, gate_tail_s, gate_last_s = _ffn_in(h2_s, w_ffn_in[l], state_ffn_conv[l].transpose(1, 0, 2), conv_f[l],
                                              n_small, 768, 3, 0, n_s + SUB)
    act_p, tail_p = _ffn_in(h2_p, w_ffn_in[l], gate_last_s[None], conv_f[l], 1024, 768, 3, seq_p // 1024)
    y_s = _ffn_out(act_s, wfo_bf, x1_s, row(norm_f), n_s, 1024, 1792)
    y_p = _ffn_out(act_p, wfo_bf, x1_p, row(norm_f), nbp * seq_p, 1024, 1792)

    def states(conv_new, last, ffn_new, nb, sd, sw):
        shift_new = jnp.concatenate([last[..., RKV0:SM0], last[..., SM0 + SM_WA:SM0 + SM_WA + W_LORA + A_LORA],
                                     last[..., SM0 + SM_G:SM0 + SM_G + G_LORA]], axis=-1)
        return (sd[None], conv_new[None], sw.reshape(nb, H_B, HEAD_B, HEAD_B)[None], shift_new[None],
                ffn_new[None])

    p3 = proj_p.reshape(nbp, seq_p, P_CAT)
    tail_s = proj_tail_s.transpose(1, 0, 2)
    return ((y_p.reshape(nbp, seq_p, D_MODEL), y_s.reshape(nbs, seq_s, D_MODEL))
            + states(p3[:, seq_p - 3:, :3 * W_A], p3[:, seq_p - 1:, :], tail_p[:, SUB - 2:, :], nbp, sd_p, sw_p)
            + states(tail_s[:, :, :3 * W_A], tail_s[:, 2:, :], gate_tail_s.transpose(1, 0, 2), nbs, sd_s, sw_s))
```
